```python
import jax, jax.numpy as jnp
from jax import lax
import numpy as np

D_MODEL = 2048
BATCH = 16
SEQ = 2048
DEPTH = 2

GRID_W = 64
Q_BLOCK = 128
ROPE_THETA = 10000.0
EPS = 1e-6

GQA_HEADS = 6
GQA_KV_HEADS = 2
GQA_HEAD_DIM = 128
GQA_WIDTH = GQA_HEADS * GQA_HEAD_DIM
GQA_KV_WIDTH = GQA_KV_HEADS * GQA_HEAD_DIM

MLA_HEADS = 4
MLA_Q_LORA = 512
MLA_KV_LORA = 256
MLA_NOPE_DIM = 128
MLA_ROPE_DIM = 64
MLA_V_DIM = 128
MLA_WIDTH = MLA_HEADS * MLA_V_DIM

SSD_HEADS = 12
SSD_HEAD_DIM = 64
SSD_GROUPS = 2
SSD_STATE = 128
SSD_CONV = 5
SSD_CHUNK = 128
SSD_INNER = SSD_HEADS * SSD_HEAD_DIM
SSD_CONV_DIM = SSD_INNER + 2 * SSD_GROUPS * SSD_STATE

MIX_WIDTH = GQA_WIDTH + MLA_WIDTH + SSD_INNER
IN_SPLITS = (GQA_WIDTH, GQA_KV_WIDTH, GQA_KV_WIDTH, MLA_Q_LORA, MLA_KV_LORA, MLA_ROPE_DIM, SSD_INNER, SSD_CONV_DIM, 2 * SSD_HEADS)
IN_COLS = GQA_WIDTH + 2 * GQA_KV_WIDTH + MLA_Q_LORA + MLA_KV_LORA + MLA_ROPE_DIM + SSD_INNER + SSD_CONV_DIM + 2 * SSD_HEADS

FFN_HIDDEN = -(-8 * D_MODEL // (3 * 256)) * 256

kernel_name = 'hybrid_gqa_mla_ssd_encoder_block'


def rms_norm(x, g):
    xf = x.astype(jnp.float32)
    y = xf * lax.rsqrt(jnp.mean(xf * xf, axis=-1, keepdims=True) + EPS)
    return (y * g).astype(x.dtype)


def axial_rope_tables(seq_len, rot_dim, dtype):
    rows = seq_len // GRID_W
    row_idx = jnp.repeat(jnp.arange(rows), GRID_W).astype(jnp.float32)
    col_idx = jnp.tile(jnp.arange(GRID_W), rows).astype(jnp.float32)
    axis_dim = rot_dim // 2
    inv_freq = jnp.power(ROPE_THETA, -jnp.arange(0, axis_dim, 2, dtype=jnp.float32) / axis_dim)
    ang_r = row_idx[:, None] * inv_freq[None, :]
    ang_c = col_idx[:, None] * inv_freq[None, :]
    return (jnp.cos(ang_r).astype(dtype), jnp.sin(ang_r).astype(dtype),
            jnp.cos(ang_c).astype(dtype), jnp.sin(ang_c).astype(dtype))


def rotate(x, cos, sin):
    x1, x2 = jnp.split(x, 2, axis=-1)
    cos = cos[:, None, :]
    sin = sin[:, None, :]
    return jnp.concatenate([x1 * cos - x2 * sin, x1 * sin + x2 * cos], axis=-1)


def apply_axial_rope(x, tables):
    cos_r, sin_r, cos_c, sin_c = tables
    x_row, x_col = jnp.split(x, 2, axis=-1)
    return jnp.concatenate([rotate(x_row, cos_r, sin_r), rotate(x_col, cos_c, sin_c)], axis=-1)


def blocked_attention(q, k, v, scale):
    b, s, h, dk = q.shape
    hkv, dv = k.shape[2], v.shape[-1]
    rep = h // hkv
    nb = s // Q_BLOCK
    qb = q.reshape(b, nb, Q_BLOCK, hkv, rep, dk).transpose(1, 0, 2, 3, 4, 5)

    def attend(q_blk):
        logits = jnp.einsum('bqgrd,bsgd->bgrqs', q_blk, k).astype(jnp.float32) * scale
        probs = jax.nn.softmax(logits, axis=-1).astype(v.dtype)
        return jnp.einsum('bgrqs,bsgd->bqgrd', probs, v)

    out = lax.map(attend, qb)
    return out.transpose(1, 0, 2, 3, 4, 5).reshape(b, s, h * dv)


def segsum(a):
    t = a.shape[-1]
    a_rep = jnp.broadcast_to(a[..., :, None], a.shape + (t,))
    strict_lower = jnp.tril(jnp.ones((t, t), dtype=bool), k=-1)
    seg = jnp.cumsum(jnp.where(strict_lower, a_rep, 0.0), axis=-2)
    lower = jnp.tril(jnp.ones((t, t), dtype=bool))
    return jnp.where(lower, seg, -jnp.inf)


def ssd_chunked(x, dt, a_neg, bm, cm):
    b, s, h, p = x.shape
    g, n = bm.shape[2], bm.shape[3]
    e = h // g
    nc = s // SSD_CHUNK
    f32 = jnp.float32
    xd = (x.astype(f32) * dt[..., None]).reshape(b, nc, SSD_CHUNK, g, e, p)
    a = (dt * a_neg).reshape(b, nc, SSD_CHUNK, g, e).transpose(0, 3, 4, 1, 2)
    bc = bm.astype(f32).reshape(b, nc, SSD_CHUNK, g, n)
    cc = cm.astype(f32).reshape(b, nc, SSD_CHUNK, g, n)
    a_cs = jnp.cumsum(a, axis=-1)
    cb = jnp.einsum('bclgn,bcsgn->bgcls', cc, bc)
    w_diag = cb[:, :, None] * jnp.exp(segsum(a))
    y_diag = jnp.einsum('bgecls,bcsgep->bclgep', w_diag, xd)
    to_end = jnp.exp(a_cs[..., -1:] - a_cs).transpose(0, 3, 4, 1, 2)
    states = jnp.einsum('bclgn,bclgep->bcgepn', bc, xd * to_end[..., None])
    states = jnp.concatenate([jnp.zeros_like(states[:, :1]), states], axis=1)
    chunk_a = jnp.pad(a_cs[..., -1], ((0, 0), (0, 0), (0, 0), (1, 0)))
    chunk_decay = jnp.exp(segsum(chunk_a))
    states = jnp.einsum('bgezc,bcgepn->bzgepn', chunk_decay, states)[:, :-1]
    from_start = jnp.exp(a_cs).transpose(0, 3, 4, 1, 2)
    y_off = jnp.einsum('bclgn,bcgepn->bclgep', cc, states) * from_start[..., None]
    return (y_diag + y_off).reshape(b, s, h, p)


def depthwise_centred_conv(x, w):
    pad = SSD_CONV // 2
    return lax.conv_general_dilated(x, w[:, None, :], window_strides=(1,), padding=[(pad, pad)],
                                    dimension_numbers=('NWC', 'WIO', 'NWC'), feature_group_count=x.shape[-1])


def gqa_group(q, k, v, q_norm_g, k_norm_g, rope):
    b, s = q.shape[:2]
    q = q.reshape(b, s, GQA_HEADS, GQA_HEAD_DIM)
    k = k.reshape(b, s, GQA_KV_HEADS, GQA_HEAD_DIM)
    v = v.reshape(b, s, GQA_KV_HEADS, GQA_HEAD_DIM)
    q = apply_axial_rope(rms_norm(q, q_norm_g), rope)
    k = apply_axial_rope(rms_norm(k, k_norm_g), rope)
    return blocked_attention(q, k, v, GQA_HEAD_DIM ** -0.5)


def mla_group(c_q, c_kv, k_pe, q_norm_g, w_uq, kv_norm_g, w_ukv, rope):
    b, s = c_q.shape[:2]
    q = (rms_norm(c_q, q_norm_g) @ w_uq).reshape(b, s, MLA_HEADS, MLA_NOPE_DIM + MLA_ROPE_DIM)
    q_nope, q_pe = q[..., :MLA_NOPE_DIM], q[..., MLA_NOPE_DIM:]
    kv = (rms_norm(c_kv, kv_norm_g) @ w_ukv).reshape(b, s, MLA_HEADS, MLA_NOPE_DIM + MLA_V_DIM)
    k_nope, v = kv[..., :MLA_NOPE_DIM], kv[..., MLA_NOPE_DIM:]
    q_pe = apply_axial_rope(q_pe, rope)
    k_pe = apply_axial_rope(k_pe[:, :, None, :], rope)
    q = jnp.concatenate([q_nope, q_pe], axis=-1)
    k = jnp.concatenate([k_nope, jnp.broadcast_to(k_pe, (b, s, MLA_HEADS, MLA_ROPE_DIM))], axis=-1)
    return blocked_attention(q, k, v, (MLA_NOPE_DIM + MLA_ROPE_DIM) ** -0.5)


def ssd_group(z, xbc, dt_raw, conv_w, conv_b, dt_bias, a_log, d_skip, norm_g):
    b, s = z.shape[:2]
    xbc = jax.nn.silu(depthwise_centred_conv(xbc, conv_w) + conv_b)
    xs, bm, cm = jnp.split(xbc, [SSD_INNER, SSD_INNER + SSD_GROUPS * SSD_STATE], axis=-1)
    xs = xs.reshape(b, s, SSD_HEADS, SSD_HEAD_DIM)
    bm = bm.reshape(b, s, SSD_GROUPS, SSD_STATE)
    cm = cm.reshape(b, s, SSD_GROUPS, SSD_STATE)
    dt = jax.nn.softplus(dt_raw.reshape(b, s, 2, SSD_HEADS).astype(jnp.float32) + dt_bias)
    a_neg = -jnp.exp(a_log.astype(jnp.float32))
    y_fwd = ssd_chunked(xs, dt[:, :, 0], a_neg[0], bm, cm)
    flip = lambda t: jnp.flip(t, axis=1)
    y_bwd = flip(ssd_chunked(flip(xs), flip(dt[:, :, 1]), a_neg[1], flip(bm), flip(cm)))
    y = y_fwd + y_bwd + xs * d_skip[:, None]
    y = y.reshape(b, s, SSD_INNER) * jax.nn.silu(z)
    y = rms_norm(y.reshape(b, s, SSD_GROUPS, SSD_INNER // SSD_GROUPS), norm_g.reshape(SSD_GROUPS, -1))
    return y.reshape(b, s, SSD_INNER)


def hybrid_mixer(h, w_in, q_norm_g, k_norm_g, mla_q_norm_g, w_uq, mla_kv_norm_g, w_ukv,
                 conv_w, conv_b, dt_bias, a_log, d_skip, ssd_norm_g, w_out, rope_a, rope_b):
    proj = h @ w_in
    idx = np.cumsum(IN_SPLITS)[:-1].tolist()
    q_a, k_a, v_a, cq_b, ckv_b, kpe_b, z_c, xbc_c, dt_c = jnp.split(proj, idx, axis=-1)
    o_a = gqa_group(q_a, k_a, v_a, q_norm_g, k_norm_g, rope_a)
    o_b = mla_group(cq_b, ckv_b, kpe_b, mla_q_norm_g, w_uq, mla_kv_norm_g, w_ukv, rope_b)
    o_c = ssd_group(z_c, xbc_c, dt_c, conv_w, conv_b, dt_bias, a_log, d_skip, ssd_norm_g)
    o = jnp.concatenate([o_a.astype(h.dtype), o_b.astype(h.dtype), o_c.astype(h.dtype)], axis=-1)
    return o @ w_out


def swiglu_ffn(h, w_gate_up, w_down):
    gate, up = jnp.split(h @ w_gate_up, 2, axis=-1)
    return (jax.nn.silu(gate) * up) @ w_down


def _fwd_setup_inputs(seed: int = 0) -> dict:
    key = jax.random.key(seed)
    ks = iter(jax.random.split(key, 32))
    f32 = jnp.float32
    D, L = D_MODEL, DEPTH

    def nrm(shape, std):
        return std * jax.random.normal(next(ks), shape, f32)

    def gain(shape):
        return 1.0 + nrm(shape, 0.02)

    dt_init = jnp.exp(jax.random.uniform(next(ks), (L, 2, SSD_HEADS), f32, np.log(1e-3), np.log(1e-1)))
    dt_bias = dt_init + jnp.log(-jnp.expm1(-dt_init))
    a_log = jnp.log(jax.random.uniform(next(ks), (L, 2, SSD_HEADS), f32, 1.0, 16.0))
    return {
        'x': nrm((BATCH, SEQ, D), 1.0),
        'c': nrm((BATCH, D), 1.0),
        'w_ada': nrm((L, D, 6 * D), 0.5 * D ** -0.5),
        'b_ada': nrm((L, 6 * D), 0.01),
        'norm1_g': gain((L, D)),
        'norm2_g': gain((L, D)),
        'w_in': nrm((L, D, IN_COLS), D ** -0.5),
        'q_norm_g': gain((L, GQA_HEAD_DIM)),
        'k_norm_g': gain((L, GQA_HEAD_DIM)),
        'mla_q_norm_g': gain((L, MLA_Q_LORA)),
        'w_uq': nrm((L, MLA_Q_LORA, MLA_HEADS * (MLA_NOPE_DIM + MLA_ROPE_DIM)), MLA_Q_LORA ** -0.5),
        'mla_kv_norm_g': gain((L, MLA_KV_LORA)),
        'w_ukv': nrm((L, MLA_KV_LORA, MLA_HEADS * (MLA_NOPE_DIM + MLA_V_DIM)), MLA_KV_LORA ** -0.5),
        'conv_w': nrm((L, SSD_CONV, SSD_CONV_DIM), SSD_CONV ** -0.5),
        'conv_b': nrm((L, SSD_CONV_DIM), 0.01),
        'dt_bias': dt_bias,
        'a_log': a_log,
        'd_skip': gain((L, SSD_HEADS)),
        'ssd_norm_g': gain((L, SSD_INNER)),
        'w_out': nrm((L, MIX_WIDTH, D), MIX_WIDTH ** -0.5),
        'w_gate_up': nrm((L, D, 2 * FFN_HIDDEN), D ** -0.5),
        'w_down': nrm((L, FFN_HIDDEN, D), FFN_HIDDEN ** -0.5),
        'final_norm_g': gain((D,)),
    }


def _fwd_reference(x, c, w_ada, b_ada, norm1_g, norm2_g, w_in, q_norm_g, k_norm_g, mla_q_norm_g, w_uq,
              mla_kv_norm_g, w_ukv, conv_w, conv_b, dt_bias, a_log, d_skip, ssd_norm_g, w_out,
              w_gate_up, w_down, final_norm_g):
    seq_len = x.shape[1]
    rope_a = axial_rope_tables(seq_len, GQA_HEAD_DIM, x.dtype)
    rope_b = axial_rope_tables(seq_len, MLA_ROPE_DIM, x.dtype)
    c_act = jax.nn.silu(c)
    for l in range(DEPTH):
        mod = c_act @ w_ada[l] + b_ada[l]
        shift1, scale1, gate1, shift2, scale2, gate2 = [m[:, None, :] for m in jnp.split(mod, 6, axis=-1)]
        h = rms_norm(x, norm1_g[l]) * (1 + scale1) + shift1
        mix = hybrid_mixer(h, w_in[l], q_norm_g[l], k_norm_g[l], mla_q_norm_g[l], w_uq[l], mla_kv_norm_g[l],
                           w_ukv[l], conv_w[l], conv_b[l], dt_bias[l], a_log[l], d_skip[l], ssd_norm_g[l],
                           w_out[l], rope_a, rope_b)
        x = x + gate1 * mix
        h = rms_norm(x, norm2_g[l]) * (1 + scale2) + shift2
        x = x + gate2 * swiglu_ffn(h, w_gate_up[l], w_down[l])
    return rms_norm(x, final_norm_g)


import jax as _jax
import jax.numpy as _jnp

TWIN_FORMAT = 'train_step'
FWD_PARAMS = ['x', 'c', 'w_ada', 'b_ada', 'norm1_g', 'norm2_g', 'w_in', 'q_norm_g', 'k_norm_g', 'mla_q_norm_g', 'w_uq', 'mla_kv_norm_g', 'w_ukv', 'conv_w', 'conv_b', 'dt_bias', 'a_log', 'd_skip', 'ssd_norm_g', 'w_out', 'w_gate_up', 'w_down', 'final_norm_g']
TWIN_WEIGHTS = ['w_ada', 'b_ada', 'norm1_g', 'norm2_g', 'w_in', 'q_norm_g', 'k_norm_g', 'mla_q_norm_g', 'w_uq', 'mla_kv_norm_g', 'w_ukv', 'conv_w', 'conv_b', 'dt_bias', 'a_log', 'd_skip', 'ssd_norm_g', 'w_out', 'w_gate_up', 'w_down', 'final_norm_g']
TWIN_DIFF_INPUT = 'x'
TWIN_INPUTS = ['x', 'c', 'w_ada', 'b_ada', 'norm1_g', 'norm2_g', 'w_in', 'q_norm_g', 'k_norm_g', 'mla_q_norm_g', 'w_uq', 'mla_kv_norm_g', 'w_ukv', 'conv_w', 'conv_b', 'dt_bias', 'a_log', 'd_skip', 'ssd_norm_g', 'w_out', 'w_gate_up', 'w_down', 'final_norm_g', 'loss_target', 'm_w_ada', 'm_b_ada', 'm_norm1_g', 'm_norm2_g', 'm_w_in', 'm_q_norm_g', 'm_k_norm_g', 'm_mla_q_norm_g', 'm_w_uq', 'm_mla_kv_norm_g', 'm_w_ukv', 'm_conv_w', 'm_conv_b', 'm_dt_bias', 'm_a_log', 'm_d_skip', 'm_ssd_norm_g', 'm_w_out', 'm_w_gate_up', 'm_w_down', 'm_final_norm_g', 'v_w_ada', 'v_b_ada', 'v_norm1_g', 'v_norm2_g', 'v_w_in', 'v_q_norm_g', 'v_k_norm_g', 'v_mla_q_norm_g', 'v_w_uq', 'v_mla_kv_norm_g', 'v_w_ukv', 'v_conv_w', 'v_conv_b', 'v_dt_bias', 'v_a_log', 'v_d_skip', 'v_ssd_norm_g', 'v_w_out', 'v_w_gate_up', 'v_w_down', 'v_final_norm_g']
TWIN_OUTPUTS = ['loss', 'grad_x', 'grad_w_ada', 'grad_b_ada', 'grad_norm1_g', 'grad_norm2_g', 'grad_w_in', 'grad_q_norm_g', 'grad_k_norm_g', 'grad_mla_q_norm_g', 'grad_w_uq', 'grad_mla_kv_norm_g', 'grad_w_ukv', 'grad_conv_w', 'grad_conv_b', 'grad_dt_bias', 'grad_a_log', 'grad_d_skip', 'grad_ssd_norm_g', 'grad_w_out', 'grad_w_gate_up', 'grad_w_down', 'grad_final_norm_g', 'delta_w_ada', 'delta_b_ada', 'delta_norm1_g', 'delta_norm2_g', 'delta_w_in', 'delta_q_norm_g', 'delta_k_norm_g', 'delta_mla_q_norm_g', 'delta_w_uq', 'delta_mla_kv_norm_g', 'delta_w_ukv', 'delta_conv_w', 'delta_conv_b', 'delta_dt_bias', 'delta_a_log', 'delta_d_skip', 'delta_ssd_norm_g', 'delta_w_out', 'delta_w_gate_up', 'delta_w_down', 'delta_final_norm_g', 'new_m_w_ada', 'new_m_b_ada', 'new_m_norm1_g', 'new_m_norm2_g', 'new_m_w_in', 'new_m_q_norm_g', 'new_m_k_norm_g', 'new_m_mla_q_norm_g', 'new_m_w_uq', 'new_m_mla_kv_norm_g', 'new_m_w_ukv', 'new_m_conv_w', 'new_m_conv_b', 'new_m_dt_bias', 'new_m_a_log', 'new_m_d_skip', 'new_m_ssd_norm_g', 'new_m_w_out', 'new_m_w_gate_up', 'new_m_w_down', 'new_m_final_norm_g', 'new_v_w_ada', 'new_v_b_ada', 'new_v_norm1_g', 'new_v_norm2_g', 'new_v_w_in', 'new_v_q_norm_g', 'new_v_k_norm_g', 'new_v_mla_q_norm_g', 'new_v_w_uq', 'new_v_mla_kv_norm_g', 'new_v_w_ukv', 'new_v_conv_w', 'new_v_conv_b', 'new_v_dt_bias', 'new_v_a_log', 'new_v_d_skip', 'new_v_ssd_norm_g', 'new_v_w_out', 'new_v_w_gate_up', 'new_v_w_down', 'new_v_final_norm_g']
TWIN_LEAF_KINDS = {'loss': 'loss', 'grad_x': 'grad_x', 'grad_w_ada': 'grad_w', 'grad_b_ada': 'grad_w', 'grad_norm1_g': 'grad_w', 'grad_norm2_g': 'grad_w', 'grad_w_in': 'grad_w', 'grad_q_norm_g': 'grad_w', 'grad_k_norm_g': 'grad_w', 'grad_mla_q_norm_g': 'grad_w', 'grad_w_uq': 'grad_w', 'grad_mla_kv_norm_g': 'grad_w', 'grad_w_ukv': 'grad_w', 'grad_conv_w': 'grad_w', 'grad_conv_b': 'grad_w', 'grad_dt_bias': 'grad_w', 'grad_a_log': 'grad_w', 'grad_d_skip': 'grad_w', 'grad_ssd_norm_g': 'grad_w', 'grad_w_out': 'grad_w', 'grad_w_gate_up': 'grad_w', 'grad_w_down': 'grad_w', 'grad_final_norm_g': 'grad_w', 'delta_w_ada': 'delta_w', 'delta_b_ada': 'delta_w', 'delta_norm1_g': 'delta_w', 'delta_norm2_g': 'delta_w', 'delta_w_in': 'delta_w', 'delta_q_norm_g': 'delta_w', 'delta_k_norm_g': 'delta_w', 'delta_mla_q_norm_g': 'delta_w', 'delta_w_uq': 'delta_w', 'delta_mla_kv_norm_g': 'delta_w', 'delta_w_ukv': 'delta_w', 'delta_conv_w': 'delta_w', 'delta_conv_b': 'delta_w', 'delta_dt_bias': 'delta_w', 'delta_a_log': 'delta_w', 'delta_d_skip': 'delta_w', 'delta_ssd_norm_g': 'delta_w', 'delta_w_out': 'delta_w', 'delta_w_gate_up': 'delta_w', 'delta_w_down': 'delta_w', 'delta_final_norm_g': 'delta_w', 'new_m_w_ada': 'new_m', 'new_m_b_ada': 'new_m', 'new_m_norm1_g': 'new_m', 'new_m_norm2_g': 'new_m', 'new_m_w_in': 'new_m', 'new_m_q_norm_g': 'new_m', 'new_m_k_norm_g': 'new_m', 'new_m_mla_q_norm_g': 'new_m', 'new_m_w_uq': 'new_m', 'new_m_mla_kv_norm_g': 'new_m', 'new_m_w_ukv': 'new_m', 'new_m_conv_w': 'new_m', 'new_m_conv_b': 'new_m', 'new_m_dt_bias': 'new_m', 'new_m_a_log': 'new_m', 'new_m_d_skip': 'new_m', 'new_m_ssd_norm_g': 'new_m', 'new_m_w_out': 'new_m', 'new_m_w_gate_up': 'new_m', 'new_m_w_down': 'new_m', 'new_m_final_norm_g': 'new_m', 'new_v_w_ada': 'new_v', 'new_v_b_ada': 'new_v', 'new_v_norm1_g': 'new_v', 'new_v_norm2_g': 'new_v', 'new_v_w_in': 'new_v', 'new_v_q_norm_g': 'new_v', 'new_v_k_norm_g': 'new_v', 'new_v_mla_q_norm_g': 'new_v', 'new_v_w_uq': 'new_v', 'new_v_mla_kv_norm_g': 'new_v', 'new_v_w_ukv': 'new_v', 'new_v_conv_w': 'new_v', 'new_v_conv_b': 'new_v', 'new_v_dt_bias': 'new_v', 'new_v_a_log': 'new_v', 'new_v_d_skip': 'new_v', 'new_v_ssd_norm_g': 'new_v', 'new_v_w_out': 'new_v', 'new_v_w_gate_up': 'new_v', 'new_v_w_down': 'new_v', 'new_v_final_norm_g': 'new_v'}


def _forward(args):
    return _fwd_reference(*[args[k] for k in FWD_PARAMS])


def _output_shape():
    out = _jax.eval_shape(lambda: _forward(_fwd_setup_inputs(0)))
    return out.shape, out.dtype

N_MICROBATCH = 1
ADAM_LR = 0.001
ADAM_B1 = 0.9
ADAM_B2 = 0.999
ADAM_EPS = 1e-08
ADAM_WD = 0.01
ADAM_STEP = 10
PER_EXAMPLE_BATCH_AXIS = {'x': 0, 'c': 0, 'loss_target': 0}
SHARED_INPUTS = []
_WEIGHT_DTYPES = {'w_ada': _jnp.float32, 'b_ada': _jnp.float32, 'norm1_g': _jnp.float32, 'norm2_g': _jnp.float32, 'w_in': _jnp.float32, 'q_norm_g': _jnp.float32, 'k_norm_g': _jnp.float32, 'mla_q_norm_g': _jnp.float32, 'w_uq': _jnp.float32, 'mla_kv_norm_g': _jnp.float32, 'w_ukv': _jnp.float32, 'conv_w': _jnp.float32, 'conv_b': _jnp.float32, 'dt_bias': _jnp.float32, 'a_log': _jnp.float32, 'd_skip': _jnp.float32, 'ssd_norm_g': _jnp.float32, 'w_out': _jnp.float32, 'w_gate_up': _jnp.float32, 'w_down': _jnp.float32, 'final_norm_g': _jnp.float32}
MOMENT_SCALE = {'w_ada': 2.507053e-02, 'b_ada': 4.240943e-02, 'norm1_g': 2.255146e-02, 'norm2_g': 2.498299e-02, 'w_in': 1.767615e-02, 'q_norm_g': 6.310263e-03, 'k_norm_g': 6.234270e-03, 'mla_q_norm_g': 2.818031e-03, 'w_uq': 2.286155e-03, 'mla_kv_norm_g': 1.301367e-02, 'w_ukv': 5.866110e-03, 'conv_w': 2.128445e-02, 'conv_b': 3.104218e-02, 'dt_bias': 5.261734e-02, 'a_log': 8.964179e-02, 'd_skip': 8.148994e-02, 'ssd_norm_g': 2.745132e-02, 'w_out': 1.744663e-02, 'w_gate_up': 1.096245e-02, 'w_down': 1.787889e-02, 'final_norm_g': 1.597771e+01}


def _to_microbatches(a, axis):
    t = _jnp.moveaxis(a, axis, 0)
    t = t.reshape((N_MICROBATCH, t.shape[0] // N_MICROBATCH) + t.shape[1:])
    return _jnp.moveaxis(t, 1, axis + 1)


def setup_inputs(seed: int = 0) -> dict:
    inp = _fwd_setup_inputs(seed)
    key = _jax.random.fold_in(_jax.random.key(seed), 7919)
    shape, _ = _output_shape()
    out = dict(inp)
    out["loss_target"] = _jax.random.normal(_jax.random.fold_in(key, 0), shape, _jnp.float32)
    for i, name in enumerate(TWIN_WEIGHTS):
        w = inp[name].astype(_jnp.float32)
        if MOMENT_SCALE is None:
            s = _jnp.sqrt(_jnp.mean(_jnp.square(w)) + 1e-30)
        else:
            s = MOMENT_SCALE[name]
        km, kv = _jax.random.split(_jax.random.fold_in(key, i + 1))
        out[name] = w
        out["m_" + name] = s * _jax.random.normal(km, w.shape, _jnp.float32)
        out["v_" + name] = (s * s) * _jax.random.uniform(kv, w.shape, _jnp.float32, 0.5, 1.5)
    if N_MICROBATCH > 1:
        for name, axis in PER_EXAMPLE_BATCH_AXIS.items():
            out[name] = _to_microbatches(out[name], axis)
    return {'x': out['x'], 'c': out['c'], 'w_ada': out['w_ada'], 'b_ada': out['b_ada'], 'norm1_g': out['norm1_g'], 'norm2_g': out['norm2_g'], 'w_in': out['w_in'], 'q_norm_g': out['q_norm_g'], 'k_norm_g': out['k_norm_g'], 'mla_q_norm_g': out['mla_q_norm_g'], 'w_uq': out['w_uq'], 'mla_kv_norm_g': out['mla_kv_norm_g'], 'w_ukv': out['w_ukv'], 'conv_w': out['conv_w'], 'conv_b': out['conv_b'], 'dt_bias': out['dt_bias'], 'a_log': out['a_log'], 'd_skip': out['d_skip'], 'ssd_norm_g': out['ssd_norm_g'], 'w_out': out['w_out'], 'w_gate_up': out['w_gate_up'], 'w_down': out['w_down'], 'final_norm_g': out['final_norm_g'], 'loss_target': out['loss_target'], 'm_w_ada': out['m_w_ada'], 'm_b_ada': out['m_b_ada'], 'm_norm1_g': out['m_norm1_g'], 'm_norm2_g': out['m_norm2_g'], 'm_w_in': out['m_w_in'], 'm_q_norm_g': out['m_q_norm_g'], 'm_k_norm_g': out['m_k_norm_g'], 'm_mla_q_norm_g': out['m_mla_q_norm_g'], 'm_w_uq': out['m_w_uq'], 'm_mla_kv_norm_g': out['m_mla_kv_norm_g'], 'm_w_ukv': out['m_w_ukv'], 'm_conv_w': out['m_conv_w'], 'm_conv_b': out['m_conv_b'], 'm_dt_bias': out['m_dt_bias'], 'm_a_log': out['m_a_log'], 'm_d_skip': out['m_d_skip'], 'm_ssd_norm_g': out['m_ssd_norm_g'], 'm_w_out': out['m_w_out'], 'm_w_gate_up': out['m_w_gate_up'], 'm_w_down': out['m_w_down'], 'm_final_norm_g': out['m_final_norm_g'], 'v_w_ada': out['v_w_ada'], 'v_b_ada': out['v_b_ada'], 'v_norm1_g': out['v_norm1_g'], 'v_norm2_g': out['v_norm2_g'], 'v_w_in': out['v_w_in'], 'v_q_norm_g': out['v_q_norm_g'], 'v_k_norm_g': out['v_k_norm_g'], 'v_mla_q_norm_g': out['v_mla_q_norm_g'], 'v_w_uq': out['v_w_uq'], 'v_mla_kv_norm_g': out['v_mla_kv_norm_g'], 'v_w_ukv': out['v_w_ukv'], 'v_conv_w': out['v_conv_w'], 'v_conv_b': out['v_conv_b'], 'v_dt_bias': out['v_dt_bias'], 'v_a_log': out['v_a_log'], 'v_d_skip': out['v_d_skip'], 'v_ssd_norm_g': out['v_ssd_norm_g'], 'v_w_out': out['v_w_out'], 'v_w_gate_up': out['v_w_gate_up'], 'v_w_down': out['v_w_down'], 'v_final_norm_g': out['v_final_norm_g']}


def _loss(weights, diff, rest, loss_target):
    with _jax.named_scope("forward"):
        args = {**rest, TWIN_DIFF_INPUT: diff, **{k: w.astype(_WEIGHT_DTYPES[k]) for k, w in weights.items()}}
        y = _forward(args)
    with _jax.named_scope("loss_head"):
        err = _jnp.square(y.astype(_jnp.float32) - loss_target)
        return 0.5 * _jnp.sum(_jnp.mean(err, axis=-1)) if err.ndim else 0.5 * err


def _adamw(w, g, m, v):
    m = ADAM_B1 * m + (1.0 - ADAM_B1) * g
    v = ADAM_B2 * v + (1.0 - ADAM_B2) * _jnp.square(g)
    m_hat = m / (1.0 - ADAM_B1 ** ADAM_STEP)
    v_hat = v / (1.0 - ADAM_B2 ** ADAM_STEP)
    delta = -ADAM_LR * (m_hat / (_jnp.sqrt(v_hat) + ADAM_EPS) + ADAM_WD * w)
    return delta, m, v


def reference(x, c, w_ada, b_ada, norm1_g, norm2_g, w_in, q_norm_g, k_norm_g, mla_q_norm_g, w_uq, mla_kv_norm_g, w_ukv, conv_w, conv_b, dt_bias, a_log, d_skip, ssd_norm_g, w_out, w_gate_up, w_down, final_norm_g, loss_target, m_w_ada, m_b_ada, m_norm1_g, m_norm2_g, m_w_in, m_q_norm_g, m_k_norm_g, m_mla_q_norm_g, m_w_uq, m_mla_kv_norm_g, m_w_ukv, m_conv_w, m_conv_b, m_dt_bias, m_a_log, m_d_skip, m_ssd_norm_g, m_w_out, m_w_gate_up, m_w_down, m_final_norm_g, v_w_ada, v_b_ada, v_norm1_g, v_norm2_g, v_w_in, v_q_norm_g, v_k_norm_g, v_mla_q_norm_g, v_w_uq, v_mla_kv_norm_g, v_w_ukv, v_conv_w, v_conv_b, v_dt_bias, v_a_log, v_d_skip, v_ssd_norm_g, v_w_out, v_w_gate_up, v_w_down, v_final_norm_g):
    given = dict(x=x, c=c, w_ada=w_ada, b_ada=b_ada, norm1_g=norm1_g, norm2_g=norm2_g, w_in=w_in, q_norm_g=q_norm_g, k_norm_g=k_norm_g, mla_q_norm_g=mla_q_norm_g, w_uq=w_uq, mla_kv_norm_g=mla_kv_norm_g, w_ukv=w_ukv, conv_w=conv_w, conv_b=conv_b, dt_bias=dt_bias, a_log=a_log, d_skip=d_skip, ssd_norm_g=ssd_norm_g, w_out=w_out, w_gate_up=w_gate_up, w_down=w_down, final_norm_g=final_norm_g, loss_target=loss_target, m_w_ada=m_w_ada, m_b_ada=m_b_ada, m_norm1_g=m_norm1_g, m_norm2_g=m_norm2_g, m_w_in=m_w_in, m_q_norm_g=m_q_norm_g, m_k_norm_g=m_k_norm_g, m_mla_q_norm_g=m_mla_q_norm_g, m_w_uq=m_w_uq, m_mla_kv_norm_g=m_mla_kv_norm_g, m_w_ukv=m_w_ukv, m_conv_w=m_conv_w, m_conv_b=m_conv_b, m_dt_bias=m_dt_bias, m_a_log=m_a_log, m_d_skip=m_d_skip, m_ssd_norm_g=m_ssd_norm_g, m_w_out=m_w_out, m_w_gate_up=m_w_gate_up, m_w_down=m_w_down, m_final_norm_g=m_final_norm_g, v_w_ada=v_w_ada, v_b_ada=v_b_ada, v_norm1_g=v_norm1_g, v_norm2_g=v_norm2_g, v_w_in=v_w_in, v_q_norm_g=v_q_norm_g, v_k_norm_g=v_k_norm_g, v_mla_q_norm_g=v_mla_q_norm_g, v_w_uq=v_w_uq, v_mla_kv_norm_g=v_mla_kv_norm_g, v_w_ukv=v_w_ukv, v_conv_w=v_conv_w, v_conv_b=v_conv_b, v_dt_bias=v_dt_bias, v_a_log=v_a_log, v_d_skip=v_d_skip, v_ssd_norm_g=v_ssd_norm_g, v_w_out=v_w_out, v_w_gate_up=v_w_gate_up, v_w_down=v_w_down, v_final_norm_g=v_final_norm_g)
    weights = {n: given[n] for n in TWIN_WEIGHTS}
    shared = {n: given[n] for n in SHARED_INPUTS}
    per_example = {n: given[n] for n in ['x', 'c']}
    grad_fn = _jax.value_and_grad(_loss, argnums=(0, 1))

    def one_microbatch(ex, loss_target):
        ex = dict(ex)
        diff = ex.pop(TWIN_DIFF_INPUT)
        return grad_fn(weights, diff, {**shared, **ex}, loss_target)

    if N_MICROBATCH == 1:
        loss, (grad_w, grad_x) = one_microbatch(per_example, given["loss_target"])
    else:
        def body(carry, xs):
            loss_sum, grad_sum = carry
            l_k, (gw_k, gx_k) = one_microbatch(xs[0], xs[1])
            with _jax.named_scope("update"):
                return (loss_sum + l_k, _jax.tree.map(_jnp.add, grad_sum, gw_k)), gx_k

        init = (_jnp.zeros((), _jnp.float32), _jax.tree.map(_jnp.zeros_like, weights))
        (loss, grad_w), grad_x = _jax.lax.scan(body, init, (per_example, given["loss_target"]))
    with _jax.named_scope("update"):
        delta_w, new_m, new_v = {}, {}, {}
        for n in TWIN_WEIGHTS:
            delta_w[n], new_m[n], new_v[n] = _adamw(weights[n], grad_w[n], given["m_" + n], given["v_" + n])
    return (loss, grad_x, *[grad_w[n] for n in TWIN_WEIGHTS], *[delta_w[n] for n in TWIN_WEIGHTS],
            *[new_m[n] for n in TWIN_WEIGHTS], *[new_v[n] for n in TWIN_WEIGHTS])
```

```python
import functools
import math

import jax
import jax.numpy as jnp
import numpy as np
from jax import lax
from jax.experimental import pallas as pl
from jax.experimental.pallas import tpu as pltpu

F32 = jnp.float32
BF16 = jnp.bfloat16
N_DEV = 8
EPS = 1e-6
ROPE_THETA = 10000.0
GRID_W = 64

GQA_H, GQA_KV, HEAD = 6, 2, 128
MLA_H, MLA_QL, MLA_KVL, MLA_NOPE, MLA_ROPE, MLA_V = 4, 512, 256, 128, 64, 128
MLA_DK = 256
SSD_H, SSD_P, SSD_G, SSD_N, SSD_K, CHUNK = 12, 64, 2, 128, 5, 128
SSD_INNER = SSD_H * SSD_P
SSD_CONV_DIM = SSD_INNER + 2 * SSD_G * SSD_N
N_PAIR = SSD_H // 2
LANES = 128
IN_SPLITS = (768, 256, 256, 512, 256, 64, 768, 1280, 24)
IN_COLS = sum(IN_SPLITS)
SEG_ORDER = (0, 1, 2, 3, 4, 6, 7, 5, 8)
SEG_PAD = {5: 128, 8: 128}

ADAM_LR, ADAM_B1, ADAM_B2, ADAM_EPS, ADAM_WD, ADAM_STEP = 0.001, 0.9, 0.999, 1e-08, 0.01, 10
VMEM_LIMIT = 56 * 1024 * 1024


def _cp(**kw):
    return pltpu.CompilerParams(vmem_limit_bytes=VMEM_LIMIT, **kw)


def _tile(dim, cap, mult=128):
    if dim <= cap:
        return dim
    best = None
    t = mult
    while t <= cap:
        if dim % t == 0:
            best = t
        t += mult
    assert best is not None, (dim, cap)
    return best


def _sigmoid(x):
    return 1.0 / (1.0 + jnp.exp(-x))


def _dot(a, b, dims):
    return lax.dot_general(a, b, (dims, ((), ())), preferred_element_type=F32)


NN = ((1,), (0,))
NT = ((1,), (1,))
TN = ((0,), (0,))


def _dotf(a, b, dims=NN):
    return lax.dot_general(a, b, (dims, ((), ())), preferred_element_type=F32,
                           precision=lax.Precision.HIGHEST)


def _bf(x):
    return x.astype(BF16)


def mm(a, b, *, ta=False, tb=False, out_dtype=F32, a_silu=False, name):
    if ta:
        K, M = a.shape
    else:
        M, K = a.shape
    if tb:
        N, K2 = b.shape
    else:
        K2, N = b.shape
    assert K == K2, (a.shape, b.shape, ta, tb)
    tm, tn, tk = _tile(M, 512), _tile(N, 512), _tile(K, 1024)
    nk = K // tk
    dims = ((0 if ta else 1,), (1 if tb else 0,))

    def body(a_ref, b_ref, o_ref, acc_ref):
        k = pl.program_id(2)

        @pl.when(k == 0)
        def _():
            acc_ref[...] = jnp.zeros_like(acc_ref)

        av = a_ref[...]
        if a_silu:
            av = av.astype(F32)
            av = av * _sigmoid(av)
        acc_ref[...] += _dot(_bf(av), _bf(b_ref[...]), dims)

        @pl.when(k == nk - 1)
        def _():
            o_ref[...] = acc_ref[...].astype(o_ref.dtype)

    a_spec = (pl.BlockSpec((tk, tm), lambda i, j, k: (k, i)) if ta
              else pl.BlockSpec((tm, tk), lambda i, j, k: (i, k)))
    b_spec = (pl.BlockSpec((tn, tk), lambda i, j, k: (j, k)) if tb
              else pl.BlockSpec((tk, tn), lambda i, j, k: (k, j)))
    return pl.pallas_call(
        body, name=name, grid=(M // tm, N // tn, nk),
        in_specs=[a_spec, b_spec],
        out_specs=pl.BlockSpec((tm, tn), lambda i, j, k: (i, j)),
        out_shape=jax.ShapeDtypeStruct((M, N), out_dtype),
        scratch_shapes=[pltpu.VMEM((tm, tn), F32)],
        compiler_params=_cp(dimension_semantics=("parallel", "parallel", "arbitrary")),
    )(a, b)


def linear(a, w, *, out_dtype, name, a_silu=False):
    @jax.custom_vjp
    def f(a, w):
        return mm(a, w, out_dtype=out_dtype, a_silu=a_silu, name=name + "_fwd")

    def fwd(a, w):
        return f(a, w), (a, w)

    def bwd(res, g):
        a, w = res
        if a_silu:
            da = jnp.zeros_like(a)
        else:
            da = mm(g, w, tb=True, out_dtype=a.dtype, name=name + "_da")
        dw = mm(a, g, ta=True, out_dtype=w.dtype, a_silu=a_silu, name=name + "_dw")
        return da, dw

    f.defvjp(fwd, bwd)
    return f(a, w)


def rmsmod(x, g, scale, shift, *, name):
    B, S, D = x.shape
    ts = _tile(S, 256, 8)
    row = pl.BlockSpec((1, ts, D), lambda b, j: (b, j, 0))
    per_b = pl.BlockSpec((1, 1, D), lambda b, j: (b, 0, 0))
    gspec = pl.BlockSpec((1, D), lambda b, j: (0, 0))

    def fwd_call(x, g, scale, shift):
        def body(x_ref, g_ref, sc_ref, sh_ref, o_ref):
            xv = x_ref[0]
            r = lax.rsqrt(jnp.mean(xv * xv, axis=-1, keepdims=True) + EPS)
            y = xv * r * g_ref[...]
            o_ref[0] = (y * (1.0 + sc_ref[0]) + sh_ref[0]).astype(o_ref.dtype)

        return pl.pallas_call(
            body, name=name + "_fwd", grid=(B, S // ts),
            in_specs=[row, gspec, per_b, per_b], out_specs=row,
            out_shape=jax.ShapeDtypeStruct((B, S, D), BF16), compiler_params=_cp(),
        )(x, g, scale, shift)

    def bwd_call(x, g, scale, dh):
        def body(x_ref, g_ref, sc_ref, dh_ref, dx_ref, dg_ref, dsc_ref, dsh_ref):
            j = pl.program_id(1)
            xv = x_ref[0]
            dh = dh_ref[0].astype(F32)
            r = lax.rsqrt(jnp.mean(xv * xv, axis=-1, keepdims=True) + EPS)
            xn = xv * r
            gv = g_ref[...]
            dy = dh * (1.0 + sc_ref[0])
            dxn = dy * gv
            dx_ref[0] = r * (dxn - xn * jnp.mean(dxn * xn, axis=-1, keepdims=True))

            @pl.when(j == 0)
            def _():
                dg_ref[...] = jnp.zeros_like(dg_ref)
                dsc_ref[...] = jnp.zeros_like(dsc_ref)
                dsh_ref[...] = jnp.zeros_like(dsh_ref)

            dg_ref[0] += jnp.sum(dy * xn, axis=0, keepdims=True)
            dsc_ref[0] += jnp.sum(dh * xn * gv, axis=0, keepdims=True)
            dsh_ref[0] += jnp.sum(dh, axis=0, keepdims=True)

        vec = jax.ShapeDtypeStruct((B, 1, D), F32)
        return pl.pallas_call(
            body, name=name + "_bwd", grid=(B, S // ts),
            in_specs=[row, gspec, per_b, row], out_specs=[row, per_b, per_b, per_b],
            out_shape=[jax.ShapeDtypeStruct((B, S, D), F32), vec, vec, vec], compiler_params=_cp(),
        )(x, g, scale, dh)

    @jax.custom_vjp
    def f(x, g, scale, shift):
        return fwd_call(x, g, scale, shift)

    def fwd(x, g, scale, shift):
        return fwd_call(x, g, scale, shift), (x, g, scale)

    def bwd(res, dh):
        x, g, scale = res
        dx, dg, dsc, dsh = bwd_call(x, g, scale, dh)
        return dx, jnp.sum(dg, axis=0), dsc, dsh

    f.defvjp(fwd, bwd)
    return f(x, g, scale, shift)


def group_rmsnorm(x, g, *, gs, out_dtype, name):
    T, W = x.shape
    ng = W // gs
    tr = _tile(T, 512, 8)
    row = pl.BlockSpec((tr, W), lambda i: (i, 0))
    gspec = pl.BlockSpec((1, W), lambda i: (0, 0))

    def fwd_call(x, g):
        def body(x_ref, g_ref, o_ref):
            for i in range(ng):
                sl = slice(i * gs, (i + 1) * gs)
                xv = x_ref[:, sl]
                r = lax.rsqrt(jnp.mean(xv * xv, axis=-1, keepdims=True) + EPS)
                o_ref[:, sl] = (xv * r * g_ref[:, sl]).astype(o_ref.dtype)

        return pl.pallas_call(
            body, name=name + "_fwd", grid=(T // tr,), in_specs=[row, gspec], out_specs=row,
            out_shape=jax.ShapeDtypeStruct((T, W), out_dtype), compiler_params=_cp(),
        )(x, g)

    def bwd_call(x, g, dy):
        def body(x_ref, g_ref, dy_ref, dx_ref, dg_ref):
            @pl.when(pl.program_id(0) == 0)
            def _():
                dg_ref[...] = jnp.zeros_like(dg_ref)

            for i in range(ng):
                sl = slice(i * gs, (i + 1) * gs)
                xv = x_ref[:, sl]
                dyv = dy_ref[:, sl].astype(F32)
                r = lax.rsqrt(jnp.mean(xv * xv, axis=-1, keepdims=True) + EPS)
                xn = xv * r
                dxn = dyv * g_ref[:, sl]
                dx_ref[:, sl] = r * (dxn - xn * jnp.mean(dxn * xn, axis=-1, keepdims=True))
                dg_ref[:, sl] += jnp.sum(dyv * xn, axis=0, keepdims=True)

        return pl.pallas_call(
            body, name=name + "_bwd", grid=(T // tr,), in_specs=[row, gspec, row],
            out_specs=[row, gspec],
            out_shape=[jax.ShapeDtypeStruct((T, W), F32), jax.ShapeDtypeStruct((1, W), F32)],
            compiler_params=_cp(),
        )(x, g, dy)

    @jax.custom_vjp
    def f(x, g):
        return fwd_call(x, g)

    def fwd(x, g):
        return fwd_call(x, g), (x, g)

    def bwd(res, dy):
        return bwd_call(res[0], res[1], dy)

    f.defvjp(fwd, bwd)
    return f(x, g)


def rope_tables(seq_len, rot_dim, width):
    rows = seq_len // GRID_W
    row_idx = jnp.repeat(jnp.arange(rows), GRID_W).astype(F32)
    col_idx = jnp.tile(jnp.arange(GRID_W), rows).astype(F32)
    axis_dim = rot_dim // 2
    inv_freq = jnp.power(ROPE_THETA, -jnp.arange(0, axis_dim, 2, dtype=F32) / axis_dim)
    ang_r = row_idx[:, None] * inv_freq[None, :]
    ang_c = col_idx[:, None] * inv_freq[None, :]
    cos = jnp.concatenate([jnp.cos(ang_r), jnp.cos(ang_r), jnp.cos(ang_c), jnp.cos(ang_c)], axis=-1)
    sin = jnp.concatenate([-jnp.sin(ang_r), jnp.sin(ang_r), -jnp.sin(ang_c), jnp.sin(ang_c)], axis=-1)
    reps = width // rot_dim
    return jnp.tile(cos, (1, reps)), jnp.tile(sin, (1, reps))


def rope(x, cos, sin, *, d, name):
    B, S, W = x.shape
    ts = _tile(S, 512, 8)
    row = pl.BlockSpec((1, ts, W), lambda b, j: (b, j, 0))
    tab = pl.BlockSpec((ts, W), lambda b, j: (j, 0))

    def call(x, inverse, nm):
        def body(x_ref, c_ref, s_ref, o_ref):
            xv = x_ref[0]
            lane = lax.broadcasted_iota(jnp.int32, xv.shape, 1)
            first = (lane // d) % 2 == 0

            def swap(v):
                return jnp.where(first, pltpu.roll(v, W - d, 1), pltpu.roll(v, d, 1))

            if inverse:
                o_ref[0] = xv * c_ref[...] + swap(xv * s_ref[...])
            else:
                o_ref[0] = xv * c_ref[...] + swap(xv) * s_ref[...]

        return pl.pallas_call(
            body, name=nm, grid=(B, S // ts), in_specs=[row, tab, tab], out_specs=row,
            out_shape=jax.ShapeDtypeStruct((B, S, W), F32), compiler_params=_cp(),
        )(x, cos, sin)

    @jax.custom_vjp
    def f(x):
        return call(x, False, name + "_fwd")

    def fwd(x):
        return call(x, False, name + "_fwd"), None

    def bwd(_, g):
        return (call(g, True, name + "_bwd"),)

    f.defvjp(fwd, bwd)
    return f(x)


def attention(q, k, v, *, H, Hkv, dk, dv, scale, name):
    B, S, _ = q.shape
    rep = H // Hkv
    tq = _tile(S, 256, 8)
    tkb = _tile(S, 256, 8)

    def fwd_call(q, k, v):
        def body(q_ref, k_ref, v_ref, o_ref, lse_ref):
            s = _dot(_bf(q_ref[0]), _bf(k_ref[0]), NT) * scale
            m = jnp.max(s, axis=-1, keepdims=True)
            p = jnp.exp(s - m)
            l = jnp.sum(p, axis=-1, keepdims=True)
            o_ref[0] = _dot(_bf(p), _bf(v_ref[0]), NN) / l
            lse_ref[0, 0] = m + jnp.log(l)

        return pl.pallas_call(
            body, name=name + "_fwd", grid=(B, H, S // tq),
            in_specs=[pl.BlockSpec((1, tq, dk), lambda b, h, i: (b, i, h)),
                      pl.BlockSpec((1, S, dk), lambda b, h, i: (b, 0, h // rep)),
                      pl.BlockSpec((1, S, dv), lambda b, h, i: (b, 0, h // rep))],
            out_specs=[pl.BlockSpec((1, tq, dv), lambda b, h, i: (b, i, h)),
                       pl.BlockSpec((1, 1, tq, 1), lambda b, h, i: (b, h, i, 0))],
            out_shape=[jax.ShapeDtypeStruct((B, S, H * dv), F32),
                       jax.ShapeDtypeStruct((B, H, S, 1), F32)],
            compiler_params=_cp(),
        )(q, k, v)

    def dq_call(q, k, v, o, do, lse):
        def body(q_ref, k_ref, v_ref, o_ref, do_ref, lse_ref, dq_ref, delta_ref):
            kb = _bf(k_ref[0])
            s = _dot(_bf(q_ref[0]), kb, NT) * scale
            p = jnp.exp(s - lse_ref[0, 0])
            dov = do_ref[0]
            delta = jnp.sum(dov * o_ref[0], axis=-1, keepdims=True)
            dp = _dot(_bf(dov), _bf(v_ref[0]), NT)
            ds = p * (dp - delta)
            dq_ref[0] = _dot(_bf(ds), kb, NN) * scale
            delta_ref[0, 0] = delta

        qs = pl.BlockSpec((1, tq, dk), lambda b, h, i: (b, i, h))
        os_ = pl.BlockSpec((1, tq, dv), lambda b, h, i: (b, i, h))
        col = pl.BlockSpec((1, 1, tq, 1), lambda b, h, i: (b, h, i, 0))
        return pl.pallas_call(
            body, name=name + "_dq", grid=(B, H, S // tq),
            in_specs=[qs, pl.BlockSpec((1, S, dk), lambda b, h, i: (b, 0, h // rep)),
                      pl.BlockSpec((1, S, dv), lambda b, h, i: (b, 0, h // rep)), os_, os_, col],
            out_specs=[qs, col],
            out_shape=[jax.ShapeDtypeStruct(q.shape, F32), jax.ShapeDtypeStruct((B, H, S, 1), F32)],
            compiler_params=_cp(),
        )(q, k, v, o, do, lse)

    def dkv_call(q, k, v, do, lse, delta):
        def body(q_ref, k_ref, v_ref, do_ref, lse_ref, delta_ref, dk_ref, dv_ref):
            @pl.when(pl.program_id(3) == 0)
            def _():
                dk_ref[...] = jnp.zeros_like(dk_ref)
                dv_ref[...] = jnp.zeros_like(dv_ref)

            qb = _bf(q_ref[0])
            dob = _bf(do_ref[0])
            s = _dot(qb, _bf(k_ref[0]), NT) * scale
            p = jnp.exp(s - lse_ref[0, 0])
            dv_ref[0] += _dot(_bf(p), dob, TN)
            dp = _dot(dob, _bf(v_ref[0]), NT)
            ds = p * (dp - delta_ref[0, 0])
            dk_ref[0] += _dot(_bf(ds), qb, TN) * scale

        hq = lambda b, g, j, r: (b, 0, g * rep + r)
        colq = pl.BlockSpec((1, 1, S, 1), lambda b, g, j, r: (b, g * rep + r, 0, 0))
        ks = pl.BlockSpec((1, tkb, dk), lambda b, g, j, r: (b, j, g))
        vs = pl.BlockSpec((1, tkb, dv), lambda b, g, j, r: (b, j, g))
        return pl.pallas_call(
            body, name=name + "_dkv", grid=(B, Hkv, S // tkb, rep),
            in_specs=[pl.BlockSpec((1, S, dk), hq), ks, vs, pl.BlockSpec((1, S, dv), hq), colq, colq],
            out_specs=[ks, vs],
            out_shape=[jax.ShapeDtypeStruct(k.shape, F32), jax.ShapeDtypeStruct(v.shape, F32)],
            compiler_params=_cp(),
        )(q, k, v, do, lse, delta)

    @jax.custom_vjp
    def f(q, k, v):
        return fwd_call(q, k, v)[0]

    def fwd(q, k, v):
        o, lse = fwd_call(q, k, v)
        return o, (q, k, v, o, lse)

    def bwd(res, do):
        q, k, v, o, lse = res
        dq, delta = dq_call(q, k, v, o, do, lse)
        dk_, dv_ = dkv_call(q, k, v, do, lse, delta)
        return dq, dk_, dv_

    f.defvjp(fwd, bwd)
    return f(q, k, v)


def conv_silu(x, w, b, *, name):
    B, S, C = x.shape
    tc = _tile(C, 256)
    pad = SSD_K // 2
    xs = pl.BlockSpec((1, S, tc), lambda bi, j: (bi, 0, j))
    ws = pl.BlockSpec((SSD_K, tc), lambda bi, j: (0, j))
    bs = pl.BlockSpec((1, tc), lambda bi, j: (0, j))

    def shifted(v, off):
        if off == 0:
            return v
        t = lax.broadcasted_iota(jnp.int32, v.shape, 0)
        r = pltpu.roll(v, (-off) % S, 0)
        return jnp.where((t + off >= 0) & (t + off < S), r, 0.0)

    def pre_act(xv, wv, bv):
        u = jnp.zeros_like(xv) + bv
        for k in range(SSD_K):
            u = u + wv[k:k + 1, :] * shifted(xv, k - pad)
        return u

    def fwd_call(x, w, b):
        def body(x_ref, w_ref, b_ref, o_ref):
            u = pre_act(x_ref[0], w_ref[...], b_ref[...])
            o_ref[0] = u * _sigmoid(u)

        return pl.pallas_call(
            body, name=name + "_fwd", grid=(B, C // tc), in_specs=[xs, ws, bs], out_specs=xs,
            out_shape=jax.ShapeDtypeStruct((B, S, C), F32), compiler_params=_cp(),
        )(x, w, b)

    def bwd_call(x, w, b, dy):
        def body(x_ref, w_ref, b_ref, dy_ref, dx_ref, dw_ref):
            xv = x_ref[0]
            wv = w_ref[...]
            u = pre_act(xv, wv, b_ref[...])
            sg = _sigmoid(u)
            du = dy_ref[0] * (sg * (1.0 + u * (1.0 - sg)))
            dx = jnp.zeros_like(xv)
            for k in range(SSD_K):
                dx = dx + wv[k:k + 1, :] * shifted(du, pad - k)
                dw_ref[0, k:k + 1, :] = jnp.sum(du * shifted(xv, k - pad), axis=0, keepdims=True)
            dw_ref[0, SSD_K:SSD_K + 1, :] = jnp.sum(du, axis=0, keepdims=True)
            dw_ref[0, SSD_K + 1:8, :] = jnp.zeros((8 - SSD_K - 1, tc), F32)
            dx_ref[0] = dx

        return pl.pallas_call(
            body, name=name + "_bwd", grid=(B, C // tc), in_specs=[xs, ws, bs, xs],
            out_specs=[xs, pl.BlockSpec((1, 8, tc), lambda bi, j: (bi, 0, j))],
            out_shape=[jax.ShapeDtypeStruct((B, S, C), F32), jax.ShapeDtypeStruct((B, 8, C), F32)],
            compiler_params=_cp(),
        )(x, w, b, dy)

    @jax.custom_vjp
    def f(x, w, b):
        return fwd_call(x, w, b)

    def fwd(x, w, b):
        return fwd_call(x, w, b), (x, w, b)

    def bwd(res, dy):
        x, w, b = res
        dx, dwb = bwd_call(x, w, b, dy)
        dwb = jnp.sum(dwb, axis=0)
        return dx, dwb[:SSD_K], dwb[SSD_K:SSD_K + 1]

    f.defvjp(fwd, bwd)
    return f(x, w, b)


def _softplus(x):
    return jnp.maximum(x, 0.0) + jnp.log1p(jnp.exp(-jnp.abs(x)))


def _ssd_prep(raw, raw_t, brow, arow, bcol, acol, rev):
    li = lax.broadcasted_iota(jnp.int32, (CHUNK, CHUNK), 0)
    ki = lax.broadcasted_iota(jnp.int32, (CHUNK, CHUNK), 1)
    later = (li <= ki) if rev else (li >= ki)
    dt = _softplus(raw + brow)
    a = dt * arow
    cs = _dotf(later.astype(F32), a)
    tot = jnp.sum(a, axis=0, keepdims=True)
    a_t = _softplus(raw_t + bcol) * acol
    earlier = (li >= ki) if rev else (li <= ki)
    cs_t = _dotf(a_t, earlier.astype(F32))
    return dt, a, cs, tot, cs_t, later


def _lane_pick(mat, j):
    lane = lax.broadcasted_iota(jnp.int32, mat.shape, 1)
    return jnp.sum(jnp.where(lane == j, mat, 0.0), axis=1, keepdims=True)


def _head_sum(t, first):
    s0 = jnp.sum(jnp.where(first, t, 0.0), axis=1, keepdims=True)
    s1 = jnp.sum(jnp.where(first, 0.0, t), axis=1, keepdims=True)
    return s0, s1


def ssd_scan(xbc, raw, brow, arow, *, rev, name):
    B, S, _ = xbc.shape
    NC = S // CHUNK
    off = SSD_H if rev else 0
    n_dt = 2 * SSD_H

    def chunk_of(c):
        return (NC - 1 - c) if rev else c

    def specs(cmap):
        return dict(
            x=pl.BlockSpec((1, CHUNK, SSD_INNER), lambda b, c: (b, cmap(c), 0)),
            bm=pl.BlockSpec((1, CHUNK, 2 * SSD_N), lambda b, c: (b, cmap(c), SSD_INNER // (2 * SSD_N))),
            cm=pl.BlockSpec((1, CHUNK, 2 * SSD_N), lambda b, c: (b, cmap(c), SSD_INNER // (2 * SSD_N) + 1)),
            raw=pl.BlockSpec((1, CHUNK, LANES), lambda b, c: (b, cmap(c), 0)),
            raw_t=pl.BlockSpec((1, n_dt, CHUNK), lambda b, c: (b, 0, cmap(c))),
            row=pl.BlockSpec((1, LANES), lambda b, c: (0, 0)),
            colv=pl.BlockSpec((n_dt, 1), lambda b, c: (0, 0)),
            hs=pl.BlockSpec((1, 1, N_PAIR, SSD_N, LANES), lambda b, c: (b, cmap(c), 0, 0, 0)),
        )

    def head_terms(prep, j, first_dummy=None):
        dt, a, cs, tot, cs_t, later = prep
        cs_c = _lane_pick(cs, j)
        cs_r = cs_t[j:j + 1, :]
        dt_c = _lane_pick(dt, j)
        tot_j = _lane_pick(tot, j)
        L = jnp.exp(jnp.where(later, cs_c - cs_r, -1e30))
        return cs_c, cs_r, dt_c, tot_j, L

    def fwd_call(xbc, raw, raw_t, brow, arow, bcol, acol):
        def body(x_ref, bm_ref, cm_ref, raw_ref, rawt_ref, brow_ref, arow_ref, bcol_ref, acol_ref,
                 y_ref, hs_ref, st_ref):
            @pl.when(pl.program_id(1) == 0)
            def _():
                st_ref[...] = jnp.zeros_like(st_ref)

            prep = _ssd_prep(raw_ref[0], rawt_ref[0], brow_ref[...], arow_ref[...],
                             bcol_ref[...], acol_ref[...], rev)
            lane = lax.broadcasted_iota(jnp.int32, (CHUNK, LANES), 1)
            first = lane < SSD_P
            for g in range(SSD_G):
                Bg = _bf(bm_ref[0, :, g * SSD_N:(g + 1) * SSD_N])
                Cg = _bf(cm_ref[0, :, g * SSD_N:(g + 1) * SSD_N])
                G = _dot(Cg, Bg, NT)
                for pp in range(N_PAIR // SSD_G):
                    pi = g * (N_PAIR // SSD_G) + pp
                    c0, _, d0, t0, L0 = head_terms(prep, off + 2 * pi)
                    c1, _, d1, t1, L1 = head_terms(prep, off + 2 * pi + 1)
                    xd = x_ref[0, :, pi * LANES:(pi + 1) * LANES] * jnp.where(first, d0, d1)
                    xdb = _bf(xd)
                    y = jnp.where(first, _dot(_bf(G * L0), xdb, NN), _dot(_bf(G * L1), xdb, NN))
                    dec = jnp.where(first, jnp.exp(t0 - c0), jnp.exp(t1 - c1))
                    h_prev = st_ref[pi]
                    hs_ref[0, 0, pi] = h_prev
                    y = y + _dot(Cg, _bf(h_prev), NN) * jnp.where(first, jnp.exp(c0), jnp.exp(c1))
                    y_ref[0, :, pi * LANES:(pi + 1) * LANES] = y
                    etot = jnp.where(first[:1], jnp.exp(t0), jnp.exp(t1))
                    st_ref[pi] = h_prev * etot + _dot(Bg, _bf(xd * dec), TN)

        sp = specs(chunk_of)
        return pl.pallas_call(
            body, name=name + "_fwd", grid=(B, NC),
            in_specs=[sp["x"], sp["bm"], sp["cm"], sp["raw"], sp["raw_t"], sp["row"], sp["row"],
                      sp["colv"], sp["colv"]],
            out_specs=[sp["x"], sp["hs"]],
            out_shape=[jax.ShapeDtypeStruct((B, S, SSD_INNER), F32),
                       jax.ShapeDtypeStruct((B, NC, N_PAIR, SSD_N, LANES), F32)],
            scratch_shapes=[pltpu.VMEM((N_PAIR, SSD_N, LANES), F32)],
            compiler_params=_cp(),
        )(xbc, xbc, xbc, raw, raw_t, brow, arow, bcol, acol)

    def bwd_call(xbc, raw, raw_t, brow, arow, bcol, acol, hs, dy):
        def body(x_ref, bm_ref, cm_ref, raw_ref, rawt_ref, brow_ref, arow_ref, bcol_ref, acol_ref,
                 hs_ref, dy_ref, dxbc_ref, draw_ref, da_ref, dst_ref):
            @pl.when(pl.program_id(1) == 0)
            def _():
                dst_ref[...] = jnp.zeros_like(dst_ref)
                da_ref[...] = jnp.zeros_like(da_ref)

            raw_v = raw_ref[0]
            prep = _ssd_prep(raw_v, rawt_ref[0], brow_ref[...], arow_ref[...],
                             bcol_ref[...], acol_ref[...], rev)
            dt, a, cs, tot, cs_t, later = prep
            li = lax.broadcasted_iota(jnp.int32, (CHUNK, CHUNK), 0)
            ki = lax.broadcasted_iota(jnp.int32, (CHUNK, CHUNK), 1)
            later_t = (li >= ki) if rev else (li <= ki)
            lane = lax.broadcasted_iota(jnp.int32, (CHUNK, LANES), 1)
            first = lane < SSD_P
            dcs_all = jnp.zeros((CHUNK, LANES), F32)
            ddt_all = jnp.zeros((CHUNK, LANES), F32)
            dtot_all = jnp.zeros((1, LANES), F32)
            for g in range(SSD_G):
                Bg = _bf(bm_ref[0, :, g * SSD_N:(g + 1) * SSD_N])
                Cg = _bf(cm_ref[0, :, g * SSD_N:(g + 1) * SSD_N])
                G = _dot(Cg, Bg, NT)
                Gt = _dot(Bg, Cg, NT)
                dG = jnp.zeros((CHUNK, CHUNK), F32)
                dB = jnp.zeros((CHUNK, SSD_N), F32)
                dC = jnp.zeros((CHUNK, SSD_N), F32)
                for pp in range(N_PAIR // SSD_G):
                    pi = g * (N_PAIR // SSD_G) + pp
                    j0, j1 = off + 2 * pi, off + 2 * pi + 1
                    c0, r0, d0, t0, L0 = head_terms(prep, j0)
                    c1, r1, d1, t1, L1 = head_terms(prep, j1)
                    Lt0 = jnp.exp(jnp.where(later_t, r0 - c0, -1e30))
                    Lt1 = jnp.exp(jnp.where(later_t, r1 - c1, -1e30))
                    xv = x_ref[0, :, pi * LANES:(pi + 1) * LANES]
                    dtp = jnp.where(first, d0, d1)
                    xd = xv * dtp
                    xdb = _bf(xd)
                    dyv = dy_ref[0, :, pi * LANES:(pi + 1) * LANES]
                    dyb = _bf(dyv)
                    dec = jnp.where(first, jnp.exp(t0 - c0), jnp.exp(t1 - c1))
                    ecs = jnp.where(first, jnp.exp(c0), jnp.exp(c1))
                    et0, et1 = jnp.exp(t0), jnp.exp(t1)
                    etot = jnp.where(first[:1], et0, et1)
                    h_prev = hs_ref[0, 0, pi]
                    hpb = _bf(h_prev)
                    dhn = dst_ref[pi]
                    dhb = _bf(dhn)
                    W0, W1 = G * L0, G * L1
                    Wt0, Wt1 = Gt * Lt0, Gt * Lt1
                    bdh = _dot(Bg, dhb, NN)
                    dxd = jnp.where(first, _dot(_bf(Wt0), dyb, NN), _dot(_bf(Wt1), dyb, NN)) + bdh * dec
                    dy0 = _bf(jnp.where(first, dyv, 0.0))
                    dy1 = _bf(jnp.where(first, 0.0, dyv))
                    Q0, Q1 = _dot(dy0, xdb, NT), _dot(dy1, xdb, NT)
                    Qt0, Qt1 = _dot(xdb, dy0, NT), _dot(xdb, dy1, NT)
                    dG = dG + Q0 * L0 + Q1 * L1
                    dcs0 = (jnp.sum(Q0 * W0, axis=1, keepdims=True)
                            - jnp.sum(Qt0 * Wt0, axis=1, keepdims=True))
                    dcs1 = (jnp.sum(Q1 * W1, axis=1, keepdims=True)
                            - jnp.sum(Qt1 * Wt1, axis=1, keepdims=True))
                    dye = dyv * ecs
                    dyeb = _bf(dye)
                    s0, s1 = _head_sum(dye * _dot(Cg, hpb, NN), first)
                    dcs0, dcs1 = dcs0 + s0, dcs1 + s1
                    dC = dC + _dot(dyeb, hpb, NT)
                    dB = dB + _dot(_bf(xd * dec), dhb, NT)
                    u0, u1 = _head_sum(xd * bdh * dec, first)
                    dcs0, dcs1 = dcs0 - u0, dcs1 - u1
                    w = jnp.sum(dhn * h_prev, axis=0, keepdims=True)
                    w0, w1 = _head_sum(w, first[:1])
                    dt0 = jnp.sum(u0, axis=0, keepdims=True) + et0 * w0
                    dt1 = jnp.sum(u1, axis=0, keepdims=True) + et1 * w1
                    dst_ref[pi] = _dot(Cg, dyeb, TN) + dhn * etot
                    q0, q1 = _head_sum(dxd * xv, first)
                    dxbc_ref[0, :, pi * LANES:(pi + 1) * LANES] = dxd * dtp
                    dcs_all = dcs_all + jnp.where(lane == j0, dcs0, 0.0) + jnp.where(lane == j1, dcs1, 0.0)
                    ddt_all = ddt_all + jnp.where(lane == j0, q0, 0.0) + jnp.where(lane == j1, q1, 0.0)
                    dtot_all = (dtot_all + jnp.where(lane[:1] == j0, dt0, 0.0)
                                + jnp.where(lane[:1] == j1, dt1, 0.0))
                dGb = _bf(dG)
                dC = dC + _dot(dGb, Bg, NN)
                dB = dB + _dot(dGb, Cg, TN)
                dxbc_ref[0, :, SSD_INNER + g * SSD_N:SSD_INNER + (g + 1) * SSD_N] = dB
                dxbc_ref[0, :, SSD_INNER + (SSD_G + g) * SSD_N:SSD_INNER + (SSD_G + g + 1) * SSD_N] = dC
            da = _dotf(later_t.astype(F32), dcs_all) + dtot_all
            ddt = ddt_all + da * arow_ref[...]
            da_ref[0] += jnp.sum(da * dt, axis=0, keepdims=True)
            draw_ref[0] = ddt * _sigmoid(raw_v + brow_ref[...])

        def rchunk(c):
            return c if rev else (NC - 1 - c)

        sp = specs(rchunk)
        full = pl.BlockSpec((1, CHUNK, SSD_CONV_DIM), lambda b, c: (b, rchunk(c), 0))
        return pl.pallas_call(
            body, name=name + "_bwd", grid=(B, NC),
            in_specs=[sp["x"], sp["bm"], sp["cm"], sp["raw"], sp["raw_t"], sp["row"], sp["row"],
                      sp["colv"], sp["colv"], sp["hs"], sp["x"]],
            out_specs=[full, sp["raw"], pl.BlockSpec((1, 1, LANES), lambda b, c: (b, 0, 0))],
            out_shape=[jax.ShapeDtypeStruct((B, S, SSD_CONV_DIM), F32),
                       jax.ShapeDtypeStruct((B, S, LANES), F32),
                       jax.ShapeDtypeStruct((B, 1, LANES), F32)],
            scratch_shapes=[pltpu.VMEM((N_PAIR, SSD_N, LANES), F32)],
            compiler_params=_cp(),
        )(xbc, xbc, xbc, raw, raw_t, brow, arow, bcol, acol, hs, dy)

    def aux(raw, brow, arow):
        raw_t = jnp.swapaxes(raw[:, :, :n_dt], 1, 2)
        return raw_t, brow[0, :n_dt][:, None], arow[0, :n_dt][:, None]

    @jax.custom_vjp
    def f(xbc, raw, brow, arow):
        raw_t, bcol, acol = aux(raw, brow, arow)
        return fwd_call(xbc, raw, raw_t, brow, arow, bcol, acol)[0]

    def fwd(xbc, raw, brow, arow):
        raw_t, bcol, acol = aux(raw, brow, arow)
        y, hs = fwd_call(xbc, raw, raw_t, brow, arow, bcol, acol)
        return y, (xbc, raw, brow, arow, hs)

    def bwd(res, dy):
        xbc, raw, brow, arow, hs = res
        raw_t, bcol, acol = aux(raw, brow, arow)
        dxbc, draw, da = bwd_call(xbc, raw, raw_t, brow, arow, bcol, acol, hs, dy)
        dbrow = jnp.sum(draw, axis=(0, 1))[None, :]
        return dxbc, draw, dbrow, jnp.sum(da, axis=0)

    f.defvjp(fwd, bwd)
    return f(xbc, raw, brow, arow)


def ssd_out(yf, yb, xbc, z, dsk, g, *, name):
    T, W = yf.shape
    gs = W // SSD_G
    tr = _tile(T, 512, 8)
    row = pl.BlockSpec((tr, W), lambda i: (i, 0))
    vec = pl.BlockSpec((1, W), lambda i: (0, 0))

    def normed(yv, gv):
        outs, rs = [], []
        for i in range(SSD_G):
            sl = slice(i * gs, (i + 1) * gs)
            r = lax.rsqrt(jnp.mean(yv[:, sl] * yv[:, sl], axis=-1, keepdims=True) + EPS)
            rs.append(r)
            outs.append(yv[:, sl] * r)
        return outs, rs

    def fwd_call(yf, yb, xbc, z, dsk, g):
        def body(yf_ref, yb_ref, xs_ref, z_ref, dsk_ref, g_ref, o_ref):
            zv = z_ref[...]
            yv = (yf_ref[...] + yb_ref[...] + xs_ref[...] * dsk_ref[...]) * (zv * _sigmoid(zv))
            outs, _ = normed(yv, g_ref[...])
            for i in range(SSD_G):
                sl = slice(i * gs, (i + 1) * gs)
                o_ref[:, sl] = (outs[i] * g_ref[:, sl]).astype(o_ref.dtype)

        return pl.pallas_call(
            body, name=name + "_fwd", grid=(T // tr,), in_specs=[row, row, row, row, vec, vec],
            out_specs=row, out_shape=jax.ShapeDtypeStruct((T, W), BF16), compiler_params=_cp(),
        )(yf, yb, xbc, z, dsk, g)

    def bwd_call(yf, yb, xbc, z, dsk, g, do):
        def body(yf_ref, yb_ref, xs_ref, z_ref, dsk_ref, g_ref, do_ref, dy_ref, dxs_ref, dz_ref,
                 ddsk_ref, dg_ref):
            @pl.when(pl.program_id(0) == 0)
            def _():
                ddsk_ref[...] = jnp.zeros_like(ddsk_ref)
                dg_ref[...] = jnp.zeros_like(dg_ref)

            zv = z_ref[...]
            sg = _sigmoid(zv)
            sz = zv * sg
            xs = xs_ref[...]
            pre = yf_ref[...] + yb_ref[...] + xs * dsk_ref[...]
            yv = pre * sz
            outs, rs = normed(yv, g_ref[...])
            for i in range(SSD_G):
                sl = slice(i * gs, (i + 1) * gs)
                dov = do_ref[:, sl].astype(F32)
                xn = outs[i]
                dxn = dov * g_ref[:, sl]
                dyv = rs[i] * (dxn - xn * jnp.mean(dxn * xn, axis=-1, keepdims=True))
                dg_ref[:, sl] += jnp.sum(dov * xn, axis=0, keepdims=True)
                dpre = dyv * sz[:, sl]
                dy_ref[:, sl] = dpre
                dxs_ref[:, sl] = dpre * dsk_ref[:, sl]
                ddsk_ref[:, sl] += jnp.sum(dpre * xs[:, sl], axis=0, keepdims=True)
                dz_ref[:, sl] = dyv * pre[:, sl] * (sg[:, sl] * (1.0 + zv[:, sl] * (1.0 - sg[:, sl])))

        o = jax.ShapeDtypeStruct((T, W), F32)
        v = jax.ShapeDtypeStruct((1, W), F32)
        return pl.pallas_call(
            body, name=name + "_bwd", grid=(T // tr,), in_specs=[row, row, row, row, vec, vec, row],
            out_specs=[row, row, row, vec, vec], out_shape=[o, o, o, v, v], compiler_params=_cp(),
        )(yf, yb, xbc, z, dsk, g, do)

    @jax.custom_vjp
    def f(yf, yb, xbc, z, dsk, g):
        return fwd_call(yf, yb, xbc, z, dsk, g)

    def fwd(yf, yb, xbc, z, dsk, g):
        return fwd_call(yf, yb, xbc, z, dsk, g), (yf, yb, xbc, z, dsk, g)

    def bwd(res, do):
        dy, dxs, dz, ddsk, dg = bwd_call(*res, do)
        dxbc = jnp.pad(dxs, ((0, 0), (0, res[2].shape[1] - W)))
        return dy, dy, dxbc, dz, ddsk, dg

    f.defvjp(fwd, bwd)
    return f(yf, yb, xbc, z, dsk, g)


def swiglu(gu, *, name):
    T, F2 = gu.shape
    Fh = F2 // 2
    tr, tf = _tile(T, 512, 8), _tile(Fh, 512)
    nf = Fh // tf
    gs = pl.BlockSpec((tr, tf), lambda i, j: (i, j))
    us = pl.BlockSpec((tr, tf), lambda i, j: (i, j + nf))

    def fwd_call(gu):
        def body(g_ref, u_ref, o_ref):
            gv = g_ref[...].astype(F32)
            o_ref[...] = (gv * _sigmoid(gv) * u_ref[...].astype(F32)).astype(o_ref.dtype)

        return pl.pallas_call(
            body, name=name + "_fwd", grid=(T // tr, nf), in_specs=[gs, us], out_specs=gs,
            out_shape=jax.ShapeDtypeStruct((T, Fh), BF16), compiler_params=_cp(),
        )(gu, gu)

    def bwd_call(gu, da):
        def body(g_ref, u_ref, da_ref, dgu_ref):
            j = pl.program_id(1)
            gv = g_ref[...].astype(F32)
            uv = u_ref[...].astype(F32)
            dav = da_ref[...].astype(F32)
            sg = _sigmoid(gv)
            d_gate = dav * uv * (sg * (1.0 + gv * (1.0 - sg)))
            d_up = dav * gv * sg
            dgu_ref[...] = jnp.where(j < nf, d_gate, d_up).astype(dgu_ref.dtype)

        gsel = pl.BlockSpec((tr, tf), lambda i, j: (i, j % nf))
        usel = pl.BlockSpec((tr, tf), lambda i, j: (i, j % nf + nf))
        return pl.pallas_call(
            body, name=name + "_bwd", grid=(T // tr, 2 * nf), in_specs=[gsel, usel, gsel],
            out_specs=pl.BlockSpec((tr, tf), lambda i, j: (i, j)),
            out_shape=jax.ShapeDtypeStruct((T, F2), BF16), compiler_params=_cp(),
        )(gu, gu, da)

    @jax.custom_vjp
    def f(gu):
        return fwd_call(gu)

    def fwd(gu):
        return fwd_call(gu), gu

    def bwd(gu, da):
        return (bwd_call(gu, da),)

    f.defvjp(fwd, bwd)
    return f(gu)


def gated_residual(x, gate, y, *, name):
    B, S, D = x.shape
    ts = _tile(S, 256, 8)
    row = pl.BlockSpec((1, ts, D), lambda b, j: (b, j, 0))
    per_b = pl.BlockSpec((1, 1, D), lambda b, j: (b, 0, 0))

    def fwd_call(x, gate, y):
        def body(x_ref, gt_ref, y_ref, o_ref):
            o_ref[0] = x_ref[0] + gt_ref[0] * y_ref[0]

        return pl.pallas_call(
            body, name=name + "_fwd", grid=(B, S // ts), in_specs=[row, per_b, row], out_specs=row,
            out_shape=jax.ShapeDtypeStruct((B, S, D), F32), compiler_params=_cp(),
        )(x, gate, y)

    def bwd_call(gate, y, g):
        def body(gt_ref, y_ref, g_ref, dy_ref, dgt_ref):
            @pl.when(pl.program_id(1) == 0)
            def _():
                dgt_ref[...] = jnp.zeros_like(dgt_ref)

            gv = g_ref[0]
            dy_ref[0] = gt_ref[0] * gv
            dgt_ref[0] += jnp.sum(gv * y_ref[0], axis=0, keepdims=True)

        return pl.pallas_call(
            body, name=name + "_bwd", grid=(B, S // ts), in_specs=[per_b, row, row],
            out_specs=[row, per_b],
            out_shape=[jax.ShapeDtypeStruct((B, S, D), F32), jax.ShapeDtypeStruct((B, 1, D), F32)],
            compiler_params=_cp(),
        )(gate, y, g)

    @jax.custom_vjp
    def f(x, gate, y):
        return fwd_call(x, gate, y)

    def fwd(x, gate, y):
        return fwd_call(x, gate, y), (gate, y)

    def bwd(res, g):
        dy, dgate = bwd_call(res[0], res[1], g)
        return g, dgate, dy

    f.defvjp(fwd, bwd)
    return f(x, gate, y)


def final_loss(x, g, target, *, name):
    T, D = x.shape
    tr = _tile(T, 256, 8)
    row = pl.BlockSpec((tr, D), lambda i: (i, 0))
    vec = pl.BlockSpec((1, D), lambda i: (0, 0))

    def fwd_call(x, g, target):
        def body(x_ref, g_ref, t_ref, o_ref):
            @pl.when(pl.program_id(0) == 0)
            def _():
                o_ref[...] = jnp.zeros_like(o_ref)

            xv = x_ref[...]
            r = lax.rsqrt(jnp.mean(xv * xv, axis=-1, keepdims=True) + EPS)
            e = xv * r * g_ref[...] - t_ref[...]
            o_ref[...] += jnp.sum(e * e, axis=0, keepdims=True)

        part = pl.pallas_call(
            body, name=name + "_fwd", grid=(T // tr,), in_specs=[row, vec, row], out_specs=vec,
            out_shape=jax.ShapeDtypeStruct((1, D), F32), compiler_params=_cp(),
        )(x, g, target)
        return (0.5 / D) * jnp.sum(part)

    def bwd_call(x, g, target, ct):
        def body(x_ref, g_ref, t_ref, ct_ref, dx_ref, dg_ref):
            @pl.when(pl.program_id(0) == 0)
            def _():
                dg_ref[...] = jnp.zeros_like(dg_ref)

            xv = x_ref[...]
            gv = g_ref[...]
            r = lax.rsqrt(jnp.mean(xv * xv, axis=-1, keepdims=True) + EPS)
            xn = xv * r
            dy = (xn * gv - t_ref[...]) * (ct_ref[...] * (1.0 / D))
            dxn = dy * gv
            dx_ref[...] = r * (dxn - xn * jnp.mean(dxn * xn, axis=-1, keepdims=True))
            dg_ref[...] += jnp.sum(dy * xn, axis=0, keepdims=True)

        return pl.pallas_call(
            body, name=name + "_bwd", grid=(T // tr,),
            in_specs=[row, vec, row, pl.BlockSpec((1, 1), lambda i: (0, 0))], out_specs=[row, vec],
            out_shape=[jax.ShapeDtypeStruct((T, D), F32), jax.ShapeDtypeStruct((1, D), F32)],
            compiler_params=_cp(),
        )(x, g, target, ct)

    @jax.custom_vjp
    def f(x, g, target):
        return fwd_call(x, g, target)

    def fwd(x, g, target):
        return fwd_call(x, g, target), (x, g, target)

    def bwd(res, ct):
        x, g, target = res
        dx, dg = bwd_call(x, g, target, jnp.reshape(ct, (1, 1)).astype(F32))
        return dx, dg, jnp.zeros_like(target)

    f.defvjp(fwd, bwd)
    return f(x, g, target)


def adamw(w, g, m, v, *, name):
    R, C = w.shape
    tr = _tile(R, 512, 8)
    spec = pl.BlockSpec((tr, C), lambda i: (i, 0))
    c1 = 1.0 / (1.0 - ADAM_B1 ** ADAM_STEP)
    c2 = 1.0 / (1.0 - ADAM_B2 ** ADAM_STEP)

    def body(w_ref, g_ref, m_ref, v_ref, d_ref, nm_ref, nv_ref):
        gv = g_ref[...]
        nm = ADAM_B1 * m_ref[...] + (1.0 - ADAM_B1) * gv
        nv = ADAM_B2 * v_ref[...] + (1.0 - ADAM_B2) * (gv * gv)
        nm_ref[...] = nm
        nv_ref[...] = nv
        d_ref[...] = -ADAM_LR * ((nm * c1) / (jnp.sqrt(nv * c2) + ADAM_EPS) + ADAM_WD * w_ref[...])

    o = jax.ShapeDtypeStruct((R, C), F32)
    return pl.pallas_call(
        body, name=name, grid=(R // tr,), in_specs=[spec] * 4, out_specs=[spec] * 3,
        out_shape=[o, o, o], compiler_params=_cp(),
    )(w, g, m, v)


def _position():
    x, y, c = lax.axis_index("x"), lax.axis_index("y"), lax.axis_index("c")
    return x, y, c


def all_gather(shard, *, name):
    R, C = shard.shape

    def body(x_ref, out_ref, send_sems, recv_sems, local_sem):
        x, y, c = _position()
        me, sibling = (x, y, c), (x, y, 1 - c)
        chips = [(1 - x, y), (x, 1 - y), (1 - x, 1 - y)]

        def block(px, py, pc):
            return out_ref.at[4 * px + 2 * py + pc]

        def copy(k, blk, to, src=None):
            return pltpu.make_async_remote_copy(
                src_ref=block(*blk) if src is None else src, dst_ref=block(*blk),
                send_sem=send_sems.at[k], recv_sem=recv_sems.at[k],
                device_id=to, device_id_type=pl.DeviceIdType.MESH)

        mine = pltpu.make_async_copy(x_ref, block(*me), local_sem)
        mine.start()
        first = [copy(0, me, sibling, src=x_ref)]
        first += [copy(1 + j, me, (*chip, c), src=x_ref) for j, chip in enumerate(chips)]
        for cp in first:
            cp.start()
        passed = [copy(4 + j, (*chip, c), sibling) for j, chip in enumerate(chips)]
        for j, chip in enumerate(chips):
            copy(1 + j, (*chip, c), me).wait_recv()
            passed[j].start()
        copy(0, sibling, me).wait_recv()
        for j, chip in enumerate(chips):
            copy(4 + j, (*chip, 1 - c), me).wait_recv()
        for cp in first + passed:
            cp.wait_send()
        mine.wait()

    return pl.pallas_call(
        body, name=name, out_shape=jax.ShapeDtypeStruct((N_DEV, R, C), shard.dtype),
        in_specs=[pl.BlockSpec(memory_space=pl.ANY)], out_specs=pl.BlockSpec(memory_space=pl.ANY),
        scratch_shapes=[pltpu.SemaphoreType.DMA((7,)), pltpu.SemaphoreType.DMA((7,)),
                        pltpu.SemaphoreType.DMA],
    )(shard)


def exchange(stack, *, name):
    _, R, C = stack.shape

    def body(x_ref, out_ref, send_sems, recv_sems, local_sem):
        x, y, c = _position()
        me = 4 * x + 2 * y + c
        mine = pltpu.make_async_copy(x_ref.at[me], out_ref.at[me], local_sem)
        mine.start()
        copies = []
        for r in range(1, N_DEV):
            px = (1 - x) if r & 4 else x
            py = (1 - y) if r & 2 else y
            pc = (1 - c) if r & 1 else c
            peer = 4 * px + 2 * py + pc
            copies.append(pltpu.make_async_remote_copy(
                src_ref=x_ref.at[peer], dst_ref=out_ref.at[me],
                send_sem=send_sems.at[r - 1], recv_sem=recv_sems.at[r - 1],
                device_id=(px, py, pc), device_id_type=pl.DeviceIdType.MESH))
        for cp in copies:
            cp.start()
        for cp in copies:
            cp.wait_recv()
        for cp in copies:
            cp.wait_send()
        mine.wait()

    return pl.pallas_call(
        body, name=name, out_shape=jax.ShapeDtypeStruct(stack.shape, stack.dtype),
        in_specs=[pl.BlockSpec(memory_space=pl.ANY)], out_specs=pl.BlockSpec(memory_space=pl.ANY),
        scratch_shapes=[pltpu.SemaphoreType.DMA((7,)), pltpu.SemaphoreType.DMA((7,)),
                        pltpu.SemaphoreType.DMA],
    )(stack)


def sum_blocks(stack, *, name):
    n, R, C = stack.shape
    tr = _tile(R, 256, 8)

    def body(x_ref, o_ref):
        acc = x_ref[0].astype(F32)
        for i in range(1, n):
            acc = acc + x_ref[i].astype(F32)
        o_ref[...] = acc

    return pl.pallas_call(
        body, name=name, grid=(R // tr,),
        in_specs=[pl.BlockSpec((n, tr, C), lambda i: (0, i, 0))],
        out_specs=pl.BlockSpec((tr, C), lambda i: (i, 0)),
        out_shape=jax.ShapeDtypeStruct((R, C), F32), compiler_params=_cp(),
    )(stack)


PACK_COLS = 1024
PACK_ROW_MULT = 256


def _pack(arrays, dtype):
    flat = jnp.concatenate([a.reshape(-1).astype(dtype) for a in arrays])
    n = flat.shape[0]
    unit = PACK_COLS * PACK_ROW_MULT
    padded = -(-n // unit) * unit
    return jnp.pad(flat, (0, padded - n)).reshape(padded // PACK_COLS, PACK_COLS)


def _unpack(packed, shapes, lead=()):
    flat = packed.reshape(*lead, -1)
    out, o = [], 0
    for s in shapes:
        n = int(np.prod(s))
        out.append(flat[..., o:o + n].reshape(*lead, *s))
        o += n
    return out


def fsdp_gather(packed_shard, *, name):
    @jax.custom_vjp
    def f(p):
        return all_gather(p.astype(BF16), name=name + "_ag")

    def fwd(p):
        return f(p), None

    def bwd(_, g):
        return (sum_blocks(exchange(g, name=name + "_a2a"), name=name + "_sum"),)

    f.defvjp(fwd, bwd)
    return f(packed_shard)


def gather_rows(part, me, *, name):
    rows, n = part.shape
    per = rows // N_DEV

    @jax.custom_vjp
    def f(part):
        full = all_gather(part, name=name + "_fwd")
        mine = lax.dynamic_slice_in_dim(full, me * per, per, axis=1)
        return jnp.swapaxes(mine, 0, 1).reshape(per, N_DEV * n)

    def fwd(part):
        return f(part), None

    def bwd(_, g):
        full = all_gather(g, name=name + "_bwd")
        mine = lax.dynamic_slice_in_dim(full, me * n, n, axis=2)
        return (mine.reshape(rows, n),)

    f.defvjp(fwd, bwd)
    return f(part)


def _seg_layout():
    offs = np.concatenate([[0], np.cumsum(IN_SPLITS)])
    cols, widths = [], []
    for s in SEG_ORDER:
        cols.append((int(offs[s]), int(offs[s + 1])))
        widths.append(SEG_PAD.get(s, IN_SPLITS[s]))
    return cols, widths


def _arrange_w_in(w):
    cols, widths = _seg_layout()
    parts = []
    for (a, b), wd in zip(cols, widths):
        seg = w[:, a:b]
        if wd != b - a:
            seg = jnp.pad(seg, ((0, 0), (0, wd - (b - a))))
        parts.append(seg)
    return jnp.concatenate(parts, axis=1)


def split_cols(proj, widths):
    @jax.custom_vjp
    def f(p):
        outs, o = [], 0
        for wd in widths:
            outs.append(p[:, o:o + wd])
            o += wd
        return tuple(outs)

    def fwd(p):
        return f(p), None

    def bwd(_, gs):
        return (jnp.concatenate(gs, axis=1),)

    f.defvjp(fwd, bwd)
    return f(proj)


BIG = ("w_in", "w_uq", "w_ukv", "conv_w", "w_out", "w_gate_up", "w_down")
SMALL = ("b_ada", "norm1_g", "norm2_g", "q_norm_g", "k_norm_g", "mla_q_norm_g", "mla_kv_norm_g",
         "conv_b", "dt_bias", "a_log", "d_skip", "ssd_norm_g", "final_norm_g")
WEIGHTS = ("w_ada", "b_ada", "norm1_g", "norm2_g", "w_in", "q_norm_g", "k_norm_g", "mla_q_norm_g",
           "w_uq", "mla_kv_norm_g", "w_ukv", "conv_w", "conv_b", "dt_bias", "a_log", "d_skip",
           "ssd_norm_g", "w_out", "w_gate_up", "w_down", "final_norm_g")


def _full_weights(gathered, shard_shapes, L):
    parts = dict(zip(BIG, _unpack(gathered, [shard_shapes[n] for n in BIG], lead=(N_DEV,))))

    def cols(a):
        return jnp.moveaxis(a, 0, 2).reshape(L, a.shape[2], N_DEV * a.shape[3])

    def rows(a):
        return jnp.swapaxes(a, 0, 1).reshape(L, N_DEV * a.shape[2], a.shape[3])

    full = {n: cols(parts[n]) for n in ("w_in", "w_uq", "w_ukv", "conv_w", "w_gate_up")}
    full.update({n: rows(parts[n]) for n in ("w_out", "w_down")})
    return full


def _layer(l, x, mod, W, P, tabs):
    B, S, D = x.shape
    T = B * S
    nm = f"l{l}_"
    shift1, scale1, gate1, shift2, scale2, gate2 = [m[:, None, :] for m in jnp.split(mod, 6, axis=-1)]
    cos_a, sin_a, cos_b, sin_b, cos_k, sin_k = tabs

    h = rmsmod(x, P["norm1_g"][l][None], scale1, shift1, name=nm + "norm1")
    w_in = _arrange_w_in(W["w_in"][l])
    proj = linear(h.reshape(T, D), w_in, out_dtype=F32, name=nm + "in")
    q_a, k_a, v_a, cq, ckv, z, xbc, kpe, dtr = split_cols(proj, _seg_layout()[1])

    qn = group_rmsnorm(q_a, jnp.tile(P["q_norm_g"][l], GQA_H)[None], gs=HEAD, out_dtype=F32, name=nm + "qnorm")
    kn = group_rmsnorm(k_a, jnp.tile(P["k_norm_g"][l], GQA_KV)[None], gs=HEAD, out_dtype=F32, name=nm + "knorm")
    qr = rope(qn.reshape(B, S, -1), cos_a[:, :GQA_H * HEAD], sin_a[:, :GQA_H * HEAD], d=HEAD // 4, name=nm + "qrope")
    kr = rope(kn.reshape(B, S, -1), cos_a[:, :GQA_KV * HEAD], sin_a[:, :GQA_KV * HEAD], d=HEAD // 4, name=nm + "krope")
    o_a = attention(qr, kr, v_a.reshape(B, S, -1), H=GQA_H, Hkv=GQA_KV, dk=HEAD, dv=HEAD,
                    scale=HEAD ** -0.5, name=nm + "gqa")

    w_uq = W["w_uq"][l].reshape(MLA_QL, MLA_H, MLA_NOPE + MLA_ROPE)
    w_uq = jnp.concatenate([w_uq[:, :, :MLA_NOPE].reshape(MLA_QL, -1), w_uq[:, :, MLA_NOPE:].reshape(MLA_QL, -1)], axis=1)
    w_ukv = W["w_ukv"][l].reshape(MLA_KVL, MLA_H, MLA_NOPE + MLA_V)
    w_ukv = jnp.concatenate([w_ukv[:, :, :MLA_NOPE].reshape(MLA_KVL, -1), w_ukv[:, :, MLA_NOPE:].reshape(MLA_KVL, -1)], axis=1)
    cqn = group_rmsnorm(cq, P["mla_q_norm_g"][l][None], gs=MLA_QL, out_dtype=BF16, name=nm + "cqnorm")
    ckvn = group_rmsnorm(ckv, P["mla_kv_norm_g"][l][None], gs=MLA_KVL, out_dtype=BF16, name=nm + "ckvnorm")
    qb = linear(cqn, w_uq, out_dtype=F32, name=nm + "uq")
    kvb = linear(ckvn, w_ukv, out_dtype=F32, name=nm + "ukv")
    q_nope, q_pe = split_cols(qb, (MLA_H * MLA_NOPE, MLA_H * MLA_ROPE))
    k_nope, v_b = split_cols(kvb, (MLA_H * MLA_NOPE, MLA_H * MLA_V))
    q_pe = rope(q_pe.reshape(B, S, -1), cos_b, sin_b, d=MLA_ROPE // 4, name=nm + "qpe_rope")
    k_pe = rope(kpe.reshape(B, S, -1), cos_k, sin_k, d=MLA_ROPE // 4, name=nm + "kpe_rope")[:, :, :MLA_ROPE]
    zpad = jnp.zeros((B, S, MLA_H, MLA_DK - MLA_NOPE - MLA_ROPE), F32)
    q_cat = jnp.concatenate([q_nope.reshape(B, S, MLA_H, MLA_NOPE), q_pe.reshape(B, S, MLA_H, MLA_ROPE), zpad],
                            axis=-1).reshape(B, S, MLA_H * MLA_DK)
    k_cat = jnp.concatenate([k_nope.reshape(B, S, MLA_H, MLA_NOPE),
                             jnp.broadcast_to(k_pe[:, :, None, :], (B, S, MLA_H, MLA_ROPE)), zpad],
                            axis=-1).reshape(B, S, MLA_H * MLA_DK)
    o_b = attention(q_cat, k_cat, v_b.reshape(B, S, -1), H=MLA_H, Hkv=MLA_H, dk=MLA_DK, dv=MLA_V,
                    scale=(MLA_NOPE + MLA_ROPE) ** -0.5, name=nm + "mla")

    xact = conv_silu(xbc.reshape(B, S, -1), W["conv_w"][l].astype(F32), P["conv_b"][l][None], name=nm + "conv")
    brow = jnp.pad(P["dt_bias"][l].reshape(1, -1), ((0, 0), (0, LANES - 2 * SSD_H)))
    arow = jnp.pad(-jnp.exp(P["a_log"][l].reshape(1, -1)), ((0, 0), (0, LANES - 2 * SSD_H)))
    raw = dtr.reshape(B, S, LANES)
    y_f = ssd_scan(xact, raw, brow, arow, rev=False, name=nm + "ssd_f")
    y_b = ssd_scan(xact, raw, brow, arow, rev=True, name=nm + "ssd_b")
    dsk = jnp.repeat(P["d_skip"][l], SSD_P)[None]
    o_c = ssd_out(y_f.reshape(T, -1), y_b.reshape(T, -1), xact.reshape(T, -1), z, dsk,
                  P["ssd_norm_g"][l][None], name=nm + "ssd_out")

    o = jnp.concatenate([o_a.reshape(T, -1).astype(BF16), o_b.reshape(T, -1).astype(BF16), o_c], axis=-1)
    mix = linear(o, W["w_out"][l], out_dtype=F32, name=nm + "out")
    x = gated_residual(x, gate1, mix.reshape(B, S, D), name=nm + "res1")

    h = rmsmod(x, P["norm2_g"][l][None], scale2, shift2, name=nm + "norm2")
    gu = linear(h.reshape(T, D), W["w_gate_up"][l], out_dtype=BF16, name=nm + "gate_up")
    act = swiglu(gu, name=nm + "swiglu")
    ffn = linear(act, W["w_down"][l], out_dtype=F32, name=nm + "down")
    return gated_residual(x, gate2, ffn.reshape(B, S, D), name=nm + "res2")


def kernel(x, c, w_ada, b_ada, norm1_g, norm2_g, w_in, q_norm_g, k_norm_g, mla_q_norm_g, w_uq, mla_kv_norm_g, w_ukv, conv_w, conv_b, dt_bias, a_log, d_skip, ssd_norm_g, w_out, w_gate_up, w_down, final_norm_g, loss_target, m_w_ada, m_b_ada, m_norm1_g, m_norm2_g, m_w_in, m_q_norm_g, m_k_norm_g, m_mla_q_norm_g, m_w_uq, m_mla_kv_norm_g, m_w_ukv, m_conv_w, m_conv_b, m_dt_bias, m_a_log, m_d_skip, m_ssd_norm_g, m_w_out, m_w_gate_up, m_w_down, m_final_norm_g, v_w_ada, v_b_ada, v_norm1_g, v_norm2_g, v_w_in, v_q_norm_g, v_k_norm_g, v_mla_q_norm_g, v_w_uq, v_mla_kv_norm_g, v_w_ukv, v_conv_w, v_conv_b, v_dt_bias, v_a_log, v_d_skip, v_ssd_norm_g, v_w_out, v_w_gate_up, v_w_down, v_final_norm_g):
    args = dict(locals())
    weights = {n: args[n] for n in WEIGHTS}
    moments_m = {n: args["m_" + n] for n in WEIGHTS}
    moments_v = {n: args["v_" + n] for n in WEIGHTS}
    B, S, D = x.shape
    L = w_ada.shape[0]
    T = B * S
    px, py, pc = _position()
    me = 4 * px + 2 * py + pc
    shard_shapes = {n: weights[n].shape for n in BIG}
    small_shapes = [weights[n].shape for n in SMALL]

    tabs = (*rope_tables(S, HEAD, GQA_H * HEAD), *rope_tables(S, MLA_ROPE, MLA_H * MLA_ROPE),
            *rope_tables(S, MLA_ROPE, 2 * MLA_ROPE))
    c_all = all_gather(c, name="gather_c").reshape(N_DEV * B, D)

    def local_loss(big_packed, w_ada_s, small, x):
        W = _full_weights(fsdp_gather(big_packed, name="weights"), shard_shapes, L)
        P = dict(zip(SMALL, small))
        for l in range(L):
            part = linear(c_all, w_ada_s[l], out_dtype=F32, a_silu=True, name=f"l{l}_ada")
            mod = gather_rows(part, me, name=f"l{l}_mod") + P["b_ada"][l][None]
            x = _layer(l, x, mod, W, P, tabs)
        return final_loss(x.reshape(T, D), P["final_norm_g"][None], loss_target.reshape(T, D), name="loss")

    big_packed = _pack([weights[n] for n in BIG], F32)
    small = tuple(weights[n] for n in SMALL)
    loss, (g_big, g_ada, g_small, grad_x) = jax.value_and_grad(local_loss, argnums=(0, 1, 2, 3))(
        big_packed, w_ada, small, x)
    loss = lax.psum(loss, ("x", "y", "c"))

    grads = dict(zip(BIG, _unpack(g_big, [shard_shapes[n] for n in BIG])))
    grads["w_ada"] = g_ada
    g_small_sum = sum_blocks(all_gather(_pack(g_small, F32), name="small_grads_ag"), name="small_grads_sum")
    grads.update(zip(SMALL, _unpack(g_small_sum, small_shapes)))

    delta, new_m, new_v = {}, {}, {}
    for n in ("w_ada",) + BIG:
        shp = weights[n].shape
        two_d = (int(np.prod(shp[:-1])), shp[-1])
        d_, m_, v_ = adamw(weights[n].reshape(two_d), grads[n].reshape(two_d), moments_m[n].reshape(two_d),
                           moments_v[n].reshape(two_d), name="adamw_" + n)
        delta[n], new_m[n], new_v[n] = d_.reshape(shp), m_.reshape(shp), v_.reshape(shp)
    d_, m_, v_ = adamw(_pack([weights[n] for n in SMALL], F32), g_small_sum,
                       _pack([moments_m[n] for n in SMALL], F32), _pack([moments_v[n] for n in SMALL], F32),
                       name="adamw_small")
    for tgt, packed in ((delta, d_), (new_m, m_), (new_v, v_)):
        tgt.update(zip(SMALL, _unpack(packed, small_shapes)))

    return (loss, grad_x, *[grads[n] for n in WEIGHTS], *[delta[n] for n in WEIGHTS],
            *[new_m[n] for n in WEIGHTS], *[new_v[n] for n in WEIGHTS])
```

```python
import functools
import math

import jax
import jax.numpy as jnp
import numpy as np
from jax import lax
from jax.experimental import pallas as pl
from jax.experimental.pallas import tpu as pltpu

F32 = jnp.float32
BF16 = jnp.bfloat16
N_DEV = 8
EPS = 1e-6
ROPE_THETA = 10000.0
GRID_W = 64

GQA_H, GQA_KV, HEAD = 6, 2, 128
MLA_H, MLA_QL, MLA_KVL, MLA_NOPE, MLA_ROPE, MLA_V = 4, 512, 256, 128, 64, 128
MLA_DK = 256
SSD_H, SSD_P, SSD_G, SSD_N, SSD_K, CHUNK = 12, 64, 2, 128, 5, 128
SSD_INNER = SSD_H * SSD_P
SSD_CONV_DIM = SSD_INNER + 2 * SSD_G * SSD_N
N_PAIR = SSD_H // 2
LANES = 128
IN_SPLITS = (768, 256, 256, 512, 256, 64, 768, 1280, 24)
IN_COLS = sum(IN_SPLITS)
SEG_ORDER = (0, 1, 2, 3, 4, 6, 7, 5, 8)
SEG_PAD = {5: 128, 8: 128}
IN_WIDTH = 4608

ADAM_LR, ADAM_B1, ADAM_B2, ADAM_EPS, ADAM_WD, ADAM_STEP = 0.001, 0.9, 0.999, 1e-08, 0.01, 10
VMEM_LIMIT = 56 * 1024 * 1024
MM_TM, MM_TN, MM_TK = 1024, 1024, 512


def _cp(**kw):
    return pltpu.CompilerParams(vmem_limit_bytes=VMEM_LIMIT, **kw)


def _tile(dim, cap, mult=128):
    if dim <= cap:
        return dim
    best = None
    t = mult
    while t <= cap:
        if dim % t == 0:
            best = t
        t += mult
    assert best is not None, (dim, cap)
    return best


def _sigmoid(x):
    return 1.0 / (1.0 + jnp.exp(-x))


def _dot(a, b, dims):
    return lax.dot_general(a, b, (dims, ((), ())), preferred_element_type=F32)


NN = ((1,), (0,))
NT = ((1,), (1,))
TN = ((0,), (0,))


def _dotf(a, b, dims=NN):
    return lax.dot_general(a, b, (dims, ((), ())), preferred_element_type=F32,
                           precision=lax.Precision.HIGHEST)


def _bf(x):
    return x.astype(BF16)


def mm(a, b, *, ta=False, tb=False, out_dtype=F32, a_silu=False, name):
    if ta:
        K, M = a.shape
    else:
        M, K = a.shape
    if tb:
        N, K2 = b.shape
    else:
        K2, N = b.shape
    assert K == K2, (a.shape, b.shape, ta, tb)
    tm, tn, tk = _tile(M, MM_TM), _tile(N, MM_TN), _tile(K, MM_TK)
    nk = K // tk
    dims = ((0 if ta else 1,), (1 if tb else 0,))

    def body(a_ref, b_ref, o_ref, acc_ref):
        k = pl.program_id(2)

        @pl.when(k == 0)
        def _():
            acc_ref[...] = jnp.zeros_like(acc_ref)

        av = a_ref[...]
        if a_silu:
            av = av.astype(F32)
            av = av * _sigmoid(av)
        acc_ref[...] += _dot(_bf(av), _bf(b_ref[...]), dims)

        @pl.when(k == nk - 1)
        def _():
            o_ref[...] = acc_ref[...].astype(o_ref.dtype)

    a_spec = (pl.BlockSpec((tk, tm), lambda i, j, k: (k, i)) if ta
              else pl.BlockSpec((tm, tk), lambda i, j, k: (i, k)))
    b_spec = (pl.BlockSpec((tn, tk), lambda i, j, k: (j, k)) if tb
              else pl.BlockSpec((tk, tn), lambda i, j, k: (k, j)))
    return pl.pallas_call(
        body, name=name, grid=(M // tm, N // tn, nk),
        in_specs=[a_spec, b_spec],
        out_specs=pl.BlockSpec((tm, tn), lambda i, j, k: (i, j)),
        out_shape=jax.ShapeDtypeStruct((M, N), out_dtype),
        scratch_shapes=[pltpu.VMEM((tm, tn), F32)],
        compiler_params=_cp(dimension_semantics=("parallel", "parallel", "arbitrary")),
    )(a, b)


def linear(a, w, *, out_dtype, name, a_silu=False):
    @jax.custom_vjp
    def f(a, w):
        return mm(a, w, out_dtype=out_dtype, a_silu=a_silu, name=name + "_fwd")

    def fwd(a, w):
        return f(a, w), (a, w)

    def bwd(res, g):
        a, w = res
        if a_silu:
            da = jnp.zeros_like(a)
        else:
            da = mm(g, w, tb=True, out_dtype=a.dtype, name=name + "_da")
        dw = mm(a, g, ta=True, out_dtype=w.dtype, a_silu=a_silu, name=name + "_dw")
        return da, dw

    f.defvjp(fwd, bwd)
    return f(a, w)


def rmsmod(x, g, scale, shift, *, name):
    B, S, D = x.shape
    ts = _tile(S, 256, 8)
    row = pl.BlockSpec((1, ts, D), lambda b, j: (b, j, 0))
    per_b = pl.BlockSpec((1, 1, D), lambda b, j: (b, 0, 0))
    gspec = pl.BlockSpec((1, D), lambda b, j: (0, 0))

    def fwd_call(x, g, scale, shift):
        def body(x_ref, g_ref, sc_ref, sh_ref, o_ref):
            xv = x_ref[0]
            r = lax.rsqrt(jnp.mean(xv * xv, axis=-1, keepdims=True) + EPS)
            y = xv * r * g_ref[...]
            o_ref[0] = (y * (1.0 + sc_ref[0]) + sh_ref[0]).astype(o_ref.dtype)

        return pl.pallas_call(
            body, name=name + "_fwd", grid=(B, S // ts),
            in_specs=[row, gspec, per_b, per_b], out_specs=row,
            out_shape=jax.ShapeDtypeStruct((B, S, D), BF16), compiler_params=_cp(),
        )(x, g, scale, shift)

    def bwd_call(x, g, scale, dh):
        def body(x_ref, g_ref, sc_ref, dh_ref, dx_ref, dg_ref, dsc_ref, dsh_ref):
            j = pl.program_id(1)
            xv = x_ref[0]
            dh = dh_ref[0].astype(F32)
            r = lax.rsqrt(jnp.mean(xv * xv, axis=-1, keepdims=True) + EPS)
            xn = xv * r
            gv = g_ref[...]
            dy = dh * (1.0 + sc_ref[0])
            dxn = dy * gv
            dx_ref[0] = r * (dxn - xn * jnp.mean(dxn * xn, axis=-1, keepdims=True))

            @pl.when(j == 0)
            def _():
                dg_ref[...] = jnp.zeros_like(dg_ref)
                dsc_ref[...] = jnp.zeros_like(dsc_ref)
                dsh_ref[...] = jnp.zeros_like(dsh_ref)

            dg_ref[0] += jnp.sum(dy * xn, axis=0, keepdims=True)
            dsc_ref[0] += jnp.sum(dh * xn * gv, axis=0, keepdims=True)
            dsh_ref[0] += jnp.sum(dh, axis=0, keepdims=True)

        vec = jax.ShapeDtypeStruct((B, 1, D), F32)
        return pl.pallas_call(
            body, name=name + "_bwd", grid=(B, S // ts),
            in_specs=[row, gspec, per_b, row], out_specs=[row, per_b, per_b, per_b],
            out_shape=[jax.ShapeDtypeStruct((B, S, D), F32), vec, vec, vec], compiler_params=_cp(),
        )(x, g, scale, dh)

    @jax.custom_vjp
    def f(x, g, scale, shift):
        return fwd_call(x, g, scale, shift)

    def fwd(x, g, scale, shift):
        return fwd_call(x, g, scale, shift), (x, g, scale)

    def bwd(res, dh):
        x, g, scale = res
        dx, dg, dsc, dsh = bwd_call(x, g, scale, dh)
        return dx, jnp.sum(dg, axis=0), dsc, dsh

    f.defvjp(fwd, bwd)
    return f(x, g, scale, shift)


def group_rmsnorm(x, g, *, gs, out_dtype, name):
    T, W = x.shape
    ng = W // gs
    tr = _tile(T, 512, 8)
    row = pl.BlockSpec((tr, W), lambda i: (i, 0))
    gspec = pl.BlockSpec((1, W), lambda i: (0, 0))

    def fwd_call(x, g):
        def body(x_ref, g_ref, o_ref):
            for i in range(ng):
                sl = slice(i * gs, (i + 1) * gs)
                xv = x_ref[:, sl]
                r = lax.rsqrt(jnp.mean(xv * xv, axis=-1, keepdims=True) + EPS)
                o_ref[:, sl] = (xv * r * g_ref[:, sl]).astype(o_ref.dtype)

        return pl.pallas_call(
            body, name=name + "_fwd", grid=(T // tr,), in_specs=[row, gspec], out_specs=row,
            out_shape=jax.ShapeDtypeStruct((T, W), out_dtype), compiler_params=_cp(),
        )(x, g)

    def bwd_call(x, g, dy):
        def body(x_ref, g_ref, dy_ref, dx_ref, dg_ref):
            @pl.when(pl.program_id(0) == 0)
            def _():
                dg_ref[...] = jnp.zeros_like(dg_ref)

            for i in range(ng):
                sl = slice(i * gs, (i + 1) * gs)
                xv = x_ref[:, sl]
                dyv = dy_ref[:, sl].astype(F32)
                r = lax.rsqrt(jnp.mean(xv * xv, axis=-1, keepdims=True) + EPS)
                xn = xv * r
                dxn = dyv * g_ref[:, sl]
                dx_ref[:, sl] = r * (dxn - xn * jnp.mean(dxn * xn, axis=-1, keepdims=True))
                dg_ref[:, sl] += jnp.sum(dyv * xn, axis=0, keepdims=True)

        return pl.pallas_call(
            body, name=name + "_bwd", grid=(T // tr,), in_specs=[row, gspec, row],
            out_specs=[row, gspec],
            out_shape=[jax.ShapeDtypeStruct((T, W), F32), jax.ShapeDtypeStruct((1, W), F32)],
            compiler_params=_cp(),
        )(x, g, dy)

    @jax.custom_vjp
    def f(x, g):
        return fwd_call(x, g)

    def fwd(x, g):
        return fwd_call(x, g), (x, g)

    def bwd(res, dy):
        return bwd_call(res[0], res[1], dy)

    f.defvjp(fwd, bwd)
    return f(x, g)


def rope_tables(seq_len, rot_dim, width):
    rows = seq_len // GRID_W
    row_idx = jnp.repeat(jnp.arange(rows), GRID_W).astype(F32)
    col_idx = jnp.tile(jnp.arange(GRID_W), rows).astype(F32)
    axis_dim = rot_dim // 2
    inv_freq = jnp.power(ROPE_THETA, -jnp.arange(0, axis_dim, 2, dtype=F32) / axis_dim)
    ang_r = row_idx[:, None] * inv_freq[None, :]
    ang_c = col_idx[:, None] * inv_freq[None, :]
    cos = jnp.concatenate([jnp.cos(ang_r), jnp.cos(ang_r), jnp.cos(ang_c), jnp.cos(ang_c)], axis=-1)
    sin = jnp.concatenate([-jnp.sin(ang_r), jnp.sin(ang_r), -jnp.sin(ang_c), jnp.sin(ang_c)], axis=-1)
    reps = width // rot_dim
    return jnp.tile(cos, (1, reps)), jnp.tile(sin, (1, reps))


def rope(x, cos, sin, *, d, name):
    B, S, W = x.shape
    ts = _tile(S, 512, 8)
    row = pl.BlockSpec((1, ts, W), lambda b, j: (b, j, 0))
    tab = pl.BlockSpec((ts, W), lambda b, j: (j, 0))

    def call(x, inverse, nm):
        def body(x_ref, c_ref, s_ref, o_ref):
            xv = x_ref[0]
            lane = lax.broadcasted_iota(jnp.int32, xv.shape, 1)
            first = (lane // d) % 2 == 0

            def swap(v):
                return jnp.where(first, pltpu.roll(v, W - d, 1), pltpu.roll(v, d, 1))

            if inverse:
                o_ref[0] = xv * c_ref[...] + swap(xv * s_ref[...])
            else:
                o_ref[0] = xv * c_ref[...] + swap(xv) * s_ref[...]

        return pl.pallas_call(
            body, name=nm, grid=(B, S // ts), in_specs=[row, tab, tab], out_specs=row,
            out_shape=jax.ShapeDtypeStruct((B, S, W), F32), compiler_params=_cp(),
        )(x, cos, sin)

    @jax.custom_vjp
    def f(x):
        return call(x, False, name + "_fwd")

    def fwd(x):
        return call(x, False, name + "_fwd"), None

    def bwd(_, g):
        return (call(g, True, name + "_bwd"),)

    f.defvjp(fwd, bwd)
    return f(x)


def attention(q, k, v, *, H, Hkv, dk, dv, scale, name):
    B, S, _ = q.shape
    rep = H // Hkv
    tq = _tile(S, 256, 8)
    tkb = _tile(S, 256, 8)

    def fwd_call(q, k, v):
        def body(q_ref, k_ref, v_ref, o_ref, lse_ref):
            s = _dot(_bf(q_ref[0]), _bf(k_ref[0]), NT) * scale
            m = jnp.max(s, axis=-1, keepdims=True)
            p = jnp.exp(s - m)
            l = jnp.sum(p, axis=-1, keepdims=True)
            o_ref[0] = _dot(_bf(p), _bf(v_ref[0]), NN) / l
            lse_ref[0, 0] = m + jnp.log(l)

        return pl.pallas_call(
            body, name=name + "_fwd", grid=(B, H, S // tq),
            in_specs=[pl.BlockSpec((1, tq, dk), lambda b, h, i: (b, i, h)),
                      pl.BlockSpec((1, S, dk), lambda b, h, i: (b, 0, h // rep)),
                      pl.BlockSpec((1, S, dv), lambda b, h, i: (b, 0, h // rep))],
            out_specs=[pl.BlockSpec((1, tq, dv), lambda b, h, i: (b, i, h)),
                       pl.BlockSpec((1, 1, tq, 1), lambda b, h, i: (b, h, i, 0))],
            out_shape=[jax.ShapeDtypeStruct((B, S, H * dv), F32),
                       jax.ShapeDtypeStruct((B, H, S, 1), F32)],
            compiler_params=_cp(),
        )(q, k, v)

    def dq_call(q, k, v, o, do, lse):
        def body(q_ref, k_ref, v_ref, o_ref, do_ref, lse_ref, dq_ref, delta_ref):
            kb = _bf(k_ref[0])
            s = _dot(_bf(q_ref[0]), kb, NT) * scale
            p = jnp.exp(s - lse_ref[0, 0])
            dov = do_ref[0]
            delta = jnp.sum(dov * o_ref[0], axis=-1, keepdims=True)
            dp = _dot(_bf(dov), _bf(v_ref[0]), NT)
            ds = p * (dp - delta)
            dq_ref[0] = _dot(_bf(ds), kb, NN) * scale
            delta_ref[0, 0] = delta

        qs = pl.BlockSpec((1, tq, dk), lambda b, h, i: (b, i, h))
        os_ = pl.BlockSpec((1, tq, dv), lambda b, h, i: (b, i, h))
        col = pl.BlockSpec((1, 1, tq, 1), lambda b, h, i: (b, h, i, 0))
        return pl.pallas_call(
            body, name=name + "_dq", grid=(B, H, S // tq),
            in_specs=[qs, pl.BlockSpec((1, S, dk), lambda b, h, i: (b, 0, h // rep)),
                      pl.BlockSpec((1, S, dv), lambda b, h, i: (b, 0, h // rep)), os_, os_, col],
            out_specs=[qs, col],
            out_shape=[jax.ShapeDtypeStruct(q.shape, F32), jax.ShapeDtypeStruct((B, H, S, 1), F32)],
            compiler_params=_cp(),
        )(q, k, v, o, do, lse)

    def dkv_call(q, k, v, do, lse, delta):
        def body(q_ref, k_ref, v_ref, do_ref, lse_ref, delta_ref, dk_ref, dv_ref):
            @pl.when(pl.program_id(3) == 0)
            def _():
                dk_ref[...] = jnp.zeros_like(dk_ref)
                dv_ref[...] = jnp.zeros_like(dv_ref)

            qb = _bf(q_ref[0])
            dob = _bf(do_ref[0])
            s = _dot(qb, _bf(k_ref[0]), NT) * scale
            p = jnp.exp(s - lse_ref[0, 0])
            dv_ref[0] += _dot(_bf(p), dob, TN)
            dp = _dot(dob, _bf(v_ref[0]), NT)
            ds = p * (dp - delta_ref[0, 0])
            dk_ref[0] += _dot(_bf(ds), qb, TN) * scale

        hq = lambda b, g, j, r: (b, 0, g * rep + r)
        colq = pl.BlockSpec((1, 1, S, 1), lambda b, g, j, r: (b, g * rep + r, 0, 0))
        ks = pl.BlockSpec((1, tkb, dk), lambda b, g, j, r: (b, j, g))
        vs = pl.BlockSpec((1, tkb, dv), lambda b, g, j, r: (b, j, g))
        return pl.pallas_call(
            body, name=name + "_dkv", grid=(B, Hkv, S // tkb, rep),
            in_specs=[pl.BlockSpec((1, S, dk), hq), ks, vs, pl.BlockSpec((1, S, dv), hq), colq, colq],
            out_specs=[ks, vs],
            out_shape=[jax.ShapeDtypeStruct(k.shape, F32), jax.ShapeDtypeStruct(v.shape, F32)],
            compiler_params=_cp(),
        )(q, k, v, do, lse, delta)

    @jax.custom_vjp
    def f(q, k, v):
        return fwd_call(q, k, v)[0]

    def fwd(q, k, v):
        o, lse = fwd_call(q, k, v)
        return o, (q, k, v, o, lse)

    def bwd(res, do):
        q, k, v, o, lse = res
        dq, delta = dq_call(q, k, v, o, do, lse)
        dk_, dv_ = dkv_call(q, k, v, do, lse, delta)
        return dq, dk_, dv_

    f.defvjp(fwd, bwd)
    return f(q, k, v)


def conv_silu(x, w, b, *, name):
    B, S, C = x.shape
    tc = _tile(C, 256)
    pad = SSD_K // 2
    xs = pl.BlockSpec((1, S, tc), lambda bi, j: (bi, 0, j))
    ws = pl.BlockSpec((SSD_K, tc), lambda bi, j: (0, j))
    bs = pl.BlockSpec((1, tc), lambda bi, j: (0, j))

    def shifted(v, off):
        if off == 0:
            return v
        t = lax.broadcasted_iota(jnp.int32, v.shape, 0)
        r = pltpu.roll(v, (-off) % S, 0)
        return jnp.where((t + off >= 0) & (t + off < S), r, 0.0)

    def pre_act(xv, wv, bv):
        u = jnp.zeros_like(xv) + bv
        for k in range(SSD_K):
            u = u + wv[k:k + 1, :] * shifted(xv, k - pad)
        return u

    def fwd_call(x, w, b):
        def body(x_ref, w_ref, b_ref, o_ref):
            u = pre_act(x_ref[0], w_ref[...], b_ref[...])
            o_ref[0] = u * _sigmoid(u)

        return pl.pallas_call(
            body, name=name + "_fwd", grid=(B, C // tc), in_specs=[xs, ws, bs], out_specs=xs,
            out_shape=jax.ShapeDtypeStruct((B, S, C), F32), compiler_params=_cp(),
        )(x, w, b)

    def bwd_call(x, w, b, dy):
        def body(x_ref, w_ref, b_ref, dy_ref, dx_ref, dw_ref):
            xv = x_ref[0]
            wv = w_ref[...]
            u = pre_act(xv, wv, b_ref[...])
            sg = _sigmoid(u)
            du = dy_ref[0] * (sg * (1.0 + u * (1.0 - sg)))
            dx = jnp.zeros_like(xv)
            for k in range(SSD_K):
                dx = dx + wv[k:k + 1, :] * shifted(du, pad - k)
                dw_ref[0, k:k + 1, :] = jnp.sum(du * shifted(xv, k - pad), axis=0, keepdims=True)
            dw_ref[0, SSD_K:SSD_K + 1, :] = jnp.sum(du, axis=0, keepdims=True)
            dw_ref[0, SSD_K + 1:8, :] = jnp.zeros((8 - SSD_K - 1, tc), F32)
            dx_ref[0] = dx

        return pl.pallas_call(
            body, name=name + "_bwd", grid=(B, C // tc), in_specs=[xs, ws, bs, xs],
            out_specs=[xs, pl.BlockSpec((1, 8, tc), lambda bi, j: (bi, 0, j))],
            out_shape=[jax.ShapeDtypeStruct((B, S, C), F32), jax.ShapeDtypeStruct((B, 8, C), F32)],
            compiler_params=_cp(),
        )(x, w, b, dy)

    @jax.custom_vjp
    def f(x, w, b):
        return fwd_call(x, w, b)

    def fwd(x, w, b):
        return fwd_call(x, w, b), (x, w, b)

    def bwd(res, dy):
        x, w, b = res
        dx, dwb = bwd_call(x, w, b, dy)
        dwb = jnp.sum(dwb, axis=0)
        return dx, dwb[:SSD_K], dwb[SSD_K:SSD_K + 1]

    f.defvjp(fwd, bwd)
    return f(x, w, b)


def _softplus(x):
    return jnp.maximum(x, 0.0) + jnp.log1p(jnp.exp(-jnp.abs(x)))


def _ssd_prep(raw, raw_t, brow, arow, bcol, acol, rev):
    li = lax.broadcasted_iota(jnp.int32, (CHUNK, CHUNK), 0)
    ki = lax.broadcasted_iota(jnp.int32, (CHUNK, CHUNK), 1)
    later = (li <= ki) if rev else (li >= ki)
    dt = _softplus(raw + brow)
    a = dt * arow
    cs = _dotf(later.astype(F32), a)
    tot = jnp.sum(a, axis=0, keepdims=True)
    a_t = _softplus(raw_t + bcol) * acol
    earlier = (li >= ki) if rev else (li <= ki)
    cs_t = _dotf(a_t, earlier.astype(F32))
    return dt, a, cs, tot, cs_t, later


def _lane_pick(mat, j):
    lane = lax.broadcasted_iota(jnp.int32, mat.shape, 1)
    return jnp.sum(jnp.where(lane == j, mat, 0.0), axis=1, keepdims=True)


def _head_sum(t, first):
    s0 = jnp.sum(jnp.where(first, t, 0.0), axis=1, keepdims=True)
    s1 = jnp.sum(jnp.where(first, 0.0, t), axis=1, keepdims=True)
    return s0, s1


def ssd_scan(xbc, raw, brow, arow, *, rev, name):
    B, S, _ = xbc.shape
    NC = S // CHUNK
    off = SSD_H if rev else 0
    n_dt = 2 * SSD_H

    def chunk_of(c):
        return (NC - 1 - c) if rev else c

    def specs(cmap):
        return dict(
            x=pl.BlockSpec((1, CHUNK, SSD_INNER), lambda b, c: (b, cmap(c), 0)),
            bm=pl.BlockSpec((1, CHUNK, 2 * SSD_N), lambda b, c: (b, cmap(c), SSD_INNER // (2 * SSD_N))),
            cm=pl.BlockSpec((1, CHUNK, 2 * SSD_N), lambda b, c: (b, cmap(c), SSD_INNER // (2 * SSD_N) + 1)),
            raw=pl.BlockSpec((1, CHUNK, LANES), lambda b, c: (b, cmap(c), 0)),
            raw_t=pl.BlockSpec((1, n_dt, CHUNK), lambda b, c: (b, 0, cmap(c))),
            row=pl.BlockSpec((1, LANES), lambda b, c: (0, 0)),
            colv=pl.BlockSpec((n_dt, 1), lambda b, c: (0, 0)),
            hs=pl.BlockSpec((1, 1, N_PAIR, SSD_N, LANES), lambda b, c: (b, cmap(c), 0, 0, 0)),
        )

    def head_terms(prep, j, first_dummy=None):
        dt, a, cs, tot, cs_t, later = prep
        cs_c = _lane_pick(cs, j)
        cs_r = cs_t[j:j + 1, :]
        dt_c = _lane_pick(dt, j)
        tot_j = _lane_pick(tot, j)
        L = jnp.exp(jnp.where(later, cs_c - cs_r, -1e30))
        return cs_c, cs_r, dt_c, tot_j, L

    def fwd_call(xbc, raw, raw_t, brow, arow, bcol, acol):
        def body(x_ref, bm_ref, cm_ref, raw_ref, rawt_ref, brow_ref, arow_ref, bcol_ref, acol_ref,
                 y_ref, hs_ref, st_ref):
            @pl.when(pl.program_id(1) == 0)
            def _():
                st_ref[...] = jnp.zeros_like(st_ref)

            prep = _ssd_prep(raw_ref[0], rawt_ref[0], brow_ref[...], arow_ref[...],
                             bcol_ref[...], acol_ref[...], rev)
            lane = lax.broadcasted_iota(jnp.int32, (CHUNK, LANES), 1)
            first = lane < SSD_P
            for g in range(SSD_G):
                Bg = _bf(bm_ref[0, :, g * SSD_N:(g + 1) * SSD_N])
                Cg = _bf(cm_ref[0, :, g * SSD_N:(g + 1) * SSD_N])
                G = _dot(Cg, Bg, NT)
                for pp in range(N_PAIR // SSD_G):
                    pi = g * (N_PAIR // SSD_G) + pp
                    c0, _, d0, t0, L0 = head_terms(prep, off + 2 * pi)
                    c1, _, d1, t1, L1 = head_terms(prep, off + 2 * pi + 1)
                    xd = x_ref[0, :, pi * LANES:(pi + 1) * LANES] * jnp.where(first, d0, d1)
                    xdb = _bf(xd)
                    y = jnp.where(first, _dot(_bf(G * L0), xdb, NN), _dot(_bf(G * L1), xdb, NN))
                    dec = jnp.where(first, jnp.exp(t0 - c0), jnp.exp(t1 - c1))
                    h_prev = st_ref[pi]
                    hs_ref[0, 0, pi] = h_prev
                    y = y + _dot(Cg, _bf(h_prev), NN) * jnp.where(first, jnp.exp(c0), jnp.exp(c1))
                    y_ref[0, :, pi * LANES:(pi + 1) * LANES] = y
                    etot = jnp.where(first[:1], jnp.exp(t0), jnp.exp(t1))
                    st_ref[pi] = h_prev * etot + _dot(Bg, _bf(xd * dec), TN)

        sp = specs(chunk_of)
        return pl.pallas_call(
            body, name=name + "_fwd", grid=(B, NC),
            in_specs=[sp["x"], sp["bm"], sp["cm"], sp["raw"], sp["raw_t"], sp["row"], sp["row"],
                      sp["colv"], sp["colv"]],
            out_specs=[sp["x"], sp["hs"]],
            out_shape=[jax.ShapeDtypeStruct((B, S, SSD_INNER), F32),
                       jax.ShapeDtypeStruct((B, NC, N_PAIR, SSD_N, LANES), F32)],
            scratch_shapes=[pltpu.VMEM((N_PAIR, SSD_N, LANES), F32)],
            compiler_params=_cp(),
        )(xbc, xbc, xbc, raw, raw_t, brow, arow, bcol, acol)

    def bwd_call(xbc, raw, raw_t, brow, arow, bcol, acol, hs, dy):
        def body(x_ref, bm_ref, cm_ref, raw_ref, rawt_ref, brow_ref, arow_ref, bcol_ref, acol_ref,
                 hs_ref, dy_ref, dxbc_ref, draw_ref, da_ref, dst_ref):
            @pl.when(pl.program_id(1) == 0)
            def _():
                dst_ref[...] = jnp.zeros_like(dst_ref)
                da_ref[...] = jnp.zeros_like(da_ref)

            raw_v = raw_ref[0]
            prep = _ssd_prep(raw_v, rawt_ref[0], brow_ref[...], arow_ref[...],
                             bcol_ref[...], acol_ref[...], rev)
            dt, a, cs, tot, cs_t, later = prep
            li = lax.broadcasted_iota(jnp.int32, (CHUNK, CHUNK), 0)
            ki = lax.broadcasted_iota(jnp.int32, (CHUNK, CHUNK), 1)
            later_t = (li >= ki) if rev else (li <= ki)
            lane = lax.broadcasted_iota(jnp.int32, (CHUNK, LANES), 1)
            first = lane < SSD_P
            dcs_all = jnp.zeros((CHUNK, LANES), F32)
            ddt_all = jnp.zeros((CHUNK, LANES), F32)
            dtot_all = jnp.zeros((1, LANES), F32)
            for g in range(SSD_G):
                Bg = _bf(bm_ref[0, :, g * SSD_N:(g + 1) * SSD_N])
                Cg = _bf(cm_ref[0, :, g * SSD_N:(g + 1) * SSD_N])
                G = _dot(Cg, Bg, NT)
                Gt = _dot(Bg, Cg, NT)
                dG = jnp.zeros((CHUNK, CHUNK), F32)
                dB = jnp.zeros((CHUNK, SSD_N), F32)
                dC = jnp.zeros((CHUNK, SSD_N), F32)
                for pp in range(N_PAIR // SSD_G):
                    pi = g * (N_PAIR // SSD_G) + pp
                    j0, j1 = off + 2 * pi, off + 2 * pi + 1
                    c0, r0, d0, t0, L0 = head_terms(prep, j0)
                    c1, r1, d1, t1, L1 = head_terms(prep, j1)
                    Lt0 = jnp.exp(jnp.where(later_t, r0 - c0, -1e30))
                    Lt1 = jnp.exp(jnp.where(later_t, r1 - c1, -1e30))
                    xv = x_ref[0, :, pi * LANES:(pi + 1) * LANES]
                    dtp = jnp.where(first, d0, d1)
                    xd = xv * dtp
                    xdb = _bf(xd)
                    dyv = dy_ref[0, :, pi * LANES:(pi + 1) * LANES]
                    dyb = _bf(dyv)
                    dec = jnp.where(first, jnp.exp(t0 - c0), jnp.exp(t1 - c1))
                    ecs = jnp.where(first, jnp.exp(c0), jnp.exp(c1))
                    et0, et1 = jnp.exp(t0), jnp.exp(t1)
                    etot = jnp.where(first[:1], et0, et1)
                    h_prev = hs_ref[0, 0, pi]
                    hpb = _bf(h_prev)
                    dhn = dst_ref[pi]
                    dhb = _bf(dhn)
                    W0, W1 = G * L0, G * L1
                    Wt0, Wt1 = Gt * Lt0, Gt * Lt1
                    bdh = _dot(Bg, dhb, NN)
                    dxd = jnp.where(first, _dot(_bf(Wt0), dyb, NN), _dot(_bf(Wt1), dyb, NN)) + bdh * dec
                    dy0 = _bf(jnp.where(first, dyv, 0.0))
                    dy1 = _bf(jnp.where(first, 0.0, dyv))
                    Q0, Q1 = _dot(dy0, xdb, NT), _dot(dy1, xdb, NT)
                    Qt0, Qt1 = _dot(xdb, dy0, NT), _dot(xdb, dy1, NT)
                    dG = dG + Q0 * L0 + Q1 * L1
                    dcs0 = (jnp.sum(Q0 * W0, axis=1, keepdims=True)
                            - jnp.sum(Qt0 * Wt0, axis=1, keepdims=True))
                    dcs1 = (jnp.sum(Q1 * W1, axis=1, keepdims=True)
                            - jnp.sum(Qt1 * Wt1, axis=1, keepdims=True))
                    dye = dyv * ecs
                    dyeb = _bf(dye)
                    s0, s1 = _head_sum(dye * _dot(Cg, hpb, NN), first)
                    dcs0, dcs1 = dcs0 + s0, dcs1 + s1
                    dC = dC + _dot(dyeb, hpb, NT)
                    dB = dB + _dot(_bf(xd * dec), dhb, NT)
                    u0, u1 = _head_sum(xd * bdh * dec, first)
                    dcs0, dcs1 = dcs0 - u0, dcs1 - u1
                    w = jnp.sum(dhn * h_prev, axis=0, keepdims=True)
                    w0, w1 = _head_sum(w, first[:1])
                    dt0 = jnp.sum(u0, axis=0, keepdims=True) + et0 * w0
                    dt1 = jnp.sum(u1, axis=0, keepdims=True) + et1 * w1
                    dst_ref[pi] = _dot(Cg, dyeb, TN) + dhn * etot
                    q0, q1 = _head_sum(dxd * xv, first)
                    dxbc_ref[0, :, pi * LANES:(pi + 1) * LANES] = dxd * dtp
                    dcs_all = dcs_all + jnp.where(lane == j0, dcs0, 0.0) + jnp.where(lane == j1, dcs1, 0.0)
                    ddt_all = ddt_all + jnp.where(lane == j0, q0, 0.0) + jnp.where(lane == j1, q1, 0.0)
                    dtot_all = (dtot_all + jnp.where(lane[:1] == j0, dt0, 0.0)
                                + jnp.where(lane[:1] == j1, dt1, 0.0))
                dGb = _bf(dG)
                dC = dC + _dot(dGb, Bg, NN)
                dB = dB + _dot(dGb, Cg, TN)
                dxbc_ref[0, :, SSD_INNER + g * SSD_N:SSD_INNER + (g + 1) * SSD_N] = dB
                dxbc_ref[0, :, SSD_INNER + (SSD_G + g) * SSD_N:SSD_INNER + (SSD_G + g + 1) * SSD_N] = dC
            da = _dotf(later_t.astype(F32), dcs_all) + dtot_all
            ddt = ddt_all + da * arow_ref[...]
            da_ref[0] += jnp.sum(da * dt, axis=0, keepdims=True)
            draw_ref[0] = ddt * _sigmoid(raw_v + brow_ref[...])

        def rchunk(c):
            return c if rev else (NC - 1 - c)

        sp = specs(rchunk)
        full = pl.BlockSpec((1, CHUNK, SSD_CONV_DIM), lambda b, c: (b, rchunk(c), 0))
        return pl.pallas_call(
            body, name=name + "_bwd", grid=(B, NC),
            in_specs=[sp["x"], sp["bm"], sp["cm"], sp["raw"], sp["raw_t"], sp["row"], sp["row"],
                      sp["colv"], sp["colv"], sp["hs"], sp["x"]],
            out_specs=[full, sp["raw"], pl.BlockSpec((1, 1, LANES), lambda b, c: (b, 0, 0))],
            out_shape=[jax.ShapeDtypeStruct((B, S, SSD_CONV_DIM), F32),
                       jax.ShapeDtypeStruct((B, S, LANES), F32),
                       jax.ShapeDtypeStruct((B, 1, LANES), F32)],
            scratch_shapes=[pltpu.VMEM((N_PAIR, SSD_N, LANES), F32)],
            compiler_params=_cp(),
        )(xbc, xbc, xbc, raw, raw_t, brow, arow, bcol, acol, hs, dy)

    def aux(raw, brow, arow):
        raw_t = jnp.swapaxes(raw[:, :, :n_dt], 1, 2)
        return raw_t, brow[0, :n_dt][:, None], arow[0, :n_dt][:, None]

    @jax.custom_vjp
    def f(xbc, raw, brow, arow):
        raw_t, bcol, acol = aux(raw, brow, arow)
        return fwd_call(xbc, raw, raw_t, brow, arow, bcol, acol)[0]

    def fwd(xbc, raw, brow, arow):
        raw_t, bcol, acol = aux(raw, brow, arow)
        y, hs = fwd_call(xbc, raw, raw_t, brow, arow, bcol, acol)
        return y, (xbc, raw, brow, arow, hs)

    def bwd(res, dy):
        xbc, raw, brow, arow, hs = res
        raw_t, bcol, acol = aux(raw, brow, arow)
        dxbc, draw, da = bwd_call(xbc, raw, raw_t, brow, arow, bcol, acol, hs, dy)
        dbrow = jnp.sum(draw, axis=(0, 1))[None, :]
        return dxbc, draw, dbrow, jnp.sum(da, axis=0)

    f.defvjp(fwd, bwd)
    return f(xbc, raw, brow, arow)


def ssd_out(yf, yb, xbc, z, dsk, g, *, name):
    T, W = yf.shape
    gs = W // SSD_G
    tr = _tile(T, 512, 8)
    row = pl.BlockSpec((tr, W), lambda i: (i, 0))
    vec = pl.BlockSpec((1, W), lambda i: (0, 0))

    def normed(yv, gv):
        outs, rs = [], []
        for i in range(SSD_G):
            sl = slice(i * gs, (i + 1) * gs)
            r = lax.rsqrt(jnp.mean(yv[:, sl] * yv[:, sl], axis=-1, keepdims=True) + EPS)
            rs.append(r)
            outs.append(yv[:, sl] * r)
        return outs, rs

    def fwd_call(yf, yb, xbc, z, dsk, g):
        def body(yf_ref, yb_ref, xs_ref, z_ref, dsk_ref, g_ref, o_ref):
            zv = z_ref[...]
            yv = (yf_ref[...] + yb_ref[...] + xs_ref[...] * dsk_ref[...]) * (zv * _sigmoid(zv))
            outs, _ = normed(yv, g_ref[...])
            for i in range(SSD_G):
                sl = slice(i * gs, (i + 1) * gs)
                o_ref[:, sl] = (outs[i] * g_ref[:, sl]).astype(o_ref.dtype)

        return pl.pallas_call(
            body, name=name + "_fwd", grid=(T // tr,), in_specs=[row, row, row, row, vec, vec],
            out_specs=row, out_shape=jax.ShapeDtypeStruct((T, W), BF16), compiler_params=_cp(),
        )(yf, yb, xbc, z, dsk, g)

    def bwd_call(yf, yb, xbc, z, dsk, g, do):
        def body(yf_ref, yb_ref, xs_ref, z_ref, dsk_ref, g_ref, do_ref, dy_ref, dxs_ref, dz_ref,
                 ddsk_ref, dg_ref):
            @pl.when(pl.program_id(0) == 0)
            def _():
                ddsk_ref[...] = jnp.zeros_like(ddsk_ref)
                dg_ref[...] = jnp.zeros_like(dg_ref)

            zv = z_ref[...]
            sg = _sigmoid(zv)
            sz = zv * sg
            xs = xs_ref[...]
            pre = yf_ref[...] + yb_ref[...] + xs * dsk_ref[...]
            yv = pre * sz
            outs, rs = normed(yv, g_ref[...])
            for i in range(SSD_G):
                sl = slice(i * gs, (i + 1) * gs)
                dov = do_ref[:, sl].astype(F32)
                xn = outs[i]
                dxn = dov * g_ref[:, sl]
                dyv = rs[i] * (dxn - xn * jnp.mean(dxn * xn, axis=-1, keepdims=True))
                dg_ref[:, sl] += jnp.sum(dov * xn, axis=0, keepdims=True)
                dpre = dyv * sz[:, sl]
                dy_ref[:, sl] = dpre
                dxs_ref[:, sl] = dpre * dsk_ref[:, sl]
                ddsk_ref[:, sl] += jnp.sum(dpre * xs[:, sl], axis=0, keepdims=True)
                dz_ref[:, sl] = dyv * pre[:, sl] * (sg[:, sl] * (1.0 + zv[:, sl] * (1.0 - sg[:, sl])))

        o = jax.ShapeDtypeStruct((T, W), F32)
        v = jax.ShapeDtypeStruct((1, W), F32)
        return pl.pallas_call(
            body, name=name + "_bwd", grid=(T // tr,), in_specs=[row, row, row, row, vec, vec, row],
            out_specs=[row, row, row, vec, vec], out_shape=[o, o, o, v, v], compiler_params=_cp(),
        )(yf, yb, xbc, z, dsk, g, do)

    @jax.custom_vjp
    def f(yf, yb, xbc, z, dsk, g):
        return fwd_call(yf, yb, xbc, z, dsk, g)

    def fwd(yf, yb, xbc, z, dsk, g):
        return fwd_call(yf, yb, xbc, z, dsk, g), (yf, yb, xbc, z, dsk, g)

    def bwd(res, do):
        dy, dxs, dz, ddsk, dg = bwd_call(*res, do)
        dxbc = jnp.pad(dxs, ((0, 0), (0, res[2].shape[1] - W)))
        return dy, dy, dxbc, dz, ddsk, dg

    f.defvjp(fwd, bwd)
    return f(yf, yb, xbc, z, dsk, g)


def swiglu(gu, *, name):
    T, F2 = gu.shape
    Fh = F2 // 2
    tr, tf = _tile(T, 512, 8), _tile(Fh, 512)
    nf = Fh // tf
    gs = pl.BlockSpec((tr, tf), lambda i, j: (i, j))
    us = pl.BlockSpec((tr, tf), lambda i, j: (i, j + nf))

    def fwd_call(gu):
        def body(g_ref, u_ref, o_ref):
            gv = g_ref[...].astype(F32)
            o_ref[...] = (gv * _sigmoid(gv) * u_ref[...].astype(F32)).astype(o_ref.dtype)

        return pl.pallas_call(
            body, name=name + "_fwd", grid=(T // tr, nf), in_specs=[gs, us], out_specs=gs,
            out_shape=jax.ShapeDtypeStruct((T, Fh), BF16), compiler_params=_cp(),
        )(gu, gu)

    def bwd_call(gu, da):
        def body(g_ref, u_ref, da_ref, dgu_ref):
            j = pl.program_id(1)
            gv = g_ref[...].astype(F32)
            uv = u_ref[...].astype(F32)
            dav = da_ref[...].astype(F32)
            sg = _sigmoid(gv)
            d_gate = dav * uv * (sg * (1.0 + gv * (1.0 - sg)))
            d_up = dav * gv * sg
            dgu_ref[...] = jnp.where(j < nf, d_gate, d_up).astype(dgu_ref.dtype)

        gsel = pl.BlockSpec((tr, tf), lambda i, j: (i, j % nf))
        usel = pl.BlockSpec((tr, tf), lambda i, j: (i, j % nf + nf))
        return pl.pallas_call(
            body, name=name + "_bwd", grid=(T // tr, 2 * nf), in_specs=[gsel, usel, gsel],
            out_specs=pl.BlockSpec((tr, tf), lambda i, j: (i, j)),
            out_shape=jax.ShapeDtypeStruct((T, F2), BF16), compiler_params=_cp(),
        )(gu, gu, da)

    @jax.custom_vjp
    def f(gu):
        return fwd_call(gu)

    def fwd(gu):
        return fwd_call(gu), gu

    def bwd(gu, da):
        return (bwd_call(gu, da),)

    f.defvjp(fwd, bwd)
    return f(gu)


def gated_residual(x, gate, y, *, name):
    B, S, D = x.shape
    ts = _tile(S, 256, 8)
    row = pl.BlockSpec((1, ts, D), lambda b, j: (b, j, 0))
    per_b = pl.BlockSpec((1, 1, D), lambda b, j: (b, 0, 0))

    def fwd_call(x, gate, y):
        def body(x_ref, gt_ref, y_ref, o_ref):
            o_ref[0] = x_ref[0] + gt_ref[0] * y_ref[0]

        return pl.pallas_call(
            body, name=name + "_fwd", grid=(B, S // ts), in_specs=[row, per_b, row], out_specs=row,
            out_shape=jax.ShapeDtypeStruct((B, S, D), F32), compiler_params=_cp(),
        )(x, gate, y)

    def bwd_call(gate, y, g):
        def body(gt_ref, y_ref, g_ref, dy_ref, dgt_ref):
            @pl.when(pl.program_id(1) == 0)
            def _():
                dgt_ref[...] = jnp.zeros_like(dgt_ref)

            gv = g_ref[0]
            dy_ref[0] = gt_ref[0] * gv
            dgt_ref[0] += jnp.sum(gv * y_ref[0], axis=0, keepdims=True)

        return pl.pallas_call(
            body, name=name + "_bwd", grid=(B, S // ts), in_specs=[per_b, row, row],
            out_specs=[row, per_b],
            out_shape=[jax.ShapeDtypeStruct((B, S, D), F32), jax.ShapeDtypeStruct((B, 1, D), F32)],
            compiler_params=_cp(),
        )(gate, y, g)

    @jax.custom_vjp
    def f(x, gate, y):
        return fwd_call(x, gate, y)

    def fwd(x, gate, y):
        return fwd_call(x, gate, y), (gate, y)

    def bwd(res, g):
        dy, dgate = bwd_call(res[0], res[1], g)
        return g, dgate, dy

    f.defvjp(fwd, bwd)
    return f(x, gate, y)


def final_loss(x, g, target, *, name):
    T, D = x.shape
    tr = _tile(T, 256, 8)
    row = pl.BlockSpec((tr, D), lambda i: (i, 0))
    vec = pl.BlockSpec((1, D), lambda i: (0, 0))

    def fwd_call(x, g, target):
        def body(x_ref, g_ref, t_ref, o_ref):
            @pl.when(pl.program_id(0) == 0)
            def _():
                o_ref[...] = jnp.zeros_like(o_ref)

            xv = x_ref[...]
            r = lax.rsqrt(jnp.mean(xv * xv, axis=-1, keepdims=True) + EPS)
            e = xv * r * g_ref[...] - t_ref[...]
            o_ref[...] += jnp.sum(e * e, axis=0, keepdims=True)

        part = pl.pallas_call(
            body, name=name + "_fwd", grid=(T // tr,), in_specs=[row, vec, row], out_specs=vec,
            out_shape=jax.ShapeDtypeStruct((1, D), F32), compiler_params=_cp(),
        )(x, g, target)
        return (0.5 / D) * jnp.sum(part)

    def bwd_call(x, g, target, ct):
        def body(x_ref, g_ref, t_ref, ct_ref, dx_ref, dg_ref):
            @pl.when(pl.program_id(0) == 0)
            def _():
                dg_ref[...] = jnp.zeros_like(dg_ref)

            xv = x_ref[...]
            gv = g_ref[...]
            r = lax.rsqrt(jnp.mean(xv * xv, axis=-1, keepdims=True) + EPS)
            xn = xv * r
            dy = (xn * gv - t_ref[...]) * (ct_ref[...] * (1.0 / D))
            dxn = dy * gv
            dx_ref[...] = r * (dxn - xn * jnp.mean(dxn * xn, axis=-1, keepdims=True))
            dg_ref[...] += jnp.sum(dy * xn, axis=0, keepdims=True)

        return pl.pallas_call(
            body, name=name + "_bwd", grid=(T // tr,),
            in_specs=[row, vec, row, pl.BlockSpec((1, 1), lambda i: (0, 0))], out_specs=[row, vec],
            out_shape=[jax.ShapeDtypeStruct((T, D), F32), jax.ShapeDtypeStruct((1, D), F32)],
            compiler_params=_cp(),
        )(x, g, target, ct)

    @jax.custom_vjp
    def f(x, g, target):
        return fwd_call(x, g, target)

    def fwd(x, g, target):
        return fwd_call(x, g, target), (x, g, target)

    def bwd(res, ct):
        x, g, target = res
        dx, dg = bwd_call(x, g, target, jnp.reshape(ct, (1, 1)).astype(F32))
        return dx, dg, jnp.zeros_like(target)

    f.defvjp(fwd, bwd)
    return f(x, g, target)


def adamw(w, g, m, v, *, name):
    R, C = w.shape
    tr = _tile(R, 512, 8)
    spec = pl.BlockSpec((tr, C), lambda i: (i, 0))
    c1 = 1.0 / (1.0 - ADAM_B1 ** ADAM_STEP)
    c2 = 1.0 / (1.0 - ADAM_B2 ** ADAM_STEP)

    def body(w_ref, g_ref, m_ref, v_ref, d_ref, nm_ref, nv_ref):
        gv = g_ref[...]
        nm = ADAM_B1 * m_ref[...] + (1.0 - ADAM_B1) * gv
        nv = ADAM_B2 * v_ref[...] + (1.0 - ADAM_B2) * (gv * gv)
        nm_ref[...] = nm
        nv_ref[...] = nv
        d_ref[...] = -ADAM_LR * ((nm * c1) / (jnp.sqrt(nv * c2) + ADAM_EPS) + ADAM_WD * w_ref[...])

    o = jax.ShapeDtypeStruct((R, C), F32)
    return pl.pallas_call(
        body, name=name, grid=(R // tr,), in_specs=[spec] * 4, out_specs=[spec] * 3,
        out_shape=[o, o, o], compiler_params=_cp(),
    )(w, g, m, v)


def _position():
    x, y, c = lax.axis_index("x"), lax.axis_index("y"), lax.axis_index("c")
    return x, y, c


def all_gather(shard, *, name, cols=False):
    R, C = shard.shape
    assert not cols or C % LANES == 0

    def body(x_ref, out_ref, send_sems, recv_sems, local_sem):
        x, y, c = _position()
        me, sibling = (x, y, c), (x, y, 1 - c)
        chips = [(1 - x, y), (x, 1 - y), (1 - x, 1 - y)]

        def block(px, py, pc):
            idx = 4 * px + 2 * py + pc
            if cols:
                return out_ref.at[:, pl.ds(pl.multiple_of(idx * C, LANES), C)]
            return out_ref.at[idx]

        def copy(k, blk, to, src=None):
            return pltpu.make_async_remote_copy(
                src_ref=block(*blk) if src is None else src, dst_ref=block(*blk),
                send_sem=send_sems.at[k], recv_sem=recv_sems.at[k],
                device_id=to, device_id_type=pl.DeviceIdType.MESH)

        mine = pltpu.make_async_copy(x_ref, block(*me), local_sem)
        mine.start()
        first = [copy(0, me, sibling, src=x_ref)]
        first += [copy(1 + j, me, (*chip, c), src=x_ref) for j, chip in enumerate(chips)]
        for cp in first:
            cp.start()
        passed = [copy(4 + j, (*chip, c), sibling) for j, chip in enumerate(chips)]
        for j, chip in enumerate(chips):
            copy(1 + j, (*chip, c), me).wait_recv()
            passed[j].start()
        copy(0, sibling, me).wait_recv()
        for j, chip in enumerate(chips):
            copy(4 + j, (*chip, 1 - c), me).wait_recv()
        for cp in first + passed:
            cp.wait_send()
        mine.wait()

    return pl.pallas_call(
        body, name=name,
        out_shape=jax.ShapeDtypeStruct((R, N_DEV * C) if cols else (N_DEV, R, C), shard.dtype),
        in_specs=[pl.BlockSpec(memory_space=pl.ANY)], out_specs=pl.BlockSpec(memory_space=pl.ANY),
        scratch_shapes=[pltpu.SemaphoreType.DMA((7,)), pltpu.SemaphoreType.DMA((7,)),
                        pltpu.SemaphoreType.DMA],
    )(shard)


def exchange(stack, *, name, cols=False):
    if cols:
        R, C = stack.shape[0], stack.shape[1] // N_DEV
        assert C % LANES == 0
    else:
        _, R, C = stack.shape

    def src_block(x_ref, idx):
        if cols:
            return x_ref.at[:, pl.ds(pl.multiple_of(idx * C, LANES), C)]
        return x_ref.at[idx]

    def body(x_ref, out_ref, send_sems, recv_sems, local_sem):
        x, y, c = _position()
        me = 4 * x + 2 * y + c
        mine = pltpu.make_async_copy(src_block(x_ref, me), out_ref.at[me], local_sem)
        mine.start()
        copies = []
        for r in range(1, N_DEV):
            px = (1 - x) if r & 4 else x
            py = (1 - y) if r & 2 else y
            pc = (1 - c) if r & 1 else c
            peer = 4 * px + 2 * py + pc
            copies.append(pltpu.make_async_remote_copy(
                src_ref=src_block(x_ref, peer), dst_ref=out_ref.at[me],
                send_sem=send_sems.at[r - 1], recv_sem=recv_sems.at[r - 1],
                device_id=(px, py, pc), device_id_type=pl.DeviceIdType.MESH))
        for cp in copies:
            cp.start()
        for cp in copies:
            cp.wait_recv()
        for cp in copies:
            cp.wait_send()
        mine.wait()

    return pl.pallas_call(
        body, name=name, out_shape=jax.ShapeDtypeStruct((N_DEV, R, C), stack.dtype),
        in_specs=[pl.BlockSpec(memory_space=pl.ANY)], out_specs=pl.BlockSpec(memory_space=pl.ANY),
        scratch_shapes=[pltpu.SemaphoreType.DMA((7,)), pltpu.SemaphoreType.DMA((7,)),
                        pltpu.SemaphoreType.DMA],
    )(stack)


def sum_blocks(stack, *, name):
    n, R, C = stack.shape
    tr = _tile(R, 256, 8)

    def body(x_ref, o_ref):
        acc = x_ref[0].astype(F32)
        for i in range(1, n):
            acc = acc + x_ref[i].astype(F32)
        o_ref[...] = acc

    return pl.pallas_call(
        body, name=name, grid=(R // tr,),
        in_specs=[pl.BlockSpec((n, tr, C), lambda i: (0, i, 0))],
        out_specs=pl.BlockSpec((tr, C), lambda i: (i, 0)),
        out_shape=jax.ShapeDtypeStruct((R, C), F32), compiler_params=_cp(),
    )(stack)


PACK_COLS = 1024
PACK_ROW_MULT = 8


def _pack(arrays, dtype):
    flat = jnp.concatenate([a.reshape(-1).astype(dtype) for a in arrays])
    n = flat.shape[0]
    unit = PACK_COLS * PACK_ROW_MULT
    padded = -(-n // unit) * unit
    return jnp.pad(flat, (0, padded - n)).reshape(padded // PACK_COLS, PACK_COLS)


def _unpack(packed, shapes):
    flat = packed.reshape(-1)
    out, o = [], 0
    for s in shapes:
        n = int(np.prod(s))
        out.append(flat[o:o + n].reshape(s))
        o += n
    return out


def fsdp_cols(shard, *, name):
    K, n = shard.shape
    npad = -(-n // LANES) * LANES

    @jax.custom_vjp
    def f(p):
        p = jnp.pad(p, ((0, 0), (0, npad - n))) if npad != n else p
        return all_gather(p.astype(BF16), cols=True, name=name + "_ag")

    def fwd(p):
        return f(p), None

    def bwd(_, g):
        d = sum_blocks(exchange(g, cols=True, name=name + "_a2a"), name=name + "_sum")
        return (d[:, :n] if npad != n else d,)

    f.defvjp(fwd, bwd)
    return f(shard)


def fsdp_rows(shard, *, name):
    k, N = shard.shape

    @jax.custom_vjp
    def f(p):
        return all_gather(p.astype(BF16), name=name + "_ag").reshape(N_DEV * k, N)

    def fwd(p):
        return f(p), None

    def bwd(_, g):
        return (sum_blocks(exchange(g.reshape(N_DEV, k, N), name=name + "_a2a"), name=name + "_sum"),)

    f.defvjp(fwd, bwd)
    return f(shard)


def _unpad_cols(w, n):
    K = w.shape[0]
    npad = w.shape[1] // N_DEV
    if npad == n:
        return w
    return w.reshape(K, N_DEV, npad)[:, :, :n].reshape(K, N_DEV * n)


def gather_rows(part, me, *, name):
    rows, n = part.shape
    per = rows // N_DEV

    @jax.custom_vjp
    def f(part):
        full = all_gather(part, name=name + "_fwd")
        mine = lax.dynamic_slice_in_dim(full, me * per, per, axis=1)
        return jnp.swapaxes(mine, 0, 1).reshape(per, N_DEV * n)

    def fwd(part):
        return f(part), None

    def bwd(_, g):
        full = all_gather(g, name=name + "_bwd")
        mine = lax.dynamic_slice_in_dim(full, me * n, n, axis=2)
        return (mine.reshape(rows, n),)

    f.defvjp(fwd, bwd)
    return f(part)


def _seg_layout():
    offs = np.concatenate([[0], np.cumsum(IN_SPLITS)])
    cols, widths = [], []
    for s in SEG_ORDER:
        cols.append((int(offs[s]), int(offs[s + 1])))
        widths.append(SEG_PAD.get(s, IN_SPLITS[s]))
    return cols, widths


def _arrange_w_in(w):
    cols, widths = _seg_layout()
    parts = []
    for (a, b), wd in zip(cols, widths):
        seg = w[:, a:b]
        if wd != b - a:
            seg = jnp.pad(seg, ((0, 0), (0, wd - (b - a))))
        parts.append(seg)
    parts.append(jnp.zeros((w.shape[0], IN_WIDTH - sum(widths)), w.dtype))
    return jnp.concatenate(parts, axis=1)


def split_cols(proj, widths):
    @jax.custom_vjp
    def f(p):
        outs, o = [], 0
        for wd in widths:
            outs.append(p[:, o:o + wd])
            o += wd
        return tuple(outs)

    def fwd(p):
        return f(p), None

    def bwd(_, gs):
        rest = proj.shape[1] - sum(widths)
        tail = [jnp.zeros((proj.shape[0], rest), proj.dtype)] if rest else []
        return (jnp.concatenate(list(gs) + tail, axis=1),)

    f.defvjp(fwd, bwd)
    return f(proj)


BIG = ("w_in", "w_uq", "w_ukv", "conv_w", "w_out", "w_gate_up", "w_down")
SMALL = ("b_ada", "norm1_g", "norm2_g", "q_norm_g", "k_norm_g", "mla_q_norm_g", "mla_kv_norm_g",
         "conv_b", "dt_bias", "a_log", "d_skip", "ssd_norm_g", "final_norm_g")
WEIGHTS = ("w_ada", "b_ada", "norm1_g", "norm2_g", "w_in", "q_norm_g", "k_norm_g", "mla_q_norm_g",
           "w_uq", "mla_kv_norm_g", "w_ukv", "conv_w", "conv_b", "dt_bias", "a_log", "d_skip",
           "ssd_norm_g", "w_out", "w_gate_up", "w_down", "final_norm_g")


def _layer(l, x, mod, W, P, tabs):
    B, S, D = x.shape
    T = B * S
    nm = f"l{l}_"
    shift1, scale1, gate1, shift2, scale2, gate2 = [m[:, None, :] for m in jnp.split(mod, 6, axis=-1)]
    cos_a, sin_a, cos_b, sin_b, cos_k, sin_k = tabs

    h = rmsmod(x, P["norm1_g"][l][None], scale1, shift1, name=nm + "norm1")
    w_in = _arrange_w_in(W["w_in"])
    proj = linear(h.reshape(T, D), w_in, out_dtype=F32, name=nm + "in")
    q_a, k_a, v_a, cq, ckv, z, xbc, kpe, dtr = split_cols(proj, _seg_layout()[1])

    qn = group_rmsnorm(q_a, jnp.tile(P["q_norm_g"][l], GQA_H)[None], gs=HEAD, out_dtype=F32, name=nm + "qnorm")
    kn = group_rmsnorm(k_a, jnp.tile(P["k_norm_g"][l], GQA_KV)[None], gs=HEAD, out_dtype=F32, name=nm + "knorm")
    qr = rope(qn.reshape(B, S, -1), cos_a[:, :GQA_H * HEAD], sin_a[:, :GQA_H * HEAD], d=HEAD // 4, name=nm + "qrope")
    kr = rope(kn.reshape(B, S, -1), cos_a[:, :GQA_KV * HEAD], sin_a[:, :GQA_KV * HEAD], d=HEAD // 4, name=nm + "krope")
    o_a = attention(qr, kr, v_a.reshape(B, S, -1), H=GQA_H, Hkv=GQA_KV, dk=HEAD, dv=HEAD,
                    scale=HEAD ** -0.5, name=nm + "gqa")

    w_uq = W["w_uq"].reshape(MLA_QL, MLA_H, MLA_NOPE + MLA_ROPE)
    w_uq = jnp.concatenate([w_uq[:, :, :MLA_NOPE].reshape(MLA_QL, -1), w_uq[:, :, MLA_NOPE:].reshape(MLA_QL, -1)], axis=1)
    w_ukv = W["w_ukv"].reshape(MLA_KVL, MLA_H, MLA_NOPE + MLA_V)
    w_ukv = jnp.concatenate([w_ukv[:, :, :MLA_NOPE].reshape(MLA_KVL, -1), w_ukv[:, :, MLA_NOPE:].reshape(MLA_KVL, -1)], axis=1)
    cqn = group_rmsnorm(cq, P["mla_q_norm_g"][l][None], gs=MLA_QL, out_dtype=BF16, name=nm + "cqnorm")
    ckvn = group_rmsnorm(ckv, P["mla_kv_norm_g"][l][None], gs=MLA_KVL, out_dtype=BF16, name=nm + "ckvnorm")
    qb = linear(cqn, w_uq, out_dtype=F32, name=nm + "uq")
    kvb = linear(ckvn, w_ukv, out_dtype=F32, name=nm + "ukv")
    q_nope, q_pe = split_cols(qb, (MLA_H * MLA_NOPE, MLA_H * MLA_ROPE))
    k_nope, v_b = split_cols(kvb, (MLA_H * MLA_NOPE, MLA_H * MLA_V))
    q_pe = rope(q_pe.reshape(B, S, -1), cos_b, sin_b, d=MLA_ROPE // 4, name=nm + "qpe_rope")
    k_pe = rope(kpe.reshape(B, S, -1), cos_k, sin_k, d=MLA_ROPE // 4, name=nm + "kpe_rope")[:, :, :MLA_ROPE]
    zpad = jnp.zeros((B, S, MLA_H, MLA_DK - MLA_NOPE - MLA_ROPE), F32)
    q_cat = jnp.concatenate([q_nope.reshape(B, S, MLA_H, MLA_NOPE), q_pe.reshape(B, S, MLA_H, MLA_ROPE), zpad],
                            axis=-1).reshape(B, S, MLA_H * MLA_DK)
    k_cat = jnp.concatenate([k_nope.reshape(B, S, MLA_H, MLA_NOPE),
                             jnp.broadcast_to(k_pe[:, :, None, :], (B, S, MLA_H, MLA_ROPE)), zpad],
                            axis=-1).reshape(B, S, MLA_H * MLA_DK)
    o_b = attention(q_cat, k_cat, v_b.reshape(B, S, -1), H=MLA_H, Hkv=MLA_H, dk=MLA_DK, dv=MLA_V,
                    scale=(MLA_NOPE + MLA_ROPE) ** -0.5, name=nm + "mla")

    xact = conv_silu(xbc.reshape(B, S, -1), W["conv_w"].astype(F32), P["conv_b"][l][None], name=nm + "conv")
    brow = jnp.pad(P["dt_bias"][l].reshape(1, -1), ((0, 0), (0, LANES - 2 * SSD_H)))
    arow = jnp.pad(-jnp.exp(P["a_log"][l].reshape(1, -1)), ((0, 0), (0, LANES - 2 * SSD_H)))
    raw = dtr.reshape(B, S, LANES)
    y_f = ssd_scan(xact, raw, brow, arow, rev=False, name=nm + "ssd_f")
    y_b = ssd_scan(xact, raw, brow, arow, rev=True, name=nm + "ssd_b")
    dsk = jnp.repeat(P["d_skip"][l], SSD_P)[None]
    o_c = ssd_out(y_f.reshape(T, -1), y_b.reshape(T, -1), xact.reshape(T, -1), z, dsk,
                  P["ssd_norm_g"][l][None], name=nm + "ssd_out")

    o = jnp.concatenate([o_a.reshape(T, -1).astype(BF16), o_b.reshape(T, -1).astype(BF16), o_c], axis=-1)
    mix = linear(o, W["w_out"], out_dtype=F32, name=nm + "out")
    x = gated_residual(x, gate1, mix.reshape(B, S, D), name=nm + "res1")

    h = rmsmod(x, P["norm2_g"][l][None], scale2, shift2, name=nm + "norm2")
    gu = linear(h.reshape(T, D), W["w_gate_up"], out_dtype=BF16, name=nm + "gate_up")
    act = swiglu(gu, name=nm + "swiglu")
    ffn = linear(act, W["w_down"], out_dtype=F32, name=nm + "down")
    return gated_residual(x, gate2, ffn.reshape(B, S, D), name=nm + "res2")


def kernel(x, c, w_ada, b_ada, norm1_g, norm2_g, w_in, q_norm_g, k_norm_g, mla_q_norm_g, w_uq, mla_kv_norm_g, w_ukv, conv_w, conv_b, dt_bias, a_log, d_skip, ssd_norm_g, w_out, w_gate_up, w_down, final_norm_g, loss_target, m_w_ada, m_b_ada, m_norm1_g, m_norm2_g, m_w_in, m_q_norm_g, m_k_norm_g, m_mla_q_norm_g, m_w_uq, m_mla_kv_norm_g, m_w_ukv, m_conv_w, m_conv_b, m_dt_bias, m_a_log, m_d_skip, m_ssd_norm_g, m_w_out, m_w_gate_up, m_w_down, m_final_norm_g, v_w_ada, v_b_ada, v_norm1_g, v_norm2_g, v_w_in, v_q_norm_g, v_k_norm_g, v_mla_q_norm_g, v_w_uq, v_mla_kv_norm_g, v_w_ukv, v_conv_w, v_conv_b, v_dt_bias, v_a_log, v_d_skip, v_ssd_norm_g, v_w_out, v_w_gate_up, v_w_down, v_final_norm_g):
    args = dict(locals())
    weights = {n: args[n] for n in WEIGHTS}
    moments_m = {n: args["m_" + n] for n in WEIGHTS}
    moments_v = {n: args["v_" + n] for n in WEIGHTS}
    B, S, D = x.shape
    L = w_ada.shape[0]
    T = B * S
    px, py, pc = _position()
    me = 4 * px + 2 * py + pc
    small_shapes = [weights[n].shape for n in SMALL]

    tabs = (*rope_tables(S, HEAD, GQA_H * HEAD), *rope_tables(S, MLA_ROPE, MLA_H * MLA_ROPE),
            *rope_tables(S, MLA_ROPE, 2 * MLA_ROPE))
    c_all = all_gather(c, name="gather_c").reshape(N_DEV * B, D)

    def local_loss(big, w_ada_s, small, x):
        P = dict(zip(SMALL, small))
        for l in range(L):
            W = {}
            for n in ("w_in", "w_uq", "w_ukv", "w_gate_up"):
                W[n] = _unpad_cols(fsdp_cols(big[n][l], name=f"l{l}_{n}"), big[n].shape[2])
            W["conv_w"] = _unpad_cols(fsdp_cols(big["conv_w"][l], name=f"l{l}_conv_w"), big["conv_w"].shape[2])
            for n in ("w_out", "w_down"):
                W[n] = fsdp_rows(big[n][l], name=f"l{l}_{n}")
            part = linear(c_all, w_ada_s[l], out_dtype=F32, a_silu=True, name=f"l{l}_ada")
            mod = gather_rows(part, me, name=f"l{l}_mod") + P["b_ada"][l][None]
            x = _layer(l, x, mod, W, P, tabs)
        return final_loss(x.reshape(T, D), P["final_norm_g"][None], loss_target.reshape(T, D), name="loss")

    big = {n: weights[n] for n in BIG}
    small = tuple(weights[n] for n in SMALL)
    loss, (g_big, g_ada, g_small, grad_x) = jax.value_and_grad(local_loss, argnums=(0, 1, 2, 3))(
        big, w_ada, small, x)
    loss = lax.psum(loss, ("x", "y", "c"))

    grads = dict(g_big)
    grads["w_ada"] = g_ada
    g_small_sum = sum_blocks(all_gather(_pack(g_small, F32), name="small_grads_ag"), name="small_grads_sum")
    grads.update(zip(SMALL, _unpack(g_small_sum, small_shapes)))

    delta, new_m, new_v = {}, {}, {}
    for n in ("w_ada",) + BIG:
        shp = weights[n].shape
        two_d = (int(np.prod(shp[:-1])), shp[-1])
        d_, m_, v_ = adamw(weights[n].reshape(two_d), grads[n].reshape(two_d), moments_m[n].reshape(two_d),
                           moments_v[n].reshape(two_d), name="adamw_" + n)
        delta[n], new_m[n], new_v[n] = d_.reshape(shp), m_.reshape(shp), v_.reshape(shp)
    d_, m_, v_ = adamw(_pack([weights[n] for n in SMALL], F32), g_small_sum,
                       _pack([moments_m[n] for n in SMALL], F32), _pack([moments_v[n] for n in SMALL], F32),
                       name="adamw_small")
    for tgt, packed in ((delta, d_), (new_m, m_), (new_v, v_)):
        tgt.update(zip(SMALL, _unpack(packed, small_shapes)))

    return (loss, grad_x, *[grads[n] for n in WEIGHTS], *[delta[n] for n in WEIGHTS],
            *[new_m[n] for n in WEIGHTS], *[new_v[n] for n in WEIGHTS])
```

```python
import functools
import math

import jax
import jax.numpy as jnp
import numpy as np
from jax import lax
from jax.experimental import pallas as pl
from jax.experimental.pallas import tpu as pltpu

F32 = jnp.float32
BF16 = jnp.bfloat16
N_DEV = 8
EPS = 1e-6
ROPE_THETA = 10000.0
GRID_W = 64

GQA_H, GQA_KV, HEAD = 6, 2, 128
MLA_H, MLA_QL, MLA_KVL, MLA_NOPE, MLA_ROPE, MLA_V = 4, 512, 256, 128, 64, 128
MLA_DK = 256
SSD_H, SSD_P, SSD_G, SSD_N, SSD_K, CHUNK = 12, 64, 2, 128, 5, 128
SSD_INNER = SSD_H * SSD_P
SSD_CONV_DIM = SSD_INNER + 2 * SSD_G * SSD_N
N_PAIR = SSD_H // 2
LANES = 128
IN_SPLITS = (768, 256, 256, 512, 256, 64, 768, 1280, 24)
IN_COLS = sum(IN_SPLITS)
SEG_ORDER = (0, 1, 2, 3, 4, 6, 7, 5, 8)
SEG_PAD = {5: 128, 8: 128}
IN_WIDTH = 4608

ADAM_LR, ADAM_B1, ADAM_B2, ADAM_EPS, ADAM_WD, ADAM_STEP = 0.001, 0.9, 0.999, 1e-08, 0.01, 10
VMEM_LIMIT = 56 * 1024 * 1024
MM_TM, MM_TN, MM_TK = 1024, 1024, 512


def _cp(**kw):
    return pltpu.CompilerParams(vmem_limit_bytes=VMEM_LIMIT, **kw)


def _tile(dim, cap, mult=128):
    if dim <= cap:
        return dim
    best = None
    t = mult
    while t <= cap:
        if dim % t == 0:
            best = t
        t += mult
    assert best is not None, (dim, cap)
    return best


def _sigmoid(x):
    return 1.0 / (1.0 + jnp.exp(-x))


def _dot(a, b, dims):
    return lax.dot_general(a, b, (dims, ((), ())), preferred_element_type=F32)


NN = ((1,), (0,))
NT = ((1,), (1,))
TN = ((0,), (0,))


def _dotf(a, b, dims=NN):
    return lax.dot_general(a, b, (dims, ((), ())), preferred_element_type=F32,
                           precision=lax.Precision.HIGHEST)


def _bf(x):
    return x.astype(BF16)


def mm(a, b, *, ta=False, tb=False, out_dtype=F32, a_silu=False, name):
    if ta:
        K, M = a.shape
    else:
        M, K = a.shape
    if tb:
        N, K2 = b.shape
    else:
        K2, N = b.shape
    assert K == K2, (a.shape, b.shape, ta, tb)
    tm, tn, tk = _tile(M, MM_TM), _tile(N, MM_TN), _tile(K, MM_TK)
    nk = K // tk
    dims = ((0 if ta else 1,), (1 if tb else 0,))

    def body(a_ref, b_ref, o_ref, acc_ref):
        k = pl.program_id(2)

        @pl.when(k == 0)
        def _():
            acc_ref[...] = jnp.zeros_like(acc_ref)

        av = a_ref[...]
        if a_silu:
            av = av.astype(F32)
            av = av * _sigmoid(av)
        acc_ref[...] += _dot(_bf(av), _bf(b_ref[...]), dims)

        @pl.when(k == nk - 1)
        def _():
            o_ref[...] = acc_ref[...].astype(o_ref.dtype)

    a_spec = (pl.BlockSpec((tk, tm), lambda i, j, k: (k, i)) if ta
              else pl.BlockSpec((tm, tk), lambda i, j, k: (i, k)))
    b_spec = (pl.BlockSpec((tn, tk), lambda i, j, k: (j, k)) if tb
              else pl.BlockSpec((tk, tn), lambda i, j, k: (k, j)))
    return pl.pallas_call(
        body, name=name, grid=(M // tm, N // tn, nk),
        in_specs=[a_spec, b_spec],
        out_specs=pl.BlockSpec((tm, tn), lambda i, j, k: (i, j)),
        out_shape=jax.ShapeDtypeStruct((M, N), out_dtype),
        scratch_shapes=[pltpu.VMEM((tm, tn), F32)],
        compiler_params=_cp(dimension_semantics=("parallel", "parallel", "arbitrary")),
    )(a, b)


def linear(a, w, *, out_dtype, name, a_silu=False):
    @jax.custom_vjp
    def f(a, w):
        return mm(a, w, out_dtype=out_dtype, a_silu=a_silu, name=name + "_fwd")

    def fwd(a, w):
        return f(a, w), (a, w)

    def bwd(res, g):
        a, w = res
        if a_silu:
            da = jnp.zeros_like(a)
        else:
            da = mm(g, w, tb=True, out_dtype=a.dtype, name=name + "_da")
        dw = mm(a, g, ta=True, out_dtype=w.dtype, a_silu=a_silu, name=name + "_dw")
        return da, dw

    f.defvjp(fwd, bwd)
    return f(a, w)


def rmsmod(x, g, scale, shift, *, name):
    B, S, D = x.shape
    ts = _tile(S, 256, 8)
    row = pl.BlockSpec((1, ts, D), lambda b, j: (b, j, 0))
    per_b = pl.BlockSpec((1, 1, D), lambda b, j: (b, 0, 0))
    gspec = pl.BlockSpec((1, D), lambda b, j: (0, 0))

    def fwd_call(x, g, scale, shift):
        def body(x_ref, g_ref, sc_ref, sh_ref, o_ref):
            xv = x_ref[0]
            r = lax.rsqrt(jnp.mean(xv * xv, axis=-1, keepdims=True) + EPS)
            y = xv * r * g_ref[...]
            o_ref[0] = (y * (1.0 + sc_ref[0]) + sh_ref[0]).astype(o_ref.dtype)

        return pl.pallas_call(
            body, name=name + "_fwd", grid=(B, S // ts),
            in_specs=[row, gspec, per_b, per_b], out_specs=row,
            out_shape=jax.ShapeDtypeStruct((B, S, D), BF16), compiler_params=_cp(),
        )(x, g, scale, shift)

    def bwd_call(x, g, scale, dh):
        def body(x_ref, g_ref, sc_ref, dh_ref, dx_ref, dg_ref, dsc_ref, dsh_ref):
            j = pl.program_id(1)
            xv = x_ref[0]
            dh = dh_ref[0].astype(F32)
            r = lax.rsqrt(jnp.mean(xv * xv, axis=-1, keepdims=True) + EPS)
            xn = xv * r
            gv = g_ref[...]
            dy = dh * (1.0 + sc_ref[0])
            dxn = dy * gv
            dx_ref[0] = r * (dxn - xn * jnp.mean(dxn * xn, axis=-1, keepdims=True))

            @pl.when(j == 0)
            def _():
                dg_ref[...] = jnp.zeros_like(dg_ref)
                dsc_ref[...] = jnp.zeros_like(dsc_ref)
                dsh_ref[...] = jnp.zeros_like(dsh_ref)

            dg_ref[0] += jnp.sum(dy * xn, axis=0, keepdims=True)
            dsc_ref[0] += jnp.sum(dh * xn * gv, axis=0, keepdims=True)
            dsh_ref[0] += jnp.sum(dh, axis=0, keepdims=True)

        vec = jax.ShapeDtypeStruct((B, 1, D), F32)
        return pl.pallas_call(
            body, name=name + "_bwd", grid=(B, S // ts),
            in_specs=[row, gspec, per_b, row], out_specs=[row, per_b, per_b, per_b],
            out_shape=[jax.ShapeDtypeStruct((B, S, D), F32), vec, vec, vec], compiler_params=_cp(),
        )(x, g, scale, dh)

    @jax.custom_vjp
    def f(x, g, scale, shift):
        return fwd_call(x, g, scale, shift)

    def fwd(x, g, scale, shift):
        return fwd_call(x, g, scale, shift), (x, g, scale)

    def bwd(res, dh):
        x, g, scale = res
        dx, dg, dsc, dsh = bwd_call(x, g, scale, dh)
        return dx, jnp.sum(dg, axis=0), dsc, dsh

    f.defvjp(fwd, bwd)
    return f(x, g, scale, shift)


def group_rmsnorm(x, g, *, gs, out_dtype, name):
    T, W = x.shape
    ng = W // gs
    tr = _tile(T, 512, 8)
    row = pl.BlockSpec((tr, W), lambda i: (i, 0))
    gspec = pl.BlockSpec((1, W), lambda i: (0, 0))

    def fwd_call(x, g):
        def body(x_ref, g_ref, o_ref):
            for i in range(ng):
                sl = slice(i * gs, (i + 1) * gs)
                xv = x_ref[:, sl]
                r = lax.rsqrt(jnp.mean(xv * xv, axis=-1, keepdims=True) + EPS)
                o_ref[:, sl] = (xv * r * g_ref[:, sl]).astype(o_ref.dtype)

        return pl.pallas_call(
            body, name=name + "_fwd", grid=(T // tr,), in_specs=[row, gspec], out_specs=row,
            out_shape=jax.ShapeDtypeStruct((T, W), out_dtype), compiler_params=_cp(),
        )(x, g)

    def bwd_call(x, g, dy):
        def body(x_ref, g_ref, dy_ref, dx_ref, dg_ref):
            @pl.when(pl.program_id(0) == 0)
            def _():
                dg_ref[...] = jnp.zeros_like(dg_ref)

            for i in range(ng):
                sl = slice(i * gs, (i + 1) * gs)
                xv = x_ref[:, sl]
                dyv = dy_ref[:, sl].astype(F32)
                r = lax.rsqrt(jnp.mean(xv * xv, axis=-1, keepdims=True) + EPS)
                xn = xv * r
                dxn = dyv * g_ref[:, sl]
                dx_ref[:, sl] = r * (dxn - xn * jnp.mean(dxn * xn, axis=-1, keepdims=True))
                dg_ref[:, sl] += jnp.sum(dyv * xn, axis=0, keepdims=True)

        return pl.pallas_call(
            body, name=name + "_bwd", grid=(T // tr,), in_specs=[row, gspec, row],
            out_specs=[row, gspec],
            out_shape=[jax.ShapeDtypeStruct((T, W), F32), jax.ShapeDtypeStruct((1, W), F32)],
            compiler_params=_cp(),
        )(x, g, dy)

    @jax.custom_vjp
    def f(x, g):
        return fwd_call(x, g)

    def fwd(x, g):
        return fwd_call(x, g), (x, g)

    def bwd(res, dy):
        return bwd_call(res[0], res[1], dy)

    f.defvjp(fwd, bwd)
    return f(x, g)


def rope_tables(seq_len, rot_dim, width):
    rows = seq_len // GRID_W
    row_idx = jnp.repeat(jnp.arange(rows), GRID_W).astype(F32)
    col_idx = jnp.tile(jnp.arange(GRID_W), rows).astype(F32)
    axis_dim = rot_dim // 2
    inv_freq = jnp.power(ROPE_THETA, -jnp.arange(0, axis_dim, 2, dtype=F32) / axis_dim)
    ang_r = row_idx[:, None] * inv_freq[None, :]
    ang_c = col_idx[:, None] * inv_freq[None, :]
    cos = jnp.concatenate([jnp.cos(ang_r), jnp.cos(ang_r), jnp.cos(ang_c), jnp.cos(ang_c)], axis=-1)
    sin = jnp.concatenate([-jnp.sin(ang_r), jnp.sin(ang_r), -jnp.sin(ang_c), jnp.sin(ang_c)], axis=-1)
    reps = width // rot_dim
    return jnp.tile(cos, (1, reps)), jnp.tile(sin, (1, reps))


def rope(x, cos, sin, *, d, name):
    B, S, W = x.shape
    ts = _tile(S, 512, 8)
    row = pl.BlockSpec((1, ts, W), lambda b, j: (b, j, 0))
    tab = pl.BlockSpec((ts, W), lambda b, j: (j, 0))

    def call(x, inverse, nm):
        def body(x_ref, c_ref, s_ref, o_ref):
            xv = x_ref[0]
            lane = lax.broadcasted_iota(jnp.int32, xv.shape, 1)
            first = (lane // d) % 2 == 0

            def swap(v):
                return jnp.where(first, pltpu.roll(v, W - d, 1), pltpu.roll(v, d, 1))

            if inverse:
                o_ref[0] = xv * c_ref[...] + swap(xv * s_ref[...])
            else:
                o_ref[0] = xv * c_ref[...] + swap(xv) * s_ref[...]

        return pl.pallas_call(
            body, name=nm, grid=(B, S // ts), in_specs=[row, tab, tab], out_specs=row,
            out_shape=jax.ShapeDtypeStruct((B, S, W), F32), compiler_params=_cp(),
        )(x, cos, sin)

    @jax.custom_vjp
    def f(x):
        return call(x, False, name + "_fwd")

    def fwd(x):
        return call(x, False, name + "_fwd"), None

    def bwd(_, g):
        return (call(g, True, name + "_bwd"),)

    f.defvjp(fwd, bwd)
    return f(x)


def attention(q, k, v, *, H, Hkv, dk, dv, scale, name):
    B, S, _ = q.shape
    rep = H // Hkv
    tq = _tile(S, 256, 8)
    tkb = _tile(S, 256, 8)

    def fwd_call(q, k, v):
        def body(q_ref, k_ref, v_ref, o_ref, lse_ref):
            s = _dot(_bf(q_ref[0]), _bf(k_ref[0]), NT) * scale
            m = jnp.max(s, axis=-1, keepdims=True)
            p = jnp.exp(s - m)
            l = jnp.sum(p, axis=-1, keepdims=True)
            o_ref[0] = _dot(_bf(p), _bf(v_ref[0]), NN) / l
            lse_ref[0, 0] = m + jnp.log(l)

        return pl.pallas_call(
            body, name=name + "_fwd", grid=(B, H, S // tq),
            in_specs=[pl.BlockSpec((1, tq, dk), lambda b, h, i: (b, i, h)),
                      pl.BlockSpec((1, S, dk), lambda b, h, i: (b, 0, h // rep)),
                      pl.BlockSpec((1, S, dv), lambda b, h, i: (b, 0, h // rep))],
            out_specs=[pl.BlockSpec((1, tq, dv), lambda b, h, i: (b, i, h)),
                       pl.BlockSpec((1, 1, tq, 1), lambda b, h, i: (b, h, i, 0))],
            out_shape=[jax.ShapeDtypeStruct((B, S, H * dv), F32),
                       jax.ShapeDtypeStruct((B, H, S, 1), F32)],
            compiler_params=_cp(),
        )(q, k, v)

    def dq_call(q, k, v, o, do, lse):
        def body(q_ref, k_ref, v_ref, o_ref, do_ref, lse_ref, dq_ref, delta_ref):
            kb = _bf(k_ref[0])
            s = _dot(_bf(q_ref[0]), kb, NT) * scale
            p = jnp.exp(s - lse_ref[0, 0])
            dov = do_ref[0]
            delta = jnp.sum(dov * o_ref[0], axis=-1, keepdims=True)
            dp = _dot(_bf(dov), _bf(v_ref[0]), NT)
            ds = p * (dp - delta)
            dq_ref[0] = _dot(_bf(ds), kb, NN) * scale
            delta_ref[0, 0] = delta

        qs = pl.BlockSpec((1, tq, dk), lambda b, h, i: (b, i, h))
        os_ = pl.BlockSpec((1, tq, dv), lambda b, h, i: (b, i, h))
        col = pl.BlockSpec((1, 1, tq, 1), lambda b, h, i: (b, h, i, 0))
        return pl.pallas_call(
            body, name=name + "_dq", grid=(B, H, S // tq),
            in_specs=[qs, pl.BlockSpec((1, S, dk), lambda b, h, i: (b, 0, h // rep)),
                      pl.BlockSpec((1, S, dv), lambda b, h, i: (b, 0, h // rep)), os_, os_, col],
            out_specs=[qs, col],
            out_shape=[jax.ShapeDtypeStruct(q.shape, F32), jax.ShapeDtypeStruct((B, H, S, 1), F32)],
            compiler_params=_cp(),
        )(q, k, v, o, do, lse)

    def dkv_call(q, k, v, do, lse, delta):
        def body(q_ref, k_ref, v_ref, do_ref, lse_ref, delta_ref, dk_ref, dv_ref):
            @pl.when(pl.program_id(3) == 0)
            def _():
                dk_ref[...] = jnp.zeros_like(dk_ref)
                dv_ref[...] = jnp.zeros_like(dv_ref)

            qb = _bf(q_ref[0])
            dob = _bf(do_ref[0])
            s = _dot(qb, _bf(k_ref[0]), NT) * scale
            p = jnp.exp(s - lse_ref[0, 0])
            dv_ref[0] += _dot(_bf(p), dob, TN)
            dp = _dot(dob, _bf(v_ref[0]), NT)
            ds = p * (dp - delta_ref[0, 0])
            dk_ref[0] += _dot(_bf(ds), qb, TN) * scale

        hq = lambda b, g, j, r: (b, 0, g * rep + r)
        colq = pl.BlockSpec((1, 1, S, 1), lambda b, g, j, r: (b, g * rep + r, 0, 0))
        ks = pl.BlockSpec((1, tkb, dk), lambda b, g, j, r: (b, j, g))
        vs = pl.BlockSpec((1, tkb, dv), lambda b, g, j, r: (b, j, g))
        return pl.pallas_call(
            body, name=name + "_dkv", grid=(B, Hkv, S // tkb, rep),
            in_specs=[pl.BlockSpec((1, S, dk), hq), ks, vs, pl.BlockSpec((1, S, dv), hq), colq, colq],
            out_specs=[ks, vs],
            out_shape=[jax.ShapeDtypeStruct(k.shape, F32), jax.ShapeDtypeStruct(v.shape, F32)],
            compiler_params=_cp(),
        )(q, k, v, do, lse, delta)

    @jax.custom_vjp
    def f(q, k, v):
        return fwd_call(q, k, v)[0]

    def fwd(q, k, v):
        o, lse = fwd_call(q, k, v)
        return o, (q, k, v, o, lse)

    def bwd(res, do):
        q, k, v, o, lse = res
        dq, delta = dq_call(q, k, v, o, do, lse)
        dk_, dv_ = dkv_call(q, k, v, do, lse, delta)
        return dq, dk_, dv_

    f.defvjp(fwd, bwd)
    return f(q, k, v)


def conv_silu(x, w, b, *, name):
    B, S, C = x.shape
    tc = _tile(C, 256)
    pad = SSD_K // 2
    xs = pl.BlockSpec((1, S, tc), lambda bi, j: (bi, 0, j))
    ws = pl.BlockSpec((SSD_K, tc), lambda bi, j: (0, j))
    bs = pl.BlockSpec((1, tc), lambda bi, j: (0, j))

    def shifted(v, off):
        if off == 0:
            return v
        t = lax.broadcasted_iota(jnp.int32, v.shape, 0)
        r = pltpu.roll(v, (-off) % S, 0)
        return jnp.where((t + off >= 0) & (t + off < S), r, 0.0)

    def pre_act(xv, wv, bv):
        u = jnp.zeros_like(xv) + bv
        for k in range(SSD_K):
            u = u + wv[k:k + 1, :] * shifted(xv, k - pad)
        return u

    def fwd_call(x, w, b):
        def body(x_ref, w_ref, b_ref, o_ref):
            u = pre_act(x_ref[0], w_ref[...], b_ref[...])
            o_ref[0] = u * _sigmoid(u)

        return pl.pallas_call(
            body, name=name + "_fwd", grid=(B, C // tc), in_specs=[xs, ws, bs], out_specs=xs,
            out_shape=jax.ShapeDtypeStruct((B, S, C), F32), compiler_params=_cp(),
        )(x, w, b)

    def bwd_call(x, w, b, dy):
        def body(x_ref, w_ref, b_ref, dy_ref, dx_ref, dw_ref):
            xv = x_ref[0]
            wv = w_ref[...]
            u = pre_act(xv, wv, b_ref[...])
            sg = _sigmoid(u)
            du = dy_ref[0] * (sg * (1.0 + u * (1.0 - sg)))
            dx = jnp.zeros_like(xv)
            for k in range(SSD_K):
                dx = dx + wv[k:k + 1, :] * shifted(du, pad - k)
                dw_ref[0, k:k + 1, :] = jnp.sum(du * shifted(xv, k - pad), axis=0, keepdims=True)
            dw_ref[0, SSD_K:SSD_K + 1, :] = jnp.sum(du, axis=0, keepdims=True)
            dw_ref[0, SSD_K + 1:8, :] = jnp.zeros((8 - SSD_K - 1, tc), F32)
            dx_ref[0] = dx

        return pl.pallas_call(
            body, name=name + "_bwd", grid=(B, C // tc), in_specs=[xs, ws, bs, xs],
            out_specs=[xs, pl.BlockSpec((1, 8, tc), lambda bi, j: (bi, 0, j))],
            out_shape=[jax.ShapeDtypeStruct((B, S, C), F32), jax.ShapeDtypeStruct((B, 8, C), F32)],
            compiler_params=_cp(),
        )(x, w, b, dy)

    @jax.custom_vjp
    def f(x, w, b):
        return fwd_call(x, w, b)

    def fwd(x, w, b):
        return fwd_call(x, w, b), (x, w, b)

    def bwd(res, dy):
        x, w, b = res
        dx, dwb = bwd_call(x, w, b, dy)
        dwb = jnp.sum(dwb, axis=0)
        return dx, dwb[:SSD_K], dwb[SSD_K:SSD_K + 1]

    f.defvjp(fwd, bwd)
    return f(x, w, b)


def _softplus(x):
    return jnp.maximum(x, 0.0) + jnp.log1p(jnp.exp(-jnp.abs(x)))


def _ssd_prep(raw, raw_t, brow, arow, bcol, acol, rev):
    li = lax.broadcasted_iota(jnp.int32, (CHUNK, CHUNK), 0)
    ki = lax.broadcasted_iota(jnp.int32, (CHUNK, CHUNK), 1)
    later = (li <= ki) if rev else (li >= ki)
    dt = _softplus(raw + brow)
    a = dt * arow
    cs = _dotf(later.astype(F32), a)
    tot = jnp.sum(a, axis=0, keepdims=True)
    a_t = _softplus(raw_t + bcol) * acol
    earlier = (li >= ki) if rev else (li <= ki)
    cs_t = _dotf(a_t, earlier.astype(F32))
    return dt, a, cs, tot, cs_t, later


def _lane_pick(mat, j):
    lane = lax.broadcasted_iota(jnp.int32, mat.shape, 1)
    return jnp.sum(jnp.where(lane == j, mat, 0.0), axis=1, keepdims=True)


def _head_sum(t, first):
    s0 = jnp.sum(jnp.where(first, t, 0.0), axis=1, keepdims=True)
    s1 = jnp.sum(jnp.where(first, 0.0, t), axis=1, keepdims=True)
    return s0, s1


def ssd_scan(xbc, raw, brow, arow, *, rev, name):
    B, S, _ = xbc.shape
    NC = S // CHUNK
    off = SSD_H if rev else 0
    n_dt = 2 * SSD_H

    def chunk_of(c):
        return (NC - 1 - c) if rev else c

    def specs(cmap):
        return dict(
            x=pl.BlockSpec((1, CHUNK, SSD_INNER), lambda b, c: (b, cmap(c), 0)),
            bm=pl.BlockSpec((1, CHUNK, 2 * SSD_N), lambda b, c: (b, cmap(c), SSD_INNER // (2 * SSD_N))),
            cm=pl.BlockSpec((1, CHUNK, 2 * SSD_N), lambda b, c: (b, cmap(c), SSD_INNER // (2 * SSD_N) + 1)),
            raw=pl.BlockSpec((1, CHUNK, LANES), lambda b, c: (b, cmap(c), 0)),
            raw_t=pl.BlockSpec((1, n_dt, CHUNK), lambda b, c: (b, 0, cmap(c))),
            row=pl.BlockSpec((1, LANES), lambda b, c: (0, 0)),
            colv=pl.BlockSpec((n_dt, 1), lambda b, c: (0, 0)),
            hs=pl.BlockSpec((1, 1, N_PAIR, SSD_N, LANES), lambda b, c: (b, cmap(c), 0, 0, 0)),
        )

    def head_terms(prep, j, first_dummy=None):
        dt, a, cs, tot, cs_t, later = prep
        cs_c = _lane_pick(cs, j)
        cs_r = cs_t[j:j + 1, :]
        dt_c = _lane_pick(dt, j)
        tot_j = _lane_pick(tot, j)
        L = jnp.exp(jnp.where(later, cs_c - cs_r, -1e30))
        return cs_c, cs_r, dt_c, tot_j, L

    def fwd_call(xbc, raw, raw_t, brow, arow, bcol, acol):
        def body(x_ref, bm_ref, cm_ref, raw_ref, rawt_ref, brow_ref, arow_ref, bcol_ref, acol_ref,
                 y_ref, hs_ref, st_ref):
            @pl.when(pl.program_id(1) == 0)
            def _():
                st_ref[...] = jnp.zeros_like(st_ref)

            prep = _ssd_prep(raw_ref[0], rawt_ref[0], brow_ref[...], arow_ref[...],
                             bcol_ref[...], acol_ref[...], rev)
            lane = lax.broadcasted_iota(jnp.int32, (CHUNK, LANES), 1)
            first = lane < SSD_P
            for g in range(SSD_G):
                Bg = _bf(bm_ref[0, :, g * SSD_N:(g + 1) * SSD_N])
                Cg = _bf(cm_ref[0, :, g * SSD_N:(g + 1) * SSD_N])
                G = _dot(Cg, Bg, NT)
                for pp in range(N_PAIR // SSD_G):
                    pi = g * (N_PAIR // SSD_G) + pp
                    c0, _, d0, t0, L0 = head_terms(prep, off + 2 * pi)
                    c1, _, d1, t1, L1 = head_terms(prep, off + 2 * pi + 1)
                    xd = x_ref[0, :, pi * LANES:(pi + 1) * LANES] * jnp.where(first, d0, d1)
                    xdb = _bf(xd)
                    y = jnp.where(first, _dot(_bf(G * L0), xdb, NN), _dot(_bf(G * L1), xdb, NN))
                    dec = jnp.where(first, jnp.exp(t0 - c0), jnp.exp(t1 - c1))
                    h_prev = st_ref[pi]
                    hs_ref[0, 0, pi] = h_prev
                    y = y + _dot(Cg, _bf(h_prev), NN) * jnp.where(first, jnp.exp(c0), jnp.exp(c1))
                    y_ref[0, :, pi * LANES:(pi + 1) * LANES] = y
                    etot = jnp.where(first[:1], jnp.exp(t0), jnp.exp(t1))
                    st_ref[pi] = h_prev * etot + _dot(Bg, _bf(xd * dec), TN)

        sp = specs(chunk_of)
        return pl.pallas_call(
            body, name=name + "_fwd", grid=(B, NC),
            in_specs=[sp["x"], sp["bm"], sp["cm"], sp["raw"], sp["raw_t"], sp["row"], sp["row"],
                      sp["colv"], sp["colv"]],
            out_specs=[sp["x"], sp["hs"]],
            out_shape=[jax.ShapeDtypeStruct((B, S, SSD_INNER), F32),
                       jax.ShapeDtypeStruct((B, NC, N_PAIR, SSD_N, LANES), F32)],
            scratch_shapes=[pltpu.VMEM((N_PAIR, SSD_N, LANES), F32)],
            compiler_params=_cp(),
        )(xbc, xbc, xbc, raw, raw_t, brow, arow, bcol, acol)

    def bwd_call(xbc, raw, raw_t, brow, arow, bcol, acol, hs, dy):
        def body(x_ref, bm_ref, cm_ref, raw_ref, rawt_ref, brow_ref, arow_ref, bcol_ref, acol_ref,
                 hs_ref, dy_ref, dxbc_ref, draw_ref, da_ref, dst_ref):
            @pl.when(pl.program_id(1) == 0)
            def _():
                dst_ref[...] = jnp.zeros_like(dst_ref)
                da_ref[...] = jnp.zeros_like(da_ref)

            raw_v = raw_ref[0]
            prep = _ssd_prep(raw_v, rawt_ref[0], brow_ref[...], arow_ref[...],
                             bcol_ref[...], acol_ref[...], rev)
            dt, a, cs, tot, cs_t, later = prep
            li = lax.broadcasted_iota(jnp.int32, (CHUNK, CHUNK), 0)
            ki = lax.broadcasted_iota(jnp.int32, (CHUNK, CHUNK), 1)
            later_t = (li >= ki) if rev else (li <= ki)
            lane = lax.broadcasted_iota(jnp.int32, (CHUNK, LANES), 1)
            first = lane < SSD_P
            dcs_all = jnp.zeros((CHUNK, LANES), F32)
            ddt_all = jnp.zeros((CHUNK, LANES), F32)
            dtot_all = jnp.zeros((1, LANES), F32)
            for g in range(SSD_G):
                Bg = _bf(bm_ref[0, :, g * SSD_N:(g + 1) * SSD_N])
                Cg = _bf(cm_ref[0, :, g * SSD_N:(g + 1) * SSD_N])
                G = _dot(Cg, Bg, NT)
                Gt = _dot(Bg, Cg, NT)
                dG = jnp.zeros((CHUNK, CHUNK), F32)
                dB = jnp.zeros((CHUNK, SSD_N), F32)
                dC = jnp.zeros((CHUNK, SSD_N), F32)
                for pp in range(N_PAIR // SSD_G):
                    pi = g * (N_PAIR // SSD_G) + pp
                    j0, j1 = off + 2 * pi, off + 2 * pi + 1
                    c0, r0, d0, t0, L0 = head_terms(prep, j0)
                    c1, r1, d1, t1, L1 = head_terms(prep, j1)
                    Lt0 = jnp.exp(jnp.where(later_t, r0 - c0, -1e30))
                    Lt1 = jnp.exp(jnp.where(later_t, r1 - c1, -1e30))
                    xv = x_ref[0, :, pi * LANES:(pi + 1) * LANES]
                    dtp = jnp.where(first, d0, d1)
                    xd = xv * dtp
                    xdb = _bf(xd)
                    dyv = dy_ref[0, :, pi * LANES:(pi + 1) * LANES]
                    dyb = _bf(dyv)
                    dec = jnp.where(first, jnp.exp(t0 - c0), jnp.exp(t1 - c1))
                    ecs = jnp.where(first, jnp.exp(c0), jnp.exp(c1))
                    et0, et1 = jnp.exp(t0), jnp.exp(t1)
                    etot = jnp.where(first[:1], et0, et1)
                    h_prev = hs_ref[0, 0, pi]
                    hpb = _bf(h_prev)
                    dhn = dst_ref[pi]
                    dhb = _bf(dhn)
                    W0, W1 = G * L0, G * L1
                    Wt0, Wt1 = Gt * Lt0, Gt * Lt1
                    bdh = _dot(Bg, dhb, NN)
                    dxd = jnp.where(first, _dot(_bf(Wt0), dyb, NN), _dot(_bf(Wt1), dyb, NN)) + bdh * dec
                    dy0 = _bf(jnp.where(first, dyv, 0.0))
                    dy1 = _bf(jnp.where(first, 0.0, dyv))
                    Q0, Q1 = _dot(dy0, xdb, NT), _dot(dy1, xdb, NT)
                    Qt0, Qt1 = _dot(xdb, dy0, NT), _dot(xdb, dy1, NT)
                    dG = dG + Q0 * L0 + Q1 * L1
                    dcs0 = (jnp.sum(Q0 * W0, axis=1, keepdims=True)
                            - jnp.sum(Qt0 * Wt0, axis=1, keepdims=True))
                    dcs1 = (jnp.sum(Q1 * W1, axis=1, keepdims=True)
                            - jnp.sum(Qt1 * Wt1, axis=1, keepdims=True))
                    dye = dyv * ecs
                    dyeb = _bf(dye)
                    s0, s1 = _head_sum(dye * _dot(Cg, hpb, NN), first)
                    dcs0, dcs1 = dcs0 + s0, dcs1 + s1
                    dC = dC + _dot(dyeb, hpb, NT)
                    dB = dB + _dot(_bf(xd * dec), dhb, NT)
                    u0, u1 = _head_sum(xd * bdh * dec, first)
                    dcs0, dcs1 = dcs0 - u0, dcs1 - u1
                    w = jnp.sum(dhn * h_prev, axis=0, keepdims=True)
                    w0, w1 = _head_sum(w, first[:1])
                    dt0 = jnp.sum(u0, axis=0, keepdims=True) + et0 * w0
                    dt1 = jnp.sum(u1, axis=0, keepdims=True) + et1 * w1
                    dst_ref[pi] = _dot(Cg, dyeb, TN) + dhn * etot
                    q0, q1 = _head_sum(dxd * xv, first)
                    dxbc_ref[0, :, pi * LANES:(pi + 1) * LANES] = dxd * dtp
                    dcs_all = dcs_all + jnp.where(lane == j0, dcs0, 0.0) + jnp.where(lane == j1, dcs1, 0.0)
                    ddt_all = ddt_all + jnp.where(lane == j0, q0, 0.0) + jnp.where(lane == j1, q1, 0.0)
                    dtot_all = (dtot_all + jnp.where(lane[:1] == j0, dt0, 0.0)
                                + jnp.where(lane[:1] == j1, dt1, 0.0))
                dGb = _bf(dG)
                dC = dC + _dot(dGb, Bg, NN)
                dB = dB + _dot(dGb, Cg, TN)
                dxbc_ref[0, :, SSD_INNER + g * SSD_N:SSD_INNER + (g + 1) * SSD_N] = dB
                dxbc_ref[0, :, SSD_INNER + (SSD_G + g) * SSD_N:SSD_INNER + (SSD_G + g + 1) * SSD_N] = dC
            da = _dotf(later_t.astype(F32), dcs_all) + dtot_all
            ddt = ddt_all + da * arow_ref[...]
            da_ref[0] += jnp.sum(da * dt, axis=0, keepdims=True)
            draw_ref[0] = ddt * _sigmoid(raw_v + brow_ref[...])

        def rchunk(c):
            return c if rev else (NC - 1 - c)

        sp = specs(rchunk)
        full = pl.BlockSpec((1, CHUNK, SSD_CONV_DIM), lambda b, c: (b, rchunk(c), 0))
        return pl.pallas_call(
            body, name=name + "_bwd", grid=(B, NC),
            in_specs=[sp["x"], sp["bm"], sp["cm"], sp["raw"], sp["raw_t"], sp["row"], sp["row"],
                      sp["colv"], sp["colv"], sp["hs"], sp["x"]],
            out_specs=[full, sp["raw"], pl.BlockSpec((1, 1, LANES), lambda b, c: (b, 0, 0))],
            out_shape=[jax.ShapeDtypeStruct((B, S, SSD_CONV_DIM), F32),
                       jax.ShapeDtypeStruct((B, S, LANES), F32),
                       jax.ShapeDtypeStruct((B, 1, LANES), F32)],
            scratch_shapes=[pltpu.VMEM((N_PAIR, SSD_N, LANES), F32)],
            compiler_params=_cp(),
        )(xbc, xbc, xbc, raw, raw_t, brow, arow, bcol, acol, hs, dy)

    def aux(raw, brow, arow):
        raw_t = jnp.swapaxes(raw[:, :, :n_dt], 1, 2)
        return raw_t, brow[0, :n_dt][:, None], arow[0, :n_dt][:, None]

    @jax.custom_vjp
    def f(xbc, raw, brow, arow):
        raw_t, bcol, acol = aux(raw, brow, arow)
        return fwd_call(xbc, raw, raw_t, brow, arow, bcol, acol)[0]

    def fwd(xbc, raw, brow, arow):
        raw_t, bcol, acol = aux(raw, brow, arow)
        y, hs = fwd_call(xbc, raw, raw_t, brow, arow, bcol, acol)
        return y, (xbc, raw, brow, arow, hs)

    def bwd(res, dy):
        xbc, raw, brow, arow, hs = res
        raw_t, bcol, acol = aux(raw, brow, arow)
        dxbc, draw, da = bwd_call(xbc, raw, raw_t, brow, arow, bcol, acol, hs, dy)
        dbrow = jnp.sum(draw, axis=(0, 1))[None, :]
        return dxbc, draw, dbrow, jnp.sum(da, axis=0)

    f.defvjp(fwd, bwd)
    return f(xbc, raw, brow, arow)


def ssd_out(yf, yb, xbc, z, dsk, g, *, name):
    T, W = yf.shape
    gs = W // SSD_G
    tr = _tile(T, 512, 8)
    row = pl.BlockSpec((tr, W), lambda i: (i, 0))
    vec = pl.BlockSpec((1, W), lambda i: (0, 0))

    def normed(yv, gv):
        outs, rs = [], []
        for i in range(SSD_G):
            sl = slice(i * gs, (i + 1) * gs)
            r = lax.rsqrt(jnp.mean(yv[:, sl] * yv[:, sl], axis=-1, keepdims=True) + EPS)
            rs.append(r)
            outs.append(yv[:, sl] * r)
        return outs, rs

    def fwd_call(yf, yb, xbc, z, dsk, g):
        def body(yf_ref, yb_ref, xs_ref, z_ref, dsk_ref, g_ref, o_ref):
            zv = z_ref[...]
            yv = (yf_ref[...] + yb_ref[...] + xs_ref[...] * dsk_ref[...]) * (zv * _sigmoid(zv))
            outs, _ = normed(yv, g_ref[...])
            for i in range(SSD_G):
                sl = slice(i * gs, (i + 1) * gs)
                o_ref[:, sl] = (outs[i] * g_ref[:, sl]).astype(o_ref.dtype)

        return pl.pallas_call(
            body, name=name + "_fwd", grid=(T // tr,), in_specs=[row, row, row, row, vec, vec],
            out_specs=row, out_shape=jax.ShapeDtypeStruct((T, W), BF16), compiler_params=_cp(),
        )(yf, yb, xbc, z, dsk, g)

    def bwd_call(yf, yb, xbc, z, dsk, g, do):
        def body(yf_ref, yb_ref, xs_ref, z_ref, dsk_ref, g_ref, do_ref, dy_ref, dxs_ref, dz_ref,
                 ddsk_ref, dg_ref):
            @pl.when(pl.program_id(0) == 0)
            def _():
                ddsk_ref[...] = jnp.zeros_like(ddsk_ref)
                dg_ref[...] = jnp.zeros_like(dg_ref)

            zv = z_ref[...]
            sg = _sigmoid(zv)
            sz = zv * sg
            xs = xs_ref[...]
            pre = yf_ref[...] + yb_ref[...] + xs * dsk_ref[...]
            yv = pre * sz
            outs, rs = normed(yv, g_ref[...])
            for i in range(SSD_G):
                sl = slice(i * gs, (i + 1) * gs)
                dov = do_ref[:, sl].astype(F32)
                xn = outs[i]
                dxn = dov * g_ref[:, sl]
                dyv = rs[i] * (dxn - xn * jnp.mean(dxn * xn, axis=-1, keepdims=True))
                dg_ref[:, sl] += jnp.sum(dov * xn, axis=0, keepdims=True)
                dpre = dyv * sz[:, sl]
                dy_ref[:, sl] = dpre
                dxs_ref[:, sl] = dpre * dsk_ref[:, sl]
                ddsk_ref[:, sl] += jnp.sum(dpre * xs[:, sl], axis=0, keepdims=True)
                dz_ref[:, sl] = dyv * pre[:, sl] * (sg[:, sl] * (1.0 + zv[:, sl] * (1.0 - sg[:, sl])))

        o = jax.ShapeDtypeStruct((T, W), F32)
        v = jax.ShapeDtypeStruct((1, W), F32)
        return pl.pallas_call(
            body, name=name + "_bwd", grid=(T // tr,), in_specs=[row, row, row, row, vec, vec, row],
            out_specs=[row, row, row, vec, vec], out_shape=[o, o, o, v, v], compiler_params=_cp(),
        )(yf, yb, xbc, z, dsk, g, do)

    @jax.custom_vjp
    def f(yf, yb, xbc, z, dsk, g):
        return fwd_call(yf, yb, xbc, z, dsk, g)

    def fwd(yf, yb, xbc, z, dsk, g):
        return fwd_call(yf, yb, xbc, z, dsk, g), (yf, yb, xbc, z, dsk, g)

    def bwd(res, do):
        dy, dxs, dz, ddsk, dg = bwd_call(*res, do)
        dxbc = jnp.pad(dxs, ((0, 0), (0, res[2].shape[1] - W)))
        return dy, dy, dxbc, dz, ddsk, dg

    f.defvjp(fwd, bwd)
    return f(yf, yb, xbc, z, dsk, g)


def swiglu(gu, *, name):
    T, F2 = gu.shape
    Fh = F2 // 2
    tr, tf = _tile(T, 512, 8), _tile(Fh, 512)
    nf = Fh // tf
    gs = pl.BlockSpec((tr, tf), lambda i, j: (i, j))
    us = pl.BlockSpec((tr, tf), lambda i, j: (i, j + nf))

    def fwd_call(gu):
        def body(g_ref, u_ref, o_ref):
            gv = g_ref[...].astype(F32)
            o_ref[...] = (gv * _sigmoid(gv) * u_ref[...].astype(F32)).astype(o_ref.dtype)

        return pl.pallas_call(
            body, name=name + "_fwd", grid=(T // tr, nf), in_specs=[gs, us], out_specs=gs,
            out_shape=jax.ShapeDtypeStruct((T, Fh), BF16), compiler_params=_cp(),
        )(gu, gu)

    def bwd_call(gu, da):
        def body(g_ref, u_ref, da_ref, dgu_ref):
            j = pl.program_id(1)
            gv = g_ref[...].astype(F32)
            uv = u_ref[...].astype(F32)
            dav = da_ref[...].astype(F32)
            sg = _sigmoid(gv)

            @pl.when(j < nf)
            def _():
                dgu_ref[...] = (dav * uv * (sg * (1.0 + gv * (1.0 - sg)))).astype(dgu_ref.dtype)

            @pl.when(j >= nf)
            def _():
                dgu_ref[...] = (dav * gv * sg).astype(dgu_ref.dtype)

        gsel = pl.BlockSpec((tr, tf), lambda i, j: (i, j % nf))
        usel = pl.BlockSpec((tr, tf), lambda i, j: (i, j % nf + nf))
        return pl.pallas_call(
            body, name=name + "_bwd", grid=(T // tr, 2 * nf), in_specs=[gsel, usel, gsel],
            out_specs=pl.BlockSpec((tr, tf), lambda i, j: (i, j)),
            out_shape=jax.ShapeDtypeStruct((T, F2), BF16), compiler_params=_cp(),
        )(gu, gu, da)

    @jax.custom_vjp
    def f(gu):
        return fwd_call(gu)

    def fwd(gu):
        return fwd_call(gu), gu

    def bwd(gu, da):
        return (bwd_call(gu, da),)

    f.defvjp(fwd, bwd)
    return f(gu)


def gated_residual(x, gate, y, *, name):
    B, S, D = x.shape
    ts = _tile(S, 256, 8)
    row = pl.BlockSpec((1, ts, D), lambda b, j: (b, j, 0))
    per_b = pl.BlockSpec((1, 1, D), lambda b, j: (b, 0, 0))

    def fwd_call(x, gate, y):
        def body(x_ref, gt_ref, y_ref, o_ref):
            o_ref[0] = x_ref[0] + gt_ref[0] * y_ref[0]

        return pl.pallas_call(
            body, name=name + "_fwd", grid=(B, S // ts), in_specs=[row, per_b, row], out_specs=row,
            out_shape=jax.ShapeDtypeStruct((B, S, D), F32), compiler_params=_cp(),
        )(x, gate, y)

    def bwd_call(gate, y, g):
        def body(gt_ref, y_ref, g_ref, dy_ref, dgt_ref):
            @pl.when(pl.program_id(1) == 0)
            def _():
                dgt_ref[...] = jnp.zeros_like(dgt_ref)

            gv = g_ref[0]
            dy_ref[0] = gt_ref[0] * gv
            dgt_ref[0] += jnp.sum(gv * y_ref[0], axis=0, keepdims=True)

        return pl.pallas_call(
            body, name=name + "_bwd", grid=(B, S // ts), in_specs=[per_b, row, row],
            out_specs=[row, per_b],
            out_shape=[jax.ShapeDtypeStruct((B, S, D), F32), jax.ShapeDtypeStruct((B, 1, D), F32)],
            compiler_params=_cp(),
        )(gate, y, g)

    @jax.custom_vjp
    def f(x, gate, y):
        return fwd_call(x, gate, y)

    def fwd(x, gate, y):
        return fwd_call(x, gate, y), (gate, y)

    def bwd(res, g):
        dy, dgate = bwd_call(res[0], res[1], g)
        return g, dgate, dy

    f.defvjp(fwd, bwd)
    return f(x, gate, y)


def final_loss(x, g, target, *, name):
    T, D = x.shape
    tr = _tile(T, 256, 8)
    row = pl.BlockSpec((tr, D), lambda i: (i, 0))
    vec = pl.BlockSpec((1, D), lambda i: (0, 0))

    def fwd_call(x, g, target):
        def body(x_ref, g_ref, t_ref, o_ref):
            @pl.when(pl.program_id(0) == 0)
            def _():
                o_ref[...] = jnp.zeros_like(o_ref)

            xv = x_ref[...]
            r = lax.rsqrt(jnp.mean(xv * xv, axis=-1, keepdims=True) + EPS)
            e = xv * r * g_ref[...] - t_ref[...]
            o_ref[...] += jnp.sum(e * e, axis=0, keepdims=True)

        part = pl.pallas_call(
            body, name=name + "_fwd", grid=(T // tr,), in_specs=[row, vec, row], out_specs=vec,
            out_shape=jax.ShapeDtypeStruct((1, D), F32), compiler_params=_cp(),
        )(x, g, target)
        return (0.5 / D) * jnp.sum(part)

    def bwd_call(x, g, target, ct):
        def body(x_ref, g_ref, t_ref, ct_ref, dx_ref, dg_ref):
            @pl.when(pl.program_id(0) == 0)
            def _():
                dg_ref[...] = jnp.zeros_like(dg_ref)

            xv = x_ref[...]
            gv = g_ref[...]
            r = lax.rsqrt(jnp.mean(xv * xv, axis=-1, keepdims=True) + EPS)
            xn = xv * r
            dy = (xn * gv - t_ref[...]) * (ct_ref[...] * (1.0 / D))
            dxn = dy * gv
            dx_ref[...] = r * (dxn - xn * jnp.mean(dxn * xn, axis=-1, keepdims=True))
            dg_ref[...] += jnp.sum(dy * xn, axis=0, keepdims=True)

        return pl.pallas_call(
            body, name=name + "_bwd", grid=(T // tr,),
            in_specs=[row, vec, row, pl.BlockSpec((1, 1), lambda i: (0, 0))], out_specs=[row, vec],
            out_shape=[jax.ShapeDtypeStruct((T, D), F32), jax.ShapeDtypeStruct((1, D), F32)],
            compiler_params=_cp(),
        )(x, g, target, ct)

    @jax.custom_vjp
    def f(x, g, target):
        return fwd_call(x, g, target)

    def fwd(x, g, target):
        return fwd_call(x, g, target), (x, g, target)

    def bwd(res, ct):
        x, g, target = res
        dx, dg = bwd_call(x, g, target, jnp.reshape(ct, (1, 1)).astype(F32))
        return dx, dg, jnp.zeros_like(target)

    f.defvjp(fwd, bwd)
    return f(x, g, target)


def adamw(w, g, m, v, *, name):
    R, C = w.shape
    tr = _tile(R, 512, 8)
    spec = pl.BlockSpec((tr, C), lambda i: (i, 0))
    c1 = 1.0 / (1.0 - ADAM_B1 ** ADAM_STEP)
    c2 = 1.0 / (1.0 - ADAM_B2 ** ADAM_STEP)

    def body(w_ref, g_ref, m_ref, v_ref, d_ref, nm_ref, nv_ref):
        gv = g_ref[...]
        nm = ADAM_B1 * m_ref[...] + (1.0 - ADAM_B1) * gv
        nv = ADAM_B2 * v_ref[...] + (1.0 - ADAM_B2) * (gv * gv)
        nm_ref[...] = nm
        nv_ref[...] = nv
        d_ref[...] = -ADAM_LR * ((nm * c1) / (jnp.sqrt(nv * c2) + ADAM_EPS) + ADAM_WD * w_ref[...])

    o = jax.ShapeDtypeStruct((R, C), F32)
    return pl.pallas_call(
        body, name=name, grid=(R // tr,), in_specs=[spec] * 4, out_specs=[spec] * 3,
        out_shape=[o, o, o], compiler_params=_cp(),
    )(w, g, m, v)


def _position():
    x, y, c = lax.axis_index("x"), lax.axis_index("y"), lax.axis_index("c")
    return x, y, c


def all_gather(shard, *, name, cols=False):
    R, C = shard.shape
    assert not cols or C % LANES == 0

    def body(x_ref, out_ref, send_sems, recv_sems, local_sem):
        x, y, c = _position()
        me, sibling = (x, y, c), (x, y, 1 - c)
        chips = [(1 - x, y), (x, 1 - y), (1 - x, 1 - y)]

        def block(px, py, pc):
            idx = 4 * px + 2 * py + pc
            if cols:
                return out_ref.at[:, pl.ds(pl.multiple_of(idx * C, LANES), C)]
            return out_ref.at[idx]

        def copy(k, blk, to, src=None):
            return pltpu.make_async_remote_copy(
                src_ref=block(*blk) if src is None else src, dst_ref=block(*blk),
                send_sem=send_sems.at[k], recv_sem=recv_sems.at[k],
                device_id=to, device_id_type=pl.DeviceIdType.MESH)

        mine = pltpu.make_async_copy(x_ref, block(*me), local_sem)
        mine.start()
        first = [copy(0, me, sibling, src=x_ref)]
        first += [copy(1 + j, me, (*chip, c), src=x_ref) for j, chip in enumerate(chips)]
        for cp in first:
            cp.start()
        passed = [copy(4 + j, (*chip, c), sibling) for j, chip in enumerate(chips)]
        for j, chip in enumerate(chips):
            copy(1 + j, (*chip, c), me).wait_recv()
            passed[j].start()
        copy(0, sibling, me).wait_recv()
        for j, chip in enumerate(chips):
            copy(4 + j, (*chip, 1 - c), me).wait_recv()
        for cp in first + passed:
            cp.wait_send()
        mine.wait()

    return pl.pallas_call(
        body, name=name,
        out_shape=jax.ShapeDtypeStruct((R, N_DEV * C) if cols else (N_DEV, R, C), shard.dtype),
        in_specs=[pl.BlockSpec(memory_space=pl.ANY)], out_specs=pl.BlockSpec(memory_space=pl.ANY),
        scratch_shapes=[pltpu.SemaphoreType.DMA((7,)), pltpu.SemaphoreType.DMA((7,)),
                        pltpu.SemaphoreType.DMA],
    )(shard)


N_CHIP = 4


def rs_pair_routine(g, cols):
    if cols:
        R, C = g.shape[0], g.shape[1] // N_DEV
        assert C % LANES == 0
    else:
        _, R, C = g.shape

    def blk(ref, idx):
        if cols:
            return ref.at[:, pl.ds(pl.multiple_of(idx * C, LANES), C)]
        return ref.at[idx]

    def copies(ins, outs, send_sems, recv_sems, local_sems):
        (g_ref,), (own_ref, got_ref) = ins, outs
        x, y, c = _position()
        local, remote = [], []
        for q in range(N_CHIP):
            local.append(pltpu.make_async_copy(blk(g_ref, 2 * q + c), own_ref.at[q], local_sems.at[q]))
            remote.append(pltpu.make_async_remote_copy(
                src_ref=blk(g_ref, 2 * q + 1 - c), dst_ref=got_ref.at[q],
                send_sem=send_sems.at[q], recv_sem=recv_sems.at[q],
                device_id=(x, y, 1 - c), device_id_type=pl.DeviceIdType.MESH))
        return local, remote

    def start(*refs):
        local, remote = copies(*refs)
        for cp in remote + local:
            cp.start()

    def finish(*refs):
        local, remote = copies(*refs)
        for cp in remote:
            cp.wait_recv()
        for cp in remote:
            cp.wait_send()
        for cp in local:
            cp.wait()

    o = jax.ShapeDtypeStruct((N_CHIP, R, C), g.dtype)
    return dict(ins=[g], outs=[o, o], n_sem=N_CHIP, n_local=N_CHIP, start=start, finish=finish)


def rs_chip_routine(h):
    _, R, C = h.shape
    RELATIONS = ((0, 1), (1, 0), (1, 1))

    def copies(ins, outs, send_sems, recv_sems, local_sems):
        (h_ref,), (out_ref,) = ins, outs
        x, y, c = _position()
        local = [pltpu.make_async_copy(h_ref.at[2 * x + y], out_ref.at[3], local_sems.at[0])]
        remote = []
        for k, (fx, fy) in enumerate(RELATIONS):
            px = (1 - x) if fx else x
            py = (1 - y) if fy else y
            remote.append(pltpu.make_async_remote_copy(
                src_ref=h_ref.at[2 * px + py], dst_ref=out_ref.at[k],
                send_sem=send_sems.at[k], recv_sem=recv_sems.at[k],
                device_id=(px, py, c), device_id_type=pl.DeviceIdType.MESH))
        return local, remote

    def start(*refs):
        local, remote = copies(*refs)
        for cp in remote + local:
            cp.start()

    def finish(*refs):
        local, remote = copies(*refs)
        for cp in remote:
            cp.wait_recv()
        for cp in remote:
            cp.wait_send()
        for cp in local:
            cp.wait()

    return dict(ins=[h], outs=[jax.ShapeDtypeStruct((N_CHIP, R, C), h.dtype)], n_sem=3, n_local=1,
                start=start, finish=finish)


def comm_call(routine, *, name):
    n_in, n_out = len(routine["ins"]), len(routine["outs"])

    def body(*refs):
        ins, outs, sems = refs[:n_in], refs[n_in:n_in + n_out], refs[n_in + n_out:]
        routine["start"](ins, outs, *sems)
        routine["finish"](ins, outs, *sems)

    any_spec = pl.BlockSpec(memory_space=pl.ANY)
    return pl.pallas_call(
        body, name=name, out_shape=routine["outs"],
        in_specs=[any_spec] * n_in, out_specs=[any_spec] * n_out,
        scratch_shapes=[pltpu.SemaphoreType.DMA((routine["n_sem"],)), pltpu.SemaphoreType.DMA((routine["n_sem"],)),
                        pltpu.SemaphoreType.DMA((routine["n_local"],))],
    )(*routine["ins"])


def add_pairs(a, b, *, name):
    n, R, C = a.shape
    tr = _tile(R, 256, 8)
    spec = pl.BlockSpec((1, tr, C), lambda q, i: (q, i, 0))

    def body(a_ref, b_ref, o_ref):
        o_ref[...] = (a_ref[...].astype(F32) + b_ref[...].astype(F32)).astype(o_ref.dtype)

    return pl.pallas_call(
        body, name=name, grid=(n, R // tr), in_specs=[spec, spec], out_specs=spec,
        out_shape=jax.ShapeDtypeStruct((n, R, C), a.dtype), compiler_params=_cp(),
    )(a, b)


def reduce_scatter(g, *, cols, name):
    own, got = comm_call(rs_pair_routine(g, cols), name=name + "_pair")
    h = add_pairs(own, got, name=name + "_add")
    return sum_blocks(comm_call(rs_chip_routine(h), name=name + "_chip")[0], name=name + "_sum")


def sum_blocks(stack, *, name):
    n, R, C = stack.shape
    tr = _tile(R, 256, 8)

    def body(x_ref, o_ref):
        acc = x_ref[0].astype(F32)
        for i in range(1, n):
            acc = acc + x_ref[i].astype(F32)
        o_ref[...] = acc

    return pl.pallas_call(
        body, name=name, grid=(R // tr,),
        in_specs=[pl.BlockSpec((n, tr, C), lambda i: (0, i, 0))],
        out_specs=pl.BlockSpec((tr, C), lambda i: (i, 0)),
        out_shape=jax.ShapeDtypeStruct((R, C), F32), compiler_params=_cp(),
    )(stack)


PACK_COLS = 1024
PACK_ROW_MULT = 8


def _pack(arrays, dtype):
    flat = jnp.concatenate([a.reshape(-1).astype(dtype) for a in arrays])
    n = flat.shape[0]
    unit = PACK_COLS * PACK_ROW_MULT
    padded = -(-n // unit) * unit
    return jnp.pad(flat, (0, padded - n)).reshape(padded // PACK_COLS, PACK_COLS)


def _unpack(packed, shapes):
    flat = packed.reshape(-1)
    out, o = [], 0
    for s in shapes:
        n = int(np.prod(s))
        out.append(flat[o:o + n].reshape(s))
        o += n
    return out


def fsdp_cols(shard, *, name):
    K, n = shard.shape
    npad = -(-n // LANES) * LANES

    @jax.custom_vjp
    def f(p):
        p = jnp.pad(p, ((0, 0), (0, npad - n))) if npad != n else p
        return all_gather(p.astype(BF16), cols=True, name=name + "_ag")

    def fwd(p):
        return f(p), None

    def bwd(_, g):
        d = reduce_scatter(g, cols=True, name=name + "_rs")
        return (d[:, :n] if npad != n else d,)

    f.defvjp(fwd, bwd)
    return f(shard)


def fsdp_rows(shard, *, name):
    k, N = shard.shape

    @jax.custom_vjp
    def f(p):
        return all_gather(p.astype(BF16), name=name + "_ag").reshape(N_DEV * k, N)

    def fwd(p):
        return f(p), None

    def bwd(_, g):
        return (reduce_scatter(g.reshape(N_DEV, k, N), cols=False, name=name + "_rs"),)

    f.defvjp(fwd, bwd)
    return f(shard)


def _unpad_cols(w, n):
    K = w.shape[0]
    npad = w.shape[1] // N_DEV
    if npad == n:
        return w
    return w.reshape(K, N_DEV, npad)[:, :, :n].reshape(K, N_DEV * n)


def gather_rows(part, me, *, name):
    rows, n = part.shape
    per = rows // N_DEV

    @jax.custom_vjp
    def f(part):
        full = all_gather(part, name=name + "_fwd")
        mine = lax.dynamic_slice_in_dim(full, me * per, per, axis=1)
        return jnp.swapaxes(mine, 0, 1).reshape(per, N_DEV * n)

    def fwd(part):
        return f(part), None

    def bwd(_, g):
        full = all_gather(g, name=name + "_bwd")
        mine = lax.dynamic_slice_in_dim(full, me * n, n, axis=2)
        return (mine.reshape(rows, n),)

    f.defvjp(fwd, bwd)
    return f(part)


def _seg_layout():
    offs = np.concatenate([[0], np.cumsum(IN_SPLITS)])
    cols, widths = [], []
    for s in SEG_ORDER:
        cols.append((int(offs[s]), int(offs[s + 1])))
        widths.append(SEG_PAD.get(s, IN_SPLITS[s]))
    return cols, widths


def _arrange_w_in(w):
    cols, widths = _seg_layout()
    parts = []
    for (a, b), wd in zip(cols, widths):
        seg = w[:, a:b]
        if wd != b - a:
            seg = jnp.pad(seg, ((0, 0), (0, wd - (b - a))))
        parts.append(seg)
    parts.append(jnp.zeros((w.shape[0], IN_WIDTH - sum(widths)), w.dtype))
    return jnp.concatenate(parts, axis=1)


def split_cols(proj, widths):
    @jax.custom_vjp
    def f(p):
        outs, o = [], 0
        for wd in widths:
            outs.append(p[:, o:o + wd])
            o += wd
        return tuple(outs)

    def fwd(p):
        return f(p), None

    def bwd(_, gs):
        rest = proj.shape[1] - sum(widths)
        tail = [jnp.zeros((proj.shape[0], rest), proj.dtype)] if rest else []
        return (jnp.concatenate(list(gs) + tail, axis=1),)

    f.defvjp(fwd, bwd)
    return f(proj)


BIG = ("w_in", "w_uq", "w_ukv", "conv_w", "w_out", "w_gate_up", "w_down")
SMALL = ("b_ada", "norm1_g", "norm2_g", "q_norm_g", "k_norm_g", "mla_q_norm_g", "mla_kv_norm_g",
         "conv_b", "dt_bias", "a_log", "d_skip", "ssd_norm_g", "final_norm_g")
WEIGHTS = ("w_ada", "b_ada", "norm1_g", "norm2_g", "w_in", "q_norm_g", "k_norm_g", "mla_q_norm_g",
           "w_uq", "mla_kv_norm_g", "w_ukv", "conv_w", "conv_b", "dt_bias", "a_log", "d_skip",
           "ssd_norm_g", "w_out", "w_gate_up", "w_down", "final_norm_g")


def _layer(l, x, mod, W, P, tabs):
    B, S, D = x.shape
    T = B * S
    nm = f"l{l}_"
    shift1, scale1, gate1, shift2, scale2, gate2 = [m[:, None, :] for m in jnp.split(mod, 6, axis=-1)]
    cos_a, sin_a, cos_b, sin_b, cos_k, sin_k = tabs

    h = rmsmod(x, P["norm1_g"][l][None], scale1, shift1, name=nm + "norm1")
    w_in = _arrange_w_in(W["w_in"])
    proj = linear(h.reshape(T, D), w_in, out_dtype=F32, name=nm + "in")
    q_a, k_a, v_a, cq, ckv, z, xbc, kpe, dtr = split_cols(proj, _seg_layout()[1])

    qn = group_rmsnorm(q_a, jnp.tile(P["q_norm_g"][l], GQA_H)[None], gs=HEAD, out_dtype=F32, name=nm + "qnorm")
    kn = group_rmsnorm(k_a, jnp.tile(P["k_norm_g"][l], GQA_KV)[None], gs=HEAD, out_dtype=F32, name=nm + "knorm")
    qr = rope(qn.reshape(B, S, -1), cos_a[:, :GQA_H * HEAD], sin_a[:, :GQA_H * HEAD], d=HEAD // 4, name=nm + "qrope")
    kr = rope(kn.reshape(B, S, -1), cos_a[:, :GQA_KV * HEAD], sin_a[:, :GQA_KV * HEAD], d=HEAD // 4, name=nm + "krope")
    o_a = attention(qr, kr, v_a.reshape(B, S, -1), H=GQA_H, Hkv=GQA_KV, dk=HEAD, dv=HEAD,
                    scale=HEAD ** -0.5, name=nm + "gqa")

    w_uq = W["w_uq"].reshape(MLA_QL, MLA_H, MLA_NOPE + MLA_ROPE)
    w_uq = jnp.concatenate([w_uq[:, :, :MLA_NOPE].reshape(MLA_QL, -1), w_uq[:, :, MLA_NOPE:].reshape(MLA_QL, -1)], axis=1)
    w_ukv = W["w_ukv"].reshape(MLA_KVL, MLA_H, MLA_NOPE + MLA_V)
    w_ukv = jnp.concatenate([w_ukv[:, :, :MLA_NOPE].reshape(MLA_KVL, -1), w_ukv[:, :, MLA_NOPE:].reshape(MLA_KVL, -1)], axis=1)
    cqn = group_rmsnorm(cq, P["mla_q_norm_g"][l][None], gs=MLA_QL, out_dtype=BF16, name=nm + "cqnorm")
    ckvn = group_rmsnorm(ckv, P["mla_kv_norm_g"][l][None], gs=MLA_KVL, out_dtype=BF16, name=nm + "ckvnorm")
    qb = linear(cqn, w_uq, out_dtype=F32, name=nm + "uq")
    kvb = linear(ckvn, w_ukv, out_dtype=F32, name=nm + "ukv")
    q_nope, q_pe = split_cols(qb, (MLA_H * MLA_NOPE, MLA_H * MLA_ROPE))
    k_nope, v_b = split_cols(kvb, (MLA_H * MLA_NOPE, MLA_H * MLA_V))
    q_pe = rope(q_pe.reshape(B, S, -1), cos_b, sin_b, d=MLA_ROPE // 4, name=nm + "qpe_rope")
    k_pe = rope(kpe.reshape(B, S, -1), cos_k, sin_k, d=MLA_ROPE // 4, name=nm + "kpe_rope")[:, :, :MLA_ROPE]
    zpad = jnp.zeros((B, S, MLA_H, MLA_DK - MLA_NOPE - MLA_ROPE), F32)
    q_cat = jnp.concatenate([q_nope.reshape(B, S, MLA_H, MLA_NOPE), q_pe.reshape(B, S, MLA_H, MLA_ROPE), zpad],
                            axis=-1).reshape(B, S, MLA_H * MLA_DK)
    k_cat = jnp.concatenate([k_nope.reshape(B, S, MLA_H, MLA_NOPE),
                             jnp.broadcast_to(k_pe[:, :, None, :], (B, S, MLA_H, MLA_ROPE)), zpad],
                            axis=-1).reshape(B, S, MLA_H * MLA_DK)
    o_b = attention(q_cat, k_cat, v_b.reshape(B, S, -1), H=MLA_H, Hkv=MLA_H, dk=MLA_DK, dv=MLA_V,
                    scale=(MLA_NOPE + MLA_ROPE) ** -0.5, name=nm + "mla")

    xact = conv_silu(xbc.reshape(B, S, -1), W["conv_w"].astype(F32), P["conv_b"][l][None], name=nm + "conv")
    brow = jnp.pad(P["dt_bias"][l].reshape(1, -1), ((0, 0), (0, LANES - 2 * SSD_H)))
    arow = jnp.pad(-jnp.exp(P["a_log"][l].reshape(1, -1)), ((0, 0), (0, LANES - 2 * SSD_H)))
    raw = dtr.reshape(B, S, LANES)
    y_f = ssd_scan(xact, raw, brow, arow, rev=False, name=nm + "ssd_f")
    y_b = ssd_scan(xact, raw, brow, arow, rev=True, name=nm + "ssd_b")
    dsk = jnp.repeat(P["d_skip"][l], SSD_P)[None]
    o_c = ssd_out(y_f.reshape(T, -1), y_b.reshape(T, -1), xact.reshape(T, -1), z, dsk,
                  P["ssd_norm_g"][l][None], name=nm + "ssd_out")

    o = jnp.concatenate([o_a.reshape(T, -1).astype(BF16), o_b.reshape(T, -1).astype(BF16), o_c], axis=-1)
    mix = linear(o, W["w_out"], out_dtype=F32, name=nm + "out")
    x = gated_residual(x, gate1, mix.reshape(B, S, D), name=nm + "res1")

    h = rmsmod(x, P["norm2_g"][l][None], scale2, shift2, name=nm + "norm2")
    gu = linear(h.reshape(T, D), W["w_gate_up"], out_dtype=BF16, name=nm + "gate_up")
    act = swiglu(gu, name=nm + "swiglu")
    ffn = linear(act, W["w_down"], out_dtype=F32, name=nm + "down")
    return gated_residual(x, gate2, ffn.reshape(B, S, D), name=nm + "res2")


def kernel(x, c, w_ada, b_ada, norm1_g, norm2_g, w_in, q_norm_g, k_norm_g, mla_q_norm_g, w_uq, mla_kv_norm_g, w_ukv, conv_w, conv_b, dt_bias, a_log, d_skip, ssd_norm_g, w_out, w_gate_up, w_down, final_norm_g, loss_target, m_w_ada, m_b_ada, m_norm1_g, m_norm2_g, m_w_in, m_q_norm_g, m_k_norm_g, m_mla_q_norm_g, m_w_uq, m_mla_kv_norm_g, m_w_ukv, m_conv_w, m_conv_b, m_dt_bias, m_a_log, m_d_skip, m_ssd_norm_g, m_w_out, m_w_gate_up, m_w_down, m_final_norm_g, v_w_ada, v_b_ada, v_norm1_g, v_norm2_g, v_w_in, v_q_norm_g, v_k_norm_g, v_mla_q_norm_g, v_w_uq, v_mla_kv_norm_g, v_w_ukv, v_conv_w, v_conv_b, v_dt_bias, v_a_log, v_d_skip, v_ssd_norm_g, v_w_out, v_w_gate_up, v_w_down, v_final_norm_g):
    args = dict(locals())
    weights = {n: args[n] for n in WEIGHTS}
    moments_m = {n: args["m_" + n] for n in WEIGHTS}
    moments_v = {n: args["v_" + n] for n in WEIGHTS}
    B, S, D = x.shape
    L = w_ada.shape[0]
    T = B * S
    px, py, pc = _position()
    me = 4 * px + 2 * py + pc
    small_shapes = [weights[n].shape for n in SMALL]

    tabs = (*rope_tables(S, HEAD, GQA_H * HEAD), *rope_tables(S, MLA_ROPE, MLA_H * MLA_ROPE),
            *rope_tables(S, MLA_ROPE, 2 * MLA_ROPE))
    c_all = all_gather(c, name="gather_c").reshape(N_DEV * B, D)

    def local_loss(big, w_ada_s, small, x):
        P = dict(zip(SMALL, small))
        for l in range(L):
            W = {}
            for n in ("w_in", "w_uq", "w_ukv", "w_gate_up"):
                W[n] = _unpad_cols(fsdp_cols(big[n][l], name=f"l{l}_{n}"), big[n].shape[2])
            W["conv_w"] = _unpad_cols(fsdp_cols(big["conv_w"][l], name=f"l{l}_conv_w"), big["conv_w"].shape[2])
            for n in ("w_out", "w_down"):
                W[n] = fsdp_rows(big[n][l], name=f"l{l}_{n}")
            part = linear(c_all, w_ada_s[l], out_dtype=F32, a_silu=True, name=f"l{l}_ada")
            mod = gather_rows(part, me, name=f"l{l}_mod") + P["b_ada"][l][None]
            x = _layer(l, x, mod, W, P, tabs)
        return final_loss(x.reshape(T, D), P["final_norm_g"][None], loss_target.reshape(T, D), name="loss")

    big = {n: weights[n] for n in BIG}
    small = tuple(weights[n] for n in SMALL)
    loss, (g_big, g_ada, g_small, grad_x) = jax.value_and_grad(local_loss, argnums=(0, 1, 2, 3))(
        big, w_ada, small, x)
    loss = lax.psum(loss, ("x", "y", "c"))

    grads = dict(g_big)
    grads["w_ada"] = g_ada
    g_small_sum = sum_blocks(all_gather(_pack(g_small, F32), name="small_grads_ag"), name="small_grads_sum")
    grads.update(zip(SMALL, _unpack(g_small_sum, small_shapes)))

    delta, new_m, new_v = {}, {}, {}
    for n in ("w_ada",) + BIG:
        shp = weights[n].shape
        two_d = (int(np.prod(shp[:-1])), shp[-1])
        d_, m_, v_ = adamw(weights[n].reshape(two_d), grads[n].reshape(two_d), moments_m[n].reshape(two_d),
                           moments_v[n].reshape(two_d), name="adamw_" + n)
        delta[n], new_m[n], new_v[n] = d_.reshape(shp), m_.reshape(shp), v_.reshape(shp)
    d_, m_, v_ = adamw(_pack([weights[n] for n in SMALL], F32), g_small_sum,
                       _pack([moments_m[n] for n in SMALL], F32), _pack([moments_v[n] for n in SMALL], F32),
                       name="adamw_small")
    for tgt, packed in ((delta, d_), (new_m, m_), (new_v, v_)):
        tgt.update(zip(SMALL, _unpack(packed, small_shapes)))

    return (loss, grad_x, *[grads[n] for n in WEIGHTS], *[delta[n] for n in WEIGHTS],
            *[new_m[n] for n in WEIGHTS], *[new_v[n] for n in WEIGHTS])
```

```python
import functools
import math

import jax
import jax.numpy as jnp
import numpy as np
from jax import lax
from jax.experimental import pallas as pl
from jax.experimental.pallas import tpu as pltpu

F32 = jnp.float32
BF16 = jnp.bfloat16
N_DEV = 8
EPS = 1e-6
ROPE_THETA = 10000.0
GRID_W = 64

GQA_H, GQA_KV, HEAD = 6, 2, 128
MLA_H, MLA_QL, MLA_KVL, MLA_NOPE, MLA_ROPE, MLA_V = 4, 512, 256, 128, 64, 128
MLA_DK = 256
SSD_H, SSD_P, SSD_G, SSD_N, SSD_K, CHUNK = 12, 64, 2, 128, 5, 128
SSD_INNER = SSD_H * SSD_P
SSD_CONV_DIM = SSD_INNER + 2 * SSD_G * SSD_N
N_PAIR = SSD_H // 2
LANES = 128
IN_SPLITS = (768, 256, 256, 512, 256, 64, 768, 1280, 24)
IN_COLS = sum(IN_SPLITS)
SEG_ORDER = (0, 1, 2, 3, 4, 6, 7, 5, 8)
SEG_PAD = {5: 128, 8: 128}
IN_WIDTH = 4608

ADAM_LR, ADAM_B1, ADAM_B2, ADAM_EPS, ADAM_WD, ADAM_STEP = 0.001, 0.9, 0.999, 1e-08, 0.01, 10
VMEM_LIMIT = 56 * 1024 * 1024
MM_TM, MM_TN, MM_TK = 1024, 1024, 2048


def _cp(**kw):
    return pltpu.CompilerParams(vmem_limit_bytes=VMEM_LIMIT, **kw)


def _tile(dim, cap, mult=128):
    if dim <= cap:
        return dim
    best = None
    t = mult
    while t <= cap:
        if dim % t == 0:
            best = t
        t += mult
    assert best is not None, (dim, cap)
    return best


def _sigmoid(x):
    return 1.0 / (1.0 + jnp.exp(-x))


def _dot(a, b, dims):
    return lax.dot_general(a, b, (dims, ((), ())), preferred_element_type=F32)


NN = ((1,), (0,))
NT = ((1,), (1,))
TN = ((0,), (0,))


def _dotf(a, b, dims=NN):
    return lax.dot_general(a, b, (dims, ((), ())), preferred_element_type=F32,
                           precision=lax.Precision.HIGHEST)


def _bf(x):
    return x.astype(BF16)


def mm(a, b, *, ta=False, tb=False, out_dtype=F32, a_silu=False, name):
    if ta:
        K, M = a.shape
    else:
        M, K = a.shape
    if tb:
        N, K2 = b.shape
    else:
        K2, N = b.shape
    assert K == K2, (a.shape, b.shape, ta, tb)
    tm, tn, tk = _tile(M, MM_TM), _tile(N, MM_TN), _tile(K, MM_TK)
    nk = K // tk
    dims = ((0 if ta else 1,), (1 if tb else 0,))

    def partial_product(a_ref, b_ref):
        av = a_ref[...]
        if a_silu:
            av = av.astype(F32)
            av = av * _sigmoid(av)
        return _dot(_bf(av), _bf(b_ref[...]), dims)

    def body_single(a_ref, b_ref, o_ref):
        o_ref[...] = partial_product(a_ref, b_ref).astype(o_ref.dtype)

    def body_acc(a_ref, b_ref, o_ref, acc_ref):
        k = pl.program_id(2)

        @pl.when(k == 0)
        def _():
            acc_ref[...] = partial_product(a_ref, b_ref)

        @pl.when(k > 0)
        def _():
            acc_ref[...] += partial_product(a_ref, b_ref)

        @pl.when(k == nk - 1)
        def _():
            o_ref[...] = acc_ref[...].astype(o_ref.dtype)

    body = body_single if nk == 1 else body_acc

    a_spec = (pl.BlockSpec((tk, tm), lambda i, j, k: (k, i)) if ta
              else pl.BlockSpec((tm, tk), lambda i, j, k: (i, k)))
    b_spec = (pl.BlockSpec((tn, tk), lambda i, j, k: (j, k)) if tb
              else pl.BlockSpec((tk, tn), lambda i, j, k: (k, j)))
    return pl.pallas_call(
        body, name=name, grid=(M // tm, N // tn, nk),
        in_specs=[a_spec, b_spec],
        out_specs=pl.BlockSpec((tm, tn), lambda i, j, k: (i, j)),
        out_shape=jax.ShapeDtypeStruct((M, N), out_dtype),
        scratch_shapes=[] if nk == 1 else [pltpu.VMEM((tm, tn), F32)],
        compiler_params=_cp(dimension_semantics=("parallel", "parallel", "arbitrary")),
    )(a, b)


def linear(a, w, *, out_dtype, name, a_silu=False):
    @jax.custom_vjp
    def f(a, w):
        return mm(a, w, out_dtype=out_dtype, a_silu=a_silu, name=name + "_fwd")

    def fwd(a, w):
        return f(a, w), (a, w)

    def bwd(res, g):
        a, w = res
        if a_silu:
            da = jnp.zeros_like(a)
        else:
            da = mm(g, w, tb=True, out_dtype=a.dtype, name=name + "_da")
        dw = mm(a, g, ta=True, out_dtype=w.dtype, a_silu=a_silu, name=name + "_dw")
        return da, dw

    f.defvjp(fwd, bwd)
    return f(a, w)


def rmsmod(x, g, scale, shift, *, name):
    B, S, D = x.shape
    ts = _tile(S, 256, 8)
    row = pl.BlockSpec((1, ts, D), lambda b, j: (b, j, 0))
    per_b = pl.BlockSpec((1, 1, D), lambda b, j: (b, 0, 0))
    gspec = pl.BlockSpec((1, D), lambda b, j: (0, 0))

    def fwd_call(x, g, scale, shift):
        def body(x_ref, g_ref, sc_ref, sh_ref, o_ref):
            xv = x_ref[0]
            r = lax.rsqrt(jnp.mean(xv * xv, axis=-1, keepdims=True) + EPS)
            y = xv * r * g_ref[...]
            o_ref[0] = (y * (1.0 + sc_ref[0]) + sh_ref[0]).astype(o_ref.dtype)

        return pl.pallas_call(
            body, name=name + "_fwd", grid=(B, S // ts),
            in_specs=[row, gspec, per_b, per_b], out_specs=row,
            out_shape=jax.ShapeDtypeStruct((B, S, D), BF16), compiler_params=_cp(),
        )(x, g, scale, shift)

    def bwd_call(x, g, scale, dh):
        def body(x_ref, g_ref, sc_ref, dh_ref, dx_ref, dg_ref, dsc_ref, dsh_ref):
            j = pl.program_id(1)
            xv = x_ref[0]
            dh = dh_ref[0].astype(F32)
            r = lax.rsqrt(jnp.mean(xv * xv, axis=-1, keepdims=True) + EPS)
            xn = xv * r
            gv = g_ref[...]
            dy = dh * (1.0 + sc_ref[0])
            dxn = dy * gv
            dx_ref[0] = r * (dxn - xn * jnp.mean(dxn * xn, axis=-1, keepdims=True))

            @pl.when(j == 0)
            def _():
                dg_ref[...] = jnp.zeros_like(dg_ref)
                dsc_ref[...] = jnp.zeros_like(dsc_ref)
                dsh_ref[...] = jnp.zeros_like(dsh_ref)

            dg_ref[0] += jnp.sum(dy * xn, axis=0, keepdims=True)
            dsc_ref[0] += jnp.sum(dh * xn * gv, axis=0, keepdims=True)
            dsh_ref[0] += jnp.sum(dh, axis=0, keepdims=True)

        vec = jax.ShapeDtypeStruct((B, 1, D), F32)
        return pl.pallas_call(
            body, name=name + "_bwd", grid=(B, S // ts),
            in_specs=[row, gspec, per_b, row], out_specs=[row, per_b, per_b, per_b],
            out_shape=[jax.ShapeDtypeStruct((B, S, D), F32), vec, vec, vec], compiler_params=_cp(),
        )(x, g, scale, dh)

    @jax.custom_vjp
    def f(x, g, scale, shift):
        return fwd_call(x, g, scale, shift)

    def fwd(x, g, scale, shift):
        return fwd_call(x, g, scale, shift), (x, g, scale)

    def bwd(res, dh):
        x, g, scale = res
        dx, dg, dsc, dsh = bwd_call(x, g, scale, dh)
        return dx, jnp.sum(dg, axis=0), dsc, dsh

    f.defvjp(fwd, bwd)
    return f(x, g, scale, shift)


def group_rmsnorm(x, g, *, gs, out_dtype, name):
    T, W = x.shape
    ng = W // gs
    tr = _tile(T, 512, 8)
    row = pl.BlockSpec((tr, W), lambda i: (i, 0))
    gspec = pl.BlockSpec((1, W), lambda i: (0, 0))

    def fwd_call(x, g):
        def body(x_ref, g_ref, o_ref):
            for i in range(ng):
                sl = slice(i * gs, (i + 1) * gs)
                xv = x_ref[:, sl]
                r = lax.rsqrt(jnp.mean(xv * xv, axis=-1, keepdims=True) + EPS)
                o_ref[:, sl] = (xv * r * g_ref[:, sl]).astype(o_ref.dtype)

        return pl.pallas_call(
            body, name=name + "_fwd", grid=(T // tr,), in_specs=[row, gspec], out_specs=row,
            out_shape=jax.ShapeDtypeStruct((T, W), out_dtype), compiler_params=_cp(),
        )(x, g)

    def bwd_call(x, g, dy):
        def body(x_ref, g_ref, dy_ref, dx_ref, dg_ref):
            @pl.when(pl.program_id(0) == 0)
            def _():
                dg_ref[...] = jnp.zeros_like(dg_ref)

            for i in range(ng):
                sl = slice(i * gs, (i + 1) * gs)
                xv = x_ref[:, sl]
                dyv = dy_ref[:, sl].astype(F32)
                r = lax.rsqrt(jnp.mean(xv * xv, axis=-1, keepdims=True) + EPS)
                xn = xv * r
                dxn = dyv * g_ref[:, sl]
                dx_ref[:, sl] = r * (dxn - xn * jnp.mean(dxn * xn, axis=-1, keepdims=True))
                dg_ref[:, sl] += jnp.sum(dyv * xn, axis=0, keepdims=True)

        return pl.pallas_call(
            body, name=name + "_bwd", grid=(T // tr,), in_specs=[row, gspec, row],
            out_specs=[row, gspec],
            out_shape=[jax.ShapeDtypeStruct((T, W), F32), jax.ShapeDtypeStruct((1, W), F32)],
            compiler_params=_cp(),
        )(x, g, dy)

    @jax.custom_vjp
    def f(x, g):
        return fwd_call(x, g)

    def fwd(x, g):
        return fwd_call(x, g), (x, g)

    def bwd(res, dy):
        return bwd_call(res[0], res[1], dy)

    f.defvjp(fwd, bwd)
    return f(x, g)


def rope_tables(seq_len, rot_dim, width):
    rows = seq_len // GRID_W
    row_idx = jnp.repeat(jnp.arange(rows), GRID_W).astype(F32)
    col_idx = jnp.tile(jnp.arange(GRID_W), rows).astype(F32)
    axis_dim = rot_dim // 2
    inv_freq = jnp.power(ROPE_THETA, -jnp.arange(0, axis_dim, 2, dtype=F32) / axis_dim)
    ang_r = row_idx[:, None] * inv_freq[None, :]
    ang_c = col_idx[:, None] * inv_freq[None, :]
    cos = jnp.concatenate([jnp.cos(ang_r), jnp.cos(ang_r), jnp.cos(ang_c), jnp.cos(ang_c)], axis=-1)
    sin = jnp.concatenate([-jnp.sin(ang_r), jnp.sin(ang_r), -jnp.sin(ang_c), jnp.sin(ang_c)], axis=-1)
    reps = width // rot_dim
    return jnp.tile(cos, (1, reps)), jnp.tile(sin, (1, reps))


def rope(x, cos, sin, *, d, name):
    B, S, W = x.shape
    ts = _tile(S, 512, 8)
    row = pl.BlockSpec((1, ts, W), lambda b, j: (b, j, 0))
    tab = pl.BlockSpec((ts, W), lambda b, j: (j, 0))

    def call(x, inverse, nm):
        def body(x_ref, c_ref, s_ref, o_ref):
            xv = x_ref[0]
            lane = lax.broadcasted_iota(jnp.int32, xv.shape, 1)
            first = (lane // d) % 2 == 0

            def swap(v):
                return jnp.where(first, pltpu.roll(v, W - d, 1), pltpu.roll(v, d, 1))

            if inverse:
                o_ref[0] = xv * c_ref[...] + swap(xv * s_ref[...])
            else:
                o_ref[0] = xv * c_ref[...] + swap(xv) * s_ref[...]

        return pl.pallas_call(
            body, name=nm, grid=(B, S // ts), in_specs=[row, tab, tab], out_specs=row,
            out_shape=jax.ShapeDtypeStruct((B, S, W), F32), compiler_params=_cp(),
        )(x, cos, sin)

    @jax.custom_vjp
    def f(x):
        return call(x, False, name + "_fwd")

    def fwd(x):
        return call(x, False, name + "_fwd"), None

    def bwd(_, g):
        return (call(g, True, name + "_bwd"),)

    f.defvjp(fwd, bwd)
    return f(x)


def attention(q, k, v, *, H, Hkv, dk, dv, scale, name):
    B, S, _ = q.shape
    rep = H // Hkv
    tq = _tile(S, 256, 8)
    tkb = _tile(S, 256, 8)

    def fwd_call(q, k, v):
        def body(q_ref, k_ref, v_ref, o_ref, lse_ref):
            s = _dot(_bf(q_ref[0]), _bf(k_ref[0]), NT) * scale
            m = jnp.max(s, axis=-1, keepdims=True)
            p = jnp.exp(s - m)
            l = jnp.sum(p, axis=-1, keepdims=True)
            o_ref[0] = _dot(_bf(p), _bf(v_ref[0]), NN) / l
            lse_ref[0, 0] = m + jnp.log(l)

        return pl.pallas_call(
            body, name=name + "_fwd", grid=(B, H, S // tq),
            in_specs=[pl.BlockSpec((1, tq, dk), lambda b, h, i: (b, i, h)),
                      pl.BlockSpec((1, S, dk), lambda b, h, i: (b, 0, h // rep)),
                      pl.BlockSpec((1, S, dv), lambda b, h, i: (b, 0, h // rep))],
            out_specs=[pl.BlockSpec((1, tq, dv), lambda b, h, i: (b, i, h)),
                       pl.BlockSpec((1, 1, tq, 1), lambda b, h, i: (b, h, i, 0))],
            out_shape=[jax.ShapeDtypeStruct((B, S, H * dv), F32),
                       jax.ShapeDtypeStruct((B, H, S, 1), F32)],
            compiler_params=_cp(),
        )(q, k, v)

    def dq_call(q, k, v, o, do, lse):
        def body(q_ref, k_ref, v_ref, o_ref, do_ref, lse_ref, dq_ref, delta_ref):
            kb = _bf(k_ref[0])
            s = _dot(_bf(q_ref[0]), kb, NT) * scale
            p = jnp.exp(s - lse_ref[0, 0])
            dov = do_ref[0]
            delta = jnp.sum(dov * o_ref[0], axis=-1, keepdims=True)
            dp = _dot(_bf(dov), _bf(v_ref[0]), NT)
            ds = p * (dp - delta)
            dq_ref[0] = _dot(_bf(ds), kb, NN) * scale
            delta_ref[0, 0] = delta

        qs = pl.BlockSpec((1, tq, dk), lambda b, h, i: (b, i, h))
        os_ = pl.BlockSpec((1, tq, dv), lambda b, h, i: (b, i, h))
        col = pl.BlockSpec((1, 1, tq, 1), lambda b, h, i: (b, h, i, 0))
        return pl.pallas_call(
            body, name=name + "_dq", grid=(B, H, S // tq),
            in_specs=[qs, pl.BlockSpec((1, S, dk), lambda b, h, i: (b, 0, h // rep)),
                      pl.BlockSpec((1, S, dv), lambda b, h, i: (b, 0, h // rep)), os_, os_, col],
            out_specs=[qs, col],
            out_shape=[jax.ShapeDtypeStruct(q.shape, F32), jax.ShapeDtypeStruct((B, H, S, 1), F32)],
            compiler_params=_cp(),
        )(q, k, v, o, do, lse)

    def dkv_call(q, k, v, do, lse, delta):
        def body(q_ref, k_ref, v_ref, do_ref, lse_ref, delta_ref, dk_ref, dv_ref):
            @pl.when(pl.program_id(3) == 0)
            def _():
                dk_ref[...] = jnp.zeros_like(dk_ref)
                dv_ref[...] = jnp.zeros_like(dv_ref)

            qb = _bf(q_ref[0])
            dob = _bf(do_ref[0])
            s = _dot(qb, _bf(k_ref[0]), NT) * scale
            p = jnp.exp(s - lse_ref[0, 0])
            dv_ref[0] += _dot(_bf(p), dob, TN)
            dp = _dot(dob, _bf(v_ref[0]), NT)
            ds = p * (dp - delta_ref[0, 0])
            dk_ref[0] += _dot(_bf(ds), qb, TN) * scale

        hq = lambda b, g, j, r: (b, 0, g * rep + r)
        colq = pl.BlockSpec((1, 1, S, 1), lambda b, g, j, r: (b, g * rep + r, 0, 0))
        ks = pl.BlockSpec((1, tkb, dk), lambda b, g, j, r: (b, j, g))
        vs = pl.BlockSpec((1, tkb, dv), lambda b, g, j, r: (b, j, g))
        return pl.pallas_call(
            body, name=name + "_dkv", grid=(B, Hkv, S // tkb, rep),
            in_specs=[pl.BlockSpec((1, S, dk), hq), ks, vs, pl.BlockSpec((1, S, dv), hq), colq, colq],
            out_specs=[ks, vs],
            out_shape=[jax.ShapeDtypeStruct(k.shape, F32), jax.ShapeDtypeStruct(v.shape, F32)],
            compiler_params=_cp(),
        )(q, k, v, do, lse, delta)

    @jax.custom_vjp
    def f(q, k, v):
        return fwd_call(q, k, v)[0]

    def fwd(q, k, v):
        o, lse = fwd_call(q, k, v)
        return o, (q, k, v, o, lse)

    def bwd(res, do):
        q, k, v, o, lse = res
        dq, delta = dq_call(q, k, v, o, do, lse)
        dk_, dv_ = dkv_call(q, k, v, do, lse, delta)
        return dq, dk_, dv_

    f.defvjp(fwd, bwd)
    return f(q, k, v)


def conv_silu(x, w, b, *, name):
    B, S, C = x.shape
    tc = _tile(C, 256)
    pad = SSD_K // 2
    xs = pl.BlockSpec((1, S, tc), lambda bi, j: (bi, 0, j))
    ws = pl.BlockSpec((SSD_K, tc), lambda bi, j: (0, j))
    bs = pl.BlockSpec((1, tc), lambda bi, j: (0, j))

    def shifted(v, off):
        if off == 0:
            return v
        t = lax.broadcasted_iota(jnp.int32, v.shape, 0)
        r = pltpu.roll(v, (-off) % S, 0)
        return jnp.where((t + off >= 0) & (t + off < S), r, 0.0)

    def pre_act(xv, wv, bv):
        u = jnp.zeros_like(xv) + bv
        for k in range(SSD_K):
            u = u + wv[k:k + 1, :] * shifted(xv, k - pad)
        return u

    def fwd_call(x, w, b):
        def body(x_ref, w_ref, b_ref, o_ref):
            u = pre_act(x_ref[0], w_ref[...], b_ref[...])
            o_ref[0] = u * _sigmoid(u)

        return pl.pallas_call(
            body, name=name + "_fwd", grid=(B, C // tc), in_specs=[xs, ws, bs], out_specs=xs,
            out_shape=jax.ShapeDtypeStruct((B, S, C), F32), compiler_params=_cp(),
        )(x, w, b)

    def bwd_call(x, w, b, dy):
        def body(x_ref, w_ref, b_ref, dy_ref, dx_ref, dw_ref):
            xv = x_ref[0]
            wv = w_ref[...]
            u = pre_act(xv, wv, b_ref[...])
            sg = _sigmoid(u)
            du = dy_ref[0] * (sg * (1.0 + u * (1.0 - sg)))
            dx = jnp.zeros_like(xv)
            for k in range(SSD_K):
                dx = dx + wv[k:k + 1, :] * shifted(du, pad - k)
                dw_ref[0, k:k + 1, :] = jnp.sum(du * shifted(xv, k - pad), axis=0, keepdims=True)
            dw_ref[0, SSD_K:SSD_K + 1, :] = jnp.sum(du, axis=0, keepdims=True)
            dw_ref[0, SSD_K + 1:8, :] = jnp.zeros((8 - SSD_K - 1, tc), F32)
            dx_ref[0] = dx

        return pl.pallas_call(
            body, name=name + "_bwd", grid=(B, C // tc), in_specs=[xs, ws, bs, xs],
            out_specs=[xs, pl.BlockSpec((1, 8, tc), lambda bi, j: (bi, 0, j))],
            out_shape=[jax.ShapeDtypeStruct((B, S, C), F32), jax.ShapeDtypeStruct((B, 8, C), F32)],
            compiler_params=_cp(),
        )(x, w, b, dy)

    @jax.custom_vjp
    def f(x, w, b):
        return fwd_call(x, w, b)

    def fwd(x, w, b):
        return fwd_call(x, w, b), (x, w, b)

    def bwd(res, dy):
        x, w, b = res
        dx, dwb = bwd_call(x, w, b, dy)
        dwb = jnp.sum(dwb, axis=0)
        return dx, dwb[:SSD_K], dwb[SSD_K:SSD_K + 1]

    f.defvjp(fwd, bwd)
    return f(x, w, b)


def _softplus(x):
    return jnp.maximum(x, 0.0) + jnp.log1p(jnp.exp(-jnp.abs(x)))


def _ssd_prep(raw, raw_t, brow, arow, bcol, acol, rev):
    li = lax.broadcasted_iota(jnp.int32, (CHUNK, CHUNK), 0)
    ki = lax.broadcasted_iota(jnp.int32, (CHUNK, CHUNK), 1)
    later = (li <= ki) if rev else (li >= ki)
    dt = _softplus(raw + brow)
    a = dt * arow
    cs = _dotf(later.astype(F32), a)
    tot = jnp.sum(a, axis=0, keepdims=True)
    a_t = _softplus(raw_t + bcol) * acol
    earlier = (li >= ki) if rev else (li <= ki)
    cs_t = _dotf(a_t, earlier.astype(F32))
    return dt, a, cs, tot, cs_t, later


def _lane_pick(mat, j):
    lane = lax.broadcasted_iota(jnp.int32, mat.shape, 1)
    return jnp.sum(jnp.where(lane == j, mat, 0.0), axis=1, keepdims=True)


def _head_sum(t, first):
    s0 = jnp.sum(jnp.where(first, t, 0.0), axis=1, keepdims=True)
    s1 = jnp.sum(jnp.where(first, 0.0, t), axis=1, keepdims=True)
    return s0, s1


def ssd_scan(xbc, raw, brow, arow, *, rev, name):
    B, S, _ = xbc.shape
    NC = S // CHUNK
    off = SSD_H if rev else 0
    n_dt = 2 * SSD_H

    def chunk_of(c):
        return (NC - 1 - c) if rev else c

    def specs(cmap):
        return dict(
            x=pl.BlockSpec((1, CHUNK, SSD_INNER), lambda b, c: (b, cmap(c), 0)),
            bm=pl.BlockSpec((1, CHUNK, 2 * SSD_N), lambda b, c: (b, cmap(c), SSD_INNER // (2 * SSD_N))),
            cm=pl.BlockSpec((1, CHUNK, 2 * SSD_N), lambda b, c: (b, cmap(c), SSD_INNER // (2 * SSD_N) + 1)),
            raw=pl.BlockSpec((1, CHUNK, LANES), lambda b, c: (b, cmap(c), 0)),
            raw_t=pl.BlockSpec((1, n_dt, CHUNK), lambda b, c: (b, 0, cmap(c))),
            row=pl.BlockSpec((1, LANES), lambda b, c: (0, 0)),
            colv=pl.BlockSpec((n_dt, 1), lambda b, c: (0, 0)),
            hs=pl.BlockSpec((1, 1, N_PAIR, SSD_N, LANES), lambda b, c: (b, cmap(c), 0, 0, 0)),
        )

    def head_terms(prep, j, first_dummy=None):
        dt, a, cs, tot, cs_t, later = prep
        cs_c = _lane_pick(cs, j)
        cs_r = cs_t[j:j + 1, :]
        dt_c = _lane_pick(dt, j)
        tot_j = _lane_pick(tot, j)
        L = jnp.exp(jnp.where(later, cs_c - cs_r, -1e30))
        return cs_c, cs_r, dt_c, tot_j, L

    def fwd_call(xbc, raw, raw_t, brow, arow, bcol, acol):
        def body(x_ref, bm_ref, cm_ref, raw_ref, rawt_ref, brow_ref, arow_ref, bcol_ref, acol_ref,
                 y_ref, hs_ref, st_ref):
            @pl.when(pl.program_id(1) == 0)
            def _():
                st_ref[...] = jnp.zeros_like(st_ref)

            prep = _ssd_prep(raw_ref[0], rawt_ref[0], brow_ref[...], arow_ref[...],
                             bcol_ref[...], acol_ref[...], rev)
            lane = lax.broadcasted_iota(jnp.int32, (CHUNK, LANES), 1)
            first = lane < SSD_P
            for g in range(SSD_G):
                Bg = _bf(bm_ref[0, :, g * SSD_N:(g + 1) * SSD_N])
                Cg = _bf(cm_ref[0, :, g * SSD_N:(g + 1) * SSD_N])
                G = _dot(Cg, Bg, NT)
                for pp in range(N_PAIR // SSD_G):
                    pi = g * (N_PAIR // SSD_G) + pp
                    c0, _, d0, t0, L0 = head_terms(prep, off + 2 * pi)
                    c1, _, d1, t1, L1 = head_terms(prep, off + 2 * pi + 1)
                    xd = x_ref[0, :, pi * LANES:(pi + 1) * LANES] * jnp.where(first, d0, d1)
                    xdb = _bf(xd)
                    y = jnp.where(first, _dot(_bf(G * L0), xdb, NN), _dot(_bf(G * L1), xdb, NN))
                    dec = jnp.where(first, jnp.exp(t0 - c0), jnp.exp(t1 - c1))
                    h_prev = st_ref[pi]
                    hs_ref[0, 0, pi] = h_prev
                    y = y + _dot(Cg, _bf(h_prev), NN) * jnp.where(first, jnp.exp(c0), jnp.exp(c1))
                    y_ref[0, :, pi * LANES:(pi + 1) * LANES] = y
                    etot = jnp.where(first[:1], jnp.exp(t0), jnp.exp(t1))
                    st_ref[pi] = h_prev * etot + _dot(Bg, _bf(xd * dec), TN)

        sp = specs(chunk_of)
        return pl.pallas_call(
            body, name=name + "_fwd", grid=(B, NC),
            in_specs=[sp["x"], sp["bm"], sp["cm"], sp["raw"], sp["raw_t"], sp["row"], sp["row"],
                      sp["colv"], sp["colv"]],
            out_specs=[sp["x"], sp["hs"]],
            out_shape=[jax.ShapeDtypeStruct((B, S, SSD_INNER), F32),
                       jax.ShapeDtypeStruct((B, NC, N_PAIR, SSD_N, LANES), F32)],
            scratch_shapes=[pltpu.VMEM((N_PAIR, SSD_N, LANES), F32)],
            compiler_params=_cp(),
        )(xbc, xbc, xbc, raw, raw_t, brow, arow, bcol, acol)

    def bwd_call(xbc, raw, raw_t, brow, arow, bcol, acol, hs, dy):
        def body(x_ref, bm_ref, cm_ref, raw_ref, rawt_ref, brow_ref, arow_ref, bcol_ref, acol_ref,
                 hs_ref, dy_ref, dxbc_ref, draw_ref, da_ref, dst_ref):
            @pl.when(pl.program_id(1) == 0)
            def _():
                dst_ref[...] = jnp.zeros_like(dst_ref)
                da_ref[...] = jnp.zeros_like(da_ref)

            raw_v = raw_ref[0]
            prep = _ssd_prep(raw_v, rawt_ref[0], brow_ref[...], arow_ref[...],
                             bcol_ref[...], acol_ref[...], rev)
            dt, a, cs, tot, cs_t, later = prep
            li = lax.broadcasted_iota(jnp.int32, (CHUNK, CHUNK), 0)
            ki = lax.broadcasted_iota(jnp.int32, (CHUNK, CHUNK), 1)
            later_t = (li >= ki) if rev else (li <= ki)
            lane = lax.broadcasted_iota(jnp.int32, (CHUNK, LANES), 1)
            first = lane < SSD_P
            dcs_all = jnp.zeros((CHUNK, LANES), F32)
            ddt_all = jnp.zeros((CHUNK, LANES), F32)
            dtot_all = jnp.zeros((1, LANES), F32)
            for g in range(SSD_G):
                Bg = _bf(bm_ref[0, :, g * SSD_N:(g + 1) * SSD_N])
                Cg = _bf(cm_ref[0, :, g * SSD_N:(g + 1) * SSD_N])
                G = _dot(Cg, Bg, NT)
                Gt = _dot(Bg, Cg, NT)
                dG = jnp.zeros((CHUNK, CHUNK), F32)
                dB = jnp.zeros((CHUNK, SSD_N), F32)
                dC = jnp.zeros((CHUNK, SSD_N), F32)
                for pp in range(N_PAIR // SSD_G):
                    pi = g * (N_PAIR // SSD_G) + pp
                    j0, j1 = off + 2 * pi, off + 2 * pi + 1
                    c0, r0, d0, t0, L0 = head_terms(prep, j0)
                    c1, r1, d1, t1, L1 = head_terms(prep, j1)
                    Lt0 = jnp.exp(jnp.where(later_t, r0 - c0, -1e30))
                    Lt1 = jnp.exp(jnp.where(later_t, r1 - c1, -1e30))
                    xv = x_ref[0, :, pi * LANES:(pi + 1) * LANES]
                    dtp = jnp.where(first, d0, d1)
                    xd = xv * dtp
                    xdb = _bf(xd)
                    dyv = dy_ref[0, :, pi * LANES:(pi + 1) * LANES]
                    dyb = _bf(dyv)
                    dec = jnp.where(first, jnp.exp(t0 - c0), jnp.exp(t1 - c1))
                    ecs = jnp.where(first, jnp.exp(c0), jnp.exp(c1))
                    et0, et1 = jnp.exp(t0), jnp.exp(t1)
                    etot = jnp.where(first[:1], et0, et1)
                    h_prev = hs_ref[0, 0, pi]
                    hpb = _bf(h_prev)
                    dhn = dst_ref[pi]
                    dhb = _bf(dhn)
                    W0, W1 = G * L0, G * L1
                    Wt0, Wt1 = Gt * Lt0, Gt * Lt1
                    bdh = _dot(Bg, dhb, NN)
                    dxd = jnp.where(first, _dot(_bf(Wt0), dyb, NN), _dot(_bf(Wt1), dyb, NN)) + bdh * dec
                    dy0 = _bf(jnp.where(first, dyv, 0.0))
                    dy1 = _bf(jnp.where(first, 0.0, dyv))
                    Q0, Q1 = _dot(dy0, xdb, NT), _dot(dy1, xdb, NT)
                    Qt0, Qt1 = _dot(xdb, dy0, NT), _dot(xdb, dy1, NT)
                    dG = dG + Q0 * L0 + Q1 * L1
                    dcs0 = (jnp.sum(Q0 * W0, axis=1, keepdims=True)
                            - jnp.sum(Qt0 * Wt0, axis=1, keepdims=True))
                    dcs1 = (jnp.sum(Q1 * W1, axis=1, keepdims=True)
                            - jnp.sum(Qt1 * Wt1, axis=1, keepdims=True))
                    dye = dyv * ecs
                    dyeb = _bf(dye)
                    s0, s1 = _head_sum(dye * _dot(Cg, hpb, NN), first)
                    dcs0, dcs1 = dcs0 + s0, dcs1 + s1
                    dC = dC + _dot(dyeb, hpb, NT)
                    dB = dB + _dot(_bf(xd * dec), dhb, NT)
                    u0, u1 = _head_sum(xd * bdh * dec, first)
                    dcs0, dcs1 = dcs0 - u0, dcs1 - u1
                    w = jnp.sum(dhn * h_prev, axis=0, keepdims=True)
                    w0, w1 = _head_sum(w, first[:1])
                    dt0 = jnp.sum(u0, axis=0, keepdims=True) + et0 * w0
                    dt1 = jnp.sum(u1, axis=0, keepdims=True) + et1 * w1
                    dst_ref[pi] = _dot(Cg, dyeb, TN) + dhn * etot
                    q0, q1 = _head_sum(dxd * xv, first)
                    dxbc_ref[0, :, pi * LANES:(pi + 1) * LANES] = dxd * dtp
                    dcs_all = dcs_all + jnp.where(lane == j0, dcs0, 0.0) + jnp.where(lane == j1, dcs1, 0.0)
                    ddt_all = ddt_all + jnp.where(lane == j0, q0, 0.0) + jnp.where(lane == j1, q1, 0.0)
                    dtot_all = (dtot_all + jnp.where(lane[:1] == j0, dt0, 0.0)
                                + jnp.where(lane[:1] == j1, dt1, 0.0))
                dGb = _bf(dG)
                dC = dC + _dot(dGb, Bg, NN)
                dB = dB + _dot(dGb, Cg, TN)
                dxbc_ref[0, :, SSD_INNER + g * SSD_N:SSD_INNER + (g + 1) * SSD_N] = dB
                dxbc_ref[0, :, SSD_INNER + (SSD_G + g) * SSD_N:SSD_INNER + (SSD_G + g + 1) * SSD_N] = dC
            da = _dotf(later_t.astype(F32), dcs_all) + dtot_all
            ddt = ddt_all + da * arow_ref[...]
            da_ref[0] += jnp.sum(da * dt, axis=0, keepdims=True)
            draw_ref[0] = ddt * _sigmoid(raw_v + brow_ref[...])

        def rchunk(c):
            return c if rev else (NC - 1 - c)

        sp = specs(rchunk)
        full = pl.BlockSpec((1, CHUNK, SSD_CONV_DIM), lambda b, c: (b, rchunk(c), 0))
        return pl.pallas_call(
            body, name=name + "_bwd", grid=(B, NC),
            in_specs=[sp["x"], sp["bm"], sp["cm"], sp["raw"], sp["raw_t"], sp["row"], sp["row"],
                      sp["colv"], sp["colv"], sp["hs"], sp["x"]],
            out_specs=[full, sp["raw"], pl.BlockSpec((1, 1, LANES), lambda b, c: (b, 0, 0))],
            out_shape=[jax.ShapeDtypeStruct((B, S, SSD_CONV_DIM), F32),
                       jax.ShapeDtypeStruct((B, S, LANES), F32),
                       jax.ShapeDtypeStruct((B, 1, LANES), F32)],
            scratch_shapes=[pltpu.VMEM((N_PAIR, SSD_N, LANES), F32)],
            compiler_params=_cp(),
        )(xbc, xbc, xbc, raw, raw_t, brow, arow, bcol, acol, hs, dy)

    def aux(raw, brow, arow):
        raw_t = jnp.swapaxes(raw[:, :, :n_dt], 1, 2)
        return raw_t, brow[0, :n_dt][:, None], arow[0, :n_dt][:, None]

    @jax.custom_vjp
    def f(xbc, raw, brow, arow):
        raw_t, bcol, acol = aux(raw, brow, arow)
        return fwd_call(xbc, raw, raw_t, brow, arow, bcol, acol)[0]

    def fwd(xbc, raw, brow, arow):
        raw_t, bcol, acol = aux(raw, brow, arow)
        y, hs = fwd_call(xbc, raw, raw_t, brow, arow, bcol, acol)
        return y, (xbc, raw, brow, arow, hs)

    def bwd(res, dy):
        xbc, raw, brow, arow, hs = res
        raw_t, bcol, acol = aux(raw, brow, arow)
        dxbc, draw, da = bwd_call(xbc, raw, raw_t, brow, arow, bcol, acol, hs, dy)
        dbrow = jnp.sum(draw, axis=(0, 1))[None, :]
        return dxbc, draw, dbrow, jnp.sum(da, axis=0)

    f.defvjp(fwd, bwd)
    return f(xbc, raw, brow, arow)


def ssd_out(yf, yb, xbc, z, dsk, g, *, name):
    T, W = yf.shape
    gs = W // SSD_G
    tr = _tile(T, 512, 8)
    row = pl.BlockSpec((tr, W), lambda i: (i, 0))
    vec = pl.BlockSpec((1, W), lambda i: (0, 0))

    def normed(yv, gv):
        outs, rs = [], []
        for i in range(SSD_G):
            sl = slice(i * gs, (i + 1) * gs)
            r = lax.rsqrt(jnp.mean(yv[:, sl] * yv[:, sl], axis=-1, keepdims=True) + EPS)
            rs.append(r)
            outs.append(yv[:, sl] * r)
        return outs, rs

    def fwd_call(yf, yb, xbc, z, dsk, g):
        def body(yf_ref, yb_ref, xs_ref, z_ref, dsk_ref, g_ref, o_ref):
            zv = z_ref[...]
            yv = (yf_ref[...] + yb_ref[...] + xs_ref[...] * dsk_ref[...]) * (zv * _sigmoid(zv))
            outs, _ = normed(yv, g_ref[...])
            for i in range(SSD_G):
                sl = slice(i * gs, (i + 1) * gs)
                o_ref[:, sl] = (outs[i] * g_ref[:, sl]).astype(o_ref.dtype)

        return pl.pallas_call(
            body, name=name + "_fwd", grid=(T // tr,), in_specs=[row, row, row, row, vec, vec],
            out_specs=row, out_shape=jax.ShapeDtypeStruct((T, W), BF16), compiler_params=_cp(),
        )(yf, yb, xbc, z, dsk, g)

    def bwd_call(yf, yb, xbc, z, dsk, g, do):
        def body(yf_ref, yb_ref, xs_ref, z_ref, dsk_ref, g_ref, do_ref, dy_ref, dxs_ref, dz_ref,
                 ddsk_ref, dg_ref):
            @pl.when(pl.program_id(0) == 0)
            def _():
                ddsk_ref[...] = jnp.zeros_like(ddsk_ref)
                dg_ref[...] = jnp.zeros_like(dg_ref)

            zv = z_ref[...]
            sg = _sigmoid(zv)
            sz = zv * sg
            xs = xs_ref[...]
            pre = yf_ref[...] + yb_ref[...] + xs * dsk_ref[...]
            yv = pre * sz
            outs, rs = normed(yv, g_ref[...])
            for i in range(SSD_G):
                sl = slice(i * gs, (i + 1) * gs)
                dov = do_ref[:, sl].astype(F32)
                xn = outs[i]
                dxn = dov * g_ref[:, sl]
                dyv = rs[i] * (dxn - xn * jnp.mean(dxn * xn, axis=-1, keepdims=True))
                dg_ref[:, sl] += jnp.sum(dov * xn, axis=0, keepdims=True)
                dpre = dyv * sz[:, sl]
                dy_ref[:, sl] = dpre
                dxs_ref[:, sl] = dpre * dsk_ref[:, sl]
                ddsk_ref[:, sl] += jnp.sum(dpre * xs[:, sl], axis=0, keepdims=True)
                dz_ref[:, sl] = dyv * pre[:, sl] * (sg[:, sl] * (1.0 + zv[:, sl] * (1.0 - sg[:, sl])))

        o = jax.ShapeDtypeStruct((T, W), F32)
        v = jax.ShapeDtypeStruct((1, W), F32)
        return pl.pallas_call(
            body, name=name + "_bwd", grid=(T // tr,), in_specs=[row, row, row, row, vec, vec, row],
            out_specs=[row, row, row, vec, vec], out_shape=[o, o, o, v, v], compiler_params=_cp(),
        )(yf, yb, xbc, z, dsk, g, do)

    @jax.custom_vjp
    def f(yf, yb, xbc, z, dsk, g):
        return fwd_call(yf, yb, xbc, z, dsk, g)

    def fwd(yf, yb, xbc, z, dsk, g):
        return fwd_call(yf, yb, xbc, z, dsk, g), (yf, yb, xbc, z, dsk, g)

    def bwd(res, do):
        dy, dxs, dz, ddsk, dg = bwd_call(*res, do)
        dxbc = jnp.pad(dxs, ((0, 0), (0, res[2].shape[1] - W)))
        return dy, dy, dxbc, dz, ddsk, dg

    f.defvjp(fwd, bwd)
    return f(yf, yb, xbc, z, dsk, g)


def swiglu(gu, *, name):
    T, F2 = gu.shape
    Fh = F2 // 2
    tr, tf = _tile(T, 512, 8), _tile(Fh, 512)
    nf = Fh // tf
    gs = pl.BlockSpec((tr, tf), lambda i, j: (i, j))
    us = pl.BlockSpec((tr, tf), lambda i, j: (i, j + nf))

    def fwd_call(gu):
        def body(g_ref, u_ref, o_ref):
            gv = g_ref[...].astype(F32)
            o_ref[...] = (gv * _sigmoid(gv) * u_ref[...].astype(F32)).astype(o_ref.dtype)

        return pl.pallas_call(
            body, name=name + "_fwd", grid=(T // tr, nf), in_specs=[gs, us], out_specs=gs,
            out_shape=jax.ShapeDtypeStruct((T, Fh), BF16), compiler_params=_cp(),
        )(gu, gu)

    def bwd_call(gu, da):
        def body(g_ref, u_ref, da_ref, dgu_ref):
            j = pl.program_id(1)
            gv = g_ref[...].astype(F32)
            uv = u_ref[...].astype(F32)
            dav = da_ref[...].astype(F32)
            sg = _sigmoid(gv)

            @pl.when(j < nf)
            def _():
                dgu_ref[...] = (dav * uv * (sg * (1.0 + gv * (1.0 - sg)))).astype(dgu_ref.dtype)

            @pl.when(j >= nf)
            def _():
                dgu_ref[...] = (dav * gv * sg).astype(dgu_ref.dtype)

        gsel = pl.BlockSpec((tr, tf), lambda i, j: (i, j % nf))
        usel = pl.BlockSpec((tr, tf), lambda i, j: (i, j % nf + nf))
        return pl.pallas_call(
            body, name=name + "_bwd", grid=(T // tr, 2 * nf), in_specs=[gsel, usel, gsel],
            out_specs=pl.BlockSpec((tr, tf), lambda i, j: (i, j)),
            out_shape=jax.ShapeDtypeStruct((T, F2), BF16), compiler_params=_cp(),
        )(gu, gu, da)

    @jax.custom_vjp
    def f(gu):
        return fwd_call(gu)

    def fwd(gu):
        return fwd_call(gu), gu

    def bwd(gu, da):
        return (bwd_call(gu, da),)

    f.defvjp(fwd, bwd)
    return f(gu)


def gated_residual(x, gate, y, *, name):
    B, S, D = x.shape
    ts = _tile(S, 256, 8)
    row = pl.BlockSpec((1, ts, D), lambda b, j: (b, j, 0))
    per_b = pl.BlockSpec((1, 1, D), lambda b, j: (b, 0, 0))

    def fwd_call(x, gate, y):
        def body(x_ref, gt_ref, y_ref, o_ref):
            o_ref[0] = x_ref[0] + gt_ref[0] * y_ref[0]

        return pl.pallas_call(
            body, name=name + "_fwd", grid=(B, S // ts), in_specs=[row, per_b, row], out_specs=row,
            out_shape=jax.ShapeDtypeStruct((B, S, D), F32), compiler_params=_cp(),
        )(x, gate, y)

    def bwd_call(gate, y, g):
        def body(gt_ref, y_ref, g_ref, dy_ref, dgt_ref):
            @pl.when(pl.program_id(1) == 0)
            def _():
                dgt_ref[...] = jnp.zeros_like(dgt_ref)

            gv = g_ref[0]
            dy_ref[0] = gt_ref[0] * gv
            dgt_ref[0] += jnp.sum(gv * y_ref[0], axis=0, keepdims=True)

        return pl.pallas_call(
            body, name=name + "_bwd", grid=(B, S // ts), in_specs=[per_b, row, row],
            out_specs=[row, per_b],
            out_shape=[jax.ShapeDtypeStruct((B, S, D), F32), jax.ShapeDtypeStruct((B, 1, D), F32)],
            compiler_params=_cp(),
        )(gate, y, g)

    @jax.custom_vjp
    def f(x, gate, y):
        return fwd_call(x, gate, y)

    def fwd(x, gate, y):
        return fwd_call(x, gate, y), (gate, y)

    def bwd(res, g):
        dy, dgate = bwd_call(res[0], res[1], g)
        return g, dgate, dy

    f.defvjp(fwd, bwd)
    return f(x, gate, y)


def final_loss(x, g, target, *, name):
    T, D = x.shape
    tr = _tile(T, 256, 8)
    row = pl.BlockSpec((tr, D), lambda i: (i, 0))
    vec = pl.BlockSpec((1, D), lambda i: (0, 0))

    def fwd_call(x, g, target):
        def body(x_ref, g_ref, t_ref, o_ref):
            @pl.when(pl.program_id(0) == 0)
            def _():
                o_ref[...] = jnp.zeros_like(o_ref)

            xv = x_ref[...]
            r = lax.rsqrt(jnp.mean(xv * xv, axis=-1, keepdims=True) + EPS)
            e = xv * r * g_ref[...] - t_ref[...]
            o_ref[...] += jnp.sum(e * e, axis=0, keepdims=True)

        part = pl.pallas_call(
            body, name=name + "_fwd", grid=(T // tr,), in_specs=[row, vec, row], out_specs=vec,
            out_shape=jax.ShapeDtypeStruct((1, D), F32), compiler_params=_cp(),
        )(x, g, target)
        return (0.5 / D) * jnp.sum(part)

    def bwd_call(x, g, target, ct):
        def body(x_ref, g_ref, t_ref, ct_ref, dx_ref, dg_ref):
            @pl.when(pl.program_id(0) == 0)
            def _():
                dg_ref[...] = jnp.zeros_like(dg_ref)

            xv = x_ref[...]
            gv = g_ref[...]
            r = lax.rsqrt(jnp.mean(xv * xv, axis=-1, keepdims=True) + EPS)
            xn = xv * r
            dy = (xn * gv - t_ref[...]) * (ct_ref[...] * (1.0 / D))
            dxn = dy * gv
            dx_ref[...] = r * (dxn - xn * jnp.mean(dxn * xn, axis=-1, keepdims=True))
            dg_ref[...] += jnp.sum(dy * xn, axis=0, keepdims=True)

        return pl.pallas_call(
            body, name=name + "_bwd", grid=(T // tr,),
            in_specs=[row, vec, row, pl.BlockSpec((1, 1), lambda i: (0, 0))], out_specs=[row, vec],
            out_shape=[jax.ShapeDtypeStruct((T, D), F32), jax.ShapeDtypeStruct((1, D), F32)],
            compiler_params=_cp(),
        )(x, g, target, ct)

    @jax.custom_vjp
    def f(x, g, target):
        return fwd_call(x, g, target)

    def fwd(x, g, target):
        return fwd_call(x, g, target), (x, g, target)

    def bwd(res, ct):
        x, g, target = res
        dx, dg = bwd_call(x, g, target, jnp.reshape(ct, (1, 1)).astype(F32))
        return dx, dg, jnp.zeros_like(target)

    f.defvjp(fwd, bwd)
    return f(x, g, target)


def adamw(w, g, m, v, *, name):
    R, C = w.shape
    tr = _tile(R, 512, 8)
    spec = pl.BlockSpec((tr, C), lambda i: (i, 0))
    c1 = 1.0 / (1.0 - ADAM_B1 ** ADAM_STEP)
    c2 = 1.0 / (1.0 - ADAM_B2 ** ADAM_STEP)

    def body(w_ref, g_ref, m_ref, v_ref, d_ref, nm_ref, nv_ref):
        gv = g_ref[...]
        nm = ADAM_B1 * m_ref[...] + (1.0 - ADAM_B1) * gv
        nv = ADAM_B2 * v_ref[...] + (1.0 - ADAM_B2) * (gv * gv)
        nm_ref[...] = nm
        nv_ref[...] = nv
        d_ref[...] = -ADAM_LR * ((nm * c1) / (jnp.sqrt(nv * c2) + ADAM_EPS) + ADAM_WD * w_ref[...])

    o = jax.ShapeDtypeStruct((R, C), F32)
    return pl.pallas_call(
        body, name=name, grid=(R // tr,), in_specs=[spec] * 4, out_specs=[spec] * 3,
        out_shape=[o, o, o], compiler_params=_cp(),
    )(w, g, m, v)


def _position():
    x, y, c = lax.axis_index("x"), lax.axis_index("y"), lax.axis_index("c")
    return x, y, c


def all_gather(shard, *, name, cols=False):
    R, C = shard.shape
    assert not cols or C % LANES == 0

    def body(x_ref, out_ref, send_sems, recv_sems, local_sem):
        x, y, c = _position()
        me, sibling = (x, y, c), (x, y, 1 - c)
        chips = [(1 - x, y), (x, 1 - y), (1 - x, 1 - y)]

        def block(px, py, pc):
            idx = 4 * px + 2 * py + pc
            if cols:
                return out_ref.at[:, pl.ds(pl.multiple_of(idx * C, LANES), C)]
            return out_ref.at[idx]

        def copy(k, blk, to, src=None):
            return pltpu.make_async_remote_copy(
                src_ref=block(*blk) if src is None else src, dst_ref=block(*blk),
                send_sem=send_sems.at[k], recv_sem=recv_sems.at[k],
                device_id=to, device_id_type=pl.DeviceIdType.MESH)

        mine = pltpu.make_async_copy(x_ref, block(*me), local_sem)
        mine.start()
        first = [copy(0, me, sibling, src=x_ref)]
        first += [copy(1 + j, me, (*chip, c), src=x_ref) for j, chip in enumerate(chips)]
        for cp in first:
            cp.start()
        passed = [copy(4 + j, (*chip, c), sibling) for j, chip in enumerate(chips)]
        for j, chip in enumerate(chips):
            copy(1 + j, (*chip, c), me).wait_recv()
            passed[j].start()
        copy(0, sibling, me).wait_recv()
        for j, chip in enumerate(chips):
            copy(4 + j, (*chip, 1 - c), me).wait_recv()
        for cp in first + passed:
            cp.wait_send()
        mine.wait()

    return pl.pallas_call(
        body, name=name,
        out_shape=jax.ShapeDtypeStruct((R, N_DEV * C) if cols else (N_DEV, R, C), shard.dtype),
        in_specs=[pl.BlockSpec(memory_space=pl.ANY)], out_specs=pl.BlockSpec(memory_space=pl.ANY),
        scratch_shapes=[pltpu.SemaphoreType.DMA((7,)), pltpu.SemaphoreType.DMA((7,)),
                        pltpu.SemaphoreType.DMA],
    )(shard)


N_CHIP = 4


def rs_pair_routine(g, cols):
    if cols:
        R, C = g.shape[0], g.shape[1] // N_DEV
        assert C % LANES == 0
    else:
        _, R, C = g.shape

    def blk(ref, idx):
        if cols:
            return ref.at[:, pl.ds(pl.multiple_of(idx * C, LANES), C)]
        return ref.at[idx]

    def copies(ins, outs, send_sems, recv_sems, local_sems):
        (g_ref,), (got_ref,) = ins, outs
        x, y, c = _position()
        local, remote = [], []
        for q in range(N_CHIP):
            remote.append(pltpu.make_async_remote_copy(
                src_ref=blk(g_ref, 2 * q + 1 - c), dst_ref=got_ref.at[q],
                send_sem=send_sems.at[q], recv_sem=recv_sems.at[q],
                device_id=(x, y, 1 - c), device_id_type=pl.DeviceIdType.MESH))
        return local, remote

    def start(*refs):
        local, remote = copies(*refs)
        for cp in remote + local:
            cp.start()

    def finish(*refs):
        local, remote = copies(*refs)
        for cp in remote:
            cp.wait_recv()
        for cp in remote:
            cp.wait_send()
        for cp in local:
            cp.wait()

    o = jax.ShapeDtypeStruct((N_CHIP, R, C), g.dtype)
    return dict(ins=[g], outs=[o], n_sem=N_CHIP, n_local=1, start=start, finish=finish)


def rs_chip_routine(h):
    _, R, C = h.shape
    RELATIONS = ((0, 1), (1, 0), (1, 1))

    def copies(ins, outs, send_sems, recv_sems, local_sems):
        (h_ref,), (out_ref,) = ins, outs
        x, y, c = _position()
        local, remote = [], []
        for k, (fx, fy) in enumerate(RELATIONS):
            px = (1 - x) if fx else x
            py = (1 - y) if fy else y
            remote.append(pltpu.make_async_remote_copy(
                src_ref=h_ref.at[2 * px + py], dst_ref=out_ref.at[k],
                send_sem=send_sems.at[k], recv_sem=recv_sems.at[k],
                device_id=(px, py, c), device_id_type=pl.DeviceIdType.MESH))
        return local, remote

    def start(*refs):
        local, remote = copies(*refs)
        for cp in remote + local:
            cp.start()

    def finish(*refs):
        local, remote = copies(*refs)
        for cp in remote:
            cp.wait_recv()
        for cp in remote:
            cp.wait_send()
        for cp in local:
            cp.wait()

    return dict(ins=[h], outs=[jax.ShapeDtypeStruct((N_CHIP - 1, R, C), h.dtype)], n_sem=3, n_local=1,
                start=start, finish=finish)


def comm_call(routine, *, name):
    n_in, n_out = len(routine["ins"]), len(routine["outs"])

    def body(*refs):
        ins, outs, sems = refs[:n_in], refs[n_in:n_in + n_out], refs[n_in + n_out:]
        routine["start"](ins, outs, *sems)
        routine["finish"](ins, outs, *sems)

    any_spec = pl.BlockSpec(memory_space=pl.ANY)
    return pl.pallas_call(
        body, name=name, out_shape=routine["outs"],
        in_specs=[any_spec] * n_in, out_specs=[any_spec] * n_out,
        scratch_shapes=[pltpu.SemaphoreType.DMA((routine["n_sem"],)), pltpu.SemaphoreType.DMA((routine["n_sem"],)),
                        pltpu.SemaphoreType.DMA((routine["n_local"],))],
    )(*routine["ins"])


def add_own(g, got, core, *, cols, name):
    n, R, C = got.shape
    tr = _tile(R, 256, 8)
    if cols:
        gspec = pl.BlockSpec((tr, C), lambda q, i, c_ref: (i, 2 * q + c_ref[0]))
    else:
        gspec = pl.BlockSpec((1, tr, C), lambda q, i, c_ref: (2 * q + c_ref[0], i, 0))
    spec = pl.BlockSpec((1, tr, C), lambda q, i, c_ref: (q, i, 0))

    def body(c_ref, g_ref, b_ref, o_ref):
        gv = g_ref[...] if cols else g_ref[0]
        o_ref[0] = (gv.astype(F32) + b_ref[0].astype(F32)).astype(o_ref.dtype)

    return pl.pallas_call(
        body, name=name,
        grid_spec=pltpu.PrefetchScalarGridSpec(num_scalar_prefetch=1, grid=(n, R // tr),
                                               in_specs=[gspec, spec], out_specs=spec),
        out_shape=jax.ShapeDtypeStruct((n, R, C), g.dtype), compiler_params=_cp(),
    )(core, g, got)


def sum_chips(h, got, chip, *, name):
    _, R, C = h.shape
    tr = _tile(R, 256, 8)

    def body(q_ref, h_ref, g_ref, o_ref):
        acc = h_ref[0].astype(F32)
        for k in range(N_CHIP - 1):
            acc = acc + g_ref[k].astype(F32)
        o_ref[...] = acc

    return pl.pallas_call(
        body, name=name,
        grid_spec=pltpu.PrefetchScalarGridSpec(
            num_scalar_prefetch=1, grid=(R // tr,),
            in_specs=[pl.BlockSpec((1, tr, C), lambda i, q_ref: (q_ref[0], i, 0)),
                      pl.BlockSpec((N_CHIP - 1, tr, C), lambda i, q_ref: (0, i, 0))],
            out_specs=pl.BlockSpec((tr, C), lambda i, q_ref: (i, 0))),
        out_shape=jax.ShapeDtypeStruct((R, C), F32), compiler_params=_cp(),
    )(chip, h, got)


def reduce_scatter(g, *, cols, name):
    x, y, c = _position()
    core = jnp.reshape(c, (1,)).astype(jnp.int32)
    chip = jnp.reshape(2 * x + y, (1,)).astype(jnp.int32)
    got = comm_call(rs_pair_routine(g, cols), name=name + "_pair")[0]
    h = add_own(g, got, core, cols=cols, name=name + "_add")
    return sum_chips(h, comm_call(rs_chip_routine(h), name=name + "_chip")[0], chip, name=name + "_sum")


def sum_blocks(stack, *, name):
    n, R, C = stack.shape
    tr = _tile(R, 256, 8)

    def body(x_ref, o_ref):
        acc = x_ref[0].astype(F32)
        for i in range(1, n):
            acc = acc + x_ref[i].astype(F32)
        o_ref[...] = acc

    return pl.pallas_call(
        body, name=name, grid=(R // tr,),
        in_specs=[pl.BlockSpec((n, tr, C), lambda i: (0, i, 0))],
        out_specs=pl.BlockSpec((tr, C), lambda i: (i, 0)),
        out_shape=jax.ShapeDtypeStruct((R, C), F32), compiler_params=_cp(),
    )(stack)


PACK_COLS = 1024
PACK_ROW_MULT = 8


def _pack(arrays, dtype):
    flat = jnp.concatenate([a.reshape(-1).astype(dtype) for a in arrays])
    n = flat.shape[0]
    unit = PACK_COLS * PACK_ROW_MULT
    padded = -(-n // unit) * unit
    return jnp.pad(flat, (0, padded - n)).reshape(padded // PACK_COLS, PACK_COLS)


def _unpack(packed, shapes):
    flat = packed.reshape(-1)
    out, o = [], 0
    for s in shapes:
        n = int(np.prod(s))
        out.append(flat[o:o + n].reshape(s))
        o += n
    return out


def fsdp_cols(shard, *, name):
    K, n = shard.shape
    npad = -(-n // LANES) * LANES

    @jax.custom_vjp
    def f(p):
        p = jnp.pad(p, ((0, 0), (0, npad - n))) if npad != n else p
        return all_gather(p.astype(BF16), cols=True, name=name + "_ag")

    def fwd(p):
        return f(p), None

    def bwd(_, g):
        d = reduce_scatter(g, cols=True, name=name + "_rs")
        return (d[:, :n] if npad != n else d,)

    f.defvjp(fwd, bwd)
    return f(shard)


def fsdp_rows(shard, *, name):
    k, N = shard.shape

    @jax.custom_vjp
    def f(p):
        return all_gather(p.astype(BF16), name=name + "_ag").reshape(N_DEV * k, N)

    def fwd(p):
        return f(p), None

    def bwd(_, g):
        return (reduce_scatter(g.reshape(N_DEV, k, N), cols=False, name=name + "_rs"),)

    f.defvjp(fwd, bwd)
    return f(shard)


def _unpad_cols(w, n):
    K = w.shape[0]
    npad = w.shape[1] // N_DEV
    if npad == n:
        return w
    return w.reshape(K, N_DEV, npad)[:, :, :n].reshape(K, N_DEV * n)


def gather_rows(part, me, *, name):
    rows, n = part.shape
    per = rows // N_DEV

    @jax.custom_vjp
    def f(part):
        full = all_gather(part, name=name + "_fwd")
        mine = lax.dynamic_slice_in_dim(full, me * per, per, axis=1)
        return jnp.swapaxes(mine, 0, 1).reshape(per, N_DEV * n)

    def fwd(part):
        return f(part), None

    def bwd(_, g):
        full = all_gather(g, name=name + "_bwd")
        mine = lax.dynamic_slice_in_dim(full, me * n, n, axis=2)
        return (mine.reshape(rows, n),)

    f.defvjp(fwd, bwd)
    return f(part)


def _seg_layout():
    offs = np.concatenate([[0], np.cumsum(IN_SPLITS)])
    cols, widths = [], []
    for s in SEG_ORDER:
        cols.append((int(offs[s]), int(offs[s + 1])))
        widths.append(SEG_PAD.get(s, IN_SPLITS[s]))
    return cols, widths


def _arrange_w_in(w):
    cols, widths = _seg_layout()
    parts = []
    for (a, b), wd in zip(cols, widths):
        seg = w[:, a:b]
        if wd != b - a:
            seg = jnp.pad(seg, ((0, 0), (0, wd - (b - a))))
        parts.append(seg)
    parts.append(jnp.zeros((w.shape[0], IN_WIDTH - sum(widths)), w.dtype))
    return jnp.concatenate(parts, axis=1)


def split_cols(proj, widths):
    @jax.custom_vjp
    def f(p):
        outs, o = [], 0
        for wd in widths:
            outs.append(p[:, o:o + wd])
            o += wd
        return tuple(outs)

    def fwd(p):
        return f(p), None

    def bwd(_, gs):
        rest = proj.shape[1] - sum(widths)
        tail = [jnp.zeros((proj.shape[0], rest), proj.dtype)] if rest else []
        return (jnp.concatenate(list(gs) + tail, axis=1),)

    f.defvjp(fwd, bwd)
    return f(proj)


BIG = ("w_in", "w_uq", "w_ukv", "conv_w", "w_out", "w_gate_up", "w_down")
SMALL = ("b_ada", "norm1_g", "norm2_g", "q_norm_g", "k_norm_g", "mla_q_norm_g", "mla_kv_norm_g",
         "conv_b", "dt_bias", "a_log", "d_skip", "ssd_norm_g", "final_norm_g")
WEIGHTS = ("w_ada", "b_ada", "norm1_g", "norm2_g", "w_in", "q_norm_g", "k_norm_g", "mla_q_norm_g",
           "w_uq", "mla_kv_norm_g", "w_ukv", "conv_w", "conv_b", "dt_bias", "a_log", "d_skip",
           "ssd_norm_g", "w_out", "w_gate_up", "w_down", "final_norm_g")


def _layer(l, x, mod, W, P, tabs):
    B, S, D = x.shape
    T = B * S
    nm = f"l{l}_"
    shift1, scale1, gate1, shift2, scale2, gate2 = [m[:, None, :] for m in jnp.split(mod, 6, axis=-1)]
    cos_a, sin_a, cos_b, sin_b, cos_k, sin_k = tabs

    h = rmsmod(x, P["norm1_g"][l][None], scale1, shift1, name=nm + "norm1")
    w_in = _arrange_w_in(W["w_in"])
    proj = linear(h.reshape(T, D), w_in, out_dtype=F32, name=nm + "in")
    q_a, k_a, v_a, cq, ckv, z, xbc, kpe, dtr = split_cols(proj, _seg_layout()[1])

    qn = group_rmsnorm(q_a, jnp.tile(P["q_norm_g"][l], GQA_H)[None], gs=HEAD, out_dtype=F32, name=nm + "qnorm")
    kn = group_rmsnorm(k_a, jnp.tile(P["k_norm_g"][l], GQA_KV)[None], gs=HEAD, out_dtype=F32, name=nm + "knorm")
    qr = rope(qn.reshape(B, S, -1), cos_a[:, :GQA_H * HEAD], sin_a[:, :GQA_H * HEAD], d=HEAD // 4, name=nm + "qrope")
    kr = rope(kn.reshape(B, S, -1), cos_a[:, :GQA_KV * HEAD], sin_a[:, :GQA_KV * HEAD], d=HEAD // 4, name=nm + "krope")
    o_a = attention(qr, kr, v_a.reshape(B, S, -1), H=GQA_H, Hkv=GQA_KV, dk=HEAD, dv=HEAD,
                    scale=HEAD ** -0.5, name=nm + "gqa")

    w_uq = W["w_uq"].reshape(MLA_QL, MLA_H, MLA_NOPE + MLA_ROPE)
    w_uq = jnp.concatenate([w_uq[:, :, :MLA_NOPE].reshape(MLA_QL, -1), w_uq[:, :, MLA_NOPE:].reshape(MLA_QL, -1)], axis=1)
    w_ukv = W["w_ukv"].reshape(MLA_KVL, MLA_H, MLA_NOPE + MLA_V)
    w_ukv = jnp.concatenate([w_ukv[:, :, :MLA_NOPE].reshape(MLA_KVL, -1), w_ukv[:, :, MLA_NOPE:].reshape(MLA_KVL, -1)], axis=1)
    cqn = group_rmsnorm(cq, P["mla_q_norm_g"][l][None], gs=MLA_QL, out_dtype=BF16, name=nm + "cqnorm")
    ckvn = group_rmsnorm(ckv, P["mla_kv_norm_g"][l][None], gs=MLA_KVL, out_dtype=BF16, name=nm + "ckvnorm")
    qb = linear(cqn, w_uq, out_dtype=F32, name=nm + "uq")
    kvb = linear(ckvn, w_ukv, out_dtype=F32, name=nm + "ukv")
    q_nope, q_pe = split_cols(qb, (MLA_H * MLA_NOPE, MLA_H * MLA_ROPE))
    k_nope, v_b = split_cols(kvb, (MLA_H * MLA_NOPE, MLA_H * MLA_V))
    q_pe = rope(q_pe.reshape(B, S, -1), cos_b, sin_b, d=MLA_ROPE // 4, name=nm + "qpe_rope")
    k_pe = rope(kpe.reshape(B, S, -1), cos_k, sin_k, d=MLA_ROPE // 4, name=nm + "kpe_rope")[:, :, :MLA_ROPE]
    zpad = jnp.zeros((B, S, MLA_H, MLA_DK - MLA_NOPE - MLA_ROPE), F32)
    q_cat = jnp.concatenate([q_nope.reshape(B, S, MLA_H, MLA_NOPE), q_pe.reshape(B, S, MLA_H, MLA_ROPE), zpad],
                            axis=-1).reshape(B, S, MLA_H * MLA_DK)
    k_cat = jnp.concatenate([k_nope.reshape(B, S, MLA_H, MLA_NOPE),
                             jnp.broadcast_to(k_pe[:, :, None, :], (B, S, MLA_H, MLA_ROPE)), zpad],
                            axis=-1).reshape(B, S, MLA_H * MLA_DK)
    o_b = attention(q_cat, k_cat, v_b.reshape(B, S, -1), H=MLA_H, Hkv=MLA_H, dk=MLA_DK, dv=MLA_V,
                    scale=(MLA_NOPE + MLA_ROPE) ** -0.5, name=nm + "mla")

    xact = conv_silu(xbc.reshape(B, S, -1), W["conv_w"].astype(F32), P["conv_b"][l][None], name=nm + "conv")
    brow = jnp.pad(P["dt_bias"][l].reshape(1, -1), ((0, 0), (0, LANES - 2 * SSD_H)))
    arow = jnp.pad(-jnp.exp(P["a_log"][l].reshape(1, -1)), ((0, 0), (0, LANES - 2 * SSD_H)))
    raw = dtr.reshape(B, S, LANES)
    y_f = ssd_scan(xact, raw, brow, arow, rev=False, name=nm + "ssd_f")
    y_b = ssd_scan(xact, raw, brow, arow, rev=True, name=nm + "ssd_b")
    dsk = jnp.repeat(P["d_skip"][l], SSD_P)[None]
    o_c = ssd_out(y_f.reshape(T, -1), y_b.reshape(T, -1), xact.reshape(T, -1), z, dsk,
                  P["ssd_norm_g"][l][None], name=nm + "ssd_out")

    o = jnp.concatenate([o_a.reshape(T, -1).astype(BF16), o_b.reshape(T, -1).astype(BF16), o_c], axis=-1)
    mix = linear(o, W["w_out"], out_dtype=F32, name=nm + "out")
    x = gated_residual(x, gate1, mix.reshape(B, S, D), name=nm + "res1")

    h = rmsmod(x, P["norm2_g"][l][None], scale2, shift2, name=nm + "norm2")
    gu = linear(h.reshape(T, D), W["w_gate_up"], out_dtype=BF16, name=nm + "gate_up")
    act = swiglu(gu, name=nm + "swiglu")
    ffn = linear(act, W["w_down"], out_dtype=F32, name=nm + "down")
    return gated_residual(x, gate2, ffn.reshape(B, S, D), name=nm + "res2")


def kernel(x, c, w_ada, b_ada, norm1_g, norm2_g, w_in, q_norm_g, k_norm_g, mla_q_norm_g, w_uq, mla_kv_norm_g, w_ukv, conv_w, conv_b, dt_bias, a_log, d_skip, ssd_norm_g, w_out, w_gate_up, w_down, final_norm_g, loss_target, m_w_ada, m_b_ada, m_norm1_g, m_norm2_g, m_w_in, m_q_norm_g, m_k_norm_g, m_mla_q_norm_g, m_w_uq, m_mla_kv_norm_g, m_w_ukv, m_conv_w, m_conv_b, m_dt_bias, m_a_log, m_d_skip, m_ssd_norm_g, m_w_out, m_w_gate_up, m_w_down, m_final_norm_g, v_w_ada, v_b_ada, v_norm1_g, v_norm2_g, v_w_in, v_q_norm_g, v_k_norm_g, v_mla_q_norm_g, v_w_uq, v_mla_kv_norm_g, v_w_ukv, v_conv_w, v_conv_b, v_dt_bias, v_a_log, v_d_skip, v_ssd_norm_g, v_w_out, v_w_gate_up, v_w_down, v_final_norm_g):
    args = dict(locals())
    weights = {n: args[n] for n in WEIGHTS}
    moments_m = {n: args["m_" + n] for n in WEIGHTS}
    moments_v = {n: args["v_" + n] for n in WEIGHTS}
    B, S, D = x.shape
    L = w_ada.shape[0]
    T = B * S
    px, py, pc = _position()
    me = 4 * px + 2 * py + pc
    small_shapes = [weights[n].shape for n in SMALL]

    tabs = (*rope_tables(S, HEAD, GQA_H * HEAD), *rope_tables(S, MLA_ROPE, MLA_H * MLA_ROPE),
            *rope_tables(S, MLA_ROPE, 2 * MLA_ROPE))
    c_all = all_gather(c, name="gather_c").reshape(N_DEV * B, D)

    def local_loss(big, w_ada_s, small, x):
        P = dict(zip(SMALL, small))
        for l in range(L):
            W = {}
            for n in ("w_in", "w_uq", "w_ukv", "w_gate_up"):
                W[n] = _unpad_cols(fsdp_cols(big[n][l], name=f"l{l}_{n}"), big[n].shape[2])
            W["conv_w"] = _unpad_cols(fsdp_cols(big["conv_w"][l], name=f"l{l}_conv_w"), big["conv_w"].shape[2])
            for n in ("w_out", "w_down"):
                W[n] = fsdp_rows(big[n][l], name=f"l{l}_{n}")
            part = linear(c_all, w_ada_s[l], out_dtype=F32, a_silu=True, name=f"l{l}_ada")
            mod = gather_rows(part, me, name=f"l{l}_mod") + P["b_ada"][l][None]
            x = _layer(l, x, mod, W, P, tabs)
        return final_loss(x.reshape(T, D), P["final_norm_g"][None], loss_target.reshape(T, D), name="loss")

    big = {n: weights[n] for n in BIG}
    small = tuple(weights[n] for n in SMALL)
    loss, (g_big, g_ada, g_small, grad_x) = jax.value_and_grad(local_loss, argnums=(0, 1, 2, 3))(
        big, w_ada, small, x)
    loss = lax.psum(loss, ("x", "y", "c"))

    grads = dict(g_big)
    grads["w_ada"] = g_ada
    g_small_sum = sum_blocks(all_gather(_pack(g_small, F32), name="small_grads_ag"), name="small_grads_sum")
    grads.update(zip(SMALL, _unpack(g_small_sum, small_shapes)))

    delta, new_m, new_v = {}, {}, {}
    for n in ("w_ada",) + BIG:
        shp = weights[n].shape
        two_d = (int(np.prod(shp[:-1])), shp[-1])
        d_, m_, v_ = adamw(weights[n].reshape(two_d), grads[n].reshape(two_d), moments_m[n].reshape(two_d),
                           moments_v[n].reshape(two_d), name="adamw_" + n)
        delta[n], new_m[n], new_v[n] = d_.reshape(shp), m_.reshape(shp), v_.reshape(shp)
    d_, m_, v_ = adamw(_pack([weights[n] for n in SMALL], F32), g_small_sum,
                       _pack([moments_m[n] for n in SMALL], F32), _pack([moments_v[n] for n in SMALL], F32),
                       name="adamw_small")
    for tgt, packed in ((delta, d_), (new_m, m_), (new_v, v_)):
        tgt.update(zip(SMALL, _unpack(packed, small_shapes)))

    return (loss, grad_x, *[grads[n] for n in WEIGHTS], *[delta[n] for n in WEIGHTS],
            *[new_m[n] for n in WEIGHTS], *[new_v[n] for n in WEIGHTS])
```

```python
import functools
import math

import jax
import jax.numpy as jnp
import numpy as np
from jax import lax
from jax.experimental import pallas as pl
from jax.experimental.pallas import tpu as pltpu

F32 = jnp.float32
BF16 = jnp.bfloat16
N_DEV = 8
EPS = 1e-6
ROPE_THETA = 10000.0
GRID_W = 64

GQA_H, GQA_KV, HEAD = 6, 2, 128
MLA_H, MLA_QL, MLA_KVL, MLA_NOPE, MLA_ROPE, MLA_V = 4, 512, 256, 128, 64, 128
MLA_DK = 256
SSD_H, SSD_P, SSD_G, SSD_N, SSD_K, CHUNK = 12, 64, 2, 128, 5, 128
SSD_INNER = SSD_H * SSD_P
SSD_CONV_DIM = SSD_INNER + 2 * SSD_G * SSD_N
N_PAIR = SSD_H // 2
LANES = 128
IN_SPLITS = (768, 256, 256, 512, 256, 64, 768, 1280, 24)
IN_COLS = sum(IN_SPLITS)
SEG_ORDER = (0, 1, 2, 3, 4, 6, 7, 5, 8)
SEG_PAD = {5: 128, 8: 128}
IN_WIDTH = 4608

ADAM_LR, ADAM_B1, ADAM_B2, ADAM_EPS, ADAM_WD, ADAM_STEP = 0.001, 0.9, 0.999, 1e-08, 0.01, 10
VMEM_LIMIT = 56 * 1024 * 1024
MM_TM, MM_TN, MM_TK = 1024, 1024, 2048


def _cp(**kw):
    return pltpu.CompilerParams(vmem_limit_bytes=VMEM_LIMIT, **kw)


def _tile(dim, cap, mult=128):
    if dim <= cap:
        return dim
    best = None
    t = mult
    while t <= cap:
        if dim % t == 0:
            best = t
        t += mult
    assert best is not None, (dim, cap)
    return best


def _sigmoid(x):
    return 1.0 / (1.0 + jnp.exp(-x))


def _dot(a, b, dims):
    return lax.dot_general(a, b, (dims, ((), ())), preferred_element_type=F32)


NN = ((1,), (0,))
NT = ((1,), (1,))
TN = ((0,), (0,))


def _dotf(a, b, dims=NN):
    return lax.dot_general(a, b, (dims, ((), ())), preferred_element_type=F32,
                           precision=lax.Precision.HIGHEST)


def _bf(x):
    return x.astype(BF16)


def mm(a, b, *, ta=False, tb=False, out_dtype=F32, a_silu=False, name, carry=()):
    if ta:
        K, M = a.shape
    else:
        M, K = a.shape
    if tb:
        N, K2 = b.shape
    else:
        K2, N = b.shape
    assert K == K2, (a.shape, b.shape, ta, tb)
    tm, tn, tk = _tile(M, MM_TM), _tile(N, MM_TN), _tile(K, MM_TK)
    nk = K // tk
    dims = ((0 if ta else 1,), (1 if tb else 0,))

    def partial_product(a_ref, b_ref):
        av = a_ref[...]
        if a_silu:
            av = av.astype(F32)
            av = av * _sigmoid(av)
        return _dot(_bf(av), _bf(b_ref[...]), dims)

    def body_single(a_ref, b_ref, o_ref):
        o_ref[...] = partial_product(a_ref, b_ref).astype(o_ref.dtype)

    def body_acc(a_ref, b_ref, o_ref, acc_ref):
        k = pl.program_id(2)

        @pl.when(k == 0)
        def _():
            acc_ref[...] = partial_product(a_ref, b_ref)

        @pl.when(k > 0)
        def _():
            acc_ref[...] += partial_product(a_ref, b_ref)

        @pl.when(k == nk - 1)
        def _():
            o_ref[...] = acc_ref[...].astype(o_ref.dtype)

    body = body_single if nk == 1 else body_acc

    a_spec = (pl.BlockSpec((tk, tm), lambda i, j, k: (k, i)) if ta
              else pl.BlockSpec((tm, tk), lambda i, j, k: (i, k)))
    b_spec = (pl.BlockSpec((tn, tk), lambda i, j, k: (j, k)) if tb
              else pl.BlockSpec((tk, tn), lambda i, j, k: (k, j)))
    res = carry_call(
        body, name=name, grid=(M // tm, N // tn, nk),
        in_specs=[a_spec, b_spec],
        out_specs=[pl.BlockSpec((tm, tn), lambda i, j, k: (i, j))],
        out_shape=[jax.ShapeDtypeStruct((M, N), out_dtype)],
        scratch_shapes=[] if nk == 1 else [pltpu.VMEM((tm, tn), F32)],
        carry=carry, dims=("parallel", "parallel", "arbitrary"),
    )(a, b)
    return res if carry else res[0]


def _shard_prep(p, cols):
    if cols and p.shape[1] % LANES:
        p = jnp.pad(p, ((0, 0), (0, -p.shape[1] % LANES)))
    return p.astype(BF16)


def _full_post(full, cols):
    return full if cols else full.reshape(full.shape[0] * full.shape[1], full.shape[2])


def _rs_begin(g_full, cols, shard_shape, name):
    gg = g_full if cols else g_full.reshape(N_DEV, shard_shape[0], shard_shape[1])
    x, y, c = _position()
    got = comm_call(rs_pair_routine(gg, cols), name=name + "_pair")[0]
    return add_own(gg, got, jnp.reshape(c, (1,)).astype(jnp.int32), cols=cols, name=name + "_add")


def _rs_end(h, got, cols, shard_shape, name):
    x, y, c = _position()
    d = sum_chips(h, got, jnp.reshape(2 * x + y, (1,)).astype(jnp.int32), name=name + "_sum")
    return d[:, :shard_shape[1]] if cols else d


def linear(a, w, *, out_dtype, name, a_silu=False, carry=()):
    kinds = [(cols, s.shape, tag) for s, cols, tag in carry]

    def run(a, w, *shards):
        routines = [ag_routine(_shard_prep(s, cols), cols) for s, (cols, _, _) in zip(shards, kinds)]
        res = mm(a, w, out_dtype=out_dtype, a_silu=a_silu, name=name + "_fwd", carry=routines)
        if not routines:
            return (res,)
        return (res[0], *[_full_post(fu, cols) for fu, (cols, _, _) in zip(res[1:], kinds)])

    @jax.custom_vjp
    def f(a, w, *shards):
        return run(a, w, *shards)

    def fwd(a, w, *shards):
        return run(a, w, *shards), (a, w)

    def bwd(res, cts):
        a, w = res
        g = cts[0]
        hs = [_rs_begin(gf, cols, shp, name + "_" + tag) for gf, (cols, shp, tag) in zip(cts[1:], kinds)]
        routines = [rs_chip_routine(h) for h in hs]
        if a_silu:
            assert not routines
            da = jnp.zeros_like(a)
            gots = []
        else:
            r = mm(g, w, tb=True, out_dtype=a.dtype, name=name + "_da", carry=routines)
            da, gots = (r[0], r[1:]) if routines else (r, [])
        dw = mm(a, g, ta=True, out_dtype=w.dtype, a_silu=a_silu, name=name + "_dw")
        dsh = [_rs_end(h, got, cols, shp, name + "_" + tag)
               for h, got, (cols, shp, tag) in zip(hs, gots, kinds)]
        return (da, dw, *dsh)

    f.defvjp(fwd, bwd)
    out = f(a, w, *[s for s, _, _ in carry])
    return (out[0], list(out[1:])) if carry else out[0]


def rmsmod(x, g, scale, shift, *, name):
    B, S, D = x.shape
    ts = _tile(S, 256, 8)
    row = pl.BlockSpec((1, ts, D), lambda b, j: (b, j, 0))
    per_b = pl.BlockSpec((1, 1, D), lambda b, j: (b, 0, 0))
    gspec = pl.BlockSpec((1, D), lambda b, j: (0, 0))

    def fwd_call(x, g, scale, shift):
        def body(x_ref, g_ref, sc_ref, sh_ref, o_ref):
            xv = x_ref[0]
            r = lax.rsqrt(jnp.mean(xv * xv, axis=-1, keepdims=True) + EPS)
            y = xv * r * g_ref[...]
            o_ref[0] = (y * (1.0 + sc_ref[0]) + sh_ref[0]).astype(o_ref.dtype)

        return pl.pallas_call(
            body, name=name + "_fwd", grid=(B, S // ts),
            in_specs=[row, gspec, per_b, per_b], out_specs=row,
            out_shape=jax.ShapeDtypeStruct((B, S, D), BF16), compiler_params=_cp(),
        )(x, g, scale, shift)

    def bwd_call(x, g, scale, dh):
        def body(x_ref, g_ref, sc_ref, dh_ref, dx_ref, dg_ref, dsc_ref, dsh_ref):
            j = pl.program_id(1)
            xv = x_ref[0]
            dh = dh_ref[0].astype(F32)
            r = lax.rsqrt(jnp.mean(xv * xv, axis=-1, keepdims=True) + EPS)
            xn = xv * r
            gv = g_ref[...]
            dy = dh * (1.0 + sc_ref[0])
            dxn = dy * gv
            dx_ref[0] = r * (dxn - xn * jnp.mean(dxn * xn, axis=-1, keepdims=True))

            @pl.when(j == 0)
            def _():
                dg_ref[...] = jnp.zeros_like(dg_ref)
                dsc_ref[...] = jnp.zeros_like(dsc_ref)
                dsh_ref[...] = jnp.zeros_like(dsh_ref)

            dg_ref[0] += jnp.sum(dy * xn, axis=0, keepdims=True)
            dsc_ref[0] += jnp.sum(dh * xn * gv, axis=0, keepdims=True)
            dsh_ref[0] += jnp.sum(dh, axis=0, keepdims=True)

        vec = jax.ShapeDtypeStruct((B, 1, D), F32)
        return pl.pallas_call(
            body, name=name + "_bwd", grid=(B, S // ts),
            in_specs=[row, gspec, per_b, row], out_specs=[row, per_b, per_b, per_b],
            out_shape=[jax.ShapeDtypeStruct((B, S, D), F32), vec, vec, vec], compiler_params=_cp(),
        )(x, g, scale, dh)

    @jax.custom_vjp
    def f(x, g, scale, shift):
        return fwd_call(x, g, scale, shift)

    def fwd(x, g, scale, shift):
        return fwd_call(x, g, scale, shift), (x, g, scale)

    def bwd(res, dh):
        x, g, scale = res
        dx, dg, dsc, dsh = bwd_call(x, g, scale, dh)
        return dx, jnp.sum(dg, axis=0), dsc, dsh

    f.defvjp(fwd, bwd)
    return f(x, g, scale, shift)


def group_rmsnorm(x, g, *, gs, out_dtype, name):
    T, W = x.shape
    ng = W // gs
    tr = _tile(T, 512, 8)
    row = pl.BlockSpec((tr, W), lambda i: (i, 0))
    gspec = pl.BlockSpec((1, W), lambda i: (0, 0))

    def fwd_call(x, g):
        def body(x_ref, g_ref, o_ref):
            for i in range(ng):
                sl = slice(i * gs, (i + 1) * gs)
                xv = x_ref[:, sl]
                r = lax.rsqrt(jnp.mean(xv * xv, axis=-1, keepdims=True) + EPS)
                o_ref[:, sl] = (xv * r * g_ref[:, sl]).astype(o_ref.dtype)

        return pl.pallas_call(
            body, name=name + "_fwd", grid=(T // tr,), in_specs=[row, gspec], out_specs=row,
            out_shape=jax.ShapeDtypeStruct((T, W), out_dtype), compiler_params=_cp(),
        )(x, g)

    def bwd_call(x, g, dy):
        def body(x_ref, g_ref, dy_ref, dx_ref, dg_ref):
            @pl.when(pl.program_id(0) == 0)
            def _():
                dg_ref[...] = jnp.zeros_like(dg_ref)

            for i in range(ng):
                sl = slice(i * gs, (i + 1) * gs)
                xv = x_ref[:, sl]
                dyv = dy_ref[:, sl].astype(F32)
                r = lax.rsqrt(jnp.mean(xv * xv, axis=-1, keepdims=True) + EPS)
                xn = xv * r
                dxn = dyv * g_ref[:, sl]
                dx_ref[:, sl] = r * (dxn - xn * jnp.mean(dxn * xn, axis=-1, keepdims=True))
                dg_ref[:, sl] += jnp.sum(dyv * xn, axis=0, keepdims=True)

        return pl.pallas_call(
            body, name=name + "_bwd", grid=(T // tr,), in_specs=[row, gspec, row],
            out_specs=[row, gspec],
            out_shape=[jax.ShapeDtypeStruct((T, W), F32), jax.ShapeDtypeStruct((1, W), F32)],
            compiler_params=_cp(),
        )(x, g, dy)

    @jax.custom_vjp
    def f(x, g):
        return fwd_call(x, g)

    def fwd(x, g):
        return fwd_call(x, g), (x, g)

    def bwd(res, dy):
        return bwd_call(res[0], res[1], dy)

    f.defvjp(fwd, bwd)
    return f(x, g)


def rope_tables(seq_len, rot_dim, width):
    rows = seq_len // GRID_W
    row_idx = jnp.repeat(jnp.arange(rows), GRID_W).astype(F32)
    col_idx = jnp.tile(jnp.arange(GRID_W), rows).astype(F32)
    axis_dim = rot_dim // 2
    inv_freq = jnp.power(ROPE_THETA, -jnp.arange(0, axis_dim, 2, dtype=F32) / axis_dim)
    ang_r = row_idx[:, None] * inv_freq[None, :]
    ang_c = col_idx[:, None] * inv_freq[None, :]
    cos = jnp.concatenate([jnp.cos(ang_r), jnp.cos(ang_r), jnp.cos(ang_c), jnp.cos(ang_c)], axis=-1)
    sin = jnp.concatenate([-jnp.sin(ang_r), jnp.sin(ang_r), -jnp.sin(ang_c), jnp.sin(ang_c)], axis=-1)
    reps = width // rot_dim
    return jnp.tile(cos, (1, reps)), jnp.tile(sin, (1, reps))


def rope(x, cos, sin, *, d, name):
    B, S, W = x.shape
    ts = _tile(S, 512, 8)
    row = pl.BlockSpec((1, ts, W), lambda b, j: (b, j, 0))
    tab = pl.BlockSpec((ts, W), lambda b, j: (j, 0))

    def call(x, inverse, nm):
        def body(x_ref, c_ref, s_ref, o_ref):
            xv = x_ref[0]
            lane = lax.broadcasted_iota(jnp.int32, xv.shape, 1)
            first = (lane // d) % 2 == 0

            def swap(v):
                return jnp.where(first, pltpu.roll(v, W - d, 1), pltpu.roll(v, d, 1))

            if inverse:
                o_ref[0] = xv * c_ref[...] + swap(xv * s_ref[...])
            else:
                o_ref[0] = xv * c_ref[...] + swap(xv) * s_ref[...]

        return pl.pallas_call(
            body, name=nm, grid=(B, S // ts), in_specs=[row, tab, tab], out_specs=row,
            out_shape=jax.ShapeDtypeStruct((B, S, W), F32), compiler_params=_cp(),
        )(x, cos, sin)

    @jax.custom_vjp
    def f(x):
        return call(x, False, name + "_fwd")

    def fwd(x):
        return call(x, False, name + "_fwd"), None

    def bwd(_, g):
        return (call(g, True, name + "_bwd"),)

    f.defvjp(fwd, bwd)
    return f(x)


def attention(q, k, v, *, H, Hkv, dk, dv, scale, name, carry=()):
    B, S, _ = q.shape
    rep = H // Hkv
    tq = _tile(S, 256, 8)
    tkb = _tile(S, 256, 8)
    kinds = [(cols, s.shape, tag) for s, cols, tag in carry]

    def fwd_call(q, k, v, routines=()):
        def body(q_ref, k_ref, v_ref, o_ref, lse_ref):
            s = _dot(_bf(q_ref[0]), _bf(k_ref[0]), NT) * scale
            m = jnp.max(s, axis=-1, keepdims=True)
            p = jnp.exp(s - m)
            l = jnp.sum(p, axis=-1, keepdims=True)
            o_ref[0] = _dot(_bf(p), _bf(v_ref[0]), NN) / l
            lse_ref[0, 0] = m + jnp.log(l)

        return carry_call(
            body, name=name + "_fwd", grid=(B, H, S // tq),
            in_specs=[pl.BlockSpec((1, tq, dk), lambda b, h, i: (b, i, h)),
                      pl.BlockSpec((1, S, dk), lambda b, h, i: (b, 0, h // rep)),
                      pl.BlockSpec((1, S, dv), lambda b, h, i: (b, 0, h // rep))],
            out_specs=[pl.BlockSpec((1, tq, dv), lambda b, h, i: (b, i, h)),
                       pl.BlockSpec((1, 1, tq, 1), lambda b, h, i: (b, h, i, 0))],
            out_shape=[jax.ShapeDtypeStruct((B, S, H * dv), F32),
                       jax.ShapeDtypeStruct((B, H, S, 1), F32)],
            carry=routines,
        )(q, k, v)

    def dq_call(q, k, v, o, do, lse):
        def body(q_ref, k_ref, v_ref, o_ref, do_ref, lse_ref, dq_ref, delta_ref):
            kb = _bf(k_ref[0])
            s = _dot(_bf(q_ref[0]), kb, NT) * scale
            p = jnp.exp(s - lse_ref[0, 0])
            dov = do_ref[0]
            delta = jnp.sum(dov * o_ref[0], axis=-1, keepdims=True)
            dp = _dot(_bf(dov), _bf(v_ref[0]), NT)
            ds = p * (dp - delta)
            dq_ref[0] = _dot(_bf(ds), kb, NN) * scale
            delta_ref[0, 0] = delta

        qs = pl.BlockSpec((1, tq, dk), lambda b, h, i: (b, i, h))
        os_ = pl.BlockSpec((1, tq, dv), lambda b, h, i: (b, i, h))
        col = pl.BlockSpec((1, 1, tq, 1), lambda b, h, i: (b, h, i, 0))
        return pl.pallas_call(
            body, name=name + "_dq", grid=(B, H, S // tq),
            in_specs=[qs, pl.BlockSpec((1, S, dk), lambda b, h, i: (b, 0, h // rep)),
                      pl.BlockSpec((1, S, dv), lambda b, h, i: (b, 0, h // rep)), os_, os_, col],
            out_specs=[qs, col],
            out_shape=[jax.ShapeDtypeStruct(q.shape, F32), jax.ShapeDtypeStruct((B, H, S, 1), F32)],
            compiler_params=_cp(),
        )(q, k, v, o, do, lse)

    def dkv_call(q, k, v, do, lse, delta, routines=()):
        def body(q_ref, k_ref, v_ref, do_ref, lse_ref, delta_ref, dk_ref, dv_ref):
            @pl.when(pl.program_id(3) == 0)
            def _():
                dk_ref[...] = jnp.zeros_like(dk_ref)
                dv_ref[...] = jnp.zeros_like(dv_ref)

            qb = _bf(q_ref[0])
            dob = _bf(do_ref[0])
            s = _dot(qb, _bf(k_ref[0]), NT) * scale
            p = jnp.exp(s - lse_ref[0, 0])
            dv_ref[0] += _dot(_bf(p), dob, TN)
            dp = _dot(dob, _bf(v_ref[0]), NT)
            ds = p * (dp - delta_ref[0, 0])
            dk_ref[0] += _dot(_bf(ds), qb, TN) * scale

        hq = lambda b, g, j, r: (b, 0, g * rep + r)
        colq = pl.BlockSpec((1, 1, S, 1), lambda b, g, j, r: (b, g * rep + r, 0, 0))
        ks = pl.BlockSpec((1, tkb, dk), lambda b, g, j, r: (b, j, g))
        vs = pl.BlockSpec((1, tkb, dv), lambda b, g, j, r: (b, j, g))
        return carry_call(
            body, name=name + "_dkv", grid=(B, Hkv, S // tkb, rep),
            in_specs=[pl.BlockSpec((1, S, dk), hq), ks, vs, pl.BlockSpec((1, S, dv), hq), colq, colq],
            out_specs=[ks, vs],
            out_shape=[jax.ShapeDtypeStruct(k.shape, F32), jax.ShapeDtypeStruct(v.shape, F32)],
            carry=routines,
        )(q, k, v, do, lse, delta)

    def run(q, k, v, *shards):
        routines = [ag_routine(_shard_prep(s, cols), cols) for s, (cols, _, _) in zip(shards, kinds)]
        o, lse, *fulls = fwd_call(q, k, v, routines)
        return (o, *[_full_post(fu, cols) for fu, (cols, _, _) in zip(fulls, kinds)]), lse

    @jax.custom_vjp
    def f(q, k, v, *shards):
        return run(q, k, v, *shards)[0]

    def fwd(q, k, v, *shards):
        outs, lse = run(q, k, v, *shards)
        return outs, (q, k, v, outs[0], lse)

    def bwd(res, cts):
        q, k, v, o, lse = res
        hs = [_rs_begin(gf, cols, shp, name + "_" + tag) for gf, (cols, shp, tag) in zip(cts[1:], kinds)]
        dq, delta = dq_call(q, k, v, o, cts[0], lse)
        dk_, dv_, *gots = dkv_call(q, k, v, cts[0], lse, delta, [rs_chip_routine(h) for h in hs])
        dsh = [_rs_end(h, got, cols, shp, name + "_" + tag)
               for h, got, (cols, shp, tag) in zip(hs, gots, kinds)]
        return (dq, dk_, dv_, *dsh)

    f.defvjp(fwd, bwd)
    out = f(q, k, v, *[s for s, _, _ in carry])
    return (out[0], list(out[1:])) if carry else out[0]


def conv_silu(x, w, b, *, name):
    B, S, C = x.shape
    tc = _tile(C, 256)
    pad = SSD_K // 2
    xs = pl.BlockSpec((1, S, tc), lambda bi, j: (bi, 0, j))
    ws = pl.BlockSpec((SSD_K, tc), lambda bi, j: (0, j))
    bs = pl.BlockSpec((1, tc), lambda bi, j: (0, j))

    def shifted(v, off):
        if off == 0:
            return v
        t = lax.broadcasted_iota(jnp.int32, v.shape, 0)
        r = pltpu.roll(v, (-off) % S, 0)
        return jnp.where((t + off >= 0) & (t + off < S), r, 0.0)

    def pre_act(xv, wv, bv):
        u = jnp.zeros_like(xv) + bv
        for k in range(SSD_K):
            u = u + wv[k:k + 1, :] * shifted(xv, k - pad)
        return u

    def fwd_call(x, w, b):
        def body(x_ref, w_ref, b_ref, o_ref):
            u = pre_act(x_ref[0], w_ref[...], b_ref[...])
            o_ref[0] = u * _sigmoid(u)

        return pl.pallas_call(
            body, name=name + "_fwd", grid=(B, C // tc), in_specs=[xs, ws, bs], out_specs=xs,
            out_shape=jax.ShapeDtypeStruct((B, S, C), F32), compiler_params=_cp(),
        )(x, w, b)

    def bwd_call(x, w, b, dy):
        def body(x_ref, w_ref, b_ref, dy_ref, dx_ref, dw_ref):
            xv = x_ref[0]
            wv = w_ref[...]
            u = pre_act(xv, wv, b_ref[...])
            sg = _sigmoid(u)
            du = dy_ref[0] * (sg * (1.0 + u * (1.0 - sg)))
            dx = jnp.zeros_like(xv)
            for k in range(SSD_K):
                dx = dx + wv[k:k + 1, :] * shifted(du, pad - k)
                dw_ref[0, k:k + 1, :] = jnp.sum(du * shifted(xv, k - pad), axis=0, keepdims=True)
            dw_ref[0, SSD_K:SSD_K + 1, :] = jnp.sum(du, axis=0, keepdims=True)
            dw_ref[0, SSD_K + 1:8, :] = jnp.zeros((8 - SSD_K - 1, tc), F32)
            dx_ref[0] = dx

        return pl.pallas_call(
            body, name=name + "_bwd", grid=(B, C // tc), in_specs=[xs, ws, bs, xs],
            out_specs=[xs, pl.BlockSpec((1, 8, tc), lambda bi, j: (bi, 0, j))],
            out_shape=[jax.ShapeDtypeStruct((B, S, C), F32), jax.ShapeDtypeStruct((B, 8, C), F32)],
            compiler_params=_cp(),
        )(x, w, b, dy)

    @jax.custom_vjp
    def f(x, w, b):
        return fwd_call(x, w, b)

    def fwd(x, w, b):
        return fwd_call(x, w, b), (x, w, b)

    def bwd(res, dy):
        x, w, b = res
        dx, dwb = bwd_call(x, w, b, dy)
        dwb = jnp.sum(dwb, axis=0)
        return dx, dwb[:SSD_K], dwb[SSD_K:SSD_K + 1]

    f.defvjp(fwd, bwd)
    return f(x, w, b)


def _softplus(x):
    return jnp.maximum(x, 0.0) + jnp.log1p(jnp.exp(-jnp.abs(x)))


def _ssd_prep(raw, raw_t, brow, arow, bcol, acol, rev):
    li = lax.broadcasted_iota(jnp.int32, (CHUNK, CHUNK), 0)
    ki = lax.broadcasted_iota(jnp.int32, (CHUNK, CHUNK), 1)
    later = (li <= ki) if rev else (li >= ki)
    dt = _softplus(raw + brow)
    a = dt * arow
    cs = _dotf(later.astype(F32), a)
    tot = jnp.sum(a, axis=0, keepdims=True)
    a_t = _softplus(raw_t + bcol) * acol
    earlier = (li >= ki) if rev else (li <= ki)
    cs_t = _dotf(a_t, earlier.astype(F32))
    return dt, a, cs, tot, cs_t, later


def _lane_pick(mat, j):
    lane = lax.broadcasted_iota(jnp.int32, mat.shape, 1)
    return jnp.sum(jnp.where(lane == j, mat, 0.0), axis=1, keepdims=True)


def _head_sum(t, first):
    s0 = jnp.sum(jnp.where(first, t, 0.0), axis=1, keepdims=True)
    s1 = jnp.sum(jnp.where(first, 0.0, t), axis=1, keepdims=True)
    return s0, s1


def ssd_scan(xbc, raw, brow, arow, *, rev, name):
    B, S, _ = xbc.shape
    NC = S // CHUNK
    off = SSD_H if rev else 0
    n_dt = 2 * SSD_H

    def chunk_of(c):
        return (NC - 1 - c) if rev else c

    def specs(cmap):
        return dict(
            x=pl.BlockSpec((1, CHUNK, SSD_INNER), lambda b, c: (b, cmap(c), 0)),
            bm=pl.BlockSpec((1, CHUNK, 2 * SSD_N), lambda b, c: (b, cmap(c), SSD_INNER // (2 * SSD_N))),
            cm=pl.BlockSpec((1, CHUNK, 2 * SSD_N), lambda b, c: (b, cmap(c), SSD_INNER // (2 * SSD_N) + 1)),
            raw=pl.BlockSpec((1, CHUNK, LANES), lambda b, c: (b, cmap(c), 0)),
            raw_t=pl.BlockSpec((1, n_dt, CHUNK), lambda b, c: (b, 0, cmap(c))),
            row=pl.BlockSpec((1, LANES), lambda b, c: (0, 0)),
            colv=pl.BlockSpec((n_dt, 1), lambda b, c: (0, 0)),
            hs=pl.BlockSpec((1, 1, N_PAIR, SSD_N, LANES), lambda b, c: (b, cmap(c), 0, 0, 0)),
        )

    def head_terms(prep, j, first_dummy=None):
        dt, a, cs, tot, cs_t, later = prep
        cs_c = _lane_pick(cs, j)
        cs_r = cs_t[j:j + 1, :]
        dt_c = _lane_pick(dt, j)
        tot_j = _lane_pick(tot, j)
        L = jnp.exp(jnp.where(later, cs_c - cs_r, -1e30))
        return cs_c, cs_r, dt_c, tot_j, L

    def fwd_call(xbc, raw, raw_t, brow, arow, bcol, acol):
        def body(x_ref, bm_ref, cm_ref, raw_ref, rawt_ref, brow_ref, arow_ref, bcol_ref, acol_ref,
                 y_ref, hs_ref, st_ref):
            @pl.when(pl.program_id(1) == 0)
            def _():
                st_ref[...] = jnp.zeros_like(st_ref)

            prep = _ssd_prep(raw_ref[0], rawt_ref[0], brow_ref[...], arow_ref[...],
                             bcol_ref[...], acol_ref[...], rev)
            lane = lax.broadcasted_iota(jnp.int32, (CHUNK, LANES), 1)
            first = lane < SSD_P
            for g in range(SSD_G):
                Bg = _bf(bm_ref[0, :, g * SSD_N:(g + 1) * SSD_N])
                Cg = _bf(cm_ref[0, :, g * SSD_N:(g + 1) * SSD_N])
                G = _dot(Cg, Bg, NT)
                for pp in range(N_PAIR // SSD_G):
                    pi = g * (N_PAIR // SSD_G) + pp
                    c0, _, d0, t0, L0 = head_terms(prep, off + 2 * pi)
                    c1, _, d1, t1, L1 = head_terms(prep, off + 2 * pi + 1)
                    xd = x_ref[0, :, pi * LANES:(pi + 1) * LANES] * jnp.where(first, d0, d1)
                    xdb = _bf(xd)
                    y = jnp.where(first, _dot(_bf(G * L0), xdb, NN), _dot(_bf(G * L1), xdb, NN))
                    dec = jnp.where(first, jnp.exp(t0 - c0), jnp.exp(t1 - c1))
                    h_prev = st_ref[pi]
                    hs_ref[0, 0, pi] = h_prev
                    y = y + _dot(Cg, _bf(h_prev), NN) * jnp.where(first, jnp.exp(c0), jnp.exp(c1))
                    y_ref[0, :, pi * LANES:(pi + 1) * LANES] = y
                    etot = jnp.where(first[:1], jnp.exp(t0), jnp.exp(t1))
                    st_ref[pi] = h_prev * etot + _dot(Bg, _bf(xd * dec), TN)

        sp = specs(chunk_of)
        return pl.pallas_call(
            body, name=name + "_fwd", grid=(B, NC),
            in_specs=[sp["x"], sp["bm"], sp["cm"], sp["raw"], sp["raw_t"], sp["row"], sp["row"],
                      sp["colv"], sp["colv"]],
            out_specs=[sp["x"], sp["hs"]],
            out_shape=[jax.ShapeDtypeStruct((B, S, SSD_INNER), F32),
                       jax.ShapeDtypeStruct((B, NC, N_PAIR, SSD_N, LANES), F32)],
            scratch_shapes=[pltpu.VMEM((N_PAIR, SSD_N, LANES), F32)],
            compiler_params=_cp(),
        )(xbc, xbc, xbc, raw, raw_t, brow, arow, bcol, acol)

    def bwd_call(xbc, raw, raw_t, brow, arow, bcol, acol, hs, dy):
        def body(x_ref, bm_ref, cm_ref, raw_ref, rawt_ref, brow_ref, arow_ref, bcol_ref, acol_ref,
                 hs_ref, dy_ref, dxbc_ref, draw_ref, da_ref, dst_ref):
            @pl.when(pl.program_id(1) == 0)
            def _():
                dst_ref[...] = jnp.zeros_like(dst_ref)
                da_ref[...] = jnp.zeros_like(da_ref)

            raw_v = raw_ref[0]
            prep = _ssd_prep(raw_v, rawt_ref[0], brow_ref[...], arow_ref[...],
                             bcol_ref[...], acol_ref[...], rev)
            dt, a, cs, tot, cs_t, later = prep
            li = lax.broadcasted_iota(jnp.int32, (CHUNK, CHUNK), 0)
            ki = lax.broadcasted_iota(jnp.int32, (CHUNK, CHUNK), 1)
            later_t = (li >= ki) if rev else (li <= ki)
            lane = lax.broadcasted_iota(jnp.int32, (CHUNK, LANES), 1)
            first = lane < SSD_P
            dcs_all = jnp.zeros((CHUNK, LANES), F32)
            ddt_all = jnp.zeros((CHUNK, LANES), F32)
            dtot_all = jnp.zeros((1, LANES), F32)
            for g in range(SSD_G):
                Bg = _bf(bm_ref[0, :, g * SSD_N:(g + 1) * SSD_N])
                Cg = _bf(cm_ref[0, :, g * SSD_N:(g + 1) * SSD_N])
                G = _dot(Cg, Bg, NT)
                Gt = _dot(Bg, Cg, NT)
                dG = jnp.zeros((CHUNK, CHUNK), F32)
                dB = jnp.zeros((CHUNK, SSD_N), F32)
                dC = jnp.zeros((CHUNK, SSD_N), F32)
                for pp in range(N_PAIR // SSD_G):
                    pi = g * (N_PAIR // SSD_G) + pp
                    j0, j1 = off + 2 * pi, off + 2 * pi + 1
                    c0, r0, d0, t0, L0 = head_terms(prep, j0)
                    c1, r1, d1, t1, L1 = head_terms(prep, j1)
                    Lt0 = jnp.exp(jnp.where(later_t, r0 - c0, -1e30))
                    Lt1 = jnp.exp(jnp.where(later_t, r1 - c1, -1e30))
                    xv = x_ref[0, :, pi * LANES:(pi + 1) * LANES]
                    dtp = jnp.where(first, d0, d1)
                    xd = xv * dtp
                    xdb = _bf(xd)
                    dyv = dy_ref[0, :, pi * LANES:(pi + 1) * LANES]
                    dyb = _bf(dyv)
                    dec = jnp.where(first, jnp.exp(t0 - c0), jnp.exp(t1 - c1))
                    ecs = jnp.where(first, jnp.exp(c0), jnp.exp(c1))
                    et0, et1 = jnp.exp(t0), jnp.exp(t1)
                    etot = jnp.where(first[:1], et0, et1)
                    h_prev = hs_ref[0, 0, pi]
                    hpb = _bf(h_prev)
                    dhn = dst_ref[pi]
                    dhb = _bf(dhn)
                    W0, W1 = G * L0, G * L1
                    Wt0, Wt1 = Gt * Lt0, Gt * Lt1
                    bdh = _dot(Bg, dhb, NN)
                    dxd = jnp.where(first, _dot(_bf(Wt0), dyb, NN), _dot(_bf(Wt1), dyb, NN)) + bdh * dec
                    dy0 = _bf(jnp.where(first, dyv, 0.0))
                    dy1 = _bf(jnp.where(first, 0.0, dyv))
                    Q0, Q1 = _dot(dy0, xdb, NT), _dot(dy1, xdb, NT)
                    Qt0, Qt1 = _dot(xdb, dy0, NT), _dot(xdb, dy1, NT)
                    dG = dG + Q0 * L0 + Q1 * L1
                    dcs0 = (jnp.sum(Q0 * W0, axis=1, keepdims=True)
                            - jnp.sum(Qt0 * Wt0, axis=1, keepdims=True))
                    dcs1 = (jnp.sum(Q1 * W1, axis=1, keepdims=True)
                            - jnp.sum(Qt1 * Wt1, axis=1, keepdims=True))
                    dye = dyv * ecs
                    dyeb = _bf(dye)
                    s0, s1 = _head_sum(dye * _dot(Cg, hpb, NN), first)
                    dcs0, dcs1 = dcs0 + s0, dcs1 + s1
                    dC = dC + _dot(dyeb, hpb, NT)
                    dB = dB + _dot(_bf(xd * dec), dhb, NT)
                    u0, u1 = _head_sum(xd * bdh * dec, first)
                    dcs0, dcs1 = dcs0 - u0, dcs1 - u1
                    w = jnp.sum(dhn * h_prev, axis=0, keepdims=True)
                    w0, w1 = _head_sum(w, first[:1])
                    dt0 = jnp.sum(u0, axis=0, keepdims=True) + et0 * w0
                    dt1 = jnp.sum(u1, axis=0, keepdims=True) + et1 * w1
                    dst_ref[pi] = _dot(Cg, dyeb, TN) + dhn * etot
                    q0, q1 = _head_sum(dxd * xv, first)
                    dxbc_ref[0, :, pi * LANES:(pi + 1) * LANES] = dxd * dtp
                    dcs_all = dcs_all + jnp.where(lane == j0, dcs0, 0.0) + jnp.where(lane == j1, dcs1, 0.0)
                    ddt_all = ddt_all + jnp.where(lane == j0, q0, 0.0) + jnp.where(lane == j1, q1, 0.0)
                    dtot_all = (dtot_all + jnp.where(lane[:1] == j0, dt0, 0.0)
                                + jnp.where(lane[:1] == j1, dt1, 0.0))
                dGb = _bf(dG)
                dC = dC + _dot(dGb, Bg, NN)
                dB = dB + _dot(dGb, Cg, TN)
                dxbc_ref[0, :, SSD_INNER + g * SSD_N:SSD_INNER + (g + 1) * SSD_N] = dB
                dxbc_ref[0, :, SSD_INNER + (SSD_G + g) * SSD_N:SSD_INNER + (SSD_G + g + 1) * SSD_N] = dC
            da = _dotf(later_t.astype(F32), dcs_all) + dtot_all
            ddt = ddt_all + da * arow_ref[...]
            da_ref[0] += jnp.sum(da * dt, axis=0, keepdims=True)
            draw_ref[0] = ddt * _sigmoid(raw_v + brow_ref[...])

        def rchunk(c):
            return c if rev else (NC - 1 - c)

        sp = specs(rchunk)
        full = pl.BlockSpec((1, CHUNK, SSD_CONV_DIM), lambda b, c: (b, rchunk(c), 0))
        return pl.pallas_call(
            body, name=name + "_bwd", grid=(B, NC),
            in_specs=[sp["x"], sp["bm"], sp["cm"], sp["raw"], sp["raw_t"], sp["row"], sp["row"],
                      sp["colv"], sp["colv"], sp["hs"], sp["x"]],
            out_specs=[full, sp["raw"], pl.BlockSpec((1, 1, LANES), lambda b, c: (b, 0, 0))],
            out_shape=[jax.ShapeDtypeStruct((B, S, SSD_CONV_DIM), F32),
                       jax.ShapeDtypeStruct((B, S, LANES), F32),
                       jax.ShapeDtypeStruct((B, 1, LANES), F32)],
            scratch_shapes=[pltpu.VMEM((N_PAIR, SSD_N, LANES), F32)],
            compiler_params=_cp(),
        )(xbc, xbc, xbc, raw, raw_t, brow, arow, bcol, acol, hs, dy)

    def aux(raw, brow, arow):
        raw_t = jnp.swapaxes(raw[:, :, :n_dt], 1, 2)
        return raw_t, brow[0, :n_dt][:, None], arow[0, :n_dt][:, None]

    @jax.custom_vjp
    def f(xbc, raw, brow, arow):
        raw_t, bcol, acol = aux(raw, brow, arow)
        return fwd_call(xbc, raw, raw_t, brow, arow, bcol, acol)[0]

    def fwd(xbc, raw, brow, arow):
        raw_t, bcol, acol = aux(raw, brow, arow)
        y, hs = fwd_call(xbc, raw, raw_t, brow, arow, bcol, acol)
        return y, (xbc, raw, brow, arow, hs)

    def bwd(res, dy):
        xbc, raw, brow, arow, hs = res
        raw_t, bcol, acol = aux(raw, brow, arow)
        dxbc, draw, da = bwd_call(xbc, raw, raw_t, brow, arow, bcol, acol, hs, dy)
        dbrow = jnp.sum(draw, axis=(0, 1))[None, :]
        return dxbc, draw, dbrow, jnp.sum(da, axis=0)

    f.defvjp(fwd, bwd)
    return f(xbc, raw, brow, arow)


def ssd_out(yf, yb, xbc, z, dsk, g, *, name):
    T, W = yf.shape
    gs = W // SSD_G
    tr = _tile(T, 512, 8)
    row = pl.BlockSpec((tr, W), lambda i: (i, 0))
    vec = pl.BlockSpec((1, W), lambda i: (0, 0))

    def normed(yv, gv):
        outs, rs = [], []
        for i in range(SSD_G):
            sl = slice(i * gs, (i + 1) * gs)
            r = lax.rsqrt(jnp.mean(yv[:, sl] * yv[:, sl], axis=-1, keepdims=True) + EPS)
            rs.append(r)
            outs.append(yv[:, sl] * r)
        return outs, rs

    def fwd_call(yf, yb, xbc, z, dsk, g):
        def body(yf_ref, yb_ref, xs_ref, z_ref, dsk_ref, g_ref, o_ref):
            zv = z_ref[...]
            yv = (yf_ref[...] + yb_ref[...] + xs_ref[...] * dsk_ref[...]) * (zv * _sigmoid(zv))
            outs, _ = normed(yv, g_ref[...])
            for i in range(SSD_G):
                sl = slice(i * gs, (i + 1) * gs)
                o_ref[:, sl] = (outs[i] * g_ref[:, sl]).astype(o_ref.dtype)

        return pl.pallas_call(
            body, name=name + "_fwd", grid=(T // tr,), in_specs=[row, row, row, row, vec, vec],
            out_specs=row, out_shape=jax.ShapeDtypeStruct((T, W), BF16), compiler_params=_cp(),
        )(yf, yb, xbc, z, dsk, g)

    def bwd_call(yf, yb, xbc, z, dsk, g, do):
        def body(yf_ref, yb_ref, xs_ref, z_ref, dsk_ref, g_ref, do_ref, dy_ref, dxs_ref, dz_ref,
                 ddsk_ref, dg_ref):
            @pl.when(pl.program_id(0) == 0)
            def _():
                ddsk_ref[...] = jnp.zeros_like(ddsk_ref)
                dg_ref[...] = jnp.zeros_like(dg_ref)

            zv = z_ref[...]
            sg = _sigmoid(zv)
            sz = zv * sg
            xs = xs_ref[...]
            pre = yf_ref[...] + yb_ref[...] + xs * dsk_ref[...]
            yv = pre * sz
            outs, rs = normed(yv, g_ref[...])
            for i in range(SSD_G):
                sl = slice(i * gs, (i + 1) * gs)
                dov = do_ref[:, sl].astype(F32)
                xn = outs[i]
                dxn = dov * g_ref[:, sl]
                dyv = rs[i] * (dxn - xn * jnp.mean(dxn * xn, axis=-1, keepdims=True))
                dg_ref[:, sl] += jnp.sum(dov * xn, axis=0, keepdims=True)
                dpre = dyv * sz[:, sl]
                dy_ref[:, sl] = dpre
                dxs_ref[:, sl] = dpre * dsk_ref[:, sl]
                ddsk_ref[:, sl] += jnp.sum(dpre * xs[:, sl], axis=0, keepdims=True)
                dz_ref[:, sl] = dyv * pre[:, sl] * (sg[:, sl] * (1.0 + zv[:, sl] * (1.0 - sg[:, sl])))

        o = jax.ShapeDtypeStruct((T, W), F32)
        v = jax.ShapeDtypeStruct((1, W), F32)
        return pl.pallas_call(
            body, name=name + "_bwd", grid=(T // tr,), in_specs=[row, row, row, row, vec, vec, row],
            out_specs=[row, row, row, vec, vec], out_shape=[o, o, o, v, v], compiler_params=_cp(),
        )(yf, yb, xbc, z, dsk, g, do)

    @jax.custom_vjp
    def f(yf, yb, xbc, z, dsk, g):
        return fwd_call(yf, yb, xbc, z, dsk, g)

    def fwd(yf, yb, xbc, z, dsk, g):
        return fwd_call(yf, yb, xbc, z, dsk, g), (yf, yb, xbc, z, dsk, g)

    def bwd(res, do):
        dy, dxs, dz, ddsk, dg = bwd_call(*res, do)
        dxbc = jnp.pad(dxs, ((0, 0), (0, res[2].shape[1] - W)))
        return dy, dy, dxbc, dz, ddsk, dg

    f.defvjp(fwd, bwd)
    return f(yf, yb, xbc, z, dsk, g)


def swiglu(gu, *, name):
    T, F2 = gu.shape
    Fh = F2 // 2
    tr, tf = _tile(T, 512, 8), _tile(Fh, 512)
    nf = Fh // tf
    gs = pl.BlockSpec((tr, tf), lambda i, j: (i, j))
    us = pl.BlockSpec((tr, tf), lambda i, j: (i, j + nf))

    def fwd_call(gu):
        def body(g_ref, u_ref, o_ref):
            gv = g_ref[...].astype(F32)
            o_ref[...] = (gv * _sigmoid(gv) * u_ref[...].astype(F32)).astype(o_ref.dtype)

        return pl.pallas_call(
            body, name=name + "_fwd", grid=(T // tr, nf), in_specs=[gs, us], out_specs=gs,
            out_shape=jax.ShapeDtypeStruct((T, Fh), BF16), compiler_params=_cp(),
        )(gu, gu)

    def bwd_call(gu, da):
        def body(g_ref, u_ref, da_ref, dgu_ref):
            j = pl.program_id(1)
            gv = g_ref[...].astype(F32)
            uv = u_ref[...].astype(F32)
            dav = da_ref[...].astype(F32)
            sg = _sigmoid(gv)

            @pl.when(j < nf)
            def _():
                dgu_ref[...] = (dav * uv * (sg * (1.0 + gv * (1.0 - sg)))).astype(dgu_ref.dtype)

            @pl.when(j >= nf)
            def _():
                dgu_ref[...] = (dav * gv * sg).astype(dgu_ref.dtype)

        gsel = pl.BlockSpec((tr, tf), lambda i, j: (i, j % nf))
        usel = pl.BlockSpec((tr, tf), lambda i, j: (i, j % nf + nf))
        return pl.pallas_call(
            body, name=name + "_bwd", grid=(T // tr, 2 * nf), in_specs=[gsel, usel, gsel],
            out_specs=pl.BlockSpec((tr, tf), lambda i, j: (i, j)),
            out_shape=jax.ShapeDtypeStruct((T, F2), BF16), compiler_params=_cp(),
        )(gu, gu, da)

    @jax.custom_vjp
    def f(gu):
        return fwd_call(gu)

    def fwd(gu):
        return fwd_call(gu), gu

    def bwd(gu, da):
        return (bwd_call(gu, da),)

    f.defvjp(fwd, bwd)
    return f(gu)


def gated_residual(x, gate, y, *, name):
    B, S, D = x.shape
    ts = _tile(S, 256, 8)
    row = pl.BlockSpec((1, ts, D), lambda b, j: (b, j, 0))
    per_b = pl.BlockSpec((1, 1, D), lambda b, j: (b, 0, 0))

    def fwd_call(x, gate, y):
        def body(x_ref, gt_ref, y_ref, o_ref):
            o_ref[0] = x_ref[0] + gt_ref[0] * y_ref[0]

        return pl.pallas_call(
            body, name=name + "_fwd", grid=(B, S // ts), in_specs=[row, per_b, row], out_specs=row,
            out_shape=jax.ShapeDtypeStruct((B, S, D), F32), compiler_params=_cp(),
        )(x, gate, y)

    def bwd_call(gate, y, g):
        def body(gt_ref, y_ref, g_ref, dy_ref, dgt_ref):
            @pl.when(pl.program_id(1) == 0)
            def _():
                dgt_ref[...] = jnp.zeros_like(dgt_ref)

            gv = g_ref[0]
            dy_ref[0] = gt_ref[0] * gv
            dgt_ref[0] += jnp.sum(gv * y_ref[0], axis=0, keepdims=True)

        return pl.pallas_call(
            body, name=name + "_bwd", grid=(B, S // ts), in_specs=[per_b, row, row],
            out_specs=[row, per_b],
            out_shape=[jax.ShapeDtypeStruct((B, S, D), F32), jax.ShapeDtypeStruct((B, 1, D), F32)],
            compiler_params=_cp(),
        )(gate, y, g)

    @jax.custom_vjp
    def f(x, gate, y):
        return fwd_call(x, gate, y)

    def fwd(x, gate, y):
        return fwd_call(x, gate, y), (gate, y)

    def bwd(res, g):
        dy, dgate = bwd_call(res[0], res[1], g)
        return g, dgate, dy

    f.defvjp(fwd, bwd)
    return f(x, gate, y)


def final_loss(x, g, target, *, name):
    T, D = x.shape
    tr = _tile(T, 256, 8)
    row = pl.BlockSpec((tr, D), lambda i: (i, 0))
    vec = pl.BlockSpec((1, D), lambda i: (0, 0))

    def fwd_call(x, g, target):
        def body(x_ref, g_ref, t_ref, o_ref):
            @pl.when(pl.program_id(0) == 0)
            def _():
                o_ref[...] = jnp.zeros_like(o_ref)

            xv = x_ref[...]
            r = lax.rsqrt(jnp.mean(xv * xv, axis=-1, keepdims=True) + EPS)
            e = xv * r * g_ref[...] - t_ref[...]
            o_ref[...] += jnp.sum(e * e, axis=0, keepdims=True)

        part = pl.pallas_call(
            body, name=name + "_fwd", grid=(T // tr,), in_specs=[row, vec, row], out_specs=vec,
            out_shape=jax.ShapeDtypeStruct((1, D), F32), compiler_params=_cp(),
        )(x, g, target)
        return (0.5 / D) * jnp.sum(part)

    def bwd_call(x, g, target, ct):
        def body(x_ref, g_ref, t_ref, ct_ref, dx_ref, dg_ref):
            @pl.when(pl.program_id(0) == 0)
            def _():
                dg_ref[...] = jnp.zeros_like(dg_ref)

            xv = x_ref[...]
            gv = g_ref[...]
            r = lax.rsqrt(jnp.mean(xv * xv, axis=-1, keepdims=True) + EPS)
            xn = xv * r
            dy = (xn * gv - t_ref[...]) * (ct_ref[...] * (1.0 / D))
            dxn = dy * gv
            dx_ref[...] = r * (dxn - xn * jnp.mean(dxn * xn, axis=-1, keepdims=True))
            dg_ref[...] += jnp.sum(dy * xn, axis=0, keepdims=True)

        return pl.pallas_call(
            body, name=name + "_bwd", grid=(T // tr,),
            in_specs=[row, vec, row, pl.BlockSpec((1, 1), lambda i: (0, 0))], out_specs=[row, vec],
            out_shape=[jax.ShapeDtypeStruct((T, D), F32), jax.ShapeDtypeStruct((1, D), F32)],
            compiler_params=_cp(),
        )(x, g, target, ct)

    @jax.custom_vjp
    def f(x, g, target):
        return fwd_call(x, g, target)

    def fwd(x, g, target):
        return fwd_call(x, g, target), (x, g, target)

    def bwd(res, ct):
        x, g, target = res
        dx, dg = bwd_call(x, g, target, jnp.reshape(ct, (1, 1)).astype(F32))
        return dx, dg, jnp.zeros_like(target)

    f.defvjp(fwd, bwd)
    return f(x, g, target)


def adamw(w, g, m, v, *, name):
    R, C = w.shape
    tr = _tile(R, 512, 8)
    spec = pl.BlockSpec((tr, C), lambda i: (i, 0))
    c1 = 1.0 / (1.0 - ADAM_B1 ** ADAM_STEP)
    c2 = 1.0 / (1.0 - ADAM_B2 ** ADAM_STEP)

    def body(w_ref, g_ref, m_ref, v_ref, d_ref, nm_ref, nv_ref):
        gv = g_ref[...]
        nm = ADAM_B1 * m_ref[...] + (1.0 - ADAM_B1) * gv
        nv = ADAM_B2 * v_ref[...] + (1.0 - ADAM_B2) * (gv * gv)
        nm_ref[...] = nm
        nv_ref[...] = nv
        d_ref[...] = -ADAM_LR * ((nm * c1) / (jnp.sqrt(nv * c2) + ADAM_EPS) + ADAM_WD * w_ref[...])

    o = jax.ShapeDtypeStruct((R, C), F32)
    return pl.pallas_call(
        body, name=name, grid=(R // tr,), in_specs=[spec] * 4, out_specs=[spec] * 3,
        out_shape=[o, o, o], compiler_params=_cp(),
    )(w, g, m, v)


def _position():
    x, y, c = lax.axis_index("x"), lax.axis_index("y"), lax.axis_index("c")
    return x, y, c


def ag_routine(shard, cols=False):
    R, C = shard.shape
    assert not cols or C % LANES == 0

    def parts(ins, outs, send_sems, recv_sems, local_sems):
        (x_ref,), (out_ref,) = ins, outs
        x, y, c = _position()
        me, sibling = (x, y, c), (x, y, 1 - c)
        chips = [(1 - x, y), (x, 1 - y), (1 - x, 1 - y)]

        def block(px, py, pc):
            idx = 4 * px + 2 * py + pc
            if cols:
                return out_ref.at[:, pl.ds(pl.multiple_of(idx * C, LANES), C)]
            return out_ref.at[idx]

        def copy(k, blk, to, src=None):
            return pltpu.make_async_remote_copy(
                src_ref=block(*blk) if src is None else src, dst_ref=block(*blk),
                send_sem=send_sems.at[k], recv_sem=recv_sems.at[k],
                device_id=to, device_id_type=pl.DeviceIdType.MESH)

        mine = pltpu.make_async_copy(x_ref, block(*me), local_sems.at[0])
        first = [copy(0, me, sibling, src=x_ref)]
        first += [copy(1 + j, me, (*chip, c), src=x_ref) for j, chip in enumerate(chips)]
        passed = [copy(4 + j, (*chip, c), sibling) for j, chip in enumerate(chips)]
        return me, sibling, c, chips, copy, mine, first, passed

    def start(*refs):
        me, sibling, c, chips, copy, mine, first, passed = parts(*refs)
        mine.start()
        for cp in first:
            cp.start()

    def finish(*refs):
        me, sibling, c, chips, copy, mine, first, passed = parts(*refs)
        for j, chip in enumerate(chips):
            copy(1 + j, (*chip, c), me).wait_recv()
            passed[j].start()
        copy(0, sibling, me).wait_recv()
        for j, chip in enumerate(chips):
            copy(4 + j, (*chip, 1 - c), me).wait_recv()
        for cp in first + passed:
            cp.wait_send()
        mine.wait()

    out = jax.ShapeDtypeStruct((R, N_DEV * C) if cols else (N_DEV, R, C), shard.dtype)
    return dict(ins=[shard], outs=[out], n_sem=7, n_local=1, start=start, finish=finish)


def all_gather(shard, *, name, cols=False):
    return comm_call(ag_routine(shard, cols), name=name)[0]


def carry_call(body, *, name, grid, in_specs, out_specs, out_shape, scratch_shapes=(), carry=(),
               dims=None):
    in_specs, out_specs, out_shape = list(in_specs), list(out_specs), list(out_shape)
    scratch_shapes = list(scratch_shapes)
    if not carry:
        call = pl.pallas_call(body, name=name, grid=grid, in_specs=in_specs, out_specs=out_specs,
                              out_shape=out_shape, scratch_shapes=scratch_shapes,
                              compiler_params=_cp(dimension_semantics=dims) if dims else _cp())
        return lambda *args: list(call(*args))
    n_in, n_out, n_scr = len(in_specs), len(out_specs), len(scratch_shapes)
    c_ins = [a for r in carry for a in r["ins"]]
    c_outs = [o for r in carry for o in r["outs"]]
    sems = []
    for r in carry:
        sems += [pltpu.SemaphoreType.DMA((r["n_sem"],)), pltpu.SemaphoreType.DMA((r["n_sem"],)),
                 pltpu.SemaphoreType.DMA((r["n_local"],))]

    def wrapped(*refs):
        refs = list(refs)
        ins, refs = refs[:n_in], refs[n_in:]
        cin, refs = refs[:len(c_ins)], refs[len(c_ins):]
        outs, refs = refs[:n_out], refs[n_out:]
        cout, refs = refs[:len(c_outs)], refs[len(c_outs):]
        scr, csem = refs[:n_scr], refs[n_scr:]
        ids = [pl.program_id(i) for i in range(len(grid))]
        first = functools.reduce(jnp.logical_and, [i == 0 for i in ids])
        last = functools.reduce(jnp.logical_and, [i == g - 1 for i, g in zip(ids, grid)])

        def each(which):
            io = oo = 0
            for j, r in enumerate(carry):
                r[which](cin[io:io + len(r["ins"])], cout[oo:oo + len(r["outs"])], *csem[3 * j:3 * j + 3])
                io += len(r["ins"])
                oo += len(r["outs"])

        @pl.when(first)
        def _():
            each("start")

        body(*ins, *outs, *scr)

        @pl.when(last)
        def _():
            each("finish")

    any_spec = pl.BlockSpec(memory_space=pl.ANY)
    call = pl.pallas_call(
        wrapped, name=name, grid=grid, in_specs=in_specs + [any_spec] * len(c_ins),
        out_specs=out_specs + [any_spec] * len(c_outs), out_shape=out_shape + c_outs,
        scratch_shapes=scratch_shapes + sems, compiler_params=_cp())
    return lambda *args: list(call(*args, *c_ins))


N_CHIP = 4


def rs_pair_routine(g, cols):
    if cols:
        R, C = g.shape[0], g.shape[1] // N_DEV
        assert C % LANES == 0
    else:
        _, R, C = g.shape

    def blk(ref, idx):
        if cols:
            return ref.at[:, pl.ds(pl.multiple_of(idx * C, LANES), C)]
        return ref.at[idx]

    def copies(ins, outs, send_sems, recv_sems, local_sems):
        (g_ref,), (got_ref,) = ins, outs
        x, y, c = _position()
        local, remote = [], []
        for q in range(N_CHIP):
            remote.append(pltpu.make_async_remote_copy(
                src_ref=blk(g_ref, 2 * q + 1 - c), dst_ref=got_ref.at[q],
                send_sem=send_sems.at[q], recv_sem=recv_sems.at[q],
                device_id=(x, y, 1 - c), device_id_type=pl.DeviceIdType.MESH))
        return local, remote

    def start(*refs):
        local, remote = copies(*refs)
        for cp in remote + local:
            cp.start()

    def finish(*refs):
        local, remote = copies(*refs)
        for cp in remote:
            cp.wait_recv()
        for cp in remote:
            cp.wait_send()
        for cp in local:
            cp.wait()

    o = jax.ShapeDtypeStruct((N_CHIP, R, C), g.dtype)
    return dict(ins=[g], outs=[o], n_sem=N_CHIP, n_local=1, start=start, finish=finish)


def rs_chip_routine(h):
    _, R, C = h.shape
    RELATIONS = ((0, 1), (1, 0), (1, 1))

    def copies(ins, outs, send_sems, recv_sems, local_sems):
        (h_ref,), (out_ref,) = ins, outs
        x, y, c = _position()
        local, remote = [], []
        for k, (fx, fy) in enumerate(RELATIONS):
            px = (1 - x) if fx else x
            py = (1 - y) if fy else y
            remote.append(pltpu.make_async_remote_copy(
                src_ref=h_ref.at[2 * px + py], dst_ref=out_ref.at[k],
                send_sem=send_sems.at[k], recv_sem=recv_sems.at[k],
                device_id=(px, py, c), device_id_type=pl.DeviceIdType.MESH))
        return local, remote

    def start(*refs):
        local, remote = copies(*refs)
        for cp in remote + local:
            cp.start()

    def finish(*refs):
        local, remote = copies(*refs)
        for cp in remote:
            cp.wait_recv()
        for cp in remote:
            cp.wait_send()
        for cp in local:
            cp.wait()

    return dict(ins=[h], outs=[jax.ShapeDtypeStruct((N_CHIP - 1, R, C), h.dtype)], n_sem=3, n_local=1,
                start=start, finish=finish)


def comm_call(routine, *, name):
    n_in, n_out = len(routine["ins"]), len(routine["outs"])

    def body(*refs):
        ins, outs, sems = refs[:n_in], refs[n_in:n_in + n_out], refs[n_in + n_out:]
        routine["start"](ins, outs, *sems)
        routine["finish"](ins, outs, *sems)

    any_spec = pl.BlockSpec(memory_space=pl.ANY)
    return pl.pallas_call(
        body, name=name, out_shape=routine["outs"],
        in_specs=[any_spec] * n_in, out_specs=[any_spec] * n_out,
        scratch_shapes=[pltpu.SemaphoreType.DMA((routine["n_sem"],)), pltpu.SemaphoreType.DMA((routine["n_sem"],)),
                        pltpu.SemaphoreType.DMA((routine["n_local"],))],
    )(*routine["ins"])


def add_own(g, got, core, *, cols, name):
    n, R, C = got.shape
    tr = _tile(R, 256, 8)
    if cols:
        gspec = pl.BlockSpec((tr, C), lambda q, i, c_ref: (i, 2 * q + c_ref[0]))
    else:
        gspec = pl.BlockSpec((1, tr, C), lambda q, i, c_ref: (2 * q + c_ref[0], i, 0))
    spec = pl.BlockSpec((1, tr, C), lambda q, i, c_ref: (q, i, 0))

    def body(c_ref, g_ref, b_ref, o_ref):
        gv = g_ref[...] if cols else g_ref[0]
        o_ref[0] = (gv.astype(F32) + b_ref[0].astype(F32)).astype(o_ref.dtype)

    return pl.pallas_call(
        body, name=name,
        grid_spec=pltpu.PrefetchScalarGridSpec(num_scalar_prefetch=1, grid=(n, R // tr),
                                               in_specs=[gspec, spec], out_specs=spec),
        out_shape=jax.ShapeDtypeStruct((n, R, C), g.dtype), compiler_params=_cp(),
    )(core, g, got)


def sum_chips(h, got, chip, *, name):
    _, R, C = h.shape
    tr = _tile(R, 256, 8)

    def body(q_ref, h_ref, g_ref, o_ref):
        acc = h_ref[0].astype(F32)
        for k in range(N_CHIP - 1):
            acc = acc + g_ref[k].astype(F32)
        o_ref[...] = acc

    return pl.pallas_call(
        body, name=name,
        grid_spec=pltpu.PrefetchScalarGridSpec(
            num_scalar_prefetch=1, grid=(R // tr,),
            in_specs=[pl.BlockSpec((1, tr, C), lambda i, q_ref: (q_ref[0], i, 0)),
                      pl.BlockSpec((N_CHIP - 1, tr, C), lambda i, q_ref: (0, i, 0))],
            out_specs=pl.BlockSpec((tr, C), lambda i, q_ref: (i, 0))),
        out_shape=jax.ShapeDtypeStruct((R, C), F32), compiler_params=_cp(),
    )(chip, h, got)


def reduce_scatter(g, *, cols, name):
    x, y, c = _position()
    core = jnp.reshape(c, (1,)).astype(jnp.int32)
    chip = jnp.reshape(2 * x + y, (1,)).astype(jnp.int32)
    got = comm_call(rs_pair_routine(g, cols), name=name + "_pair")[0]
    h = add_own(g, got, core, cols=cols, name=name + "_add")
    return sum_chips(h, comm_call(rs_chip_routine(h), name=name + "_chip")[0], chip, name=name + "_sum")


def sum_blocks(stack, *, name):
    n, R, C = stack.shape
    tr = _tile(R, 256, 8)

    def body(x_ref, o_ref):
        acc = x_ref[0].astype(F32)
        for i in range(1, n):
            acc = acc + x_ref[i].astype(F32)
        o_ref[...] = acc

    return pl.pallas_call(
        body, name=name, grid=(R // tr,),
        in_specs=[pl.BlockSpec((n, tr, C), lambda i: (0, i, 0))],
        out_specs=pl.BlockSpec((tr, C), lambda i: (i, 0)),
        out_shape=jax.ShapeDtypeStruct((R, C), F32), compiler_params=_cp(),
    )(stack)


PACK_COLS = 1024
PACK_ROW_MULT = 8


def _pack(arrays, dtype):
    flat = jnp.concatenate([a.reshape(-1).astype(dtype) for a in arrays])
    n = flat.shape[0]
    unit = PACK_COLS * PACK_ROW_MULT
    padded = -(-n // unit) * unit
    return jnp.pad(flat, (0, padded - n)).reshape(padded // PACK_COLS, PACK_COLS)


def _unpack(packed, shapes):
    flat = packed.reshape(-1)
    out, o = [], 0
    for s in shapes:
        n = int(np.prod(s))
        out.append(flat[o:o + n].reshape(s))
        o += n
    return out


def fsdp_cols(shard, *, name):
    K, n = shard.shape
    npad = -(-n // LANES) * LANES

    @jax.custom_vjp
    def f(p):
        p = jnp.pad(p, ((0, 0), (0, npad - n))) if npad != n else p
        return all_gather(p.astype(BF16), cols=True, name=name + "_ag")

    def fwd(p):
        return f(p), None

    def bwd(_, g):
        d = reduce_scatter(g, cols=True, name=name + "_rs")
        return (d[:, :n] if npad != n else d,)

    f.defvjp(fwd, bwd)
    return f(shard)


def fsdp_rows(shard, *, name):
    k, N = shard.shape

    @jax.custom_vjp
    def f(p):
        return all_gather(p.astype(BF16), name=name + "_ag").reshape(N_DEV * k, N)

    def fwd(p):
        return f(p), None

    def bwd(_, g):
        return (reduce_scatter(g.reshape(N_DEV, k, N), cols=False, name=name + "_rs"),)

    f.defvjp(fwd, bwd)
    return f(shard)


def _unpad_cols(w, n):
    K = w.shape[0]
    npad = w.shape[1] // N_DEV
    if npad == n:
        return w
    return w.reshape(K, N_DEV, npad)[:, :, :n].reshape(K, N_DEV * n)


def gather_rows(part, me, *, name):
    rows, n = part.shape
    per = rows // N_DEV

    @jax.custom_vjp
    def f(part):
        full = all_gather(part, name=name + "_fwd")
        mine = lax.dynamic_slice_in_dim(full, me * per, per, axis=1)
        return jnp.swapaxes(mine, 0, 1).reshape(per, N_DEV * n)

    def fwd(part):
        return f(part), None

    def bwd(_, g):
        full = all_gather(g, name=name + "_bwd")
        mine = lax.dynamic_slice_in_dim(full, me * n, n, axis=2)
        return (mine.reshape(rows, n),)

    f.defvjp(fwd, bwd)
    return f(part)


def _seg_layout():
    offs = np.concatenate([[0], np.cumsum(IN_SPLITS)])
    cols, widths = [], []
    for s in SEG_ORDER:
        cols.append((int(offs[s]), int(offs[s + 1])))
        widths.append(SEG_PAD.get(s, IN_SPLITS[s]))
    return cols, widths


def _arrange_w_in(w):
    cols, widths = _seg_layout()
    parts = []
    for (a, b), wd in zip(cols, widths):
        seg = w[:, a:b]
        if wd != b - a:
            seg = jnp.pad(seg, ((0, 0), (0, wd - (b - a))))
        parts.append(seg)
    parts.append(jnp.zeros((w.shape[0], IN_WIDTH - sum(widths)), w.dtype))
    return jnp.concatenate(parts, axis=1)


def split_cols(proj, widths):
    @jax.custom_vjp
    def f(p):
        outs, o = [], 0
        for wd in widths:
            outs.append(p[:, o:o + wd])
            o += wd
        return tuple(outs)

    def fwd(p):
        return f(p), None

    def bwd(_, gs):
        rest = proj.shape[1] - sum(widths)
        tail = [jnp.zeros((proj.shape[0], rest), proj.dtype)] if rest else []
        return (jnp.concatenate(list(gs) + tail, axis=1),)

    f.defvjp(fwd, bwd)
    return f(proj)


BIG = ("w_in", "w_uq", "w_ukv", "conv_w", "w_out", "w_gate_up", "w_down")
SMALL = ("b_ada", "norm1_g", "norm2_g", "q_norm_g", "k_norm_g", "mla_q_norm_g", "mla_kv_norm_g",
         "conv_b", "dt_bias", "a_log", "d_skip", "ssd_norm_g", "final_norm_g")
WEIGHTS = ("w_ada", "b_ada", "norm1_g", "norm2_g", "w_in", "q_norm_g", "k_norm_g", "mla_q_norm_g",
           "w_uq", "mla_kv_norm_g", "w_ukv", "conv_w", "conv_b", "dt_bias", "a_log", "d_skip",
           "ssd_norm_g", "w_out", "w_gate_up", "w_down", "final_norm_g")


PRE = ("w_in", "w_uq", "w_ukv", "conv_w")


def _layer(l, x, mod, W, shards, nxt, P, tabs):
    B, S, D = x.shape
    T = B * S
    nm = f"l{l}_"
    shift1, scale1, gate1, shift2, scale2, gate2 = [m[:, None, :] for m in jnp.split(mod, 6, axis=-1)]
    cos_a, sin_a, cos_b, sin_b, cos_k, sin_k = tabs

    h = rmsmod(x, P["norm1_g"][l][None], scale1, shift1, name=nm + "norm1")
    w_in = _arrange_w_in(W["w_in"])
    proj, (w_out_full,) = linear(h.reshape(T, D), w_in, out_dtype=F32, name=nm + "in",
                                 carry=((shards["w_out"], False, "w_out"),))
    q_a, k_a, v_a, cq, ckv, z, xbc, kpe, dtr = split_cols(proj, _seg_layout()[1])

    qn = group_rmsnorm(q_a, jnp.tile(P["q_norm_g"][l], GQA_H)[None], gs=HEAD, out_dtype=F32, name=nm + "qnorm")
    kn = group_rmsnorm(k_a, jnp.tile(P["k_norm_g"][l], GQA_KV)[None], gs=HEAD, out_dtype=F32, name=nm + "knorm")
    qr = rope(qn.reshape(B, S, -1), cos_a[:, :GQA_H * HEAD], sin_a[:, :GQA_H * HEAD], d=HEAD // 4, name=nm + "qrope")
    kr = rope(kn.reshape(B, S, -1), cos_a[:, :GQA_KV * HEAD], sin_a[:, :GQA_KV * HEAD], d=HEAD // 4, name=nm + "krope")
    o_a, (w_gu_full,) = attention(qr, kr, v_a.reshape(B, S, -1), H=GQA_H, Hkv=GQA_KV, dk=HEAD, dv=HEAD,
                                  scale=HEAD ** -0.5, name=nm + "gqa",
                                  carry=((shards["w_gate_up"], True, "w_gate_up"),))

    w_uq = W["w_uq"].reshape(MLA_QL, MLA_H, MLA_NOPE + MLA_ROPE)
    w_uq = jnp.concatenate([w_uq[:, :, :MLA_NOPE].reshape(MLA_QL, -1), w_uq[:, :, MLA_NOPE:].reshape(MLA_QL, -1)], axis=1)
    w_ukv = W["w_ukv"].reshape(MLA_KVL, MLA_H, MLA_NOPE + MLA_V)
    w_ukv = jnp.concatenate([w_ukv[:, :, :MLA_NOPE].reshape(MLA_KVL, -1), w_ukv[:, :, MLA_NOPE:].reshape(MLA_KVL, -1)], axis=1)
    cqn = group_rmsnorm(cq, P["mla_q_norm_g"][l][None], gs=MLA_QL, out_dtype=BF16, name=nm + "cqnorm")
    ckvn = group_rmsnorm(ckv, P["mla_kv_norm_g"][l][None], gs=MLA_KVL, out_dtype=BF16, name=nm + "ckvnorm")
    qb = linear(cqn, w_uq, out_dtype=F32, name=nm + "uq")
    kvb = linear(ckvn, w_ukv, out_dtype=F32, name=nm + "ukv")
    q_nope, q_pe = split_cols(qb, (MLA_H * MLA_NOPE, MLA_H * MLA_ROPE))
    k_nope, v_b = split_cols(kvb, (MLA_H * MLA_NOPE, MLA_H * MLA_V))
    q_pe = rope(q_pe.reshape(B, S, -1), cos_b, sin_b, d=MLA_ROPE // 4, name=nm + "qpe_rope")
    k_pe = rope(kpe.reshape(B, S, -1), cos_k, sin_k, d=MLA_ROPE // 4, name=nm + "kpe_rope")[:, :, :MLA_ROPE]
    zpad = jnp.zeros((B, S, MLA_H, MLA_DK - MLA_NOPE - MLA_ROPE), F32)
    q_cat = jnp.concatenate([q_nope.reshape(B, S, MLA_H, MLA_NOPE), q_pe.reshape(B, S, MLA_H, MLA_ROPE), zpad],
                            axis=-1).reshape(B, S, MLA_H * MLA_DK)
    k_cat = jnp.concatenate([k_nope.reshape(B, S, MLA_H, MLA_NOPE),
                             jnp.broadcast_to(k_pe[:, :, None, :], (B, S, MLA_H, MLA_ROPE)), zpad],
                            axis=-1).reshape(B, S, MLA_H * MLA_DK)
    o_b = attention(q_cat, k_cat, v_b.reshape(B, S, -1), H=MLA_H, Hkv=MLA_H, dk=MLA_DK, dv=MLA_V,
                    scale=(MLA_NOPE + MLA_ROPE) ** -0.5, name=nm + "mla")

    xact = conv_silu(xbc.reshape(B, S, -1), W["conv_w"].astype(F32), P["conv_b"][l][None], name=nm + "conv")
    brow = jnp.pad(P["dt_bias"][l].reshape(1, -1), ((0, 0), (0, LANES - 2 * SSD_H)))
    arow = jnp.pad(-jnp.exp(P["a_log"][l].reshape(1, -1)), ((0, 0), (0, LANES - 2 * SSD_H)))
    raw = dtr.reshape(B, S, LANES)
    y_f = ssd_scan(xact, raw, brow, arow, rev=False, name=nm + "ssd_f")
    y_b = ssd_scan(xact, raw, brow, arow, rev=True, name=nm + "ssd_b")
    dsk = jnp.repeat(P["d_skip"][l], SSD_P)[None]
    o_c = ssd_out(y_f.reshape(T, -1), y_b.reshape(T, -1), xact.reshape(T, -1), z, dsk,
                  P["ssd_norm_g"][l][None], name=nm + "ssd_out")

    o = jnp.concatenate([o_a.reshape(T, -1).astype(BF16), o_b.reshape(T, -1).astype(BF16), o_c], axis=-1)
    mix = linear(o, w_out_full, out_dtype=F32, name=nm + "out")
    x = gated_residual(x, gate1, mix.reshape(B, S, D), name=nm + "res1")

    h = rmsmod(x, P["norm2_g"][l][None], scale2, shift2, name=nm + "norm2")
    gu, (w_down_full,) = linear(h.reshape(T, D), w_gu_full, out_dtype=BF16, name=nm + "gate_up",
                                carry=((shards["w_down"], False, "w_down"),))
    act = swiglu(gu, name=nm + "swiglu")
    if nxt is None:
        ffn, nxt_full = linear(act, w_down_full, out_dtype=F32, name=nm + "down"), None
    else:
        ffn, nxt_full = linear(act, w_down_full, out_dtype=F32, name=nm + "down",
                               carry=tuple((nxt[n], True, "next_" + n) for n in PRE))
    return gated_residual(x, gate2, ffn.reshape(B, S, D), name=nm + "res2"), nxt_full


def kernel(x, c, w_ada, b_ada, norm1_g, norm2_g, w_in, q_norm_g, k_norm_g, mla_q_norm_g, w_uq, mla_kv_norm_g, w_ukv, conv_w, conv_b, dt_bias, a_log, d_skip, ssd_norm_g, w_out, w_gate_up, w_down, final_norm_g, loss_target, m_w_ada, m_b_ada, m_norm1_g, m_norm2_g, m_w_in, m_q_norm_g, m_k_norm_g, m_mla_q_norm_g, m_w_uq, m_mla_kv_norm_g, m_w_ukv, m_conv_w, m_conv_b, m_dt_bias, m_a_log, m_d_skip, m_ssd_norm_g, m_w_out, m_w_gate_up, m_w_down, m_final_norm_g, v_w_ada, v_b_ada, v_norm1_g, v_norm2_g, v_w_in, v_q_norm_g, v_k_norm_g, v_mla_q_norm_g, v_w_uq, v_mla_kv_norm_g, v_w_ukv, v_conv_w, v_conv_b, v_dt_bias, v_a_log, v_d_skip, v_ssd_norm_g, v_w_out, v_w_gate_up, v_w_down, v_final_norm_g):
    args = dict(locals())
    weights = {n: args[n] for n in WEIGHTS}
    moments_m = {n: args["m_" + n] for n in WEIGHTS}
    moments_v = {n: args["v_" + n] for n in WEIGHTS}
    B, S, D = x.shape
    L = w_ada.shape[0]
    T = B * S
    px, py, pc = _position()
    me = 4 * px + 2 * py + pc
    small_shapes = [weights[n].shape for n in SMALL]

    tabs = (*rope_tables(S, HEAD, GQA_H * HEAD), *rope_tables(S, MLA_ROPE, MLA_H * MLA_ROPE),
            *rope_tables(S, MLA_ROPE, 2 * MLA_ROPE))
    c_all = all_gather(c, name="gather_c").reshape(N_DEV * B, D)

    def local_loss(big, w_ada_s, small, x):
        P = dict(zip(SMALL, small))
        pre = [fsdp_cols(big[n][0], name=f"l0_{n}") for n in PRE]
        for l in range(L):
            W = {n: _unpad_cols(full, big[n].shape[2]) for n, full in zip(PRE, pre)}
            shards = {n: big[n][l] for n in ("w_out", "w_gate_up", "w_down")}
            nxt = {n: big[n][l + 1] for n in PRE} if l + 1 < L else None
            part = linear(c_all, w_ada_s[l], out_dtype=F32, a_silu=True, name=f"l{l}_ada")
            mod = gather_rows(part, me, name=f"l{l}_mod") + P["b_ada"][l][None]
            x, pre = _layer(l, x, mod, W, shards, nxt, P, tabs)
        return final_loss(x.reshape(T, D), P["final_norm_g"][None], loss_target.reshape(T, D), name="loss")

    big = {n: weights[n] for n in BIG}
    small = tuple(weights[n] for n in SMALL)
    loss, (g_big, g_ada, g_small, grad_x) = jax.value_and_grad(local_loss, argnums=(0, 1, 2, 3))(
        big, w_ada, small, x)
    loss = lax.psum(loss, ("x", "y", "c"))

    grads = dict(g_big)
    grads["w_ada"] = g_ada
    g_small_sum = sum_blocks(all_gather(_pack(g_small, F32), name="small_grads_ag"), name="small_grads_sum")
    grads.update(zip(SMALL, _unpack(g_small_sum, small_shapes)))

    delta, new_m, new_v = {}, {}, {}
    for n in ("w_ada",) + BIG:
        shp = weights[n].shape
        two_d = (int(np.prod(shp[:-1])), shp[-1])
        d_, m_, v_ = adamw(weights[n].reshape(two_d), grads[n].reshape(two_d), moments_m[n].reshape(two_d),
                           moments_v[n].reshape(two_d), name="adamw_" + n)
        delta[n], new_m[n], new_v[n] = d_.reshape(shp), m_.reshape(shp), v_.reshape(shp)
    d_, m_, v_ = adamw(_pack([weights[n] for n in SMALL], F32), g_small_sum,
                       _pack([moments_m[n] for n in SMALL], F32), _pack([moments_v[n] for n in SMALL], F32),
                       name="adamw_small")
    for tgt, packed in ((delta, d_), (new_m, m_), (new_v, v_)):
        tgt.update(zip(SMALL, _unpack(packed, small_shapes)))

    return (loss, grad_x, *[grads[n] for n in WEIGHTS], *[delta[n] for n in WEIGHTS],
            *[new_m[n] for n in WEIGHTS], *[new_v[n] for n in WEIGHTS])
```

```python
import functools
import math

import jax
import jax.numpy as jnp
import numpy as np
from jax import lax
from jax.experimental import pallas as pl
from jax.experimental.pallas import tpu as pltpu

F32 = jnp.float32
BF16 = jnp.bfloat16
N_DEV = 8
EPS = 1e-6
ROPE_THETA = 10000.0
GRID_W = 64

GQA_H, GQA_KV, HEAD = 6, 2, 128
MLA_H, MLA_QL, MLA_KVL, MLA_NOPE, MLA_ROPE, MLA_V = 4, 512, 256, 128, 64, 128
MLA_DK = 256
SSD_H, SSD_P, SSD_G, SSD_N, SSD_K, CHUNK = 12, 64, 2, 128, 5, 128
SSD_INNER = SSD_H * SSD_P
SSD_CONV_DIM = SSD_INNER + 2 * SSD_G * SSD_N
N_PAIR = SSD_H // 2
LANES = 128
IN_SPLITS = (768, 256, 256, 512, 256, 64, 768, 1280, 24)
IN_COLS = sum(IN_SPLITS)
SEG_ORDER = (0, 1, 2, 3, 4, 6, 7, 5, 8)
SEG_PAD = {5: 256, 8: 128}
SEG_LEAD = {5: 128}
IN_WIDTH = 4608

ADAM_LR, ADAM_B1, ADAM_B2, ADAM_EPS, ADAM_WD, ADAM_STEP = 0.001, 0.9, 0.999, 1e-08, 0.01, 10
VMEM_LIMIT = 56 * 1024 * 1024
MM_TM, MM_TN, MM_TK = 1024, 1024, 2048


def _cp(**kw):
    return pltpu.CompilerParams(vmem_limit_bytes=VMEM_LIMIT, **kw)


def _tile(dim, cap, mult=128):
    if dim <= cap:
        return dim
    best = None
    t = mult
    while t <= cap:
        if dim % t == 0:
            best = t
        t += mult
    assert best is not None, (dim, cap)
    return best


def _sigmoid(x):
    return 1.0 / (1.0 + jnp.exp(-x))


def _dot(a, b, dims):
    return lax.dot_general(a, b, (dims, ((), ())), preferred_element_type=F32)


NN = ((1,), (0,))
NT = ((1,), (1,))
TN = ((0,), (0,))


def _dotf(a, b, dims=NN):
    return lax.dot_general(a, b, (dims, ((), ())), preferred_element_type=F32,
                           precision=lax.Precision.HIGHEST)


def _bf(x):
    return x.astype(BF16)


def mm(a, b, *, ta=False, tb=False, out_dtype=F32, a_silu=False, name, carry=()):
    if ta:
        K, M = a.shape
    else:
        M, K = a.shape
    if tb:
        N, K2 = b.shape
    else:
        K2, N = b.shape
    assert K == K2, (a.shape, b.shape, ta, tb)
    tm, tn, tk = _tile(M, MM_TM), _tile(N, MM_TN), _tile(K, MM_TK)
    nk = K // tk
    dims = ((0 if ta else 1,), (1 if tb else 0,))

    def partial_product(a_ref, b_ref):
        av = a_ref[...]
        if a_silu:
            av = av.astype(F32)
            av = av * _sigmoid(av)
        return _dot(_bf(av), _bf(b_ref[...]), dims)

    def body_single(a_ref, b_ref, o_ref):
        o_ref[...] = partial_product(a_ref, b_ref).astype(o_ref.dtype)

    def body_acc(a_ref, b_ref, o_ref, acc_ref):
        k = pl.program_id(2)

        @pl.when(k == 0)
        def _():
            acc_ref[...] = partial_product(a_ref, b_ref)

        @pl.when(k > 0)
        def _():
            acc_ref[...] += partial_product(a_ref, b_ref)

        @pl.when(k == nk - 1)
        def _():
            o_ref[...] = acc_ref[...].astype(o_ref.dtype)

    body = body_single if nk == 1 else body_acc

    a_spec = (pl.BlockSpec((tk, tm), lambda i, j, k: (k, i)) if ta
              else pl.BlockSpec((tm, tk), lambda i, j, k: (i, k)))
    b_spec = (pl.BlockSpec((tn, tk), lambda i, j, k: (j, k)) if tb
              else pl.BlockSpec((tk, tn), lambda i, j, k: (k, j)))
    res = carry_call(
        body, name=name, grid=(M // tm, N // tn, nk),
        in_specs=[a_spec, b_spec],
        out_specs=[pl.BlockSpec((tm, tn), lambda i, j, k: (i, j))],
        out_shape=[jax.ShapeDtypeStruct((M, N), out_dtype)],
        scratch_shapes=[] if nk == 1 else [pltpu.VMEM((tm, tn), F32)],
        carry=carry, dims=("parallel", "parallel", "arbitrary"),
    )(a, b)
    return res if carry else res[0]


def _shard_prep(p, cols):
    if cols and p.shape[1] % LANES:
        p = jnp.pad(p, ((0, 0), (0, -p.shape[1] % LANES)))
    return p.astype(BF16)


def _full_post(full, cols):
    return full if cols else full.reshape(full.shape[0] * full.shape[1], full.shape[2])


def _rs_begin(g_full, cols, shard_shape, name):
    gg = g_full if cols else g_full.reshape(N_DEV, shard_shape[0], shard_shape[1])
    x, y, c = _position()
    got = comm_call(rs_pair_routine(gg, cols), name=name + "_pair")[0]
    return add_own(gg, got, jnp.reshape(c, (1,)).astype(jnp.int32), cols=cols, name=name + "_add")


def _rs_end(h, got, cols, shard_shape, name):
    x, y, c = _position()
    d = sum_chips(h, got, jnp.reshape(2 * x + y, (1,)).astype(jnp.int32), name=name + "_sum")
    return d[:, :shard_shape[1]] if cols else d


def linear(a, w, *, out_dtype, name, a_silu=False, carry=()):
    kinds = [(cols, s.shape, tag) for s, cols, tag in carry]

    def run(a, w, *shards):
        routines = [ag_routine(_shard_prep(s, cols), cols) for s, (cols, _, _) in zip(shards, kinds)]
        res = mm(a, w, out_dtype=out_dtype, a_silu=a_silu, name=name + "_fwd", carry=routines)
        if not routines:
            return (res,)
        return (res[0], *[_full_post(fu, cols) for fu, (cols, _, _) in zip(res[1:], kinds)])

    @jax.custom_vjp
    def f(a, w, *shards):
        return run(a, w, *shards)

    def fwd(a, w, *shards):
        return run(a, w, *shards), (a, w)

    def bwd(res, cts):
        a, w = res
        g = cts[0]
        hs = [_rs_begin(gf, cols, shp, name + "_" + tag) for gf, (cols, shp, tag) in zip(cts[1:], kinds)]
        routines = [rs_chip_routine(h) for h in hs]
        if a_silu:
            assert not routines
            da = jnp.zeros_like(a)
            gots = []
        else:
            r = mm(g, w, tb=True, out_dtype=a.dtype, name=name + "_da", carry=routines)
            da, gots = (r[0], r[1:]) if routines else (r, [])
        dw = mm(a, g, ta=True, out_dtype=w.dtype, a_silu=a_silu, name=name + "_dw")
        dsh = [_rs_end(h, got, cols, shp, name + "_" + tag)
               for h, got, (cols, shp, tag) in zip(hs, gots, kinds)]
        return (da, dw, *dsh)

    f.defvjp(fwd, bwd)
    out = f(a, w, *[s for s, _, _ in carry])
    return (out[0], list(out[1:])) if carry else out[0]


def rmsmod(x, g, scale, shift, *, name):
    B, S, D = x.shape
    ts = _tile(S, 256, 8)
    row = pl.BlockSpec((1, ts, D), lambda b, j: (b, j, 0))
    per_b = pl.BlockSpec((1, 1, D), lambda b, j: (b, 0, 0))
    gspec = pl.BlockSpec((1, D), lambda b, j: (0, 0))

    def fwd_call(x, g, scale, shift):
        def body(x_ref, g_ref, sc_ref, sh_ref, o_ref):
            xv = x_ref[0]
            r = lax.rsqrt(jnp.mean(xv * xv, axis=-1, keepdims=True) + EPS)
            y = xv * r * g_ref[...]
            o_ref[0] = (y * (1.0 + sc_ref[0]) + sh_ref[0]).astype(o_ref.dtype)

        return pl.pallas_call(
            body, name=name + "_fwd", grid=(B, S // ts),
            in_specs=[row, gspec, per_b, per_b], out_specs=row,
            out_shape=jax.ShapeDtypeStruct((B, S, D), BF16), compiler_params=_cp(),
        )(x, g, scale, shift)

    def bwd_call(x, g, scale, dh):
        def body(x_ref, g_ref, sc_ref, dh_ref, dx_ref, dg_ref, dsc_ref, dsh_ref):
            j = pl.program_id(1)
            xv = x_ref[0]
            dh = dh_ref[0].astype(F32)
            r = lax.rsqrt(jnp.mean(xv * xv, axis=-1, keepdims=True) + EPS)
            xn = xv * r
            gv = g_ref[...]
            dy = dh * (1.0 + sc_ref[0])
            dxn = dy * gv
            dx_ref[0] = r * (dxn - xn * jnp.mean(dxn * xn, axis=-1, keepdims=True))

            @pl.when(j == 0)
            def _():
                dg_ref[...] = jnp.zeros_like(dg_ref)
                dsc_ref[...] = jnp.zeros_like(dsc_ref)
                dsh_ref[...] = jnp.zeros_like(dsh_ref)

            dg_ref[0] += jnp.sum(dy * xn, axis=0, keepdims=True)
            dsc_ref[0] += jnp.sum(dh * xn * gv, axis=0, keepdims=True)
            dsh_ref[0] += jnp.sum(dh, axis=0, keepdims=True)

        vec = jax.ShapeDtypeStruct((B, 1, D), F32)
        return pl.pallas_call(
            body, name=name + "_bwd", grid=(B, S // ts),
            in_specs=[row, gspec, per_b, row], out_specs=[row, per_b, per_b, per_b],
            out_shape=[jax.ShapeDtypeStruct((B, S, D), F32), vec, vec, vec], compiler_params=_cp(),
        )(x, g, scale, dh)

    @jax.custom_vjp
    def f(x, g, scale, shift):
        return fwd_call(x, g, scale, shift)

    def fwd(x, g, scale, shift):
        return fwd_call(x, g, scale, shift), (x, g, scale)

    def bwd(res, dh):
        x, g, scale = res
        dx, dg, dsc, dsh = bwd_call(x, g, scale, dh)
        return dx, jnp.sum(dg, axis=0), dsc, dsh

    f.defvjp(fwd, bwd)
    return f(x, g, scale, shift)


def group_rmsnorm(x, g, *, gs, out_dtype, name):
    T, W = x.shape
    ng = W // gs
    tr = _tile(T, 512, 8)
    row = pl.BlockSpec((tr, W), lambda i: (i, 0))
    gspec = pl.BlockSpec((1, W), lambda i: (0, 0))

    def fwd_call(x, g):
        def body(x_ref, g_ref, o_ref):
            for i in range(ng):
                sl = slice(i * gs, (i + 1) * gs)
                xv = x_ref[:, sl]
                r = lax.rsqrt(jnp.mean(xv * xv, axis=-1, keepdims=True) + EPS)
                o_ref[:, sl] = (xv * r * g_ref[:, sl]).astype(o_ref.dtype)

        return pl.pallas_call(
            body, name=name + "_fwd", grid=(T // tr,), in_specs=[row, gspec], out_specs=row,
            out_shape=jax.ShapeDtypeStruct((T, W), out_dtype), compiler_params=_cp(),
        )(x, g)

    def bwd_call(x, g, dy):
        def body(x_ref, g_ref, dy_ref, dx_ref, dg_ref):
            @pl.when(pl.program_id(0) == 0)
            def _():
                dg_ref[...] = jnp.zeros_like(dg_ref)

            for i in range(ng):
                sl = slice(i * gs, (i + 1) * gs)
                xv = x_ref[:, sl]
                dyv = dy_ref[:, sl].astype(F32)
                r = lax.rsqrt(jnp.mean(xv * xv, axis=-1, keepdims=True) + EPS)
                xn = xv * r
                dxn = dyv * g_ref[:, sl]
                dx_ref[:, sl] = r * (dxn - xn * jnp.mean(dxn * xn, axis=-1, keepdims=True))
                dg_ref[:, sl] += jnp.sum(dyv * xn, axis=0, keepdims=True)

        return pl.pallas_call(
            body, name=name + "_bwd", grid=(T // tr,), in_specs=[row, gspec, row],
            out_specs=[row, gspec],
            out_shape=[jax.ShapeDtypeStruct((T, W), F32), jax.ShapeDtypeStruct((1, W), F32)],
            compiler_params=_cp(),
        )(x, g, dy)

    @jax.custom_vjp
    def f(x, g):
        return fwd_call(x, g)

    def fwd(x, g):
        return fwd_call(x, g), (x, g)

    def bwd(res, dy):
        return bwd_call(res[0], res[1], dy)

    f.defvjp(fwd, bwd)
    return f(x, g)


def rope_tables(seq_len, rot_dim, width, slot=0, lead=0):
    rows = seq_len // GRID_W
    row_idx = jnp.repeat(jnp.arange(rows), GRID_W).astype(F32)
    col_idx = jnp.tile(jnp.arange(GRID_W), rows).astype(F32)
    axis_dim = rot_dim // 2
    inv_freq = jnp.power(ROPE_THETA, -jnp.arange(0, axis_dim, 2, dtype=F32) / axis_dim)
    ang_r = row_idx[:, None] * inv_freq[None, :]
    ang_c = col_idx[:, None] * inv_freq[None, :]
    cos = jnp.concatenate([jnp.cos(ang_r), jnp.cos(ang_r), jnp.cos(ang_c), jnp.cos(ang_c)], axis=-1)
    sin = jnp.concatenate([-jnp.sin(ang_r), jnp.sin(ang_r), -jnp.sin(ang_c), jnp.sin(ang_c)], axis=-1)
    if slot:
        ones = jnp.ones((seq_len, 1), F32)
        cos = jnp.concatenate([ones * jnp.ones((1, lead)), cos, ones * jnp.ones((1, slot - lead - rot_dim))], axis=-1)
        sin = jnp.concatenate([ones * jnp.zeros((1, lead)), sin, ones * jnp.zeros((1, slot - lead - rot_dim))], axis=-1)
        rot_dim = slot
    reps = width // rot_dim
    return jnp.tile(cos, (1, reps)), jnp.tile(sin, (1, reps))


def rope(x, cos, sin, *, d, name):
    B, S, W = x.shape
    ts = _tile(S, 512, 8)
    row = pl.BlockSpec((1, ts, W), lambda b, j: (b, j, 0))
    tab = pl.BlockSpec((ts, W), lambda b, j: (j, 0))

    def call(x, inverse, nm):
        def body(x_ref, c_ref, s_ref, o_ref):
            xv = x_ref[0]
            lane = lax.broadcasted_iota(jnp.int32, xv.shape, 1)
            first = (lane // d) % 2 == 0

            def swap(v):
                return jnp.where(first, pltpu.roll(v, W - d, 1), pltpu.roll(v, d, 1))

            if inverse:
                o_ref[0] = xv * c_ref[...] + swap(xv * s_ref[...])
            else:
                o_ref[0] = xv * c_ref[...] + swap(xv) * s_ref[...]

        return pl.pallas_call(
            body, name=nm, grid=(B, S // ts), in_specs=[row, tab, tab], out_specs=row,
            out_shape=jax.ShapeDtypeStruct((B, S, W), F32), compiler_params=_cp(),
        )(x, cos, sin)

    @jax.custom_vjp
    def f(x):
        return call(x, False, name + "_fwd")

    def fwd(x):
        return call(x, False, name + "_fwd"), None

    def bwd(_, g):
        return (call(g, True, name + "_bwd"),)

    f.defvjp(fwd, bwd)
    return f(x)


def attention(q, k, v, *, H, Hkv, dk, dv, scale, name, carry=()):
    B, S, _ = q.shape
    rep = H // Hkv
    tq = _tile(S, 256, 8)
    tkb = _tile(S, 256, 8)
    kinds = [(cols, s.shape, tag) for s, cols, tag in carry]

    def fwd_call(q, k, v, routines=()):
        def body(q_ref, k_ref, v_ref, o_ref, lse_ref):
            s = _dot(_bf(q_ref[0]), _bf(k_ref[0]), NT) * scale
            m = jnp.max(s, axis=-1, keepdims=True)
            p = jnp.exp(s - m)
            l = jnp.sum(p, axis=-1, keepdims=True)
            o_ref[0] = _dot(_bf(p), _bf(v_ref[0]), NN) / l
            lse_ref[0, 0] = m + jnp.log(l)

        return carry_call(
            body, name=name + "_fwd", grid=(B, H, S // tq),
            in_specs=[pl.BlockSpec((1, tq, dk), lambda b, h, i: (b, i, h)),
                      pl.BlockSpec((1, S, dk), lambda b, h, i: (b, 0, h // rep)),
                      pl.BlockSpec((1, S, dv), lambda b, h, i: (b, 0, h // rep))],
            out_specs=[pl.BlockSpec((1, tq, dv), lambda b, h, i: (b, i, h)),
                       pl.BlockSpec((1, 1, tq, 1), lambda b, h, i: (b, h, i, 0))],
            out_shape=[jax.ShapeDtypeStruct((B, S, H * dv), F32),
                       jax.ShapeDtypeStruct((B, H, S, 1), F32)],
            carry=routines,
        )(q, k, v)

    def dq_call(q, k, v, o, do, lse):
        def body(q_ref, k_ref, v_ref, o_ref, do_ref, lse_ref, dq_ref, delta_ref):
            kb = _bf(k_ref[0])
            s = _dot(_bf(q_ref[0]), kb, NT) * scale
            p = jnp.exp(s - lse_ref[0, 0])
            dov = do_ref[0]
            delta = jnp.sum(dov * o_ref[0], axis=-1, keepdims=True)
            dp = _dot(_bf(dov), _bf(v_ref[0]), NT)
            ds = p * (dp - delta)
            dq_ref[0] = _dot(_bf(ds), kb, NN) * scale
            delta_ref[0, 0] = delta

        qs = pl.BlockSpec((1, tq, dk), lambda b, h, i: (b, i, h))
        os_ = pl.BlockSpec((1, tq, dv), lambda b, h, i: (b, i, h))
        col = pl.BlockSpec((1, 1, tq, 1), lambda b, h, i: (b, h, i, 0))
        return pl.pallas_call(
            body, name=name + "_dq", grid=(B, H, S // tq),
            in_specs=[qs, pl.BlockSpec((1, S, dk), lambda b, h, i: (b, 0, h // rep)),
                      pl.BlockSpec((1, S, dv), lambda b, h, i: (b, 0, h // rep)), os_, os_, col],
            out_specs=[qs, col],
            out_shape=[jax.ShapeDtypeStruct(q.shape, F32), jax.ShapeDtypeStruct((B, H, S, 1), F32)],
            compiler_params=_cp(),
        )(q, k, v, o, do, lse)

    def dkv_call(q, k, v, do, lse, delta, routines=()):
        def body(q_ref, k_ref, v_ref, do_ref, lse_ref, delta_ref, dk_ref, dv_ref):
            @pl.when(pl.program_id(3) == 0)
            def _():
                dk_ref[...] = jnp.zeros_like(dk_ref)
                dv_ref[...] = jnp.zeros_like(dv_ref)

            qb = _bf(q_ref[0])
            dob = _bf(do_ref[0])
            s = _dot(qb, _bf(k_ref[0]), NT) * scale
            p = jnp.exp(s - lse_ref[0, 0])
            dv_ref[0] += _dot(_bf(p), dob, TN)
            dp = _dot(dob, _bf(v_ref[0]), NT)
            ds = p * (dp - delta_ref[0, 0])
            dk_ref[0] += _dot(_bf(ds), qb, TN) * scale

        hq = lambda b, g, j, r: (b, 0, g * rep + r)
        colq = pl.BlockSpec((1, 1, S, 1), lambda b, g, j, r: (b, g * rep + r, 0, 0))
        ks = pl.BlockSpec((1, tkb, dk), lambda b, g, j, r: (b, j, g))
        vs = pl.BlockSpec((1, tkb, dv), lambda b, g, j, r: (b, j, g))
        return carry_call(
            body, name=name + "_dkv", grid=(B, Hkv, S // tkb, rep),
            in_specs=[pl.BlockSpec((1, S, dk), hq), ks, vs, pl.BlockSpec((1, S, dv), hq), colq, colq],
            out_specs=[ks, vs],
            out_shape=[jax.ShapeDtypeStruct(k.shape, F32), jax.ShapeDtypeStruct(v.shape, F32)],
            carry=routines,
        )(q, k, v, do, lse, delta)

    def run(q, k, v, *shards):
        routines = [ag_routine(_shard_prep(s, cols), cols) for s, (cols, _, _) in zip(shards, kinds)]
        o, lse, *fulls = fwd_call(q, k, v, routines)
        return (o, *[_full_post(fu, cols) for fu, (cols, _, _) in zip(fulls, kinds)]), lse

    @jax.custom_vjp
    def f(q, k, v, *shards):
        return run(q, k, v, *shards)[0]

    def fwd(q, k, v, *shards):
        outs, lse = run(q, k, v, *shards)
        return outs, (q, k, v, outs[0], lse)

    def bwd(res, cts):
        q, k, v, o, lse = res
        hs = [_rs_begin(gf, cols, shp, name + "_" + tag) for gf, (cols, shp, tag) in zip(cts[1:], kinds)]
        dq, delta = dq_call(q, k, v, o, cts[0], lse)
        dk_, dv_, *gots = dkv_call(q, k, v, cts[0], lse, delta, [rs_chip_routine(h) for h in hs])
        dsh = [_rs_end(h, got, cols, shp, name + "_" + tag)
               for h, got, (cols, shp, tag) in zip(hs, gots, kinds)]
        return (dq, dk_, dv_, *dsh)

    f.defvjp(fwd, bwd)
    out = f(q, k, v, *[s for s, _, _ in carry])
    return (out[0], list(out[1:])) if carry else out[0]


def conv_silu(x, w, b, *, name):
    B, S, C = x.shape
    tc = _tile(C, 256)
    pad = SSD_K // 2
    xs = pl.BlockSpec((1, S, tc), lambda bi, j: (bi, 0, j))
    ws = pl.BlockSpec((SSD_K, tc), lambda bi, j: (0, j))
    bs = pl.BlockSpec((1, tc), lambda bi, j: (0, j))

    def shifted(v, off):
        if off == 0:
            return v
        t = lax.broadcasted_iota(jnp.int32, v.shape, 0)
        r = pltpu.roll(v, (-off) % S, 0)
        return jnp.where((t + off >= 0) & (t + off < S), r, 0.0)

    def pre_act(xv, wv, bv):
        u = jnp.zeros_like(xv) + bv
        for k in range(SSD_K):
            u = u + wv[k:k + 1, :] * shifted(xv, k - pad)
        return u

    def fwd_call(x, w, b):
        def body(x_ref, w_ref, b_ref, o_ref):
            u = pre_act(x_ref[0], w_ref[...], b_ref[...])
            o_ref[0] = u * _sigmoid(u)

        return pl.pallas_call(
            body, name=name + "_fwd", grid=(B, C // tc), in_specs=[xs, ws, bs], out_specs=xs,
            out_shape=jax.ShapeDtypeStruct((B, S, C), F32), compiler_params=_cp(),
        )(x, w, b)

    def bwd_call(x, w, b, dy):
        def body(x_ref, w_ref, b_ref, dy_ref, dx_ref, dw_ref):
            xv = x_ref[0]
            wv = w_ref[...]
            u = pre_act(xv, wv, b_ref[...])
            sg = _sigmoid(u)
            du = dy_ref[0] * (sg * (1.0 + u * (1.0 - sg)))
            dx = jnp.zeros_like(xv)
            for k in range(SSD_K):
                dx = dx + wv[k:k + 1, :] * shifted(du, pad - k)
                dw_ref[0, k:k + 1, :] = jnp.sum(du * shifted(xv, k - pad), axis=0, keepdims=True)
            dw_ref[0, SSD_K:SSD_K + 1, :] = jnp.sum(du, axis=0, keepdims=True)
            dw_ref[0, SSD_K + 1:8, :] = jnp.zeros((8 - SSD_K - 1, tc), F32)
            dx_ref[0] = dx

        return pl.pallas_call(
            body, name=name + "_bwd", grid=(B, C // tc), in_specs=[xs, ws, bs, xs],
            out_specs=[xs, pl.BlockSpec((1, 8, tc), lambda bi, j: (bi, 0, j))],
            out_shape=[jax.ShapeDtypeStruct((B, S, C), F32), jax.ShapeDtypeStruct((B, 8, C), F32)],
            compiler_params=_cp(),
        )(x, w, b, dy)

    @jax.custom_vjp
    def f(x, w, b):
        return fwd_call(x, w, b)

    def fwd(x, w, b):
        return fwd_call(x, w, b), (x, w, b)

    def bwd(res, dy):
        x, w, b = res
        dx, dwb = bwd_call(x, w, b, dy)
        dwb = jnp.sum(dwb, axis=0)
        return dx, dwb[:SSD_K], dwb[SSD_K:SSD_K + 1]

    f.defvjp(fwd, bwd)
    return f(x, w, b)


def _softplus(x):
    return jnp.maximum(x, 0.0) + jnp.log1p(jnp.exp(-jnp.abs(x)))


def _ssd_prep(raw, raw_t, brow, arow, bcol, acol, rev):
    li = lax.broadcasted_iota(jnp.int32, (CHUNK, CHUNK), 0)
    ki = lax.broadcasted_iota(jnp.int32, (CHUNK, CHUNK), 1)
    later = (li <= ki) if rev else (li >= ki)
    dt = _softplus(raw + brow)
    a = dt * arow
    cs = _dotf(later.astype(F32), a)
    tot = jnp.sum(a, axis=0, keepdims=True)
    a_t = _softplus(raw_t + bcol) * acol
    earlier = (li >= ki) if rev else (li <= ki)
    cs_t = _dotf(a_t, earlier.astype(F32))
    return dt, a, cs, tot, cs_t, later


def _lane_pick(mat, j):
    lane = lax.broadcasted_iota(jnp.int32, mat.shape, 1)
    return jnp.sum(jnp.where(lane == j, mat, 0.0), axis=1, keepdims=True)


def _head_sum(t, first):
    s0 = jnp.sum(jnp.where(first, t, 0.0), axis=1, keepdims=True)
    s1 = jnp.sum(jnp.where(first, 0.0, t), axis=1, keepdims=True)
    return s0, s1


def ssd_scan(xbc, raw, brow, arow, *, rev, name):
    B, S, _ = xbc.shape
    NC = S // CHUNK
    off = SSD_H if rev else 0
    n_dt = 2 * SSD_H

    def chunk_of(c):
        return (NC - 1 - c) if rev else c

    def specs(cmap):
        return dict(
            x=pl.BlockSpec((1, CHUNK, SSD_INNER), lambda b, c: (b, cmap(c), 0)),
            bm=pl.BlockSpec((1, CHUNK, 2 * SSD_N), lambda b, c: (b, cmap(c), SSD_INNER // (2 * SSD_N))),
            cm=pl.BlockSpec((1, CHUNK, 2 * SSD_N), lambda b, c: (b, cmap(c), SSD_INNER // (2 * SSD_N) + 1)),
            raw=pl.BlockSpec((1, CHUNK, LANES), lambda b, c: (b, cmap(c), 0)),
            raw_t=pl.BlockSpec((1, n_dt, CHUNK), lambda b, c: (b, 0, cmap(c))),
            row=pl.BlockSpec((1, LANES), lambda b, c: (0, 0)),
            colv=pl.BlockSpec((n_dt, 1), lambda b, c: (0, 0)),
            hs=pl.BlockSpec((1, 1, N_PAIR, SSD_N, LANES), lambda b, c: (b, cmap(c), 0, 0, 0)),
        )

    def head_terms(prep, j, first_dummy=None):
        dt, a, cs, tot, cs_t, later = prep
        cs_c = _lane_pick(cs, j)
        cs_r = cs_t[j:j + 1, :]
        dt_c = _lane_pick(dt, j)
        tot_j = _lane_pick(tot, j)
        L = jnp.exp(jnp.where(later, cs_c - cs_r, -1e30))
        return cs_c, cs_r, dt_c, tot_j, L

    def fwd_call(xbc, raw, raw_t, brow, arow, bcol, acol):
        def body(x_ref, bm_ref, cm_ref, raw_ref, rawt_ref, brow_ref, arow_ref, bcol_ref, acol_ref,
                 y_ref, hs_ref, st_ref):
            @pl.when(pl.program_id(1) == 0)
            def _():
                st_ref[...] = jnp.zeros_like(st_ref)

            prep = _ssd_prep(raw_ref[0], rawt_ref[0], brow_ref[...], arow_ref[...],
                             bcol_ref[...], acol_ref[...], rev)
            lane = lax.broadcasted_iota(jnp.int32, (CHUNK, LANES), 1)
            first = lane < SSD_P
            for g in range(SSD_G):
                Bg = _bf(bm_ref[0, :, g * SSD_N:(g + 1) * SSD_N])
                Cg = _bf(cm_ref[0, :, g * SSD_N:(g + 1) * SSD_N])
                G = _dot(Cg, Bg, NT)
                for pp in range(N_PAIR // SSD_G):
                    pi = g * (N_PAIR // SSD_G) + pp
                    c0, _, d0, t0, L0 = head_terms(prep, off + 2 * pi)
                    c1, _, d1, t1, L1 = head_terms(prep, off + 2 * pi + 1)
                    xd = x_ref[0, :, pi * LANES:(pi + 1) * LANES] * jnp.where(first, d0, d1)
                    xdb = _bf(xd)
                    y = jnp.where(first, _dot(_bf(G * L0), xdb, NN), _dot(_bf(G * L1), xdb, NN))
                    dec = jnp.where(first, jnp.exp(t0 - c0), jnp.exp(t1 - c1))
                    h_prev = st_ref[pi]
                    hs_ref[0, 0, pi] = h_prev
                    y = y + _dot(Cg, _bf(h_prev), NN) * jnp.where(first, jnp.exp(c0), jnp.exp(c1))
                    y_ref[0, :, pi * LANES:(pi + 1) * LANES] = y
                    etot = jnp.where(first[:1], jnp.exp(t0), jnp.exp(t1))
                    st_ref[pi] = h_prev * etot + _dot(Bg, _bf(xd * dec), TN)

        sp = specs(chunk_of)
        return pl.pallas_call(
            body, name=name + "_fwd", grid=(B, NC),
            in_specs=[sp["x"], sp["bm"], sp["cm"], sp["raw"], sp["raw_t"], sp["row"], sp["row"],
                      sp["colv"], sp["colv"]],
            out_specs=[sp["x"], sp["hs"]],
            out_shape=[jax.ShapeDtypeStruct((B, S, SSD_INNER), F32),
                       jax.ShapeDtypeStruct((B, NC, N_PAIR, SSD_N, LANES), F32)],
            scratch_shapes=[pltpu.VMEM((N_PAIR, SSD_N, LANES), F32)],
            compiler_params=_cp(),
        )(xbc, xbc, xbc, raw, raw_t, brow, arow, bcol, acol)

    def bwd_call(xbc, raw, raw_t, brow, arow, bcol, acol, hs, dy):
        def body(x_ref, bm_ref, cm_ref, raw_ref, rawt_ref, brow_ref, arow_ref, bcol_ref, acol_ref,
                 hs_ref, dy_ref, dxbc_ref, draw_ref, da_ref, dst_ref):
            @pl.when(pl.program_id(1) == 0)
            def _():
                dst_ref[...] = jnp.zeros_like(dst_ref)
                da_ref[...] = jnp.zeros_like(da_ref)

            raw_v = raw_ref[0]
            prep = _ssd_prep(raw_v, rawt_ref[0], brow_ref[...], arow_ref[...],
                             bcol_ref[...], acol_ref[...], rev)
            dt, a, cs, tot, cs_t, later = prep
            li = lax.broadcasted_iota(jnp.int32, (CHUNK, CHUNK), 0)
            ki = lax.broadcasted_iota(jnp.int32, (CHUNK, CHUNK), 1)
            later_t = (li >= ki) if rev else (li <= ki)
            lane = lax.broadcasted_iota(jnp.int32, (CHUNK, LANES), 1)
            first = lane < SSD_P
            dcs_all = jnp.zeros((CHUNK, LANES), F32)
            ddt_all = jnp.zeros((CHUNK, LANES), F32)
            dtot_all = jnp.zeros((1, LANES), F32)
            for g in range(SSD_G):
                Bg = _bf(bm_ref[0, :, g * SSD_N:(g + 1) * SSD_N])
                Cg = _bf(cm_ref[0, :, g * SSD_N:(g + 1) * SSD_N])
                G = _dot(Cg, Bg, NT)
                Gt = _dot(Bg, Cg, NT)
                dG = jnp.zeros((CHUNK, CHUNK), F32)
                dB = jnp.zeros((CHUNK, SSD_N), F32)
                dC = jnp.zeros((CHUNK, SSD_N), F32)
                for pp in range(N_PAIR // SSD_G):
                    pi = g * (N_PAIR // SSD_G) + pp
                    j0, j1 = off + 2 * pi, off + 2 * pi + 1
                    c0, r0, d0, t0, L0 = head_terms(prep, j0)
                    c1, r1, d1, t1, L1 = head_terms(prep, j1)
                    Lt0 = jnp.exp(jnp.where(later_t, r0 - c0, -1e30))
                    Lt1 = jnp.exp(jnp.where(later_t, r1 - c1, -1e30))
                    xv = x_ref[0, :, pi * LANES:(pi + 1) * LANES]
                    dtp = jnp.where(first, d0, d1)
                    xd = xv * dtp
                    xdb = _bf(xd)
                    dyv = dy_ref[0, :, pi * LANES:(pi + 1) * LANES]
                    dyb = _bf(dyv)
                    dec = jnp.where(first, jnp.exp(t0 - c0), jnp.exp(t1 - c1))
                    ecs = jnp.where(first, jnp.exp(c0), jnp.exp(c1))
                    et0, et1 = jnp.exp(t0), jnp.exp(t1)
                    etot = jnp.where(first[:1], et0, et1)
                    h_prev = hs_ref[0, 0, pi]
                    hpb = _bf(h_prev)
                    dhn = dst_ref[pi]
                    dhb = _bf(dhn)
                    W0, W1 = G * L0, G * L1
                    Wt0, Wt1 = Gt * Lt0, Gt * Lt1
                    bdh = _dot(Bg, dhb, NN)
                    dxd = jnp.where(first, _dot(_bf(Wt0), dyb, NN), _dot(_bf(Wt1), dyb, NN)) + bdh * dec
                    dy0 = _bf(jnp.where(first, dyv, 0.0))
                    dy1 = _bf(jnp.where(first, 0.0, dyv))
                    Q0, Q1 = _dot(dy0, xdb, NT), _dot(dy1, xdb, NT)
                    Qt0, Qt1 = _dot(xdb, dy0, NT), _dot(xdb, dy1, NT)
                    dG = dG + Q0 * L0 + Q1 * L1
                    dcs0 = (jnp.sum(Q0 * W0, axis=1, keepdims=True)
                            - jnp.sum(Qt0 * Wt0, axis=1, keepdims=True))
                    dcs1 = (jnp.sum(Q1 * W1, axis=1, keepdims=True)
                            - jnp.sum(Qt1 * Wt1, axis=1, keepdims=True))
                    dye = dyv * ecs
                    dyeb = _bf(dye)
                    s0, s1 = _head_sum(dye * _dot(Cg, hpb, NN), first)
                    dcs0, dcs1 = dcs0 + s0, dcs1 + s1
                    dC = dC + _dot(dyeb, hpb, NT)
                    dB = dB + _dot(_bf(xd * dec), dhb, NT)
                    u0, u1 = _head_sum(xd * bdh * dec, first)
                    dcs0, dcs1 = dcs0 - u0, dcs1 - u1
                    w = jnp.sum(dhn * h_prev, axis=0, keepdims=True)
                    w0, w1 = _head_sum(w, first[:1])
                    dt0 = jnp.sum(u0, axis=0, keepdims=True) + et0 * w0
                    dt1 = jnp.sum(u1, axis=0, keepdims=True) + et1 * w1
                    dst_ref[pi] = _dot(Cg, dyeb, TN) + dhn * etot
                    q0, q1 = _head_sum(dxd * xv, first)
                    dxbc_ref[0, :, pi * LANES:(pi + 1) * LANES] = dxd * dtp
                    dcs_all = dcs_all + jnp.where(lane == j0, dcs0, 0.0) + jnp.where(lane == j1, dcs1, 0.0)
                    ddt_all = ddt_all + jnp.where(lane == j0, q0, 0.0) + jnp.where(lane == j1, q1, 0.0)
                    dtot_all = (dtot_all + jnp.where(lane[:1] == j0, dt0, 0.0)
                                + jnp.where(lane[:1] == j1, dt1, 0.0))
                dGb = _bf(dG)
                dC = dC + _dot(dGb, Bg, NN)
                dB = dB + _dot(dGb, Cg, TN)
                dxbc_ref[0, :, SSD_INNER + g * SSD_N:SSD_INNER + (g + 1) * SSD_N] = dB
                dxbc_ref[0, :, SSD_INNER + (SSD_G + g) * SSD_N:SSD_INNER + (SSD_G + g + 1) * SSD_N] = dC
            da = _dotf(later_t.astype(F32), dcs_all) + dtot_all
            ddt = ddt_all + da * arow_ref[...]
            da_ref[0] += jnp.sum(da * dt, axis=0, keepdims=True)
            draw_ref[0] = ddt * _sigmoid(raw_v + brow_ref[...])

        def rchunk(c):
            return c if rev else (NC - 1 - c)

        sp = specs(rchunk)
        full = pl.BlockSpec((1, CHUNK, SSD_CONV_DIM), lambda b, c: (b, rchunk(c), 0))
        return pl.pallas_call(
            body, name=name + "_bwd", grid=(B, NC),
            in_specs=[sp["x"], sp["bm"], sp["cm"], sp["raw"], sp["raw_t"], sp["row"], sp["row"],
                      sp["colv"], sp["colv"], sp["hs"], sp["x"]],
            out_specs=[full, sp["raw"], pl.BlockSpec((1, 1, LANES), lambda b, c: (b, 0, 0))],
            out_shape=[jax.ShapeDtypeStruct((B, S, SSD_CONV_DIM), F32),
                       jax.ShapeDtypeStruct((B, S, LANES), F32),
                       jax.ShapeDtypeStruct((B, 1, LANES), F32)],
            scratch_shapes=[pltpu.VMEM((N_PAIR, SSD_N, LANES), F32)],
            compiler_params=_cp(),
        )(xbc, xbc, xbc, raw, raw_t, brow, arow, bcol, acol, hs, dy)

    def aux(raw, brow, arow):
        raw_t = jnp.swapaxes(raw[:, :, :n_dt], 1, 2)
        return raw_t, brow[0, :n_dt][:, None], arow[0, :n_dt][:, None]

    @jax.custom_vjp
    def f(xbc, raw, brow, arow):
        raw_t, bcol, acol = aux(raw, brow, arow)
        return fwd_call(xbc, raw, raw_t, brow, arow, bcol, acol)[0]

    def fwd(xbc, raw, brow, arow):
        raw_t, bcol, acol = aux(raw, brow, arow)
        y, hs = fwd_call(xbc, raw, raw_t, brow, arow, bcol, acol)
        return y, (xbc, raw, brow, arow, hs)

    def bwd(res, dy):
        xbc, raw, brow, arow, hs = res
        raw_t, bcol, acol = aux(raw, brow, arow)
        dxbc, draw, da = bwd_call(xbc, raw, raw_t, brow, arow, bcol, acol, hs, dy)
        dbrow = jnp.sum(draw, axis=(0, 1))[None, :]
        return dxbc, draw, dbrow, jnp.sum(da, axis=0)

    f.defvjp(fwd, bwd)
    return f(xbc, raw, brow, arow)


def ssd_out(yf, yb, xbc, z, dsk, g, *, name):
    T, W = yf.shape
    gs = W // SSD_G
    tr = _tile(T, 512, 8)
    row = pl.BlockSpec((tr, W), lambda i: (i, 0))
    vec = pl.BlockSpec((1, W), lambda i: (0, 0))

    def normed(yv, gv):
        outs, rs = [], []
        for i in range(SSD_G):
            sl = slice(i * gs, (i + 1) * gs)
            r = lax.rsqrt(jnp.mean(yv[:, sl] * yv[:, sl], axis=-1, keepdims=True) + EPS)
            rs.append(r)
            outs.append(yv[:, sl] * r)
        return outs, rs

    def fwd_call(yf, yb, xbc, z, dsk, g):
        def body(yf_ref, yb_ref, xs_ref, z_ref, dsk_ref, g_ref, o_ref):
            zv = z_ref[...]
            yv = (yf_ref[...] + yb_ref[...] + xs_ref[...] * dsk_ref[...]) * (zv * _sigmoid(zv))
            outs, _ = normed(yv, g_ref[...])
            for i in range(SSD_G):
                sl = slice(i * gs, (i + 1) * gs)
                o_ref[:, sl] = (outs[i] * g_ref[:, sl]).astype(o_ref.dtype)

        return pl.pallas_call(
            body, name=name + "_fwd", grid=(T // tr,), in_specs=[row, row, row, row, vec, vec],
            out_specs=row, out_shape=jax.ShapeDtypeStruct((T, W), BF16), compiler_params=_cp(),
        )(yf, yb, xbc, z, dsk, g)

    def bwd_call(yf, yb, xbc, z, dsk, g, do):
        def body(yf_ref, yb_ref, xs_ref, z_ref, dsk_ref, g_ref, do_ref, dy_ref, dxs_ref, dz_ref,
                 ddsk_ref, dg_ref):
            @pl.when(pl.program_id(0) == 0)
            def _():
                ddsk_ref[...] = jnp.zeros_like(ddsk_ref)
                dg_ref[...] = jnp.zeros_like(dg_ref)

            zv = z_ref[...]
            sg = _sigmoid(zv)
            sz = zv * sg
            xs = xs_ref[...]
            pre = yf_ref[...] + yb_ref[...] + xs * dsk_ref[...]
            yv = pre * sz
            outs, rs = normed(yv, g_ref[...])
            for i in range(SSD_G):
                sl = slice(i * gs, (i + 1) * gs)
                dov = do_ref[:, sl].astype(F32)
                xn = outs[i]
                dxn = dov * g_ref[:, sl]
                dyv = rs[i] * (dxn - xn * jnp.mean(dxn * xn, axis=-1, keepdims=True))
                dg_ref[:, sl] += jnp.sum(dov * xn, axis=0, keepdims=True)
                dpre = dyv * sz[:, sl]
                dy_ref[:, sl] = dpre
                dxs_ref[:, sl] = dpre * dsk_ref[:, sl]
                ddsk_ref[:, sl] += jnp.sum(dpre * xs[:, sl], axis=0, keepdims=True)
                dz_ref[:, sl] = dyv * pre[:, sl] * (sg[:, sl] * (1.0 + zv[:, sl] * (1.0 - sg[:, sl])))

        o = jax.ShapeDtypeStruct((T, W), F32)
        v = jax.ShapeDtypeStruct((1, W), F32)
        return pl.pallas_call(
            body, name=name + "_bwd", grid=(T // tr,), in_specs=[row, row, row, row, vec, vec, row],
            out_specs=[row, row, row, vec, vec], out_shape=[o, o, o, v, v], compiler_params=_cp(),
        )(yf, yb, xbc, z, dsk, g, do)

    @jax.custom_vjp
    def f(yf, yb, xbc, z, dsk, g):
        return fwd_call(yf, yb, xbc, z, dsk, g)

    def fwd(yf, yb, xbc, z, dsk, g):
        return fwd_call(yf, yb, xbc, z, dsk, g), (yf, yb, xbc, z, dsk, g)

    def bwd(res, do):
        dy, dxs, dz, ddsk, dg = bwd_call(*res, do)
        dxbc = jnp.pad(dxs, ((0, 0), (0, res[2].shape[1] - W)))
        return dy, dy, dxbc, dz, ddsk, dg

    f.defvjp(fwd, bwd)
    return f(yf, yb, xbc, z, dsk, g)


def swiglu(gu, *, name):
    T, F2 = gu.shape
    Fh = F2 // 2
    tr, tf = _tile(T, 512, 8), _tile(Fh, 512)
    nf = Fh // tf
    gs = pl.BlockSpec((tr, tf), lambda i, j: (i, j))
    us = pl.BlockSpec((tr, tf), lambda i, j: (i, j + nf))

    def fwd_call(gu):
        def body(g_ref, u_ref, o_ref):
            gv = g_ref[...].astype(F32)
            o_ref[...] = (gv * _sigmoid(gv) * u_ref[...].astype(F32)).astype(o_ref.dtype)

        return pl.pallas_call(
            body, name=name + "_fwd", grid=(T // tr, nf), in_specs=[gs, us], out_specs=gs,
            out_shape=jax.ShapeDtypeStruct((T, Fh), BF16), compiler_params=_cp(),
        )(gu, gu)

    def bwd_call(gu, da):
        def body(g_ref, u_ref, da_ref, dgu_ref):
            j = pl.program_id(1)
            gv = g_ref[...].astype(F32)
            uv = u_ref[...].astype(F32)
            dav = da_ref[...].astype(F32)
            sg = _sigmoid(gv)

            @pl.when(j < nf)
            def _():
                dgu_ref[...] = (dav * uv * (sg * (1.0 + gv * (1.0 - sg)))).astype(dgu_ref.dtype)

            @pl.when(j >= nf)
            def _():
                dgu_ref[...] = (dav * gv * sg).astype(dgu_ref.dtype)

        gsel = pl.BlockSpec((tr, tf), lambda i, j: (i, j % nf))
        usel = pl.BlockSpec((tr, tf), lambda i, j: (i, j % nf + nf))
        return pl.pallas_call(
            body, name=name + "_bwd", grid=(T // tr, 2 * nf), in_specs=[gsel, usel, gsel],
            out_specs=pl.BlockSpec((tr, tf), lambda i, j: (i, j)),
            out_shape=jax.ShapeDtypeStruct((T, F2), BF16), compiler_params=_cp(),
        )(gu, gu, da)

    @jax.custom_vjp
    def f(gu):
        return fwd_call(gu)

    def fwd(gu):
        return fwd_call(gu), gu

    def bwd(gu, da):
        return (bwd_call(gu, da),)

    f.defvjp(fwd, bwd)
    return f(gu)


def gated_residual(x, gate, y, *, name):
    B, S, D = x.shape
    ts = _tile(S, 256, 8)
    row = pl.BlockSpec((1, ts, D), lambda b, j: (b, j, 0))
    per_b = pl.BlockSpec((1, 1, D), lambda b, j: (b, 0, 0))

    def fwd_call(x, gate, y):
        def body(x_ref, gt_ref, y_ref, o_ref):
            o_ref[0] = x_ref[0] + gt_ref[0] * y_ref[0]

        return pl.pallas_call(
            body, name=name + "_fwd", grid=(B, S // ts), in_specs=[row, per_b, row], out_specs=row,
            out_shape=jax.ShapeDtypeStruct((B, S, D), F32), compiler_params=_cp(),
        )(x, gate, y)

    def bwd_call(gate, y, g):
        def body(gt_ref, y_ref, g_ref, dy_ref, dgt_ref):
            @pl.when(pl.program_id(1) == 0)
            def _():
                dgt_ref[...] = jnp.zeros_like(dgt_ref)

            gv = g_ref[0]
            dy_ref[0] = gt_ref[0] * gv
            dgt_ref[0] += jnp.sum(gv * y_ref[0], axis=0, keepdims=True)

        return pl.pallas_call(
            body, name=name + "_bwd", grid=(B, S // ts), in_specs=[per_b, row, row],
            out_specs=[row, per_b],
            out_shape=[jax.ShapeDtypeStruct((B, S, D), F32), jax.ShapeDtypeStruct((B, 1, D), F32)],
            compiler_params=_cp(),
        )(gate, y, g)

    @jax.custom_vjp
    def f(x, gate, y):
        return fwd_call(x, gate, y)

    def fwd(x, gate, y):
        return fwd_call(x, gate, y), (gate, y)

    def bwd(res, g):
        dy, dgate = bwd_call(res[0], res[1], g)
        return g, dgate, dy

    f.defvjp(fwd, bwd)
    return f(x, gate, y)


def final_loss(x, g, target, *, name):
    T, D = x.shape
    tr = _tile(T, 256, 8)
    row = pl.BlockSpec((tr, D), lambda i: (i, 0))
    vec = pl.BlockSpec((1, D), lambda i: (0, 0))

    def fwd_call(x, g, target):
        def body(x_ref, g_ref, t_ref, o_ref):
            @pl.when(pl.program_id(0) == 0)
            def _():
                o_ref[...] = jnp.zeros_like(o_ref)

            xv = x_ref[...]
            r = lax.rsqrt(jnp.mean(xv * xv, axis=-1, keepdims=True) + EPS)
            e = xv * r * g_ref[...] - t_ref[...]
            o_ref[...] += jnp.sum(e * e, axis=0, keepdims=True)

        part = pl.pallas_call(
            body, name=name + "_fwd", grid=(T // tr,), in_specs=[row, vec, row], out_specs=vec,
            out_shape=jax.ShapeDtypeStruct((1, D), F32), compiler_params=_cp(),
        )(x, g, target)
        return (0.5 / D) * jnp.sum(part)

    def bwd_call(x, g, target, ct):
        def body(x_ref, g_ref, t_ref, ct_ref, dx_ref, dg_ref):
            @pl.when(pl.program_id(0) == 0)
            def _():
                dg_ref[...] = jnp.zeros_like(dg_ref)

            xv = x_ref[...]
            gv = g_ref[...]
            r = lax.rsqrt(jnp.mean(xv * xv, axis=-1, keepdims=True) + EPS)
            xn = xv * r
            dy = (xn * gv - t_ref[...]) * (ct_ref[...] * (1.0 / D))
            dxn = dy * gv
            dx_ref[...] = r * (dxn - xn * jnp.mean(dxn * xn, axis=-1, keepdims=True))
            dg_ref[...] += jnp.sum(dy * xn, axis=0, keepdims=True)

        return pl.pallas_call(
            body, name=name + "_bwd", grid=(T // tr,),
            in_specs=[row, vec, row, pl.BlockSpec((1, 1), lambda i: (0, 0))], out_specs=[row, vec],
            out_shape=[jax.ShapeDtypeStruct((T, D), F32), jax.ShapeDtypeStruct((1, D), F32)],
            compiler_params=_cp(),
        )(x, g, target, ct)

    @jax.custom_vjp
    def f(x, g, target):
        return fwd_call(x, g, target)

    def fwd(x, g, target):
        return fwd_call(x, g, target), (x, g, target)

    def bwd(res, ct):
        x, g, target = res
        dx, dg = bwd_call(x, g, target, jnp.reshape(ct, (1, 1)).astype(F32))
        return dx, dg, jnp.zeros_like(target)

    f.defvjp(fwd, bwd)
    return f(x, g, target)


def adamw(w, g, m, v, *, name):
    L, R, C = w.shape
    tr = _tile(R, 512, 8)
    spec = pl.BlockSpec((1, tr, C), lambda l, i: (l, i, 0))
    c1 = 1.0 / (1.0 - ADAM_B1 ** ADAM_STEP)
    c2 = 1.0 / (1.0 - ADAM_B2 ** ADAM_STEP)

    def body(w_ref, g_ref, m_ref, v_ref, d_ref, nm_ref, nv_ref):
        gv = g_ref[...]
        nm = ADAM_B1 * m_ref[...] + (1.0 - ADAM_B1) * gv
        nv = ADAM_B2 * v_ref[...] + (1.0 - ADAM_B2) * (gv * gv)
        nm_ref[...] = nm
        nv_ref[...] = nv
        d_ref[...] = -ADAM_LR * ((nm * c1) / (jnp.sqrt(nv * c2) + ADAM_EPS) + ADAM_WD * w_ref[...])

    o = jax.ShapeDtypeStruct((L, R, C), F32)
    return pl.pallas_call(
        body, name=name, grid=(L, R // tr), in_specs=[spec] * 4, out_specs=[spec] * 3,
        out_shape=[o, o, o], compiler_params=_cp(),
    )(w, g, m, v)


def _position():
    x, y, c = lax.axis_index("x"), lax.axis_index("y"), lax.axis_index("c")
    return x, y, c


def ag_routine(shard, cols=False):
    R, C = shard.shape
    assert not cols or C % LANES == 0

    def parts(ins, outs, send_sems, recv_sems, local_sems):
        (x_ref,), (out_ref,) = ins, outs
        x, y, c = _position()
        me, sibling = (x, y, c), (x, y, 1 - c)
        chips = [(1 - x, y), (x, 1 - y), (1 - x, 1 - y)]

        def block(px, py, pc):
            idx = 4 * px + 2 * py + pc
            if cols:
                return out_ref.at[:, pl.ds(pl.multiple_of(idx * C, LANES), C)]
            return out_ref.at[idx]

        def copy(k, blk, to, src=None):
            return pltpu.make_async_remote_copy(
                src_ref=block(*blk) if src is None else src, dst_ref=block(*blk),
                send_sem=send_sems.at[k], recv_sem=recv_sems.at[k],
                device_id=to, device_id_type=pl.DeviceIdType.MESH)

        mine = pltpu.make_async_copy(x_ref, block(*me), local_sems.at[0])
        first = [copy(0, me, sibling, src=x_ref)]
        first += [copy(1 + j, me, (*chip, c), src=x_ref) for j, chip in enumerate(chips)]
        passed = [copy(4 + j, (*chip, c), sibling) for j, chip in enumerate(chips)]
        return me, sibling, c, chips, copy, mine, first, passed

    def start(*refs):
        me, sibling, c, chips, copy, mine, first, passed = parts(*refs)
        mine.start()
        for cp in first:
            cp.start()

    def finish(*refs):
        me, sibling, c, chips, copy, mine, first, passed = parts(*refs)
        for j, chip in enumerate(chips):
            copy(1 + j, (*chip, c), me).wait_recv()
            passed[j].start()
        copy(0, sibling, me).wait_recv()
        for j, chip in enumerate(chips):
            copy(4 + j, (*chip, 1 - c), me).wait_recv()
        for cp in first + passed:
            cp.wait_send()
        mine.wait()

    out = jax.ShapeDtypeStruct((R, N_DEV * C) if cols else (N_DEV, R, C), shard.dtype)
    return dict(ins=[shard], outs=[out], n_sem=7, n_local=1, start=start, finish=finish)


def all_gather(shard, *, name, cols=False):
    return comm_call(ag_routine(shard, cols), name=name)[0]


def carry_call(body, *, name, grid, in_specs, out_specs, out_shape, scratch_shapes=(), carry=(),
               dims=None):
    in_specs, out_specs, out_shape = list(in_specs), list(out_specs), list(out_shape)
    scratch_shapes = list(scratch_shapes)
    if not carry:
        call = pl.pallas_call(body, name=name, grid=grid, in_specs=in_specs, out_specs=out_specs,
                              out_shape=out_shape, scratch_shapes=scratch_shapes,
                              compiler_params=_cp(dimension_semantics=dims) if dims else _cp())
        return lambda *args: list(call(*args))
    n_in, n_out, n_scr = len(in_specs), len(out_specs), len(scratch_shapes)
    c_ins = [a for r in carry for a in r["ins"]]
    c_outs = [o for r in carry for o in r["outs"]]
    sems = []
    for r in carry:
        sems += [pltpu.SemaphoreType.DMA((r["n_sem"],)), pltpu.SemaphoreType.DMA((r["n_sem"],)),
                 pltpu.SemaphoreType.DMA((r["n_local"],))]

    def wrapped(*refs):
        refs = list(refs)
        ins, refs = refs[:n_in], refs[n_in:]
        cin, refs = refs[:len(c_ins)], refs[len(c_ins):]
        outs, refs = refs[:n_out], refs[n_out:]
        cout, refs = refs[:len(c_outs)], refs[len(c_outs):]
        scr, csem = refs[:n_scr], refs[n_scr:]
        ids = [pl.program_id(i) for i in range(len(grid))]
        first = functools.reduce(jnp.logical_and, [i == 0 for i in ids])
        last = functools.reduce(jnp.logical_and, [i == g - 1 for i, g in zip(ids, grid)])

        def each(which):
            io = oo = 0
            for j, r in enumerate(carry):
                r[which](cin[io:io + len(r["ins"])], cout[oo:oo + len(r["outs"])], *csem[3 * j:3 * j + 3])
                io += len(r["ins"])
                oo += len(r["outs"])

        @pl.when(first)
        def _():
            each("start")

        body(*ins, *outs, *scr)

        @pl.when(last)
        def _():
            each("finish")

    any_spec = pl.BlockSpec(memory_space=pl.ANY)
    call = pl.pallas_call(
        wrapped, name=name, grid=grid, in_specs=in_specs + [any_spec] * len(c_ins),
        out_specs=out_specs + [any_spec] * len(c_outs), out_shape=out_shape + c_outs,
        scratch_shapes=scratch_shapes + sems, compiler_params=_cp())
    return lambda *args: list(call(*args, *c_ins))


N_CHIP = 4


def rs_pair_routine(g, cols):
    if cols:
        R, C = g.shape[0], g.shape[1] // N_DEV
        assert C % LANES == 0
    else:
        _, R, C = g.shape

    def blk(ref, idx):
        if cols:
            return ref.at[:, pl.ds(pl.multiple_of(idx * C, LANES), C)]
        return ref.at[idx]

    def copies(ins, outs, send_sems, recv_sems, local_sems):
        (g_ref,), (got_ref,) = ins, outs
        x, y, c = _position()
        local, remote = [], []
        for q in range(N_CHIP):
            remote.append(pltpu.make_async_remote_copy(
                src_ref=blk(g_ref, 2 * q + 1 - c), dst_ref=got_ref.at[q],
                send_sem=send_sems.at[q], recv_sem=recv_sems.at[q],
                device_id=(x, y, 1 - c), device_id_type=pl.DeviceIdType.MESH))
        return local, remote

    def start(*refs):
        local, remote = copies(*refs)
        for cp in remote + local:
            cp.start()

    def finish(*refs):
        local, remote = copies(*refs)
        for cp in remote:
            cp.wait_recv()
        for cp in remote:
            cp.wait_send()
        for cp in local:
            cp.wait()

    o = jax.ShapeDtypeStruct((N_CHIP, R, C), g.dtype)
    return dict(ins=[g], outs=[o], n_sem=N_CHIP, n_local=1, start=start, finish=finish)


def rs_chip_routine(h):
    _, R, C = h.shape
    RELATIONS = ((0, 1), (1, 0), (1, 1))

    def copies(ins, outs, send_sems, recv_sems, local_sems):
        (h_ref,), (out_ref,) = ins, outs
        x, y, c = _position()
        local, remote = [], []
        for k, (fx, fy) in enumerate(RELATIONS):
            px = (1 - x) if fx else x
            py = (1 - y) if fy else y
            remote.append(pltpu.make_async_remote_copy(
                src_ref=h_ref.at[2 * px + py], dst_ref=out_ref.at[k],
                send_sem=send_sems.at[k], recv_sem=recv_sems.at[k],
                device_id=(px, py, c), device_id_type=pl.DeviceIdType.MESH))
        return local, remote

    def start(*refs):
        local, remote = copies(*refs)
        for cp in remote + local:
            cp.start()

    def finish(*refs):
        local, remote = copies(*refs)
        for cp in remote:
            cp.wait_recv()
        for cp in remote:
            cp.wait_send()
        for cp in local:
            cp.wait()

    return dict(ins=[h], outs=[jax.ShapeDtypeStruct((N_CHIP - 1, R, C), h.dtype)], n_sem=3, n_local=1,
                start=start, finish=finish)


def comm_call(routine, *, name):
    n_in, n_out = len(routine["ins"]), len(routine["outs"])

    def body(*refs):
        ins, outs, sems = refs[:n_in], refs[n_in:n_in + n_out], refs[n_in + n_out:]
        routine["start"](ins, outs, *sems)
        routine["finish"](ins, outs, *sems)

    any_spec = pl.BlockSpec(memory_space=pl.ANY)
    return pl.pallas_call(
        body, name=name, out_shape=routine["outs"],
        in_specs=[any_spec] * n_in, out_specs=[any_spec] * n_out,
        scratch_shapes=[pltpu.SemaphoreType.DMA((routine["n_sem"],)), pltpu.SemaphoreType.DMA((routine["n_sem"],)),
                        pltpu.SemaphoreType.DMA((routine["n_local"],))],
    )(*routine["ins"])


def add_own(g, got, core, *, cols, name):
    n, R, C = got.shape
    tr = _tile(R, 256, 8)
    if cols:
        gspec = pl.BlockSpec((tr, C), lambda q, i, c_ref: (i, 2 * q + c_ref[0]))
    else:
        gspec = pl.BlockSpec((1, tr, C), lambda q, i, c_ref: (2 * q + c_ref[0], i, 0))
    spec = pl.BlockSpec((1, tr, C), lambda q, i, c_ref: (q, i, 0))

    def body(c_ref, g_ref, b_ref, o_ref):
        gv = g_ref[...] if cols else g_ref[0]
        o_ref[0] = (gv.astype(F32) + b_ref[0].astype(F32)).astype(o_ref.dtype)

    return pl.pallas_call(
        body, name=name,
        grid_spec=pltpu.PrefetchScalarGridSpec(num_scalar_prefetch=1, grid=(n, R // tr),
                                               in_specs=[gspec, spec], out_specs=spec),
        out_shape=jax.ShapeDtypeStruct((n, R, C), g.dtype), compiler_params=_cp(),
    )(core, g, got)


def sum_chips(h, got, chip, *, name):
    _, R, C = h.shape
    tr = _tile(R, 256, 8)

    def body(q_ref, h_ref, g_ref, o_ref):
        acc = h_ref[0].astype(F32)
        for k in range(N_CHIP - 1):
            acc = acc + g_ref[k].astype(F32)
        o_ref[...] = acc

    return pl.pallas_call(
        body, name=name,
        grid_spec=pltpu.PrefetchScalarGridSpec(
            num_scalar_prefetch=1, grid=(R // tr,),
            in_specs=[pl.BlockSpec((1, tr, C), lambda i, q_ref: (q_ref[0], i, 0)),
                      pl.BlockSpec((N_CHIP - 1, tr, C), lambda i, q_ref: (0, i, 0))],
            out_specs=pl.BlockSpec((tr, C), lambda i, q_ref: (i, 0))),
        out_shape=jax.ShapeDtypeStruct((R, C), F32), compiler_params=_cp(),
    )(chip, h, got)


def reduce_scatter(g, *, cols, name):
    x, y, c = _position()
    core = jnp.reshape(c, (1,)).astype(jnp.int32)
    chip = jnp.reshape(2 * x + y, (1,)).astype(jnp.int32)
    got = comm_call(rs_pair_routine(g, cols), name=name + "_pair")[0]
    h = add_own(g, got, core, cols=cols, name=name + "_add")
    return sum_chips(h, comm_call(rs_chip_routine(h), name=name + "_chip")[0], chip, name=name + "_sum")


def sum_blocks(stack, *, name):
    n, R, C = stack.shape
    tr = _tile(R, 256, 8)

    def body(x_ref, o_ref):
        acc = x_ref[0].astype(F32)
        for i in range(1, n):
            acc = acc + x_ref[i].astype(F32)
        o_ref[...] = acc

    return pl.pallas_call(
        body, name=name, grid=(R // tr,),
        in_specs=[pl.BlockSpec((n, tr, C), lambda i: (0, i, 0))],
        out_specs=pl.BlockSpec((tr, C), lambda i: (i, 0)),
        out_shape=jax.ShapeDtypeStruct((R, C), F32), compiler_params=_cp(),
    )(stack)


PACK_COLS = 1024
PACK_ROW_MULT = 8


def _pack(arrays, dtype):
    flat = jnp.concatenate([a.reshape(-1).astype(dtype) for a in arrays])
    n = flat.shape[0]
    unit = PACK_COLS * PACK_ROW_MULT
    padded = -(-n // unit) * unit
    return jnp.pad(flat, (0, padded - n)).reshape(padded // PACK_COLS, PACK_COLS)


def _unpack(packed, shapes):
    flat = packed.reshape(-1)
    out, o = [], 0
    for s in shapes:
        n = int(np.prod(s))
        out.append(flat[o:o + n].reshape(s))
        o += n
    return out


def fsdp_cols(shard, *, name):
    K, n = shard.shape
    npad = -(-n // LANES) * LANES

    @jax.custom_vjp
    def f(p):
        p = jnp.pad(p, ((0, 0), (0, npad - n))) if npad != n else p
        return all_gather(p.astype(BF16), cols=True, name=name + "_ag")

    def fwd(p):
        return f(p), None

    def bwd(_, g):
        d = reduce_scatter(g, cols=True, name=name + "_rs")
        return (d[:, :n] if npad != n else d,)

    f.defvjp(fwd, bwd)
    return f(shard)


def fsdp_rows(shard, *, name):
    k, N = shard.shape

    @jax.custom_vjp
    def f(p):
        return all_gather(p.astype(BF16), name=name + "_ag").reshape(N_DEV * k, N)

    def fwd(p):
        return f(p), None

    def bwd(_, g):
        return (reduce_scatter(g.reshape(N_DEV, k, N), cols=False, name=name + "_rs"),)

    f.defvjp(fwd, bwd)
    return f(shard)


def _unpad_cols(w, n):
    K = w.shape[0]
    npad = w.shape[1] // N_DEV
    if npad == n:
        return w
    return w.reshape(K, N_DEV, npad)[:, :, :n].reshape(K, N_DEV * n)


def gather_rows(part, me, *, name):
    rows, n = part.shape
    per = rows // N_DEV

    @jax.custom_vjp
    def f(part):
        full = all_gather(part, name=name + "_fwd")
        mine = lax.dynamic_slice_in_dim(full, me * per, per, axis=1)
        return jnp.swapaxes(mine, 0, 1).reshape(per, N_DEV * n)

    def fwd(part):
        return f(part), None

    def bwd(_, g):
        full = all_gather(g, name=name + "_bwd")
        mine = lax.dynamic_slice_in_dim(full, me * n, n, axis=2)
        return (mine.reshape(rows, n),)

    f.defvjp(fwd, bwd)
    return f(part)


def _seg_layout():
    offs = np.concatenate([[0], np.cumsum(IN_SPLITS)])
    cols, widths, leads = [], [], []
    for s in SEG_ORDER:
        cols.append((int(offs[s]), int(offs[s + 1])))
        widths.append(SEG_PAD.get(s, IN_SPLITS[s]))
        leads.append(SEG_LEAD.get(s, 0))
    return cols, widths, leads


def _arrange_w_in(w, n):
    D = w.shape[0]
    npad = w.shape[1] // N_DEV
    cols, widths, leads = _seg_layout()

    def pieces_of(a, b):
        out = []
        for d in range(N_DEV):
            lo, hi = max(a, n * d), min(b, n * (d + 1))
            if lo < hi:
                out.append((d, lo, hi))
        return out

    @jax.custom_vjp
    def f(w):
        parts = []
        for (a, b), wd, ld in zip(cols, widths, leads):
            if ld:
                parts.append(jnp.zeros((D, ld), w.dtype))
            parts += [w[:, npad * d + lo - n * d:npad * d + hi - n * d] for d, lo, hi in pieces_of(a, b)]
            if wd != ld + b - a:
                parts.append(jnp.zeros((D, wd - ld - (b - a)), w.dtype))
        parts.append(jnp.zeros((D, IN_WIDTH - sum(widths)), w.dtype))
        return jnp.concatenate(parts, axis=1)

    def fwd(w):
        return f(w), None

    def bwd(_, g):
        offs = np.concatenate([[0], np.cumsum(widths)])
        runs = []
        for ((a, b), off, ld) in zip(cols, offs[:-1], leads):
            runs += [(lo, int(off) + ld + lo - a, hi - lo) for _, lo, hi in pieces_of(a, b)]
        runs.sort()
        parts, d_next = [], 1
        for lo, o, ln in runs:
            while lo >= n * d_next:
                parts.append(jnp.zeros((D, npad - n), g.dtype))
                d_next += 1
            parts.append(g[:, o:o + ln])
        parts.append(jnp.zeros((D, npad - n), g.dtype))
        return (jnp.concatenate(parts, axis=1),)

    f.defvjp(fwd, bwd)
    return f(w)


def split_cols(proj, widths):
    @jax.custom_vjp
    def f(p):
        outs, o = [], 0
        for wd in widths:
            outs.append(p[:, o:o + wd])
            o += wd
        return tuple(outs)

    def fwd(p):
        return f(p), None

    def bwd(_, gs):
        rest = proj.shape[1] - sum(widths)
        tail = [jnp.zeros((proj.shape[0], rest), proj.dtype)] if rest else []
        return (jnp.concatenate(list(gs) + tail, axis=1),)

    f.defvjp(fwd, bwd)
    return f(proj)


BIG = ("w_in", "w_uq", "w_ukv", "conv_w", "w_out", "w_gate_up", "w_down")
SMALL = ("b_ada", "norm1_g", "norm2_g", "q_norm_g", "k_norm_g", "mla_q_norm_g", "mla_kv_norm_g",
         "conv_b", "dt_bias", "a_log", "d_skip", "ssd_norm_g", "final_norm_g")
WEIGHTS = ("w_ada", "b_ada", "norm1_g", "norm2_g", "w_in", "q_norm_g", "k_norm_g", "mla_q_norm_g",
           "w_uq", "mla_kv_norm_g", "w_ukv", "conv_w", "conv_b", "dt_bias", "a_log", "d_skip",
           "ssd_norm_g", "w_out", "w_gate_up", "w_down", "final_norm_g")


PRE = ("w_in", "w_uq", "w_ukv", "conv_w")


def _layer(l, x, mod, W, shards, nxt, P, tabs):
    B, S, D = x.shape
    T = B * S
    nm = f"l{l}_"
    shift1, scale1, gate1, shift2, scale2, gate2 = [m[:, None, :] for m in jnp.split(mod, 6, axis=-1)]
    cos_a, sin_a, cos_b, sin_b, cos_k, sin_k = tabs

    h = rmsmod(x, P["norm1_g"][l][None], scale1, shift1, name=nm + "norm1")
    w_in = _arrange_w_in(W["w_in"], IN_COLS // N_DEV)
    proj, (w_out_full,) = linear(h.reshape(T, D), w_in, out_dtype=F32, name=nm + "in",
                                 carry=((shards["w_out"], False, "w_out"),))
    q_a, k_a, v_a, cq, ckv, z, xbc, kpe, dtr = split_cols(proj, _seg_layout()[1])

    qn = group_rmsnorm(q_a, jnp.tile(P["q_norm_g"][l], GQA_H)[None], gs=HEAD, out_dtype=F32, name=nm + "qnorm")
    kn = group_rmsnorm(k_a, jnp.tile(P["k_norm_g"][l], GQA_KV)[None], gs=HEAD, out_dtype=F32, name=nm + "knorm")
    qr = rope(qn.reshape(B, S, -1), cos_a[:, :GQA_H * HEAD], sin_a[:, :GQA_H * HEAD], d=HEAD // 4, name=nm + "qrope")
    kr = rope(kn.reshape(B, S, -1), cos_a[:, :GQA_KV * HEAD], sin_a[:, :GQA_KV * HEAD], d=HEAD // 4, name=nm + "krope")
    o_a, (w_gu_full,) = attention(qr, kr, v_a.reshape(B, S, -1), H=GQA_H, Hkv=GQA_KV, dk=HEAD, dv=HEAD,
                                  scale=HEAD ** -0.5, name=nm + "gqa",
                                  carry=((shards["w_gate_up"], True, "w_gate_up"),))

    slot_pad = MLA_DK - MLA_NOPE - MLA_ROPE
    w_uq = jnp.pad(W["w_uq"].reshape(MLA_QL, MLA_H, MLA_NOPE + MLA_ROPE), ((0, 0), (0, 0), (0, slot_pad)))
    w_uq = w_uq.reshape(MLA_QL, MLA_H * MLA_DK)
    w_ukv = W["w_ukv"].reshape(MLA_KVL, MLA_H, MLA_NOPE + MLA_V)
    w_ukv = jnp.concatenate(
        [jnp.pad(w_ukv[:, :, :MLA_NOPE], ((0, 0), (0, 0), (0, MLA_DK - MLA_NOPE))).reshape(MLA_KVL, -1),
         w_ukv[:, :, MLA_NOPE:].reshape(MLA_KVL, -1)], axis=1)
    cqn = group_rmsnorm(cq, P["mla_q_norm_g"][l][None], gs=MLA_QL, out_dtype=BF16, name=nm + "cqnorm")
    ckvn = group_rmsnorm(ckv, P["mla_kv_norm_g"][l][None], gs=MLA_KVL, out_dtype=BF16, name=nm + "ckvnorm")
    qb = linear(cqn, w_uq, out_dtype=F32, name=nm + "uq")
    kvb = linear(ckvn, w_ukv, out_dtype=F32, name=nm + "ukv")
    k_slots, v_b = split_cols(kvb, (MLA_H * MLA_DK, MLA_H * MLA_V))
    q_cat = rope(qb.reshape(B, S, -1), cos_b, sin_b, d=MLA_ROPE // 4, name=nm + "qpe_rope")
    k_pe = rope(kpe.reshape(B, S, -1), cos_k, sin_k, d=MLA_ROPE // 4, name=nm + "kpe_rope")
    k_cat = k_slots.reshape(B, S, -1) + jnp.tile(k_pe, (1, 1, MLA_H))
    o_b = attention(q_cat, k_cat, v_b.reshape(B, S, -1), H=MLA_H, Hkv=MLA_H, dk=MLA_DK, dv=MLA_V,
                    scale=(MLA_NOPE + MLA_ROPE) ** -0.5, name=nm + "mla")

    xact = conv_silu(xbc.reshape(B, S, -1), W["conv_w"].astype(F32), P["conv_b"][l][None], name=nm + "conv")
    brow = jnp.pad(P["dt_bias"][l].reshape(1, -1), ((0, 0), (0, LANES - 2 * SSD_H)))
    arow = jnp.pad(-jnp.exp(P["a_log"][l].reshape(1, -1)), ((0, 0), (0, LANES - 2 * SSD_H)))
    raw = dtr.reshape(B, S, LANES)
    y_f = ssd_scan(xact, raw, brow, arow, rev=False, name=nm + "ssd_f")
    y_b = ssd_scan(xact, raw, brow, arow, rev=True, name=nm + "ssd_b")
    dsk = jnp.repeat(P["d_skip"][l], SSD_P)[None]
    o_c = ssd_out(y_f.reshape(T, -1), y_b.reshape(T, -1), xact.reshape(T, -1), z, dsk,
                  P["ssd_norm_g"][l][None], name=nm + "ssd_out")

    o = jnp.concatenate([o_a.reshape(T, -1).astype(BF16), o_b.reshape(T, -1).astype(BF16), o_c], axis=-1)
    mix = linear(o, w_out_full, out_dtype=F32, name=nm + "out")
    x = gated_residual(x, gate1, mix.reshape(B, S, D), name=nm + "res1")

    h = rmsmod(x, P["norm2_g"][l][None], scale2, shift2, name=nm + "norm2")
    gu, (w_down_full,) = linear(h.reshape(T, D), w_gu_full, out_dtype=BF16, name=nm + "gate_up",
                                carry=((shards["w_down"], False, "w_down"),))
    act = swiglu(gu, name=nm + "swiglu")
    if nxt is None:
        ffn, nxt_full = linear(act, w_down_full, out_dtype=F32, name=nm + "down"), None
    else:
        ffn, nxt_full = linear(act, w_down_full, out_dtype=F32, name=nm + "down",
                               carry=tuple((nxt[n], True, "next_" + n) for n in PRE))
    return gated_residual(x, gate2, ffn.reshape(B, S, D), name=nm + "res2"), nxt_full


def kernel(x, c, w_ada, b_ada, norm1_g, norm2_g, w_in, q_norm_g, k_norm_g, mla_q_norm_g, w_uq, mla_kv_norm_g, w_ukv, conv_w, conv_b, dt_bias, a_log, d_skip, ssd_norm_g, w_out, w_gate_up, w_down, final_norm_g, loss_target, m_w_ada, m_b_ada, m_norm1_g, m_norm2_g, m_w_in, m_q_norm_g, m_k_norm_g, m_mla_q_norm_g, m_w_uq, m_mla_kv_norm_g, m_w_ukv, m_conv_w, m_conv_b, m_dt_bias, m_a_log, m_d_skip, m_ssd_norm_g, m_w_out, m_w_gate_up, m_w_down, m_final_norm_g, v_w_ada, v_b_ada, v_norm1_g, v_norm2_g, v_w_in, v_q_norm_g, v_k_norm_g, v_mla_q_norm_g, v_w_uq, v_mla_kv_norm_g, v_w_ukv, v_conv_w, v_conv_b, v_dt_bias, v_a_log, v_d_skip, v_ssd_norm_g, v_w_out, v_w_gate_up, v_w_down, v_final_norm_g):
    args = dict(locals())
    weights = {n: args[n] for n in WEIGHTS}
    moments_m = {n: args["m_" + n] for n in WEIGHTS}
    moments_v = {n: args["v_" + n] for n in WEIGHTS}
    B, S, D = x.shape
    L = w_ada.shape[0]
    T = B * S
    px, py, pc = _position()
    me = 4 * px + 2 * py + pc
    small_shapes = [weights[n].shape for n in SMALL]

    tabs = (*rope_tables(S, HEAD, GQA_H * HEAD),
            *rope_tables(S, MLA_ROPE, MLA_H * MLA_DK, slot=MLA_DK, lead=MLA_NOPE),
            *rope_tables(S, MLA_ROPE, MLA_DK, slot=MLA_DK, lead=MLA_NOPE))
    c_all = all_gather(c, name="gather_c").reshape(N_DEV * B, D)

    def local_loss(big, w_ada_s, small, x):
        P = dict(zip(SMALL, small))
        pre = [fsdp_cols(big[n][0], name=f"l0_{n}") for n in PRE]
        for l in range(L):
            W = {n: full if n == "w_in" else _unpad_cols(full, big[n].shape[2]) for n, full in zip(PRE, pre)}
            shards = {n: big[n][l] for n in ("w_out", "w_gate_up", "w_down")}
            nxt = {n: big[n][l + 1] for n in PRE} if l + 1 < L else None
            part = linear(c_all, w_ada_s[l], out_dtype=F32, a_silu=True, name=f"l{l}_ada")
            mod = gather_rows(part, me, name=f"l{l}_mod") + P["b_ada"][l][None]
            x, pre = _layer(l, x, mod, W, shards, nxt, P, tabs)
        return final_loss(x.reshape(T, D), P["final_norm_g"][None], loss_target.reshape(T, D), name="loss")

    big = {n: weights[n] for n in BIG}
    small = tuple(weights[n] for n in SMALL)
    loss, (g_big, g_ada, g_small, grad_x) = jax.value_and_grad(local_loss, argnums=(0, 1, 2, 3))(
        big, w_ada, small, x)
    loss = lax.psum(loss, ("x", "y", "c"))

    grads = dict(g_big)
    grads["w_ada"] = g_ada
    g_small_sum = sum_blocks(all_gather(_pack(g_small, F32), name="small_grads_ag"), name="small_grads_sum")
    grads.update(zip(SMALL, _unpack(g_small_sum, small_shapes)))

    delta, new_m, new_v = {}, {}, {}
    for n in ("w_ada",) + BIG:
        delta[n], new_m[n], new_v[n] = adamw(weights[n], grads[n], moments_m[n], moments_v[n], name="adamw_" + n)
    d_, m_, v_ = adamw(_pack([weights[n] for n in SMALL], F32)[None], g_small_sum[None],
                       _pack([moments_m[n] for n in SMALL], F32)[None], _pack([moments_v[n] for n in SMALL], F32)[None],
                       name="adamw_small")
    for tgt, packed in ((delta, d_), (new_m, m_), (new_v, v_)):
        tgt.update(zip(SMALL, _unpack(packed[0], small_shapes)))

    return (loss, grad_x, *[grads[n] for n in WEIGHTS], *[delta[n] for n in WEIGHTS],
            *[new_m[n] for n in WEIGHTS], *[new_v[n] for n in WEIGHTS])
```

```python
import functools
import math

import jax
import jax.numpy as jnp
import numpy as np
from jax import lax
from jax.experimental import pallas as pl
from jax.experimental.pallas import tpu as pltpu

F32 = jnp.float32
BF16 = jnp.bfloat16
N_DEV = 8
EPS = 1e-6
ROPE_THETA = 10000.0
GRID_W = 64

GQA_H, GQA_KV, HEAD = 6, 2, 128
MLA_H, MLA_QL, MLA_KVL, MLA_NOPE, MLA_ROPE, MLA_V = 4, 512, 256, 128, 64, 128
MLA_DK = 256
SSD_H, SSD_P, SSD_G, SSD_N, SSD_K, CHUNK = 12, 64, 2, 128, 5, 128
SSD_INNER = SSD_H * SSD_P
SSD_CONV_DIM = SSD_INNER + 2 * SSD_G * SSD_N
N_PAIR = SSD_H // 2
LANES = 128
IN_SPLITS = (768, 256, 256, 512, 256, 64, 768, 1280, 24)
IN_COLS = sum(IN_SPLITS)
SEG_ORDER = (0, 1, 2, 3, 4, 6, 7, 5, 8)
SEG_PAD = {5: 256, 8: 128}
SEG_LEAD = {5: 128}
IN_WIDTH = 4608

ADAM_LR, ADAM_B1, ADAM_B2, ADAM_EPS, ADAM_WD, ADAM_STEP = 0.001, 0.9, 0.999, 1e-08, 0.01, 10
VMEM_LIMIT = 56 * 1024 * 1024
MM_TM, MM_TN, MM_TK = 1024, 1024, 2048


def _cp(**kw):
    return pltpu.CompilerParams(vmem_limit_bytes=VMEM_LIMIT, **kw)


def _tile(dim, cap, mult=128):
    if dim <= cap:
        return dim
    best = None
    t = mult
    while t <= cap:
        if dim % t == 0:
            best = t
        t += mult
    assert best is not None, (dim, cap)
    return best


def _sigmoid(x):
    return 1.0 / (1.0 + jnp.exp(-x))


def _dot(a, b, dims):
    return lax.dot_general(a, b, (dims, ((), ())), preferred_element_type=F32)


NN = ((1,), (0,))
NT = ((1,), (1,))
TN = ((0,), (0,))


def _dotf(a, b, dims=NN):
    return lax.dot_general(a, b, (dims, ((), ())), preferred_element_type=F32,
                           precision=lax.Precision.HIGHEST)


def _bf(x):
    return x.astype(BF16)


def mm(a, b, *, ta=False, tb=False, out_dtype=F32, a_silu=False, name, carry=(),
       a_halves=False, b_halves=False):
    if a_halves:
        assert not ta
        M, K = a.shape[1], 2 * a.shape[2]
    elif ta:
        K, M = a.shape
    else:
        M, K = a.shape
    if b_halves:
        assert not tb
        K2, N = b.shape[1], 2 * b.shape[2]
    elif tb:
        N, K2 = b.shape
    else:
        K2, N = b.shape
    assert K == K2, (a.shape, b.shape, ta, tb)
    tm = _tile(M, MM_TM)
    tn = _tile(N // 2, MM_TN) if b_halves else _tile(N, MM_TN)
    tk = _tile(K // 2, MM_TK) if a_halves else _tile(K, MM_TK)
    nk = K // tk
    dims = ((0 if ta else 1,), (1 if tb else 0,))

    def partial_product(a_ref, b_ref):
        av = a_ref[...]
        if a_silu:
            av = av.astype(F32)
            av = av * _sigmoid(av)
        return _dot(_bf(av), _bf(b_ref[...]), dims)

    def body_single(a_ref, b_ref, o_ref):
        o_ref[...] = partial_product(a_ref, b_ref).astype(o_ref.dtype)

    def body_acc(a_ref, b_ref, o_ref, acc_ref):
        k = pl.program_id(2)

        @pl.when(k == 0)
        def _():
            acc_ref[...] = partial_product(a_ref, b_ref)

        @pl.when(k > 0)
        def _():
            acc_ref[...] += partial_product(a_ref, b_ref)

        @pl.when(k == nk - 1)
        def _():
            o_ref[...] = acc_ref[...].astype(o_ref.dtype)

    body = body_single if nk == 1 else body_acc

    a_spec = (pl.BlockSpec((tk, tm), lambda i, j, k: (k, i)) if ta
              else pl.BlockSpec((tm, tk), lambda i, j, k: (i, k)))
    b_spec = (pl.BlockSpec((tn, tk), lambda i, j, k: (j, k)) if tb
              else pl.BlockSpec((tk, tn), lambda i, j, k: (k, j)))
    if a_halves:
        a_spec = pl.BlockSpec((None, tm, tk), lambda i, j, k: (k // (nk // 2), i, k % (nk // 2)))
    if b_halves:
        nnh = N // tn // 2
        b_spec = pl.BlockSpec((None, tk, tn), lambda i, j, k: (j // nnh, k, j % nnh))
    res = carry_call(
        body, name=name, grid=(M // tm, N // tn, nk),
        in_specs=[a_spec, b_spec],
        out_specs=[pl.BlockSpec((tm, tn), lambda i, j, k: (i, j))],
        out_shape=[jax.ShapeDtypeStruct((M, N), out_dtype)],
        scratch_shapes=[] if nk == 1 else [pltpu.VMEM((tm, tn), F32)],
        carry=carry, dims=("parallel", "parallel", "arbitrary"),
    )(a, b)
    return res if carry else res[0]


def _shard_prep(p, cols):
    if cols and p.shape[1] % LANES:
        p = jnp.pad(p, ((0, 0), (0, -p.shape[1] % LANES)))
    return p.astype(BF16)


def _full_post(full, cols):
    return full if cols else full.reshape(full.shape[0] * full.shape[1], full.shape[2])


def _rs_begin(g_full, cols, shard_shape, name):
    gg = g_full if cols else g_full.reshape(N_DEV, shard_shape[0], shard_shape[1])
    x, y, c = _position()
    got = comm_call(rs_pair_routine(gg, cols), name=name + "_pair")[0]
    return add_own(gg, got, jnp.reshape(c, (1,)).astype(jnp.int32), cols=cols, name=name + "_add")


def _rs_end(h, got, cols, shard_shape, name):
    x, y, c = _position()
    d = sum_chips(h, got, jnp.reshape(2 * x + y, (1,)).astype(jnp.int32), name=name + "_sum")
    return d[:, :shard_shape[1]] if cols else d


def linear(a, w, *, out_dtype, name, a_silu=False, carry=()):
    kinds = [(cols, s.shape, tag) for s, cols, tag in carry]

    def run(a, w, *shards):
        routines = [ag_routine(_shard_prep(s, cols), cols) for s, (cols, _, _) in zip(shards, kinds)]
        res = mm(a, w, out_dtype=out_dtype, a_silu=a_silu, name=name + "_fwd", carry=routines)
        if not routines:
            return (res,)
        return (res[0], *[_full_post(fu, cols) for fu, (cols, _, _) in zip(res[1:], kinds)])

    @jax.custom_vjp
    def f(a, w, *shards):
        return run(a, w, *shards)

    def fwd(a, w, *shards):
        return run(a, w, *shards), (a, w)

    def bwd(res, cts):
        a, w = res
        g = cts[0]
        hs = [_rs_begin(gf, cols, shp, name + "_" + tag) for gf, (cols, shp, tag) in zip(cts[1:], kinds)]
        routines = [rs_chip_routine(h) for h in hs]
        if a_silu:
            assert not routines
            da = jnp.zeros_like(a)
            gots = []
        else:
            r = mm(g, w, tb=True, out_dtype=a.dtype, name=name + "_da", carry=routines)
            da, gots = (r[0], r[1:]) if routines else (r, [])
        dw = mm(a, g, ta=True, out_dtype=w.dtype, a_silu=a_silu, name=name + "_dw")
        dsh = [_rs_end(h, got, cols, shp, name + "_" + tag)
               for h, got, (cols, shp, tag) in zip(hs, gots, kinds)]
        return (da, dw, *dsh)

    f.defvjp(fwd, bwd)
    out = f(a, w, *[s for s, _, _ in carry])
    return (out[0], list(out[1:])) if carry else out[0]


def swiglu_up(h, w_gu, *, name, carry=()):
    T, D = h.shape
    F = w_gu.shape[1] // 2
    assert D <= MM_TK
    tm, tn = _tile(T, MM_TM), _tile(F, 512)
    nf = F // tn

    def body(a_ref, bg_ref, bu_ref, gu_ref, act_ref):
        av = a_ref[...]
        g = _dot(av, bg_ref[...], NN)
        u = _dot(av, bu_ref[...], NN)
        gu_ref[0] = g.astype(gu_ref.dtype)
        gu_ref[1] = u.astype(gu_ref.dtype)
        act_ref[...] = (g * _sigmoid(g) * u).astype(act_ref.dtype)

    return carry_call(
        body, name=name, grid=(T // tm, nf),
        in_specs=[pl.BlockSpec((tm, D), lambda i, j: (i, 0)), pl.BlockSpec((D, tn), lambda i, j: (0, j)),
                  pl.BlockSpec((D, tn), lambda i, j: (0, j + nf))],
        out_specs=[pl.BlockSpec((2, tm, tn), lambda i, j: (0, i, j)), pl.BlockSpec((tm, tn), lambda i, j: (i, j))],
        out_shape=[jax.ShapeDtypeStruct((2, T, F), BF16), jax.ShapeDtypeStruct((T, F), BF16)],
        carry=carry,
    )(h, w_gu, w_gu)


def swiglu_down_bwd(g, w_dn, gu, *, name, carry=()):
    T, D = g.shape
    F = w_dn.shape[0]
    assert D <= MM_TK
    tm, tn = _tile(T, MM_TM), _tile(F, 512)

    def body(a_ref, b_ref, gu_ref, o_ref):
        dact = _dot(_bf(a_ref[...]), b_ref[...], NT)
        gv = gu_ref[0].astype(F32)
        uv = gu_ref[1].astype(F32)
        sg = _sigmoid(gv)
        o_ref[0] = (dact * uv * (sg * (1.0 + gv * (1.0 - sg)))).astype(o_ref.dtype)
        o_ref[1] = (dact * gv * sg).astype(o_ref.dtype)

    half = pl.BlockSpec((2, tm, tn), lambda i, j: (0, i, j))
    return carry_call(
        body, name=name, grid=(T // tm, F // tn),
        in_specs=[pl.BlockSpec((tm, D), lambda i, j: (i, 0)), pl.BlockSpec((tn, D), lambda i, j: (j, 0)), half],
        out_specs=[half], out_shape=[jax.ShapeDtypeStruct((2, T, F), BF16)], carry=carry,
    )(g, w_dn, gu)


def ffn(h, w_gu, w_dn_shard, *, name, carry=()):
    kinds = [(cols, s.shape, tag) for s, cols, tag in carry]
    dn_shape = w_dn_shard.shape

    def run(h, w_gu, w_dn_shard, *shards):
        gu, act, w_dn = swiglu_up(h, w_gu, name=name + "_up", carry=[ag_routine(_shard_prep(w_dn_shard, False))])
        w_dn = _full_post(w_dn, False)
        routines = [ag_routine(_shard_prep(s, cols), cols) for s, (cols, _, _) in zip(shards, kinds)]
        res = mm(act, w_dn, out_dtype=F32, name=name + "_down", carry=routines)
        y, fulls = (res[0], res[1:]) if routines else (res, [])
        return (y, *[_full_post(fu, cols) for fu, (cols, _, _) in zip(fulls, kinds)]), (h, w_gu, w_dn, gu, act)

    @jax.custom_vjp
    def f(h, w_gu, w_dn_shard, *shards):
        return run(h, w_gu, w_dn_shard, *shards)[0]

    def fwd(h, w_gu, w_dn_shard, *shards):
        return run(h, w_gu, w_dn_shard, *shards)

    def bwd(res, cts):
        h, w_gu, w_dn, gu, act = res
        g = cts[0]
        hs = [_rs_begin(gf, cols, shp, name + "_" + tag) for gf, (cols, shp, tag) in zip(cts[1:], kinds)]
        dgu, *gots = swiglu_down_bwd(g, w_dn, gu, name=name + "_down_da", carry=[rs_chip_routine(x) for x in hs])
        dw_dn = mm(act, g, ta=True, out_dtype=w_dn.dtype, name=name + "_down_dw")
        h_dn = _rs_begin(dw_dn, False, dn_shape, name + "_w_down")
        dh, got_dn = mm(dgu, w_gu, tb=True, a_halves=True, out_dtype=h.dtype, name=name + "_up_da",
                        carry=[rs_chip_routine(h_dn)])
        dw_gu = mm(h, dgu, ta=True, b_halves=True, out_dtype=w_gu.dtype, name=name + "_up_dw")
        d_dn = _rs_end(h_dn, got_dn, False, dn_shape, name + "_w_down")
        dsh = [_rs_end(x, got, cols, shp, name + "_" + tag) for x, got, (cols, shp, tag) in zip(hs, gots, kinds)]
        return (dh, dw_gu, d_dn, *dsh)

    f.defvjp(fwd, bwd)
    out = f(h, w_gu, w_dn_shard, *[s for s, _, _ in carry])
    return out[0], list(out[1:])


def rmsmod(x, g, scale, shift, *, name):
    B, S, D = x.shape
    ts = _tile(S, 256, 8)
    row = pl.BlockSpec((1, ts, D), lambda b, j: (b, j, 0))
    per_b = pl.BlockSpec((1, 1, D), lambda b, j: (b, 0, 0))
    gspec = pl.BlockSpec((1, D), lambda b, j: (0, 0))

    def fwd_call(x, g, scale, shift):
        def body(x_ref, g_ref, sc_ref, sh_ref, o_ref):
            xv = x_ref[0]
            r = lax.rsqrt(jnp.mean(xv * xv, axis=-1, keepdims=True) + EPS)
            y = xv * r * g_ref[...]
            o_ref[0] = (y * (1.0 + sc_ref[0]) + sh_ref[0]).astype(o_ref.dtype)

        return pl.pallas_call(
            body, name=name + "_fwd", grid=(B, S // ts),
            in_specs=[row, gspec, per_b, per_b], out_specs=row,
            out_shape=jax.ShapeDtypeStruct((B, S, D), BF16), compiler_params=_cp(),
        )(x, g, scale, shift)

    def bwd_call(x, g, scale, dh):
        def body(x_ref, g_ref, sc_ref, dh_ref, dx_ref, dg_ref, dsc_ref, dsh_ref):
            j = pl.program_id(1)
            xv = x_ref[0]
            dh = dh_ref[0].astype(F32)
            r = lax.rsqrt(jnp.mean(xv * xv, axis=-1, keepdims=True) + EPS)
            xn = xv * r
            gv = g_ref[...]
            dy = dh * (1.0 + sc_ref[0])
            dxn = dy * gv
            dx_ref[0] = r * (dxn - xn * jnp.mean(dxn * xn, axis=-1, keepdims=True))

            @pl.when(j == 0)
            def _():
                dg_ref[...] = jnp.zeros_like(dg_ref)
                dsc_ref[...] = jnp.zeros_like(dsc_ref)
                dsh_ref[...] = jnp.zeros_like(dsh_ref)

            dg_ref[0] += jnp.sum(dy * xn, axis=0, keepdims=True)
            dsc_ref[0] += jnp.sum(dh * xn * gv, axis=0, keepdims=True)
            dsh_ref[0] += jnp.sum(dh, axis=0, keepdims=True)

        vec = jax.ShapeDtypeStruct((B, 1, D), F32)
        return pl.pallas_call(
            body, name=name + "_bwd", grid=(B, S // ts),
            in_specs=[row, gspec, per_b, row], out_specs=[row, per_b, per_b, per_b],
            out_shape=[jax.ShapeDtypeStruct((B, S, D), F32), vec, vec, vec], compiler_params=_cp(),
        )(x, g, scale, dh)

    @jax.custom_vjp
    def f(x, g, scale, shift):
        return fwd_call(x, g, scale, shift)

    def fwd(x, g, scale, shift):
        return fwd_call(x, g, scale, shift), (x, g, scale)

    def bwd(res, dh):
        x, g, scale = res
        dx, dg, dsc, dsh = bwd_call(x, g, scale, dh)
        return dx, jnp.sum(dg, axis=0), dsc, dsh

    f.defvjp(fwd, bwd)
    return f(x, g, scale, shift)


def group_rmsnorm(x, g, *, gs, out_dtype, name):
    T, W = x.shape
    ng = W // gs
    tr = _tile(T, 512, 8)
    row = pl.BlockSpec((tr, W), lambda i: (i, 0))
    gspec = pl.BlockSpec((1, W), lambda i: (0, 0))

    def fwd_call(x, g):
        def body(x_ref, g_ref, o_ref):
            for i in range(ng):
                sl = slice(i * gs, (i + 1) * gs)
                xv = x_ref[:, sl]
                r = lax.rsqrt(jnp.mean(xv * xv, axis=-1, keepdims=True) + EPS)
                o_ref[:, sl] = (xv * r * g_ref[:, sl]).astype(o_ref.dtype)

        return pl.pallas_call(
            body, name=name + "_fwd", grid=(T // tr,), in_specs=[row, gspec], out_specs=row,
            out_shape=jax.ShapeDtypeStruct((T, W), out_dtype), compiler_params=_cp(),
        )(x, g)

    def bwd_call(x, g, dy):
        def body(x_ref, g_ref, dy_ref, dx_ref, dg_ref):
            @pl.when(pl.program_id(0) == 0)
            def _():
                dg_ref[...] = jnp.zeros_like(dg_ref)

            for i in range(ng):
                sl = slice(i * gs, (i + 1) * gs)
                xv = x_ref[:, sl]
                dyv = dy_ref[:, sl].astype(F32)
                r = lax.rsqrt(jnp.mean(xv * xv, axis=-1, keepdims=True) + EPS)
                xn = xv * r
                dxn = dyv * g_ref[:, sl]
                dx_ref[:, sl] = r * (dxn - xn * jnp.mean(dxn * xn, axis=-1, keepdims=True))
                dg_ref[:, sl] += jnp.sum(dyv * xn, axis=0, keepdims=True)

        return pl.pallas_call(
            body, name=name + "_bwd", grid=(T // tr,), in_specs=[row, gspec, row],
            out_specs=[row, gspec],
            out_shape=[jax.ShapeDtypeStruct((T, W), F32), jax.ShapeDtypeStruct((1, W), F32)],
            compiler_params=_cp(),
        )(x, g, dy)

    @jax.custom_vjp
    def f(x, g):
        return fwd_call(x, g)

    def fwd(x, g):
        return fwd_call(x, g), (x, g)

    def bwd(res, dy):
        return bwd_call(res[0], res[1], dy)

    f.defvjp(fwd, bwd)
    return f(x, g)


def rope_tables(seq_len, rot_dim, width, slot=0, lead=0):
    rows = seq_len // GRID_W
    row_idx = jnp.repeat(jnp.arange(rows), GRID_W).astype(F32)
    col_idx = jnp.tile(jnp.arange(GRID_W), rows).astype(F32)
    axis_dim = rot_dim // 2
    inv_freq = jnp.power(ROPE_THETA, -jnp.arange(0, axis_dim, 2, dtype=F32) / axis_dim)
    ang_r = row_idx[:, None] * inv_freq[None, :]
    ang_c = col_idx[:, None] * inv_freq[None, :]
    cos = jnp.concatenate([jnp.cos(ang_r), jnp.cos(ang_r), jnp.cos(ang_c), jnp.cos(ang_c)], axis=-1)
    sin = jnp.concatenate([-jnp.sin(ang_r), jnp.sin(ang_r), -jnp.sin(ang_c), jnp.sin(ang_c)], axis=-1)
    if slot:
        ones = jnp.ones((seq_len, 1), F32)
        cos = jnp.concatenate([ones * jnp.ones((1, lead)), cos, ones * jnp.ones((1, slot - lead - rot_dim))], axis=-1)
        sin = jnp.concatenate([ones * jnp.zeros((1, lead)), sin, ones * jnp.zeros((1, slot - lead - rot_dim))], axis=-1)
        rot_dim = slot
    reps = width // rot_dim
    return jnp.tile(cos, (1, reps)), jnp.tile(sin, (1, reps))


def rope(x, cos, sin, *, d, name):
    B, S, W = x.shape
    ts = _tile(S, 512, 8)
    row = pl.BlockSpec((1, ts, W), lambda b, j: (b, j, 0))
    tab = pl.BlockSpec((ts, W), lambda b, j: (j, 0))

    def call(x, inverse, nm):
        def body(x_ref, c_ref, s_ref, o_ref):
            xv = x_ref[0]
            lane = lax.broadcasted_iota(jnp.int32, xv.shape, 1)
            first = (lane // d) % 2 == 0

            def swap(v):
                return jnp.where(first, pltpu.roll(v, W - d, 1), pltpu.roll(v, d, 1))

            if inverse:
                o_ref[0] = xv * c_ref[...] + swap(xv * s_ref[...])
            else:
                o_ref[0] = xv * c_ref[...] + swap(xv) * s_ref[...]

        return pl.pallas_call(
            body, name=nm, grid=(B, S // ts), in_specs=[row, tab, tab], out_specs=row,
            out_shape=jax.ShapeDtypeStruct((B, S, W), F32), compiler_params=_cp(),
        )(x, cos, sin)

    @jax.custom_vjp
    def f(x):
        return call(x, False, name + "_fwd")

    def fwd(x):
        return call(x, False, name + "_fwd"), None

    def bwd(_, g):
        return (call(g, True, name + "_bwd"),)

    f.defvjp(fwd, bwd)
    return f(x)


def attention(q, k, v, *, H, Hkv, dk, dv, scale, name, carry=()):
    B, S, _ = q.shape
    rep = H // Hkv
    tq = _tile(S, 256, 8)
    kinds = [(cols, s.shape, tag) for s, cols, tag in carry]

    def fwd_call(q, k, v, routines=()):
        def body(q_ref, k_ref, v_ref, o_ref, lse_ref):
            s = _dot(_bf(q_ref[0]), _bf(k_ref[0]), NT) * scale
            m = jnp.max(s, axis=-1, keepdims=True)
            p = jnp.exp(s - m)
            l = jnp.sum(p, axis=-1, keepdims=True)
            o_ref[0] = _dot(_bf(p), _bf(v_ref[0]), NN) / l
            lse_ref[0, 0] = m + jnp.log(l)

        return carry_call(
            body, name=name + "_fwd", grid=(B, H, S // tq),
            in_specs=[pl.BlockSpec((1, tq, dk), lambda b, h, i: (b, i, h)),
                      pl.BlockSpec((1, S, dk), lambda b, h, i: (b, 0, h // rep)),
                      pl.BlockSpec((1, S, dv), lambda b, h, i: (b, 0, h // rep))],
            out_specs=[pl.BlockSpec((1, tq, dv), lambda b, h, i: (b, i, h)),
                       pl.BlockSpec((1, 1, tq, 1), lambda b, h, i: (b, h, i, 0))],
            out_shape=[jax.ShapeDtypeStruct((B, S, H * dv), F32),
                       jax.ShapeDtypeStruct((B, H, S, 1), F32)],
            carry=routines,
        )(q, k, v)

    def bwd_call(q, k, v, o, do, lse, routines=()):
        def body(q_ref, k_ref, v_ref, o_ref, do_ref, lse_ref, dq_ref, dk_ref, dv_ref):
            @pl.when((pl.program_id(2) == 0) & (pl.program_id(3) == 0))
            def _():
                dk_ref[...] = jnp.zeros_like(dk_ref)
                dv_ref[...] = jnp.zeros_like(dv_ref)

            qb = _bf(q_ref[0])
            kb = _bf(k_ref[0])
            dov = do_ref[0]
            dob = _bf(dov)
            s = _dot(qb, kb, NT) * scale
            p = jnp.exp(s - lse_ref[0, 0])
            delta = jnp.sum(dov * o_ref[0], axis=-1, keepdims=True)
            dp = _dot(dob, _bf(v_ref[0]), NT)
            dsb = _bf(p * (dp - delta))
            dq_ref[0] = _dot(dsb, kb, NN) * scale
            dk_ref[0] += _dot(dsb, qb, TN) * scale
            dv_ref[0] += _dot(_bf(p), dob, TN)

        qs = pl.BlockSpec((1, tq, dk), lambda b, g, r, i: (b, i, g * rep + r))
        os_ = pl.BlockSpec((1, tq, dv), lambda b, g, r, i: (b, i, g * rep + r))
        ks = pl.BlockSpec((1, S, dk), lambda b, g, r, i: (b, 0, g))
        vs = pl.BlockSpec((1, S, dv), lambda b, g, r, i: (b, 0, g))
        col = pl.BlockSpec((1, 1, tq, 1), lambda b, g, r, i: (b, g * rep + r, i, 0))
        return carry_call(
            body, name=name + "_bwd", grid=(B, Hkv, rep, S // tq),
            in_specs=[qs, ks, vs, os_, os_, col], out_specs=[qs, ks, vs],
            out_shape=[jax.ShapeDtypeStruct(q.shape, F32), jax.ShapeDtypeStruct(k.shape, F32),
                       jax.ShapeDtypeStruct(v.shape, F32)],
            carry=routines,
        )(q, k, v, o, do, lse)

    def run(q, k, v, *shards):
        routines = [ag_routine(_shard_prep(s, cols), cols) for s, (cols, _, _) in zip(shards, kinds)]
        o, lse, *fulls = fwd_call(q, k, v, routines)
        return (o, *[_full_post(fu, cols) for fu, (cols, _, _) in zip(fulls, kinds)]), lse

    @jax.custom_vjp
    def f(q, k, v, *shards):
        return run(q, k, v, *shards)[0]

    def fwd(q, k, v, *shards):
        outs, lse = run(q, k, v, *shards)
        return outs, (q, k, v, outs[0], lse)

    def bwd(res, cts):
        q, k, v, o, lse = res
        hs = [_rs_begin(gf, cols, shp, name + "_" + tag) for gf, (cols, shp, tag) in zip(cts[1:], kinds)]
        dq, dk_, dv_, *gots = bwd_call(q, k, v, o, cts[0], lse, [rs_chip_routine(h) for h in hs])
        dsh = [_rs_end(h, got, cols, shp, name + "_" + tag)
               for h, got, (cols, shp, tag) in zip(hs, gots, kinds)]
        return (dq, dk_, dv_, *dsh)

    f.defvjp(fwd, bwd)
    out = f(q, k, v, *[s for s, _, _ in carry])
    return (out[0], list(out[1:])) if carry else out[0]


def conv_silu(x, w, b, *, name):
    B, S, C = x.shape
    tc = _tile(C, 256)
    pad = SSD_K // 2
    xs = pl.BlockSpec((1, S, tc), lambda bi, j: (bi, 0, j))
    ws = pl.BlockSpec((SSD_K, tc), lambda bi, j: (0, j))
    bs = pl.BlockSpec((1, tc), lambda bi, j: (0, j))

    def shifted(v, off):
        if off == 0:
            return v
        t = lax.broadcasted_iota(jnp.int32, v.shape, 0)
        r = pltpu.roll(v, (-off) % S, 0)
        return jnp.where((t + off >= 0) & (t + off < S), r, 0.0)

    def pre_act(xv, wv, bv):
        u = jnp.zeros_like(xv) + bv
        for k in range(SSD_K):
            u = u + wv[k:k + 1, :] * shifted(xv, k - pad)
        return u

    def fwd_call(x, w, b):
        def body(x_ref, w_ref, b_ref, o_ref):
            u = pre_act(x_ref[0], w_ref[...], b_ref[...])
            o_ref[0] = u * _sigmoid(u)

        return pl.pallas_call(
            body, name=name + "_fwd", grid=(B, C // tc), in_specs=[xs, ws, bs], out_specs=xs,
            out_shape=jax.ShapeDtypeStruct((B, S, C), F32), compiler_params=_cp(),
        )(x, w, b)

    def bwd_call(x, w, b, dy):
        def body(x_ref, w_ref, b_ref, dy_ref, dx_ref, dw_ref):
            xv = x_ref[0]
            wv = w_ref[...]
            u = pre_act(xv, wv, b_ref[...])
            sg = _sigmoid(u)
            du = dy_ref[0] * (sg * (1.0 + u * (1.0 - sg)))
            dx = jnp.zeros_like(xv)
            for k in range(SSD_K):
                dx = dx + wv[k:k + 1, :] * shifted(du, pad - k)
                dw_ref[0, k:k + 1, :] = jnp.sum(du * shifted(xv, k - pad), axis=0, keepdims=True)
            dw_ref[0, SSD_K:SSD_K + 1, :] = jnp.sum(du, axis=0, keepdims=True)
            dw_ref[0, SSD_K + 1:8, :] = jnp.zeros((8 - SSD_K - 1, tc), F32)
            dx_ref[0] = dx

        return pl.pallas_call(
            body, name=name + "_bwd", grid=(B, C // tc), in_specs=[xs, ws, bs, xs],
            out_specs=[xs, pl.BlockSpec((1, 8, tc), lambda bi, j: (bi, 0, j))],
            out_shape=[jax.ShapeDtypeStruct((B, S, C), F32), jax.ShapeDtypeStruct((B, 8, C), F32)],
            compiler_params=_cp(),
        )(x, w, b, dy)

    @jax.custom_vjp
    def f(x, w, b):
        return fwd_call(x, w, b)

    def fwd(x, w, b):
        return fwd_call(x, w, b), (x, w, b)

    def bwd(res, dy):
        x, w, b = res
        dx, dwb = bwd_call(x, w, b, dy)
        dwb = jnp.sum(dwb, axis=0)
        return dx, dwb[:SSD_K], dwb[SSD_K:SSD_K + 1]

    f.defvjp(fwd, bwd)
    return f(x, w, b)


def _softplus(x):
    return jnp.maximum(x, 0.0) + jnp.log1p(jnp.exp(-jnp.abs(x)))


def _ssd_prep(raw, raw_t, brow, arow, bcol, acol, rev):
    li = lax.broadcasted_iota(jnp.int32, (CHUNK, CHUNK), 0)
    ki = lax.broadcasted_iota(jnp.int32, (CHUNK, CHUNK), 1)
    later = (li <= ki) if rev else (li >= ki)
    dt = _softplus(raw + brow)
    a = dt * arow
    cs = _dotf(later.astype(F32), a)
    tot = jnp.sum(a, axis=0, keepdims=True)
    a_t = _softplus(raw_t + bcol) * acol
    earlier = (li >= ki) if rev else (li <= ki)
    cs_t = _dotf(a_t, earlier.astype(F32))
    return dt, a, cs, tot, cs_t, later


def _lane_pick(mat, j):
    lane = lax.broadcasted_iota(jnp.int32, mat.shape, 1)
    return jnp.sum(jnp.where(lane == j, mat, 0.0), axis=1, keepdims=True)


def _head_sum(t, first):
    s0 = jnp.sum(jnp.where(first, t, 0.0), axis=1, keepdims=True)
    s1 = jnp.sum(jnp.where(first, 0.0, t), axis=1, keepdims=True)
    return s0, s1


def ssd_scan(xbc, raw, brow, arow, *, rev, name):
    B, S, _ = xbc.shape
    NC = S // CHUNK
    off = SSD_H if rev else 0
    n_dt = 2 * SSD_H

    def chunk_of(c):
        return (NC - 1 - c) if rev else c

    def specs(cmap):
        return dict(
            x=pl.BlockSpec((1, CHUNK, SSD_INNER), lambda b, c: (b, cmap(c), 0)),
            bm=pl.BlockSpec((1, CHUNK, 2 * SSD_N), lambda b, c: (b, cmap(c), SSD_INNER // (2 * SSD_N))),
            cm=pl.BlockSpec((1, CHUNK, 2 * SSD_N), lambda b, c: (b, cmap(c), SSD_INNER // (2 * SSD_N) + 1)),
            raw=pl.BlockSpec((1, CHUNK, LANES), lambda b, c: (b, cmap(c), 0)),
            raw_t=pl.BlockSpec((1, n_dt, CHUNK), lambda b, c: (b, 0, cmap(c))),
            row=pl.BlockSpec((1, LANES), lambda b, c: (0, 0)),
            colv=pl.BlockSpec((n_dt, 1), lambda b, c: (0, 0)),
            hs=pl.BlockSpec((1, 1, N_PAIR, SSD_N, LANES), lambda b, c: (b, cmap(c), 0, 0, 0)),
        )

    def head_terms(prep, j, first_dummy=None):
        dt, a, cs, tot, cs_t, later = prep
        cs_c = _lane_pick(cs, j)
        cs_r = cs_t[j:j + 1, :]
        dt_c = _lane_pick(dt, j)
        tot_j = _lane_pick(tot, j)
        L = jnp.exp(jnp.where(later, cs_c - cs_r, -1e30))
        return cs_c, cs_r, dt_c, tot_j, L

    def fwd_call(xbc, raw, raw_t, brow, arow, bcol, acol):
        def body(x_ref, bm_ref, cm_ref, raw_ref, rawt_ref, brow_ref, arow_ref, bcol_ref, acol_ref,
                 y_ref, hs_ref, st_ref):
            @pl.when(pl.program_id(1) == 0)
            def _():
                st_ref[...] = jnp.zeros_like(st_ref)

            prep = _ssd_prep(raw_ref[0], rawt_ref[0], brow_ref[...], arow_ref[...],
                             bcol_ref[...], acol_ref[...], rev)
            lane = lax.broadcasted_iota(jnp.int32, (CHUNK, LANES), 1)
            first = lane < SSD_P
            for g in range(SSD_G):
                Bg = _bf(bm_ref[0, :, g * SSD_N:(g + 1) * SSD_N])
                Cg = _bf(cm_ref[0, :, g * SSD_N:(g + 1) * SSD_N])
                G = _dot(Cg, Bg, NT)
                for pp in range(N_PAIR // SSD_G):
                    pi = g * (N_PAIR // SSD_G) + pp
                    c0, _, d0, t0, L0 = head_terms(prep, off + 2 * pi)
                    c1, _, d1, t1, L1 = head_terms(prep, off + 2 * pi + 1)
                    xd = x_ref[0, :, pi * LANES:(pi + 1) * LANES] * jnp.where(first, d0, d1)
                    xdb = _bf(xd)
                    y = jnp.where(first, _dot(_bf(G * L0), xdb, NN), _dot(_bf(G * L1), xdb, NN))
                    dec = jnp.where(first, jnp.exp(t0 - c0), jnp.exp(t1 - c1))
                    h_prev = st_ref[pi]
                    hs_ref[0, 0, pi] = h_prev
                    y = y + _dot(Cg, _bf(h_prev), NN) * jnp.where(first, jnp.exp(c0), jnp.exp(c1))
                    y_ref[0, :, pi * LANES:(pi + 1) * LANES] = y
                    etot = jnp.where(first[:1], jnp.exp(t0), jnp.exp(t1))
                    st_ref[pi] = h_prev * etot + _dot(Bg, _bf(xd * dec), TN)

        sp = specs(chunk_of)
        return pl.pallas_call(
            body, name=name + "_fwd", grid=(B, NC),
            in_specs=[sp["x"], sp["bm"], sp["cm"], sp["raw"], sp["raw_t"], sp["row"], sp["row"],
                      sp["colv"], sp["colv"]],
            out_specs=[sp["x"], sp["hs"]],
            out_shape=[jax.ShapeDtypeStruct((B, S, SSD_INNER), F32),
                       jax.ShapeDtypeStruct((B, NC, N_PAIR, SSD_N, LANES), F32)],
            scratch_shapes=[pltpu.VMEM((N_PAIR, SSD_N, LANES), F32)],
            compiler_params=_cp(),
        )(xbc, xbc, xbc, raw, raw_t, brow, arow, bcol, acol)

    def bwd_call(xbc, raw, raw_t, brow, arow, bcol, acol, hs, dy):
        def body(x_ref, bm_ref, cm_ref, raw_ref, rawt_ref, brow_ref, arow_ref, bcol_ref, acol_ref,
                 hs_ref, dy_ref, dxbc_ref, draw_ref, da_ref, dst_ref):
            @pl.when(pl.program_id(1) == 0)
            def _():
                dst_ref[...] = jnp.zeros_like(dst_ref)
                da_ref[...] = jnp.zeros_like(da_ref)

            raw_v = raw_ref[0]
            prep = _ssd_prep(raw_v, rawt_ref[0], brow_ref[...], arow_ref[...],
                             bcol_ref[...], acol_ref[...], rev)
            dt, a, cs, tot, cs_t, later = prep
            li = lax.broadcasted_iota(jnp.int32, (CHUNK, CHUNK), 0)
            ki = lax.broadcasted_iota(jnp.int32, (CHUNK, CHUNK), 1)
            later_t = (li >= ki) if rev else (li <= ki)
            lane = lax.broadcasted_iota(jnp.int32, (CHUNK, LANES), 1)
            first = lane < SSD_P
            dcs_all = jnp.zeros((CHUNK, LANES), F32)
            ddt_all = jnp.zeros((CHUNK, LANES), F32)
            dtot_all = jnp.zeros((1, LANES), F32)
            for g in range(SSD_G):
                Bg = _bf(bm_ref[0, :, g * SSD_N:(g + 1) * SSD_N])
                Cg = _bf(cm_ref[0, :, g * SSD_N:(g + 1) * SSD_N])
                G = _dot(Cg, Bg, NT)
                Gt = _dot(Bg, Cg, NT)
                dG = jnp.zeros((CHUNK, CHUNK), F32)
                dB = jnp.zeros((CHUNK, SSD_N), F32)
                dC = jnp.zeros((CHUNK, SSD_N), F32)
                for pp in range(N_PAIR // SSD_G):
                    pi = g * (N_PAIR // SSD_G) + pp
                    j0, j1 = off + 2 * pi, off + 2 * pi + 1
                    c0, r0, d0, t0, L0 = head_terms(prep, j0)
                    c1, r1, d1, t1, L1 = head_terms(prep, j1)
                    Lt0 = jnp.exp(jnp.where(later_t, r0 - c0, -1e30))
                    Lt1 = jnp.exp(jnp.where(later_t, r1 - c1, -1e30))
                    xv = x_ref[0, :, pi * LANES:(pi + 1) * LANES]
                    dtp = jnp.where(first, d0, d1)
                    xd = xv * dtp
                    xdb = _bf(xd)
                    dyv = dy_ref[0, :, pi * LANES:(pi + 1) * LANES]
                    dyb = _bf(dyv)
                    dec = jnp.where(first, jnp.exp(t0 - c0), jnp.exp(t1 - c1))
                    ecs = jnp.where(first, jnp.exp(c0), jnp.exp(c1))
                    et0, et1 = jnp.exp(t0), jnp.exp(t1)
                    etot = jnp.where(first[:1], et0, et1)
                    h_prev = hs_ref[0, 0, pi]
                    hpb = _bf(h_prev)
                    dhn = dst_ref[pi]
                    dhb = _bf(dhn)
                    W0, W1 = G * L0, G * L1
                    Wt0, Wt1 = Gt * Lt0, Gt * Lt1
                    bdh = _dot(Bg, dhb, NN)
                    dxd = jnp.where(first, _dot(_bf(Wt0), dyb, NN), _dot(_bf(Wt1), dyb, NN)) + bdh * dec
                    dy0 = _bf(jnp.where(first, dyv, 0.0))
                    dy1 = _bf(jnp.where(first, 0.0, dyv))
                    Q0, Q1 = _dot(dy0, xdb, NT), _dot(dy1, xdb, NT)
                    Qt0, Qt1 = _dot(xdb, dy0, NT), _dot(xdb, dy1, NT)
                    dG = dG + Q0 * L0 + Q1 * L1
                    dcs0 = (jnp.sum(Q0 * W0, axis=1, keepdims=True)
                            - jnp.sum(Qt0 * Wt0, axis=1, keepdims=True))
                    dcs1 = (jnp.sum(Q1 * W1, axis=1, keepdims=True)
                            - jnp.sum(Qt1 * Wt1, axis=1, keepdims=True))
                    dye = dyv * ecs
                    dyeb = _bf(dye)
                    s0, s1 = _head_sum(dye * _dot(Cg, hpb, NN), first)
                    dcs0, dcs1 = dcs0 + s0, dcs1 + s1
                    dC = dC + _dot(dyeb, hpb, NT)
                    dB = dB + _dot(_bf(xd * dec), dhb, NT)
                    u0, u1 = _head_sum(xd * bdh * dec, first)
                    dcs0, dcs1 = dcs0 - u0, dcs1 - u1
                    w = jnp.sum(dhn * h_prev, axis=0, keepdims=True)
                    w0, w1 = _head_sum(w, first[:1])
                    dt0 = jnp.sum(u0, axis=0, keepdims=True) + et0 * w0
                    dt1 = jnp.sum(u1, axis=0, keepdims=True) + et1 * w1
                    dst_ref[pi] = _dot(Cg, dyeb, TN) + dhn * etot
                    q0, q1 = _head_sum(dxd * xv, first)
                    dxbc_ref[0, :, pi * LANES:(pi + 1) * LANES] = dxd * dtp
                    dcs_all = dcs_all + jnp.where(lane == j0, dcs0, 0.0) + jnp.where(lane == j1, dcs1, 0.0)
                    ddt_all = ddt_all + jnp.where(lane == j0, q0, 0.0) + jnp.where(lane == j1, q1, 0.0)
                    dtot_all = (dtot_all + jnp.where(lane[:1] == j0, dt0, 0.0)
                                + jnp.where(lane[:1] == j1, dt1, 0.0))
                dGb = _bf(dG)
                dC = dC + _dot(dGb, Bg, NN)
                dB = dB + _dot(dGb, Cg, TN)
                dxbc_ref[0, :, SSD_INNER + g * SSD_N:SSD_INNER + (g + 1) * SSD_N] = dB
                dxbc_ref[0, :, SSD_INNER + (SSD_G + g) * SSD_N:SSD_INNER + (SSD_G + g + 1) * SSD_N] = dC
            da = _dotf(later_t.astype(F32), dcs_all) + dtot_all
            ddt = ddt_all + da * arow_ref[...]
            da_ref[0] += jnp.sum(da * dt, axis=0, keepdims=True)
            draw_ref[0] = ddt * _sigmoid(raw_v + brow_ref[...])

        def rchunk(c):
            return c if rev else (NC - 1 - c)

        sp = specs(rchunk)
        full = pl.BlockSpec((1, CHUNK, SSD_CONV_DIM), lambda b, c: (b, rchunk(c), 0))
        return pl.pallas_call(
            body, name=name + "_bwd", grid=(B, NC),
            in_specs=[sp["x"], sp["bm"], sp["cm"], sp["raw"], sp["raw_t"], sp["row"], sp["row"],
                      sp["colv"], sp["colv"], sp["hs"], sp["x"]],
            out_specs=[full, sp["raw"], pl.BlockSpec((1, 1, LANES), lambda b, c: (b, 0, 0))],
            out_shape=[jax.ShapeDtypeStruct((B, S, SSD_CONV_DIM), F32),
                       jax.ShapeDtypeStruct((B, S, LANES), F32),
                       jax.ShapeDtypeStruct((B, 1, LANES), F32)],
            scratch_shapes=[pltpu.VMEM((N_PAIR, SSD_N, LANES), F32)],
            compiler_params=_cp(),
        )(xbc, xbc, xbc, raw, raw_t, brow, arow, bcol, acol, hs, dy)

    def aux(raw, brow, arow):
        raw_t = jnp.swapaxes(raw[:, :, :n_dt], 1, 2)
        return raw_t, brow[0, :n_dt][:, None], arow[0, :n_dt][:, None]

    @jax.custom_vjp
    def f(xbc, raw, brow, arow):
        raw_t, bcol, acol = aux(raw, brow, arow)
        return fwd_call(xbc, raw, raw_t, brow, arow, bcol, acol)[0]

    def fwd(xbc, raw, brow, arow):
        raw_t, bcol, acol = aux(raw, brow, arow)
        y, hs = fwd_call(xbc, raw, raw_t, brow, arow, bcol, acol)
        return y, (xbc, raw, brow, arow, hs)

    def bwd(res, dy):
        xbc, raw, brow, arow, hs = res
        raw_t, bcol, acol = aux(raw, brow, arow)
        dxbc, draw, da = bwd_call(xbc, raw, raw_t, brow, arow, bcol, acol, hs, dy)
        dbrow = jnp.sum(draw, axis=(0, 1))[None, :]
        return dxbc, draw, dbrow, jnp.sum(da, axis=0)

    f.defvjp(fwd, bwd)
    return f(xbc, raw, brow, arow)


def ssd_out(yf, yb, xbc, z, dsk, g, *, name):
    T, W = yf.shape
    gs = W // SSD_G
    tr = _tile(T, 512, 8)
    row = pl.BlockSpec((tr, W), lambda i: (i, 0))
    vec = pl.BlockSpec((1, W), lambda i: (0, 0))

    def normed(yv, gv):
        outs, rs = [], []
        for i in range(SSD_G):
            sl = slice(i * gs, (i + 1) * gs)
            r = lax.rsqrt(jnp.mean(yv[:, sl] * yv[:, sl], axis=-1, keepdims=True) + EPS)
            rs.append(r)
            outs.append(yv[:, sl] * r)
        return outs, rs

    def fwd_call(yf, yb, xbc, z, dsk, g):
        def body(yf_ref, yb_ref, xs_ref, z_ref, dsk_ref, g_ref, o_ref):
            zv = z_ref[...]
            yv = (yf_ref[...] + yb_ref[...] + xs_ref[...] * dsk_ref[...]) * (zv * _sigmoid(zv))
            outs, _ = normed(yv, g_ref[...])
            for i in range(SSD_G):
                sl = slice(i * gs, (i + 1) * gs)
                o_ref[:, sl] = (outs[i] * g_ref[:, sl]).astype(o_ref.dtype)

        return pl.pallas_call(
            body, name=name + "_fwd", grid=(T // tr,), in_specs=[row, row, row, row, vec, vec],
            out_specs=row, out_shape=jax.ShapeDtypeStruct((T, W), BF16), compiler_params=_cp(),
        )(yf, yb, xbc, z, dsk, g)

    def bwd_call(yf, yb, xbc, z, dsk, g, do):
        def body(yf_ref, yb_ref, xs_ref, z_ref, dsk_ref, g_ref, do_ref, dy_ref, dxs_ref, dz_ref,
                 ddsk_ref, dg_ref):
            @pl.when(pl.program_id(0) == 0)
            def _():
                ddsk_ref[...] = jnp.zeros_like(ddsk_ref)
                dg_ref[...] = jnp.zeros_like(dg_ref)

            zv = z_ref[...]
            sg = _sigmoid(zv)
            sz = zv * sg
            xs = xs_ref[...]
            pre = yf_ref[...] + yb_ref[...] + xs * dsk_ref[...]
            yv = pre * sz
            outs, rs = normed(yv, g_ref[...])
            for i in range(SSD_G):
                sl = slice(i * gs, (i + 1) * gs)
                dov = do_ref[:, sl].astype(F32)
                xn = outs[i]
                dxn = dov * g_ref[:, sl]
                dyv = rs[i] * (dxn - xn * jnp.mean(dxn * xn, axis=-1, keepdims=True))
                dg_ref[:, sl] += jnp.sum(dov * xn, axis=0, keepdims=True)
                dpre = dyv * sz[:, sl]
                dy_ref[:, sl] = dpre
                dxs_ref[:, sl] = dpre * dsk_ref[:, sl]
                ddsk_ref[:, sl] += jnp.sum(dpre * xs[:, sl], axis=0, keepdims=True)
                dz_ref[:, sl] = dyv * pre[:, sl] * (sg[:, sl] * (1.0 + zv[:, sl] * (1.0 - sg[:, sl])))

        o = jax.ShapeDtypeStruct((T, W), F32)
        v = jax.ShapeDtypeStruct((1, W), F32)
        return pl.pallas_call(
            body, name=name + "_bwd", grid=(T // tr,), in_specs=[row, row, row, row, vec, vec, row],
            out_specs=[row, row, row, vec, vec], out_shape=[o, o, o, v, v], compiler_params=_cp(),
        )(yf, yb, xbc, z, dsk, g, do)

    @jax.custom_vjp
    def f(yf, yb, xbc, z, dsk, g):
        return fwd_call(yf, yb, xbc, z, dsk, g)

    def fwd(yf, yb, xbc, z, dsk, g):
        return fwd_call(yf, yb, xbc, z, dsk, g), (yf, yb, xbc, z, dsk, g)

    def bwd(res, do):
        dy, dxs, dz, ddsk, dg = bwd_call(*res, do)
        dxbc = jnp.pad(dxs, ((0, 0), (0, res[2].shape[1] - W)))
        return dy, dy, dxbc, dz, ddsk, dg

    f.defvjp(fwd, bwd)
    return f(yf, yb, xbc, z, dsk, g)


def swiglu(gu, *, name):
    T, F2 = gu.shape
    Fh = F2 // 2
    tr, tf = _tile(T, 512, 8), _tile(Fh, 512)
    nf = Fh // tf
    gs = pl.BlockSpec((tr, tf), lambda i, j: (i, j))
    us = pl.BlockSpec((tr, tf), lambda i, j: (i, j + nf))

    def fwd_call(gu):
        def body(g_ref, u_ref, o_ref):
            gv = g_ref[...].astype(F32)
            o_ref[...] = (gv * _sigmoid(gv) * u_ref[...].astype(F32)).astype(o_ref.dtype)

        return pl.pallas_call(
            body, name=name + "_fwd", grid=(T // tr, nf), in_specs=[gs, us], out_specs=gs,
            out_shape=jax.ShapeDtypeStruct((T, Fh), BF16), compiler_params=_cp(),
        )(gu, gu)

    def bwd_call(gu, da):
        def body(g_ref, u_ref, da_ref, dgu_ref):
            j = pl.program_id(1)
            gv = g_ref[...].astype(F32)
            uv = u_ref[...].astype(F32)
            dav = da_ref[...].astype(F32)
            sg = _sigmoid(gv)

            @pl.when(j < nf)
            def _():
                dgu_ref[...] = (dav * uv * (sg * (1.0 + gv * (1.0 - sg)))).astype(dgu_ref.dtype)

            @pl.when(j >= nf)
            def _():
                dgu_ref[...] = (dav * gv * sg).astype(dgu_ref.dtype)

        gsel = pl.BlockSpec((tr, tf), lambda i, j: (i, j % nf))
        usel = pl.BlockSpec((tr, tf), lambda i, j: (i, j % nf + nf))
        return pl.pallas_call(
            body, name=name + "_bwd", grid=(T // tr, 2 * nf), in_specs=[gsel, usel, gsel],
            out_specs=pl.BlockSpec((tr, tf), lambda i, j: (i, j)),
            out_shape=jax.ShapeDtypeStruct((T, F2), BF16), compiler_params=_cp(),
        )(gu, gu, da)

    @jax.custom_vjp
    def f(gu):
        return fwd_call(gu)

    def fwd(gu):
        return fwd_call(gu), gu

    def bwd(gu, da):
        return (bwd_call(gu, da),)

    f.defvjp(fwd, bwd)
    return f(gu)


def gated_residual(x, gate, y, *, name):
    B, S, D = x.shape
    ts = _tile(S, 256, 8)
    row = pl.BlockSpec((1, ts, D), lambda b, j: (b, j, 0))
    per_b = pl.BlockSpec((1, 1, D), lambda b, j: (b, 0, 0))

    def fwd_call(x, gate, y):
        def body(x_ref, gt_ref, y_ref, o_ref):
            o_ref[0] = x_ref[0] + gt_ref[0] * y_ref[0]

        return pl.pallas_call(
            body, name=name + "_fwd", grid=(B, S // ts), in_specs=[row, per_b, row], out_specs=row,
            out_shape=jax.ShapeDtypeStruct((B, S, D), F32), compiler_params=_cp(),
        )(x, gate, y)

    def bwd_call(gate, y, g):
        def body(gt_ref, y_ref, g_ref, dy_ref, dgt_ref):
            @pl.when(pl.program_id(1) == 0)
            def _():
                dgt_ref[...] = jnp.zeros_like(dgt_ref)

            gv = g_ref[0]
            dy_ref[0] = gt_ref[0] * gv
            dgt_ref[0] += jnp.sum(gv * y_ref[0], axis=0, keepdims=True)

        return pl.pallas_call(
            body, name=name + "_bwd", grid=(B, S // ts), in_specs=[per_b, row, row],
            out_specs=[row, per_b],
            out_shape=[jax.ShapeDtypeStruct((B, S, D), F32), jax.ShapeDtypeStruct((B, 1, D), F32)],
            compiler_params=_cp(),
        )(gate, y, g)

    @jax.custom_vjp
    def f(x, gate, y):
        return fwd_call(x, gate, y)

    def fwd(x, gate, y):
        return fwd_call(x, gate, y), (gate, y)

    def bwd(res, g):
        dy, dgate = bwd_call(res[0], res[1], g)
        return g, dgate, dy

    f.defvjp(fwd, bwd)
    return f(x, gate, y)


def final_loss(x, g, target, *, name):
    T, D = x.shape
    tr = _tile(T, 256, 8)
    row = pl.BlockSpec((tr, D), lambda i: (i, 0))
    vec = pl.BlockSpec((1, D), lambda i: (0, 0))

    def fwd_call(x, g, target):
        def body(x_ref, g_ref, t_ref, o_ref):
            @pl.when(pl.program_id(0) == 0)
            def _():
                o_ref[...] = jnp.zeros_like(o_ref)

            xv = x_ref[...]
            r = lax.rsqrt(jnp.mean(xv * xv, axis=-1, keepdims=True) + EPS)
            e = xv * r * g_ref[...] - t_ref[...]
            o_ref[...] += jnp.sum(e * e, axis=0, keepdims=True)

        part = pl.pallas_call(
            body, name=name + "_fwd", grid=(T // tr,), in_specs=[row, vec, row], out_specs=vec,
            out_shape=jax.ShapeDtypeStruct((1, D), F32), compiler_params=_cp(),
        )(x, g, target)
        return (0.5 / D) * jnp.sum(part)

    def bwd_call(x, g, target, ct):
        def body(x_ref, g_ref, t_ref, ct_ref, dx_ref, dg_ref):
            @pl.when(pl.program_id(0) == 0)
            def _():
                dg_ref[...] = jnp.zeros_like(dg_ref)

            xv = x_ref[...]
            gv = g_ref[...]
            r = lax.rsqrt(jnp.mean(xv * xv, axis=-1, keepdims=True) + EPS)
            xn = xv * r
            dy = (xn * gv - t_ref[...]) * (ct_ref[...] * (1.0 / D))
            dxn = dy * gv
            dx_ref[...] = r * (dxn - xn * jnp.mean(dxn * xn, axis=-1, keepdims=True))
            dg_ref[...] += jnp.sum(dy * xn, axis=0, keepdims=True)

        return pl.pallas_call(
            body, name=name + "_bwd", grid=(T // tr,),
            in_specs=[row, vec, row, pl.BlockSpec((1, 1), lambda i: (0, 0))], out_specs=[row, vec],
            out_shape=[jax.ShapeDtypeStruct((T, D), F32), jax.ShapeDtypeStruct((1, D), F32)],
            compiler_params=_cp(),
        )(x, g, target, ct)

    @jax.custom_vjp
    def f(x, g, target):
        return fwd_call(x, g, target)

    def fwd(x, g, target):
        return fwd_call(x, g, target), (x, g, target)

    def bwd(res, ct):
        x, g, target = res
        dx, dg = bwd_call(x, g, target, jnp.reshape(ct, (1, 1)).astype(F32))
        return dx, dg, jnp.zeros_like(target)

    f.defvjp(fwd, bwd)
    return f(x, g, target)


def adamw(w, g, m, v, *, name):
    L, R, C = w.shape
    tr = _tile(R, 512, 8)
    spec = pl.BlockSpec((1, tr, C), lambda l, i: (l, i, 0))
    c1 = 1.0 / (1.0 - ADAM_B1 ** ADAM_STEP)
    c2 = 1.0 / (1.0 - ADAM_B2 ** ADAM_STEP)

    def body(w_ref, g_ref, m_ref, v_ref, d_ref, nm_ref, nv_ref):
        gv = g_ref[...]
        nm = ADAM_B1 * m_ref[...] + (1.0 - ADAM_B1) * gv
        nv = ADAM_B2 * v_ref[...] + (1.0 - ADAM_B2) * (gv * gv)
        nm_ref[...] = nm
        nv_ref[...] = nv
        d_ref[...] = -ADAM_LR * ((nm * c1) / (jnp.sqrt(nv * c2) + ADAM_EPS) + ADAM_WD * w_ref[...])

    o = jax.ShapeDtypeStruct((L, R, C), F32)
    return pl.pallas_call(
        body, name=name, grid=(L, R // tr), in_specs=[spec] * 4, out_specs=[spec] * 3,
        out_shape=[o, o, o], compiler_params=_cp(),
    )(w, g, m, v)


def _position():
    x, y, c = lax.axis_index("x"), lax.axis_index("y"), lax.axis_index("c")
    return x, y, c


def ag_routine(shard, cols=False):
    R, C = shard.shape
    assert not cols or C % LANES == 0

    def parts(ins, outs, send_sems, recv_sems, local_sems):
        (x_ref,), (out_ref,) = ins, outs
        x, y, c = _position()
        me, sibling = (x, y, c), (x, y, 1 - c)
        chips = [(1 - x, y), (x, 1 - y), (1 - x, 1 - y)]

        def block(px, py, pc):
            idx = 4 * px + 2 * py + pc
            if cols:
                return out_ref.at[:, pl.ds(pl.multiple_of(idx * C, LANES), C)]
            return out_ref.at[idx]

        def copy(k, blk, to, src=None):
            return pltpu.make_async_remote_copy(
                src_ref=block(*blk) if src is None else src, dst_ref=block(*blk),
                send_sem=send_sems.at[k], recv_sem=recv_sems.at[k],
                device_id=to, device_id_type=pl.DeviceIdType.MESH)

        mine = pltpu.make_async_copy(x_ref, block(*me), local_sems.at[0])
        first = [copy(0, me, sibling, src=x_ref)]
        first += [copy(1 + j, me, (*chip, c), src=x_ref) for j, chip in enumerate(chips)]
        passed = [copy(4 + j, (*chip, c), sibling) for j, chip in enumerate(chips)]
        return me, sibling, c, chips, copy, mine, first, passed

    def start(*refs):
        me, sibling, c, chips, copy, mine, first, passed = parts(*refs)
        mine.start()
        for cp in first:
            cp.start()

    def finish(*refs):
        me, sibling, c, chips, copy, mine, first, passed = parts(*refs)
        for j, chip in enumerate(chips):
            copy(1 + j, (*chip, c), me).wait_recv()
            passed[j].start()
        copy(0, sibling, me).wait_recv()
        for j, chip in enumerate(chips):
            copy(4 + j, (*chip, 1 - c), me).wait_recv()
        for cp in first + passed:
            cp.wait_send()
        mine.wait()

    out = jax.ShapeDtypeStruct((R, N_DEV * C) if cols else (N_DEV, R, C), shard.dtype)
    return dict(ins=[shard], outs=[out], n_sem=7, n_local=1, start=start, finish=finish)


def all_gather(shard, *, name, cols=False):
    return comm_call(ag_routine(shard, cols), name=name)[0]


def carry_call(body, *, name, grid, in_specs, out_specs, out_shape, scratch_shapes=(), carry=(),
               dims=None):
    in_specs, out_specs, out_shape = list(in_specs), list(out_specs), list(out_shape)
    scratch_shapes = list(scratch_shapes)
    if not carry:
        call = pl.pallas_call(body, name=name, grid=grid, in_specs=in_specs, out_specs=out_specs,
                              out_shape=out_shape, scratch_shapes=scratch_shapes,
                              compiler_params=_cp(dimension_semantics=dims) if dims else _cp())
        return lambda *args: list(call(*args))
    n_in, n_out, n_scr = len(in_specs), len(out_specs), len(scratch_shapes)
    c_ins = [a for r in carry for a in r["ins"]]
    c_outs = [o for r in carry for o in r["outs"]]
    sems = []
    for r in carry:
        sems += [pltpu.SemaphoreType.DMA((r["n_sem"],)), pltpu.SemaphoreType.DMA((r["n_sem"],)),
                 pltpu.SemaphoreType.DMA((r["n_local"],))]

    def wrapped(*refs):
        refs = list(refs)
        ins, refs = refs[:n_in], refs[n_in:]
        cin, refs = refs[:len(c_ins)], refs[len(c_ins):]
        outs, refs = refs[:n_out], refs[n_out:]
        cout, refs = refs[:len(c_outs)], refs[len(c_outs):]
        scr, csem = refs[:n_scr], refs[n_scr:]
        ids = [pl.program_id(i) for i in range(len(grid))]
        first = functools.reduce(jnp.logical_and, [i == 0 for i in ids])
        last = functools.reduce(jnp.logical_and, [i == g - 1 for i, g in zip(ids, grid)])

        def each(which):
            io = oo = 0
            for j, r in enumerate(carry):
                r[which](cin[io:io + len(r["ins"])], cout[oo:oo + len(r["outs"])], *csem[3 * j:3 * j + 3])
                io += len(r["ins"])
                oo += len(r["outs"])

        @pl.when(first)
        def _():
            each("start")

        body(*ins, *outs, *scr)

        @pl.when(last)
        def _():
            each("finish")

    any_spec = pl.BlockSpec(memory_space=pl.ANY)
    call = pl.pallas_call(
        wrapped, name=name, grid=grid, in_specs=in_specs + [any_spec] * len(c_ins),
        out_specs=out_specs + [any_spec] * len(c_outs), out_shape=out_shape + c_outs,
        scratch_shapes=scratch_shapes + sems, compiler_params=_cp())
    return lambda *args: list(call(*args, *c_ins))


N_CHIP = 4


def rs_pair_routine(g, cols):
    if cols:
        R, C = g.shape[0], g.shape[1] // N_DEV
        assert C % LANES == 0
    else:
        _, R, C = g.shape

    def blk(ref, idx):
        if cols:
            return ref.at[:, pl.ds(pl.multiple_of(idx * C, LANES), C)]
        return ref.at[idx]

    def copies(ins, outs, send_sems, recv_sems, local_sems):
        (g_ref,), (got_ref,) = ins, outs
        x, y, c = _position()
        local, remote = [], []
        for q in range(N_CHIP):
            remote.append(pltpu.make_async_remote_copy(
                src_ref=blk(g_ref, 2 * q + 1 - c), dst_ref=got_ref.at[q],
                send_sem=send_sems.at[q], recv_sem=recv_sems.at[q],
                device_id=(x, y, 1 - c), device_id_type=pl.DeviceIdType.MESH))
        return local, remote

    def start(*refs):
        local, remote = copies(*refs)
        for cp in remote + local:
            cp.start()

    def finish(*refs):
        local, remote = copies(*refs)
        for cp in remote:
            cp.wait_recv()
        for cp in remote:
            cp.wait_send()
        for cp in local:
            cp.wait()

    o = jax.ShapeDtypeStruct((N_CHIP, R, C), g.dtype)
    return dict(ins=[g], outs=[o], n_sem=N_CHIP, n_local=1, start=start, finish=finish)


def rs_chip_routine(h):
    _, R, C = h.shape
    RELATIONS = ((0, 1), (1, 0), (1, 1))

    def copies(ins, outs, send_sems, recv_sems, local_sems):
        (h_ref,), (out_ref,) = ins, outs
        x, y, c = _position()
        local, remote = [], []
        for k, (fx, fy) in enumerate(RELATIONS):
            px = (1 - x) if fx else x
            py = (1 - y) if fy else y
            remote.append(pltpu.make_async_remote_copy(
                src_ref=h_ref.at[2 * px + py], dst_ref=out_ref.at[k],
                send_sem=send_sems.at[k], recv_sem=recv_sems.at[k],
                device_id=(px, py, c), device_id_type=pl.DeviceIdType.MESH))
        return local, remote

    def start(*refs):
        local, remote = copies(*refs)
        for cp in remote + local:
            cp.start()

    def finish(*refs):
        local, remote = copies(*refs)
        for cp in remote:
            cp.wait_recv()
        for cp in remote:
            cp.wait_send()
        for cp in local:
            cp.wait()

    return dict(ins=[h], outs=[jax.ShapeDtypeStruct((N_CHIP - 1, R, C), h.dtype)], n_sem=3, n_local=1,
                start=start, finish=finish)


def comm_call(routine, *, name):
    n_in, n_out = len(routine["ins"]), len(routine["outs"])

    def body(*refs):
        ins, outs, sems = refs[:n_in], refs[n_in:n_in + n_out], refs[n_in + n_out:]
        routine["start"](ins, outs, *sems)
        routine["finish"](ins, outs, *sems)

    any_spec = pl.BlockSpec(memory_space=pl.ANY)
    return pl.pallas_call(
        body, name=name, out_shape=routine["outs"],
        in_specs=[any_spec] * n_in, out_specs=[any_spec] * n_out,
        scratch_shapes=[pltpu.SemaphoreType.DMA((routine["n_sem"],)), pltpu.SemaphoreType.DMA((routine["n_sem"],)),
                        pltpu.SemaphoreType.DMA((routine["n_local"],))],
    )(*routine["ins"])


def add_own(g, got, core, *, cols, name):
    n, R, C = got.shape
    tr = _tile(R, 256, 8)
    if cols:
        gspec = pl.BlockSpec((tr, C), lambda q, i, c_ref: (i, 2 * q + c_ref[0]))
    else:
        gspec = pl.BlockSpec((1, tr, C), lambda q, i, c_ref: (2 * q + c_ref[0], i, 0))
    spec = pl.BlockSpec((1, tr, C), lambda q, i, c_ref: (q, i, 0))

    def body(c_ref, g_ref, b_ref, o_ref):
        gv = g_ref[...] if cols else g_ref[0]
        o_ref[0] = (gv.astype(F32) + b_ref[0].astype(F32)).astype(o_ref.dtype)

    return pl.pallas_call(
        body, name=name,
        grid_spec=pltpu.PrefetchScalarGridSpec(num_scalar_prefetch=1, grid=(n, R // tr),
                                               in_specs=[gspec, spec], out_specs=spec),
        out_shape=jax.ShapeDtypeStruct((n, R, C), g.dtype), compiler_params=_cp(),
    )(core, g, got)


def sum_chips(h, got, chip, *, name):
    _, R, C = h.shape
    tr = _tile(R, 256, 8)

    def body(q_ref, h_ref, g_ref, o_ref):
        acc = h_ref[0].astype(F32)
        for k in range(N_CHIP - 1):
            acc = acc + g_ref[k].astype(F32)
        o_ref[...] = acc

    return pl.pallas_call(
        body, name=name,
        grid_spec=pltpu.PrefetchScalarGridSpec(
            num_scalar_prefetch=1, grid=(R // tr,),
            in_specs=[pl.BlockSpec((1, tr, C), lambda i, q_ref: (q_ref[0], i, 0)),
                      pl.BlockSpec((N_CHIP - 1, tr, C), lambda i, q_ref: (0, i, 0))],
            out_specs=pl.BlockSpec((tr, C), lambda i, q_ref: (i, 0))),
        out_shape=jax.ShapeDtypeStruct((R, C), F32), compiler_params=_cp(),
    )(chip, h, got)


def reduce_scatter(g, *, cols, name):
    x, y, c = _position()
    core = jnp.reshape(c, (1,)).astype(jnp.int32)
    chip = jnp.reshape(2 * x + y, (1,)).astype(jnp.int32)
    got = comm_call(rs_pair_routine(g, cols), name=name + "_pair")[0]
    h = add_own(g, got, core, cols=cols, name=name + "_add")
    return sum_chips(h, comm_call(rs_chip_routine(h), name=name + "_chip")[0], chip, name=name + "_sum")


def sum_blocks(stack, *, name):
    n, R, C = stack.shape
    tr = _tile(R, 256, 8)

    def body(x_ref, o_ref):
        acc = x_ref[0].astype(F32)
        for i in range(1, n):
            acc = acc + x_ref[i].astype(F32)
        o_ref[...] = acc

    return pl.pallas_call(
        body, name=name, grid=(R // tr,),
        in_specs=[pl.BlockSpec((n, tr, C), lambda i: (0, i, 0))],
        out_specs=pl.BlockSpec((tr, C), lambda i: (i, 0)),
        out_shape=jax.ShapeDtypeStruct((R, C), F32), compiler_params=_cp(),
    )(stack)


PACK_COLS = 1024
PACK_ROW_MULT = 8


def _pack(arrays, dtype):
    flat = jnp.concatenate([a.reshape(-1).astype(dtype) for a in arrays])
    n = flat.shape[0]
    unit = PACK_COLS * PACK_ROW_MULT
    padded = -(-n // unit) * unit
    return jnp.pad(flat, (0, padded - n)).reshape(padded // PACK_COLS, PACK_COLS)


def _unpack(packed, shapes):
    flat = packed.reshape(-1)
    out, o = [], 0
    for s in shapes:
        n = int(np.prod(s))
        out.append(flat[o:o + n].reshape(s))
        o += n
    return out


def fsdp_cols(shard, *, name):
    K, n = shard.shape
    npad = -(-n // LANES) * LANES

    @jax.custom_vjp
    def f(p):
        p = jnp.pad(p, ((0, 0), (0, npad - n))) if npad != n else p
        return all_gather(p.astype(BF16), cols=True, name=name + "_ag")

    def fwd(p):
        return f(p), None

    def bwd(_, g):
        d = reduce_scatter(g, cols=True, name=name + "_rs")
        return (d[:, :n] if npad != n else d,)

    f.defvjp(fwd, bwd)
    return f(shard)


def fsdp_rows(shard, *, name):
    k, N = shard.shape

    @jax.custom_vjp
    def f(p):
        return all_gather(p.astype(BF16), name=name + "_ag").reshape(N_DEV * k, N)

    def fwd(p):
        return f(p), None

    def bwd(_, g):
        return (reduce_scatter(g.reshape(N_DEV, k, N), cols=False, name=name + "_rs"),)

    f.defvjp(fwd, bwd)
    return f(shard)


def _unpad_cols(w, n):
    K = w.shape[0]
    npad = w.shape[1] // N_DEV
    if npad == n:
        return w
    return w.reshape(K, N_DEV, npad)[:, :, :n].reshape(K, N_DEV * n)


def gather_rows(part, me, *, name):
    rows, n = part.shape
    per = rows // N_DEV

    @jax.custom_vjp
    def f(part):
        full = all_gather(part, name=name + "_fwd")
        mine = lax.dynamic_slice_in_dim(full, me * per, per, axis=1)
        return jnp.swapaxes(mine, 0, 1).reshape(per, N_DEV * n)

    def fwd(part):
        return f(part), None

    def bwd(_, g):
        full = all_gather(g, name=name + "_bwd")
        mine = lax.dynamic_slice_in_dim(full, me * n, n, axis=2)
        return (mine.reshape(rows, n),)

    f.defvjp(fwd, bwd)
    return f(part)


def _seg_layout():
    offs = np.concatenate([[0], np.cumsum(IN_SPLITS)])
    cols, widths, leads = [], [], []
    for s in SEG_ORDER:
        cols.append((int(offs[s]), int(offs[s + 1])))
        widths.append(SEG_PAD.get(s, IN_SPLITS[s]))
        leads.append(SEG_LEAD.get(s, 0))
    return cols, widths, leads


def _arrange_w_in(w, n):
    D = w.shape[0]
    npad = w.shape[1] // N_DEV
    cols, widths, leads = _seg_layout()

    def pieces_of(a, b):
        out = []
        for d in range(N_DEV):
            lo, hi = max(a, n * d), min(b, n * (d + 1))
            if lo < hi:
                out.append((d, lo, hi))
        return out

    @jax.custom_vjp
    def f(w):
        parts = []
        for (a, b), wd, ld in zip(cols, widths, leads):
            if ld:
                parts.append(jnp.zeros((D, ld), w.dtype))
            parts += [w[:, npad * d + lo - n * d:npad * d + hi - n * d] for d, lo, hi in pieces_of(a, b)]
            if wd != ld + b - a:
                parts.append(jnp.zeros((D, wd - ld - (b - a)), w.dtype))
        parts.append(jnp.zeros((D, IN_WIDTH - sum(widths)), w.dtype))
        return jnp.concatenate(parts, axis=1)

    def fwd(w):
        return f(w), None

    def bwd(_, g):
        offs = np.concatenate([[0], np.cumsum(widths)])
        runs = []
        for ((a, b), off, ld) in zip(cols, offs[:-1], leads):
            runs += [(lo, int(off) + ld + lo - a, hi - lo) for _, lo, hi in pieces_of(a, b)]
        runs.sort()
        parts, d_next = [], 1
        for lo, o, ln in runs:
            while lo >= n * d_next:
                parts.append(jnp.zeros((D, npad - n), g.dtype))
                d_next += 1
            parts.append(g[:, o:o + ln])
        parts.append(jnp.zeros((D, npad - n), g.dtype))
        return (jnp.concatenate(parts, axis=1),)

    f.defvjp(fwd, bwd)
    return f(w)


def split_cols(proj, widths):
    @jax.custom_vjp
    def f(p):
        outs, o = [], 0
        for wd in widths:
            outs.append(p[:, o:o + wd])
            o += wd
        return tuple(outs)

    def fwd(p):
        return f(p), None

    def bwd(_, gs):
        rest = proj.shape[1] - sum(widths)
        tail = [jnp.zeros((proj.shape[0], rest), proj.dtype)] if rest else []
        return (jnp.concatenate(list(gs) + tail, axis=1),)

    f.defvjp(fwd, bwd)
    return f(proj)


BIG = ("w_in", "w_uq", "w_ukv", "conv_w", "w_out", "w_gate_up", "w_down")
SMALL = ("b_ada", "norm1_g", "norm2_g", "q_norm_g", "k_norm_g", "mla_q_norm_g", "mla_kv_norm_g",
         "conv_b", "dt_bias", "a_log", "d_skip", "ssd_norm_g", "final_norm_g")
WEIGHTS = ("w_ada", "b_ada", "norm1_g", "norm2_g", "w_in", "q_norm_g", "k_norm_g", "mla_q_norm_g",
           "w_uq", "mla_kv_norm_g", "w_ukv", "conv_w", "conv_b", "dt_bias", "a_log", "d_skip",
           "ssd_norm_g", "w_out", "w_gate_up", "w_down", "final_norm_g")


PRE = ("w_in", "w_uq", "w_ukv", "conv_w")


def _layer(l, x, mod, W, shards, nxt, P, tabs):
    B, S, D = x.shape
    T = B * S
    nm = f"l{l}_"
    shift1, scale1, gate1, shift2, scale2, gate2 = [m[:, None, :] for m in jnp.split(mod, 6, axis=-1)]
    cos_a, sin_a, cos_b, sin_b, cos_k, sin_k = tabs

    h = rmsmod(x, P["norm1_g"][l][None], scale1, shift1, name=nm + "norm1")
    w_in = _arrange_w_in(W["w_in"], IN_COLS // N_DEV)
    proj, (w_out_full,) = linear(h.reshape(T, D), w_in, out_dtype=F32, name=nm + "in",
                                 carry=((shards["w_out"], False, "w_out"),))
    q_a, k_a, v_a, cq, ckv, z, xbc, kpe, dtr = split_cols(proj, _seg_layout()[1])

    qn = group_rmsnorm(q_a, jnp.tile(P["q_norm_g"][l], GQA_H)[None], gs=HEAD, out_dtype=F32, name=nm + "qnorm")
    kn = group_rmsnorm(k_a, jnp.tile(P["k_norm_g"][l], GQA_KV)[None], gs=HEAD, out_dtype=F32, name=nm + "knorm")
    qr = rope(qn.reshape(B, S, -1), cos_a[:, :GQA_H * HEAD], sin_a[:, :GQA_H * HEAD], d=HEAD // 4, name=nm + "qrope")
    kr = rope(kn.reshape(B, S, -1), cos_a[:, :GQA_KV * HEAD], sin_a[:, :GQA_KV * HEAD], d=HEAD // 4, name=nm + "krope")
    o_a, (w_gu_full,) = attention(qr, kr, v_a.reshape(B, S, -1), H=GQA_H, Hkv=GQA_KV, dk=HEAD, dv=HEAD,
                                  scale=HEAD ** -0.5, name=nm + "gqa",
                                  carry=((shards["w_gate_up"], True, "w_gate_up"),))

    slot_pad = MLA_DK - MLA_NOPE - MLA_ROPE
    w_uq = jnp.pad(W["w_uq"].reshape(MLA_QL, MLA_H, MLA_NOPE + MLA_ROPE), ((0, 0), (0, 0), (0, slot_pad)))
    w_uq = w_uq.reshape(MLA_QL, MLA_H * MLA_DK)
    w_ukv = W["w_ukv"].reshape(MLA_KVL, MLA_H, MLA_NOPE + MLA_V)
    w_ukv = jnp.concatenate(
        [jnp.pad(w_ukv[:, :, :MLA_NOPE], ((0, 0), (0, 0), (0, MLA_DK - MLA_NOPE))).reshape(MLA_KVL, -1),
         w_ukv[:, :, MLA_NOPE:].reshape(MLA_KVL, -1)], axis=1)
    cqn = group_rmsnorm(cq, P["mla_q_norm_g"][l][None], gs=MLA_QL, out_dtype=BF16, name=nm + "cqnorm")
    ckvn = group_rmsnorm(ckv, P["mla_kv_norm_g"][l][None], gs=MLA_KVL, out_dtype=BF16, name=nm + "ckvnorm")
    qb = linear(cqn, w_uq, out_dtype=F32, name=nm + "uq")
    kvb = linear(ckvn, w_ukv, out_dtype=F32, name=nm + "ukv")
    k_slots, v_b = split_cols(kvb, (MLA_H * MLA_DK, MLA_H * MLA_V))
    q_cat = rope(qb.reshape(B, S, -1), cos_b, sin_b, d=MLA_ROPE // 4, name=nm + "qpe_rope")
    k_pe = rope(kpe.reshape(B, S, -1), cos_k, sin_k, d=MLA_ROPE // 4, name=nm + "kpe_rope")
    k_cat = k_slots.reshape(B, S, -1) + jnp.tile(k_pe, (1, 1, MLA_H))
    o_b = attention(q_cat, k_cat, v_b.reshape(B, S, -1), H=MLA_H, Hkv=MLA_H, dk=MLA_DK, dv=MLA_V,
                    scale=(MLA_NOPE + MLA_ROPE) ** -0.5, name=nm + "mla")

    xact = conv_silu(xbc.reshape(B, S, -1), W["conv_w"].astype(F32), P["conv_b"][l][None], name=nm + "conv")
    brow = jnp.pad(P["dt_bias"][l].reshape(1, -1), ((0, 0), (0, LANES - 2 * SSD_H)))
    arow = jnp.pad(-jnp.exp(P["a_log"][l].reshape(1, -1)), ((0, 0), (0, LANES - 2 * SSD_H)))
    raw = dtr.reshape(B, S, LANES)
    y_f = ssd_scan(xact, raw, brow, arow, rev=False, name=nm + "ssd_f")
    y_b = ssd_scan(xact, raw, brow, arow, rev=True, name=nm + "ssd_b")
    dsk = jnp.repeat(P["d_skip"][l], SSD_P)[None]
    o_c = ssd_out(y_f.reshape(T, -1), y_b.reshape(T, -1), xact.reshape(T, -1), z, dsk,
                  P["ssd_norm_g"][l][None], name=nm + "ssd_out")

    o = jnp.concatenate([o_a.reshape(T, -1).astype(BF16), o_b.reshape(T, -1).astype(BF16), o_c], axis=-1)
    mix = linear(o, w_out_full, out_dtype=F32, name=nm + "out")
    x = gated_residual(x, gate1, mix.reshape(B, S, D), name=nm + "res1")

    h = rmsmod(x, P["norm2_g"][l][None], scale2, shift2, name=nm + "norm2")
    y, nxt_full = ffn(h.reshape(T, D), w_gu_full, shards["w_down"], name=nm + "ffn",
                      carry=tuple((nxt[n], True, "next_" + n) for n in PRE) if nxt else ())
    return gated_residual(x, gate2, y.reshape(B, S, D), name=nm + "res2"), nxt_full


def kernel(x, c, w_ada, b_ada, norm1_g, norm2_g, w_in, q_norm_g, k_norm_g, mla_q_norm_g, w_uq, mla_kv_norm_g, w_ukv, conv_w, conv_b, dt_bias, a_log, d_skip, ssd_norm_g, w_out, w_gate_up, w_down, final_norm_g, loss_target, m_w_ada, m_b_ada, m_norm1_g, m_norm2_g, m_w_in, m_q_norm_g, m_k_norm_g, m_mla_q_norm_g, m_w_uq, m_mla_kv_norm_g, m_w_ukv, m_conv_w, m_conv_b, m_dt_bias, m_a_log, m_d_skip, m_ssd_norm_g, m_w_out, m_w_gate_up, m_w_down, m_final_norm_g, v_w_ada, v_b_ada, v_norm1_g, v_norm2_g, v_w_in, v_q_norm_g, v_k_norm_g, v_mla_q_norm_g, v_w_uq, v_mla_kv_norm_g, v_w_ukv, v_conv_w, v_conv_b, v_dt_bias, v_a_log, v_d_skip, v_ssd_norm_g, v_w_out, v_w_gate_up, v_w_down, v_final_norm_g):
    args = dict(locals())
    weights = {n: args[n] for n in WEIGHTS}
    moments_m = {n: args["m_" + n] for n in WEIGHTS}
    moments_v = {n: args["v_" + n] for n in WEIGHTS}
    B, S, D = x.shape
    L = w_ada.shape[0]
    T = B * S
    px, py, pc = _position()
    me = 4 * px + 2 * py + pc
    small_shapes = [weights[n].shape for n in SMALL]

    tabs = (*rope_tables(S, HEAD, GQA_H * HEAD),
            *rope_tables(S, MLA_ROPE, MLA_H * MLA_DK, slot=MLA_DK, lead=MLA_NOPE),
            *rope_tables(S, MLA_ROPE, MLA_DK, slot=MLA_DK, lead=MLA_NOPE))
    c_all = all_gather(c, name="gather_c").reshape(N_DEV * B, D)

    def local_loss(big, w_ada_s, small, x):
        P = dict(zip(SMALL, small))
        pre = [fsdp_cols(big[n][0], name=f"l0_{n}") for n in PRE]
        for l in range(L):
            W = {n: full if n == "w_in" else _unpad_cols(full, big[n].shape[2]) for n, full in zip(PRE, pre)}
            shards = {n: big[n][l] for n in ("w_out", "w_gate_up", "w_down")}
            nxt = {n: big[n][l + 1] for n in PRE} if l + 1 < L else None
            part = linear(c_all, w_ada_s[l], out_dtype=F32, a_silu=True, name=f"l{l}_ada")
            mod = gather_rows(part, me, name=f"l{l}_mod") + P["b_ada"][l][None]
            x, pre = _layer(l, x, mod, W, shards, nxt, P, tabs)
        return final_loss(x.reshape(T, D), P["final_norm_g"][None], loss_target.reshape(T, D), name="loss")

    big = {n: weights[n] for n in BIG}
    small = tuple(weights[n] for n in SMALL)
    loss, (g_big, g_ada, g_small, grad_x) = jax.value_and_grad(local_loss, argnums=(0, 1, 2, 3))(
        big, w_ada, small, x)
    loss = lax.psum(loss, ("x", "y", "c"))

    grads = dict(g_big)
    grads["w_ada"] = g_ada
    g_small_sum = sum_blocks(all_gather(_pack(g_small, F32), name="small_grads_ag"), name="small_grads_sum")
    grads.update(zip(SMALL, _unpack(g_small_sum, small_shapes)))

    delta, new_m, new_v = {}, {}, {}
    for n in ("w_ada",) + BIG:
        delta[n], new_m[n], new_v[n] = adamw(weights[n], grads[n], moments_m[n], moments_v[n], name="adamw_" + n)
    d_, m_, v_ = adamw(_pack([weights[n] for n in SMALL], F32)[None], g_small_sum[None],
                       _pack([moments_m[n] for n in SMALL], F32)[None], _pack([moments_v[n] for n in SMALL], F32)[None],
                       name="adamw_small")
    for tgt, packed in ((delta, d_), (new_m, m_), (new_v, v_)):
        tgt.update(zip(SMALL, _unpack(packed[0], small_shapes)))

    return (loss, grad_x, *[grads[n] for n in WEIGHTS], *[delta[n] for n in WEIGHTS],
            *[new_m[n] for n in WEIGHTS], *[new_v[n] for n in WEIGHTS])
```

```python
import functools
import math

import jax
import jax.numpy as jnp
import numpy as np
from jax import lax
from jax.experimental import pallas as pl
from jax.experimental.pallas import tpu as pltpu

F32 = jnp.float32
BF16 = jnp.bfloat16
N_DEV = 8
EPS = 1e-6
ROPE_THETA = 10000.0
GRID_W = 64

GQA_H, GQA_KV, HEAD = 6, 2, 128
MLA_H, MLA_QL, MLA_KVL, MLA_NOPE, MLA_ROPE, MLA_V = 4, 512, 256, 128, 64, 128
MLA_DK = 256
SSD_H, SSD_P, SSD_G, SSD_N, SSD_K, CHUNK = 12, 64, 2, 128, 5, 128
SSD_INNER = SSD_H * SSD_P
SSD_CONV_DIM = SSD_INNER + 2 * SSD_G * SSD_N
N_PAIR = SSD_H // 2
LANES = 128
IN_SPLITS = (768, 256, 256, 512, 256, 64, 768, 1280, 24)
IN_COLS = sum(IN_SPLITS)
SEG_ORDER = (0, 1, 2, 3, 4, 6, 7, 5, 8)
SEG_PAD = {5: 256, 8: 128}
SEG_LEAD = {5: 128}
IN_WIDTH = 4608

ADAM_LR, ADAM_B1, ADAM_B2, ADAM_EPS, ADAM_WD, ADAM_STEP = 0.001, 0.9, 0.999, 1e-08, 0.01, 10
VMEM_LIMIT = 56 * 1024 * 1024
MM_TM, MM_TN, MM_TK = 1024, 1408, 2048


def _cp(**kw):
    return pltpu.CompilerParams(vmem_limit_bytes=VMEM_LIMIT, **kw)


def _tile(dim, cap, mult=128):
    if dim <= cap:
        return dim
    best = None
    t = mult
    while t <= cap:
        if dim % t == 0:
            best = t
        t += mult
    assert best is not None, (dim, cap)
    return best


def _sigmoid(x):
    return 1.0 / (1.0 + jnp.exp(-x))


def _dot(a, b, dims):
    return lax.dot_general(a, b, (dims, ((), ())), preferred_element_type=F32)


NN = ((1,), (0,))
NT = ((1,), (1,))
TN = ((0,), (0,))


def _dotf(a, b, dims=NN):
    return lax.dot_general(a, b, (dims, ((), ())), preferred_element_type=F32,
                           precision=lax.Precision.HIGHEST)


def _bf(x):
    return x.astype(BF16)


def mm(a, b, *, ta=False, tb=False, out_dtype=F32, a_silu=False, name, carry=(),
       a_halves=False, b_halves=False):
    if a_halves:
        assert not ta
        M, K = a.shape[1], 2 * a.shape[2]
    elif ta:
        K, M = a.shape
    else:
        M, K = a.shape
    if b_halves:
        assert not tb
        K2, N = b.shape[1], 2 * b.shape[2]
    elif tb:
        N, K2 = b.shape
    else:
        K2, N = b.shape
    assert K == K2, (a.shape, b.shape, ta, tb)
    tm = _tile(M, MM_TM)
    tn = _tile(N // 2, MM_TN) if b_halves else _tile(N, MM_TN)
    tk = _tile(K // 2, MM_TK) if a_halves else _tile(K, MM_TK)
    nk = K // tk
    dims = ((0 if ta else 1,), (1 if tb else 0,))

    def partial_product(a_ref, b_ref):
        av = a_ref[...]
        if a_silu:
            av = av.astype(F32)
            av = av * _sigmoid(av)
        return _dot(_bf(av), _bf(b_ref[...]), dims)

    def body_single(a_ref, b_ref, o_ref):
        o_ref[...] = partial_product(a_ref, b_ref).astype(o_ref.dtype)

    def body_acc(a_ref, b_ref, o_ref, acc_ref):
        k = pl.program_id(2)

        @pl.when(k == 0)
        def _():
            acc_ref[...] = partial_product(a_ref, b_ref)

        @pl.when(k > 0)
        def _():
            acc_ref[...] += partial_product(a_ref, b_ref)

        @pl.when(k == nk - 1)
        def _():
            o_ref[...] = acc_ref[...].astype(o_ref.dtype)

    body = body_single if nk == 1 else body_acc

    a_spec = (pl.BlockSpec((tk, tm), lambda i, j, k: (k, i)) if ta
              else pl.BlockSpec((tm, tk), lambda i, j, k: (i, k)))
    b_spec = (pl.BlockSpec((tn, tk), lambda i, j, k: (j, k)) if tb
              else pl.BlockSpec((tk, tn), lambda i, j, k: (k, j)))
    if a_halves:
        a_spec = pl.BlockSpec((None, tm, tk), lambda i, j, k: (k // (nk // 2), i, k % (nk // 2)))
    if b_halves:
        nnh = N // tn // 2
        b_spec = pl.BlockSpec((None, tk, tn), lambda i, j, k: (j // nnh, k, j % nnh))
    res = carry_call(
        body, name=name, grid=(M // tm, N // tn, nk),
        in_specs=[a_spec, b_spec],
        out_specs=[pl.BlockSpec((tm, tn), lambda i, j, k: (i, j))],
        out_shape=[jax.ShapeDtypeStruct((M, N), out_dtype)],
        scratch_shapes=[] if nk == 1 else [pltpu.VMEM((tm, tn), F32)],
        carry=carry, dims=("parallel", "parallel", "arbitrary"),
    )(a, b)
    return res if carry else res[0]


def _shard_prep(p, cols):
    if cols and p.shape[1] % LANES:
        p = jnp.pad(p, ((0, 0), (0, -p.shape[1] % LANES)))
    return p.astype(BF16)


def _full_post(full, cols):
    return full if cols else full.reshape(full.shape[0] * full.shape[1], full.shape[2])


def _rs_begin(g_full, cols, shard_shape, name):
    gg = g_full if cols else g_full.reshape(N_DEV, shard_shape[0], shard_shape[1])
    x, y, c = _position()
    got = comm_call(rs_pair_routine(gg, cols), name=name + "_pair")[0]
    return add_own(gg, got, jnp.reshape(c, (1,)).astype(jnp.int32), cols=cols, name=name + "_add")


def _rs_end(h, got, cols, shard_shape, name):
    x, y, c = _position()
    d = sum_chips(h, got, jnp.reshape(2 * x + y, (1,)).astype(jnp.int32), name=name + "_sum")
    return d[:, :shard_shape[1]] if cols else d


def linear(a, w, *, out_dtype, name, a_silu=False, carry=()):
    kinds = [(cols, s.shape, tag) for s, cols, tag in carry]

    def run(a, w, *shards):
        routines = [ag_routine(_shard_prep(s, cols), cols) for s, (cols, _, _) in zip(shards, kinds)]
        res = mm(a, w, out_dtype=out_dtype, a_silu=a_silu, name=name + "_fwd", carry=routines)
        if not routines:
            return (res,)
        return (res[0], *[_full_post(fu, cols) for fu, (cols, _, _) in zip(res[1:], kinds)])

    @jax.custom_vjp
    def f(a, w, *shards):
        return run(a, w, *shards)

    def fwd(a, w, *shards):
        return run(a, w, *shards), (a, w)

    def bwd(res, cts):
        a, w = res
        g = cts[0]
        hs = [_rs_begin(gf, cols, shp, name + "_" + tag) for gf, (cols, shp, tag) in zip(cts[1:], kinds)]
        routines = [rs_chip_routine(h) for h in hs]
        if a_silu:
            assert not routines
            da = jnp.zeros_like(a)
            gots = []
        else:
            r = mm(g, w, tb=True, out_dtype=a.dtype, name=name + "_da", carry=routines)
            da, gots = (r[0], r[1:]) if routines else (r, [])
        dw = mm(a, g, ta=True, out_dtype=w.dtype, a_silu=a_silu, name=name + "_dw")
        dsh = [_rs_end(h, got, cols, shp, name + "_" + tag)
               for h, got, (cols, shp, tag) in zip(hs, gots, kinds)]
        return (da, dw, *dsh)

    f.defvjp(fwd, bwd)
    out = f(a, w, *[s for s, _, _ in carry])
    return (out[0], list(out[1:])) if carry else out[0]


def swiglu_up(h, w_gu, *, name, carry=()):
    T, D = h.shape
    F = w_gu.shape[1] // 2
    assert D <= MM_TK
    tm, tn = _tile(T, MM_TM), _tile(F, 512)
    nf = F // tn

    def body(a_ref, bg_ref, bu_ref, gu_ref, act_ref):
        av = a_ref[...]
        g = _dot(av, bg_ref[...], NN)
        u = _dot(av, bu_ref[...], NN)
        gu_ref[0] = g.astype(gu_ref.dtype)
        gu_ref[1] = u.astype(gu_ref.dtype)
        act_ref[...] = (g * _sigmoid(g) * u).astype(act_ref.dtype)

    return carry_call(
        body, name=name, grid=(T // tm, nf),
        in_specs=[pl.BlockSpec((tm, D), lambda i, j: (i, 0)), pl.BlockSpec((D, tn), lambda i, j: (0, j)),
                  pl.BlockSpec((D, tn), lambda i, j: (0, j + nf))],
        out_specs=[pl.BlockSpec((2, tm, tn), lambda i, j: (0, i, j)), pl.BlockSpec((tm, tn), lambda i, j: (i, j))],
        out_shape=[jax.ShapeDtypeStruct((2, T, F), BF16), jax.ShapeDtypeStruct((T, F), BF16)],
        carry=carry,
    )(h, w_gu, w_gu)


def swiglu_down_bwd(g, w_dn, gu, *, name, carry=()):
    T, D = g.shape
    F = w_dn.shape[0]
    assert D <= MM_TK
    tm, tn = _tile(T, MM_TM), _tile(F, 512)

    def body(a_ref, b_ref, gu_ref, o_ref):
        dact = _dot(_bf(a_ref[...]), b_ref[...], NT)
        gv = gu_ref[0].astype(F32)
        uv = gu_ref[1].astype(F32)
        sg = _sigmoid(gv)
        o_ref[0] = (dact * uv * (sg * (1.0 + gv * (1.0 - sg)))).astype(o_ref.dtype)
        o_ref[1] = (dact * gv * sg).astype(o_ref.dtype)

    half = pl.BlockSpec((2, tm, tn), lambda i, j: (0, i, j))
    return carry_call(
        body, name=name, grid=(T // tm, F // tn),
        in_specs=[pl.BlockSpec((tm, D), lambda i, j: (i, 0)), pl.BlockSpec((tn, D), lambda i, j: (j, 0)), half],
        out_specs=[half], out_shape=[jax.ShapeDtypeStruct((2, T, F), BF16)], carry=carry,
    )(g, w_dn, gu)


def ffn(h, w_gu, w_dn_shard, *, name, carry=()):
    kinds = [(cols, s.shape, tag) for s, cols, tag in carry]
    dn_shape = w_dn_shard.shape

    def run(h, w_gu, w_dn_shard, *shards):
        gu, act, w_dn = swiglu_up(h, w_gu, name=name + "_up", carry=[ag_routine(_shard_prep(w_dn_shard, False))])
        w_dn = _full_post(w_dn, False)
        routines = [ag_routine(_shard_prep(s, cols), cols) for s, (cols, _, _) in zip(shards, kinds)]
        res = mm(act, w_dn, out_dtype=F32, name=name + "_down", carry=routines)
        y, fulls = (res[0], res[1:]) if routines else (res, [])
        return (y, *[_full_post(fu, cols) for fu, (cols, _, _) in zip(fulls, kinds)]), (h, w_gu, w_dn, gu, act)

    @jax.custom_vjp
    def f(h, w_gu, w_dn_shard, *shards):
        return run(h, w_gu, w_dn_shard, *shards)[0]

    def fwd(h, w_gu, w_dn_shard, *shards):
        return run(h, w_gu, w_dn_shard, *shards)

    def bwd(res, cts):
        h, w_gu, w_dn, gu, act = res
        g = cts[0]
        hs = [_rs_begin(gf, cols, shp, name + "_" + tag) for gf, (cols, shp, tag) in zip(cts[1:], kinds)]
        dgu, *gots = swiglu_down_bwd(g, w_dn, gu, name=name + "_down_da", carry=[rs_chip_routine(x) for x in hs])
        dw_dn = mm(act, g, ta=True, out_dtype=w_dn.dtype, name=name + "_down_dw")
        h_dn = _rs_begin(dw_dn, False, dn_shape, name + "_w_down")
        dh, got_dn = mm(dgu, w_gu, tb=True, a_halves=True, out_dtype=h.dtype, name=name + "_up_da",
                        carry=[rs_chip_routine(h_dn)])
        dw_gu = mm(h, dgu, ta=True, b_halves=True, out_dtype=w_gu.dtype, name=name + "_up_dw")
        d_dn = _rs_end(h_dn, got_dn, False, dn_shape, name + "_w_down")
        dsh = [_rs_end(x, got, cols, shp, name + "_" + tag) for x, got, (cols, shp, tag) in zip(hs, gots, kinds)]
        return (dh, dw_gu, d_dn, *dsh)

    f.defvjp(fwd, bwd)
    out = f(h, w_gu, w_dn_shard, *[s for s, _, _ in carry])
    return out[0], list(out[1:])


def rmsmod(x, g, scale, shift, *, name):
    B, S, D = x.shape
    ts = _tile(S, 256, 8)
    row = pl.BlockSpec((1, ts, D), lambda b, j: (b, j, 0))
    per_b = pl.BlockSpec((1, 1, D), lambda b, j: (b, 0, 0))
    gspec = pl.BlockSpec((1, D), lambda b, j: (0, 0))

    def fwd_call(x, g, scale, shift):
        def body(x_ref, g_ref, sc_ref, sh_ref, o_ref):
            xv = x_ref[0]
            r = lax.rsqrt(jnp.mean(xv * xv, axis=-1, keepdims=True) + EPS)
            y = xv * r * g_ref[...]
            o_ref[0] = (y * (1.0 + sc_ref[0]) + sh_ref[0]).astype(o_ref.dtype)

        return pl.pallas_call(
            body, name=name + "_fwd", grid=(B, S // ts),
            in_specs=[row, gspec, per_b, per_b], out_specs=row,
            out_shape=jax.ShapeDtypeStruct((B, S, D), BF16), compiler_params=_cp(),
        )(x, g, scale, shift)

    def bwd_call(x, g, scale, dh):
        def body(x_ref, g_ref, sc_ref, dh_ref, dx_ref, dg_ref, dsc_ref, dsh_ref):
            j = pl.program_id(1)
            xv = x_ref[0]
            dh = dh_ref[0].astype(F32)
            r = lax.rsqrt(jnp.mean(xv * xv, axis=-1, keepdims=True) + EPS)
            xn = xv * r
            gv = g_ref[...]
            dy = dh * (1.0 + sc_ref[0])
            dxn = dy * gv
            dx_ref[0] = r * (dxn - xn * jnp.mean(dxn * xn, axis=-1, keepdims=True))

            @pl.when(j == 0)
            def _():
                dg_ref[...] = jnp.zeros_like(dg_ref)
                dsc_ref[...] = jnp.zeros_like(dsc_ref)
                dsh_ref[...] = jnp.zeros_like(dsh_ref)

            dg_ref[0] += jnp.sum(dy * xn, axis=0, keepdims=True)
            dsc_ref[0] += jnp.sum(dh * xn * gv, axis=0, keepdims=True)
            dsh_ref[0] += jnp.sum(dh, axis=0, keepdims=True)

        vec = jax.ShapeDtypeStruct((B, 1, D), F32)
        return pl.pallas_call(
            body, name=name + "_bwd", grid=(B, S // ts),
            in_specs=[row, gspec, per_b, row], out_specs=[row, per_b, per_b, per_b],
            out_shape=[jax.ShapeDtypeStruct((B, S, D), F32), vec, vec, vec], compiler_params=_cp(),
        )(x, g, scale, dh)

    @jax.custom_vjp
    def f(x, g, scale, shift):
        return fwd_call(x, g, scale, shift)

    def fwd(x, g, scale, shift):
        return fwd_call(x, g, scale, shift), (x, g, scale)

    def bwd(res, dh):
        x, g, scale = res
        dx, dg, dsc, dsh = bwd_call(x, g, scale, dh)
        return dx, jnp.sum(dg, axis=0), dsc, dsh

    f.defvjp(fwd, bwd)
    return f(x, g, scale, shift)


def group_rmsnorm(x, g, *, gs, out_dtype, name):
    T, W = x.shape
    ng = W // gs
    tr = _tile(T, 512, 8)
    row = pl.BlockSpec((tr, W), lambda i: (i, 0))
    gspec = pl.BlockSpec((1, W), lambda i: (0, 0))

    def fwd_call(x, g):
        def body(x_ref, g_ref, o_ref):
            for i in range(ng):
                sl = slice(i * gs, (i + 1) * gs)
                xv = x_ref[:, sl]
                r = lax.rsqrt(jnp.mean(xv * xv, axis=-1, keepdims=True) + EPS)
                o_ref[:, sl] = (xv * r * g_ref[:, sl]).astype(o_ref.dtype)

        return pl.pallas_call(
            body, name=name + "_fwd", grid=(T // tr,), in_specs=[row, gspec], out_specs=row,
            out_shape=jax.ShapeDtypeStruct((T, W), out_dtype), compiler_params=_cp(),
        )(x, g)

    def bwd_call(x, g, dy):
        def body(x_ref, g_ref, dy_ref, dx_ref, dg_ref):
            @pl.when(pl.program_id(0) == 0)
            def _():
                dg_ref[...] = jnp.zeros_like(dg_ref)

            for i in range(ng):
                sl = slice(i * gs, (i + 1) * gs)
                xv = x_ref[:, sl]
                dyv = dy_ref[:, sl].astype(F32)
                r = lax.rsqrt(jnp.mean(xv * xv, axis=-1, keepdims=True) + EPS)
                xn = xv * r
                dxn = dyv * g_ref[:, sl]
                dx_ref[:, sl] = r * (dxn - xn * jnp.mean(dxn * xn, axis=-1, keepdims=True))
                dg_ref[:, sl] += jnp.sum(dyv * xn, axis=0, keepdims=True)

        return pl.pallas_call(
            body, name=name + "_bwd", grid=(T // tr,), in_specs=[row, gspec, row],
            out_specs=[row, gspec],
            out_shape=[jax.ShapeDtypeStruct((T, W), F32), jax.ShapeDtypeStruct((1, W), F32)],
            compiler_params=_cp(),
        )(x, g, dy)

    @jax.custom_vjp
    def f(x, g):
        return fwd_call(x, g)

    def fwd(x, g):
        return fwd_call(x, g), (x, g)

    def bwd(res, dy):
        return bwd_call(res[0], res[1], dy)

    f.defvjp(fwd, bwd)
    return f(x, g)


def rope_tables(seq_len, rot_dim, width, slot=0, lead=0):
    rows = seq_len // GRID_W
    row_idx = jnp.repeat(jnp.arange(rows), GRID_W).astype(F32)
    col_idx = jnp.tile(jnp.arange(GRID_W), rows).astype(F32)
    axis_dim = rot_dim // 2
    inv_freq = jnp.power(ROPE_THETA, -jnp.arange(0, axis_dim, 2, dtype=F32) / axis_dim)
    ang_r = row_idx[:, None] * inv_freq[None, :]
    ang_c = col_idx[:, None] * inv_freq[None, :]
    cos = jnp.concatenate([jnp.cos(ang_r), jnp.cos(ang_r), jnp.cos(ang_c), jnp.cos(ang_c)], axis=-1)
    sin = jnp.concatenate([-jnp.sin(ang_r), jnp.sin(ang_r), -jnp.sin(ang_c), jnp.sin(ang_c)], axis=-1)
    if slot:
        ones = jnp.ones((seq_len, 1), F32)
        cos = jnp.concatenate([ones * jnp.ones((1, lead)), cos, ones * jnp.ones((1, slot - lead - rot_dim))], axis=-1)
        sin = jnp.concatenate([ones * jnp.zeros((1, lead)), sin, ones * jnp.zeros((1, slot - lead - rot_dim))], axis=-1)
        rot_dim = slot
    reps = width // rot_dim
    return jnp.tile(cos, (1, reps)), jnp.tile(sin, (1, reps))


def rope(x, cos, sin, *, d, name):
    B, S, W = x.shape
    ts = _tile(S, 512, 8)
    row = pl.BlockSpec((1, ts, W), lambda b, j: (b, j, 0))
    tab = pl.BlockSpec((ts, W), lambda b, j: (j, 0))

    def call(x, inverse, nm):
        def body(x_ref, c_ref, s_ref, o_ref):
            xv = x_ref[0]
            lane = lax.broadcasted_iota(jnp.int32, xv.shape, 1)
            first = (lane // d) % 2 == 0

            def swap(v):
                return jnp.where(first, pltpu.roll(v, W - d, 1), pltpu.roll(v, d, 1))

            if inverse:
                o_ref[0] = xv * c_ref[...] + swap(xv * s_ref[...])
            else:
                o_ref[0] = xv * c_ref[...] + swap(xv) * s_ref[...]

        return pl.pallas_call(
            body, name=nm, grid=(B, S // ts), in_specs=[row, tab, tab], out_specs=row,
            out_shape=jax.ShapeDtypeStruct((B, S, W), F32), compiler_params=_cp(),
        )(x, cos, sin)

    @jax.custom_vjp
    def f(x):
        return call(x, False, name + "_fwd")

    def fwd(x):
        return call(x, False, name + "_fwd"), None

    def bwd(_, g):
        return (call(g, True, name + "_bwd"),)

    f.defvjp(fwd, bwd)
    return f(x)


def attention(q, k, v, *, H, Hkv, dk, dv, scale, name, carry=()):
    B, S, _ = q.shape
    rep = H // Hkv
    tq = _tile(S, 256, 8)
    kinds = [(cols, s.shape, tag) for s, cols, tag in carry]

    def fwd_call(q, k, v, routines=()):
        def body(q_ref, k_ref, v_ref, o_ref, lse_ref):
            s = _dot(_bf(q_ref[0]), _bf(k_ref[0]), NT) * scale
            m = jnp.max(s, axis=-1, keepdims=True)
            p = jnp.exp(s - m)
            l = jnp.sum(p, axis=-1, keepdims=True)
            o_ref[0] = _dot(_bf(p), _bf(v_ref[0]), NN) / l
            lse_ref[0, 0] = m + jnp.log(l)

        return carry_call(
            body, name=name + "_fwd", grid=(B, H, S // tq),
            in_specs=[pl.BlockSpec((1, tq, dk), lambda b, h, i: (b, i, h)),
                      pl.BlockSpec((1, S, dk), lambda b, h, i: (b, 0, h // rep)),
                      pl.BlockSpec((1, S, dv), lambda b, h, i: (b, 0, h // rep))],
            out_specs=[pl.BlockSpec((1, tq, dv), lambda b, h, i: (b, i, h)),
                       pl.BlockSpec((1, 1, tq, 1), lambda b, h, i: (b, h, i, 0))],
            out_shape=[jax.ShapeDtypeStruct((B, S, H * dv), F32),
                       jax.ShapeDtypeStruct((B, H, S, 1), F32)],
            carry=routines,
        )(q, k, v)

    def bwd_call(q, k, v, o, do, lse, routines=()):
        def body(q_ref, k_ref, v_ref, o_ref, do_ref, lse_ref, dq_ref, dk_ref, dv_ref):
            @pl.when((pl.program_id(2) == 0) & (pl.program_id(3) == 0))
            def _():
                dk_ref[...] = jnp.zeros_like(dk_ref)
                dv_ref[...] = jnp.zeros_like(dv_ref)

            qb = _bf(q_ref[0])
            kb = _bf(k_ref[0])
            dov = do_ref[0]
            dob = _bf(dov)
            s = _dot(qb, kb, NT) * scale
            p = jnp.exp(s - lse_ref[0, 0])
            delta = jnp.sum(dov * o_ref[0], axis=-1, keepdims=True)
            dp = _dot(dob, _bf(v_ref[0]), NT)
            dsb = _bf(p * (dp - delta))
            dq_ref[0] = _dot(dsb, kb, NN) * scale
            dk_ref[0] += _dot(dsb, qb, TN) * scale
            dv_ref[0] += _dot(_bf(p), dob, TN)

        qs = pl.BlockSpec((1, tq, dk), lambda b, g, r, i: (b, i, g * rep + r))
        os_ = pl.BlockSpec((1, tq, dv), lambda b, g, r, i: (b, i, g * rep + r))
        ks = pl.BlockSpec((1, S, dk), lambda b, g, r, i: (b, 0, g))
        vs = pl.BlockSpec((1, S, dv), lambda b, g, r, i: (b, 0, g))
        col = pl.BlockSpec((1, 1, tq, 1), lambda b, g, r, i: (b, g * rep + r, i, 0))
        return carry_call(
            body, name=name + "_bwd", grid=(B, Hkv, rep, S // tq),
            in_specs=[qs, ks, vs, os_, os_, col], out_specs=[qs, ks, vs],
            out_shape=[jax.ShapeDtypeStruct(q.shape, F32), jax.ShapeDtypeStruct(k.shape, F32),
                       jax.ShapeDtypeStruct(v.shape, F32)],
            carry=routines,
        )(q, k, v, o, do, lse)

    def run(q, k, v, *shards):
        routines = [ag_routine(_shard_prep(s, cols), cols) for s, (cols, _, _) in zip(shards, kinds)]
        o, lse, *fulls = fwd_call(q, k, v, routines)
        return (o, *[_full_post(fu, cols) for fu, (cols, _, _) in zip(fulls, kinds)]), lse

    @jax.custom_vjp
    def f(q, k, v, *shards):
        return run(q, k, v, *shards)[0]

    def fwd(q, k, v, *shards):
        outs, lse = run(q, k, v, *shards)
        return outs, (q, k, v, outs[0], lse)

    def bwd(res, cts):
        q, k, v, o, lse = res
        hs = [_rs_begin(gf, cols, shp, name + "_" + tag) for gf, (cols, shp, tag) in zip(cts[1:], kinds)]
        dq, dk_, dv_, *gots = bwd_call(q, k, v, o, cts[0], lse, [rs_chip_routine(h) for h in hs])
        dsh = [_rs_end(h, got, cols, shp, name + "_" + tag)
               for h, got, (cols, shp, tag) in zip(hs, gots, kinds)]
        return (dq, dk_, dv_, *dsh)

    f.defvjp(fwd, bwd)
    out = f(q, k, v, *[s for s, _, _ in carry])
    return (out[0], list(out[1:])) if carry else out[0]


def conv_silu(x, w, b, *, name):
    B, S, C = x.shape
    tc = _tile(C, 256)
    pad = SSD_K // 2
    xs = pl.BlockSpec((1, S, tc), lambda bi, j: (bi, 0, j))
    ws = pl.BlockSpec((SSD_K, tc), lambda bi, j: (0, j))
    bs = pl.BlockSpec((1, tc), lambda bi, j: (0, j))

    def shifted(v, off):
        if off == 0:
            return v
        t = lax.broadcasted_iota(jnp.int32, v.shape, 0)
        r = pltpu.roll(v, (-off) % S, 0)
        return jnp.where((t + off >= 0) & (t + off < S), r, 0.0)

    def pre_act(xv, wv, bv):
        u = jnp.zeros_like(xv) + bv
        for k in range(SSD_K):
            u = u + wv[k:k + 1, :] * shifted(xv, k - pad)
        return u

    def fwd_call(x, w, b):
        def body(x_ref, w_ref, b_ref, o_ref):
            u = pre_act(x_ref[0], w_ref[...], b_ref[...])
            o_ref[0] = u * _sigmoid(u)

        return pl.pallas_call(
            body, name=name + "_fwd", grid=(B, C // tc), in_specs=[xs, ws, bs], out_specs=xs,
            out_shape=jax.ShapeDtypeStruct((B, S, C), F32), compiler_params=_cp(),
        )(x, w, b)

    def bwd_call(x, w, b, dy):
        def body(x_ref, w_ref, b_ref, dy_ref, dx_ref, dw_ref):
            xv = x_ref[0]
            wv = w_ref[...]
            u = pre_act(xv, wv, b_ref[...])
            sg = _sigmoid(u)
            du = dy_ref[0] * (sg * (1.0 + u * (1.0 - sg)))
            dx = jnp.zeros_like(xv)
            for k in range(SSD_K):
                dx = dx + wv[k:k + 1, :] * shifted(du, pad - k)
                dw_ref[0, k:k + 1, :] = jnp.sum(du * shifted(xv, k - pad), axis=0, keepdims=True)
            dw_ref[0, SSD_K:SSD_K + 1, :] = jnp.sum(du, axis=0, keepdims=True)
            dw_ref[0, SSD_K + 1:8, :] = jnp.zeros((8 - SSD_K - 1, tc), F32)
            dx_ref[0] = dx

        return pl.pallas_call(
            body, name=name + "_bwd", grid=(B, C // tc), in_specs=[xs, ws, bs, xs],
            out_specs=[xs, pl.BlockSpec((1, 8, tc), lambda bi, j: (bi, 0, j))],
            out_shape=[jax.ShapeDtypeStruct((B, S, C), F32), jax.ShapeDtypeStruct((B, 8, C), F32)],
            compiler_params=_cp(),
        )(x, w, b, dy)

    @jax.custom_vjp
    def f(x, w, b):
        return fwd_call(x, w, b)

    def fwd(x, w, b):
        return fwd_call(x, w, b), (x, w, b)

    def bwd(res, dy):
        x, w, b = res
        dx, dwb = bwd_call(x, w, b, dy)
        dwb = jnp.sum(dwb, axis=0)
        return dx, dwb[:SSD_K], dwb[SSD_K:SSD_K + 1]

    f.defvjp(fwd, bwd)
    return f(x, w, b)


def _softplus(x):
    return jnp.maximum(x, 0.0) + jnp.log1p(jnp.exp(-jnp.abs(x)))


def _ssd_prep(raw, raw_t, brow, arow, bcol, acol, rev):
    li = lax.broadcasted_iota(jnp.int32, (CHUNK, CHUNK), 0)
    ki = lax.broadcasted_iota(jnp.int32, (CHUNK, CHUNK), 1)
    later = (li <= ki) if rev else (li >= ki)
    dt = _softplus(raw + brow)
    a = dt * arow
    cs = _dotf(later.astype(F32), a)
    tot = jnp.sum(a, axis=0, keepdims=True)
    a_t = _softplus(raw_t + bcol) * acol
    earlier = (li >= ki) if rev else (li <= ki)
    cs_t = _dotf(a_t, earlier.astype(F32))
    return dt, a, cs, tot, cs_t, later


def _lane_pick(mat, j):
    lane = lax.broadcasted_iota(jnp.int32, mat.shape, 1)
    return jnp.sum(jnp.where(lane == j, mat, 0.0), axis=1, keepdims=True)


def _head_sum(t, first):
    s0 = jnp.sum(jnp.where(first, t, 0.0), axis=1, keepdims=True)
    s1 = jnp.sum(jnp.where(first, 0.0, t), axis=1, keepdims=True)
    return s0, s1


def ssd_scan(xbc, raw, brow, arow, *, rev, name):
    B, S, _ = xbc.shape
    NC = S // CHUNK
    off = SSD_H if rev else 0
    n_dt = 2 * SSD_H

    def chunk_of(c):
        return (NC - 1 - c) if rev else c

    def specs(cmap):
        return dict(
            x=pl.BlockSpec((1, CHUNK, SSD_INNER), lambda b, c: (b, cmap(c), 0)),
            bm=pl.BlockSpec((1, CHUNK, 2 * SSD_N), lambda b, c: (b, cmap(c), SSD_INNER // (2 * SSD_N))),
            cm=pl.BlockSpec((1, CHUNK, 2 * SSD_N), lambda b, c: (b, cmap(c), SSD_INNER // (2 * SSD_N) + 1)),
            raw=pl.BlockSpec((1, CHUNK, LANES), lambda b, c: (b, cmap(c), 0)),
            raw_t=pl.BlockSpec((1, n_dt, CHUNK), lambda b, c: (b, 0, cmap(c))),
            row=pl.BlockSpec((1, LANES), lambda b, c: (0, 0)),
            colv=pl.BlockSpec((n_dt, 1), lambda b, c: (0, 0)),
            hs=pl.BlockSpec((1, 1, N_PAIR, SSD_N, LANES), lambda b, c: (b, cmap(c), 0, 0, 0)),
        )

    def head_terms(prep, j, first_dummy=None):
        dt, a, cs, tot, cs_t, later = prep
        cs_c = _lane_pick(cs, j)
        cs_r = cs_t[j:j + 1, :]
        dt_c = _lane_pick(dt, j)
        tot_j = _lane_pick(tot, j)
        L = jnp.exp(jnp.where(later, cs_c - cs_r, -1e30))
        return cs_c, cs_r, dt_c, tot_j, L

    def fwd_call(xbc, raw, raw_t, brow, arow, bcol, acol):
        def body(x_ref, bm_ref, cm_ref, raw_ref, rawt_ref, brow_ref, arow_ref, bcol_ref, acol_ref,
                 y_ref, hs_ref, st_ref):
            @pl.when(pl.program_id(1) == 0)
            def _():
                st_ref[...] = jnp.zeros_like(st_ref)

            prep = _ssd_prep(raw_ref[0], rawt_ref[0], brow_ref[...], arow_ref[...],
                             bcol_ref[...], acol_ref[...], rev)
            lane = lax.broadcasted_iota(jnp.int32, (CHUNK, LANES), 1)
            first = lane < SSD_P
            for g in range(SSD_G):
                Bg = _bf(bm_ref[0, :, g * SSD_N:(g + 1) * SSD_N])
                Cg = _bf(cm_ref[0, :, g * SSD_N:(g + 1) * SSD_N])
                G = _dot(Cg, Bg, NT)
                for pp in range(N_PAIR // SSD_G):
                    pi = g * (N_PAIR // SSD_G) + pp
                    c0, _, d0, t0, L0 = head_terms(prep, off + 2 * pi)
                    c1, _, d1, t1, L1 = head_terms(prep, off + 2 * pi + 1)
                    xd = x_ref[0, :, pi * LANES:(pi + 1) * LANES] * jnp.where(first, d0, d1)
                    xdb = _bf(xd)
                    y = jnp.where(first, _dot(_bf(G * L0), xdb, NN), _dot(_bf(G * L1), xdb, NN))
                    dec = jnp.where(first, jnp.exp(t0 - c0), jnp.exp(t1 - c1))
                    h_prev = st_ref[pi]
                    hs_ref[0, 0, pi] = h_prev
                    y = y + _dot(Cg, _bf(h_prev), NN) * jnp.where(first, jnp.exp(c0), jnp.exp(c1))
                    y_ref[0, :, pi * LANES:(pi + 1) * LANES] = y
                    etot = jnp.where(first[:1], jnp.exp(t0), jnp.exp(t1))
                    st_ref[pi] = h_prev * etot + _dot(Bg, _bf(xd * dec), TN)

        sp = specs(chunk_of)
        return pl.pallas_call(
            body, name=name + "_fwd", grid=(B, NC),
            in_specs=[sp["x"], sp["bm"], sp["cm"], sp["raw"], sp["raw_t"], sp["row"], sp["row"],
                      sp["colv"], sp["colv"]],
            out_specs=[sp["x"], sp["hs"]],
            out_shape=[jax.ShapeDtypeStruct((B, S, SSD_INNER), F32),
                       jax.ShapeDtypeStruct((B, NC, N_PAIR, SSD_N, LANES), F32)],
            scratch_shapes=[pltpu.VMEM((N_PAIR, SSD_N, LANES), F32)],
            compiler_params=_cp(),
        )(xbc, xbc, xbc, raw, raw_t, brow, arow, bcol, acol)

    def bwd_call(xbc, raw, raw_t, brow, arow, bcol, acol, hs, dy):
        def body(x_ref, bm_ref, cm_ref, raw_ref, rawt_ref, brow_ref, arow_ref, bcol_ref, acol_ref,
                 hs_ref, dy_ref, dxbc_ref, draw_ref, da_ref, dst_ref):
            @pl.when(pl.program_id(1) == 0)
            def _():
                dst_ref[...] = jnp.zeros_like(dst_ref)
                da_ref[...] = jnp.zeros_like(da_ref)

            raw_v = raw_ref[0]
            prep = _ssd_prep(raw_v, rawt_ref[0], brow_ref[...], arow_ref[...],
                             bcol_ref[...], acol_ref[...], rev)
            dt, a, cs, tot, cs_t, later = prep
            li = lax.broadcasted_iota(jnp.int32, (CHUNK, CHUNK), 0)
            ki = lax.broadcasted_iota(jnp.int32, (CHUNK, CHUNK), 1)
            later_t = (li >= ki) if rev else (li <= ki)
            lane = lax.broadcasted_iota(jnp.int32, (CHUNK, LANES), 1)
            first = lane < SSD_P
            dcs_all = jnp.zeros((CHUNK, LANES), F32)
            ddt_all = jnp.zeros((CHUNK, LANES), F32)
            dtot_all = jnp.zeros((1, LANES), F32)
            for g in range(SSD_G):
                Bg = _bf(bm_ref[0, :, g * SSD_N:(g + 1) * SSD_N])
                Cg = _bf(cm_ref[0, :, g * SSD_N:(g + 1) * SSD_N])
                G = _dot(Cg, Bg, NT)
                Gt = _dot(Bg, Cg, NT)
                dG = jnp.zeros((CHUNK, CHUNK), F32)
                dB = jnp.zeros((CHUNK, SSD_N), F32)
                dC = jnp.zeros((CHUNK, SSD_N), F32)
                for pp in range(N_PAIR // SSD_G):
                    pi = g * (N_PAIR // SSD_G) + pp
                    j0, j1 = off + 2 * pi, off + 2 * pi + 1
                    c0, r0, d0, t0, L0 = head_terms(prep, j0)
                    c1, r1, d1, t1, L1 = head_terms(prep, j1)
                    Lt0 = jnp.exp(jnp.where(later_t, r0 - c0, -1e30))
                    Lt1 = jnp.exp(jnp.where(later_t, r1 - c1, -1e30))
                    xv = x_ref[0, :, pi * LANES:(pi + 1) * LANES]
                    dtp = jnp.where(first, d0, d1)
                    xd = xv * dtp
                    xdb = _bf(xd)
                    dyv = dy_ref[0, :, pi * LANES:(pi + 1) * LANES]
                    dyb = _bf(dyv)
                    dec = jnp.where(first, jnp.exp(t0 - c0), jnp.exp(t1 - c1))
                    ecs = jnp.where(first, jnp.exp(c0), jnp.exp(c1))
                    et0, et1 = jnp.exp(t0), jnp.exp(t1)
                    etot = jnp.where(first[:1], et0, et1)
                    h_prev = hs_ref[0, 0, pi]
                    hpb = _bf(h_prev)
                    dhn = dst_ref[pi]
                    dhb = _bf(dhn)
                    W0, W1 = G * L0, G * L1
                    Wt0, Wt1 = Gt * Lt0, Gt * Lt1
                    bdh = _dot(Bg, dhb, NN)
                    dxd = jnp.where(first, _dot(_bf(Wt0), dyb, NN), _dot(_bf(Wt1), dyb, NN)) + bdh * dec
                    dy0 = _bf(jnp.where(first, dyv, 0.0))
                    dy1 = _bf(jnp.where(first, 0.0, dyv))
                    Q0, Q1 = _dot(dy0, xdb, NT), _dot(dy1, xdb, NT)
                    Qt0, Qt1 = _dot(xdb, dy0, NT), _dot(xdb, dy1, NT)
                    dG = dG + Q0 * L0 + Q1 * L1
                    dcs0 = (jnp.sum(Q0 * W0, axis=1, keepdims=True)
                            - jnp.sum(Qt0 * Wt0, axis=1, keepdims=True))
                    dcs1 = (jnp.sum(Q1 * W1, axis=1, keepdims=True)
                            - jnp.sum(Qt1 * Wt1, axis=1, keepdims=True))
                    dye = dyv * ecs
                    dyeb = _bf(dye)
                    s0, s1 = _head_sum(dye * _dot(Cg, hpb, NN), first)
                    dcs0, dcs1 = dcs0 + s0, dcs1 + s1
                    dC = dC + _dot(dyeb, hpb, NT)
                    dB = dB + _dot(_bf(xd * dec), dhb, NT)
                    u0, u1 = _head_sum(xd * bdh * dec, first)
                    dcs0, dcs1 = dcs0 - u0, dcs1 - u1
                    w = jnp.sum(dhn * h_prev, axis=0, keepdims=True)
                    w0, w1 = _head_sum(w, first[:1])
                    dt0 = jnp.sum(u0, axis=0, keepdims=True) + et0 * w0
                    dt1 = jnp.sum(u1, axis=0, keepdims=True) + et1 * w1
                    dst_ref[pi] = _dot(Cg, dyeb, TN) + dhn * etot
                    q0, q1 = _head_sum(dxd * xv, first)
                    dxbc_ref[0, :, pi * LANES:(pi + 1) * LANES] = dxd * dtp
                    dcs_all = dcs_all + jnp.where(lane == j0, dcs0, 0.0) + jnp.where(lane == j1, dcs1, 0.0)
                    ddt_all = ddt_all + jnp.where(lane == j0, q0, 0.0) + jnp.where(lane == j1, q1, 0.0)
                    dtot_all = (dtot_all + jnp.where(lane[:1] == j0, dt0, 0.0)
                                + jnp.where(lane[:1] == j1, dt1, 0.0))
                dGb = _bf(dG)
                dC = dC + _dot(dGb, Bg, NN)
                dB = dB + _dot(dGb, Cg, TN)
                dxbc_ref[0, :, SSD_INNER + g * SSD_N:SSD_INNER + (g + 1) * SSD_N] = dB
                dxbc_ref[0, :, SSD_INNER + (SSD_G + g) * SSD_N:SSD_INNER + (SSD_G + g + 1) * SSD_N] = dC
            da = _dotf(later_t.astype(F32), dcs_all) + dtot_all
            ddt = ddt_all + da * arow_ref[...]
            da_ref[0] += jnp.sum(da * dt, axis=0, keepdims=True)
            draw_ref[0] = ddt * _sigmoid(raw_v + brow_ref[...])

        def rchunk(c):
            return c if rev else (NC - 1 - c)

        sp = specs(rchunk)
        full = pl.BlockSpec((1, CHUNK, SSD_CONV_DIM), lambda b, c: (b, rchunk(c), 0))
        return pl.pallas_call(
            body, name=name + "_bwd", grid=(B, NC),
            in_specs=[sp["x"], sp["bm"], sp["cm"], sp["raw"], sp["raw_t"], sp["row"], sp["row"],
                      sp["colv"], sp["colv"], sp["hs"], sp["x"]],
            out_specs=[full, sp["raw"], pl.BlockSpec((1, 1, LANES), lambda b, c: (b, 0, 0))],
            out_shape=[jax.ShapeDtypeStruct((B, S, SSD_CONV_DIM), F32),
                       jax.ShapeDtypeStruct((B, S, LANES), F32),
                       jax.ShapeDtypeStruct((B, 1, LANES), F32)],
            scratch_shapes=[pltpu.VMEM((N_PAIR, SSD_N, LANES), F32)],
            compiler_params=_cp(),
        )(xbc, xbc, xbc, raw, raw_t, brow, arow, bcol, acol, hs, dy)

    def aux(raw, brow, arow):
        raw_t = jnp.swapaxes(raw[:, :, :n_dt], 1, 2)
        return raw_t, brow[0, :n_dt][:, None], arow[0, :n_dt][:, None]

    @jax.custom_vjp
    def f(xbc, raw, brow, arow):
        raw_t, bcol, acol = aux(raw, brow, arow)
        return fwd_call(xbc, raw, raw_t, brow, arow, bcol, acol)[0]

    def fwd(xbc, raw, brow, arow):
        raw_t, bcol, acol = aux(raw, brow, arow)
        y, hs = fwd_call(xbc, raw, raw_t, brow, arow, bcol, acol)
        return y, (xbc, raw, brow, arow, hs)

    def bwd(res, dy):
        xbc, raw, brow, arow, hs = res
        raw_t, bcol, acol = aux(raw, brow, arow)
        dxbc, draw, da = bwd_call(xbc, raw, raw_t, brow, arow, bcol, acol, hs, dy)
        dbrow = jnp.sum(draw, axis=(0, 1))[None, :]
        return dxbc, draw, dbrow, jnp.sum(da, axis=0)

    f.defvjp(fwd, bwd)
    return f(xbc, raw, brow, arow)


def ssd_out(yf, yb, xbc, z, dsk, g, *, name):
    T, W = yf.shape
    gs = W // SSD_G
    tr = _tile(T, 512, 8)
    row = pl.BlockSpec((tr, W), lambda i: (i, 0))
    vec = pl.BlockSpec((1, W), lambda i: (0, 0))

    def normed(yv, gv):
        outs, rs = [], []
        for i in range(SSD_G):
            sl = slice(i * gs, (i + 1) * gs)
            r = lax.rsqrt(jnp.mean(yv[:, sl] * yv[:, sl], axis=-1, keepdims=True) + EPS)
            rs.append(r)
            outs.append(yv[:, sl] * r)
        return outs, rs

    def fwd_call(yf, yb, xbc, z, dsk, g):
        def body(yf_ref, yb_ref, xs_ref, z_ref, dsk_ref, g_ref, o_ref):
            zv = z_ref[...]
            yv = (yf_ref[...] + yb_ref[...] + xs_ref[...] * dsk_ref[...]) * (zv * _sigmoid(zv))
            outs, _ = normed(yv, g_ref[...])
            for i in range(SSD_G):
                sl = slice(i * gs, (i + 1) * gs)
                o_ref[:, sl] = (outs[i] * g_ref[:, sl]).astype(o_ref.dtype)

        return pl.pallas_call(
            body, name=name + "_fwd", grid=(T // tr,), in_specs=[row, row, row, row, vec, vec],
            out_specs=row, out_shape=jax.ShapeDtypeStruct((T, W), BF16), compiler_params=_cp(),
        )(yf, yb, xbc, z, dsk, g)

    def bwd_call(yf, yb, xbc, z, dsk, g, do):
        def body(yf_ref, yb_ref, xs_ref, z_ref, dsk_ref, g_ref, do_ref, dy_ref, dxs_ref, dz_ref,
                 ddsk_ref, dg_ref):
            @pl.when(pl.program_id(0) == 0)
            def _():
                ddsk_ref[...] = jnp.zeros_like(ddsk_ref)
                dg_ref[...] = jnp.zeros_like(dg_ref)

            zv = z_ref[...]
            sg = _sigmoid(zv)
            sz = zv * sg
            xs = xs_ref[...]
            pre = yf_ref[...] + yb_ref[...] + xs * dsk_ref[...]
            yv = pre * sz
            outs, rs = normed(yv, g_ref[...])
            for i in range(SSD_G):
                sl = slice(i * gs, (i + 1) * gs)
                dov = do_ref[:, sl].astype(F32)
                xn = outs[i]
                dxn = dov * g_ref[:, sl]
                dyv = rs[i] * (dxn - xn * jnp.mean(dxn * xn, axis=-1, keepdims=True))
                dg_ref[:, sl] += jnp.sum(dov * xn, axis=0, keepdims=True)
                dpre = dyv * sz[:, sl]
                dy_ref[:, sl] = dpre
                dxs_ref[:, sl] = dpre * dsk_ref[:, sl]
                ddsk_ref[:, sl] += jnp.sum(dpre * xs[:, sl], axis=0, keepdims=True)
                dz_ref[:, sl] = dyv * pre[:, sl] * (sg[:, sl] * (1.0 + zv[:, sl] * (1.0 - sg[:, sl])))

        o = jax.ShapeDtypeStruct((T, W), F32)
        v = jax.ShapeDtypeStruct((1, W), F32)
        return pl.pallas_call(
            body, name=name + "_bwd", grid=(T // tr,), in_specs=[row, row, row, row, vec, vec, row],
            out_specs=[row, row, row, vec, vec], out_shape=[o, o, o, v, v], compiler_params=_cp(),
        )(yf, yb, xbc, z, dsk, g, do)

    @jax.custom_vjp
    def f(yf, yb, xbc, z, dsk, g):
        return fwd_call(yf, yb, xbc, z, dsk, g)

    def fwd(yf, yb, xbc, z, dsk, g):
        return fwd_call(yf, yb, xbc, z, dsk, g), (yf, yb, xbc, z, dsk, g)

    def bwd(res, do):
        dy, dxs, dz, ddsk, dg = bwd_call(*res, do)
        dxbc = jnp.pad(dxs, ((0, 0), (0, res[2].shape[1] - W)))
        return dy, dy, dxbc, dz, ddsk, dg

    f.defvjp(fwd, bwd)
    return f(yf, yb, xbc, z, dsk, g)


def swiglu(gu, *, name):
    T, F2 = gu.shape
    Fh = F2 // 2
    tr, tf = _tile(T, 512, 8), _tile(Fh, 512)
    nf = Fh // tf
    gs = pl.BlockSpec((tr, tf), lambda i, j: (i, j))
    us = pl.BlockSpec((tr, tf), lambda i, j: (i, j + nf))

    def fwd_call(gu):
        def body(g_ref, u_ref, o_ref):
            gv = g_ref[...].astype(F32)
            o_ref[...] = (gv * _sigmoid(gv) * u_ref[...].astype(F32)).astype(o_ref.dtype)

        return pl.pallas_call(
            body, name=name + "_fwd", grid=(T // tr, nf), in_specs=[gs, us], out_specs=gs,
            out_shape=jax.ShapeDtypeStruct((T, Fh), BF16), compiler_params=_cp(),
        )(gu, gu)

    def bwd_call(gu, da):
        def body(g_ref, u_ref, da_ref, dgu_ref):
            j = pl.program_id(1)
            gv = g_ref[...].astype(F32)
            uv = u_ref[...].astype(F32)
            dav = da_ref[...].astype(F32)
            sg = _sigmoid(gv)

            @pl.when(j < nf)
            def _():
                dgu_ref[...] = (dav * uv * (sg * (1.0 + gv * (1.0 - sg)))).astype(dgu_ref.dtype)

            @pl.when(j >= nf)
            def _():
                dgu_ref[...] = (dav * gv * sg).astype(dgu_ref.dtype)

        gsel = pl.BlockSpec((tr, tf), lambda i, j: (i, j % nf))
        usel = pl.BlockSpec((tr, tf), lambda i, j: (i, j % nf + nf))
        return pl.pallas_call(
            body, name=name + "_bwd", grid=(T // tr, 2 * nf), in_specs=[gsel, usel, gsel],
            out_specs=pl.BlockSpec((tr, tf), lambda i, j: (i, j)),
            out_shape=jax.ShapeDtypeStruct((T, F2), BF16), compiler_params=_cp(),
        )(gu, gu, da)

    @jax.custom_vjp
    def f(gu):
        return fwd_call(gu)

    def fwd(gu):
        return fwd_call(gu), gu

    def bwd(gu, da):
        return (bwd_call(gu, da),)

    f.defvjp(fwd, bwd)
    return f(gu)


def gated_residual(x, gate, y, *, name):
    B, S, D = x.shape
    ts = _tile(S, 256, 8)
    row = pl.BlockSpec((1, ts, D), lambda b, j: (b, j, 0))
    per_b = pl.BlockSpec((1, 1, D), lambda b, j: (b, 0, 0))

    def fwd_call(x, gate, y):
        def body(x_ref, gt_ref, y_ref, o_ref):
            o_ref[0] = x_ref[0] + gt_ref[0] * y_ref[0]

        return pl.pallas_call(
            body, name=name + "_fwd", grid=(B, S // ts), in_specs=[row, per_b, row], out_specs=row,
            out_shape=jax.ShapeDtypeStruct((B, S, D), F32), compiler_params=_cp(),
        )(x, gate, y)

    def bwd_call(gate, y, g):
        def body(gt_ref, y_ref, g_ref, dy_ref, dgt_ref):
            @pl.when(pl.program_id(1) == 0)
            def _():
                dgt_ref[...] = jnp.zeros_like(dgt_ref)

            gv = g_ref[0]
            dy_ref[0] = gt_ref[0] * gv
            dgt_ref[0] += jnp.sum(gv * y_ref[0], axis=0, keepdims=True)

        return pl.pallas_call(
            body, name=name + "_bwd", grid=(B, S // ts), in_specs=[per_b, row, row],
            out_specs=[row, per_b],
            out_shape=[jax.ShapeDtypeStruct((B, S, D), F32), jax.ShapeDtypeStruct((B, 1, D), F32)],
            compiler_params=_cp(),
        )(gate, y, g)

    @jax.custom_vjp
    def f(x, gate, y):
        return fwd_call(x, gate, y)

    def fwd(x, gate, y):
        return fwd_call(x, gate, y), (gate, y)

    def bwd(res, g):
        dy, dgate = bwd_call(res[0], res[1], g)
        return g, dgate, dy

    f.defvjp(fwd, bwd)
    return f(x, gate, y)


def final_loss(x, g, target, *, name):
    T, D = x.shape
    tr = _tile(T, 256, 8)
    row = pl.BlockSpec((tr, D), lambda i: (i, 0))
    vec = pl.BlockSpec((1, D), lambda i: (0, 0))

    def fwd_call(x, g, target):
        def body(x_ref, g_ref, t_ref, o_ref):
            @pl.when(pl.program_id(0) == 0)
            def _():
                o_ref[...] = jnp.zeros_like(o_ref)

            xv = x_ref[...]
            r = lax.rsqrt(jnp.mean(xv * xv, axis=-1, keepdims=True) + EPS)
            e = xv * r * g_ref[...] - t_ref[...]
            o_ref[...] += jnp.sum(e * e, axis=0, keepdims=True)

        part = pl.pallas_call(
            body, name=name + "_fwd", grid=(T // tr,), in_specs=[row, vec, row], out_specs=vec,
            out_shape=jax.ShapeDtypeStruct((1, D), F32), compiler_params=_cp(),
        )(x, g, target)
        return (0.5 / D) * jnp.sum(part)

    def bwd_call(x, g, target, ct):
        def body(x_ref, g_ref, t_ref, ct_ref, dx_ref, dg_ref):
            @pl.when(pl.program_id(0) == 0)
            def _():
                dg_ref[...] = jnp.zeros_like(dg_ref)

            xv = x_ref[...]
            gv = g_ref[...]
            r = lax.rsqrt(jnp.mean(xv * xv, axis=-1, keepdims=True) + EPS)
            xn = xv * r
            dy = (xn * gv - t_ref[...]) * (ct_ref[...] * (1.0 / D))
            dxn = dy * gv
            dx_ref[...] = r * (dxn - xn * jnp.mean(dxn * xn, axis=-1, keepdims=True))
            dg_ref[...] += jnp.sum(dy * xn, axis=0, keepdims=True)

        return pl.pallas_call(
            body, name=name + "_bwd", grid=(T // tr,),
            in_specs=[row, vec, row, pl.BlockSpec((1, 1), lambda i: (0, 0))], out_specs=[row, vec],
            out_shape=[jax.ShapeDtypeStruct((T, D), F32), jax.ShapeDtypeStruct((1, D), F32)],
            compiler_params=_cp(),
        )(x, g, target, ct)

    @jax.custom_vjp
    def f(x, g, target):
        return fwd_call(x, g, target)

    def fwd(x, g, target):
        return fwd_call(x, g, target), (x, g, target)

    def bwd(res, ct):
        x, g, target = res
        dx, dg = bwd_call(x, g, target, jnp.reshape(ct, (1, 1)).astype(F32))
        return dx, dg, jnp.zeros_like(target)

    f.defvjp(fwd, bwd)
    return f(x, g, target)


def adamw(w, g, m, v, *, name):
    L, R, C = w.shape
    tr = _tile(R, 512, 8)
    spec = pl.BlockSpec((1, tr, C), lambda l, i: (l, i, 0))
    c1 = 1.0 / (1.0 - ADAM_B1 ** ADAM_STEP)
    c2 = 1.0 / (1.0 - ADAM_B2 ** ADAM_STEP)

    def body(w_ref, g_ref, m_ref, v_ref, d_ref, nm_ref, nv_ref):
        gv = g_ref[...]
        nm = ADAM_B1 * m_ref[...] + (1.0 - ADAM_B1) * gv
        nv = ADAM_B2 * v_ref[...] + (1.0 - ADAM_B2) * (gv * gv)
        nm_ref[...] = nm
        nv_ref[...] = nv
        d_ref[...] = -ADAM_LR * ((nm * c1) / (jnp.sqrt(nv * c2) + ADAM_EPS) + ADAM_WD * w_ref[...])

    o = jax.ShapeDtypeStruct((L, R, C), F32)
    return pl.pallas_call(
        body, name=name, grid=(L, R // tr), in_specs=[spec] * 4, out_specs=[spec] * 3,
        out_shape=[o, o, o], compiler_params=_cp(),
    )(w, g, m, v)


def _position():
    x, y, c = lax.axis_index("x"), lax.axis_index("y"), lax.axis_index("c")
    return x, y, c


def ag_routine(shard, cols=False):
    R, C = shard.shape
    assert not cols or C % LANES == 0

    def parts(ins, outs, send_sems, recv_sems, local_sems):
        (x_ref,), (out_ref,) = ins, outs
        x, y, c = _position()
        me, sibling = (x, y, c), (x, y, 1 - c)
        chips = [(1 - x, y), (x, 1 - y), (1 - x, 1 - y)]

        def block(px, py, pc):
            idx = 4 * px + 2 * py + pc
            if cols:
                return out_ref.at[:, pl.ds(pl.multiple_of(idx * C, LANES), C)]
            return out_ref.at[idx]

        def copy(k, blk, to, src=None):
            return pltpu.make_async_remote_copy(
                src_ref=block(*blk) if src is None else src, dst_ref=block(*blk),
                send_sem=send_sems.at[k], recv_sem=recv_sems.at[k],
                device_id=to, device_id_type=pl.DeviceIdType.MESH)

        mine = pltpu.make_async_copy(x_ref, block(*me), local_sems.at[0])
        first = [copy(0, me, sibling, src=x_ref)]
        first += [copy(1 + j, me, (*chip, c), src=x_ref) for j, chip in enumerate(chips)]
        passed = [copy(4 + j, (*chip, c), sibling) for j, chip in enumerate(chips)]
        return me, sibling, c, chips, copy, mine, first, passed

    def start(*refs):
        me, sibling, c, chips, copy, mine, first, passed = parts(*refs)
        mine.start()
        for cp in first:
            cp.start()

    def finish(*refs):
        me, sibling, c, chips, copy, mine, first, passed = parts(*refs)
        for j, chip in enumerate(chips):
            copy(1 + j, (*chip, c), me).wait_recv()
            passed[j].start()
        copy(0, sibling, me).wait_recv()
        for j, chip in enumerate(chips):
            copy(4 + j, (*chip, 1 - c), me).wait_recv()
        for cp in first + passed:
            cp.wait_send()
        mine.wait()

    out = jax.ShapeDtypeStruct((R, N_DEV * C) if cols else (N_DEV, R, C), shard.dtype)
    return dict(ins=[shard], outs=[out], n_sem=7, n_local=1, start=start, finish=finish)


def all_gather(shard, *, name, cols=False):
    return comm_call(ag_routine(shard, cols), name=name)[0]


def carry_call(body, *, name, grid, in_specs, out_specs, out_shape, scratch_shapes=(), carry=(),
               dims=None):
    in_specs, out_specs, out_shape = list(in_specs), list(out_specs), list(out_shape)
    scratch_shapes = list(scratch_shapes)
    if not carry:
        call = pl.pallas_call(body, name=name, grid=grid, in_specs=in_specs, out_specs=out_specs,
                              out_shape=out_shape, scratch_shapes=scratch_shapes,
                              compiler_params=_cp(dimension_semantics=dims) if dims else _cp())
        return lambda *args: list(call(*args))
    n_in, n_out, n_scr = len(in_specs), len(out_specs), len(scratch_shapes)
    c_ins = [a for r in carry for a in r["ins"]]
    c_outs = [o for r in carry for o in r["outs"]]
    sems = []
    for r in carry:
        sems += [pltpu.SemaphoreType.DMA((r["n_sem"],)), pltpu.SemaphoreType.DMA((r["n_sem"],)),
                 pltpu.SemaphoreType.DMA((r["n_local"],))]

    def wrapped(*refs):
        refs = list(refs)
        ins, refs = refs[:n_in], refs[n_in:]
        cin, refs = refs[:len(c_ins)], refs[len(c_ins):]
        outs, refs = refs[:n_out], refs[n_out:]
        cout, refs = refs[:len(c_outs)], refs[len(c_outs):]
        scr, csem = refs[:n_scr], refs[n_scr:]
        ids = [pl.program_id(i) for i in range(len(grid))]
        first = functools.reduce(jnp.logical_and, [i == 0 for i in ids])
        last = functools.reduce(jnp.logical_and, [i == g - 1 for i, g in zip(ids, grid)])

        def each(which):
            io = oo = 0
            for j, r in enumerate(carry):
                r[which](cin[io:io + len(r["ins"])], cout[oo:oo + len(r["outs"])], *csem[3 * j:3 * j + 3])
                io += len(r["ins"])
                oo += len(r["outs"])

        @pl.when(first)
        def _():
            each("start")

        body(*ins, *outs, *scr)

        @pl.when(last)
        def _():
            each("finish")

    any_spec = pl.BlockSpec(memory_space=pl.ANY)
    call = pl.pallas_call(
        wrapped, name=name, grid=grid, in_specs=in_specs + [any_spec] * len(c_ins),
        out_specs=out_specs + [any_spec] * len(c_outs), out_shape=out_shape + c_outs,
        scratch_shapes=scratch_shapes + sems, compiler_params=_cp())
    return lambda *args: list(call(*args, *c_ins))


N_CHIP = 4


def rs_pair_routine(g, cols):
    if cols:
        R, C = g.shape[0], g.shape[1] // N_DEV
        assert C % LANES == 0
    else:
        _, R, C = g.shape

    def blk(ref, idx):
        if cols:
            return ref.at[:, pl.ds(pl.multiple_of(idx * C, LANES), C)]
        return ref.at[idx]

    def copies(ins, outs, send_sems, recv_sems, local_sems):
        (g_ref,), (got_ref,) = ins, outs
        x, y, c = _position()
        local, remote = [], []
        for q in range(N_CHIP):
            remote.append(pltpu.make_async_remote_copy(
                src_ref=blk(g_ref, 2 * q + 1 - c), dst_ref=got_ref.at[q],
                send_sem=send_sems.at[q], recv_sem=recv_sems.at[q],
                device_id=(x, y, 1 - c), device_id_type=pl.DeviceIdType.MESH))
        return local, remote

    def start(*refs):
        local, remote = copies(*refs)
        for cp in remote + local:
            cp.start()

    def finish(*refs):
        local, remote = copies(*refs)
        for cp in remote:
            cp.wait_recv()
        for cp in remote:
            cp.wait_send()
        for cp in local:
            cp.wait()

    o = jax.ShapeDtypeStruct((N_CHIP, R, C), g.dtype)
    return dict(ins=[g], outs=[o], n_sem=N_CHIP, n_local=1, start=start, finish=finish)


def rs_chip_routine(h):
    _, R, C = h.shape
    RELATIONS = ((0, 1), (1, 0), (1, 1))

    def copies(ins, outs, send_sems, recv_sems, local_sems):
        (h_ref,), (out_ref,) = ins, outs
        x, y, c = _position()
        local, remote = [], []
        for k, (fx, fy) in enumerate(RELATIONS):
            px = (1 - x) if fx else x
            py = (1 - y) if fy else y
            remote.append(pltpu.make_async_remote_copy(
                src_ref=h_ref.at[2 * px + py], dst_ref=out_ref.at[k],
                send_sem=send_sems.at[k], recv_sem=recv_sems.at[k],
                device_id=(px, py, c), device_id_type=pl.DeviceIdType.MESH))
        return local, remote

    def start(*refs):
        local, remote = copies(*refs)
        for cp in remote + local:
            cp.start()

    def finish(*refs):
        local, remote = copies(*refs)
        for cp in remote:
            cp.wait_recv()
        for cp in remote:
            cp.wait_send()
        for cp in local:
            cp.wait()

    return dict(ins=[h], outs=[jax.ShapeDtypeStruct((N_CHIP - 1, R, C), h.dtype)], n_sem=3, n_local=1,
                start=start, finish=finish)


def comm_call(routine, *, name):
    n_in, n_out = len(routine["ins"]), len(routine["outs"])

    def body(*refs):
        ins, outs, sems = refs[:n_in], refs[n_in:n_in + n_out], refs[n_in + n_out:]
        routine["start"](ins, outs, *sems)
        routine["finish"](ins, outs, *sems)

    any_spec = pl.BlockSpec(memory_space=pl.ANY)
    return pl.pallas_call(
        body, name=name, out_shape=routine["outs"],
        in_specs=[any_spec] * n_in, out_specs=[any_spec] * n_out,
        scratch_shapes=[pltpu.SemaphoreType.DMA((routine["n_sem"],)), pltpu.SemaphoreType.DMA((routine["n_sem"],)),
                        pltpu.SemaphoreType.DMA((routine["n_local"],))],
    )(*routine["ins"])


def add_own(g, got, core, *, cols, name):
    n, R, C = got.shape
    tr = _tile(R, 256, 8)
    if cols:
        gspec = pl.BlockSpec((tr, C), lambda q, i, c_ref: (i, 2 * q + c_ref[0]))
    else:
        gspec = pl.BlockSpec((1, tr, C), lambda q, i, c_ref: (2 * q + c_ref[0], i, 0))
    spec = pl.BlockSpec((1, tr, C), lambda q, i, c_ref: (q, i, 0))

    def body(c_ref, g_ref, b_ref, o_ref):
        gv = g_ref[...] if cols else g_ref[0]
        o_ref[0] = (gv.astype(F32) + b_ref[0].astype(F32)).astype(o_ref.dtype)

    return pl.pallas_call(
        body, name=name,
        grid_spec=pltpu.PrefetchScalarGridSpec(num_scalar_prefetch=1, grid=(n, R // tr),
                                               in_specs=[gspec, spec], out_specs=spec),
        out_shape=jax.ShapeDtypeStruct((n, R, C), g.dtype), compiler_params=_cp(),
    )(core, g, got)


def sum_chips(h, got, chip, *, name):
    _, R, C = h.shape
    tr = _tile(R, 256, 8)

    def body(q_ref, h_ref, g_ref, o_ref):
        acc = h_ref[0].astype(F32)
        for k in range(N_CHIP - 1):
            acc = acc + g_ref[k].astype(F32)
        o_ref[...] = acc

    return pl.pallas_call(
        body, name=name,
        grid_spec=pltpu.PrefetchScalarGridSpec(
            num_scalar_prefetch=1, grid=(R // tr,),
            in_specs=[pl.BlockSpec((1, tr, C), lambda i, q_ref: (q_ref[0], i, 0)),
                      pl.BlockSpec((N_CHIP - 1, tr, C), lambda i, q_ref: (0, i, 0))],
            out_specs=pl.BlockSpec((tr, C), lambda i, q_ref: (i, 0))),
        out_shape=jax.ShapeDtypeStruct((R, C), F32), compiler_params=_cp(),
    )(chip, h, got)


def reduce_scatter(g, *, cols, name):
    x, y, c = _position()
    core = jnp.reshape(c, (1,)).astype(jnp.int32)
    chip = jnp.reshape(2 * x + y, (1,)).astype(jnp.int32)
    got = comm_call(rs_pair_routine(g, cols), name=name + "_pair")[0]
    h = add_own(g, got, core, cols=cols, name=name + "_add")
    return sum_chips(h, comm_call(rs_chip_routine(h), name=name + "_chip")[0], chip, name=name + "_sum")


def sum_blocks(stack, *, name):
    n, R, C = stack.shape
    tr = _tile(R, 256, 8)

    def body(x_ref, o_ref):
        acc = x_ref[0].astype(F32)
        for i in range(1, n):
            acc = acc + x_ref[i].astype(F32)
        o_ref[...] = acc

    return pl.pallas_call(
        body, name=name, grid=(R // tr,),
        in_specs=[pl.BlockSpec((n, tr, C), lambda i: (0, i, 0))],
        out_specs=pl.BlockSpec((tr, C), lambda i: (i, 0)),
        out_shape=jax.ShapeDtypeStruct((R, C), F32), compiler_params=_cp(),
    )(stack)


PACK_COLS = 1024
PACK_ROW_MULT = 8


def _pack(arrays, dtype):
    flat = jnp.concatenate([a.reshape(-1).astype(dtype) for a in arrays])
    n = flat.shape[0]
    unit = PACK_COLS * PACK_ROW_MULT
    padded = -(-n // unit) * unit
    return jnp.pad(flat, (0, padded - n)).reshape(padded // PACK_COLS, PACK_COLS)


def _unpack(packed, shapes):
    flat = packed.reshape(-1)
    out, o = [], 0
    for s in shapes:
        n = int(np.prod(s))
        out.append(flat[o:o + n].reshape(s))
        o += n
    return out


def fsdp_cols(shard, *, name):
    K, n = shard.shape
    npad = -(-n // LANES) * LANES

    @jax.custom_vjp
    def f(p):
        p = jnp.pad(p, ((0, 0), (0, npad - n))) if npad != n else p
        return all_gather(p.astype(BF16), cols=True, name=name + "_ag")

    def fwd(p):
        return f(p), None

    def bwd(_, g):
        d = reduce_scatter(g, cols=True, name=name + "_rs")
        return (d[:, :n] if npad != n else d,)

    f.defvjp(fwd, bwd)
    return f(shard)


def fsdp_rows(shard, *, name):
    k, N = shard.shape

    @jax.custom_vjp
    def f(p):
        return all_gather(p.astype(BF16), name=name + "_ag").reshape(N_DEV * k, N)

    def fwd(p):
        return f(p), None

    def bwd(_, g):
        return (reduce_scatter(g.reshape(N_DEV, k, N), cols=False, name=name + "_rs"),)

    f.defvjp(fwd, bwd)
    return f(shard)


def _unpad_cols(w, n):
    K = w.shape[0]
    npad = w.shape[1] // N_DEV
    if npad == n:
        return w
    return w.reshape(K, N_DEV, npad)[:, :, :n].reshape(K, N_DEV * n)


def gather_rows(part, me, *, name):
    rows, n = part.shape
    per = rows // N_DEV

    @jax.custom_vjp
    def f(part):
        full = all_gather(part, name=name + "_fwd")
        mine = lax.dynamic_slice_in_dim(full, me * per, per, axis=1)
        return jnp.swapaxes(mine, 0, 1).reshape(per, N_DEV * n)

    def fwd(part):
        return f(part), None

    def bwd(_, g):
        full = all_gather(g, name=name + "_bwd")
        mine = lax.dynamic_slice_in_dim(full, me * n, n, axis=2)
        return (mine.reshape(rows, n),)

    f.defvjp(fwd, bwd)
    return f(part)


def _seg_layout():
    offs = np.concatenate([[0], np.cumsum(IN_SPLITS)])
    cols, widths, leads = [], [], []
    for s in SEG_ORDER:
        cols.append((int(offs[s]), int(offs[s + 1])))
        widths.append(SEG_PAD.get(s, IN_SPLITS[s]))
        leads.append(SEG_LEAD.get(s, 0))
    return cols, widths, leads


def _arrange_w_in(w, n):
    D = w.shape[0]
    npad = w.shape[1] // N_DEV
    cols, widths, leads = _seg_layout()

    def pieces_of(a, b):
        out = []
        for d in range(N_DEV):
            lo, hi = max(a, n * d), min(b, n * (d + 1))
            if lo < hi:
                out.append((d, lo, hi))
        return out

    @jax.custom_vjp
    def f(w):
        parts = []
        for (a, b), wd, ld in zip(cols, widths, leads):
            if ld:
                parts.append(jnp.zeros((D, ld), w.dtype))
            parts += [w[:, npad * d + lo - n * d:npad * d + hi - n * d] for d, lo, hi in pieces_of(a, b)]
            if wd != ld + b - a:
                parts.append(jnp.zeros((D, wd - ld - (b - a)), w.dtype))
        parts.append(jnp.zeros((D, IN_WIDTH - sum(widths)), w.dtype))
        return jnp.concatenate(parts, axis=1)

    def fwd(w):
        return f(w), None

    def bwd(_, g):
        offs = np.concatenate([[0], np.cumsum(widths)])
        runs = []
        for ((a, b), off, ld) in zip(cols, offs[:-1], leads):
            runs += [(lo, int(off) + ld + lo - a, hi - lo) for _, lo, hi in pieces_of(a, b)]
        runs.sort()
        parts, d_next = [], 1
        for lo, o, ln in runs:
            while lo >= n * d_next:
                parts.append(jnp.zeros((D, npad - n), g.dtype))
                d_next += 1
            parts.append(g[:, o:o + ln])
        parts.append(jnp.zeros((D, npad - n), g.dtype))
        return (jnp.concatenate(parts, axis=1),)

    f.defvjp(fwd, bwd)
    return f(w)


def split_cols(proj, widths):
    @jax.custom_vjp
    def f(p):
        outs, o = [], 0
        for wd in widths:
            outs.append(p[:, o:o + wd])
            o += wd
        return tuple(outs)

    def fwd(p):
        return f(p), None

    def bwd(_, gs):
        rest = proj.shape[1] - sum(widths)
        tail = [jnp.zeros((proj.shape[0], rest), proj.dtype)] if rest else []
        return (jnp.concatenate(list(gs) + tail, axis=1),)

    f.defvjp(fwd, bwd)
    return f(proj)


BIG = ("w_in", "w_uq", "w_ukv", "conv_w", "w_out", "w_gate_up", "w_down")
SMALL = ("b_ada", "norm1_g", "norm2_g", "q_norm_g", "k_norm_g", "mla_q_norm_g", "mla_kv_norm_g",
         "conv_b", "dt_bias", "a_log", "d_skip", "ssd_norm_g", "final_norm_g")
WEIGHTS = ("w_ada", "b_ada", "norm1_g", "norm2_g", "w_in", "q_norm_g", "k_norm_g", "mla_q_norm_g",
           "w_uq", "mla_kv_norm_g", "w_ukv", "conv_w", "conv_b", "dt_bias", "a_log", "d_skip",
           "ssd_norm_g", "w_out", "w_gate_up", "w_down", "final_norm_g")


PRE = ("w_in", "w_uq", "w_ukv", "conv_w")


def _layer(l, x, mod, W, shards, nxt, P, tabs):
    B, S, D = x.shape
    T = B * S
    nm = f"l{l}_"
    shift1, scale1, gate1, shift2, scale2, gate2 = [m[:, None, :] for m in jnp.split(mod, 6, axis=-1)]
    cos_a, sin_a, cos_b, sin_b, cos_k, sin_k = tabs

    h = rmsmod(x, P["norm1_g"][l][None], scale1, shift1, name=nm + "norm1")
    w_in = _arrange_w_in(W["w_in"], IN_COLS // N_DEV)
    proj, (w_out_full,) = linear(h.reshape(T, D), w_in, out_dtype=F32, name=nm + "in",
                                 carry=((shards["w_out"], False, "w_out"),))
    q_a, k_a, v_a, cq, ckv, z, xbc, kpe, dtr = split_cols(proj, _seg_layout()[1])

    qn = group_rmsnorm(q_a, jnp.tile(P["q_norm_g"][l], GQA_H)[None], gs=HEAD, out_dtype=F32, name=nm + "qnorm")
    kn = group_rmsnorm(k_a, jnp.tile(P["k_norm_g"][l], GQA_KV)[None], gs=HEAD, out_dtype=F32, name=nm + "knorm")
    qr = rope(qn.reshape(B, S, -1), cos_a[:, :GQA_H * HEAD], sin_a[:, :GQA_H * HEAD], d=HEAD // 4, name=nm + "qrope")
    kr = rope(kn.reshape(B, S, -1), cos_a[:, :GQA_KV * HEAD], sin_a[:, :GQA_KV * HEAD], d=HEAD // 4, name=nm + "krope")
    gu_rows = shards["w_gate_up"].shape[0] // 2
    o_a, (w_gu_top,) = attention(qr, kr, v_a.reshape(B, S, -1), H=GQA_H, Hkv=GQA_KV, dk=HEAD, dv=HEAD,
                                 scale=HEAD ** -0.5, name=nm + "gqa",
                                 carry=((shards["w_gate_up"][:gu_rows], True, "w_gate_up_top"),))

    slot_pad = MLA_DK - MLA_NOPE - MLA_ROPE
    w_uq = jnp.pad(W["w_uq"].reshape(MLA_QL, MLA_H, MLA_NOPE + MLA_ROPE), ((0, 0), (0, 0), (0, slot_pad)))
    w_uq = w_uq.reshape(MLA_QL, MLA_H * MLA_DK)
    w_ukv = W["w_ukv"].reshape(MLA_KVL, MLA_H, MLA_NOPE + MLA_V)
    w_ukv = jnp.concatenate(
        [jnp.pad(w_ukv[:, :, :MLA_NOPE], ((0, 0), (0, 0), (0, MLA_DK - MLA_NOPE))).reshape(MLA_KVL, -1),
         w_ukv[:, :, MLA_NOPE:].reshape(MLA_KVL, -1)], axis=1)
    cqn = group_rmsnorm(cq, P["mla_q_norm_g"][l][None], gs=MLA_QL, out_dtype=BF16, name=nm + "cqnorm")
    ckvn = group_rmsnorm(ckv, P["mla_kv_norm_g"][l][None], gs=MLA_KVL, out_dtype=BF16, name=nm + "ckvnorm")
    qb = linear(cqn, w_uq, out_dtype=F32, name=nm + "uq")
    kvb = linear(ckvn, w_ukv, out_dtype=F32, name=nm + "ukv")
    k_slots, v_b = split_cols(kvb, (MLA_H * MLA_DK, MLA_H * MLA_V))
    q_cat = rope(qb.reshape(B, S, -1), cos_b, sin_b, d=MLA_ROPE // 4, name=nm + "qpe_rope")
    k_pe = rope(kpe.reshape(B, S, -1), cos_k, sin_k, d=MLA_ROPE // 4, name=nm + "kpe_rope")
    k_cat = k_slots.reshape(B, S, -1) + jnp.tile(k_pe, (1, 1, MLA_H))
    o_b, (w_gu_bottom,) = attention(q_cat, k_cat, v_b.reshape(B, S, -1), H=MLA_H, Hkv=MLA_H, dk=MLA_DK, dv=MLA_V,
                                    scale=(MLA_NOPE + MLA_ROPE) ** -0.5, name=nm + "mla",
                                    carry=((shards["w_gate_up"][gu_rows:], True, "w_gate_up_bottom"),))
    w_gu_full = jnp.concatenate([w_gu_top, w_gu_bottom], axis=0)

    xact = conv_silu(xbc.reshape(B, S, -1), W["conv_w"].astype(F32), P["conv_b"][l][None], name=nm + "conv")
    brow = jnp.pad(P["dt_bias"][l].reshape(1, -1), ((0, 0), (0, LANES - 2 * SSD_H)))
    arow = jnp.pad(-jnp.exp(P["a_log"][l].reshape(1, -1)), ((0, 0), (0, LANES - 2 * SSD_H)))
    raw = dtr.reshape(B, S, LANES)
    y_f = ssd_scan(xact, raw, brow, arow, rev=False, name=nm + "ssd_f")
    y_b = ssd_scan(xact, raw, brow, arow, rev=True, name=nm + "ssd_b")
    dsk = jnp.repeat(P["d_skip"][l], SSD_P)[None]
    o_c = ssd_out(y_f.reshape(T, -1), y_b.reshape(T, -1), xact.reshape(T, -1), z, dsk,
                  P["ssd_norm_g"][l][None], name=nm + "ssd_out")

    o = jnp.concatenate([o_a.reshape(T, -1).astype(BF16), o_b.reshape(T, -1).astype(BF16), o_c], axis=-1)
    mix = linear(o, w_out_full, out_dtype=F32, name=nm + "out")
    x = gated_residual(x, gate1, mix.reshape(B, S, D), name=nm + "res1")

    h = rmsmod(x, P["norm2_g"][l][None], scale2, shift2, name=nm + "norm2")
    y, nxt_full = ffn(h.reshape(T, D), w_gu_full, shards["w_down"], name=nm + "ffn",
                      carry=tuple((nxt[n], True, "next_" + n) for n in PRE) if nxt else ())
    return gated_residual(x, gate2, y.reshape(B, S, D), name=nm + "res2"), nxt_full


def kernel(x, c, w_ada, b_ada, norm1_g, norm2_g, w_in, q_norm_g, k_norm_g, mla_q_norm_g, w_uq, mla_kv_norm_g, w_ukv, conv_w, conv_b, dt_bias, a_log, d_skip, ssd_norm_g, w_out, w_gate_up, w_down, final_norm_g, loss_target, m_w_ada, m_b_ada, m_norm1_g, m_norm2_g, m_w_in, m_q_norm_g, m_k_norm_g, m_mla_q_norm_g, m_w_uq, m_mla_kv_norm_g, m_w_ukv, m_conv_w, m_conv_b, m_dt_bias, m_a_log, m_d_skip, m_ssd_norm_g, m_w_out, m_w_gate_up, m_w_down, m_final_norm_g, v_w_ada, v_b_ada, v_norm1_g, v_norm2_g, v_w_in, v_q_norm_g, v_k_norm_g, v_mla_q_norm_g, v_w_uq, v_mla_kv_norm_g, v_w_ukv, v_conv_w, v_conv_b, v_dt_bias, v_a_log, v_d_skip, v_ssd_norm_g, v_w_out, v_w_gate_up, v_w_down, v_final_norm_g):
    args = dict(locals())
    weights = {n: args[n] for n in WEIGHTS}
    moments_m = {n: args["m_" + n] for n in WEIGHTS}
    moments_v = {n: args["v_" + n] for n in WEIGHTS}
    B, S, D = x.shape
    L = w_ada.shape[0]
    T = B * S
    px, py, pc = _position()
    me = 4 * px + 2 * py + pc
    small_shapes = [weights[n].shape for n in SMALL]

    tabs = (*rope_tables(S, HEAD, GQA_H * HEAD),
            *rope_tables(S, MLA_ROPE, MLA_H * MLA_DK, slot=MLA_DK, lead=MLA_NOPE),
            *rope_tables(S, MLA_ROPE, MLA_DK, slot=MLA_DK, lead=MLA_NOPE))
    c_all = all_gather(c, name="gather_c").reshape(N_DEV * B, D)

    def local_loss(big, w_ada_s, small, x):
        P = dict(zip(SMALL, small))
        pre = [fsdp_cols(big[n][0], name=f"l0_{n}") for n in PRE]
        for l in range(L):
            W = {n: full if n == "w_in" else _unpad_cols(full, big[n].shape[2]) for n, full in zip(PRE, pre)}
            shards = {n: big[n][l] for n in ("w_out", "w_gate_up", "w_down")}
            nxt = {n: big[n][l + 1] for n in PRE} if l + 1 < L else None
            part = linear(c_all, w_ada_s[l], out_dtype=F32, a_silu=True, name=f"l{l}_ada")
            mod = gather_rows(part, me, name=f"l{l}_mod") + P["b_ada"][l][None]
            x, pre = _layer(l, x, mod, W, shards, nxt, P, tabs)
        return final_loss(x.reshape(T, D), P["final_norm_g"][None], loss_target.reshape(T, D), name="loss")

    big = {n: weights[n] for n in BIG}
    small = tuple(weights[n] for n in SMALL)
    loss, (g_big, g_ada, g_small, grad_x) = jax.value_and_grad(local_loss, argnums=(0, 1, 2, 3))(
        big, w_ada, small, x)
    loss = lax.psum(loss, ("x", "y", "c"))

    grads = dict(g_big)
    grads["w_ada"] = g_ada
    g_small_sum = sum_blocks(all_gather(_pack(g_small, F32), name="small_grads_ag"), name="small_grads_sum")
    grads.update(zip(SMALL, _unpack(g_small_sum, small_shapes)))

    delta, new_m, new_v = {}, {}, {}
    for n in ("w_ada",) + BIG:
        delta[n], new_m[n], new_v[n] = adamw(weights[n], grads[n], moments_m[n], moments_v[n], name="adamw_" + n)
    d_, m_, v_ = adamw(_pack([weights[n] for n in SMALL], F32)[None], g_small_sum[None],
                       _pack([moments_m[n] for n in SMALL], F32)[None], _pack([moments_v[n] for n in SMALL], F32)[None],
                       name="adamw_small")
    for tgt, packed in ((delta, d_), (new_m, m_), (new_v, v_)):
        tgt.update(zip(SMALL, _unpack(packed[0], small_shapes)))

    return (loss, grad_x, *[grads[n] for n in WEIGHTS], *[delta[n] for n in WEIGHTS],
            *[new_m[n] for n in WEIGHTS], *[new_v[n] for n in WEIGHTS])
```

```python
import functools
import math

import jax
import jax.numpy as jnp
import numpy as np
from jax import lax
from jax.experimental import pallas as pl
from jax.experimental.pallas import tpu as pltpu

F32 = jnp.float32
BF16 = jnp.bfloat16
N_DEV = 8
EPS = 1e-6
ROPE_THETA = 10000.0
GRID_W = 64

GQA_H, GQA_KV, HEAD = 6, 2, 128
MLA_H, MLA_QL, MLA_KVL, MLA_NOPE, MLA_ROPE, MLA_V = 4, 512, 256, 128, 64, 128
MLA_DK = 256
SSD_H, SSD_P, SSD_G, SSD_N, SSD_K, CHUNK = 12, 64, 2, 128, 5, 128
SSD_INNER = SSD_H * SSD_P
SSD_CONV_DIM = SSD_INNER + 2 * SSD_G * SSD_N
N_PAIR = SSD_H // 2
LANES = 128
IN_SPLITS = (768, 256, 256, 512, 256, 64, 768, 1280, 24)
IN_COLS = sum(IN_SPLITS)
SEG_ORDER = (0, 1, 2, 3, 4, 6, 7, 5, 8)
SEG_PAD = {5: 256, 8: 128}
SEG_LEAD = {5: 128}
IN_WIDTH = 4608

ADAM_LR, ADAM_B1, ADAM_B2, ADAM_EPS, ADAM_WD, ADAM_STEP = 0.001, 0.9, 0.999, 1e-08, 0.01, 10
VMEM_LIMIT = 56 * 1024 * 1024
MM_TM, MM_TN, MM_TK = 1024, 1408, 2048


def _cp(**kw):
    return pltpu.CompilerParams(vmem_limit_bytes=VMEM_LIMIT, **kw)


def _tile(dim, cap, mult=128):
    if dim <= cap:
        return dim
    best = None
    t = mult
    while t <= cap:
        if dim % t == 0:
            best = t
        t += mult
    assert best is not None, (dim, cap)
    return best


def _sigmoid(x):
    return 1.0 / (1.0 + jnp.exp(-x))


def _dot(a, b, dims):
    return lax.dot_general(a, b, (dims, ((), ())), preferred_element_type=F32)


NN = ((1,), (0,))
NT = ((1,), (1,))
TN = ((0,), (0,))


def _dotf(a, b, dims=NN):
    return lax.dot_general(a, b, (dims, ((), ())), preferred_element_type=F32,
                           precision=lax.Precision.HIGHEST)


def _bf(x):
    return x.astype(BF16)


def mm(a, b, *, ta=False, tb=False, out_dtype=F32, a_silu=False, name, carry=(),
       a_halves=False, b_halves=False):
    if a_halves:
        assert not ta
        M, K = a.shape[1], 2 * a.shape[2]
    elif ta:
        K, M = a.shape
    else:
        M, K = a.shape
    if b_halves:
        assert not tb
        K2, N = b.shape[1], 2 * b.shape[2]
    elif tb:
        N, K2 = b.shape
    else:
        K2, N = b.shape
    assert K == K2, (a.shape, b.shape, ta, tb)
    tm = _tile(M, MM_TM)
    tn = _tile(N // 2, MM_TN) if b_halves else _tile(N, MM_TN)
    tk = _tile(K // 2, MM_TK) if a_halves else _tile(K, MM_TK)
    nk = K // tk
    dims = ((0 if ta else 1,), (1 if tb else 0,))

    def partial_product(a_ref, b_ref):
        av = a_ref[...]
        if a_silu:
            av = av.astype(F32)
            av = av * _sigmoid(av)
        return _dot(_bf(av), _bf(b_ref[...]), dims)

    def body_single(a_ref, b_ref, o_ref):
        o_ref[...] = partial_product(a_ref, b_ref).astype(o_ref.dtype)

    def body_acc(a_ref, b_ref, o_ref, acc_ref):
        k = pl.program_id(2)

        @pl.when(k == 0)
        def _():
            acc_ref[...] = partial_product(a_ref, b_ref)

        @pl.when(k > 0)
        def _():
            acc_ref[...] += partial_product(a_ref, b_ref)

        @pl.when(k == nk - 1)
        def _():
            o_ref[...] = acc_ref[...].astype(o_ref.dtype)

    body = body_single if nk == 1 else body_acc

    a_spec = (pl.BlockSpec((tk, tm), lambda i, j, k: (k, i)) if ta
              else pl.BlockSpec((tm, tk), lambda i, j, k: (i, k)))
    b_spec = (pl.BlockSpec((tn, tk), lambda i, j, k: (j, k)) if tb
              else pl.BlockSpec((tk, tn), lambda i, j, k: (k, j)))
    if a_halves:
        a_spec = pl.BlockSpec((None, tm, tk), lambda i, j, k: (k // (nk // 2), i, k % (nk // 2)))
    if b_halves:
        nnh = N // tn // 2
        b_spec = pl.BlockSpec((None, tk, tn), lambda i, j, k: (j // nnh, k, j % nnh))
    res = carry_call(
        body, name=name, grid=(M // tm, N // tn, nk),
        in_specs=[a_spec, b_spec],
        out_specs=[pl.BlockSpec((tm, tn), lambda i, j, k: (i, j))],
        out_shape=[jax.ShapeDtypeStruct((M, N), out_dtype)],
        scratch_shapes=[] if nk == 1 else [pltpu.VMEM((tm, tn), F32)],
        carry=carry, dims=("parallel", "parallel", "arbitrary"),
    )(a, b)
    return res if carry else res[0]


def _shard_prep(p, cols):
    if cols and p.shape[1] % LANES:
        p = jnp.pad(p, ((0, 0), (0, -p.shape[1] % LANES)))
    return p.astype(BF16)


def _full_post(full, cols):
    return full if cols else full.reshape(full.shape[0] * full.shape[1], full.shape[2])


def _rs_begin(g_full, cols, shard_shape, name):
    gg = g_full if cols else g_full.reshape(N_DEV, shard_shape[0], shard_shape[1])
    x, y, c = _position()
    got = comm_call(rs_pair_routine(gg, cols), name=name + "_pair")[0]
    return add_own(gg, got, jnp.reshape(c, (1,)).astype(jnp.int32), cols=cols, name=name + "_add")


def _rs_end(h, got, cols, shard_shape, name):
    x, y, c = _position()
    d = sum_chips(h, got, jnp.reshape(2 * x + y, (1,)).astype(jnp.int32), name=name + "_sum")
    return d[:, :shard_shape[1]] if cols else d


def linear(a, w, *, out_dtype, name, a_silu=False, carry=()):
    kinds = [(cols, s.shape, tag) for s, cols, tag in carry]

    def run(a, w, *shards):
        routines = [ag_routine(_shard_prep(s, cols), cols) for s, (cols, _, _) in zip(shards, kinds)]
        res = mm(a, w, out_dtype=out_dtype, a_silu=a_silu, name=name + "_fwd", carry=routines)
        if not routines:
            return (res,)
        return (res[0], *[_full_post(fu, cols) for fu, (cols, _, _) in zip(res[1:], kinds)])

    @jax.custom_vjp
    def f(a, w, *shards):
        return run(a, w, *shards)

    def fwd(a, w, *shards):
        return run(a, w, *shards), (a, w)

    def bwd(res, cts):
        a, w = res
        g = cts[0]
        hs = [_rs_begin(gf, cols, shp, name + "_" + tag) for gf, (cols, shp, tag) in zip(cts[1:], kinds)]
        routines = [rs_chip_routine(h) for h in hs]
        if a_silu:
            assert not routines
            da = jnp.zeros_like(a)
            gots = []
        else:
            r = mm(g, w, tb=True, out_dtype=a.dtype, name=name + "_da", carry=routines)
            da, gots = (r[0], r[1:]) if routines else (r, [])
        dw = mm(a, g, ta=True, out_dtype=w.dtype, a_silu=a_silu, name=name + "_dw")
        dsh = [_rs_end(h, got, cols, shp, name + "_" + tag)
               for h, got, (cols, shp, tag) in zip(hs, gots, kinds)]
        return (da, dw, *dsh)

    f.defvjp(fwd, bwd)
    out = f(a, w, *[s for s, _, _ in carry])
    return (out[0], list(out[1:])) if carry else out[0]


def swiglu_up(h, w_gu, *, name, carry=()):
    T, D = h.shape
    F = w_gu.shape[1] // 2
    assert D <= MM_TK
    tm, tn = _tile(T, MM_TM), _tile(F, 512)
    nf = F // tn

    def body(a_ref, bg_ref, bu_ref, gu_ref, act_ref):
        av = a_ref[...]
        g = _dot(av, bg_ref[...], NN)
        u = _dot(av, bu_ref[...], NN)
        gu_ref[0] = g.astype(gu_ref.dtype)
        gu_ref[1] = u.astype(gu_ref.dtype)
        act_ref[...] = (g * _sigmoid(g) * u).astype(act_ref.dtype)

    return carry_call(
        body, name=name, grid=(T // tm, nf),
        in_specs=[pl.BlockSpec((tm, D), lambda i, j: (i, 0)), pl.BlockSpec((D, tn), lambda i, j: (0, j)),
                  pl.BlockSpec((D, tn), lambda i, j: (0, j + nf))],
        out_specs=[pl.BlockSpec((2, tm, tn), lambda i, j: (0, i, j)), pl.BlockSpec((tm, tn), lambda i, j: (i, j))],
        out_shape=[jax.ShapeDtypeStruct((2, T, F), BF16), jax.ShapeDtypeStruct((T, F), BF16)],
        carry=carry,
    )(h, w_gu, w_gu)


def swiglu_down_bwd(g, w_dn, gu, *, name, carry=()):
    T, D = g.shape
    F = w_dn.shape[0]
    assert D <= MM_TK
    tm, tn = _tile(T, MM_TM), _tile(F, 512)

    def body(a_ref, b_ref, gu_ref, o_ref):
        dact = _dot(_bf(a_ref[...]), b_ref[...], NT)
        gv = gu_ref[0].astype(F32)
        uv = gu_ref[1].astype(F32)
        sg = _sigmoid(gv)
        o_ref[0] = (dact * uv * (sg * (1.0 + gv * (1.0 - sg)))).astype(o_ref.dtype)
        o_ref[1] = (dact * gv * sg).astype(o_ref.dtype)

    half = pl.BlockSpec((2, tm, tn), lambda i, j: (0, i, j))
    return carry_call(
        body, name=name, grid=(T // tm, F // tn),
        in_specs=[pl.BlockSpec((tm, D), lambda i, j: (i, 0)), pl.BlockSpec((tn, D), lambda i, j: (j, 0)), half],
        out_specs=[half], out_shape=[jax.ShapeDtypeStruct((2, T, F), BF16)], carry=carry,
    )(g, w_dn, gu)


def ffn(h, w_gu, w_dn_shard, *, name, carry=()):
    kinds = [(cols, s.shape, tag) for s, cols, tag in carry]
    dn_shape = w_dn_shard.shape

    def run(h, w_gu, w_dn_shard, *shards):
        gu, act, w_dn = swiglu_up(h, w_gu, name=name + "_up", carry=[ag_routine(_shard_prep(w_dn_shard, False))])
        w_dn = _full_post(w_dn, False)
        routines = [ag_routine(_shard_prep(s, cols), cols) for s, (cols, _, _) in zip(shards, kinds)]
        res = mm(act, w_dn, out_dtype=F32, name=name + "_down", carry=routines)
        y, fulls = (res[0], res[1:]) if routines else (res, [])
        return (y, *[_full_post(fu, cols) for fu, (cols, _, _) in zip(fulls, kinds)]), (h, w_gu, w_dn, gu, act)

    @jax.custom_vjp
    def f(h, w_gu, w_dn_shard, *shards):
        return run(h, w_gu, w_dn_shard, *shards)[0]

    def fwd(h, w_gu, w_dn_shard, *shards):
        return run(h, w_gu, w_dn_shard, *shards)

    def bwd(res, cts):
        h, w_gu, w_dn, gu, act = res
        g = cts[0]
        hs = [_rs_begin(gf, cols, shp, name + "_" + tag) for gf, (cols, shp, tag) in zip(cts[1:], kinds)]
        dgu, *gots = swiglu_down_bwd(g, w_dn, gu, name=name + "_down_da", carry=[rs_chip_routine(x) for x in hs])
        dw_dn = mm(act, g, ta=True, out_dtype=w_dn.dtype, name=name + "_down_dw")
        h_dn = _rs_begin(dw_dn, False, dn_shape, name + "_w_down")
        dh, got_dn = mm(dgu, w_gu, tb=True, a_halves=True, out_dtype=h.dtype, name=name + "_up_da",
                        carry=[rs_chip_routine(h_dn)])
        dw_gu = mm(h, dgu, ta=True, b_halves=True, out_dtype=w_gu.dtype, name=name + "_up_dw")
        d_dn = _rs_end(h_dn, got_dn, False, dn_shape, name + "_w_down")
        dsh = [_rs_end(x, got, cols, shp, name + "_" + tag) for x, got, (cols, shp, tag) in zip(hs, gots, kinds)]
        return (dh, dw_gu, d_dn, *dsh)

    f.defvjp(fwd, bwd)
    out = f(h, w_gu, w_dn_shard, *[s for s, _, _ in carry])
    return out[0], list(out[1:])


def rmsmod(x, g, scale, shift, *, name):
    B, S, D = x.shape
    ts = _tile(S, 256, 8)
    row = pl.BlockSpec((1, ts, D), lambda b, j: (b, j, 0))
    per_b = pl.BlockSpec((1, 1, D), lambda b, j: (b, 0, 0))
    gspec = pl.BlockSpec((1, D), lambda b, j: (0, 0))

    def fwd_call(x, g, scale, shift):
        def body(x_ref, g_ref, sc_ref, sh_ref, o_ref):
            xv = x_ref[0]
            r = lax.rsqrt(jnp.mean(xv * xv, axis=-1, keepdims=True) + EPS)
            y = xv * r * g_ref[...]
            o_ref[0] = (y * (1.0 + sc_ref[0]) + sh_ref[0]).astype(o_ref.dtype)

        return pl.pallas_call(
            body, name=name + "_fwd", grid=(B, S // ts),
            in_specs=[row, gspec, per_b, per_b], out_specs=row,
            out_shape=jax.ShapeDtypeStruct((B, S, D), BF16), compiler_params=_cp(),
        )(x, g, scale, shift)

    def bwd_call(x, g, scale, dh):
        def body(x_ref, g_ref, sc_ref, dh_ref, dx_ref, dg_ref, dsc_ref, dsh_ref):
            j = pl.program_id(1)
            xv = x_ref[0]
            dh = dh_ref[0].astype(F32)
            r = lax.rsqrt(jnp.mean(xv * xv, axis=-1, keepdims=True) + EPS)
            xn = xv * r
            gv = g_ref[...]
            dy = dh * (1.0 + sc_ref[0])
            dxn = dy * gv
            dx_ref[0] = r * (dxn - xn * jnp.mean(dxn * xn, axis=-1, keepdims=True))

            @pl.when(j == 0)
            def _():
                dg_ref[...] = jnp.zeros_like(dg_ref)
                dsc_ref[...] = jnp.zeros_like(dsc_ref)
                dsh_ref[...] = jnp.zeros_like(dsh_ref)

            dg_ref[0] += jnp.sum(dy * xn, axis=0, keepdims=True)
            dsc_ref[0] += jnp.sum(dh * xn * gv, axis=0, keepdims=True)
            dsh_ref[0] += jnp.sum(dh, axis=0, keepdims=True)

        vec = jax.ShapeDtypeStruct((B, 1, D), F32)
        return pl.pallas_call(
            body, name=name + "_bwd", grid=(B, S // ts),
            in_specs=[row, gspec, per_b, row], out_specs=[row, per_b, per_b, per_b],
            out_shape=[jax.ShapeDtypeStruct((B, S, D), F32), vec, vec, vec], compiler_params=_cp(),
        )(x, g, scale, dh)

    @jax.custom_vjp
    def f(x, g, scale, shift):
        return fwd_call(x, g, scale, shift)

    def fwd(x, g, scale, shift):
        return fwd_call(x, g, scale, shift), (x, g, scale)

    def bwd(res, dh):
        x, g, scale = res
        dx, dg, dsc, dsh = bwd_call(x, g, scale, dh)
        return dx, jnp.sum(dg, axis=0), dsc, dsh

    f.defvjp(fwd, bwd)
    return f(x, g, scale, shift)


def group_rmsnorm(x, g, *, gs, out_dtype, name):
    T, W = x.shape
    ng = W // gs
    tr = _tile(T, 512, 8)
    row = pl.BlockSpec((tr, W), lambda i: (i, 0))
    gspec = pl.BlockSpec((1, W), lambda i: (0, 0))

    def fwd_call(x, g):
        def body(x_ref, g_ref, o_ref):
            for i in range(ng):
                sl = slice(i * gs, (i + 1) * gs)
                xv = x_ref[:, sl]
                r = lax.rsqrt(jnp.mean(xv * xv, axis=-1, keepdims=True) + EPS)
                o_ref[:, sl] = (xv * r * g_ref[:, sl]).astype(o_ref.dtype)

        return pl.pallas_call(
            body, name=name + "_fwd", grid=(T // tr,), in_specs=[row, gspec], out_specs=row,
            out_shape=jax.ShapeDtypeStruct((T, W), out_dtype), compiler_params=_cp(),
        )(x, g)

    def bwd_call(x, g, dy):
        def body(x_ref, g_ref, dy_ref, dx_ref, dg_ref):
            @pl.when(pl.program_id(0) == 0)
            def _():
                dg_ref[...] = jnp.zeros_like(dg_ref)

            for i in range(ng):
                sl = slice(i * gs, (i + 1) * gs)
                xv = x_ref[:, sl]
                dyv = dy_ref[:, sl].astype(F32)
                r = lax.rsqrt(jnp.mean(xv * xv, axis=-1, keepdims=True) + EPS)
                xn = xv * r
                dxn = dyv * g_ref[:, sl]
                dx_ref[:, sl] = r * (dxn - xn * jnp.mean(dxn * xn, axis=-1, keepdims=True))
                dg_ref[:, sl] += jnp.sum(dyv * xn, axis=0, keepdims=True)

        return pl.pallas_call(
            body, name=name + "_bwd", grid=(T // tr,), in_specs=[row, gspec, row],
            out_specs=[row, gspec],
            out_shape=[jax.ShapeDtypeStruct((T, W), F32), jax.ShapeDtypeStruct((1, W), F32)],
            compiler_params=_cp(),
        )(x, g, dy)

    @jax.custom_vjp
    def f(x, g):
        return fwd_call(x, g)

    def fwd(x, g):
        return fwd_call(x, g), (x, g)

    def bwd(res, dy):
        return bwd_call(res[0], res[1], dy)

    f.defvjp(fwd, bwd)
    return f(x, g)


def rope_tables(seq_len, rot_dim, width, slot=0, lead=0):
    rows = seq_len // GRID_W
    row_idx = jnp.repeat(jnp.arange(rows), GRID_W).astype(F32)
    col_idx = jnp.tile(jnp.arange(GRID_W), rows).astype(F32)
    axis_dim = rot_dim // 2
    inv_freq = jnp.power(ROPE_THETA, -jnp.arange(0, axis_dim, 2, dtype=F32) / axis_dim)
    ang_r = row_idx[:, None] * inv_freq[None, :]
    ang_c = col_idx[:, None] * inv_freq[None, :]
    cos = jnp.concatenate([jnp.cos(ang_r), jnp.cos(ang_r), jnp.cos(ang_c), jnp.cos(ang_c)], axis=-1)
    sin = jnp.concatenate([-jnp.sin(ang_r), jnp.sin(ang_r), -jnp.sin(ang_c), jnp.sin(ang_c)], axis=-1)
    if slot:
        ones = jnp.ones((seq_len, 1), F32)
        cos = jnp.concatenate([ones * jnp.ones((1, lead)), cos, ones * jnp.ones((1, slot - lead - rot_dim))], axis=-1)
        sin = jnp.concatenate([ones * jnp.zeros((1, lead)), sin, ones * jnp.zeros((1, slot - lead - rot_dim))], axis=-1)
        rot_dim = slot
    reps = width // rot_dim
    return jnp.tile(cos, (1, reps)), jnp.tile(sin, (1, reps))


def rope(x, cos, sin, *, d, name):
    B, S, W = x.shape
    ts = _tile(S, 512, 8)
    row = pl.BlockSpec((1, ts, W), lambda b, j: (b, j, 0))
    tab = pl.BlockSpec((ts, W), lambda b, j: (j, 0))

    def call(x, inverse, nm):
        def body(x_ref, c_ref, s_ref, o_ref):
            xv = x_ref[0]
            lane = lax.broadcasted_iota(jnp.int32, xv.shape, 1)
            first = (lane // d) % 2 == 0

            def swap(v):
                return jnp.where(first, pltpu.roll(v, W - d, 1), pltpu.roll(v, d, 1))

            if inverse:
                o_ref[0] = xv * c_ref[...] + swap(xv * s_ref[...])
            else:
                o_ref[0] = xv * c_ref[...] + swap(xv) * s_ref[...]

        return pl.pallas_call(
            body, name=nm, grid=(B, S // ts), in_specs=[row, tab, tab], out_specs=row,
            out_shape=jax.ShapeDtypeStruct((B, S, W), F32), compiler_params=_cp(),
        )(x, cos, sin)

    @jax.custom_vjp
    def f(x):
        return call(x, False, name + "_fwd")

    def fwd(x):
        return call(x, False, name + "_fwd"), None

    def bwd(_, g):
        return (call(g, True, name + "_bwd"),)

    f.defvjp(fwd, bwd)
    return f(x)


def attention(q, k, v, *, H, Hkv, dk, dv, scale, name, carry=()):
    B, S, _ = q.shape
    rep = H // Hkv
    tq = _tile(S, 256, 8)
    kinds = [(cols, s.shape, tag) for s, cols, tag in carry]

    def fwd_call(q, k, v, routines=()):
        def body(q_ref, k_ref, v_ref, o_ref, lse_ref):
            s = _dot(_bf(q_ref[0]), _bf(k_ref[0]), NT) * scale
            m = jnp.max(s, axis=-1, keepdims=True)
            p = jnp.exp(s - m)
            l = jnp.sum(p, axis=-1, keepdims=True)
            o_ref[0] = _dot(_bf(p), _bf(v_ref[0]), NN) / l
            lse_ref[0, 0] = m + jnp.log(l)

        return carry_call(
            body, name=name + "_fwd", grid=(B, H, S // tq),
            in_specs=[pl.BlockSpec((1, tq, dk), lambda b, h, i: (b, i, h)),
                      pl.BlockSpec((1, S, dk), lambda b, h, i: (b, 0, h // rep)),
                      pl.BlockSpec((1, S, dv), lambda b, h, i: (b, 0, h // rep))],
            out_specs=[pl.BlockSpec((1, tq, dv), lambda b, h, i: (b, i, h)),
                       pl.BlockSpec((1, 1, tq, 1), lambda b, h, i: (b, h, i, 0))],
            out_shape=[jax.ShapeDtypeStruct((B, S, H * dv), F32),
                       jax.ShapeDtypeStruct((B, H, S, 1), F32)],
            carry=routines,
        )(q, k, v)

    def bwd_call(q, k, v, o, do, lse, routines=()):
        def body(q_ref, k_ref, v_ref, o_ref, do_ref, lse_ref, dq_ref, dk_ref, dv_ref):
            @pl.when((pl.program_id(2) == 0) & (pl.program_id(3) == 0))
            def _():
                dk_ref[...] = jnp.zeros_like(dk_ref)
                dv_ref[...] = jnp.zeros_like(dv_ref)

            qb = _bf(q_ref[0])
            kb = _bf(k_ref[0])
            dov = do_ref[0]
            dob = _bf(dov)
            s = _dot(qb, kb, NT) * scale
            p = jnp.exp(s - lse_ref[0, 0])
            delta = jnp.sum(dov * o_ref[0], axis=-1, keepdims=True)
            dp = _dot(dob, _bf(v_ref[0]), NT)
            dsb = _bf(p * (dp - delta))
            dq_ref[0] = _dot(dsb, kb, NN) * scale
            dk_ref[0] += _dot(dsb, qb, TN) * scale
            dv_ref[0] += _dot(_bf(p), dob, TN)

        qs = pl.BlockSpec((1, tq, dk), lambda b, g, r, i: (b, i, g * rep + r))
        os_ = pl.BlockSpec((1, tq, dv), lambda b, g, r, i: (b, i, g * rep + r))
        ks = pl.BlockSpec((1, S, dk), lambda b, g, r, i: (b, 0, g))
        vs = pl.BlockSpec((1, S, dv), lambda b, g, r, i: (b, 0, g))
        col = pl.BlockSpec((1, 1, tq, 1), lambda b, g, r, i: (b, g * rep + r, i, 0))
        return carry_call(
            body, name=name + "_bwd", grid=(B, Hkv, rep, S // tq),
            in_specs=[qs, ks, vs, os_, os_, col], out_specs=[qs, ks, vs],
            out_shape=[jax.ShapeDtypeStruct(q.shape, F32), jax.ShapeDtypeStruct(k.shape, F32),
                       jax.ShapeDtypeStruct(v.shape, F32)],
            carry=routines,
        )(q, k, v, o, do, lse)

    def run(q, k, v, *shards):
        routines = [ag_routine(_shard_prep(s, cols), cols) for s, (cols, _, _) in zip(shards, kinds)]
        o, lse, *fulls = fwd_call(q, k, v, routines)
        return (o, *[_full_post(fu, cols) for fu, (cols, _, _) in zip(fulls, kinds)]), lse

    @jax.custom_vjp
    def f(q, k, v, *shards):
        return run(q, k, v, *shards)[0]

    def fwd(q, k, v, *shards):
        outs, lse = run(q, k, v, *shards)
        return outs, (q, k, v, outs[0], lse)

    def bwd(res, cts):
        q, k, v, o, lse = res
        hs = [_rs_begin(gf, cols, shp, name + "_" + tag) for gf, (cols, shp, tag) in zip(cts[1:], kinds)]
        dq, dk_, dv_, *gots = bwd_call(q, k, v, o, cts[0], lse, [rs_chip_routine(h) for h in hs])
        dsh = [_rs_end(h, got, cols, shp, name + "_" + tag)
               for h, got, (cols, shp, tag) in zip(hs, gots, kinds)]
        return (dq, dk_, dv_, *dsh)

    f.defvjp(fwd, bwd)
    out = f(q, k, v, *[s for s, _, _ in carry])
    return (out[0], list(out[1:])) if carry else out[0]


def conv_silu(x, w, b, *, name):
    B, S, C = x.shape
    tc = _tile(C, 256)
    pad = SSD_K // 2
    xs = pl.BlockSpec((1, S, tc), lambda bi, j: (bi, 0, j))
    ws = pl.BlockSpec((SSD_K, tc), lambda bi, j: (0, j))
    bs = pl.BlockSpec((1, tc), lambda bi, j: (0, j))

    def shifted(v, off):
        if off == 0:
            return v
        t = lax.broadcasted_iota(jnp.int32, v.shape, 0)
        r = pltpu.roll(v, (-off) % S, 0)
        return jnp.where((t + off >= 0) & (t + off < S), r, 0.0)

    def pre_act(xv, wv, bv):
        u = jnp.zeros_like(xv) + bv
        for k in range(SSD_K):
            u = u + wv[k:k + 1, :] * shifted(xv, k - pad)
        return u

    def fwd_call(x, w, b):
        def body(x_ref, w_ref, b_ref, o_ref):
            u = pre_act(x_ref[0], w_ref[...], b_ref[...])
            o_ref[0] = u * _sigmoid(u)

        return pl.pallas_call(
            body, name=name + "_fwd", grid=(B, C // tc), in_specs=[xs, ws, bs], out_specs=xs,
            out_shape=jax.ShapeDtypeStruct((B, S, C), F32), compiler_params=_cp(),
        )(x, w, b)

    def bwd_call(x, w, b, dy):
        def body(x_ref, w_ref, b_ref, dy_ref, dx_ref, dw_ref):
            xv = x_ref[0]
            wv = w_ref[...]
            u = pre_act(xv, wv, b_ref[...])
            sg = _sigmoid(u)
            du = dy_ref[0] * (sg * (1.0 + u * (1.0 - sg)))
            dx = jnp.zeros_like(xv)
            for k in range(SSD_K):
                dx = dx + wv[k:k + 1, :] * shifted(du, pad - k)
                dw_ref[0, k:k + 1, :] = jnp.sum(du * shifted(xv, k - pad), axis=0, keepdims=True)
            dw_ref[0, SSD_K:SSD_K + 1, :] = jnp.sum(du, axis=0, keepdims=True)
            dw_ref[0, SSD_K + 1:8, :] = jnp.zeros((8 - SSD_K - 1, tc), F32)
            dx_ref[0] = dx

        return pl.pallas_call(
            body, name=name + "_bwd", grid=(B, C // tc), in_specs=[xs, ws, bs, xs],
            out_specs=[xs, pl.BlockSpec((1, 8, tc), lambda bi, j: (bi, 0, j))],
            out_shape=[jax.ShapeDtypeStruct((B, S, C), F32), jax.ShapeDtypeStruct((B, 8, C), F32)],
            compiler_params=_cp(),
        )(x, w, b, dy)

    @jax.custom_vjp
    def f(x, w, b):
        return fwd_call(x, w, b)

    def fwd(x, w, b):
        return fwd_call(x, w, b), (x, w, b)

    def bwd(res, dy):
        x, w, b = res
        dx, dwb = bwd_call(x, w, b, dy)
        dwb = jnp.sum(dwb, axis=0)
        return dx, dwb[:SSD_K], dwb[SSD_K:SSD_K + 1]

    f.defvjp(fwd, bwd)
    return f(x, w, b)


def _softplus(x):
    return jnp.maximum(x, 0.0) + jnp.log1p(jnp.exp(-jnp.abs(x)))


def _ssd_prep(raw, raw_t, brow, arow, bcol, acol, rev):
    li = lax.broadcasted_iota(jnp.int32, (CHUNK, CHUNK), 0)
    ki = lax.broadcasted_iota(jnp.int32, (CHUNK, CHUNK), 1)
    later = (li <= ki) if rev else (li >= ki)
    dt = _softplus(raw + brow)
    a = dt * arow
    cs = _dotf(later.astype(F32), a)
    tot = jnp.sum(a, axis=0, keepdims=True)
    a_t = _softplus(raw_t + bcol) * acol
    earlier = (li >= ki) if rev else (li <= ki)
    cs_t = _dotf(a_t, earlier.astype(F32))
    return dt, a, cs, tot, cs_t, later


def _lane_pick(mat, j):
    lane = lax.broadcasted_iota(jnp.int32, mat.shape, 1)
    return jnp.sum(jnp.where(lane == j, mat, 0.0), axis=1, keepdims=True)


def _head_sum(t, first):
    s0 = jnp.sum(jnp.where(first, t, 0.0), axis=1, keepdims=True)
    s1 = jnp.sum(jnp.where(first, 0.0, t), axis=1, keepdims=True)
    return s0, s1


def ssd_scan(xbc, raw, brow, arow, *, rev, name):
    B, S, _ = xbc.shape
    NC = S // CHUNK
    off = SSD_H if rev else 0
    n_dt = 2 * SSD_H

    def chunk_of(c):
        return (NC - 1 - c) if rev else c

    def specs(cmap):
        return dict(
            x=pl.BlockSpec((1, CHUNK, SSD_INNER), lambda b, c: (b, cmap(c), 0)),
            bm=pl.BlockSpec((1, CHUNK, 2 * SSD_N), lambda b, c: (b, cmap(c), SSD_INNER // (2 * SSD_N))),
            cm=pl.BlockSpec((1, CHUNK, 2 * SSD_N), lambda b, c: (b, cmap(c), SSD_INNER // (2 * SSD_N) + 1)),
            raw=pl.BlockSpec((1, CHUNK, LANES), lambda b, c: (b, cmap(c), 0)),
            raw_t=pl.BlockSpec((1, n_dt, CHUNK), lambda b, c: (b, 0, cmap(c))),
            row=pl.BlockSpec((1, LANES), lambda b, c: (0, 0)),
            colv=pl.BlockSpec((n_dt, 1), lambda b, c: (0, 0)),
            hs=pl.BlockSpec((1, 1, N_PAIR, SSD_N, LANES), lambda b, c: (b, cmap(c), 0, 0, 0)),
        )

    def head_terms(prep, j, first_dummy=None):
        dt, a, cs, tot, cs_t, later = prep
        cs_c = _lane_pick(cs, j)
        cs_r = cs_t[j:j + 1, :]
        dt_c = _lane_pick(dt, j)
        tot_j = _lane_pick(tot, j)
        L = jnp.exp(jnp.where(later, cs_c - cs_r, -1e30))
        return cs_c, cs_r, dt_c, tot_j, L

    def fwd_call(xbc, raw, raw_t, brow, arow, bcol, acol):
        def body(x_ref, bm_ref, cm_ref, raw_ref, rawt_ref, brow_ref, arow_ref, bcol_ref, acol_ref,
                 y_ref, hs_ref, st_ref):
            @pl.when(pl.program_id(1) == 0)
            def _():
                st_ref[...] = jnp.zeros_like(st_ref)

            prep = _ssd_prep(raw_ref[0], rawt_ref[0], brow_ref[...], arow_ref[...],
                             bcol_ref[...], acol_ref[...], rev)
            lane = lax.broadcasted_iota(jnp.int32, (CHUNK, LANES), 1)
            first = lane < SSD_P
            for g in range(SSD_G):
                Bg = _bf(bm_ref[0, :, g * SSD_N:(g + 1) * SSD_N])
                Cg = _bf(cm_ref[0, :, g * SSD_N:(g + 1) * SSD_N])
                G = _dot(Cg, Bg, NT)
                for pp in range(N_PAIR // SSD_G):
                    pi = g * (N_PAIR // SSD_G) + pp
                    c0, _, d0, t0, L0 = head_terms(prep, off + 2 * pi)
                    c1, _, d1, t1, L1 = head_terms(prep, off + 2 * pi + 1)
                    xd = x_ref[0, :, pi * LANES:(pi + 1) * LANES] * jnp.where(first, d0, d1)
                    xdb = _bf(xd)
                    y = jnp.where(first, _dot(_bf(G * L0), xdb, NN), _dot(_bf(G * L1), xdb, NN))
                    dec = jnp.where(first, jnp.exp(t0 - c0), jnp.exp(t1 - c1))
                    h_prev = st_ref[pi]
                    hs_ref[0, 0, pi] = h_prev
                    y = y + _dot(Cg, _bf(h_prev), NN) * jnp.where(first, jnp.exp(c0), jnp.exp(c1))
                    y_ref[0, :, pi * LANES:(pi + 1) * LANES] = y
                    etot = jnp.where(first[:1], jnp.exp(t0), jnp.exp(t1))
                    st_ref[pi] = h_prev * etot + _dot(Bg, _bf(xd * dec), TN)

        sp = specs(chunk_of)
        return pl.pallas_call(
            body, name=name + "_fwd", grid=(B, NC),
            in_specs=[sp["x"], sp["bm"], sp["cm"], sp["raw"], sp["raw_t"], sp["row"], sp["row"],
                      sp["colv"], sp["colv"]],
            out_specs=[sp["x"], sp["hs"]],
            out_shape=[jax.ShapeDtypeStruct((B, S, SSD_INNER), F32),
                       jax.ShapeDtypeStruct((B, NC, N_PAIR, SSD_N, LANES), F32)],
            scratch_shapes=[pltpu.VMEM((N_PAIR, SSD_N, LANES), F32)],
            compiler_params=_cp(),
        )(xbc, xbc, xbc, raw, raw_t, brow, arow, bcol, acol)

    def bwd_call(xbc, raw, raw_t, brow, arow, bcol, acol, hs, dy):
        def body(x_ref, bm_ref, cm_ref, raw_ref, rawt_ref, brow_ref, arow_ref, bcol_ref, acol_ref,
                 hs_ref, dy_ref, dxbc_ref, draw_ref, da_ref, dst_ref):
            @pl.when(pl.program_id(1) == 0)
            def _():
                dst_ref[...] = jnp.zeros_like(dst_ref)
                da_ref[...] = jnp.zeros_like(da_ref)

            raw_v = raw_ref[0]
            prep = _ssd_prep(raw_v, rawt_ref[0], brow_ref[...], arow_ref[...],
                             bcol_ref[...], acol_ref[...], rev)
            dt, a, cs, tot, cs_t, later = prep
            li = lax.broadcasted_iota(jnp.int32, (CHUNK, CHUNK), 0)
            ki = lax.broadcasted_iota(jnp.int32, (CHUNK, CHUNK), 1)
            later_t = (li >= ki) if rev else (li <= ki)
            lane = lax.broadcasted_iota(jnp.int32, (CHUNK, LANES), 1)
            first = lane < SSD_P
            dcs_all = jnp.zeros((CHUNK, LANES), F32)
            ddt_all = jnp.zeros((CHUNK, LANES), F32)
            dtot_all = jnp.zeros((1, LANES), F32)
            for g in range(SSD_G):
                Bg = _bf(bm_ref[0, :, g * SSD_N:(g + 1) * SSD_N])
                Cg = _bf(cm_ref[0, :, g * SSD_N:(g + 1) * SSD_N])
                G = _dot(Cg, Bg, NT)
                Gt = _dot(Bg, Cg, NT)
                dG = jnp.zeros((CHUNK, CHUNK), F32)
                dB = jnp.zeros((CHUNK, SSD_N), F32)
                dC = jnp.zeros((CHUNK, SSD_N), F32)
                for pp in range(N_PAIR // SSD_G):
                    pi = g * (N_PAIR // SSD_G) + pp
                    j0, j1 = off + 2 * pi, off + 2 * pi + 1
                    c0, r0, d0, t0, L0 = head_terms(prep, j0)
                    c1, r1, d1, t1, L1 = head_terms(prep, j1)
                    Lt0 = jnp.exp(jnp.where(later_t, r0 - c0, -1e30))
                    Lt1 = jnp.exp(jnp.where(later_t, r1 - c1, -1e30))
                    xv = x_ref[0, :, pi * LANES:(pi + 1) * LANES]
                    dtp = jnp.where(first, d0, d1)
                    xd = xv * dtp
                    xdb = _bf(xd)
                    dyv = dy_ref[0, :, pi * LANES:(pi + 1) * LANES]
                    dyb = _bf(dyv)
                    dec = jnp.where(first, jnp.exp(t0 - c0), jnp.exp(t1 - c1))
                    ecs = jnp.where(first, jnp.exp(c0), jnp.exp(c1))
                    et0, et1 = jnp.exp(t0), jnp.exp(t1)
                    etot = jnp.where(first[:1], et0, et1)
                    h_prev = hs_ref[0, 0, pi]
                    hpb = _bf(h_prev)
                    dhn = dst_ref[pi]
                    dhb = _bf(dhn)
                    W0, W1 = G * L0, G * L1
                    Wt0, Wt1 = Gt * Lt0, Gt * Lt1
                    bdh = _dot(Bg, dhb, NN)
                    dxd = jnp.where(first, _dot(_bf(Wt0), dyb, NN), _dot(_bf(Wt1), dyb, NN)) + bdh * dec
                    dy0 = _bf(jnp.where(first, dyv, 0.0))
                    dy1 = _bf(jnp.where(first, 0.0, dyv))
                    Q0, Q1 = _dot(dy0, xdb, NT), _dot(dy1, xdb, NT)
                    Qt0, Qt1 = _dot(xdb, dy0, NT), _dot(xdb, dy1, NT)
                    dG = dG + Q0 * L0 + Q1 * L1
                    dcs0 = (jnp.sum(Q0 * W0, axis=1, keepdims=True)
                            - jnp.sum(Qt0 * Wt0, axis=1, keepdims=True))
                    dcs1 = (jnp.sum(Q1 * W1, axis=1, keepdims=True)
                            - jnp.sum(Qt1 * Wt1, axis=1, keepdims=True))
                    dye = dyv * ecs
                    dyeb = _bf(dye)
                    s0, s1 = _head_sum(dye * _dot(Cg, hpb, NN), first)
                    dcs0, dcs1 = dcs0 + s0, dcs1 + s1
                    dC = dC + _dot(dyeb, hpb, NT)
                    dB = dB + _dot(_bf(xd * dec), dhb, NT)
                    u0, u1 = _head_sum(xd * bdh * dec, first)
                    dcs0, dcs1 = dcs0 - u0, dcs1 - u1
                    w = jnp.sum(dhn * h_prev, axis=0, keepdims=True)
                    w0, w1 = _head_sum(w, first[:1])
                    dt0 = jnp.sum(u0, axis=0, keepdims=True) + et0 * w0
                    dt1 = jnp.sum(u1, axis=0, keepdims=True) + et1 * w1
                    dst_ref[pi] = _dot(Cg, dyeb, TN) + dhn * etot
                    q0, q1 = _head_sum(dxd * xv, first)
                    dxbc_ref[0, :, pi * LANES:(pi + 1) * LANES] = dxd * dtp
                    dcs_all = dcs_all + jnp.where(lane == j0, dcs0, 0.0) + jnp.where(lane == j1, dcs1, 0.0)
                    ddt_all = ddt_all + jnp.where(lane == j0, q0, 0.0) + jnp.where(lane == j1, q1, 0.0)
                    dtot_all = (dtot_all + jnp.where(lane[:1] == j0, dt0, 0.0)
                                + jnp.where(lane[:1] == j1, dt1, 0.0))
                dGb = _bf(dG)
                dC = dC + _dot(dGb, Bg, NN)
                dB = dB + _dot(dGb, Cg, TN)
                dxbc_ref[0, :, SSD_INNER + g * SSD_N:SSD_INNER + (g + 1) * SSD_N] = dB
                dxbc_ref[0, :, SSD_INNER + (SSD_G + g) * SSD_N:SSD_INNER + (SSD_G + g + 1) * SSD_N] = dC
            da = _dotf(later_t.astype(F32), dcs_all) + dtot_all
            ddt = ddt_all + da * arow_ref[...]
            da_ref[0] += jnp.sum(da * dt, axis=0, keepdims=True)
            draw_ref[0] = ddt * _sigmoid(raw_v + brow_ref[...])

        def rchunk(c):
            return c if rev else (NC - 1 - c)

        sp = specs(rchunk)
        full = pl.BlockSpec((1, CHUNK, SSD_CONV_DIM), lambda b, c: (b, rchunk(c), 0))
        return pl.pallas_call(
            body, name=name + "_bwd", grid=(B, NC),
            in_specs=[sp["x"], sp["bm"], sp["cm"], sp["raw"], sp["raw_t"], sp["row"], sp["row"],
                      sp["colv"], sp["colv"], sp["hs"], sp["x"]],
            out_specs=[full, sp["raw"], pl.BlockSpec((1, 1, LANES), lambda b, c: (b, 0, 0))],
            out_shape=[jax.ShapeDtypeStruct((B, S, SSD_CONV_DIM), F32),
                       jax.ShapeDtypeStruct((B, S, LANES), F32),
                       jax.ShapeDtypeStruct((B, 1, LANES), F32)],
            scratch_shapes=[pltpu.VMEM((N_PAIR, SSD_N, LANES), F32)],
            compiler_params=_cp(),
        )(xbc, xbc, xbc, raw, raw_t, brow, arow, bcol, acol, hs, dy)

    def aux(raw, brow, arow):
        raw_t = jnp.swapaxes(raw[:, :, :n_dt], 1, 2)
        return raw_t, brow[0, :n_dt][:, None], arow[0, :n_dt][:, None]

    @jax.custom_vjp
    def f(xbc, raw, brow, arow):
        raw_t, bcol, acol = aux(raw, brow, arow)
        return fwd_call(xbc, raw, raw_t, brow, arow, bcol, acol)[0]

    def fwd(xbc, raw, brow, arow):
        raw_t, bcol, acol = aux(raw, brow, arow)
        y, hs = fwd_call(xbc, raw, raw_t, brow, arow, bcol, acol)
        return y, (xbc, raw, brow, arow, hs)

    def bwd(res, dy):
        xbc, raw, brow, arow, hs = res
        raw_t, bcol, acol = aux(raw, brow, arow)
        dxbc, draw, da = bwd_call(xbc, raw, raw_t, brow, arow, bcol, acol, hs, dy)
        dbrow = jnp.sum(draw, axis=(0, 1))[None, :]
        return dxbc, draw, dbrow, jnp.sum(da, axis=0)

    f.defvjp(fwd, bwd)
    return f(xbc, raw, brow, arow)


def ssd_out(yf, yb, xbc, z, dsk, g, *, name):
    T, W = yf.shape
    gs = W // SSD_G
    tr = _tile(T, 512, 8)
    row = pl.BlockSpec((tr, W), lambda i: (i, 0))
    vec = pl.BlockSpec((1, W), lambda i: (0, 0))

    def normed(yv, gv):
        outs, rs = [], []
        for i in range(SSD_G):
            sl = slice(i * gs, (i + 1) * gs)
            r = lax.rsqrt(jnp.mean(yv[:, sl] * yv[:, sl], axis=-1, keepdims=True) + EPS)
            rs.append(r)
            outs.append(yv[:, sl] * r)
        return outs, rs

    def fwd_call(yf, yb, xbc, z, dsk, g):
        def body(yf_ref, yb_ref, xs_ref, z_ref, dsk_ref, g_ref, o_ref):
            zv = z_ref[...]
            yv = (yf_ref[...] + yb_ref[...] + xs_ref[...] * dsk_ref[...]) * (zv * _sigmoid(zv))
            outs, _ = normed(yv, g_ref[...])
            for i in range(SSD_G):
                sl = slice(i * gs, (i + 1) * gs)
                o_ref[:, sl] = (outs[i] * g_ref[:, sl]).astype(o_ref.dtype)

        return pl.pallas_call(
            body, name=name + "_fwd", grid=(T // tr,), in_specs=[row, row, row, row, vec, vec],
            out_specs=row, out_shape=jax.ShapeDtypeStruct((T, W), BF16), compiler_params=_cp(),
        )(yf, yb, xbc, z, dsk, g)

    def bwd_call(yf, yb, xbc, z, dsk, g, do):
        def body(yf_ref, yb_ref, xs_ref, z_ref, dsk_ref, g_ref, do_ref, dy_ref, dxs_ref, dz_ref,
                 ddsk_ref, dg_ref):
            @pl.when(pl.program_id(0) == 0)
            def _():
                ddsk_ref[...] = jnp.zeros_like(ddsk_ref)
                dg_ref[...] = jnp.zeros_like(dg_ref)

            zv = z_ref[...]
            sg = _sigmoid(zv)
            sz = zv * sg
            xs = xs_ref[...]
            pre = yf_ref[...] + yb_ref[...] + xs * dsk_ref[...]
            yv = pre * sz
            outs, rs = normed(yv, g_ref[...])
            for i in range(SSD_G):
                sl = slice(i * gs, (i + 1) * gs)
                dov = do_ref[:, sl].astype(F32)
                xn = outs[i]
                dxn = dov * g_ref[:, sl]
                dyv = rs[i] * (dxn - xn * jnp.mean(dxn * xn, axis=-1, keepdims=True))
                dg_ref[:, sl] += jnp.sum(dov * xn, axis=0, keepdims=True)
                dpre = dyv * sz[:, sl]
                dy_ref[:, sl] = dpre
                dxs_ref[:, sl] = dpre * dsk_ref[:, sl]
                ddsk_ref[:, sl] += jnp.sum(dpre * xs[:, sl], axis=0, keepdims=True)
                dz_ref[:, sl] = dyv * pre[:, sl] * (sg[:, sl] * (1.0 + zv[:, sl] * (1.0 - sg[:, sl])))

        o = jax.ShapeDtypeStruct((T, W), F32)
        v = jax.ShapeDtypeStruct((1, W), F32)
        return pl.pallas_call(
            body, name=name + "_bwd", grid=(T // tr,), in_specs=[row, row, row, row, vec, vec, row],
            out_specs=[row, row, row, vec, vec], out_shape=[o, o, o, v, v], compiler_params=_cp(),
        )(yf, yb, xbc, z, dsk, g, do)

    @jax.custom_vjp
    def f(yf, yb, xbc, z, dsk, g):
        return fwd_call(yf, yb, xbc, z, dsk, g)

    def fwd(yf, yb, xbc, z, dsk, g):
        return fwd_call(yf, yb, xbc, z, dsk, g), (yf, yb, xbc, z, dsk, g)

    def bwd(res, do):
        dy, dxs, dz, ddsk, dg = bwd_call(*res, do)
        dxbc = jnp.pad(dxs, ((0, 0), (0, res[2].shape[1] - W)))
        return dy, dy, dxbc, dz, ddsk, dg

    f.defvjp(fwd, bwd)
    return f(yf, yb, xbc, z, dsk, g)


def swiglu(gu, *, name):
    T, F2 = gu.shape
    Fh = F2 // 2
    tr, tf = _tile(T, 512, 8), _tile(Fh, 512)
    nf = Fh // tf
    gs = pl.BlockSpec((tr, tf), lambda i, j: (i, j))
    us = pl.BlockSpec((tr, tf), lambda i, j: (i, j + nf))

    def fwd_call(gu):
        def body(g_ref, u_ref, o_ref):
            gv = g_ref[...].astype(F32)
            o_ref[...] = (gv * _sigmoid(gv) * u_ref[...].astype(F32)).astype(o_ref.dtype)

        return pl.pallas_call(
            body, name=name + "_fwd", grid=(T // tr, nf), in_specs=[gs, us], out_specs=gs,
            out_shape=jax.ShapeDtypeStruct((T, Fh), BF16), compiler_params=_cp(),
        )(gu, gu)

    def bwd_call(gu, da):
        def body(g_ref, u_ref, da_ref, dgu_ref):
            j = pl.program_id(1)
            gv = g_ref[...].astype(F32)
            uv = u_ref[...].astype(F32)
            dav = da_ref[...].astype(F32)
            sg = _sigmoid(gv)

            @pl.when(j < nf)
            def _():
                dgu_ref[...] = (dav * uv * (sg * (1.0 + gv * (1.0 - sg)))).astype(dgu_ref.dtype)

            @pl.when(j >= nf)
            def _():
                dgu_ref[...] = (dav * gv * sg).astype(dgu_ref.dtype)

        gsel = pl.BlockSpec((tr, tf), lambda i, j: (i, j % nf))
        usel = pl.BlockSpec((tr, tf), lambda i, j: (i, j % nf + nf))
        return pl.pallas_call(
            body, name=name + "_bwd", grid=(T // tr, 2 * nf), in_specs=[gsel, usel, gsel],
            out_specs=pl.BlockSpec((tr, tf), lambda i, j: (i, j)),
            out_shape=jax.ShapeDtypeStruct((T, F2), BF16), compiler_params=_cp(),
        )(gu, gu, da)

    @jax.custom_vjp
    def f(gu):
        return fwd_call(gu)

    def fwd(gu):
        return fwd_call(gu), gu

    def bwd(gu, da):
        return (bwd_call(gu, da),)

    f.defvjp(fwd, bwd)
    return f(gu)


def gated_residual(x, gate, y, *, name):
    B, S, D = x.shape
    ts = _tile(S, 256, 8)
    row = pl.BlockSpec((1, ts, D), lambda b, j: (b, j, 0))
    per_b = pl.BlockSpec((1, 1, D), lambda b, j: (b, 0, 0))

    def fwd_call(x, gate, y):
        def body(x_ref, gt_ref, y_ref, o_ref):
            o_ref[0] = x_ref[0] + gt_ref[0] * y_ref[0]

        return pl.pallas_call(
            body, name=name + "_fwd", grid=(B, S // ts), in_specs=[row, per_b, row], out_specs=row,
            out_shape=jax.ShapeDtypeStruct((B, S, D), F32), compiler_params=_cp(),
        )(x, gate, y)

    def bwd_call(gate, y, g):
        def body(gt_ref, y_ref, g_ref, dy_ref, dgt_ref):
            @pl.when(pl.program_id(1) == 0)
            def _():
                dgt_ref[...] = jnp.zeros_like(dgt_ref)

            gv = g_ref[0]
            dy_ref[0] = gt_ref[0] * gv
            dgt_ref[0] += jnp.sum(gv * y_ref[0], axis=0, keepdims=True)

        return pl.pallas_call(
            body, name=name + "_bwd", grid=(B, S // ts), in_specs=[per_b, row, row],
            out_specs=[row, per_b],
            out_shape=[jax.ShapeDtypeStruct((B, S, D), F32), jax.ShapeDtypeStruct((B, 1, D), F32)],
            compiler_params=_cp(),
        )(gate, y, g)

    @jax.custom_vjp
    def f(x, gate, y):
        return fwd_call(x, gate, y)

    def fwd(x, gate, y):
        return fwd_call(x, gate, y), (gate, y)

    def bwd(res, g):
        dy, dgate = bwd_call(res[0], res[1], g)
        return g, dgate, dy

    f.defvjp(fwd, bwd)
    return f(x, gate, y)


def res_norm(x, gate, y, g, scale, shift, *, name):
    B, S, D = x.shape
    ts = _tile(S, 256, 8)
    row = pl.BlockSpec((1, ts, D), lambda b, j: (b, j, 0))
    per_b = pl.BlockSpec((1, 1, D), lambda b, j: (b, 0, 0))
    gspec = pl.BlockSpec((1, D), lambda b, j: (0, 0))

    def fwd_call(x, gate, y, g, scale, shift):
        def body(x_ref, gt_ref, y_ref, g_ref, sc_ref, sh_ref, xo_ref, h_ref):
            xv = x_ref[0] + gt_ref[0] * y_ref[0]
            xo_ref[0] = xv
            r = lax.rsqrt(jnp.mean(xv * xv, axis=-1, keepdims=True) + EPS)
            h_ref[0] = (xv * r * g_ref[...] * (1.0 + sc_ref[0]) + sh_ref[0]).astype(h_ref.dtype)

        return pl.pallas_call(
            body, name=name + "_fwd", grid=(B, S // ts),
            in_specs=[row, per_b, row, gspec, per_b, per_b], out_specs=[row, row],
            out_shape=[jax.ShapeDtypeStruct((B, S, D), F32), jax.ShapeDtypeStruct((B, S, D), BF16)],
            compiler_params=_cp(),
        )(x, gate, y, g, scale, shift)

    def bwd_call(xn, g, scale, gate, y, dh, dxn):
        def body(x_ref, g_ref, sc_ref, gt_ref, y_ref, dh_ref, dxn_ref,
                 dx_ref, dy_ref, dg_ref, dsc_ref, dsh_ref, dgt_ref):
            @pl.when(pl.program_id(1) == 0)
            def _():
                for ref in (dg_ref, dsc_ref, dsh_ref, dgt_ref):
                    ref[...] = jnp.zeros_like(ref)

            xv = x_ref[0]
            dh = dh_ref[0].astype(F32)
            gv = g_ref[...]
            r = lax.rsqrt(jnp.mean(xv * xv, axis=-1, keepdims=True) + EPS)
            xh = xv * r
            dyv = dh * (1.0 + sc_ref[0])
            dxh = dyv * gv
            dx = dxn_ref[0] + r * (dxh - xh * jnp.mean(dxh * xh, axis=-1, keepdims=True))
            dx_ref[0] = dx
            dy_ref[0] = gt_ref[0] * dx
            dgt_ref[0] += jnp.sum(dx * y_ref[0], axis=0, keepdims=True)
            dg_ref[0] += jnp.sum(dyv * xh, axis=0, keepdims=True)
            dsc_ref[0] += jnp.sum(dh * xh * gv, axis=0, keepdims=True)
            dsh_ref[0] += jnp.sum(dh, axis=0, keepdims=True)

        big = jax.ShapeDtypeStruct((B, S, D), F32)
        vec = jax.ShapeDtypeStruct((B, 1, D), F32)
        return pl.pallas_call(
            body, name=name + "_bwd", grid=(B, S // ts),
            in_specs=[row, gspec, per_b, per_b, row, row, row],
            out_specs=[row, row, per_b, per_b, per_b, per_b],
            out_shape=[big, big, vec, vec, vec, vec], compiler_params=_cp(),
        )(xn, g, scale, gate, y, dh, dxn)

    @jax.custom_vjp
    def f(x, gate, y, g, scale, shift):
        return tuple(fwd_call(x, gate, y, g, scale, shift))

    def fwd(x, gate, y, g, scale, shift):
        xn, h = fwd_call(x, gate, y, g, scale, shift)
        return (xn, h), (xn, g, scale, gate, y)

    def bwd(res, cts):
        xn, g, scale, gate, y = res
        dx, dy, dg, dsc, dsh, dgt = bwd_call(xn, g, scale, gate, y, cts[1], cts[0])
        return dx, dgt, dy, jnp.sum(dg, axis=0), dsc, dsh

    f.defvjp(fwd, bwd)
    return f(x, gate, y, g, scale, shift)


def final_loss(x, g, target, *, name):
    T, D = x.shape
    tr = _tile(T, 256, 8)
    row = pl.BlockSpec((tr, D), lambda i: (i, 0))
    vec = pl.BlockSpec((1, D), lambda i: (0, 0))

    def fwd_call(x, g, target):
        def body(x_ref, g_ref, t_ref, o_ref):
            @pl.when(pl.program_id(0) == 0)
            def _():
                o_ref[...] = jnp.zeros_like(o_ref)

            xv = x_ref[...]
            r = lax.rsqrt(jnp.mean(xv * xv, axis=-1, keepdims=True) + EPS)
            e = xv * r * g_ref[...] - t_ref[...]
            o_ref[...] += jnp.sum(e * e, axis=0, keepdims=True)

        part = pl.pallas_call(
            body, name=name + "_fwd", grid=(T // tr,), in_specs=[row, vec, row], out_specs=vec,
            out_shape=jax.ShapeDtypeStruct((1, D), F32), compiler_params=_cp(),
        )(x, g, target)
        return (0.5 / D) * jnp.sum(part)

    def bwd_call(x, g, target, ct):
        def body(x_ref, g_ref, t_ref, ct_ref, dx_ref, dg_ref):
            @pl.when(pl.program_id(0) == 0)
            def _():
                dg_ref[...] = jnp.zeros_like(dg_ref)

            xv = x_ref[...]
            gv = g_ref[...]
            r = lax.rsqrt(jnp.mean(xv * xv, axis=-1, keepdims=True) + EPS)
            xn = xv * r
            dy = (xn * gv - t_ref[...]) * (ct_ref[...] * (1.0 / D))
            dxn = dy * gv
            dx_ref[...] = r * (dxn - xn * jnp.mean(dxn * xn, axis=-1, keepdims=True))
            dg_ref[...] += jnp.sum(dy * xn, axis=0, keepdims=True)

        return pl.pallas_call(
            body, name=name + "_bwd", grid=(T // tr,),
            in_specs=[row, vec, row, pl.BlockSpec((1, 1), lambda i: (0, 0))], out_specs=[row, vec],
            out_shape=[jax.ShapeDtypeStruct((T, D), F32), jax.ShapeDtypeStruct((1, D), F32)],
            compiler_params=_cp(),
        )(x, g, target, ct)

    @jax.custom_vjp
    def f(x, g, target):
        return fwd_call(x, g, target)

    def fwd(x, g, target):
        return fwd_call(x, g, target), (x, g, target)

    def bwd(res, ct):
        x, g, target = res
        dx, dg = bwd_call(x, g, target, jnp.reshape(ct, (1, 1)).astype(F32))
        return dx, dg, jnp.zeros_like(target)

    f.defvjp(fwd, bwd)
    return f(x, g, target)


def adamw(w, g, m, v, *, name):
    L, R, C = w.shape
    tr = _tile(R, 512, 8)
    spec = pl.BlockSpec((1, tr, C), lambda l, i: (l, i, 0))
    c1 = 1.0 / (1.0 - ADAM_B1 ** ADAM_STEP)
    c2 = 1.0 / (1.0 - ADAM_B2 ** ADAM_STEP)

    def body(w_ref, g_ref, m_ref, v_ref, d_ref, nm_ref, nv_ref):
        gv = g_ref[...]
        nm = ADAM_B1 * m_ref[...] + (1.0 - ADAM_B1) * gv
        nv = ADAM_B2 * v_ref[...] + (1.0 - ADAM_B2) * (gv * gv)
        nm_ref[...] = nm
        nv_ref[...] = nv
        d_ref[...] = -ADAM_LR * ((nm * c1) / (jnp.sqrt(nv * c2) + ADAM_EPS) + ADAM_WD * w_ref[...])

    o = jax.ShapeDtypeStruct((L, R, C), F32)
    return pl.pallas_call(
        body, name=name, grid=(L, R // tr), in_specs=[spec] * 4, out_specs=[spec] * 3,
        out_shape=[o, o, o], compiler_params=_cp(),
    )(w, g, m, v)


def _position():
    x, y, c = lax.axis_index("x"), lax.axis_index("y"), lax.axis_index("c")
    return x, y, c


def ag_routine(shard, cols=False):
    R, C = shard.shape
    assert not cols or C % LANES == 0

    def parts(ins, outs, send_sems, recv_sems, local_sems):
        (x_ref,), (out_ref,) = ins, outs
        x, y, c = _position()
        me, sibling = (x, y, c), (x, y, 1 - c)
        chips = [(1 - x, y), (x, 1 - y), (1 - x, 1 - y)]

        def block(px, py, pc):
            idx = 4 * px + 2 * py + pc
            if cols:
                return out_ref.at[:, pl.ds(pl.multiple_of(idx * C, LANES), C)]
            return out_ref.at[idx]

        def copy(k, blk, to, src=None):
            return pltpu.make_async_remote_copy(
                src_ref=block(*blk) if src is None else src, dst_ref=block(*blk),
                send_sem=send_sems.at[k], recv_sem=recv_sems.at[k],
                device_id=to, device_id_type=pl.DeviceIdType.MESH)

        mine = pltpu.make_async_copy(x_ref, block(*me), local_sems.at[0])
        first = [copy(0, me, sibling, src=x_ref)]
        first += [copy(1 + j, me, (*chip, c), src=x_ref) for j, chip in enumerate(chips)]
        passed = [copy(4 + j, (*chip, c), sibling) for j, chip in enumerate(chips)]
        return me, sibling, c, chips, copy, mine, first, passed

    def start(*refs):
        me, sibling, c, chips, copy, mine, first, passed = parts(*refs)
        mine.start()
        for cp in first:
            cp.start()

    def finish(*refs):
        me, sibling, c, chips, copy, mine, first, passed = parts(*refs)
        for j, chip in enumerate(chips):
            copy(1 + j, (*chip, c), me).wait_recv()
            passed[j].start()
        copy(0, sibling, me).wait_recv()
        for j, chip in enumerate(chips):
            copy(4 + j, (*chip, 1 - c), me).wait_recv()
        for cp in first + passed:
            cp.wait_send()
        mine.wait()

    out = jax.ShapeDtypeStruct((R, N_DEV * C) if cols else (N_DEV, R, C), shard.dtype)
    return dict(ins=[shard], outs=[out], n_sem=7, n_local=1, start=start, finish=finish)


def all_gather(shard, *, name, cols=False):
    return comm_call(ag_routine(shard, cols), name=name)[0]


def carry_call(body, *, name, grid, in_specs, out_specs, out_shape, scratch_shapes=(), carry=(),
               dims=None):
    in_specs, out_specs, out_shape = list(in_specs), list(out_specs), list(out_shape)
    scratch_shapes = list(scratch_shapes)
    if not carry:
        call = pl.pallas_call(body, name=name, grid=grid, in_specs=in_specs, out_specs=out_specs,
                              out_shape=out_shape, scratch_shapes=scratch_shapes,
                              compiler_params=_cp(dimension_semantics=dims) if dims else _cp())
        return lambda *args: list(call(*args))
    n_in, n_out, n_scr = len(in_specs), len(out_specs), len(scratch_shapes)
    c_ins = [a for r in carry for a in r["ins"]]
    c_outs = [o for r in carry for o in r["outs"]]
    sems = []
    for r in carry:
        sems += [pltpu.SemaphoreType.DMA((r["n_sem"],)), pltpu.SemaphoreType.DMA((r["n_sem"],)),
                 pltpu.SemaphoreType.DMA((r["n_local"],))]

    def wrapped(*refs):
        refs = list(refs)
        ins, refs = refs[:n_in], refs[n_in:]
        cin, refs = refs[:len(c_ins)], refs[len(c_ins):]
        outs, refs = refs[:n_out], refs[n_out:]
        cout, refs = refs[:len(c_outs)], refs[len(c_outs):]
        scr, csem = refs[:n_scr], refs[n_scr:]
        ids = [pl.program_id(i) for i in range(len(grid))]
        first = functools.reduce(jnp.logical_and, [i == 0 for i in ids])
        last = functools.reduce(jnp.logical_and, [i == g - 1 for i, g in zip(ids, grid)])

        def each(which):
            io = oo = 0
            for j, r in enumerate(carry):
                r[which](cin[io:io + len(r["ins"])], cout[oo:oo + len(r["outs"])], *csem[3 * j:3 * j + 3])
                io += len(r["ins"])
                oo += len(r["outs"])

        @pl.when(first)
        def _():
            each("start")

        body(*ins, *outs, *scr)

        @pl.when(last)
        def _():
            each("finish")

    any_spec = pl.BlockSpec(memory_space=pl.ANY)
    call = pl.pallas_call(
        wrapped, name=name, grid=grid, in_specs=in_specs + [any_spec] * len(c_ins),
        out_specs=out_specs + [any_spec] * len(c_outs), out_shape=out_shape + c_outs,
        scratch_shapes=scratch_shapes + sems, compiler_params=_cp())
    return lambda *args: list(call(*args, *c_ins))


N_CHIP = 4


def rs_pair_routine(g, cols):
    if cols:
        R, C = g.shape[0], g.shape[1] // N_DEV
        assert C % LANES == 0
    else:
        _, R, C = g.shape

    def blk(ref, idx):
        if cols:
            return ref.at[:, pl.ds(pl.multiple_of(idx * C, LANES), C)]
        return ref.at[idx]

    def copies(ins, outs, send_sems, recv_sems, local_sems):
        (g_ref,), (got_ref,) = ins, outs
        x, y, c = _position()
        local, remote = [], []
        for q in range(N_CHIP):
            remote.append(pltpu.make_async_remote_copy(
                src_ref=blk(g_ref, 2 * q + 1 - c), dst_ref=got_ref.at[q],
                send_sem=send_sems.at[q], recv_sem=recv_sems.at[q],
                device_id=(x, y, 1 - c), device_id_type=pl.DeviceIdType.MESH))
        return local, remote

    def start(*refs):
        local, remote = copies(*refs)
        for cp in remote + local:
            cp.start()

    def finish(*refs):
        local, remote = copies(*refs)
        for cp in remote:
            cp.wait_recv()
        for cp in remote:
            cp.wait_send()
        for cp in local:
            cp.wait()

    o = jax.ShapeDtypeStruct((N_CHIP, R, C), g.dtype)
    return dict(ins=[g], outs=[o], n_sem=N_CHIP, n_local=1, start=start, finish=finish)


def rs_chip_routine(h):
    _, R, C = h.shape
    RELATIONS = ((0, 1), (1, 0), (1, 1))

    def copies(ins, outs, send_sems, recv_sems, local_sems):
        (h_ref,), (out_ref,) = ins, outs
        x, y, c = _position()
        local, remote = [], []
        for k, (fx, fy) in enumerate(RELATIONS):
            px = (1 - x) if fx else x
            py = (1 - y) if fy else y
            remote.append(pltpu.make_async_remote_copy(
                src_ref=h_ref.at[2 * px + py], dst_ref=out_ref.at[k],
                send_sem=send_sems.at[k], recv_sem=recv_sems.at[k],
                device_id=(px, py, c), device_id_type=pl.DeviceIdType.MESH))
        return local, remote

    def start(*refs):
        local, remote = copies(*refs)
        for cp in remote + local:
            cp.start()

    def finish(*refs):
        local, remote = copies(*refs)
        for cp in remote:
            cp.wait_recv()
        for cp in remote:
            cp.wait_send()
        for cp in local:
            cp.wait()

    return dict(ins=[h], outs=[jax.ShapeDtypeStruct((N_CHIP - 1, R, C), h.dtype)], n_sem=3, n_local=1,
                start=start, finish=finish)


def comm_call(routine, *, name):
    n_in, n_out = len(routine["ins"]), len(routine["outs"])

    def body(*refs):
        ins, outs, sems = refs[:n_in], refs[n_in:n_in + n_out], refs[n_in + n_out:]
        routine["start"](ins, outs, *sems)
        routine["finish"](ins, outs, *sems)

    any_spec = pl.BlockSpec(memory_space=pl.ANY)
    return pl.pallas_call(
        body, name=name, out_shape=routine["outs"],
        in_specs=[any_spec] * n_in, out_specs=[any_spec] * n_out,
        scratch_shapes=[pltpu.SemaphoreType.DMA((routine["n_sem"],)), pltpu.SemaphoreType.DMA((routine["n_sem"],)),
                        pltpu.SemaphoreType.DMA((routine["n_local"],))],
    )(*routine["ins"])


def add_own(g, got, core, *, cols, name):
    n, R, C = got.shape
    tr = _tile(R, 256, 8)
    if cols:
        gspec = pl.BlockSpec((tr, C), lambda q, i, c_ref: (i, 2 * q + c_ref[0]))
    else:
        gspec = pl.BlockSpec((1, tr, C), lambda q, i, c_ref: (2 * q + c_ref[0], i, 0))
    spec = pl.BlockSpec((1, tr, C), lambda q, i, c_ref: (q, i, 0))

    def body(c_ref, g_ref, b_ref, o_ref):
        gv = g_ref[...] if cols else g_ref[0]
        o_ref[0] = (gv.astype(F32) + b_ref[0].astype(F32)).astype(o_ref.dtype)

    return pl.pallas_call(
        body, name=name,
        grid_spec=pltpu.PrefetchScalarGridSpec(num_scalar_prefetch=1, grid=(n, R // tr),
                                               in_specs=[gspec, spec], out_specs=spec),
        out_shape=jax.ShapeDtypeStruct((n, R, C), g.dtype), compiler_params=_cp(),
    )(core, g, got)


def sum_chips(h, got, chip, *, name):
    _, R, C = h.shape
    tr = _tile(R, 256, 8)

    def body(q_ref, h_ref, g_ref, o_ref):
        acc = h_ref[0].astype(F32)
        for k in range(N_CHIP - 1):
            acc = acc + g_ref[k].astype(F32)
        o_ref[...] = acc

    return pl.pallas_call(
        body, name=name,
        grid_spec=pltpu.PrefetchScalarGridSpec(
            num_scalar_prefetch=1, grid=(R // tr,),
            in_specs=[pl.BlockSpec((1, tr, C), lambda i, q_ref: (q_ref[0], i, 0)),
                      pl.BlockSpec((N_CHIP - 1, tr, C), lambda i, q_ref: (0, i, 0))],
            out_specs=pl.BlockSpec((tr, C), lambda i, q_ref: (i, 0))),
        out_shape=jax.ShapeDtypeStruct((R, C), F32), compiler_params=_cp(),
    )(chip, h, got)


def reduce_scatter(g, *, cols, name):
    x, y, c = _position()
    core = jnp.reshape(c, (1,)).astype(jnp.int32)
    chip = jnp.reshape(2 * x + y, (1,)).astype(jnp.int32)
    got = comm_call(rs_pair_routine(g, cols), name=name + "_pair")[0]
    h = add_own(g, got, core, cols=cols, name=name + "_add")
    return sum_chips(h, comm_call(rs_chip_routine(h), name=name + "_chip")[0], chip, name=name + "_sum")


def sum_blocks(stack, *, name):
    n, R, C = stack.shape
    tr = _tile(R, 256, 8)

    def body(x_ref, o_ref):
        acc = x_ref[0].astype(F32)
        for i in range(1, n):
            acc = acc + x_ref[i].astype(F32)
        o_ref[...] = acc

    return pl.pallas_call(
        body, name=name, grid=(R // tr,),
        in_specs=[pl.BlockSpec((n, tr, C), lambda i: (0, i, 0))],
        out_specs=pl.BlockSpec((tr, C), lambda i: (i, 0)),
        out_shape=jax.ShapeDtypeStruct((R, C), F32), compiler_params=_cp(),
    )(stack)


PACK_COLS = 1024
PACK_ROW_MULT = 8


def _pack(arrays, dtype):
    flat = jnp.concatenate([a.reshape(-1).astype(dtype) for a in arrays])
    n = flat.shape[0]
    unit = PACK_COLS * PACK_ROW_MULT
    padded = -(-n // unit) * unit
    return jnp.pad(flat, (0, padded - n)).reshape(padded // PACK_COLS, PACK_COLS)


def _unpack(packed, shapes):
    flat = packed.reshape(-1)
    out, o = [], 0
    for s in shapes:
        n = int(np.prod(s))
        out.append(flat[o:o + n].reshape(s))
        o += n
    return out


def fsdp_cols(shard, *, name):
    K, n = shard.shape
    npad = -(-n // LANES) * LANES

    @jax.custom_vjp
    def f(p):
        p = jnp.pad(p, ((0, 0), (0, npad - n))) if npad != n else p
        return all_gather(p.astype(BF16), cols=True, name=name + "_ag")

    def fwd(p):
        return f(p), None

    def bwd(_, g):
        d = reduce_scatter(g, cols=True, name=name + "_rs")
        return (d[:, :n] if npad != n else d,)

    f.defvjp(fwd, bwd)
    return f(shard)


def fsdp_rows(shard, *, name):
    k, N = shard.shape

    @jax.custom_vjp
    def f(p):
        return all_gather(p.astype(BF16), name=name + "_ag").reshape(N_DEV * k, N)

    def fwd(p):
        return f(p), None

    def bwd(_, g):
        return (reduce_scatter(g.reshape(N_DEV, k, N), cols=False, name=name + "_rs"),)

    f.defvjp(fwd, bwd)
    return f(shard)


def _unpad_cols(w, n):
    K = w.shape[0]
    npad = w.shape[1] // N_DEV
    if npad == n:
        return w
    return w.reshape(K, N_DEV, npad)[:, :, :n].reshape(K, N_DEV * n)


def gather_rows(part, me, *, name):
    rows, n = part.shape
    per = rows // N_DEV

    @jax.custom_vjp
    def f(part):
        full = all_gather(part, name=name + "_fwd")
        mine = lax.dynamic_slice_in_dim(full, me * per, per, axis=1)
        return jnp.swapaxes(mine, 0, 1).reshape(per, N_DEV * n)

    def fwd(part):
        return f(part), None

    def bwd(_, g):
        full = all_gather(g, name=name + "_bwd")
        mine = lax.dynamic_slice_in_dim(full, me * n, n, axis=2)
        return (mine.reshape(rows, n),)

    f.defvjp(fwd, bwd)
    return f(part)


def _seg_layout():
    offs = np.concatenate([[0], np.cumsum(IN_SPLITS)])
    cols, widths, leads = [], [], []
    for s in SEG_ORDER:
        cols.append((int(offs[s]), int(offs[s + 1])))
        widths.append(SEG_PAD.get(s, IN_SPLITS[s]))
        leads.append(SEG_LEAD.get(s, 0))
    return cols, widths, leads


def _arrange_w_in(w, n):
    D = w.shape[0]
    npad = w.shape[1] // N_DEV
    cols, widths, leads = _seg_layout()

    def pieces_of(a, b):
        out = []
        for d in range(N_DEV):
            lo, hi = max(a, n * d), min(b, n * (d + 1))
            if lo < hi:
                out.append((d, lo, hi))
        return out

    @jax.custom_vjp
    def f(w):
        parts = []
        for (a, b), wd, ld in zip(cols, widths, leads):
            if ld:
                parts.append(jnp.zeros((D, ld), w.dtype))
            parts += [w[:, npad * d + lo - n * d:npad * d + hi - n * d] for d, lo, hi in pieces_of(a, b)]
            if wd != ld + b - a:
                parts.append(jnp.zeros((D, wd - ld - (b - a)), w.dtype))
        parts.append(jnp.zeros((D, IN_WIDTH - sum(widths)), w.dtype))
        return jnp.concatenate(parts, axis=1)

    def fwd(w):
        return f(w), None

    def bwd(_, g):
        offs = np.concatenate([[0], np.cumsum(widths)])
        runs = []
        for ((a, b), off, ld) in zip(cols, offs[:-1], leads):
            runs += [(lo, int(off) + ld + lo - a, hi - lo) for _, lo, hi in pieces_of(a, b)]
        runs.sort()
        parts, d_next = [], 1
        for lo, o, ln in runs:
            while lo >= n * d_next:
                parts.append(jnp.zeros((D, npad - n), g.dtype))
                d_next += 1
            parts.append(g[:, o:o + ln])
        parts.append(jnp.zeros((D, npad - n), g.dtype))
        return (jnp.concatenate(parts, axis=1),)

    f.defvjp(fwd, bwd)
    return f(w)


def split_cols(proj, widths):
    @jax.custom_vjp
    def f(p):
        outs, o = [], 0
        for wd in widths:
            outs.append(p[:, o:o + wd])
            o += wd
        return tuple(outs)

    def fwd(p):
        return f(p), None

    def bwd(_, gs):
        rest = proj.shape[1] - sum(widths)
        tail = [jnp.zeros((proj.shape[0], rest), proj.dtype)] if rest else []
        return (jnp.concatenate(list(gs) + tail, axis=1),)

    f.defvjp(fwd, bwd)
    return f(proj)


BIG = ("w_in", "w_uq", "w_ukv", "conv_w", "w_out", "w_gate_up", "w_down")
SMALL = ("b_ada", "norm1_g", "norm2_g", "q_norm_g", "k_norm_g", "mla_q_norm_g", "mla_kv_norm_g",
         "conv_b", "dt_bias", "a_log", "d_skip", "ssd_norm_g", "final_norm_g")
WEIGHTS = ("w_ada", "b_ada", "norm1_g", "norm2_g", "w_in", "q_norm_g", "k_norm_g", "mla_q_norm_g",
           "w_uq", "mla_kv_norm_g", "w_ukv", "conv_w", "conv_b", "dt_bias", "a_log", "d_skip",
           "ssd_norm_g", "w_out", "w_gate_up", "w_down", "final_norm_g")


PRE = ("w_in", "w_uq", "w_ukv", "conv_w")


def _layer(l, x, h, mods, W, shards, nxt, P, tabs):
    B, S, D = x.shape
    T = B * S
    nm = f"l{l}_"
    shift1, scale1, gate1, shift2, scale2, gate2 = mods[l]
    cos_a, sin_a, cos_b, sin_b, cos_k, sin_k = tabs

    w_in = _arrange_w_in(W["w_in"], IN_COLS // N_DEV)
    proj, (w_out_full,) = linear(h.reshape(T, D), w_in, out_dtype=F32, name=nm + "in",
                                 carry=((shards["w_out"], False, "w_out"),))
    q_a, k_a, v_a, cq, ckv, z, xbc, kpe, dtr = split_cols(proj, _seg_layout()[1])

    qn = group_rmsnorm(q_a, jnp.tile(P["q_norm_g"][l], GQA_H)[None], gs=HEAD, out_dtype=F32, name=nm + "qnorm")
    kn = group_rmsnorm(k_a, jnp.tile(P["k_norm_g"][l], GQA_KV)[None], gs=HEAD, out_dtype=F32, name=nm + "knorm")
    qr = rope(qn.reshape(B, S, -1), cos_a[:, :GQA_H * HEAD], sin_a[:, :GQA_H * HEAD], d=HEAD // 4, name=nm + "qrope")
    kr = rope(kn.reshape(B, S, -1), cos_a[:, :GQA_KV * HEAD], sin_a[:, :GQA_KV * HEAD], d=HEAD // 4, name=nm + "krope")
    gu_rows = shards["w_gate_up"].shape[0] // 2
    o_a, (w_gu_top,) = attention(qr, kr, v_a.reshape(B, S, -1), H=GQA_H, Hkv=GQA_KV, dk=HEAD, dv=HEAD,
                                 scale=HEAD ** -0.5, name=nm + "gqa",
                                 carry=((shards["w_gate_up"][:gu_rows], True, "w_gate_up_top"),))

    slot_pad = MLA_DK - MLA_NOPE - MLA_ROPE
    w_uq = jnp.pad(W["w_uq"].reshape(MLA_QL, MLA_H, MLA_NOPE + MLA_ROPE), ((0, 0), (0, 0), (0, slot_pad)))
    w_uq = w_uq.reshape(MLA_QL, MLA_H * MLA_DK)
    w_ukv = W["w_ukv"].reshape(MLA_KVL, MLA_H, MLA_NOPE + MLA_V)
    w_ukv = jnp.concatenate(
        [jnp.pad(w_ukv[:, :, :MLA_NOPE], ((0, 0), (0, 0), (0, MLA_DK - MLA_NOPE))).reshape(MLA_KVL, -1),
         w_ukv[:, :, MLA_NOPE:].reshape(MLA_KVL, -1)], axis=1)
    cqn = group_rmsnorm(cq, P["mla_q_norm_g"][l][None], gs=MLA_QL, out_dtype=BF16, name=nm + "cqnorm")
    ckvn = group_rmsnorm(ckv, P["mla_kv_norm_g"][l][None], gs=MLA_KVL, out_dtype=BF16, name=nm + "ckvnorm")
    qb = linear(cqn, w_uq, out_dtype=F32, name=nm + "uq")
    kvb = linear(ckvn, w_ukv, out_dtype=F32, name=nm + "ukv")
    k_slots, v_b = split_cols(kvb, (MLA_H * MLA_DK, MLA_H * MLA_V))
    q_cat = rope(qb.reshape(B, S, -1), cos_b, sin_b, d=MLA_ROPE // 4, name=nm + "qpe_rope")
    k_pe = rope(kpe.reshape(B, S, -1), cos_k, sin_k, d=MLA_ROPE // 4, name=nm + "kpe_rope")
    k_cat = k_slots.reshape(B, S, -1) + jnp.tile(k_pe, (1, 1, MLA_H))
    o_b, (w_gu_bottom,) = attention(q_cat, k_cat, v_b.reshape(B, S, -1), H=MLA_H, Hkv=MLA_H, dk=MLA_DK, dv=MLA_V,
                                    scale=(MLA_NOPE + MLA_ROPE) ** -0.5, name=nm + "mla",
                                    carry=((shards["w_gate_up"][gu_rows:], True, "w_gate_up_bottom"),))
    w_gu_full = jnp.concatenate([w_gu_top, w_gu_bottom], axis=0)

    xact = conv_silu(xbc.reshape(B, S, -1), W["conv_w"].astype(F32), P["conv_b"][l][None], name=nm + "conv")
    brow = jnp.pad(P["dt_bias"][l].reshape(1, -1), ((0, 0), (0, LANES - 2 * SSD_H)))
    arow = jnp.pad(-jnp.exp(P["a_log"][l].reshape(1, -1)), ((0, 0), (0, LANES - 2 * SSD_H)))
    raw = dtr.reshape(B, S, LANES)
    y_f = ssd_scan(xact, raw, brow, arow, rev=False, name=nm + "ssd_f")
    y_b = ssd_scan(xact, raw, brow, arow, rev=True, name=nm + "ssd_b")
    dsk = jnp.repeat(P["d_skip"][l], SSD_P)[None]
    o_c = ssd_out(y_f.reshape(T, -1), y_b.reshape(T, -1), xact.reshape(T, -1), z, dsk,
                  P["ssd_norm_g"][l][None], name=nm + "ssd_out")

    o = jnp.concatenate([o_a.reshape(T, -1).astype(BF16), o_b.reshape(T, -1).astype(BF16), o_c], axis=-1)
    mix = linear(o, w_out_full, out_dtype=F32, name=nm + "out")
    x, h = res_norm(x, gate1, mix.reshape(B, S, D), P["norm2_g"][l][None], scale2, shift2, name=nm + "res1_norm2")

    y, nxt_full = ffn(h.reshape(T, D), w_gu_full, shards["w_down"], name=nm + "ffn",
                      carry=tuple((nxt[n], True, "next_" + n) for n in PRE) if nxt else ())
    if l + 1 < len(mods):
        x, h = res_norm(x, gate2, y.reshape(B, S, D), P["norm1_g"][l + 1][None], mods[l + 1][1], mods[l + 1][0],
                        name=nm + "res2_norm1")
    else:
        x, h = gated_residual(x, gate2, y.reshape(B, S, D), name=nm + "res2"), None
    return x, h, nxt_full


def kernel(x, c, w_ada, b_ada, norm1_g, norm2_g, w_in, q_norm_g, k_norm_g, mla_q_norm_g, w_uq, mla_kv_norm_g, w_ukv, conv_w, conv_b, dt_bias, a_log, d_skip, ssd_norm_g, w_out, w_gate_up, w_down, final_norm_g, loss_target, m_w_ada, m_b_ada, m_norm1_g, m_norm2_g, m_w_in, m_q_norm_g, m_k_norm_g, m_mla_q_norm_g, m_w_uq, m_mla_kv_norm_g, m_w_ukv, m_conv_w, m_conv_b, m_dt_bias, m_a_log, m_d_skip, m_ssd_norm_g, m_w_out, m_w_gate_up, m_w_down, m_final_norm_g, v_w_ada, v_b_ada, v_norm1_g, v_norm2_g, v_w_in, v_q_norm_g, v_k_norm_g, v_mla_q_norm_g, v_w_uq, v_mla_kv_norm_g, v_w_ukv, v_conv_w, v_conv_b, v_dt_bias, v_a_log, v_d_skip, v_ssd_norm_g, v_w_out, v_w_gate_up, v_w_down, v_final_norm_g):
    args = dict(locals())
    weights = {n: args[n] for n in WEIGHTS}
    moments_m = {n: args["m_" + n] for n in WEIGHTS}
    moments_v = {n: args["v_" + n] for n in WEIGHTS}
    B, S, D = x.shape
    L = w_ada.shape[0]
    T = B * S
    px, py, pc = _position()
    me = 4 * px + 2 * py + pc
    small_shapes = [weights[n].shape for n in SMALL]

    tabs = (*rope_tables(S, HEAD, GQA_H * HEAD),
            *rope_tables(S, MLA_ROPE, MLA_H * MLA_DK, slot=MLA_DK, lead=MLA_NOPE),
            *rope_tables(S, MLA_ROPE, MLA_DK, slot=MLA_DK, lead=MLA_NOPE))
    c_all = all_gather(c, name="gather_c").reshape(N_DEV * B, D)

    def local_loss(big, w_ada_s, small, x):
        P = dict(zip(SMALL, small))
        pre = [fsdp_cols(big[n][0], name=f"l0_{n}") for n in PRE]
        mods = []
        for l in range(L):
            part = linear(c_all, w_ada_s[l], out_dtype=F32, a_silu=True, name=f"l{l}_ada")
            mod = gather_rows(part, me, name=f"l{l}_mod") + P["b_ada"][l][None]
            mods.append([m[:, None, :] for m in jnp.split(mod, 6, axis=-1)])
        h = rmsmod(x, P["norm1_g"][0][None], mods[0][1], mods[0][0], name="l0_norm1")
        for l in range(L):
            W = {n: full if n == "w_in" else _unpad_cols(full, big[n].shape[2]) for n, full in zip(PRE, pre)}
            shards = {n: big[n][l] for n in ("w_out", "w_gate_up", "w_down")}
            nxt = {n: big[n][l + 1] for n in PRE} if l + 1 < L else None
            x, h, pre = _layer(l, x, h, mods, W, shards, nxt, P, tabs)
        return final_loss(x.reshape(T, D), P["final_norm_g"][None], loss_target.reshape(T, D), name="loss")

    big = {n: weights[n] for n in BIG}
    small = tuple(weights[n] for n in SMALL)
    loss, (g_big, g_ada, g_small, grad_x) = jax.value_and_grad(local_loss, argnums=(0, 1, 2, 3))(
        big, w_ada, small, x)
    loss = lax.psum(loss, ("x", "y", "c"))

    grads = dict(g_big)
    grads["w_ada"] = g_ada
    g_small_sum = sum_blocks(all_gather(_pack(g_small, F32), name="small_grads_ag"), name="small_grads_sum")
    grads.update(zip(SMALL, _unpack(g_small_sum, small_shapes)))

    delta, new_m, new_v = {}, {}, {}
    for n in ("w_ada",) + BIG:
        delta[n], new_m[n], new_v[n] = adamw(weights[n], grads[n], moments_m[n], moments_v[n], name="adamw_" + n)
    d_, m_, v_ = adamw(_pack([weights[n] for n in SMALL], F32)[None], g_small_sum[None],
                       _pack([moments_m[n] for n in SMALL], F32)[None], _pack([moments_v[n] for n in SMALL], F32)[None],
                       name="adamw_small")
    for tgt, packed in ((delta, d_), (new_m, m_), (new_v, v_)):
        tgt.update(zip(SMALL, _unpack(packed[0], small_shapes)))

    return (loss, grad_x, *[grads[n] for n in WEIGHTS], *[delta[n] for n in WEIGHTS],
            *[new_m[n] for n in WEIGHTS], *[new_v[n] for n in WEIGHTS])
```

```python
import functools
import math

import jax
import jax.numpy as jnp
import numpy as np
from jax import lax
from jax.experimental import pallas as pl
from jax.experimental.pallas import tpu as pltpu

F32 = jnp.float32
BF16 = jnp.bfloat16
N_DEV = 8
EPS = 1e-6
ROPE_THETA = 10000.0
GRID_W = 64

GQA_H, GQA_KV, HEAD = 6, 2, 128
MLA_H, MLA_QL, MLA_KVL, MLA_NOPE, MLA_ROPE, MLA_V = 4, 512, 256, 128, 64, 128
MLA_DK = 256
SSD_H, SSD_P, SSD_G, SSD_N, SSD_K, CHUNK = 12, 64, 2, 128, 5, 128
SSD_INNER = SSD_H * SSD_P
SSD_CONV_DIM = SSD_INNER + 2 * SSD_G * SSD_N
N_PAIR = SSD_H // 2
LANES = 128
IN_SPLITS = (768, 256, 256, 512, 256, 64, 768, 1280, 24)
IN_COLS = sum(IN_SPLITS)
SEG_ORDER = (0, 1, 2, 3, 4, 6, 7, 5, 8)
SEG_PAD = {5: 256, 8: 128}
SEG_LEAD = {5: 128}
IN_WIDTH = 4608

ADAM_LR, ADAM_B1, ADAM_B2, ADAM_EPS, ADAM_WD, ADAM_STEP = 0.001, 0.9, 0.999, 1e-08, 0.01, 10
VMEM_LIMIT = 56 * 1024 * 1024
MM_TM, MM_TN, MM_TK = 1024, 1408, 2048


def _cp(**kw):
    return pltpu.CompilerParams(vmem_limit_bytes=VMEM_LIMIT, **kw)


def _tile(dim, cap, mult=128):
    if dim <= cap:
        return dim
    best = None
    t = mult
    while t <= cap:
        if dim % t == 0:
            best = t
        t += mult
    assert best is not None, (dim, cap)
    return best


def _sigmoid(x):
    return 1.0 / (1.0 + jnp.exp(-x))


def _dot(a, b, dims):
    return lax.dot_general(a, b, (dims, ((), ())), preferred_element_type=F32)


NN = ((1,), (0,))
NT = ((1,), (1,))
TN = ((0,), (0,))


def _dotf(a, b, dims=NN):
    return lax.dot_general(a, b, (dims, ((), ())), preferred_element_type=F32,
                           precision=lax.Precision.HIGHEST)


def _bf(x):
    return x.astype(BF16)


def mm(a, b, *, ta=False, tb=False, out_dtype=F32, a_silu=False, name, carry=(),
       a_halves=False, b_halves=False):
    if a_halves:
        assert not ta
        M, K = a.shape[1], 2 * a.shape[2]
    elif ta:
        K, M = a.shape
    else:
        M, K = a.shape
    if b_halves:
        assert not tb
        K2, N = b.shape[1], 2 * b.shape[2]
    elif tb:
        N, K2 = b.shape
    else:
        K2, N = b.shape
    assert K == K2, (a.shape, b.shape, ta, tb)
    tm = _tile(M, MM_TM)
    tn = _tile(N // 2, MM_TN) if b_halves else _tile(N, MM_TN)
    tk = _tile(K // 2, MM_TK) if a_halves else _tile(K, MM_TK)
    nk = K // tk
    dims = ((0 if ta else 1,), (1 if tb else 0,))

    def partial_product(a_ref, b_ref):
        av = a_ref[...]
        if a_silu:
            av = av.astype(F32)
            av = av * _sigmoid(av)
        return _dot(_bf(av), _bf(b_ref[...]), dims)

    def body_single(a_ref, b_ref, o_ref):
        o_ref[...] = partial_product(a_ref, b_ref).astype(o_ref.dtype)

    def body_acc(a_ref, b_ref, o_ref, acc_ref):
        k = pl.program_id(2)

        @pl.when(k == 0)
        def _():
            acc_ref[...] = partial_product(a_ref, b_ref)

        @pl.when(k > 0)
        def _():
            acc_ref[...] += partial_product(a_ref, b_ref)

        @pl.when(k == nk - 1)
        def _():
            o_ref[...] = acc_ref[...].astype(o_ref.dtype)

    body = body_single if nk == 1 else body_acc

    a_spec = (pl.BlockSpec((tk, tm), lambda i, j, k: (k, i)) if ta
              else pl.BlockSpec((tm, tk), lambda i, j, k: (i, k)))
    b_spec = (pl.BlockSpec((tn, tk), lambda i, j, k: (j, k)) if tb
              else pl.BlockSpec((tk, tn), lambda i, j, k: (k, j)))
    if a_halves:
        a_spec = pl.BlockSpec((None, tm, tk), lambda i, j, k: (k // (nk // 2), i, k % (nk // 2)))
    if b_halves:
        nnh = N // tn // 2
        b_spec = pl.BlockSpec((None, tk, tn), lambda i, j, k: (j // nnh, k, j % nnh))
    res = carry_call(
        body, name=name, grid=(M // tm, N // tn, nk),
        in_specs=[a_spec, b_spec],
        out_specs=[pl.BlockSpec((tm, tn), lambda i, j, k: (i, j))],
        out_shape=[jax.ShapeDtypeStruct((M, N), out_dtype)],
        scratch_shapes=[] if nk == 1 else [pltpu.VMEM((tm, tn), F32)],
        carry=carry, dims=("parallel", "parallel", "arbitrary"),
    )(a, b)
    return res if carry else res[0]


def _shard_prep(p, cols):
    if cols and p.shape[1] % LANES:
        p = jnp.pad(p, ((0, 0), (0, -p.shape[1] % LANES)))
    return p.astype(BF16)


def _full_post(full, cols):
    return full if cols else full.reshape(full.shape[0] * full.shape[1], full.shape[2])


def _rs_begin(g_full, cols, shard_shape, name):
    gg = g_full if cols else g_full.reshape(N_DEV, shard_shape[0], shard_shape[1])
    x, y, c = _position()
    got = comm_call(rs_pair_routine(gg, cols), name=name + "_pair")[0]
    return add_own(gg, got, jnp.reshape(c, (1,)).astype(jnp.int32), cols=cols, name=name + "_add")


def _rs_end(h, got, cols, shard_shape, name):
    x, y, c = _position()
    d = sum_chips(h, got, jnp.reshape(2 * x + y, (1,)).astype(jnp.int32), name=name + "_sum")
    return d[:, :shard_shape[1]] if cols else d


def linear(a, w, *, out_dtype, name, a_silu=False, carry=()):
    kinds = [(cols, s.shape, tag) for s, cols, tag in carry]

    def run(a, w, *shards):
        routines = [ag_routine(_shard_prep(s, cols), cols) for s, (cols, _, _) in zip(shards, kinds)]
        res = mm(a, w, out_dtype=out_dtype, a_silu=a_silu, name=name + "_fwd", carry=routines)
        if not routines:
            return (res,)
        return (res[0], *[_full_post(fu, cols) for fu, (cols, _, _) in zip(res[1:], kinds)])

    @jax.custom_vjp
    def f(a, w, *shards):
        return run(a, w, *shards)

    def fwd(a, w, *shards):
        return run(a, w, *shards), (a, w)

    def bwd(res, cts):
        a, w = res
        g = cts[0]
        hs = [_rs_begin(gf, cols, shp, name + "_" + tag) for gf, (cols, shp, tag) in zip(cts[1:], kinds)]
        routines = [rs_chip_routine(h) for h in hs]
        if a_silu:
            assert not routines
            da = jnp.zeros_like(a)
            gots = []
        else:
            r = mm(g, w, tb=True, out_dtype=a.dtype, name=name + "_da", carry=routines)
            da, gots = (r[0], r[1:]) if routines else (r, [])
        dw = mm(a, g, ta=True, out_dtype=w.dtype, a_silu=a_silu, name=name + "_dw")
        dsh = [_rs_end(h, got, cols, shp, name + "_" + tag)
               for h, got, (cols, shp, tag) in zip(hs, gots, kinds)]
        return (da, dw, *dsh)

    f.defvjp(fwd, bwd)
    out = f(a, w, *[s for s, _, _ in carry])
    return (out[0], list(out[1:])) if carry else out[0]


def swiglu_up(h, w_gu, *, name, carry=()):
    T, D = h.shape
    F = w_gu.shape[1] // 2
    assert D <= MM_TK
    tm, tn = _tile(T, MM_TM), _tile(F, 512)
    nf = F // tn

    def body(a_ref, bg_ref, bu_ref, gu_ref, act_ref):
        av = a_ref[...]
        g = _dot(av, bg_ref[...], NN)
        u = _dot(av, bu_ref[...], NN)
        gu_ref[0] = g.astype(gu_ref.dtype)
        gu_ref[1] = u.astype(gu_ref.dtype)
        act_ref[...] = (g * _sigmoid(g) * u).astype(act_ref.dtype)

    return carry_call(
        body, name=name, grid=(T // tm, nf),
        in_specs=[pl.BlockSpec((tm, D), lambda i, j: (i, 0)), pl.BlockSpec((D, tn), lambda i, j: (0, j)),
                  pl.BlockSpec((D, tn), lambda i, j: (0, j + nf))],
        out_specs=[pl.BlockSpec((2, tm, tn), lambda i, j: (0, i, j)), pl.BlockSpec((tm, tn), lambda i, j: (i, j))],
        out_shape=[jax.ShapeDtypeStruct((2, T, F), BF16), jax.ShapeDtypeStruct((T, F), BF16)],
        carry=carry,
    )(h, w_gu, w_gu)


def swiglu_down_bwd(g, w_dn, gu, *, name, carry=()):
    T, D = g.shape
    F = w_dn.shape[0]
    assert D <= MM_TK
    tm, tn = _tile(T, MM_TM), _tile(F, 512)

    def body(a_ref, b_ref, gu_ref, o_ref):
        dact = _dot(_bf(a_ref[...]), b_ref[...], NT)
        gv = gu_ref[0].astype(F32)
        uv = gu_ref[1].astype(F32)
        sg = _sigmoid(gv)
        o_ref[0] = (dact * uv * (sg * (1.0 + gv * (1.0 - sg)))).astype(o_ref.dtype)
        o_ref[1] = (dact * gv * sg).astype(o_ref.dtype)

    half = pl.BlockSpec((2, tm, tn), lambda i, j: (0, i, j))
    return carry_call(
        body, name=name, grid=(T // tm, F // tn),
        in_specs=[pl.BlockSpec((tm, D), lambda i, j: (i, 0)), pl.BlockSpec((tn, D), lambda i, j: (j, 0)), half],
        out_specs=[half], out_shape=[jax.ShapeDtypeStruct((2, T, F), BF16)], carry=carry,
    )(g, w_dn, gu)


def ffn(h, w_gu, w_dn_shard, *, name, carry=()):
    kinds = [(cols, s.shape, tag) for s, cols, tag in carry]
    dn_shape = w_dn_shard.shape

    def run(h, w_gu, w_dn_shard, *shards):
        gu, act, w_dn = swiglu_up(h, w_gu, name=name + "_up", carry=[ag_routine(_shard_prep(w_dn_shard, False))])
        w_dn = _full_post(w_dn, False)
        routines = [ag_routine(_shard_prep(s, cols), cols) for s, (cols, _, _) in zip(shards, kinds)]
        res = mm(act, w_dn, out_dtype=F32, name=name + "_down", carry=routines)
        y, fulls = (res[0], res[1:]) if routines else (res, [])
        return (y, *[_full_post(fu, cols) for fu, (cols, _, _) in zip(fulls, kinds)]), (h, w_gu, w_dn, gu, act)

    @jax.custom_vjp
    def f(h, w_gu, w_dn_shard, *shards):
        return run(h, w_gu, w_dn_shard, *shards)[0]

    def fwd(h, w_gu, w_dn_shard, *shards):
        return run(h, w_gu, w_dn_shard, *shards)

    def bwd(res, cts):
        h, w_gu, w_dn, gu, act = res
        g = cts[0]
        hs = [_rs_begin(gf, cols, shp, name + "_" + tag) for gf, (cols, shp, tag) in zip(cts[1:], kinds)]
        dgu, *gots = swiglu_down_bwd(g, w_dn, gu, name=name + "_down_da", carry=[rs_chip_routine(x) for x in hs])
        dw_dn = mm(act, g, ta=True, out_dtype=w_dn.dtype, name=name + "_down_dw")
        h_dn = _rs_begin(dw_dn, False, dn_shape, name + "_w_down")
        dh, got_dn = mm(dgu, w_gu, tb=True, a_halves=True, out_dtype=h.dtype, name=name + "_up_da",
                        carry=[rs_chip_routine(h_dn)])
        dw_gu = mm(h, dgu, ta=True, b_halves=True, out_dtype=w_gu.dtype, name=name + "_up_dw")
        d_dn = _rs_end(h_dn, got_dn, False, dn_shape, name + "_w_down")
        dsh = [_rs_end(x, got, cols, shp, name + "_" + tag) for x, got, (cols, shp, tag) in zip(hs, gots, kinds)]
        return (dh, dw_gu, d_dn, *dsh)

    f.defvjp(fwd, bwd)
    out = f(h, w_gu, w_dn_shard, *[s for s, _, _ in carry])
    return out[0], list(out[1:])


def rmsmod(x, g, scale, shift, *, name):
    B, S, D = x.shape
    ts = _tile(S, 256, 8)
    row = pl.BlockSpec((1, ts, D), lambda b, j: (b, j, 0))
    per_b = pl.BlockSpec((1, 1, D), lambda b, j: (b, 0, 0))
    gspec = pl.BlockSpec((1, D), lambda b, j: (0, 0))

    def fwd_call(x, g, scale, shift):
        def body(x_ref, g_ref, sc_ref, sh_ref, o_ref):
            xv = x_ref[0]
            r = lax.rsqrt(jnp.mean(xv * xv, axis=-1, keepdims=True) + EPS)
            y = xv * r * g_ref[...]
            o_ref[0] = (y * (1.0 + sc_ref[0]) + sh_ref[0]).astype(o_ref.dtype)

        return pl.pallas_call(
            body, name=name + "_fwd", grid=(B, S // ts),
            in_specs=[row, gspec, per_b, per_b], out_specs=row,
            out_shape=jax.ShapeDtypeStruct((B, S, D), BF16), compiler_params=_cp(),
        )(x, g, scale, shift)

    def bwd_call(x, g, scale, dh):
        def body(x_ref, g_ref, sc_ref, dh_ref, dx_ref, dg_ref, dsc_ref, dsh_ref):
            j = pl.program_id(1)
            xv = x_ref[0]
            dh = dh_ref[0].astype(F32)
            r = lax.rsqrt(jnp.mean(xv * xv, axis=-1, keepdims=True) + EPS)
            xn = xv * r
            gv = g_ref[...]
            dy = dh * (1.0 + sc_ref[0])
            dxn = dy * gv
            dx_ref[0] = r * (dxn - xn * jnp.mean(dxn * xn, axis=-1, keepdims=True))

            @pl.when(j == 0)
            def _():
                dg_ref[...] = jnp.zeros_like(dg_ref)
                dsc_ref[...] = jnp.zeros_like(dsc_ref)
                dsh_ref[...] = jnp.zeros_like(dsh_ref)

            dg_ref[0] += jnp.sum(dy * xn, axis=0, keepdims=True)
            dsc_ref[0] += jnp.sum(dh * xn * gv, axis=0, keepdims=True)
            dsh_ref[0] += jnp.sum(dh, axis=0, keepdims=True)

        vec = jax.ShapeDtypeStruct((B, 1, D), F32)
        return pl.pallas_call(
            body, name=name + "_bwd", grid=(B, S // ts),
            in_specs=[row, gspec, per_b, row], out_specs=[row, per_b, per_b, per_b],
            out_shape=[jax.ShapeDtypeStruct((B, S, D), F32), vec, vec, vec], compiler_params=_cp(),
        )(x, g, scale, dh)

    @jax.custom_vjp
    def f(x, g, scale, shift):
        return fwd_call(x, g, scale, shift)

    def fwd(x, g, scale, shift):
        return fwd_call(x, g, scale, shift), (x, g, scale)

    def bwd(res, dh):
        x, g, scale = res
        dx, dg, dsc, dsh = bwd_call(x, g, scale, dh)
        return dx, jnp.sum(dg, axis=0), dsc, dsh

    f.defvjp(fwd, bwd)
    return f(x, g, scale, shift)


def group_rmsnorm(x, g, *, gs, out_dtype, name):
    T, W = x.shape
    ng = W // gs
    tr = _tile(T, 512, 8)
    row = pl.BlockSpec((tr, W), lambda i: (i, 0))
    gspec = pl.BlockSpec((1, W), lambda i: (0, 0))

    def fwd_call(x, g):
        def body(x_ref, g_ref, o_ref):
            for i in range(ng):
                sl = slice(i * gs, (i + 1) * gs)
                xv = x_ref[:, sl]
                r = lax.rsqrt(jnp.mean(xv * xv, axis=-1, keepdims=True) + EPS)
                o_ref[:, sl] = (xv * r * g_ref[:, sl]).astype(o_ref.dtype)

        return pl.pallas_call(
            body, name=name + "_fwd", grid=(T // tr,), in_specs=[row, gspec], out_specs=row,
            out_shape=jax.ShapeDtypeStruct((T, W), out_dtype), compiler_params=_cp(),
        )(x, g)

    def bwd_call(x, g, dy):
        def body(x_ref, g_ref, dy_ref, dx_ref, dg_ref):
            @pl.when(pl.program_id(0) == 0)
            def _():
                dg_ref[...] = jnp.zeros_like(dg_ref)

            for i in range(ng):
                sl = slice(i * gs, (i + 1) * gs)
                xv = x_ref[:, sl]
                dyv = dy_ref[:, sl].astype(F32)
                r = lax.rsqrt(jnp.mean(xv * xv, axis=-1, keepdims=True) + EPS)
                xn = xv * r
                dxn = dyv * g_ref[:, sl]
                dx_ref[:, sl] = r * (dxn - xn * jnp.mean(dxn * xn, axis=-1, keepdims=True))
                dg_ref[:, sl] += jnp.sum(dyv * xn, axis=0, keepdims=True)

        return pl.pallas_call(
            body, name=name + "_bwd", grid=(T // tr,), in_specs=[row, gspec, row],
            out_specs=[row, gspec],
            out_shape=[jax.ShapeDtypeStruct((T, W), F32), jax.ShapeDtypeStruct((1, W), F32)],
            compiler_params=_cp(),
        )(x, g, dy)

    @jax.custom_vjp
    def f(x, g):
        return fwd_call(x, g)

    def fwd(x, g):
        return fwd_call(x, g), (x, g)

    def bwd(res, dy):
        return bwd_call(res[0], res[1], dy)

    f.defvjp(fwd, bwd)
    return f(x, g)


def rope_tables(seq_len, rot_dim, width, slot=0, lead=0):
    rows = seq_len // GRID_W
    row_idx = jnp.repeat(jnp.arange(rows), GRID_W).astype(F32)
    col_idx = jnp.tile(jnp.arange(GRID_W), rows).astype(F32)
    axis_dim = rot_dim // 2
    inv_freq = jnp.power(ROPE_THETA, -jnp.arange(0, axis_dim, 2, dtype=F32) / axis_dim)
    ang_r = row_idx[:, None] * inv_freq[None, :]
    ang_c = col_idx[:, None] * inv_freq[None, :]
    cos = jnp.concatenate([jnp.cos(ang_r), jnp.cos(ang_r), jnp.cos(ang_c), jnp.cos(ang_c)], axis=-1)
    sin = jnp.concatenate([-jnp.sin(ang_r), jnp.sin(ang_r), -jnp.sin(ang_c), jnp.sin(ang_c)], axis=-1)
    if slot:
        ones = jnp.ones((seq_len, 1), F32)
        cos = jnp.concatenate([ones * jnp.ones((1, lead)), cos, ones * jnp.ones((1, slot - lead - rot_dim))], axis=-1)
        sin = jnp.concatenate([ones * jnp.zeros((1, lead)), sin, ones * jnp.zeros((1, slot - lead - rot_dim))], axis=-1)
        rot_dim = slot
    reps = width // rot_dim
    return jnp.tile(cos, (1, reps)), jnp.tile(sin, (1, reps))


def rope(x, cos, sin, *, d, name):
    B, S, W = x.shape
    ts = _tile(S, 512, 8)
    row = pl.BlockSpec((1, ts, W), lambda b, j: (b, j, 0))
    tab = pl.BlockSpec((ts, W), lambda b, j: (j, 0))

    def call(x, inverse, nm):
        def body(x_ref, c_ref, s_ref, o_ref):
            xv = x_ref[0]
            lane = lax.broadcasted_iota(jnp.int32, xv.shape, 1)
            first = (lane // d) % 2 == 0

            def swap(v):
                return jnp.where(first, pltpu.roll(v, W - d, 1), pltpu.roll(v, d, 1))

            if inverse:
                o_ref[0] = xv * c_ref[...] + swap(xv * s_ref[...])
            else:
                o_ref[0] = xv * c_ref[...] + swap(xv) * s_ref[...]

        return pl.pallas_call(
            body, name=nm, grid=(B, S // ts), in_specs=[row, tab, tab], out_specs=row,
            out_shape=jax.ShapeDtypeStruct((B, S, W), F32), compiler_params=_cp(),
        )(x, cos, sin)

    @jax.custom_vjp
    def f(x):
        return call(x, False, name + "_fwd")

    def fwd(x):
        return call(x, False, name + "_fwd"), None

    def bwd(_, g):
        return (call(g, True, name + "_bwd"),)

    f.defvjp(fwd, bwd)
    return f(x)


def attention(q, k, v, *, H, Hkv, dk, dv, scale, name, carry=(), k_shared=None):
    B, S, _ = q.shape
    rep = H // Hkv
    tq = _tile(S, 256, 8)
    kinds = [(cols, s.shape, tag) for s, cols, tag in carry]
    shared = [] if k_shared is None else [k_shared]
    ke_spec3 = [pl.BlockSpec((1, S, dk), lambda b, h, i: (b, 0, 0))] if shared else []
    ke_spec4 = [pl.BlockSpec((1, S, dk), lambda b, g, r, i: (b, 0, 0))] if shared else []

    def keys(k_ref, ke_refs):
        return _bf(k_ref[0] + ke_refs[0][0]) if ke_refs else _bf(k_ref[0])

    def fwd_call(q, k, v, ke, routines=()):
        def body(q_ref, k_ref, v_ref, *rest):
            ke_refs, (o_ref, lse_ref) = rest[:len(shared)], rest[len(shared):]
            s = _dot(_bf(q_ref[0]), keys(k_ref, ke_refs), NT) * scale
            m = jnp.max(s, axis=-1, keepdims=True)
            p = jnp.exp(s - m)
            l = jnp.sum(p, axis=-1, keepdims=True)
            o_ref[0] = _dot(_bf(p), _bf(v_ref[0]), NN) / l
            lse_ref[0, 0] = m + jnp.log(l)

        return carry_call(
            body, name=name + "_fwd", grid=(B, H, S // tq),
            in_specs=[pl.BlockSpec((1, tq, dk), lambda b, h, i: (b, i, h)),
                      pl.BlockSpec((1, S, dk), lambda b, h, i: (b, 0, h // rep)),
                      pl.BlockSpec((1, S, dv), lambda b, h, i: (b, 0, h // rep))] + ke_spec3,
            out_specs=[pl.BlockSpec((1, tq, dv), lambda b, h, i: (b, i, h)),
                       pl.BlockSpec((1, 1, tq, 1), lambda b, h, i: (b, h, i, 0))],
            out_shape=[jax.ShapeDtypeStruct((B, S, H * dv), F32),
                       jax.ShapeDtypeStruct((B, H, S, 1), F32)],
            carry=routines,
        )(q, k, v, *ke)

    def bwd_call(q, k, v, ke, o, do, lse, routines=()):
        def body(q_ref, k_ref, v_ref, *rest):
            ke_refs, rest = rest[:len(shared)], rest[len(shared):]
            o_ref, do_ref, lse_ref, dq_ref, dk_ref, dv_ref = rest[:6]

            @pl.when((pl.program_id(2) == 0) & (pl.program_id(3) == 0))
            def _():
                dk_ref[...] = jnp.zeros_like(dk_ref)
                dv_ref[...] = jnp.zeros_like(dv_ref)

            if shared:
                dke_ref = rest[6]

                @pl.when((pl.program_id(1) == 0) & (pl.program_id(2) == 0) & (pl.program_id(3) == 0))
                def _():
                    dke_ref[...] = jnp.zeros_like(dke_ref)

            qb = _bf(q_ref[0])
            kb = keys(k_ref, ke_refs)
            dov = do_ref[0]
            dob = _bf(dov)
            s = _dot(qb, kb, NT) * scale
            p = jnp.exp(s - lse_ref[0, 0])
            delta = jnp.sum(dov * o_ref[0], axis=-1, keepdims=True)
            dp = _dot(dob, _bf(v_ref[0]), NT)
            dsb = _bf(p * (dp - delta))
            dq_ref[0] = _dot(dsb, kb, NN) * scale
            dkv = _dot(dsb, qb, TN) * scale
            dk_ref[0] += dkv
            if shared:
                dke_ref[0] += dkv
            dv_ref[0] += _dot(_bf(p), dob, TN)

        qs = pl.BlockSpec((1, tq, dk), lambda b, g, r, i: (b, i, g * rep + r))
        os_ = pl.BlockSpec((1, tq, dv), lambda b, g, r, i: (b, i, g * rep + r))
        ks = pl.BlockSpec((1, S, dk), lambda b, g, r, i: (b, 0, g))
        vs = pl.BlockSpec((1, S, dv), lambda b, g, r, i: (b, 0, g))
        col = pl.BlockSpec((1, 1, tq, 1), lambda b, g, r, i: (b, g * rep + r, i, 0))
        return carry_call(
            body, name=name + "_bwd", grid=(B, Hkv, rep, S // tq),
            in_specs=[qs, ks, vs] + ke_spec4 + [os_, os_, col], out_specs=[qs, ks, vs] + ke_spec4,
            out_shape=[jax.ShapeDtypeStruct(q.shape, F32), jax.ShapeDtypeStruct(k.shape, F32),
                       jax.ShapeDtypeStruct(v.shape, F32)] + [jax.ShapeDtypeStruct(x.shape, F32) for x in ke],
            carry=routines,
        )(q, k, v, *ke, o, do, lse)

    n_ke = len(shared)

    def run(q, k, v, *rest):
        ke, shards = rest[:n_ke], rest[n_ke:]
        routines = [ag_routine(_shard_prep(s, cols), cols) for s, (cols, _, _) in zip(shards, kinds)]
        o, lse, *fulls = fwd_call(q, k, v, ke, routines)
        return (o, *[_full_post(fu, cols) for fu, (cols, _, _) in zip(fulls, kinds)]), lse

    @jax.custom_vjp
    def f(q, k, v, *rest):
        return run(q, k, v, *rest)[0]

    def fwd(q, k, v, *rest):
        outs, lse = run(q, k, v, *rest)
        return outs, (q, k, v, rest[:n_ke], outs[0], lse)

    def bwd(res, cts):
        q, k, v, ke, o, lse = res
        hs = [_rs_begin(gf, cols, shp, name + "_" + tag) for gf, (cols, shp, tag) in zip(cts[1:], kinds)]
        dq, dk_, dv_, *more = bwd_call(q, k, v, ke, o, cts[0], lse, [rs_chip_routine(h) for h in hs])
        dke, gots = more[:n_ke], more[n_ke:]
        dsh = [_rs_end(h, got, cols, shp, name + "_" + tag)
               for h, got, (cols, shp, tag) in zip(hs, gots, kinds)]
        return (dq, dk_, dv_, *dke, *dsh)

    f.defvjp(fwd, bwd)
    out = f(q, k, v, *shared, *[s for s, _, _ in carry])
    return (out[0], list(out[1:])) if carry else out[0]


def conv_silu(x, w, b, *, name):
    B, S, C = x.shape
    tc = _tile(C, 256)
    pad = SSD_K // 2
    xs = pl.BlockSpec((1, S, tc), lambda bi, j: (bi, 0, j))
    ws = pl.BlockSpec((SSD_K, tc), lambda bi, j: (0, j))
    bs = pl.BlockSpec((1, tc), lambda bi, j: (0, j))

    def shifted(v, off):
        if off == 0:
            return v
        t = lax.broadcasted_iota(jnp.int32, v.shape, 0)
        r = pltpu.roll(v, (-off) % S, 0)
        return jnp.where((t + off >= 0) & (t + off < S), r, 0.0)

    def pre_act(xv, wv, bv):
        u = jnp.zeros_like(xv) + bv
        for k in range(SSD_K):
            u = u + wv[k:k + 1, :] * shifted(xv, k - pad)
        return u

    def fwd_call(x, w, b):
        def body(x_ref, w_ref, b_ref, o_ref):
            u = pre_act(x_ref[0], w_ref[...], b_ref[...])
            o_ref[0] = u * _sigmoid(u)

        return pl.pallas_call(
            body, name=name + "_fwd", grid=(B, C // tc), in_specs=[xs, ws, bs], out_specs=xs,
            out_shape=jax.ShapeDtypeStruct((B, S, C), F32), compiler_params=_cp(),
        )(x, w, b)

    def bwd_call(x, w, b, dy):
        def body(x_ref, w_ref, b_ref, dy_ref, dx_ref, dw_ref):
            xv = x_ref[0]
            wv = w_ref[...]
            u = pre_act(xv, wv, b_ref[...])
            sg = _sigmoid(u)
            du = dy_ref[0] * (sg * (1.0 + u * (1.0 - sg)))
            dx = jnp.zeros_like(xv)
            for k in range(SSD_K):
                dx = dx + wv[k:k + 1, :] * shifted(du, pad - k)
                dw_ref[0, k:k + 1, :] = jnp.sum(du * shifted(xv, k - pad), axis=0, keepdims=True)
            dw_ref[0, SSD_K:SSD_K + 1, :] = jnp.sum(du, axis=0, keepdims=True)
            dw_ref[0, SSD_K + 1:8, :] = jnp.zeros((8 - SSD_K - 1, tc), F32)
            dx_ref[0] = dx

        return pl.pallas_call(
            body, name=name + "_bwd", grid=(B, C // tc), in_specs=[xs, ws, bs, xs],
            out_specs=[xs, pl.BlockSpec((1, 8, tc), lambda bi, j: (bi, 0, j))],
            out_shape=[jax.ShapeDtypeStruct((B, S, C), F32), jax.ShapeDtypeStruct((B, 8, C), F32)],
            compiler_params=_cp(),
        )(x, w, b, dy)

    @jax.custom_vjp
    def f(x, w, b):
        return fwd_call(x, w, b)

    def fwd(x, w, b):
        return fwd_call(x, w, b), (x, w, b)

    def bwd(res, dy):
        x, w, b = res
        dx, dwb = bwd_call(x, w, b, dy)
        dwb = jnp.sum(dwb, axis=0)
        return dx, dwb[:SSD_K], dwb[SSD_K:SSD_K + 1]

    f.defvjp(fwd, bwd)
    return f(x, w, b)


def _softplus(x):
    return jnp.maximum(x, 0.0) + jnp.log1p(jnp.exp(-jnp.abs(x)))


def _ssd_prep(raw, raw_t, brow, arow, bcol, acol, rev):
    li = lax.broadcasted_iota(jnp.int32, (CHUNK, CHUNK), 0)
    ki = lax.broadcasted_iota(jnp.int32, (CHUNK, CHUNK), 1)
    later = (li <= ki) if rev else (li >= ki)
    dt = _softplus(raw + brow)
    a = dt * arow
    cs = _dotf(later.astype(F32), a)
    tot = jnp.sum(a, axis=0, keepdims=True)
    a_t = _softplus(raw_t + bcol) * acol
    earlier = (li >= ki) if rev else (li <= ki)
    cs_t = _dotf(a_t, earlier.astype(F32))
    return dt, a, cs, tot, cs_t, later


def _lane_pick(mat, j):
    lane = lax.broadcasted_iota(jnp.int32, mat.shape, 1)
    return jnp.sum(jnp.where(lane == j, mat, 0.0), axis=1, keepdims=True)


def _head_sum(t, first):
    s0 = jnp.sum(jnp.where(first, t, 0.0), axis=1, keepdims=True)
    s1 = jnp.sum(jnp.where(first, 0.0, t), axis=1, keepdims=True)
    return s0, s1


def ssd_scan(xbc, raw, brow, arow, *, rev, name):
    B, S, _ = xbc.shape
    NC = S // CHUNK
    off = SSD_H if rev else 0
    n_dt = 2 * SSD_H

    def chunk_of(c):
        return (NC - 1 - c) if rev else c

    def specs(cmap):
        return dict(
            x=pl.BlockSpec((1, CHUNK, SSD_INNER), lambda b, c: (b, cmap(c), 0)),
            bm=pl.BlockSpec((1, CHUNK, 2 * SSD_N), lambda b, c: (b, cmap(c), SSD_INNER // (2 * SSD_N))),
            cm=pl.BlockSpec((1, CHUNK, 2 * SSD_N), lambda b, c: (b, cmap(c), SSD_INNER // (2 * SSD_N) + 1)),
            raw=pl.BlockSpec((1, CHUNK, LANES), lambda b, c: (b, cmap(c), 0)),
            raw_t=pl.BlockSpec((1, n_dt, CHUNK), lambda b, c: (b, 0, cmap(c))),
            row=pl.BlockSpec((1, LANES), lambda b, c: (0, 0)),
            colv=pl.BlockSpec((n_dt, 1), lambda b, c: (0, 0)),
            hs=pl.BlockSpec((1, 1, N_PAIR, SSD_N, LANES), lambda b, c: (b, cmap(c), 0, 0, 0)),
        )

    def head_terms(prep, j, first_dummy=None):
        dt, a, cs, tot, cs_t, later = prep
        cs_c = _lane_pick(cs, j)
        cs_r = cs_t[j:j + 1, :]
        dt_c = _lane_pick(dt, j)
        tot_j = _lane_pick(tot, j)
        L = jnp.exp(jnp.where(later, cs_c - cs_r, -1e30))
        return cs_c, cs_r, dt_c, tot_j, L

    def fwd_call(xbc, raw, raw_t, brow, arow, bcol, acol):
        def body(x_ref, bm_ref, cm_ref, raw_ref, rawt_ref, brow_ref, arow_ref, bcol_ref, acol_ref,
                 y_ref, hs_ref, st_ref):
            @pl.when(pl.program_id(1) == 0)
            def _():
                st_ref[...] = jnp.zeros_like(st_ref)

            prep = _ssd_prep(raw_ref[0], rawt_ref[0], brow_ref[...], arow_ref[...],
                             bcol_ref[...], acol_ref[...], rev)
            lane = lax.broadcasted_iota(jnp.int32, (CHUNK, LANES), 1)
            first = lane < SSD_P
            for g in range(SSD_G):
                Bg = _bf(bm_ref[0, :, g * SSD_N:(g + 1) * SSD_N])
                Cg = _bf(cm_ref[0, :, g * SSD_N:(g + 1) * SSD_N])
                G = _dot(Cg, Bg, NT)
                for pp in range(N_PAIR // SSD_G):
                    pi = g * (N_PAIR // SSD_G) + pp
                    c0, _, d0, t0, L0 = head_terms(prep, off + 2 * pi)
                    c1, _, d1, t1, L1 = head_terms(prep, off + 2 * pi + 1)
                    xd = x_ref[0, :, pi * LANES:(pi + 1) * LANES] * jnp.where(first, d0, d1)
                    xdb = _bf(xd)
                    y = jnp.where(first, _dot(_bf(G * L0), xdb, NN), _dot(_bf(G * L1), xdb, NN))
                    dec = jnp.where(first, jnp.exp(t0 - c0), jnp.exp(t1 - c1))
                    h_prev = st_ref[pi]
                    hs_ref[0, 0, pi] = h_prev
                    y = y + _dot(Cg, _bf(h_prev), NN) * jnp.where(first, jnp.exp(c0), jnp.exp(c1))
                    y_ref[0, :, pi * LANES:(pi + 1) * LANES] = y
                    etot = jnp.where(first[:1], jnp.exp(t0), jnp.exp(t1))
                    st_ref[pi] = h_prev * etot + _dot(Bg, _bf(xd * dec), TN)

        sp = specs(chunk_of)
        return pl.pallas_call(
            body, name=name + "_fwd", grid=(B, NC),
            in_specs=[sp["x"], sp["bm"], sp["cm"], sp["raw"], sp["raw_t"], sp["row"], sp["row"],
                      sp["colv"], sp["colv"]],
            out_specs=[sp["x"], sp["hs"]],
            out_shape=[jax.ShapeDtypeStruct((B, S, SSD_INNER), F32),
                       jax.ShapeDtypeStruct((B, NC, N_PAIR, SSD_N, LANES), F32)],
            scratch_shapes=[pltpu.VMEM((N_PAIR, SSD_N, LANES), F32)],
            compiler_params=_cp(),
        )(xbc, xbc, xbc, raw, raw_t, brow, arow, bcol, acol)

    def bwd_call(xbc, raw, raw_t, brow, arow, bcol, acol, hs, dy):
        def body(x_ref, bm_ref, cm_ref, raw_ref, rawt_ref, brow_ref, arow_ref, bcol_ref, acol_ref,
                 hs_ref, dy_ref, dxbc_ref, draw_ref, da_ref, dst_ref):
            @pl.when(pl.program_id(1) == 0)
            def _():
                dst_ref[...] = jnp.zeros_like(dst_ref)
                da_ref[...] = jnp.zeros_like(da_ref)

            raw_v = raw_ref[0]
            prep = _ssd_prep(raw_v, rawt_ref[0], brow_ref[...], arow_ref[...],
                             bcol_ref[...], acol_ref[...], rev)
            dt, a, cs, tot, cs_t, later = prep
            li = lax.broadcasted_iota(jnp.int32, (CHUNK, CHUNK), 0)
            ki = lax.broadcasted_iota(jnp.int32, (CHUNK, CHUNK), 1)
            later_t = (li >= ki) if rev else (li <= ki)
            lane = lax.broadcasted_iota(jnp.int32, (CHUNK, LANES), 1)
            first = lane < SSD_P
            dcs_all = jnp.zeros((CHUNK, LANES), F32)
            ddt_all = jnp.zeros((CHUNK, LANES), F32)
            dtot_all = jnp.zeros((1, LANES), F32)
            for g in range(SSD_G):
                Bg = _bf(bm_ref[0, :, g * SSD_N:(g + 1) * SSD_N])
                Cg = _bf(cm_ref[0, :, g * SSD_N:(g + 1) * SSD_N])
                G = _dot(Cg, Bg, NT)
                Gt = _dot(Bg, Cg, NT)
                dG = jnp.zeros((CHUNK, CHUNK), F32)
                dB = jnp.zeros((CHUNK, SSD_N), F32)
                dC = jnp.zeros((CHUNK, SSD_N), F32)
                for pp in range(N_PAIR // SSD_G):
                    pi = g * (N_PAIR // SSD_G) + pp
                    j0, j1 = off + 2 * pi, off + 2 * pi + 1
                    c0, r0, d0, t0, L0 = head_terms(prep, j0)
                    c1, r1, d1, t1, L1 = head_terms(prep, j1)
                    Lt0 = jnp.exp(jnp.where(later_t, r0 - c0, -1e30))
                    Lt1 = jnp.exp(jnp.where(later_t, r1 - c1, -1e30))
                    xv = x_ref[0, :, pi * LANES:(pi + 1) * LANES]
                    dtp = jnp.where(first, d0, d1)
                    xd = xv * dtp
                    xdb = _bf(xd)
                    dyv = dy_ref[0, :, pi * LANES:(pi + 1) * LANES]
                    dyb = _bf(dyv)
                    dec = jnp.where(first, jnp.exp(t0 - c0), jnp.exp(t1 - c1))
                    ecs = jnp.where(first, jnp.exp(c0), jnp.exp(c1))
                    et0, et1 = jnp.exp(t0), jnp.exp(t1)
                    etot = jnp.where(first[:1], et0, et1)
                    h_prev = hs_ref[0, 0, pi]
                    hpb = _bf(h_prev)
                    dhn = dst_ref[pi]
                    dhb = _bf(dhn)
                    W0, W1 = G * L0, G * L1
                    Wt0, Wt1 = Gt * Lt0, Gt * Lt1
                    bdh = _dot(Bg, dhb, NN)
                    dxd = jnp.where(first, _dot(_bf(Wt0), dyb, NN), _dot(_bf(Wt1), dyb, NN)) + bdh * dec
                    dy0 = _bf(jnp.where(first, dyv, 0.0))
                    dy1 = _bf(jnp.where(first, 0.0, dyv))
                    Q0, Q1 = _dot(dy0, xdb, NT), _dot(dy1, xdb, NT)
                    Qt0, Qt1 = _dot(xdb, dy0, NT), _dot(xdb, dy1, NT)
                    dG = dG + Q0 * L0 + Q1 * L1
                    dcs0 = (jnp.sum(Q0 * W0, axis=1, keepdims=True)
                            - jnp.sum(Qt0 * Wt0, axis=1, keepdims=True))
                    dcs1 = (jnp.sum(Q1 * W1, axis=1, keepdims=True)
                            - jnp.sum(Qt1 * Wt1, axis=1, keepdims=True))
                    dye = dyv * ecs
                    dyeb = _bf(dye)
                    s0, s1 = _head_sum(dye * _dot(Cg, hpb, NN), first)
                    dcs0, dcs1 = dcs0 + s0, dcs1 + s1
                    dC = dC + _dot(dyeb, hpb, NT)
                    dB = dB + _dot(_bf(xd * dec), dhb, NT)
                    u0, u1 = _head_sum(xd * bdh * dec, first)
                    dcs0, dcs1 = dcs0 - u0, dcs1 - u1
                    w = jnp.sum(dhn * h_prev, axis=0, keepdims=True)
                    w0, w1 = _head_sum(w, first[:1])
                    dt0 = jnp.sum(u0, axis=0, keepdims=True) + et0 * w0
                    dt1 = jnp.sum(u1, axis=0, keepdims=True) + et1 * w1
                    dst_ref[pi] = _dot(Cg, dyeb, TN) + dhn * etot
                    q0, q1 = _head_sum(dxd * xv, first)
                    dxbc_ref[0, :, pi * LANES:(pi + 1) * LANES] = dxd * dtp
                    dcs_all = dcs_all + jnp.where(lane == j0, dcs0, 0.0) + jnp.where(lane == j1, dcs1, 0.0)
                    ddt_all = ddt_all + jnp.where(lane == j0, q0, 0.0) + jnp.where(lane == j1, q1, 0.0)
                    dtot_all = (dtot_all + jnp.where(lane[:1] == j0, dt0, 0.0)
                                + jnp.where(lane[:1] == j1, dt1, 0.0))
                dGb = _bf(dG)
                dC = dC + _dot(dGb, Bg, NN)
                dB = dB + _dot(dGb, Cg, TN)
                dxbc_ref[0, :, SSD_INNER + g * SSD_N:SSD_INNER + (g + 1) * SSD_N] = dB
                dxbc_ref[0, :, SSD_INNER + (SSD_G + g) * SSD_N:SSD_INNER + (SSD_G + g + 1) * SSD_N] = dC
            da = _dotf(later_t.astype(F32), dcs_all) + dtot_all
            ddt = ddt_all + da * arow_ref[...]
            da_ref[0] += jnp.sum(da * dt, axis=0, keepdims=True)
            draw_ref[0] = ddt * _sigmoid(raw_v + brow_ref[...])

        def rchunk(c):
            return c if rev else (NC - 1 - c)

        sp = specs(rchunk)
        full = pl.BlockSpec((1, CHUNK, SSD_CONV_DIM), lambda b, c: (b, rchunk(c), 0))
        return pl.pallas_call(
            body, name=name + "_bwd", grid=(B, NC),
            in_specs=[sp["x"], sp["bm"], sp["cm"], sp["raw"], sp["raw_t"], sp["row"], sp["row"],
                      sp["colv"], sp["colv"], sp["hs"], sp["x"]],
            out_specs=[full, sp["raw"], pl.BlockSpec((1, 1, LANES), lambda b, c: (b, 0, 0))],
            out_shape=[jax.ShapeDtypeStruct((B, S, SSD_CONV_DIM), F32),
                       jax.ShapeDtypeStruct((B, S, LANES), F32),
                       jax.ShapeDtypeStruct((B, 1, LANES), F32)],
            scratch_shapes=[pltpu.VMEM((N_PAIR, SSD_N, LANES), F32)],
            compiler_params=_cp(),
        )(xbc, xbc, xbc, raw, raw_t, brow, arow, bcol, acol, hs, dy)

    def aux(raw, brow, arow):
        raw_t = jnp.swapaxes(raw[:, :, :n_dt], 1, 2)
        return raw_t, brow[0, :n_dt][:, None], arow[0, :n_dt][:, None]

    @jax.custom_vjp
    def f(xbc, raw, brow, arow):
        raw_t, bcol, acol = aux(raw, brow, arow)
        return fwd_call(xbc, raw, raw_t, brow, arow, bcol, acol)[0]

    def fwd(xbc, raw, brow, arow):
        raw_t, bcol, acol = aux(raw, brow, arow)
        y, hs = fwd_call(xbc, raw, raw_t, brow, arow, bcol, acol)
        return y, (xbc, raw, brow, arow, hs)

    def bwd(res, dy):
        xbc, raw, brow, arow, hs = res
        raw_t, bcol, acol = aux(raw, brow, arow)
        dxbc, draw, da = bwd_call(xbc, raw, raw_t, brow, arow, bcol, acol, hs, dy)
        dbrow = jnp.sum(draw, axis=(0, 1))[None, :]
        return dxbc, draw, dbrow, jnp.sum(da, axis=0)

    f.defvjp(fwd, bwd)
    return f(xbc, raw, brow, arow)


def ssd_out(yf, yb, xbc, z, dsk, g, *, name):
    T, W = yf.shape
    gs = W // SSD_G
    tr = _tile(T, 512, 8)
    row = pl.BlockSpec((tr, W), lambda i: (i, 0))
    vec = pl.BlockSpec((1, W), lambda i: (0, 0))

    def normed(yv, gv):
        outs, rs = [], []
        for i in range(SSD_G):
            sl = slice(i * gs, (i + 1) * gs)
            r = lax.rsqrt(jnp.mean(yv[:, sl] * yv[:, sl], axis=-1, keepdims=True) + EPS)
            rs.append(r)
            outs.append(yv[:, sl] * r)
        return outs, rs

    def fwd_call(yf, yb, xbc, z, dsk, g):
        def body(yf_ref, yb_ref, xs_ref, z_ref, dsk_ref, g_ref, o_ref):
            zv = z_ref[...]
            yv = (yf_ref[...] + yb_ref[...] + xs_ref[...] * dsk_ref[...]) * (zv * _sigmoid(zv))
            outs, _ = normed(yv, g_ref[...])
            for i in range(SSD_G):
                sl = slice(i * gs, (i + 1) * gs)
                o_ref[:, sl] = (outs[i] * g_ref[:, sl]).astype(o_ref.dtype)

        return pl.pallas_call(
            body, name=name + "_fwd", grid=(T // tr,), in_specs=[row, row, row, row, vec, vec],
            out_specs=row, out_shape=jax.ShapeDtypeStruct((T, W), BF16), compiler_params=_cp(),
        )(yf, yb, xbc, z, dsk, g)

    def bwd_call(yf, yb, xbc, z, dsk, g, do):
        def body(yf_ref, yb_ref, xs_ref, z_ref, dsk_ref, g_ref, do_ref, dy_ref, dxs_ref, dz_ref,
                 ddsk_ref, dg_ref):
            @pl.when(pl.program_id(0) == 0)
            def _():
                ddsk_ref[...] = jnp.zeros_like(ddsk_ref)
                dg_ref[...] = jnp.zeros_like(dg_ref)

            zv = z_ref[...]
            sg = _sigmoid(zv)
            sz = zv * sg
            xs = xs_ref[...]
            pre = yf_ref[...] + yb_ref[...] + xs * dsk_ref[...]
            yv = pre * sz
            outs, rs = normed(yv, g_ref[...])
            for i in range(SSD_G):
                sl = slice(i * gs, (i + 1) * gs)
                dov = do_ref[:, sl].astype(F32)
                xn = outs[i]
                dxn = dov * g_ref[:, sl]
                dyv = rs[i] * (dxn - xn * jnp.mean(dxn * xn, axis=-1, keepdims=True))
                dg_ref[:, sl] += jnp.sum(dov * xn, axis=0, keepdims=True)
                dpre = dyv * sz[:, sl]
                dy_ref[:, sl] = dpre
                dxs_ref[:, sl] = dpre * dsk_ref[:, sl]
                ddsk_ref[:, sl] += jnp.sum(dpre * xs[:, sl], axis=0, keepdims=True)
                dz_ref[:, sl] = dyv * pre[:, sl] * (sg[:, sl] * (1.0 + zv[:, sl] * (1.0 - sg[:, sl])))

        o = jax.ShapeDtypeStruct((T, W), F32)
        v = jax.ShapeDtypeStruct((1, W), F32)
        return pl.pallas_call(
            body, name=name + "_bwd", grid=(T // tr,), in_specs=[row, row, row, row, vec, vec, row],
            out_specs=[row, row, row, vec, vec], out_shape=[o, o, o, v, v], compiler_params=_cp(),
        )(yf, yb, xbc, z, dsk, g, do)

    @jax.custom_vjp
    def f(yf, yb, xbc, z, dsk, g):
        return fwd_call(yf, yb, xbc, z, dsk, g)

    def fwd(yf, yb, xbc, z, dsk, g):
        return fwd_call(yf, yb, xbc, z, dsk, g), (yf, yb, xbc, z, dsk, g)

    def bwd(res, do):
        dy, dxs, dz, ddsk, dg = bwd_call(*res, do)
        dxbc = jnp.pad(dxs, ((0, 0), (0, res[2].shape[1] - W)))
        return dy, dy, dxbc, dz, ddsk, dg

    f.defvjp(fwd, bwd)
    return f(yf, yb, xbc, z, dsk, g)


def swiglu(gu, *, name):
    T, F2 = gu.shape
    Fh = F2 // 2
    tr, tf = _tile(T, 512, 8), _tile(Fh, 512)
    nf = Fh // tf
    gs = pl.BlockSpec((tr, tf), lambda i, j: (i, j))
    us = pl.BlockSpec((tr, tf), lambda i, j: (i, j + nf))

    def fwd_call(gu):
        def body(g_ref, u_ref, o_ref):
            gv = g_ref[...].astype(F32)
            o_ref[...] = (gv * _sigmoid(gv) * u_ref[...].astype(F32)).astype(o_ref.dtype)

        return pl.pallas_call(
            body, name=name + "_fwd", grid=(T // tr, nf), in_specs=[gs, us], out_specs=gs,
            out_shape=jax.ShapeDtypeStruct((T, Fh), BF16), compiler_params=_cp(),
        )(gu, gu)

    def bwd_call(gu, da):
        def body(g_ref, u_ref, da_ref, dgu_ref):
            j = pl.program_id(1)
            gv = g_ref[...].astype(F32)
            uv = u_ref[...].astype(F32)
            dav = da_ref[...].astype(F32)
            sg = _sigmoid(gv)

            @pl.when(j < nf)
            def _():
                dgu_ref[...] = (dav * uv * (sg * (1.0 + gv * (1.0 - sg)))).astype(dgu_ref.dtype)

            @pl.when(j >= nf)
            def _():
                dgu_ref[...] = (dav * gv * sg).astype(dgu_ref.dtype)

        gsel = pl.BlockSpec((tr, tf), lambda i, j: (i, j % nf))
        usel = pl.BlockSpec((tr, tf), lambda i, j: (i, j % nf + nf))
        return pl.pallas_call(
            body, name=name + "_bwd", grid=(T // tr, 2 * nf), in_specs=[gsel, usel, gsel],
            out_specs=pl.BlockSpec((tr, tf), lambda i, j: (i, j)),
            out_shape=jax.ShapeDtypeStruct((T, F2), BF16), compiler_params=_cp(),
        )(gu, gu, da)

    @jax.custom_vjp
    def f(gu):
        return fwd_call(gu)

    def fwd(gu):
        return fwd_call(gu), gu

    def bwd(gu, da):
        return (bwd_call(gu, da),)

    f.defvjp(fwd, bwd)
    return f(gu)


def gated_residual(x, gate, y, *, name):
    B, S, D = x.shape
    ts = _tile(S, 256, 8)
    row = pl.BlockSpec((1, ts, D), lambda b, j: (b, j, 0))
    per_b = pl.BlockSpec((1, 1, D), lambda b, j: (b, 0, 0))

    def fwd_call(x, gate, y):
        def body(x_ref, gt_ref, y_ref, o_ref):
            o_ref[0] = x_ref[0] + gt_ref[0] * y_ref[0]

        return pl.pallas_call(
            body, name=name + "_fwd", grid=(B, S // ts), in_specs=[row, per_b, row], out_specs=row,
            out_shape=jax.ShapeDtypeStruct((B, S, D), F32), compiler_params=_cp(),
        )(x, gate, y)

    def bwd_call(gate, y, g):
        def body(gt_ref, y_ref, g_ref, dy_ref, dgt_ref):
            @pl.when(pl.program_id(1) == 0)
            def _():
                dgt_ref[...] = jnp.zeros_like(dgt_ref)

            gv = g_ref[0]
            dy_ref[0] = gt_ref[0] * gv
            dgt_ref[0] += jnp.sum(gv * y_ref[0], axis=0, keepdims=True)

        return pl.pallas_call(
            body, name=name + "_bwd", grid=(B, S // ts), in_specs=[per_b, row, row],
            out_specs=[row, per_b],
            out_shape=[jax.ShapeDtypeStruct((B, S, D), F32), jax.ShapeDtypeStruct((B, 1, D), F32)],
            compiler_params=_cp(),
        )(gate, y, g)

    @jax.custom_vjp
    def f(x, gate, y):
        return fwd_call(x, gate, y)

    def fwd(x, gate, y):
        return fwd_call(x, gate, y), (gate, y)

    def bwd(res, g):
        dy, dgate = bwd_call(res[0], res[1], g)
        return g, dgate, dy

    f.defvjp(fwd, bwd)
    return f(x, gate, y)


def res_norm(x, gate, y, g, scale, shift, *, name):
    B, S, D = x.shape
    ts = _tile(S, 256, 8)
    row = pl.BlockSpec((1, ts, D), lambda b, j: (b, j, 0))
    per_b = pl.BlockSpec((1, 1, D), lambda b, j: (b, 0, 0))
    gspec = pl.BlockSpec((1, D), lambda b, j: (0, 0))

    def fwd_call(x, gate, y, g, scale, shift):
        def body(x_ref, gt_ref, y_ref, g_ref, sc_ref, sh_ref, xo_ref, h_ref):
            xv = x_ref[0] + gt_ref[0] * y_ref[0]
            xo_ref[0] = xv
            r = lax.rsqrt(jnp.mean(xv * xv, axis=-1, keepdims=True) + EPS)
            h_ref[0] = (xv * r * g_ref[...] * (1.0 + sc_ref[0]) + sh_ref[0]).astype(h_ref.dtype)

        return pl.pallas_call(
            body, name=name + "_fwd", grid=(B, S // ts),
            in_specs=[row, per_b, row, gspec, per_b, per_b], out_specs=[row, row],
            out_shape=[jax.ShapeDtypeStruct((B, S, D), F32), jax.ShapeDtypeStruct((B, S, D), BF16)],
            compiler_params=_cp(),
        )(x, gate, y, g, scale, shift)

    def bwd_call(xn, g, scale, gate, y, dh, dxn):
        def body(x_ref, g_ref, sc_ref, gt_ref, y_ref, dh_ref, dxn_ref,
                 dx_ref, dy_ref, dg_ref, dsc_ref, dsh_ref, dgt_ref):
            @pl.when(pl.program_id(1) == 0)
            def _():
                for ref in (dg_ref, dsc_ref, dsh_ref, dgt_ref):
                    ref[...] = jnp.zeros_like(ref)

            xv = x_ref[0]
            dh = dh_ref[0].astype(F32)
            gv = g_ref[...]
            r = lax.rsqrt(jnp.mean(xv * xv, axis=-1, keepdims=True) + EPS)
            xh = xv * r
            dyv = dh * (1.0 + sc_ref[0])
            dxh = dyv * gv
            dx = dxn_ref[0] + r * (dxh - xh * jnp.mean(dxh * xh, axis=-1, keepdims=True))
            dx_ref[0] = dx
            dy_ref[0] = gt_ref[0] * dx
            dgt_ref[0] += jnp.sum(dx * y_ref[0], axis=0, keepdims=True)
            dg_ref[0] += jnp.sum(dyv * xh, axis=0, keepdims=True)
            dsc_ref[0] += jnp.sum(dh * xh * gv, axis=0, keepdims=True)
            dsh_ref[0] += jnp.sum(dh, axis=0, keepdims=True)

        big = jax.ShapeDtypeStruct((B, S, D), F32)
        vec = jax.ShapeDtypeStruct((B, 1, D), F32)
        return pl.pallas_call(
            body, name=name + "_bwd", grid=(B, S // ts),
            in_specs=[row, gspec, per_b, per_b, row, row, row],
            out_specs=[row, row, per_b, per_b, per_b, per_b],
            out_shape=[big, big, vec, vec, vec, vec], compiler_params=_cp(),
        )(xn, g, scale, gate, y, dh, dxn)

    @jax.custom_vjp
    def f(x, gate, y, g, scale, shift):
        return tuple(fwd_call(x, gate, y, g, scale, shift))

    def fwd(x, gate, y, g, scale, shift):
        xn, h = fwd_call(x, gate, y, g, scale, shift)
        return (xn, h), (xn, g, scale, gate, y)

    def bwd(res, cts):
        xn, g, scale, gate, y = res
        dx, dy, dg, dsc, dsh, dgt = bwd_call(xn, g, scale, gate, y, cts[1], cts[0])
        return dx, dgt, dy, jnp.sum(dg, axis=0), dsc, dsh

    f.defvjp(fwd, bwd)
    return f(x, gate, y, g, scale, shift)


def final_loss(x, g, target, *, name):
    T, D = x.shape
    tr = _tile(T, 256, 8)
    row = pl.BlockSpec((tr, D), lambda i: (i, 0))
    vec = pl.BlockSpec((1, D), lambda i: (0, 0))

    def fwd_call(x, g, target):
        def body(x_ref, g_ref, t_ref, o_ref):
            @pl.when(pl.program_id(0) == 0)
            def _():
                o_ref[...] = jnp.zeros_like(o_ref)

            xv = x_ref[...]
            r = lax.rsqrt(jnp.mean(xv * xv, axis=-1, keepdims=True) + EPS)
            e = xv * r * g_ref[...] - t_ref[...]
            o_ref[...] += jnp.sum(e * e, axis=0, keepdims=True)

        part = pl.pallas_call(
            body, name=name + "_fwd", grid=(T // tr,), in_specs=[row, vec, row], out_specs=vec,
            out_shape=jax.ShapeDtypeStruct((1, D), F32), compiler_params=_cp(),
        )(x, g, target)
        return (0.5 / D) * jnp.sum(part)

    def bwd_call(x, g, target, ct):
        def body(x_ref, g_ref, t_ref, ct_ref, dx_ref, dg_ref):
            @pl.when(pl.program_id(0) == 0)
            def _():
                dg_ref[...] = jnp.zeros_like(dg_ref)

            xv = x_ref[...]
            gv = g_ref[...]
            r = lax.rsqrt(jnp.mean(xv * xv, axis=-1, keepdims=True) + EPS)
            xn = xv * r
            dy = (xn * gv - t_ref[...]) * (ct_ref[...] * (1.0 / D))
            dxn = dy * gv
            dx_ref[...] = r * (dxn - xn * jnp.mean(dxn * xn, axis=-1, keepdims=True))
            dg_ref[...] += jnp.sum(dy * xn, axis=0, keepdims=True)

        return pl.pallas_call(
            body, name=name + "_bwd", grid=(T // tr,),
            in_specs=[row, vec, row, pl.BlockSpec((1, 1), lambda i: (0, 0))], out_specs=[row, vec],
            out_shape=[jax.ShapeDtypeStruct((T, D), F32), jax.ShapeDtypeStruct((1, D), F32)],
            compiler_params=_cp(),
        )(x, g, target, ct)

    @jax.custom_vjp
    def f(x, g, target):
        return fwd_call(x, g, target)

    def fwd(x, g, target):
        return fwd_call(x, g, target), (x, g, target)

    def bwd(res, ct):
        x, g, target = res
        dx, dg = bwd_call(x, g, target, jnp.reshape(ct, (1, 1)).astype(F32))
        return dx, dg, jnp.zeros_like(target)

    f.defvjp(fwd, bwd)
    return f(x, g, target)


def adamw(w, g, m, v, *, name, carry=()):
    L, R, C = w.shape
    tr = _tile(R, 512, 8)
    spec = pl.BlockSpec((1, tr, C), lambda l, i: (l, i, 0))
    c1 = 1.0 / (1.0 - ADAM_B1 ** ADAM_STEP)
    c2 = 1.0 / (1.0 - ADAM_B2 ** ADAM_STEP)

    def body(w_ref, g_ref, m_ref, v_ref, d_ref, nm_ref, nv_ref):
        gv = g_ref[...]
        nm = ADAM_B1 * m_ref[...] + (1.0 - ADAM_B1) * gv
        nv = ADAM_B2 * v_ref[...] + (1.0 - ADAM_B2) * (gv * gv)
        nm_ref[...] = nm
        nv_ref[...] = nv
        d_ref[...] = -ADAM_LR * ((nm * c1) / (jnp.sqrt(nv * c2) + ADAM_EPS) + ADAM_WD * w_ref[...])

    o = jax.ShapeDtypeStruct((L, R, C), F32)
    return carry_call(
        body, name=name, grid=(L, R // tr), in_specs=[spec] * 4, out_specs=[spec] * 3,
        out_shape=[o, o, o], carry=carry,
    )(w, g, m, v)


def _position():
    x, y, c = lax.axis_index("x"), lax.axis_index("y"), lax.axis_index("c")
    return x, y, c


def ag_routine(shard, cols=False):
    R, C = shard.shape
    assert not cols or C % LANES == 0

    def parts(ins, outs, send_sems, recv_sems, local_sems):
        (x_ref,), (out_ref,) = ins, outs
        x, y, c = _position()
        me, sibling = (x, y, c), (x, y, 1 - c)
        chips = [(1 - x, y), (x, 1 - y), (1 - x, 1 - y)]

        def block(px, py, pc):
            idx = 4 * px + 2 * py + pc
            if cols:
                return out_ref.at[:, pl.ds(pl.multiple_of(idx * C, LANES), C)]
            return out_ref.at[idx]

        def copy(k, blk, to, src=None):
            return pltpu.make_async_remote_copy(
                src_ref=block(*blk) if src is None else src, dst_ref=block(*blk),
                send_sem=send_sems.at[k], recv_sem=recv_sems.at[k],
                device_id=to, device_id_type=pl.DeviceIdType.MESH)

        mine = pltpu.make_async_copy(x_ref, block(*me), local_sems.at[0])
        first = [copy(0, me, sibling, src=x_ref)]
        first += [copy(1 + j, me, (*chip, c), src=x_ref) for j, chip in enumerate(chips)]
        passed = [copy(4 + j, (*chip, c), sibling) for j, chip in enumerate(chips)]
        return me, sibling, c, chips, copy, mine, first, passed

    def start(*refs):
        me, sibling, c, chips, copy, mine, first, passed = parts(*refs)
        mine.start()
        for cp in first:
            cp.start()

    def finish(*refs):
        me, sibling, c, chips, copy, mine, first, passed = parts(*refs)
        for j, chip in enumerate(chips):
            copy(1 + j, (*chip, c), me).wait_recv()
            passed[j].start()
        copy(0, sibling, me).wait_recv()
        for j, chip in enumerate(chips):
            copy(4 + j, (*chip, 1 - c), me).wait_recv()
        for cp in first + passed:
            cp.wait_send()
        mine.wait()

    out = jax.ShapeDtypeStruct((R, N_DEV * C) if cols else (N_DEV, R, C), shard.dtype)
    return dict(ins=[shard], outs=[out], n_sem=7, n_local=1, start=start, finish=finish)


def all_gather(shard, *, name, cols=False):
    return comm_call(ag_routine(shard, cols), name=name)[0]


def carry_call(body, *, name, grid, in_specs, out_specs, out_shape, scratch_shapes=(), carry=(),
               dims=None):
    in_specs, out_specs, out_shape = list(in_specs), list(out_specs), list(out_shape)
    scratch_shapes = list(scratch_shapes)
    if not carry:
        call = pl.pallas_call(body, name=name, grid=grid, in_specs=in_specs, out_specs=out_specs,
                              out_shape=out_shape, scratch_shapes=scratch_shapes,
                              compiler_params=_cp(dimension_semantics=dims) if dims else _cp())
        return lambda *args: list(call(*args))
    n_in, n_out, n_scr = len(in_specs), len(out_specs), len(scratch_shapes)
    c_ins = [a for r in carry for a in r["ins"]]
    c_outs = [o for r in carry for o in r["outs"]]
    sems = []
    for r in carry:
        sems += [pltpu.SemaphoreType.DMA((r["n_sem"],)), pltpu.SemaphoreType.DMA((r["n_sem"],)),
                 pltpu.SemaphoreType.DMA((r["n_local"],))]

    def wrapped(*refs):
        refs = list(refs)
        ins, refs = refs[:n_in], refs[n_in:]
        cin, refs = refs[:len(c_ins)], refs[len(c_ins):]
        outs, refs = refs[:n_out], refs[n_out:]
        cout, refs = refs[:len(c_outs)], refs[len(c_outs):]
        scr, csem = refs[:n_scr], refs[n_scr:]
        ids = [pl.program_id(i) for i in range(len(grid))]
        first = functools.reduce(jnp.logical_and, [i == 0 for i in ids])
        last = functools.reduce(jnp.logical_and, [i == g - 1 for i, g in zip(ids, grid)])

        def each(which):
            io = oo = 0
            for j, r in enumerate(carry):
                r[which](cin[io:io + len(r["ins"])], cout[oo:oo + len(r["outs"])], *csem[3 * j:3 * j + 3])
                io += len(r["ins"])
                oo += len(r["outs"])

        @pl.when(first)
        def _():
            each("start")

        body(*ins, *outs, *scr)

        @pl.when(last)
        def _():
            each("finish")

    any_spec = pl.BlockSpec(memory_space=pl.ANY)
    call = pl.pallas_call(
        wrapped, name=name, grid=grid, in_specs=in_specs + [any_spec] * len(c_ins),
        out_specs=out_specs + [any_spec] * len(c_outs), out_shape=out_shape + c_outs,
        scratch_shapes=scratch_shapes + sems, compiler_params=_cp())
    return lambda *args: list(call(*args, *c_ins))


N_CHIP = 4


def rs_pair_routine(g, cols):
    if cols:
        R, C = g.shape[0], g.shape[1] // N_DEV
        assert C % LANES == 0
    else:
        _, R, C = g.shape

    def blk(ref, idx):
        if cols:
            return ref.at[:, pl.ds(pl.multiple_of(idx * C, LANES), C)]
        return ref.at[idx]

    def copies(ins, outs, send_sems, recv_sems, local_sems):
        (g_ref,), (got_ref,) = ins, outs
        x, y, c = _position()
        local, remote = [], []
        for q in range(N_CHIP):
            remote.append(pltpu.make_async_remote_copy(
                src_ref=blk(g_ref, 2 * q + 1 - c), dst_ref=got_ref.at[q],
                send_sem=send_sems.at[q], recv_sem=recv_sems.at[q],
                device_id=(x, y, 1 - c), device_id_type=pl.DeviceIdType.MESH))
        return local, remote

    def start(*refs):
        local, remote = copies(*refs)
        for cp in remote + local:
            cp.start()

    def finish(*refs):
        local, remote = copies(*refs)
        for cp in remote:
            cp.wait_recv()
        for cp in remote:
            cp.wait_send()
        for cp in local:
            cp.wait()

    o = jax.ShapeDtypeStruct((N_CHIP, R, C), g.dtype)
    return dict(ins=[g], outs=[o], n_sem=N_CHIP, n_local=1, start=start, finish=finish)


def rs_chip_routine(h):
    _, R, C = h.shape
    RELATIONS = ((0, 1), (1, 0), (1, 1))

    def copies(ins, outs, send_sems, recv_sems, local_sems):
        (h_ref,), (out_ref,) = ins, outs
        x, y, c = _position()
        local, remote = [], []
        for k, (fx, fy) in enumerate(RELATIONS):
            px = (1 - x) if fx else x
            py = (1 - y) if fy else y
            remote.append(pltpu.make_async_remote_copy(
                src_ref=h_ref.at[2 * px + py], dst_ref=out_ref.at[k],
                send_sem=send_sems.at[k], recv_sem=recv_sems.at[k],
                device_id=(px, py, c), device_id_type=pl.DeviceIdType.MESH))
        return local, remote

    def start(*refs):
        local, remote = copies(*refs)
        for cp in remote + local:
            cp.start()

    def finish(*refs):
        local, remote = copies(*refs)
        for cp in remote:
            cp.wait_recv()
        for cp in remote:
            cp.wait_send()
        for cp in local:
            cp.wait()

    return dict(ins=[h], outs=[jax.ShapeDtypeStruct((N_CHIP - 1, R, C), h.dtype)], n_sem=3, n_local=1,
                start=start, finish=finish)


def comm_call(routine, *, name):
    n_in, n_out = len(routine["ins"]), len(routine["outs"])

    def body(*refs):
        ins, outs, sems = refs[:n_in], refs[n_in:n_in + n_out], refs[n_in + n_out:]
        routine["start"](ins, outs, *sems)
        routine["finish"](ins, outs, *sems)

    any_spec = pl.BlockSpec(memory_space=pl.ANY)
    return pl.pallas_call(
        body, name=name, out_shape=routine["outs"],
        in_specs=[any_spec] * n_in, out_specs=[any_spec] * n_out,
        scratch_shapes=[pltpu.SemaphoreType.DMA((routine["n_sem"],)), pltpu.SemaphoreType.DMA((routine["n_sem"],)),
                        pltpu.SemaphoreType.DMA((routine["n_local"],))],
    )(*routine["ins"])


def add_own(g, got, core, *, cols, name):
    n, R, C = got.shape
    tr = _tile(R, 256, 8)
    if cols:
        gspec = pl.BlockSpec((tr, C), lambda q, i, c_ref: (i, 2 * q + c_ref[0]))
    else:
        gspec = pl.BlockSpec((1, tr, C), lambda q, i, c_ref: (2 * q + c_ref[0], i, 0))
    spec = pl.BlockSpec((1, tr, C), lambda q, i, c_ref: (q, i, 0))

    def body(c_ref, g_ref, b_ref, o_ref):
        gv = g_ref[...] if cols else g_ref[0]
        o_ref[0] = (gv.astype(F32) + b_ref[0].astype(F32)).astype(o_ref.dtype)

    return pl.pallas_call(
        body, name=name,
        grid_spec=pltpu.PrefetchScalarGridSpec(num_scalar_prefetch=1, grid=(n, R // tr),
                                               in_specs=[gspec, spec], out_specs=spec),
        out_shape=jax.ShapeDtypeStruct((n, R, C), g.dtype), compiler_params=_cp(),
    )(core, g, got)


def sum_chips(h, got, chip, *, name):
    _, R, C = h.shape
    tr = _tile(R, 256, 8)

    def body(q_ref, h_ref, g_ref, o_ref):
        acc = h_ref[0].astype(F32)
        for k in range(N_CHIP - 1):
            acc = acc + g_ref[k].astype(F32)
        o_ref[...] = acc

    return pl.pallas_call(
        body, name=name,
        grid_spec=pltpu.PrefetchScalarGridSpec(
            num_scalar_prefetch=1, grid=(R // tr,),
            in_specs=[pl.BlockSpec((1, tr, C), lambda i, q_ref: (q_ref[0], i, 0)),
                      pl.BlockSpec((N_CHIP - 1, tr, C), lambda i, q_ref: (0, i, 0))],
            out_specs=pl.BlockSpec((tr, C), lambda i, q_ref: (i, 0))),
        out_shape=jax.ShapeDtypeStruct((R, C), F32), compiler_params=_cp(),
    )(chip, h, got)


def reduce_scatter(g, *, cols, name):
    x, y, c = _position()
    core = jnp.reshape(c, (1,)).astype(jnp.int32)
    chip = jnp.reshape(2 * x + y, (1,)).astype(jnp.int32)
    got = comm_call(rs_pair_routine(g, cols), name=name + "_pair")[0]
    h = add_own(g, got, core, cols=cols, name=name + "_add")
    return sum_chips(h, comm_call(rs_chip_routine(h), name=name + "_chip")[0], chip, name=name + "_sum")


def sum_blocks(stack, *, name):
    n, R, C = stack.shape
    tr = _tile(R, 256, 8)

    def body(x_ref, o_ref):
        acc = x_ref[0].astype(F32)
        for i in range(1, n):
            acc = acc + x_ref[i].astype(F32)
        o_ref[...] = acc

    return pl.pallas_call(
        body, name=name, grid=(R // tr,),
        in_specs=[pl.BlockSpec((n, tr, C), lambda i: (0, i, 0))],
        out_specs=pl.BlockSpec((tr, C), lambda i: (i, 0)),
        out_shape=jax.ShapeDtypeStruct((R, C), F32), compiler_params=_cp(),
    )(stack)


PACK_COLS = 1024
PACK_ROW_MULT = 8


def _pack(arrays, dtype):
    flat = jnp.concatenate([a.reshape(-1).astype(dtype) for a in arrays])
    n = flat.shape[0]
    unit = PACK_COLS * PACK_ROW_MULT
    padded = -(-n // unit) * unit
    return jnp.pad(flat, (0, padded - n)).reshape(padded // PACK_COLS, PACK_COLS)


def _unpack(packed, shapes):
    flat = packed.reshape(-1)
    out, o = [], 0
    for s in shapes:
        n = int(np.prod(s))
        out.append(flat[o:o + n].reshape(s))
        o += n
    return out


def fsdp_cols(shard, *, name):
    K, n = shard.shape
    npad = -(-n // LANES) * LANES

    @jax.custom_vjp
    def f(p):
        p = jnp.pad(p, ((0, 0), (0, npad - n))) if npad != n else p
        return all_gather(p.astype(BF16), cols=True, name=name + "_ag")

    def fwd(p):
        return f(p), None

    def bwd(_, g):
        d = reduce_scatter(g, cols=True, name=name + "_rs")
        return (d[:, :n] if npad != n else d,)

    f.defvjp(fwd, bwd)
    return f(shard)


def fsdp_rows(shard, *, name):
    k, N = shard.shape

    @jax.custom_vjp
    def f(p):
        return all_gather(p.astype(BF16), name=name + "_ag").reshape(N_DEV * k, N)

    def fwd(p):
        return f(p), None

    def bwd(_, g):
        return (reduce_scatter(g.reshape(N_DEV, k, N), cols=False, name=name + "_rs"),)

    f.defvjp(fwd, bwd)
    return f(shard)


def _unpad_cols(w, n):
    K = w.shape[0]
    npad = w.shape[1] // N_DEV
    if npad == n:
        return w
    return w.reshape(K, N_DEV, npad)[:, :, :n].reshape(K, N_DEV * n)


def gather_rows(part, me, *, name):
    rows, n = part.shape
    per = rows // N_DEV

    @jax.custom_vjp
    def f(part):
        full = all_gather(part, name=name + "_fwd")
        mine = lax.dynamic_slice_in_dim(full, me * per, per, axis=1)
        return jnp.swapaxes(mine, 0, 1).reshape(per, N_DEV * n)

    def fwd(part):
        return f(part), None

    def bwd(_, g):
        full = all_gather(g, name=name + "_bwd")
        mine = lax.dynamic_slice_in_dim(full, me * n, n, axis=2)
        return (mine.reshape(rows, n),)

    f.defvjp(fwd, bwd)
    return f(part)


def _seg_layout():
    offs = np.concatenate([[0], np.cumsum(IN_SPLITS)])
    cols, widths, leads = [], [], []
    for s in SEG_ORDER:
        cols.append((int(offs[s]), int(offs[s + 1])))
        widths.append(SEG_PAD.get(s, IN_SPLITS[s]))
        leads.append(SEG_LEAD.get(s, 0))
    return cols, widths, leads


def _arrange_w_in(w, n):
    D = w.shape[0]
    npad = w.shape[1] // N_DEV
    cols, widths, leads = _seg_layout()

    def pieces_of(a, b):
        out = []
        for d in range(N_DEV):
            lo, hi = max(a, n * d), min(b, n * (d + 1))
            if lo < hi:
                out.append((d, lo, hi))
        return out

    @jax.custom_vjp
    def f(w):
        parts = []
        for (a, b), wd, ld in zip(cols, widths, leads):
            if ld:
                parts.append(jnp.zeros((D, ld), w.dtype))
            parts += [w[:, npad * d + lo - n * d:npad * d + hi - n * d] for d, lo, hi in pieces_of(a, b)]
            if wd != ld + b - a:
                parts.append(jnp.zeros((D, wd - ld - (b - a)), w.dtype))
        parts.append(jnp.zeros((D, IN_WIDTH - sum(widths)), w.dtype))
        return jnp.concatenate(parts, axis=1)

    def fwd(w):
        return f(w), None

    def bwd(_, g):
        offs = np.concatenate([[0], np.cumsum(widths)])
        runs = []
        for ((a, b), off, ld) in zip(cols, offs[:-1], leads):
            runs += [(lo, int(off) + ld + lo - a, hi - lo) for _, lo, hi in pieces_of(a, b)]
        runs.sort()
        parts, d_next = [], 1
        for lo, o, ln in runs:
            while lo >= n * d_next:
                parts.append(jnp.zeros((D, npad - n), g.dtype))
                d_next += 1
            parts.append(g[:, o:o + ln])
        parts.append(jnp.zeros((D, npad - n), g.dtype))
        return (jnp.concatenate(parts, axis=1),)

    f.defvjp(fwd, bwd)
    return f(w)


def split_cols(proj, widths):
    @jax.custom_vjp
    def f(p):
        outs, o = [], 0
        for wd in widths:
            outs.append(p[:, o:o + wd])
            o += wd
        return tuple(outs)

    def fwd(p):
        return f(p), None

    def bwd(_, gs):
        rest = proj.shape[1] - sum(widths)
        tail = [jnp.zeros((proj.shape[0], rest), proj.dtype)] if rest else []
        return (jnp.concatenate(list(gs) + tail, axis=1),)

    f.defvjp(fwd, bwd)
    return f(proj)


BIG = ("w_in", "w_uq", "w_ukv", "conv_w", "w_out", "w_gate_up", "w_down")
SMALL = ("b_ada", "norm1_g", "norm2_g", "q_norm_g", "k_norm_g", "mla_q_norm_g", "mla_kv_norm_g",
         "conv_b", "dt_bias", "a_log", "d_skip", "ssd_norm_g", "final_norm_g")
WEIGHTS = ("w_ada", "b_ada", "norm1_g", "norm2_g", "w_in", "q_norm_g", "k_norm_g", "mla_q_norm_g",
           "w_uq", "mla_kv_norm_g", "w_ukv", "conv_w", "conv_b", "dt_bias", "a_log", "d_skip",
           "ssd_norm_g", "w_out", "w_gate_up", "w_down", "final_norm_g")


PRE = ("w_in", "w_uq", "w_ukv", "conv_w")


def _layer(l, x, h, mods, W, shards, nxt, P, tabs):
    B, S, D = x.shape
    T = B * S
    nm = f"l{l}_"
    shift1, scale1, gate1, shift2, scale2, gate2 = mods[l]
    cos_a, sin_a, cos_b, sin_b, cos_k, sin_k = tabs

    w_in = _arrange_w_in(W["w_in"], IN_COLS // N_DEV)
    proj, (w_out_full,) = linear(h.reshape(T, D), w_in, out_dtype=F32, name=nm + "in",
                                 carry=((shards["w_out"], False, "w_out"),))
    q_a, k_a, v_a, cq, ckv, z, xbc, kpe, dtr = split_cols(proj, _seg_layout()[1])

    qn = group_rmsnorm(q_a, jnp.tile(P["q_norm_g"][l], GQA_H)[None], gs=HEAD, out_dtype=F32, name=nm + "qnorm")
    kn = group_rmsnorm(k_a, jnp.tile(P["k_norm_g"][l], GQA_KV)[None], gs=HEAD, out_dtype=F32, name=nm + "knorm")
    qr = rope(qn.reshape(B, S, -1), cos_a[:, :GQA_H * HEAD], sin_a[:, :GQA_H * HEAD], d=HEAD // 4, name=nm + "qrope")
    kr = rope(kn.reshape(B, S, -1), cos_a[:, :GQA_KV * HEAD], sin_a[:, :GQA_KV * HEAD], d=HEAD // 4, name=nm + "krope")
    gu_rows = shards["w_gate_up"].shape[0] // 2
    o_a, (w_gu_top,) = attention(qr, kr, v_a.reshape(B, S, -1), H=GQA_H, Hkv=GQA_KV, dk=HEAD, dv=HEAD,
                                 scale=HEAD ** -0.5, name=nm + "gqa",
                                 carry=((shards["w_gate_up"][:gu_rows], True, "w_gate_up_top"),))

    slot_pad = MLA_DK - MLA_NOPE - MLA_ROPE
    w_uq = jnp.pad(W["w_uq"].reshape(MLA_QL, MLA_H, MLA_NOPE + MLA_ROPE), ((0, 0), (0, 0), (0, slot_pad)))
    w_uq = w_uq.reshape(MLA_QL, MLA_H * MLA_DK)
    w_ukv = W["w_ukv"].reshape(MLA_KVL, MLA_H, MLA_NOPE + MLA_V)
    w_ukv = jnp.concatenate(
        [jnp.pad(w_ukv[:, :, :MLA_NOPE], ((0, 0), (0, 0), (0, MLA_DK - MLA_NOPE))).reshape(MLA_KVL, -1),
         w_ukv[:, :, MLA_NOPE:].reshape(MLA_KVL, -1)], axis=1)
    cqn = group_rmsnorm(cq, P["mla_q_norm_g"][l][None], gs=MLA_QL, out_dtype=BF16, name=nm + "cqnorm")
    ckvn = group_rmsnorm(ckv, P["mla_kv_norm_g"][l][None], gs=MLA_KVL, out_dtype=BF16, name=nm + "ckvnorm")
    qb = linear(cqn, w_uq, out_dtype=F32, name=nm + "uq")
    kvb = linear(ckvn, w_ukv, out_dtype=F32, name=nm + "ukv")
    k_slots, v_b = split_cols(kvb, (MLA_H * MLA_DK, MLA_H * MLA_V))
    q_cat = rope(qb.reshape(B, S, -1), cos_b, sin_b, d=MLA_ROPE // 4, name=nm + "qpe_rope")
    k_pe = rope(kpe.reshape(B, S, -1), cos_k, sin_k, d=MLA_ROPE // 4, name=nm + "kpe_rope")
    o_b, (w_gu_bottom,) = attention(q_cat, k_slots.reshape(B, S, -1), v_b.reshape(B, S, -1), H=MLA_H, Hkv=MLA_H,
                                    dk=MLA_DK, dv=MLA_V, scale=(MLA_NOPE + MLA_ROPE) ** -0.5, name=nm + "mla",
                                    k_shared=k_pe,
                                    carry=((shards["w_gate_up"][gu_rows:], True, "w_gate_up_bottom"),))
    w_gu_full = jnp.concatenate([w_gu_top, w_gu_bottom], axis=0)

    xact = conv_silu(xbc.reshape(B, S, -1), W["conv_w"].astype(F32), P["conv_b"][l][None], name=nm + "conv")
    brow = jnp.pad(P["dt_bias"][l].reshape(1, -1), ((0, 0), (0, LANES - 2 * SSD_H)))
    arow = jnp.pad(-jnp.exp(P["a_log"][l].reshape(1, -1)), ((0, 0), (0, LANES - 2 * SSD_H)))
    raw = dtr.reshape(B, S, LANES)
    y_f = ssd_scan(xact, raw, brow, arow, rev=False, name=nm + "ssd_f")
    y_b = ssd_scan(xact, raw, brow, arow, rev=True, name=nm + "ssd_b")
    dsk = jnp.repeat(P["d_skip"][l], SSD_P)[None]
    o_c = ssd_out(y_f.reshape(T, -1), y_b.reshape(T, -1), xact.reshape(T, -1), z, dsk,
                  P["ssd_norm_g"][l][None], name=nm + "ssd_out")

    o = jnp.concatenate([o_a.reshape(T, -1).astype(BF16), o_b.reshape(T, -1).astype(BF16), o_c], axis=-1)
    mix = linear(o, w_out_full, out_dtype=F32, name=nm + "out")
    x, h = res_norm(x, gate1, mix.reshape(B, S, D), P["norm2_g"][l][None], scale2, shift2, name=nm + "res1_norm2")

    y, nxt_full = ffn(h.reshape(T, D), w_gu_full, shards["w_down"], name=nm + "ffn",
                      carry=tuple((nxt[n], True, "next_" + n) for n in PRE) if nxt else ())
    if l + 1 < len(mods):
        x, h = res_norm(x, gate2, y.reshape(B, S, D), P["norm1_g"][l + 1][None], mods[l + 1][1], mods[l + 1][0],
                        name=nm + "res2_norm1")
    else:
        x, h = gated_residual(x, gate2, y.reshape(B, S, D), name=nm + "res2"), None
    return x, h, nxt_full


def kernel(x, c, w_ada, b_ada, norm1_g, norm2_g, w_in, q_norm_g, k_norm_g, mla_q_norm_g, w_uq, mla_kv_norm_g, w_ukv, conv_w, conv_b, dt_bias, a_log, d_skip, ssd_norm_g, w_out, w_gate_up, w_down, final_norm_g, loss_target, m_w_ada, m_b_ada, m_norm1_g, m_norm2_g, m_w_in, m_q_norm_g, m_k_norm_g, m_mla_q_norm_g, m_w_uq, m_mla_kv_norm_g, m_w_ukv, m_conv_w, m_conv_b, m_dt_bias, m_a_log, m_d_skip, m_ssd_norm_g, m_w_out, m_w_gate_up, m_w_down, m_final_norm_g, v_w_ada, v_b_ada, v_norm1_g, v_norm2_g, v_w_in, v_q_norm_g, v_k_norm_g, v_mla_q_norm_g, v_w_uq, v_mla_kv_norm_g, v_w_ukv, v_conv_w, v_conv_b, v_dt_bias, v_a_log, v_d_skip, v_ssd_norm_g, v_w_out, v_w_gate_up, v_w_down, v_final_norm_g):
    args = dict(locals())
    weights = {n: args[n] for n in WEIGHTS}
    moments_m = {n: args["m_" + n] for n in WEIGHTS}
    moments_v = {n: args["v_" + n] for n in WEIGHTS}
    B, S, D = x.shape
    L = w_ada.shape[0]
    T = B * S
    px, py, pc = _position()
    me = 4 * px + 2 * py + pc
    small_shapes = [weights[n].shape for n in SMALL]

    tabs = (*rope_tables(S, HEAD, GQA_H * HEAD),
            *rope_tables(S, MLA_ROPE, MLA_H * MLA_DK, slot=MLA_DK, lead=MLA_NOPE),
            *rope_tables(S, MLA_ROPE, MLA_DK, slot=MLA_DK, lead=MLA_NOPE))
    c_all = all_gather(c, name="gather_c").reshape(N_DEV * B, D)

    pre0 = [all_gather(_shard_prep(weights[n][0], True), cols=True, name=f"l0_{n}_ag") for n in PRE]

    def local_loss(big, w_ada_s, small, x, pre):
        P = dict(zip(SMALL, small))
        mods = []
        for l in range(L):
            part = linear(c_all, w_ada_s[l], out_dtype=F32, a_silu=True, name=f"l{l}_ada")
            mod = gather_rows(part, me, name=f"l{l}_mod") + P["b_ada"][l][None]
            mods.append([m[:, None, :] for m in jnp.split(mod, 6, axis=-1)])
        h = rmsmod(x, P["norm1_g"][0][None], mods[0][1], mods[0][0], name="l0_norm1")
        for l in range(L):
            W = {n: full if n == "w_in" else _unpad_cols(full, big[n].shape[2]) for n, full in zip(PRE, pre)}
            shards = {n: big[n][l] for n in ("w_out", "w_gate_up", "w_down")}
            nxt = {n: big[n][l + 1] for n in PRE} if l + 1 < L else None
            x, h, pre = _layer(l, x, h, mods, W, shards, nxt, P, tabs)
        return final_loss(x.reshape(T, D), P["final_norm_g"][None], loss_target.reshape(T, D), name="loss")

    big = {n: weights[n] for n in BIG}
    small = tuple(weights[n] for n in SMALL)
    loss, (g_big, g_ada, g_small, grad_x, g_pre0) = jax.value_and_grad(local_loss, argnums=(0, 1, 2, 3, 4))(
        big, w_ada, small, x, pre0)
    loss = lax.psum(loss, ("x", "y", "c"))

    grads = dict(g_big)
    grads["w_ada"] = g_ada
    g_small_sum = sum_blocks(all_gather(_pack(g_small, F32), name="small_grads_ag"), name="small_grads_sum")
    grads.update(zip(SMALL, _unpack(g_small_sum, small_shapes)))

    delta, new_m, new_v = {}, {}, {}
    pre0_h = [_rs_begin(g, True, weights[n].shape[1:], f"l0_{n}") for n, g in zip(PRE, g_pre0)]
    delta["w_ada"], new_m["w_ada"], new_v["w_ada"], *pre0_got = adamw(
        w_ada, g_ada, m_w_ada, v_w_ada, name="adamw_w_ada", carry=[rs_chip_routine(h) for h in pre0_h])
    for n, h, got in zip(PRE, pre0_h, pre0_got):
        g0 = _rs_end(h, got, True, weights[n].shape[1:], f"l0_{n}")
        grads[n] = lax.dynamic_update_slice(grads[n], g0[None], (0, 0, 0))
    for n in BIG:
        delta[n], new_m[n], new_v[n] = adamw(weights[n], grads[n], moments_m[n], moments_v[n], name="adamw_" + n)
    d_, m_, v_ = adamw(_pack([weights[n] for n in SMALL], F32)[None], g_small_sum[None],
                       _pack([moments_m[n] for n in SMALL], F32)[None], _pack([moments_v[n] for n in SMALL], F32)[None],
                       name="adamw_small")
    for tgt, packed in ((delta, d_), (new_m, m_), (new_v, v_)):
        tgt.update(zip(SMALL, _unpack(packed[0], small_shapes)))

    return (loss, grad_x, *[grads[n] for n in WEIGHTS], *[delta[n] for n in WEIGHTS],
            *[new_m[n] for n in WEIGHTS], *[new_v[n] for n in WEIGHTS])
```

```python
import functools
import math

import jax
import jax.numpy as jnp
import numpy as np
from jax import lax
from jax.experimental import pallas as pl
from jax.experimental.pallas import tpu as pltpu

F32 = jnp.float32
BF16 = jnp.bfloat16
N_DEV = 8
EPS = 1e-6
ROPE_THETA = 10000.0
GRID_W = 64

GQA_H, GQA_KV, HEAD = 6, 2, 128
MLA_H, MLA_QL, MLA_KVL, MLA_NOPE, MLA_ROPE, MLA_V = 4, 512, 256, 128, 64, 128
MLA_DK = 256
SSD_H, SSD_P, SSD_G, SSD_N, SSD_K, CHUNK = 12, 64, 2, 128, 5, 128
SSD_INNER = SSD_H * SSD_P
SSD_CONV_DIM = SSD_INNER + 2 * SSD_G * SSD_N
N_PAIR = SSD_H // 2
LANES = 128
IN_SPLITS = (768, 256, 256, 512, 256, 64, 768, 1280, 24)
IN_COLS = sum(IN_SPLITS)
SEG_ORDER = (0, 1, 2, 3, 4, 6, 7, 5, 8)
SEG_PAD = {5: 256, 8: 128}
SEG_LEAD = {5: 128}
IN_WIDTH = 4608

ADAM_LR, ADAM_B1, ADAM_B2, ADAM_EPS, ADAM_WD, ADAM_STEP = 0.001, 0.9, 0.999, 1e-08, 0.01, 10
VMEM_LIMIT = 56 * 1024 * 1024
MM_TM, MM_TN, MM_TK = 1024, 1408, 2048
MM_TK_HALVES = 2816


def _cp(**kw):
    return pltpu.CompilerParams(vmem_limit_bytes=VMEM_LIMIT, **kw)


def _tile(dim, cap, mult=128):
    if dim <= cap:
        return dim
    best = None
    t = mult
    while t <= cap:
        if dim % t == 0:
            best = t
        t += mult
    assert best is not None, (dim, cap)
    return best


def _sigmoid(x):
    return 1.0 / (1.0 + jnp.exp(-x))


def _dot(a, b, dims):
    return lax.dot_general(a, b, (dims, ((), ())), preferred_element_type=F32)


NN = ((1,), (0,))
NT = ((1,), (1,))
TN = ((0,), (0,))


def _dotf(a, b, dims=NN):
    return lax.dot_general(a, b, (dims, ((), ())), preferred_element_type=F32,
                           precision=lax.Precision.HIGHEST)


def _bf(x):
    return x.astype(BF16)


def mm(a, b, *, ta=False, tb=False, out_dtype=F32, a_silu=False, name, carry=(),
       a_halves=False, b_halves=False):
    if a_halves:
        assert not ta
        M, K = a.shape[1], 2 * a.shape[2]
    elif ta:
        K, M = a.shape
    else:
        M, K = a.shape
    if b_halves:
        assert not tb
        K2, N = b.shape[1], 2 * b.shape[2]
    elif tb:
        N, K2 = b.shape
    else:
        K2, N = b.shape
    assert K == K2, (a.shape, b.shape, ta, tb)
    tm = _tile(M, MM_TM)
    tn = _tile(N // 2, MM_TN) if b_halves else _tile(N, MM_TN)
    tk = _tile(K // 2, MM_TK_HALVES) if a_halves else _tile(K, MM_TK)
    nk = K // tk
    dims = ((0 if ta else 1,), (1 if tb else 0,))

    def partial_product(a_ref, b_ref):
        av = a_ref[...]
        if a_silu:
            av = av.astype(F32)
            av = av * _sigmoid(av)
        return _dot(_bf(av), _bf(b_ref[...]), dims)

    def body_single(a_ref, b_ref, o_ref):
        o_ref[...] = partial_product(a_ref, b_ref).astype(o_ref.dtype)

    def body_acc(a_ref, b_ref, o_ref, acc_ref):
        k = pl.program_id(2)

        @pl.when(k == 0)
        def _():
            acc_ref[...] = partial_product(a_ref, b_ref)

        @pl.when(k > 0)
        def _():
            acc_ref[...] += partial_product(a_ref, b_ref)

        @pl.when(k == nk - 1)
        def _():
            o_ref[...] = acc_ref[...].astype(o_ref.dtype)

    body = body_single if nk == 1 else body_acc

    a_spec = (pl.BlockSpec((tk, tm), lambda i, j, k: (k, i)) if ta
              else pl.BlockSpec((tm, tk), lambda i, j, k: (i, k)))
    b_spec = (pl.BlockSpec((tn, tk), lambda i, j, k: (j, k)) if tb
              else pl.BlockSpec((tk, tn), lambda i, j, k: (k, j)))
    if a_halves:
        a_spec = pl.BlockSpec((None, tm, tk), lambda i, j, k: (k // (nk // 2), i, k % (nk // 2)))
    if b_halves:
        nnh = N // tn // 2
        b_spec = pl.BlockSpec((None, tk, tn), lambda i, j, k: (j // nnh, k, j % nnh))
    res = carry_call(
        body, name=name, grid=(M // tm, N // tn, nk),
        in_specs=[a_spec, b_spec],
        out_specs=[pl.BlockSpec((tm, tn), lambda i, j, k: (i, j))],
        out_shape=[jax.ShapeDtypeStruct((M, N), out_dtype)],
        scratch_shapes=[] if nk == 1 else [pltpu.VMEM((tm, tn), F32)],
        carry=carry, dims=("parallel", "parallel", "arbitrary"),
    )(a, b)
    return res if carry else res[0]


def _shard_prep(p, cols):
    if cols and p.shape[1] % LANES:
        p = jnp.pad(p, ((0, 0), (0, -p.shape[1] % LANES)))
    return p.astype(BF16)


def _full_post(full, cols):
    return full if cols else full.reshape(full.shape[0] * full.shape[1], full.shape[2])


def _rs_begin(g_full, cols, shard_shape, name):
    gg = g_full if cols else g_full.reshape(N_DEV, shard_shape[0], shard_shape[1])
    x, y, c = _position()
    got = comm_call(rs_pair_routine(gg, cols), name=name + "_pair")[0]
    return add_own(gg, got, jnp.reshape(c, (1,)).astype(jnp.int32), cols=cols, name=name + "_add")


def _rs_end(h, got, cols, shard_shape, name):
    x, y, c = _position()
    d = sum_chips(h, got, jnp.reshape(2 * x + y, (1,)).astype(jnp.int32), name=name + "_sum")
    return d[:, :shard_shape[1]] if cols else d


def linear(a, w, *, out_dtype, name, a_silu=False, carry=()):
    kinds = [(cols, s.shape, tag) for s, cols, tag in carry]

    def run(a, w, *shards):
        routines = [ag_routine(_shard_prep(s, cols), cols) for s, (cols, _, _) in zip(shards, kinds)]
        res = mm(a, w, out_dtype=out_dtype, a_silu=a_silu, name=name + "_fwd", carry=routines)
        if not routines:
            return (res,)
        return (res[0], *[_full_post(fu, cols) for fu, (cols, _, _) in zip(res[1:], kinds)])

    @jax.custom_vjp
    def f(a, w, *shards):
        return run(a, w, *shards)

    def fwd(a, w, *shards):
        return run(a, w, *shards), (a, w)

    def bwd(res, cts):
        a, w = res
        g = cts[0]
        hs = [_rs_begin(gf, cols, shp, name + "_" + tag) for gf, (cols, shp, tag) in zip(cts[1:], kinds)]
        routines = [rs_chip_routine(h) for h in hs]
        if a_silu:
            assert not routines
            da = jnp.zeros_like(a)
            gots = []
        else:
            r = mm(g, w, tb=True, out_dtype=a.dtype, name=name + "_da", carry=routines)
            da, gots = (r[0], r[1:]) if routines else (r, [])
        dw = mm(a, g, ta=True, out_dtype=w.dtype, a_silu=a_silu, name=name + "_dw")
        dsh = [_rs_end(h, got, cols, shp, name + "_" + tag)
               for h, got, (cols, shp, tag) in zip(hs, gots, kinds)]
        return (da, dw, *dsh)

    f.defvjp(fwd, bwd)
    out = f(a, w, *[s for s, _, _ in carry])
    return (out[0], list(out[1:])) if carry else out[0]


def swiglu_up(h, w_gu, *, name, carry=()):
    T, D = h.shape
    F = w_gu.shape[1] // 2
    assert D <= MM_TK
    tm, tn = _tile(T, MM_TM), _tile(F, 512)
    nf = F // tn

    def body(a_ref, bg_ref, bu_ref, gu_ref, act_ref):
        av = a_ref[...]
        g = _dot(av, bg_ref[...], NN)
        u = _dot(av, bu_ref[...], NN)
        gu_ref[0] = g.astype(gu_ref.dtype)
        gu_ref[1] = u.astype(gu_ref.dtype)
        act_ref[...] = (g * _sigmoid(g) * u).astype(act_ref.dtype)

    return carry_call(
        body, name=name, grid=(T // tm, nf),
        in_specs=[pl.BlockSpec((tm, D), lambda i, j: (i, 0)), pl.BlockSpec((D, tn), lambda i, j: (0, j)),
                  pl.BlockSpec((D, tn), lambda i, j: (0, j + nf))],
        out_specs=[pl.BlockSpec((2, tm, tn), lambda i, j: (0, i, j)), pl.BlockSpec((tm, tn), lambda i, j: (i, j))],
        out_shape=[jax.ShapeDtypeStruct((2, T, F), BF16), jax.ShapeDtypeStruct((T, F), BF16)],
        carry=carry,
    )(h, w_gu, w_gu)


def swiglu_down_bwd(g, w_dn, gu, *, name, carry=()):
    T, D = g.shape
    F = w_dn.shape[0]
    assert D <= MM_TK
    tm, tn = _tile(T, MM_TM), _tile(F, 512)

    def body(a_ref, b_ref, gu_ref, o_ref):
        dact = _dot(_bf(a_ref[...]), b_ref[...], NT)
        gv = gu_ref[0].astype(F32)
        uv = gu_ref[1].astype(F32)
        sg = _sigmoid(gv)
        o_ref[0] = (dact * uv * (sg * (1.0 + gv * (1.0 - sg)))).astype(o_ref.dtype)
        o_ref[1] = (dact * gv * sg).astype(o_ref.dtype)

    half = pl.BlockSpec((2, tm, tn), lambda i, j: (0, i, j))
    return carry_call(
        body, name=name, grid=(T // tm, F // tn),
        in_specs=[pl.BlockSpec((tm, D), lambda i, j: (i, 0)), pl.BlockSpec((tn, D), lambda i, j: (j, 0)), half],
        out_specs=[half], out_shape=[jax.ShapeDtypeStruct((2, T, F), BF16)], carry=carry,
    )(g, w_dn, gu)


def ffn(h, w_gu, w_dn_shard, *, name, carry=()):
    kinds = [(cols, s.shape, tag) for s, cols, tag in carry]
    dn_shape = w_dn_shard.shape

    def run(h, w_gu, w_dn_shard, *shards):
        gu, act, w_dn = swiglu_up(h, w_gu, name=name + "_up", carry=[ag_routine(_shard_prep(w_dn_shard, False))])
        w_dn = _full_post(w_dn, False)
        routines = [ag_routine(_shard_prep(s, cols), cols) for s, (cols, _, _) in zip(shards, kinds)]
        res = mm(act, w_dn, out_dtype=F32, name=name + "_down", carry=routines)
        y, fulls = (res[0], res[1:]) if routines else (res, [])
        return (y, *[_full_post(fu, cols) for fu, (cols, _, _) in zip(fulls, kinds)]), (h, w_gu, w_dn, gu, act)

    @jax.custom_vjp
    def f(h, w_gu, w_dn_shard, *shards):
        return run(h, w_gu, w_dn_shard, *shards)[0]

    def fwd(h, w_gu, w_dn_shard, *shards):
        return run(h, w_gu, w_dn_shard, *shards)

    def bwd(res, cts):
        h, w_gu, w_dn, gu, act = res
        g = cts[0]
        hs = [_rs_begin(gf, cols, shp, name + "_" + tag) for gf, (cols, shp, tag) in zip(cts[1:], kinds)]
        dgu, *gots = swiglu_down_bwd(g, w_dn, gu, name=name + "_down_da", carry=[rs_chip_routine(x) for x in hs])
        dw_dn = mm(act, g, ta=True, out_dtype=w_dn.dtype, name=name + "_down_dw")
        h_dn = _rs_begin(dw_dn, False, dn_shape, name + "_w_down")
        dh, got_dn = mm(dgu, w_gu, tb=True, a_halves=True, out_dtype=h.dtype, name=name + "_up_da",
                        carry=[rs_chip_routine(h_dn)])
        dw_gu = mm(h, dgu, ta=True, b_halves=True, out_dtype=w_gu.dtype, name=name + "_up_dw")
        d_dn = _rs_end(h_dn, got_dn, False, dn_shape, name + "_w_down")
        dsh = [_rs_end(x, got, cols, shp, name + "_" + tag) for x, got, (cols, shp, tag) in zip(hs, gots, kinds)]
        return (dh, dw_gu, d_dn, *dsh)

    f.defvjp(fwd, bwd)
    out = f(h, w_gu, w_dn_shard, *[s for s, _, _ in carry])
    return out[0], list(out[1:])


def rmsmod(x, g, scale, shift, *, name):
    B, S, D = x.shape
    ts = _tile(S, 256, 8)
    row = pl.BlockSpec((1, ts, D), lambda b, j: (b, j, 0))
    per_b = pl.BlockSpec((1, 1, D), lambda b, j: (b, 0, 0))
    gspec = pl.BlockSpec((1, D), lambda b, j: (0, 0))

    def fwd_call(x, g, scale, shift):
        def body(x_ref, g_ref, sc_ref, sh_ref, o_ref):
            xv = x_ref[0]
            r = lax.rsqrt(jnp.mean(xv * xv, axis=-1, keepdims=True) + EPS)
            y = xv * r * g_ref[...]
            o_ref[0] = (y * (1.0 + sc_ref[0]) + sh_ref[0]).astype(o_ref.dtype)

        return pl.pallas_call(
            body, name=name + "_fwd", grid=(B, S // ts),
            in_specs=[row, gspec, per_b, per_b], out_specs=row,
            out_shape=jax.ShapeDtypeStruct((B, S, D), BF16), compiler_params=_cp(),
        )(x, g, scale, shift)

    def bwd_call(x, g, scale, dh):
        def body(x_ref, g_ref, sc_ref, dh_ref, dx_ref, dg_ref, dsc_ref, dsh_ref):
            j = pl.program_id(1)
            xv = x_ref[0]
            dh = dh_ref[0].astype(F32)
            r = lax.rsqrt(jnp.mean(xv * xv, axis=-1, keepdims=True) + EPS)
            xn = xv * r
            gv = g_ref[...]
            dy = dh * (1.0 + sc_ref[0])
            dxn = dy * gv
            dx_ref[0] = r * (dxn - xn * jnp.mean(dxn * xn, axis=-1, keepdims=True))

            @pl.when(j == 0)
            def _():
                dg_ref[...] = jnp.zeros_like(dg_ref)
                dsc_ref[...] = jnp.zeros_like(dsc_ref)
                dsh_ref[...] = jnp.zeros_like(dsh_ref)

            dg_ref[0] += jnp.sum(dy * xn, axis=0, keepdims=True)
            dsc_ref[0] += jnp.sum(dh * xn * gv, axis=0, keepdims=True)
            dsh_ref[0] += jnp.sum(dh, axis=0, keepdims=True)

        vec = jax.ShapeDtypeStruct((B, 1, D), F32)
        return pl.pallas_call(
            body, name=name + "_bwd", grid=(B, S // ts),
            in_specs=[row, gspec, per_b, row], out_specs=[row, per_b, per_b, per_b],
            out_shape=[jax.ShapeDtypeStruct((B, S, D), F32), vec, vec, vec], compiler_params=_cp(),
        )(x, g, scale, dh)

    @jax.custom_vjp
    def f(x, g, scale, shift):
        return fwd_call(x, g, scale, shift)

    def fwd(x, g, scale, shift):
        return fwd_call(x, g, scale, shift), (x, g, scale)

    def bwd(res, dh):
        x, g, scale = res
        dx, dg, dsc, dsh = bwd_call(x, g, scale, dh)
        return dx, jnp.sum(dg, axis=0), dsc, dsh

    f.defvjp(fwd, bwd)
    return f(x, g, scale, shift)


def group_rmsnorm(x, g, *, gs, out_dtype, name):
    T, W = x.shape
    ng = W // gs
    tr = _tile(T, 512, 8)
    row = pl.BlockSpec((tr, W), lambda i: (i, 0))
    gspec = pl.BlockSpec((1, W), lambda i: (0, 0))

    def fwd_call(x, g):
        def body(x_ref, g_ref, o_ref):
            for i in range(ng):
                sl = slice(i * gs, (i + 1) * gs)
                xv = x_ref[:, sl]
                r = lax.rsqrt(jnp.mean(xv * xv, axis=-1, keepdims=True) + EPS)
                o_ref[:, sl] = (xv * r * g_ref[:, sl]).astype(o_ref.dtype)

        return pl.pallas_call(
            body, name=name + "_fwd", grid=(T // tr,), in_specs=[row, gspec], out_specs=row,
            out_shape=jax.ShapeDtypeStruct((T, W), out_dtype), compiler_params=_cp(),
        )(x, g)

    def bwd_call(x, g, dy):
        def body(x_ref, g_ref, dy_ref, dx_ref, dg_ref):
            @pl.when(pl.program_id(0) == 0)
            def _():
                dg_ref[...] = jnp.zeros_like(dg_ref)

            for i in range(ng):
                sl = slice(i * gs, (i + 1) * gs)
                xv = x_ref[:, sl]
                dyv = dy_ref[:, sl].astype(F32)
                r = lax.rsqrt(jnp.mean(xv * xv, axis=-1, keepdims=True) + EPS)
                xn = xv * r
                dxn = dyv * g_ref[:, sl]
                dx_ref[:, sl] = r * (dxn - xn * jnp.mean(dxn * xn, axis=-1, keepdims=True))
                dg_ref[:, sl] += jnp.sum(dyv * xn, axis=0, keepdims=True)

        return pl.pallas_call(
            body, name=name + "_bwd", grid=(T // tr,), in_specs=[row, gspec, row],
            out_specs=[row, gspec],
            out_shape=[jax.ShapeDtypeStruct((T, W), F32), jax.ShapeDtypeStruct((1, W), F32)],
            compiler_params=_cp(),
        )(x, g, dy)

    @jax.custom_vjp
    def f(x, g):
        return fwd_call(x, g)

    def fwd(x, g):
        return fwd_call(x, g), (x, g)

    def bwd(res, dy):
        return bwd_call(res[0], res[1], dy)

    f.defvjp(fwd, bwd)
    return f(x, g)


def rope_tables(seq_len, rot_dim, width, slot=0, lead=0):
    rows = seq_len // GRID_W
    row_idx = jnp.repeat(jnp.arange(rows), GRID_W).astype(F32)
    col_idx = jnp.tile(jnp.arange(GRID_W), rows).astype(F32)
    axis_dim = rot_dim // 2
    inv_freq = jnp.power(ROPE_THETA, -jnp.arange(0, axis_dim, 2, dtype=F32) / axis_dim)
    ang_r = row_idx[:, None] * inv_freq[None, :]
    ang_c = col_idx[:, None] * inv_freq[None, :]
    cos = jnp.concatenate([jnp.cos(ang_r), jnp.cos(ang_r), jnp.cos(ang_c), jnp.cos(ang_c)], axis=-1)
    sin = jnp.concatenate([-jnp.sin(ang_r), jnp.sin(ang_r), -jnp.sin(ang_c), jnp.sin(ang_c)], axis=-1)
    if slot:
        ones = jnp.ones((seq_len, 1), F32)
        cos = jnp.concatenate([ones * jnp.ones((1, lead)), cos, ones * jnp.ones((1, slot - lead - rot_dim))], axis=-1)
        sin = jnp.concatenate([ones * jnp.zeros((1, lead)), sin, ones * jnp.zeros((1, slot - lead - rot_dim))], axis=-1)
        rot_dim = slot
    reps = width // rot_dim
    return jnp.tile(cos, (1, reps)), jnp.tile(sin, (1, reps))


def rope(x, cos, sin, *, d, name):
    B, S, W = x.shape
    ts = _tile(S, 512, 8)
    row = pl.BlockSpec((1, ts, W), lambda b, j: (b, j, 0))
    tab = pl.BlockSpec((ts, W), lambda b, j: (j, 0))

    def call(x, inverse, nm):
        def body(x_ref, c_ref, s_ref, o_ref):
            xv = x_ref[0]
            lane = lax.broadcasted_iota(jnp.int32, xv.shape, 1)
            first = (lane // d) % 2 == 0

            def swap(v):
                return jnp.where(first, pltpu.roll(v, W - d, 1), pltpu.roll(v, d, 1))

            if inverse:
                o_ref[0] = xv * c_ref[...] + swap(xv * s_ref[...])
            else:
                o_ref[0] = xv * c_ref[...] + swap(xv) * s_ref[...]

        return pl.pallas_call(
            body, name=nm, grid=(B, S // ts), in_specs=[row, tab, tab], out_specs=row,
            out_shape=jax.ShapeDtypeStruct((B, S, W), F32), compiler_params=_cp(),
        )(x, cos, sin)

    @jax.custom_vjp
    def f(x):
        return call(x, False, name + "_fwd")

    def fwd(x):
        return call(x, False, name + "_fwd"), None

    def bwd(_, g):
        return (call(g, True, name + "_bwd"),)

    f.defvjp(fwd, bwd)
    return f(x)


def attention(q, k, v, *, H, Hkv, dk, dv, scale, name, carry=(), k_shared=None):
    B, S, _ = q.shape
    rep = H // Hkv
    tq = _tile(S, 256, 8)
    kinds = [(cols, s.shape, tag) for s, cols, tag in carry]
    shared = [] if k_shared is None else [k_shared]
    ke_spec3 = [pl.BlockSpec((1, S, dk), lambda b, h, i: (b, 0, 0))] if shared else []
    ke_spec4 = [pl.BlockSpec((1, S, dk), lambda b, g, r, i: (b, 0, 0))] if shared else []

    def keys(k_ref, ke_refs):
        return _bf(k_ref[0] + ke_refs[0][0]) if ke_refs else _bf(k_ref[0])

    def fwd_call(q, k, v, ke, routines=()):
        def body(q_ref, k_ref, v_ref, *rest):
            ke_refs, (o_ref, lse_ref) = rest[:len(shared)], rest[len(shared):]
            s = _dot(_bf(q_ref[0]), keys(k_ref, ke_refs), NT) * scale
            m = jnp.max(s, axis=-1, keepdims=True)
            p = jnp.exp(s - m)
            l = jnp.sum(p, axis=-1, keepdims=True)
            o_ref[0] = _dot(_bf(p), _bf(v_ref[0]), NN) / l
            lse_ref[0, 0] = m + jnp.log(l)

        return carry_call(
            body, name=name + "_fwd", grid=(B, H, S // tq),
            in_specs=[pl.BlockSpec((1, tq, dk), lambda b, h, i: (b, i, h)),
                      pl.BlockSpec((1, S, dk), lambda b, h, i: (b, 0, h // rep)),
                      pl.BlockSpec((1, S, dv), lambda b, h, i: (b, 0, h // rep))] + ke_spec3,
            out_specs=[pl.BlockSpec((1, tq, dv), lambda b, h, i: (b, i, h)),
                       pl.BlockSpec((1, 1, tq, 1), lambda b, h, i: (b, h, i, 0))],
            out_shape=[jax.ShapeDtypeStruct((B, S, H * dv), F32),
                       jax.ShapeDtypeStruct((B, H, S, 1), F32)],
            carry=routines,
        )(q, k, v, *ke)

    def bwd_call(q, k, v, ke, o, do, lse, routines=()):
        def body(q_ref, k_ref, v_ref, *rest):
            ke_refs, rest = rest[:len(shared)], rest[len(shared):]
            o_ref, do_ref, lse_ref, dq_ref, dk_ref, dv_ref = rest[:6]

            @pl.when((pl.program_id(2) == 0) & (pl.program_id(3) == 0))
            def _():
                dk_ref[...] = jnp.zeros_like(dk_ref)
                dv_ref[...] = jnp.zeros_like(dv_ref)

            if shared:
                dke_ref = rest[6]

                @pl.when((pl.program_id(1) == 0) & (pl.program_id(2) == 0) & (pl.program_id(3) == 0))
                def _():
                    dke_ref[...] = jnp.zeros_like(dke_ref)

            qb = _bf(q_ref[0])
            kb = keys(k_ref, ke_refs)
            dov = do_ref[0]
            dob = _bf(dov)
            s = _dot(qb, kb, NT) * scale
            p = jnp.exp(s - lse_ref[0, 0])
            delta = jnp.sum(dov * o_ref[0], axis=-1, keepdims=True)
            dp = _dot(dob, _bf(v_ref[0]), NT)
            dsb = _bf(p * (dp - delta))
            dq_ref[0] = _dot(dsb, kb, NN) * scale
            dkv = _dot(dsb, qb, TN) * scale
            dk_ref[0] += dkv
            if shared:
                dke_ref[0] += dkv
            dv_ref[0] += _dot(_bf(p), dob, TN)

        qs = pl.BlockSpec((1, tq, dk), lambda b, g, r, i: (b, i, g * rep + r))
        os_ = pl.BlockSpec((1, tq, dv), lambda b, g, r, i: (b, i, g * rep + r))
        ks = pl.BlockSpec((1, S, dk), lambda b, g, r, i: (b, 0, g))
        vs = pl.BlockSpec((1, S, dv), lambda b, g, r, i: (b, 0, g))
        col = pl.BlockSpec((1, 1, tq, 1), lambda b, g, r, i: (b, g * rep + r, i, 0))
        return carry_call(
            body, name=name + "_bwd", grid=(B, Hkv, rep, S // tq),
            in_specs=[qs, ks, vs] + ke_spec4 + [os_, os_, col], out_specs=[qs, ks, vs] + ke_spec4,
            out_shape=[jax.ShapeDtypeStruct(q.shape, F32), jax.ShapeDtypeStruct(k.shape, F32),
                       jax.ShapeDtypeStruct(v.shape, F32)] + [jax.ShapeDtypeStruct(x.shape, F32) for x in ke],
            carry=routines,
        )(q, k, v, *ke, o, do, lse)

    n_ke = len(shared)

    def run(q, k, v, *rest):
        ke, shards = rest[:n_ke], rest[n_ke:]
        routines = [ag_routine(_shard_prep(s, cols), cols) for s, (cols, _, _) in zip(shards, kinds)]
        o, lse, *fulls = fwd_call(q, k, v, ke, routines)
        return (o, *[_full_post(fu, cols) for fu, (cols, _, _) in zip(fulls, kinds)]), lse

    @jax.custom_vjp
    def f(q, k, v, *rest):
        return run(q, k, v, *rest)[0]

    def fwd(q, k, v, *rest):
        outs, lse = run(q, k, v, *rest)
        return outs, (q, k, v, rest[:n_ke], outs[0], lse)

    def bwd(res, cts):
        q, k, v, ke, o, lse = res
        hs = [_rs_begin(gf, cols, shp, name + "_" + tag) for gf, (cols, shp, tag) in zip(cts[1:], kinds)]
        dq, dk_, dv_, *more = bwd_call(q, k, v, ke, o, cts[0], lse, [rs_chip_routine(h) for h in hs])
        dke, gots = more[:n_ke], more[n_ke:]
        dsh = [_rs_end(h, got, cols, shp, name + "_" + tag)
               for h, got, (cols, shp, tag) in zip(hs, gots, kinds)]
        return (dq, dk_, dv_, *dke, *dsh)

    f.defvjp(fwd, bwd)
    out = f(q, k, v, *shared, *[s for s, _, _ in carry])
    return (out[0], list(out[1:])) if carry else out[0]


def conv_silu(x, w, b, *, name):
    B, S, C = x.shape
    tc = _tile(C, 256)
    pad = SSD_K // 2
    xs = pl.BlockSpec((1, S, tc), lambda bi, j: (bi, 0, j))
    ws = pl.BlockSpec((SSD_K, tc), lambda bi, j: (0, j))
    bs = pl.BlockSpec((1, tc), lambda bi, j: (0, j))

    def shifted(v, off):
        if off == 0:
            return v
        t = lax.broadcasted_iota(jnp.int32, v.shape, 0)
        r = pltpu.roll(v, (-off) % S, 0)
        return jnp.where((t + off >= 0) & (t + off < S), r, 0.0)

    def pre_act(xv, wv, bv):
        u = jnp.zeros_like(xv) + bv
        for k in range(SSD_K):
            u = u + wv[k:k + 1, :] * shifted(xv, k - pad)
        return u

    def fwd_call(x, w, b):
        def body(x_ref, w_ref, b_ref, o_ref):
            u = pre_act(x_ref[0], w_ref[...], b_ref[...])
            o_ref[0] = u * _sigmoid(u)

        return pl.pallas_call(
            body, name=name + "_fwd", grid=(B, C // tc), in_specs=[xs, ws, bs], out_specs=xs,
            out_shape=jax.ShapeDtypeStruct((B, S, C), F32), compiler_params=_cp(),
        )(x, w, b)

    def bwd_call(x, w, b, dy):
        def body(x_ref, w_ref, b_ref, dy_ref, dx_ref, dw_ref):
            xv = x_ref[0]
            wv = w_ref[...]
            u = pre_act(xv, wv, b_ref[...])
            sg = _sigmoid(u)
            du = dy_ref[0] * (sg * (1.0 + u * (1.0 - sg)))
            dx = jnp.zeros_like(xv)
            for k in range(SSD_K):
                dx = dx + wv[k:k + 1, :] * shifted(du, pad - k)
                dw_ref[0, k:k + 1, :] = jnp.sum(du * shifted(xv, k - pad), axis=0, keepdims=True)
            dw_ref[0, SSD_K:SSD_K + 1, :] = jnp.sum(du, axis=0, keepdims=True)
            dw_ref[0, SSD_K + 1:8, :] = jnp.zeros((8 - SSD_K - 1, tc), F32)
            dx_ref[0] = dx

        return pl.pallas_call(
            body, name=name + "_bwd", grid=(B, C // tc), in_specs=[xs, ws, bs, xs],
            out_specs=[xs, pl.BlockSpec((1, 8, tc), lambda bi, j: (bi, 0, j))],
            out_shape=[jax.ShapeDtypeStruct((B, S, C), F32), jax.ShapeDtypeStruct((B, 8, C), F32)],
            compiler_params=_cp(),
        )(x, w, b, dy)

    @jax.custom_vjp
    def f(x, w, b):
        return fwd_call(x, w, b)

    def fwd(x, w, b):
        return fwd_call(x, w, b), (x, w, b)

    def bwd(res, dy):
        x, w, b = res
        dx, dwb = bwd_call(x, w, b, dy)
        dwb = jnp.sum(dwb, axis=0)
        return dx, dwb[:SSD_K], dwb[SSD_K:SSD_K + 1]

    f.defvjp(fwd, bwd)
    return f(x, w, b)


def _softplus(x):
    return jnp.maximum(x, 0.0) + jnp.log1p(jnp.exp(-jnp.abs(x)))


def _ssd_prep(raw, raw_t, brow, arow, bcol, acol, rev):
    li = lax.broadcasted_iota(jnp.int32, (CHUNK, CHUNK), 0)
    ki = lax.broadcasted_iota(jnp.int32, (CHUNK, CHUNK), 1)
    later = (li <= ki) if rev else (li >= ki)
    dt = _softplus(raw + brow)
    a = dt * arow
    cs = _dotf(later.astype(F32), a)
    tot = jnp.sum(a, axis=0, keepdims=True)
    a_t = _softplus(raw_t + bcol) * acol
    earlier = (li >= ki) if rev else (li <= ki)
    cs_t = _dotf(a_t, earlier.astype(F32))
    return dt, a, cs, tot, cs_t, later


def _lane_pick(mat, j):
    lane = lax.broadcasted_iota(jnp.int32, mat.shape, 1)
    return jnp.sum(jnp.where(lane == j, mat, 0.0), axis=1, keepdims=True)


def _head_sum(t, first):
    s0 = jnp.sum(jnp.where(first, t, 0.0), axis=1, keepdims=True)
    s1 = jnp.sum(jnp.where(first, 0.0, t), axis=1, keepdims=True)
    return s0, s1


def ssd_scan(xbc, raw, brow, arow, *, rev, name):
    B, S, _ = xbc.shape
    NC = S // CHUNK
    off = SSD_H if rev else 0
    n_dt = 2 * SSD_H

    def chunk_of(c):
        return (NC - 1 - c) if rev else c

    def specs(cmap):
        return dict(
            x=pl.BlockSpec((1, CHUNK, SSD_INNER), lambda b, c: (b, cmap(c), 0)),
            bm=pl.BlockSpec((1, CHUNK, 2 * SSD_N), lambda b, c: (b, cmap(c), SSD_INNER // (2 * SSD_N))),
            cm=pl.BlockSpec((1, CHUNK, 2 * SSD_N), lambda b, c: (b, cmap(c), SSD_INNER // (2 * SSD_N) + 1)),
            raw=pl.BlockSpec((1, CHUNK, LANES), lambda b, c: (b, cmap(c), 0)),
            raw_t=pl.BlockSpec((1, n_dt, CHUNK), lambda b, c: (b, 0, cmap(c))),
            row=pl.BlockSpec((1, LANES), lambda b, c: (0, 0)),
            colv=pl.BlockSpec((n_dt, 1), lambda b, c: (0, 0)),
            hs=pl.BlockSpec((1, 1, N_PAIR, SSD_N, LANES), lambda b, c: (b, cmap(c), 0, 0, 0)),
        )

    def head_terms(prep, j, first_dummy=None):
        dt, a, cs, tot, cs_t, later = prep
        cs_c = _lane_pick(cs, j)
        cs_r = cs_t[j:j + 1, :]
        dt_c = _lane_pick(dt, j)
        tot_j = _lane_pick(tot, j)
        L = jnp.exp(jnp.where(later, cs_c - cs_r, -1e30))
        return cs_c, cs_r, dt_c, tot_j, L

    def fwd_call(xbc, raw, raw_t, brow, arow, bcol, acol):
        def body(x_ref, bm_ref, cm_ref, raw_ref, rawt_ref, brow_ref, arow_ref, bcol_ref, acol_ref,
                 y_ref, hs_ref, st_ref):
            @pl.when(pl.program_id(1) == 0)
            def _():
                st_ref[...] = jnp.zeros_like(st_ref)

            prep = _ssd_prep(raw_ref[0], rawt_ref[0], brow_ref[...], arow_ref[...],
                             bcol_ref[...], acol_ref[...], rev)
            lane = lax.broadcasted_iota(jnp.int32, (CHUNK, LANES), 1)
            first = lane < SSD_P
            for g in range(SSD_G):
                Bg = _bf(bm_ref[0, :, g * SSD_N:(g + 1) * SSD_N])
                Cg = _bf(cm_ref[0, :, g * SSD_N:(g + 1) * SSD_N])
                G = _dot(Cg, Bg, NT)
                for pp in range(N_PAIR // SSD_G):
                    pi = g * (N_PAIR // SSD_G) + pp
                    c0, _, d0, t0, L0 = head_terms(prep, off + 2 * pi)
                    c1, _, d1, t1, L1 = head_terms(prep, off + 2 * pi + 1)
                    xd = x_ref[0, :, pi * LANES:(pi + 1) * LANES] * jnp.where(first, d0, d1)
                    xdb = _bf(xd)
                    y = jnp.where(first, _dot(_bf(G * L0), xdb, NN), _dot(_bf(G * L1), xdb, NN))
                    dec = jnp.where(first, jnp.exp(t0 - c0), jnp.exp(t1 - c1))
                    h_prev = st_ref[pi]
                    hs_ref[0, 0, pi] = h_prev
                    y = y + _dot(Cg, _bf(h_prev), NN) * jnp.where(first, jnp.exp(c0), jnp.exp(c1))
                    y_ref[0, :, pi * LANES:(pi + 1) * LANES] = y
                    etot = jnp.where(first[:1], jnp.exp(t0), jnp.exp(t1))
                    st_ref[pi] = h_prev * etot + _dot(Bg, _bf(xd * dec), TN)

        sp = specs(chunk_of)
        return pl.pallas_call(
            body, name=name + "_fwd", grid=(B, NC),
            in_specs=[sp["x"], sp["bm"], sp["cm"], sp["raw"], sp["raw_t"], sp["row"], sp["row"],
                      sp["colv"], sp["colv"]],
            out_specs=[sp["x"], sp["hs"]],
            out_shape=[jax.ShapeDtypeStruct((B, S, SSD_INNER), F32),
                       jax.ShapeDtypeStruct((B, NC, N_PAIR, SSD_N, LANES), F32)],
            scratch_shapes=[pltpu.VMEM((N_PAIR, SSD_N, LANES), F32)],
            compiler_params=_cp(),
        )(xbc, xbc, xbc, raw, raw_t, brow, arow, bcol, acol)

    def bwd_call(xbc, raw, raw_t, brow, arow, bcol, acol, hs, dy):
        def body(x_ref, bm_ref, cm_ref, raw_ref, rawt_ref, brow_ref, arow_ref, bcol_ref, acol_ref,
                 hs_ref, dy_ref, dxbc_ref, draw_ref, da_ref, dst_ref):
            @pl.when(pl.program_id(1) == 0)
            def _():
                dst_ref[...] = jnp.zeros_like(dst_ref)
                da_ref[...] = jnp.zeros_like(da_ref)

            raw_v = raw_ref[0]
            prep = _ssd_prep(raw_v, rawt_ref[0], brow_ref[...], arow_ref[...],
                             bcol_ref[...], acol_ref[...], rev)
            dt, a, cs, tot, cs_t, later = prep
            li = lax.broadcasted_iota(jnp.int32, (CHUNK, CHUNK), 0)
            ki = lax.broadcasted_iota(jnp.int32, (CHUNK, CHUNK), 1)
            later_t = (li >= ki) if rev else (li <= ki)
            lane = lax.broadcasted_iota(jnp.int32, (CHUNK, LANES), 1)
            first = lane < SSD_P
            dcs_all = jnp.zeros((CHUNK, LANES), F32)
            ddt_all = jnp.zeros((CHUNK, LANES), F32)
            dtot_all = jnp.zeros((1, LANES), F32)
            for g in range(SSD_G):
                Bg = _bf(bm_ref[0, :, g * SSD_N:(g + 1) * SSD_N])
                Cg = _bf(cm_ref[0, :, g * SSD_N:(g + 1) * SSD_N])
                G = _dot(Cg, Bg, NT)
                Gt = _dot(Bg, Cg, NT)
                dG = jnp.zeros((CHUNK, CHUNK), F32)
                dB = jnp.zeros((CHUNK, SSD_N), F32)
                dC = jnp.zeros((CHUNK, SSD_N), F32)
                for pp in range(N_PAIR // SSD_G):
                    pi = g * (N_PAIR // SSD_G) + pp
                    j0, j1 = off + 2 * pi, off + 2 * pi + 1
                    c0, r0, d0, t0, L0 = head_terms(prep, j0)
                    c1, r1, d1, t1, L1 = head_terms(prep, j1)
                    Lt0 = jnp.exp(jnp.where(later_t, r0 - c0, -1e30))
                    Lt1 = jnp.exp(jnp.where(later_t, r1 - c1, -1e30))
                    xv = x_ref[0, :, pi * LANES:(pi + 1) * LANES]
                    dtp = jnp.where(first, d0, d1)
                    xd = xv * dtp
                    xdb = _bf(xd)
                    dyv = dy_ref[0, :, pi * LANES:(pi + 1) * LANES]
                    dyb = _bf(dyv)
                    dec = jnp.where(first, jnp.exp(t0 - c0), jnp.exp(t1 - c1))
                    ecs = jnp.where(first, jnp.exp(c0), jnp.exp(c1))
                    et0, et1 = jnp.exp(t0), jnp.exp(t1)
                    etot = jnp.where(first[:1], et0, et1)
                    h_prev = hs_ref[0, 0, pi]
                    hpb = _bf(h_prev)
                    dhn = dst_ref[pi]
                    dhb = _bf(dhn)
                    W0, W1 = G * L0, G * L1
                    Wt0, Wt1 = Gt * Lt0, Gt * Lt1
                    bdh = _dot(Bg, dhb, NN)
                    dxd = jnp.where(first, _dot(_bf(Wt0), dyb, NN), _dot(_bf(Wt1), dyb, NN)) + bdh * dec
                    dy0 = _bf(jnp.where(first, dyv, 0.0))
                    dy1 = _bf(jnp.where(first, 0.0, dyv))
                    Q0, Q1 = _dot(dy0, xdb, NT), _dot(dy1, xdb, NT)
                    Qt0, Qt1 = _dot(xdb, dy0, NT), _dot(xdb, dy1, NT)
                    dG = dG + Q0 * L0 + Q1 * L1
                    dcs0 = (jnp.sum(Q0 * W0, axis=1, keepdims=True)
                            - jnp.sum(Qt0 * Wt0, axis=1, keepdims=True))
                    dcs1 = (jnp.sum(Q1 * W1, axis=1, keepdims=True)
                            - jnp.sum(Qt1 * Wt1, axis=1, keepdims=True))
                    dye = dyv * ecs
                    dyeb = _bf(dye)
                    s0, s1 = _head_sum(dye * _dot(Cg, hpb, NN), first)
                    dcs0, dcs1 = dcs0 + s0, dcs1 + s1
                    dC = dC + _dot(dyeb, hpb, NT)
                    dB = dB + _dot(_bf(xd * dec), dhb, NT)
                    u0, u1 = _head_sum(xd * bdh * dec, first)
                    dcs0, dcs1 = dcs0 - u0, dcs1 - u1
                    w = jnp.sum(dhn * h_prev, axis=0, keepdims=True)
                    w0, w1 = _head_sum(w, first[:1])
                    dt0 = jnp.sum(u0, axis=0, keepdims=True) + et0 * w0
                    dt1 = jnp.sum(u1, axis=0, keepdims=True) + et1 * w1
                    dst_ref[pi] = _dot(Cg, dyeb, TN) + dhn * etot
                    q0, q1 = _head_sum(dxd * xv, first)
                    dxbc_ref[0, :, pi * LANES:(pi + 1) * LANES] = dxd * dtp
                    dcs_all = dcs_all + jnp.where(lane == j0, dcs0, 0.0) + jnp.where(lane == j1, dcs1, 0.0)
                    ddt_all = ddt_all + jnp.where(lane == j0, q0, 0.0) + jnp.where(lane == j1, q1, 0.0)
                    dtot_all = (dtot_all + jnp.where(lane[:1] == j0, dt0, 0.0)
                                + jnp.where(lane[:1] == j1, dt1, 0.0))
                dGb = _bf(dG)
                dC = dC + _dot(dGb, Bg, NN)
                dB = dB + _dot(dGb, Cg, TN)
                dxbc_ref[0, :, SSD_INNER + g * SSD_N:SSD_INNER + (g + 1) * SSD_N] = dB
                dxbc_ref[0, :, SSD_INNER + (SSD_G + g) * SSD_N:SSD_INNER + (SSD_G + g + 1) * SSD_N] = dC
            da = _dotf(later_t.astype(F32), dcs_all) + dtot_all
            ddt = ddt_all + da * arow_ref[...]
            da_ref[0] += jnp.sum(da * dt, axis=0, keepdims=True)
            draw_ref[0] = ddt * _sigmoid(raw_v + brow_ref[...])

        def rchunk(c):
            return c if rev else (NC - 1 - c)

        sp = specs(rchunk)
        full = pl.BlockSpec((1, CHUNK, SSD_CONV_DIM), lambda b, c: (b, rchunk(c), 0))
        return pl.pallas_call(
            body, name=name + "_bwd", grid=(B, NC),
            in_specs=[sp["x"], sp["bm"], sp["cm"], sp["raw"], sp["raw_t"], sp["row"], sp["row"],
                      sp["colv"], sp["colv"], sp["hs"], sp["x"]],
            out_specs=[full, sp["raw"], pl.BlockSpec((1, 1, LANES), lambda b, c: (b, 0, 0))],
            out_shape=[jax.ShapeDtypeStruct((B, S, SSD_CONV_DIM), F32),
                       jax.ShapeDtypeStruct((B, S, LANES), F32),
                       jax.ShapeDtypeStruct((B, 1, LANES), F32)],
            scratch_shapes=[pltpu.VMEM((N_PAIR, SSD_N, LANES), F32)],
            compiler_params=_cp(),
        )(xbc, xbc, xbc, raw, raw_t, brow, arow, bcol, acol, hs, dy)

    def aux(raw, brow, arow):
        raw_t = jnp.swapaxes(raw[:, :, :n_dt], 1, 2)
        return raw_t, brow[0, :n_dt][:, None], arow[0, :n_dt][:, None]

    @jax.custom_vjp
    def f(xbc, raw, brow, arow):
        raw_t, bcol, acol = aux(raw, brow, arow)
        return fwd_call(xbc, raw, raw_t, brow, arow, bcol, acol)[0]

    def fwd(xbc, raw, brow, arow):
        raw_t, bcol, acol = aux(raw, brow, arow)
        y, hs = fwd_call(xbc, raw, raw_t, brow, arow, bcol, acol)
        return y, (xbc, raw, brow, arow, hs)

    def bwd(res, dy):
        xbc, raw, brow, arow, hs = res
        raw_t, bcol, acol = aux(raw, brow, arow)
        dxbc, draw, da = bwd_call(xbc, raw, raw_t, brow, arow, bcol, acol, hs, dy)
        dbrow = jnp.sum(draw, axis=(0, 1))[None, :]
        return dxbc, draw, dbrow, jnp.sum(da, axis=0)

    f.defvjp(fwd, bwd)
    return f(xbc, raw, brow, arow)


def ssd_out(yf, yb, xbc, z, dsk, g, *, name):
    T, W = yf.shape
    gs = W // SSD_G
    tr = _tile(T, 512, 8)
    row = pl.BlockSpec((tr, W), lambda i: (i, 0))
    vec = pl.BlockSpec((1, W), lambda i: (0, 0))

    def normed(yv, gv):
        outs, rs = [], []
        for i in range(SSD_G):
            sl = slice(i * gs, (i + 1) * gs)
            r = lax.rsqrt(jnp.mean(yv[:, sl] * yv[:, sl], axis=-1, keepdims=True) + EPS)
            rs.append(r)
            outs.append(yv[:, sl] * r)
        return outs, rs

    def fwd_call(yf, yb, xbc, z, dsk, g):
        def body(yf_ref, yb_ref, xs_ref, z_ref, dsk_ref, g_ref, o_ref):
            zv = z_ref[...]
            yv = (yf_ref[...] + yb_ref[...] + xs_ref[...] * dsk_ref[...]) * (zv * _sigmoid(zv))
            outs, _ = normed(yv, g_ref[...])
            for i in range(SSD_G):
                sl = slice(i * gs, (i + 1) * gs)
                o_ref[:, sl] = (outs[i] * g_ref[:, sl]).astype(o_ref.dtype)

        return pl.pallas_call(
            body, name=name + "_fwd", grid=(T // tr,), in_specs=[row, row, row, row, vec, vec],
            out_specs=row, out_shape=jax.ShapeDtypeStruct((T, W), BF16), compiler_params=_cp(),
        )(yf, yb, xbc, z, dsk, g)

    def bwd_call(yf, yb, xbc, z, dsk, g, do):
        def body(yf_ref, yb_ref, xs_ref, z_ref, dsk_ref, g_ref, do_ref, dy_ref, dxs_ref, dz_ref,
                 ddsk_ref, dg_ref):
            @pl.when(pl.program_id(0) == 0)
            def _():
                ddsk_ref[...] = jnp.zeros_like(ddsk_ref)
                dg_ref[...] = jnp.zeros_like(dg_ref)

            zv = z_ref[...]
            sg = _sigmoid(zv)
            sz = zv * sg
            xs = xs_ref[...]
            pre = yf_ref[...] + yb_ref[...] + xs * dsk_ref[...]
            yv = pre * sz
            outs, rs = normed(yv, g_ref[...])
            for i in range(SSD_G):
                sl = slice(i * gs, (i + 1) * gs)
                dov = do_ref[:, sl].astype(F32)
                xn = outs[i]
                dxn = dov * g_ref[:, sl]
                dyv = rs[i] * (dxn - xn * jnp.mean(dxn * xn, axis=-1, keepdims=True))
                dg_ref[:, sl] += jnp.sum(dov * xn, axis=0, keepdims=True)
                dpre = dyv * sz[:, sl]
                dy_ref[:, sl] = dpre
                dxs_ref[:, sl] = dpre * dsk_ref[:, sl]
                ddsk_ref[:, sl] += jnp.sum(dpre * xs[:, sl], axis=0, keepdims=True)
                dz_ref[:, sl] = dyv * pre[:, sl] * (sg[:, sl] * (1.0 + zv[:, sl] * (1.0 - sg[:, sl])))

        o = jax.ShapeDtypeStruct((T, W), F32)
        v = jax.ShapeDtypeStruct((1, W), F32)
        return pl.pallas_call(
            body, name=name + "_bwd", grid=(T // tr,), in_specs=[row, row, row, row, vec, vec, row],
            out_specs=[row, row, row, vec, vec], out_shape=[o, o, o, v, v], compiler_params=_cp(),
        )(yf, yb, xbc, z, dsk, g, do)

    @jax.custom_vjp
    def f(yf, yb, xbc, z, dsk, g):
        return fwd_call(yf, yb, xbc, z, dsk, g)

    def fwd(yf, yb, xbc, z, dsk, g):
        return fwd_call(yf, yb, xbc, z, dsk, g), (yf, yb, xbc, z, dsk, g)

    def bwd(res, do):
        dy, dxs, dz, ddsk, dg = bwd_call(*res, do)
        dxbc = jnp.pad(dxs, ((0, 0), (0, res[2].shape[1] - W)))
        return dy, dy, dxbc, dz, ddsk, dg

    f.defvjp(fwd, bwd)
    return f(yf, yb, xbc, z, dsk, g)


def swiglu(gu, *, name):
    T, F2 = gu.shape
    Fh = F2 // 2
    tr, tf = _tile(T, 512, 8), _tile(Fh, 512)
    nf = Fh // tf
    gs = pl.BlockSpec((tr, tf), lambda i, j: (i, j))
    us = pl.BlockSpec((tr, tf), lambda i, j: (i, j + nf))

    def fwd_call(gu):
        def body(g_ref, u_ref, o_ref):
            gv = g_ref[...].astype(F32)
            o_ref[...] = (gv * _sigmoid(gv) * u_ref[...].astype(F32)).astype(o_ref.dtype)

        return pl.pallas_call(
            body, name=name + "_fwd", grid=(T // tr, nf), in_specs=[gs, us], out_specs=gs,
            out_shape=jax.ShapeDtypeStruct((T, Fh), BF16), compiler_params=_cp(),
        )(gu, gu)

    def bwd_call(gu, da):
        def body(g_ref, u_ref, da_ref, dgu_ref):
            j = pl.program_id(1)
            gv = g_ref[...].astype(F32)
            uv = u_ref[...].astype(F32)
            dav = da_ref[...].astype(F32)
            sg = _sigmoid(gv)

            @pl.when(j < nf)
            def _():
                dgu_ref[...] = (dav * uv * (sg * (1.0 + gv * (1.0 - sg)))).astype(dgu_ref.dtype)

            @pl.when(j >= nf)
            def _():
                dgu_ref[...] = (dav * gv * sg).astype(dgu_ref.dtype)

        gsel = pl.BlockSpec((tr, tf), lambda i, j: (i, j % nf))
        usel = pl.BlockSpec((tr, tf), lambda i, j: (i, j % nf + nf))
        return pl.pallas_call(
            body, name=name + "_bwd", grid=(T // tr, 2 * nf), in_specs=[gsel, usel, gsel],
            out_specs=pl.BlockSpec((tr, tf), lambda i, j: (i, j)),
            out_shape=jax.ShapeDtypeStruct((T, F2), BF16), compiler_params=_cp(),
        )(gu, gu, da)

    @jax.custom_vjp
    def f(gu):
        return fwd_call(gu)

    def fwd(gu):
        return fwd_call(gu), gu

    def bwd(gu, da):
        return (bwd_call(gu, da),)

    f.defvjp(fwd, bwd)
    return f(gu)


def gated_residual(x, gate, y, *, name):
    B, S, D = x.shape
    ts = _tile(S, 256, 8)
    row = pl.BlockSpec((1, ts, D), lambda b, j: (b, j, 0))
    per_b = pl.BlockSpec((1, 1, D), lambda b, j: (b, 0, 0))

    def fwd_call(x, gate, y):
        def body(x_ref, gt_ref, y_ref, o_ref):
            o_ref[0] = x_ref[0] + gt_ref[0] * y_ref[0]

        return pl.pallas_call(
            body, name=name + "_fwd", grid=(B, S // ts), in_specs=[row, per_b, row], out_specs=row,
            out_shape=jax.ShapeDtypeStruct((B, S, D), F32), compiler_params=_cp(),
        )(x, gate, y)

    def bwd_call(gate, y, g):
        def body(gt_ref, y_ref, g_ref, dy_ref, dgt_ref):
            @pl.when(pl.program_id(1) == 0)
            def _():
                dgt_ref[...] = jnp.zeros_like(dgt_ref)

            gv = g_ref[0]
            dy_ref[0] = gt_ref[0] * gv
            dgt_ref[0] += jnp.sum(gv * y_ref[0], axis=0, keepdims=True)

        return pl.pallas_call(
            body, name=name + "_bwd", grid=(B, S // ts), in_specs=[per_b, row, row],
            out_specs=[row, per_b],
            out_shape=[jax.ShapeDtypeStruct((B, S, D), F32), jax.ShapeDtypeStruct((B, 1, D), F32)],
            compiler_params=_cp(),
        )(gate, y, g)

    @jax.custom_vjp
    def f(x, gate, y):
        return fwd_call(x, gate, y)

    def fwd(x, gate, y):
        return fwd_call(x, gate, y), (gate, y)

    def bwd(res, g):
        dy, dgate = bwd_call(res[0], res[1], g)
        return g, dgate, dy

    f.defvjp(fwd, bwd)
    return f(x, gate, y)


def res_norm(x, gate, y, g, scale, shift, *, name):
    B, S, D = x.shape
    ts = _tile(S, 256, 8)
    row = pl.BlockSpec((1, ts, D), lambda b, j: (b, j, 0))
    per_b = pl.BlockSpec((1, 1, D), lambda b, j: (b, 0, 0))
    gspec = pl.BlockSpec((1, D), lambda b, j: (0, 0))

    def fwd_call(x, gate, y, g, scale, shift):
        def body(x_ref, gt_ref, y_ref, g_ref, sc_ref, sh_ref, xo_ref, h_ref):
            xv = x_ref[0] + gt_ref[0] * y_ref[0]
            xo_ref[0] = xv
            r = lax.rsqrt(jnp.mean(xv * xv, axis=-1, keepdims=True) + EPS)
            h_ref[0] = (xv * r * g_ref[...] * (1.0 + sc_ref[0]) + sh_ref[0]).astype(h_ref.dtype)

        return pl.pallas_call(
            body, name=name + "_fwd", grid=(B, S // ts),
            in_specs=[row, per_b, row, gspec, per_b, per_b], out_specs=[row, row],
            out_shape=[jax.ShapeDtypeStruct((B, S, D), F32), jax.ShapeDtypeStruct((B, S, D), BF16)],
            compiler_params=_cp(),
        )(x, gate, y, g, scale, shift)

    def bwd_call(xn, g, scale, gate, y, dh, dxn):
        def body(x_ref, g_ref, sc_ref, gt_ref, y_ref, dh_ref, dxn_ref,
                 dx_ref, dy_ref, dg_ref, dsc_ref, dsh_ref, dgt_ref):
            @pl.when(pl.program_id(1) == 0)
            def _():
                for ref in (dg_ref, dsc_ref, dsh_ref, dgt_ref):
                    ref[...] = jnp.zeros_like(ref)

            xv = x_ref[0]
            dh = dh_ref[0].astype(F32)
            gv = g_ref[...]
            r = lax.rsqrt(jnp.mean(xv * xv, axis=-1, keepdims=True) + EPS)
            xh = xv * r
            dyv = dh * (1.0 + sc_ref[0])
            dxh = dyv * gv
            dx = dxn_ref[0] + r * (dxh - xh * jnp.mean(dxh * xh, axis=-1, keepdims=True))
            dx_ref[0] = dx
            dy_ref[0] = gt_ref[0] * dx
            dgt_ref[0] += jnp.sum(dx * y_ref[0], axis=0, keepdims=True)
            dg_ref[0] += jnp.sum(dyv * xh, axis=0, keepdims=True)
            dsc_ref[0] += jnp.sum(dh * xh * gv, axis=0, keepdims=True)
            dsh_ref[0] += jnp.sum(dh, axis=0, keepdims=True)

        big = jax.ShapeDtypeStruct((B, S, D), F32)
        vec = jax.ShapeDtypeStruct((B, 1, D), F32)
        return pl.pallas_call(
            body, name=name + "_bwd", grid=(B, S // ts),
            in_specs=[row, gspec, per_b, per_b, row, row, row],
            out_specs=[row, row, per_b, per_b, per_b, per_b],
            out_shape=[big, big, vec, vec, vec, vec], compiler_params=_cp(),
        )(xn, g, scale, gate, y, dh, dxn)

    @jax.custom_vjp
    def f(x, gate, y, g, scale, shift):
        return tuple(fwd_call(x, gate, y, g, scale, shift))

    def fwd(x, gate, y, g, scale, shift):
        xn, h = fwd_call(x, gate, y, g, scale, shift)
        return (xn, h), (xn, g, scale, gate, y)

    def bwd(res, cts):
        xn, g, scale, gate, y = res
        dx, dy, dg, dsc, dsh, dgt = bwd_call(xn, g, scale, gate, y, cts[1], cts[0])
        return dx, dgt, dy, jnp.sum(dg, axis=0), dsc, dsh

    f.defvjp(fwd, bwd)
    return f(x, gate, y, g, scale, shift)


def final_loss(x, g, target, *, name):
    T, D = x.shape
    tr = _tile(T, 256, 8)
    row = pl.BlockSpec((tr, D), lambda i: (i, 0))
    vec = pl.BlockSpec((1, D), lambda i: (0, 0))

    def fwd_call(x, g, target):
        def body(x_ref, g_ref, t_ref, o_ref):
            @pl.when(pl.program_id(0) == 0)
            def _():
                o_ref[...] = jnp.zeros_like(o_ref)

            xv = x_ref[...]
            r = lax.rsqrt(jnp.mean(xv * xv, axis=-1, keepdims=True) + EPS)
            e = xv * r * g_ref[...] - t_ref[...]
            o_ref[...] += jnp.sum(e * e, axis=0, keepdims=True)

        part = pl.pallas_call(
            body, name=name + "_fwd", grid=(T // tr,), in_specs=[row, vec, row], out_specs=vec,
            out_shape=jax.ShapeDtypeStruct((1, D), F32), compiler_params=_cp(),
        )(x, g, target)
        return (0.5 / D) * jnp.sum(part)

    def bwd_call(x, g, target, ct):
        def body(x_ref, g_ref, t_ref, ct_ref, dx_ref, dg_ref):
            @pl.when(pl.program_id(0) == 0)
            def _():
                dg_ref[...] = jnp.zeros_like(dg_ref)

            xv = x_ref[...]
            gv = g_ref[...]
            r = lax.rsqrt(jnp.mean(xv * xv, axis=-1, keepdims=True) + EPS)
            xn = xv * r
            dy = (xn * gv - t_ref[...]) * (ct_ref[...] * (1.0 / D))
            dxn = dy * gv
            dx_ref[...] = r * (dxn - xn * jnp.mean(dxn * xn, axis=-1, keepdims=True))
            dg_ref[...] += jnp.sum(dy * xn, axis=0, keepdims=True)

        return pl.pallas_call(
            body, name=name + "_bwd", grid=(T // tr,),
            in_specs=[row, vec, row, pl.BlockSpec((1, 1), lambda i: (0, 0))], out_specs=[row, vec],
            out_shape=[jax.ShapeDtypeStruct((T, D), F32), jax.ShapeDtypeStruct((1, D), F32)],
            compiler_params=_cp(),
        )(x, g, target, ct)

    @jax.custom_vjp
    def f(x, g, target):
        return fwd_call(x, g, target)

    def fwd(x, g, target):
        return fwd_call(x, g, target), (x, g, target)

    def bwd(res, ct):
        x, g, target = res
        dx, dg = bwd_call(x, g, target, jnp.reshape(ct, (1, 1)).astype(F32))
        return dx, dg, jnp.zeros_like(target)

    f.defvjp(fwd, bwd)
    return f(x, g, target)


def adamw(w, g, m, v, *, name, carry=()):
    L, R, C = w.shape
    tr = _tile(R, 512, 8)
    spec = pl.BlockSpec((1, tr, C), lambda l, i: (l, i, 0))
    c1 = 1.0 / (1.0 - ADAM_B1 ** ADAM_STEP)
    c2 = 1.0 / (1.0 - ADAM_B2 ** ADAM_STEP)

    def body(w_ref, g_ref, m_ref, v_ref, d_ref, nm_ref, nv_ref):
        gv = g_ref[...]
        nm = ADAM_B1 * m_ref[...] + (1.0 - ADAM_B1) * gv
        nv = ADAM_B2 * v_ref[...] + (1.0 - ADAM_B2) * (gv * gv)
        nm_ref[...] = nm
        nv_ref[...] = nv
        d_ref[...] = -ADAM_LR * ((nm * c1) / (jnp.sqrt(nv * c2) + ADAM_EPS) + ADAM_WD * w_ref[...])

    o = jax.ShapeDtypeStruct((L, R, C), F32)
    return carry_call(
        body, name=name, grid=(L, R // tr), in_specs=[spec] * 4, out_specs=[spec] * 3,
        out_shape=[o, o, o], carry=carry,
    )(w, g, m, v)


def _position():
    x, y, c = lax.axis_index("x"), lax.axis_index("y"), lax.axis_index("c")
    return x, y, c


def ag_routine(shard, cols=False):
    R, C = shard.shape
    assert not cols or C % LANES == 0

    def parts(ins, outs, send_sems, recv_sems, local_sems):
        (x_ref,), (out_ref,) = ins, outs
        x, y, c = _position()
        me, sibling = (x, y, c), (x, y, 1 - c)
        chips = [(1 - x, y), (x, 1 - y), (1 - x, 1 - y)]

        def block(px, py, pc):
            idx = 4 * px + 2 * py + pc
            if cols:
                return out_ref.at[:, pl.ds(pl.multiple_of(idx * C, LANES), C)]
            return out_ref.at[idx]

        def copy(k, blk, to, src=None):
            return pltpu.make_async_remote_copy(
                src_ref=block(*blk) if src is None else src, dst_ref=block(*blk),
                send_sem=send_sems.at[k], recv_sem=recv_sems.at[k],
                device_id=to, device_id_type=pl.DeviceIdType.MESH)

        mine = pltpu.make_async_copy(x_ref, block(*me), local_sems.at[0])
        first = [copy(0, me, sibling, src=x_ref)]
        first += [copy(1 + j, me, (*chip, c), src=x_ref) for j, chip in enumerate(chips)]
        passed = [copy(4 + j, (*chip, c), sibling) for j, chip in enumerate(chips)]
        return me, sibling, c, chips, copy, mine, first, passed

    def start(*refs):
        me, sibling, c, chips, copy, mine, first, passed = parts(*refs)
        mine.start()
        for cp in first:
            cp.start()

    def finish(*refs):
        me, sibling, c, chips, copy, mine, first, passed = parts(*refs)
        for j, chip in enumerate(chips):
            copy(1 + j, (*chip, c), me).wait_recv()
            passed[j].start()
        copy(0, sibling, me).wait_recv()
        for j, chip in enumerate(chips):
            copy(4 + j, (*chip, 1 - c), me).wait_recv()
        for cp in first + passed:
            cp.wait_send()
        mine.wait()

    out = jax.ShapeDtypeStruct((R, N_DEV * C) if cols else (N_DEV, R, C), shard.dtype)
    return dict(ins=[shard], outs=[out], n_sem=7, n_local=1, start=start, finish=finish)


def all_gather(shard, *, name, cols=False):
    return comm_call(ag_routine(shard, cols), name=name)[0]


def carry_call(body, *, name, grid, in_specs, out_specs, out_shape, scratch_shapes=(), carry=(),
               dims=None):
    in_specs, out_specs, out_shape = list(in_specs), list(out_specs), list(out_shape)
    scratch_shapes = list(scratch_shapes)
    if not carry:
        call = pl.pallas_call(body, name=name, grid=grid, in_specs=in_specs, out_specs=out_specs,
                              out_shape=out_shape, scratch_shapes=scratch_shapes,
                              compiler_params=_cp(dimension_semantics=dims) if dims else _cp())
        return lambda *args: list(call(*args))
    n_in, n_out, n_scr = len(in_specs), len(out_specs), len(scratch_shapes)
    c_ins = [a for r in carry for a in r["ins"]]
    c_outs = [o for r in carry for o in r["outs"]]
    sems = []
    for r in carry:
        sems += [pltpu.SemaphoreType.DMA((r["n_sem"],)), pltpu.SemaphoreType.DMA((r["n_sem"],)),
                 pltpu.SemaphoreType.DMA((r["n_local"],))]

    def wrapped(*refs):
        refs = list(refs)
        ins, refs = refs[:n_in], refs[n_in:]
        cin, refs = refs[:len(c_ins)], refs[len(c_ins):]
        outs, refs = refs[:n_out], refs[n_out:]
        cout, refs = refs[:len(c_outs)], refs[len(c_outs):]
        scr, csem = refs[:n_scr], refs[n_scr:]
        ids = [pl.program_id(i) for i in range(len(grid))]
        first = functools.reduce(jnp.logical_and, [i == 0 for i in ids])
        last = functools.reduce(jnp.logical_and, [i == g - 1 for i, g in zip(ids, grid)])

        def each(which):
            io = oo = 0
            for j, r in enumerate(carry):
                r[which](cin[io:io + len(r["ins"])], cout[oo:oo + len(r["outs"])], *csem[3 * j:3 * j + 3])
                io += len(r["ins"])
                oo += len(r["outs"])

        @pl.when(first)
        def _():
            each("start")

        body(*ins, *outs, *scr)

        @pl.when(last)
        def _():
            each("finish")

    any_spec = pl.BlockSpec(memory_space=pl.ANY)
    call = pl.pallas_call(
        wrapped, name=name, grid=grid, in_specs=in_specs + [any_spec] * len(c_ins),
        out_specs=out_specs + [any_spec] * len(c_outs), out_shape=out_shape + c_outs,
        scratch_shapes=scratch_shapes + sems, compiler_params=_cp())
    return lambda *args: list(call(*args, *c_ins))


N_CHIP = 4


def rs_pair_routine(g, cols):
    if cols:
        R, C = g.shape[0], g.shape[1] // N_DEV
        assert C % LANES == 0
    else:
        _, R, C = g.shape

    def blk(ref, idx):
        if cols:
            return ref.at[:, pl.ds(pl.multiple_of(idx * C, LANES), C)]
        return ref.at[idx]

    def copies(ins, outs, send_sems, recv_sems, local_sems):
        (g_ref,), (got_ref,) = ins, outs
        x, y, c = _position()
        local, remote = [], []
        for q in range(N_CHIP):
            remote.append(pltpu.make_async_remote_copy(
                src_ref=blk(g_ref, 2 * q + 1 - c), dst_ref=got_ref.at[q],
                send_sem=send_sems.at[q], recv_sem=recv_sems.at[q],
                device_id=(x, y, 1 - c), device_id_type=pl.DeviceIdType.MESH))
        return local, remote

    def start(*refs):
        local, remote = copies(*refs)
        for cp in remote + local:
            cp.start()

    def finish(*refs):
        local, remote = copies(*refs)
        for cp in remote:
            cp.wait_recv()
        for cp in remote:
            cp.wait_send()
        for cp in local:
            cp.wait()

    o = jax.ShapeDtypeStruct((N_CHIP, R, C), g.dtype)
    return dict(ins=[g], outs=[o], n_sem=N_CHIP, n_local=1, start=start, finish=finish)


def rs_chip_routine(h):
    _, R, C = h.shape
    RELATIONS = ((0, 1), (1, 0), (1, 1))

    def copies(ins, outs, send_sems, recv_sems, local_sems):
        (h_ref,), (out_ref,) = ins, outs
        x, y, c = _position()
        local, remote = [], []
        for k, (fx, fy) in enumerate(RELATIONS):
            px = (1 - x) if fx else x
            py = (1 - y) if fy else y
            remote.append(pltpu.make_async_remote_copy(
                src_ref=h_ref.at[2 * px + py], dst_ref=out_ref.at[k],
                send_sem=send_sems.at[k], recv_sem=recv_sems.at[k],
                device_id=(px, py, c), device_id_type=pl.DeviceIdType.MESH))
        return local, remote

    def start(*refs):
        local, remote = copies(*refs)
        for cp in remote + local:
            cp.start()

    def finish(*refs):
        local, remote = copies(*refs)
        for cp in remote:
            cp.wait_recv()
        for cp in remote:
            cp.wait_send()
        for cp in local:
            cp.wait()

    return dict(ins=[h], outs=[jax.ShapeDtypeStruct((N_CHIP - 1, R, C), h.dtype)], n_sem=3, n_local=1,
                start=start, finish=finish)


def comm_call(routine, *, name):
    n_in, n_out = len(routine["ins"]), len(routine["outs"])

    def body(*refs):
        ins, outs, sems = refs[:n_in], refs[n_in:n_in + n_out], refs[n_in + n_out:]
        routine["start"](ins, outs, *sems)
        routine["finish"](ins, outs, *sems)

    any_spec = pl.BlockSpec(memory_space=pl.ANY)
    return pl.pallas_call(
        body, name=name, out_shape=routine["outs"],
        in_specs=[any_spec] * n_in, out_specs=[any_spec] * n_out,
        scratch_shapes=[pltpu.SemaphoreType.DMA((routine["n_sem"],)), pltpu.SemaphoreType.DMA((routine["n_sem"],)),
                        pltpu.SemaphoreType.DMA((routine["n_local"],))],
    )(*routine["ins"])


def add_own(g, got, core, *, cols, name):
    n, R, C = got.shape
    tr = _tile(R, 256, 8)
    if cols:
        gspec = pl.BlockSpec((tr, C), lambda q, i, c_ref: (i, 2 * q + c_ref[0]))
    else:
        gspec = pl.BlockSpec((1, tr, C), lambda q, i, c_ref: (2 * q + c_ref[0], i, 0))
    spec = pl.BlockSpec((1, tr, C), lambda q, i, c_ref: (q, i, 0))

    def body(c_ref, g_ref, b_ref, o_ref):
        gv = g_ref[...] if cols else g_ref[0]
        o_ref[0] = (gv.astype(F32) + b_ref[0].astype(F32)).astype(o_ref.dtype)

    return pl.pallas_call(
        body, name=name,
        grid_spec=pltpu.PrefetchScalarGridSpec(num_scalar_prefetch=1, grid=(n, R // tr),
                                               in_specs=[gspec, spec], out_specs=spec),
        out_shape=jax.ShapeDtypeStruct((n, R, C), g.dtype), compiler_params=_cp(),
    )(core, g, got)


def sum_chips(h, got, chip, *, name):
    _, R, C = h.shape
    tr = _tile(R, 256, 8)

    def body(q_ref, h_ref, g_ref, o_ref):
        acc = h_ref[0].astype(F32)
        for k in range(N_CHIP - 1):
            acc = acc + g_ref[k].astype(F32)
        o_ref[...] = acc

    return pl.pallas_call(
        body, name=name,
        grid_spec=pltpu.PrefetchScalarGridSpec(
            num_scalar_prefetch=1, grid=(R // tr,),
            in_specs=[pl.BlockSpec((1, tr, C), lambda i, q_ref: (q_ref[0], i, 0)),
                      pl.BlockSpec((N_CHIP - 1, tr, C), lambda i, q_ref: (0, i, 0))],
            out_specs=pl.BlockSpec((tr, C), lambda i, q_ref: (i, 0))),
        out_shape=jax.ShapeDtypeStruct((R, C), F32), compiler_params=_cp(),
    )(chip, h, got)


def reduce_scatter(g, *, cols, name):
    x, y, c = _position()
    core = jnp.reshape(c, (1,)).astype(jnp.int32)
    chip = jnp.reshape(2 * x + y, (1,)).astype(jnp.int32)
    got = comm_call(rs_pair_routine(g, cols), name=name + "_pair")[0]
    h = add_own(g, got, core, cols=cols, name=name + "_add")
    return sum_chips(h, comm_call(rs_chip_routine(h), name=name + "_chip")[0], chip, name=name + "_sum")


def sum_blocks(stack, *, name):
    n, R, C = stack.shape
    tr = _tile(R, 256, 8)

    def body(x_ref, o_ref):
        acc = x_ref[0].astype(F32)
        for i in range(1, n):
            acc = acc + x_ref[i].astype(F32)
        o_ref[...] = acc

    return pl.pallas_call(
        body, name=name, grid=(R // tr,),
        in_specs=[pl.BlockSpec((n, tr, C), lambda i: (0, i, 0))],
        out_specs=pl.BlockSpec((tr, C), lambda i: (i, 0)),
        out_shape=jax.ShapeDtypeStruct((R, C), F32), compiler_params=_cp(),
    )(stack)


PACK_COLS = 1024
PACK_ROW_MULT = 8


def _pack(arrays, dtype):
    flat = jnp.concatenate([a.reshape(-1).astype(dtype) for a in arrays])
    n = flat.shape[0]
    unit = PACK_COLS * PACK_ROW_MULT
    padded = -(-n // unit) * unit
    return jnp.pad(flat, (0, padded - n)).reshape(padded // PACK_COLS, PACK_COLS)


def _unpack(packed, shapes):
    flat = packed.reshape(-1)
    out, o = [], 0
    for s in shapes:
        n = int(np.prod(s))
        out.append(flat[o:o + n].reshape(s))
        o += n
    return out


def fsdp_cols(shard, *, name):
    K, n = shard.shape
    npad = -(-n // LANES) * LANES

    @jax.custom_vjp
    def f(p):
        p = jnp.pad(p, ((0, 0), (0, npad - n))) if npad != n else p
        return all_gather(p.astype(BF16), cols=True, name=name + "_ag")

    def fwd(p):
        return f(p), None

    def bwd(_, g):
        d = reduce_scatter(g, cols=True, name=name + "_rs")
        return (d[:, :n] if npad != n else d,)

    f.defvjp(fwd, bwd)
    return f(shard)


def fsdp_rows(shard, *, name):
    k, N = shard.shape

    @jax.custom_vjp
    def f(p):
        return all_gather(p.astype(BF16), name=name + "_ag").reshape(N_DEV * k, N)

    def fwd(p):
        return f(p), None

    def bwd(_, g):
        return (reduce_scatter(g.reshape(N_DEV, k, N), cols=False, name=name + "_rs"),)

    f.defvjp(fwd, bwd)
    return f(shard)


def _unpad_cols(w, n):
    K = w.shape[0]
    npad = w.shape[1] // N_DEV
    if npad == n:
        return w
    return w.reshape(K, N_DEV, npad)[:, :, :n].reshape(K, N_DEV * n)


def gather_rows(part, me, *, name):
    rows, n = part.shape
    per = rows // N_DEV

    @jax.custom_vjp
    def f(part):
        full = all_gather(part, name=name + "_fwd")
        mine = lax.dynamic_slice_in_dim(full, me * per, per, axis=1)
        return jnp.swapaxes(mine, 0, 1).reshape(per, N_DEV * n)

    def fwd(part):
        return f(part), None

    def bwd(_, g):
        full = all_gather(g, name=name + "_bwd")
        mine = lax.dynamic_slice_in_dim(full, me * n, n, axis=2)
        return (mine.reshape(rows, n),)

    f.defvjp(fwd, bwd)
    return f(part)


def _seg_layout():
    offs = np.concatenate([[0], np.cumsum(IN_SPLITS)])
    cols, widths, leads = [], [], []
    for s in SEG_ORDER:
        cols.append((int(offs[s]), int(offs[s + 1])))
        widths.append(SEG_PAD.get(s, IN_SPLITS[s]))
        leads.append(SEG_LEAD.get(s, 0))
    return cols, widths, leads


def _arrange_w_in(w, n):
    D = w.shape[0]
    npad = w.shape[1] // N_DEV
    cols, widths, leads = _seg_layout()

    def pieces_of(a, b):
        out = []
        for d in range(N_DEV):
            lo, hi = max(a, n * d), min(b, n * (d + 1))
            if lo < hi:
                out.append((d, lo, hi))
        return out

    @jax.custom_vjp
    def f(w):
        parts = []
        for (a, b), wd, ld in zip(cols, widths, leads):
            if ld:
                parts.append(jnp.zeros((D, ld), w.dtype))
            parts += [w[:, npad * d + lo - n * d:npad * d + hi - n * d] for d, lo, hi in pieces_of(a, b)]
            if wd != ld + b - a:
                parts.append(jnp.zeros((D, wd - ld - (b - a)), w.dtype))
        parts.append(jnp.zeros((D, IN_WIDTH - sum(widths)), w.dtype))
        return jnp.concatenate(parts, axis=1)

    def fwd(w):
        return f(w), None

    def bwd(_, g):
        offs = np.concatenate([[0], np.cumsum(widths)])
        runs = []
        for ((a, b), off, ld) in zip(cols, offs[:-1], leads):
            runs += [(lo, int(off) + ld + lo - a, hi - lo) for _, lo, hi in pieces_of(a, b)]
        runs.sort()
        parts, d_next = [], 1
        for lo, o, ln in runs:
            while lo >= n * d_next:
                parts.append(jnp.zeros((D, npad - n), g.dtype))
                d_next += 1
            parts.append(g[:, o:o + ln])
        parts.append(jnp.zeros((D, npad - n), g.dtype))
        return (jnp.concatenate(parts, axis=1),)

    f.defvjp(fwd, bwd)
    return f(w)


def split_cols(proj, widths):
    @jax.custom_vjp
    def f(p):
        outs, o = [], 0
        for wd in widths:
            outs.append(p[:, o:o + wd])
            o += wd
        return tuple(outs)

    def fwd(p):
        return f(p), None

    def bwd(_, gs):
        rest = proj.shape[1] - sum(widths)
        tail = [jnp.zeros((proj.shape[0], rest), proj.dtype)] if rest else []
        return (jnp.concatenate(list(gs) + tail, axis=1),)

    f.defvjp(fwd, bwd)
    return f(proj)


BIG = ("w_in", "w_uq", "w_ukv", "conv_w", "w_out", "w_gate_up", "w_down")
SMALL = ("b_ada", "norm1_g", "norm2_g", "q_norm_g", "k_norm_g", "mla_q_norm_g", "mla_kv_norm_g",
         "conv_b", "dt_bias", "a_log", "d_skip", "ssd_norm_g", "final_norm_g")
WEIGHTS = ("w_ada", "b_ada", "norm1_g", "norm2_g", "w_in", "q_norm_g", "k_norm_g", "mla_q_norm_g",
           "w_uq", "mla_kv_norm_g", "w_ukv", "conv_w", "conv_b", "dt_bias", "a_log", "d_skip",
           "ssd_norm_g", "w_out", "w_gate_up", "w_down", "final_norm_g")


PRE = ("w_in", "w_uq", "w_ukv", "conv_w")


def _layer(l, x, h, mods, W, shards, nxt, P, tabs):
    B, S, D = x.shape
    T = B * S
    nm = f"l{l}_"
    shift1, scale1, gate1, shift2, scale2, gate2 = mods[l]
    cos_a, sin_a, cos_b, sin_b, cos_k, sin_k = tabs

    w_in = _arrange_w_in(W["w_in"], IN_COLS // N_DEV)
    proj, (w_out_full,) = linear(h.reshape(T, D), w_in, out_dtype=F32, name=nm + "in",
                                 carry=((shards["w_out"], False, "w_out"),))
    q_a, k_a, v_a, cq, ckv, z, xbc, kpe, dtr = split_cols(proj, _seg_layout()[1])

    qn = group_rmsnorm(q_a, jnp.tile(P["q_norm_g"][l], GQA_H)[None], gs=HEAD, out_dtype=F32, name=nm + "qnorm")
    kn = group_rmsnorm(k_a, jnp.tile(P["k_norm_g"][l], GQA_KV)[None], gs=HEAD, out_dtype=F32, name=nm + "knorm")
    qr = rope(qn.reshape(B, S, -1), cos_a[:, :GQA_H * HEAD], sin_a[:, :GQA_H * HEAD], d=HEAD // 4, name=nm + "qrope")
    kr = rope(kn.reshape(B, S, -1), cos_a[:, :GQA_KV * HEAD], sin_a[:, :GQA_KV * HEAD], d=HEAD // 4, name=nm + "krope")
    gu_rows = shards["w_gate_up"].shape[0] // 2
    o_a, (w_gu_top,) = attention(qr, kr, v_a.reshape(B, S, -1), H=GQA_H, Hkv=GQA_KV, dk=HEAD, dv=HEAD,
                                 scale=HEAD ** -0.5, name=nm + "gqa",
                                 carry=((shards["w_gate_up"][:gu_rows], True, "w_gate_up_top"),))

    slot_pad = MLA_DK - MLA_NOPE - MLA_ROPE
    w_uq = jnp.pad(W["w_uq"].reshape(MLA_QL, MLA_H, MLA_NOPE + MLA_ROPE), ((0, 0), (0, 0), (0, slot_pad)))
    w_uq = w_uq.reshape(MLA_QL, MLA_H * MLA_DK)
    w_ukv = W["w_ukv"].reshape(MLA_KVL, MLA_H, MLA_NOPE + MLA_V)
    w_ukv = jnp.concatenate(
        [jnp.pad(w_ukv[:, :, :MLA_NOPE], ((0, 0), (0, 0), (0, MLA_DK - MLA_NOPE))).reshape(MLA_KVL, -1),
         w_ukv[:, :, MLA_NOPE:].reshape(MLA_KVL, -1)], axis=1)
    cqn = group_rmsnorm(cq, P["mla_q_norm_g"][l][None], gs=MLA_QL, out_dtype=BF16, name=nm + "cqnorm")
    ckvn = group_rmsnorm(ckv, P["mla_kv_norm_g"][l][None], gs=MLA_KVL, out_dtype=BF16, name=nm + "ckvnorm")
    qb = linear(cqn, w_uq, out_dtype=F32, name=nm + "uq")
    kvb = linear(ckvn, w_ukv, out_dtype=F32, name=nm + "ukv")
    k_slots, v_b = split_cols(kvb, (MLA_H * MLA_DK, MLA_H * MLA_V))
    q_cat = rope(qb.reshape(B, S, -1), cos_b, sin_b, d=MLA_ROPE // 4, name=nm + "qpe_rope")
    k_pe = rope(kpe.reshape(B, S, -1), cos_k, sin_k, d=MLA_ROPE // 4, name=nm + "kpe_rope")
    o_b, (w_gu_bottom,) = attention(q_cat, k_slots.reshape(B, S, -1), v_b.reshape(B, S, -1), H=MLA_H, Hkv=MLA_H,
                                    dk=MLA_DK, dv=MLA_V, scale=(MLA_NOPE + MLA_ROPE) ** -0.5, name=nm + "mla",
                                    k_shared=k_pe,
                                    carry=((shards["w_gate_up"][gu_rows:], True, "w_gate_up_bottom"),))
    w_gu_full = jnp.concatenate([w_gu_top, w_gu_bottom], axis=0)

    xact = conv_silu(xbc.reshape(B, S, -1), W["conv_w"].astype(F32), P["conv_b"][l][None], name=nm + "conv")
    brow = jnp.pad(P["dt_bias"][l].reshape(1, -1), ((0, 0), (0, LANES - 2 * SSD_H)))
    arow = jnp.pad(-jnp.exp(P["a_log"][l].reshape(1, -1)), ((0, 0), (0, LANES - 2 * SSD_H)))
    raw = dtr.reshape(B, S, LANES)
    y_f = ssd_scan(xact, raw, brow, arow, rev=False, name=nm + "ssd_f")
    y_b = ssd_scan(xact, raw, brow, arow, rev=True, name=nm + "ssd_b")
    dsk = jnp.repeat(P["d_skip"][l], SSD_P)[None]
    o_c = ssd_out(y_f.reshape(T, -1), y_b.reshape(T, -1), xact.reshape(T, -1), z, dsk,
                  P["ssd_norm_g"][l][None], name=nm + "ssd_out")

    o = jnp.concatenate([o_a.reshape(T, -1).astype(BF16), o_b.reshape(T, -1).astype(BF16), o_c], axis=-1)
    mix = linear(o, w_out_full, out_dtype=F32, name=nm + "out")
    x, h = res_norm(x, gate1, mix.reshape(B, S, D), P["norm2_g"][l][None], scale2, shift2, name=nm + "res1_norm2")

    y, nxt_full = ffn(h.reshape(T, D), w_gu_full, shards["w_down"], name=nm + "ffn",
                      carry=tuple((nxt[n], True, "next_" + n) for n in PRE) if nxt else ())
    if l + 1 < len(mods):
        x, h = res_norm(x, gate2, y.reshape(B, S, D), P["norm1_g"][l + 1][None], mods[l + 1][1], mods[l + 1][0],
                        name=nm + "res2_norm1")
    else:
        x, h = gated_residual(x, gate2, y.reshape(B, S, D), name=nm + "res2"), None
    return x, h, nxt_full


def kernel(x, c, w_ada, b_ada, norm1_g, norm2_g, w_in, q_norm_g, k_norm_g, mla_q_norm_g, w_uq, mla_kv_norm_g, w_ukv, conv_w, conv_b, dt_bias, a_log, d_skip, ssd_norm_g, w_out, w_gate_up, w_down, final_norm_g, loss_target, m_w_ada, m_b_ada, m_norm1_g, m_norm2_g, m_w_in, m_q_norm_g, m_k_norm_g, m_mla_q_norm_g, m_w_uq, m_mla_kv_norm_g, m_w_ukv, m_conv_w, m_conv_b, m_dt_bias, m_a_log, m_d_skip, m_ssd_norm_g, m_w_out, m_w_gate_up, m_w_down, m_final_norm_g, v_w_ada, v_b_ada, v_norm1_g, v_norm2_g, v_w_in, v_q_norm_g, v_k_norm_g, v_mla_q_norm_g, v_w_uq, v_mla_kv_norm_g, v_w_ukv, v_conv_w, v_conv_b, v_dt_bias, v_a_log, v_d_skip, v_ssd_norm_g, v_w_out, v_w_gate_up, v_w_down, v_final_norm_g):
    args = dict(locals())
    weights = {n: args[n] for n in WEIGHTS}
    moments_m = {n: args["m_" + n] for n in WEIGHTS}
    moments_v = {n: args["v_" + n] for n in WEIGHTS}
    B, S, D = x.shape
    L = w_ada.shape[0]
    T = B * S
    px, py, pc = _position()
    me = 4 * px + 2 * py + pc
    small_shapes = [weights[n].shape for n in SMALL]

    tabs = (*rope_tables(S, HEAD, GQA_H * HEAD),
            *rope_tables(S, MLA_ROPE, MLA_H * MLA_DK, slot=MLA_DK, lead=MLA_NOPE),
            *rope_tables(S, MLA_ROPE, MLA_DK, slot=MLA_DK, lead=MLA_NOPE))
    c_all = all_gather(c, name="gather_c").reshape(N_DEV * B, D)

    pre0 = [all_gather(_shard_prep(weights[n][0], True), cols=True, name=f"l0_{n}_ag") for n in PRE]

    def local_loss(big, w_ada_s, small, x, pre):
        P = dict(zip(SMALL, small))
        mods = []
        for l in range(L):
            part = linear(c_all, w_ada_s[l], out_dtype=F32, a_silu=True, name=f"l{l}_ada")
            mod = gather_rows(part, me, name=f"l{l}_mod") + P["b_ada"][l][None]
            mods.append([m[:, None, :] for m in jnp.split(mod, 6, axis=-1)])
        h = rmsmod(x, P["norm1_g"][0][None], mods[0][1], mods[0][0], name="l0_norm1")
        for l in range(L):
            W = {n: full if n == "w_in" else _unpad_cols(full, big[n].shape[2]) for n, full in zip(PRE, pre)}
            shards = {n: big[n][l] for n in ("w_out", "w_gate_up", "w_down")}
            nxt = {n: big[n][l + 1] for n in PRE} if l + 1 < L else None
            x, h, pre = _layer(l, x, h, mods, W, shards, nxt, P, tabs)
        return final_loss(x.reshape(T, D), P["final_norm_g"][None], loss_target.reshape(T, D), name="loss")

    big = {n: weights[n] for n in BIG}
    small = tuple(weights[n] for n in SMALL)
    loss, (g_big, g_ada, g_small, grad_x, g_pre0) = jax.value_and_grad(local_loss, argnums=(0, 1, 2, 3, 4))(
        big, w_ada, small, x, pre0)
    loss = lax.psum(loss, ("x", "y", "c"))

    grads = dict(g_big)
    grads["w_ada"] = g_ada
    g_small_sum = sum_blocks(all_gather(_pack(g_small, F32), name="small_grads_ag"), name="small_grads_sum")
    grads.update(zip(SMALL, _unpack(g_small_sum, small_shapes)))

    delta, new_m, new_v = {}, {}, {}
    pre0_h = [_rs_begin(g, True, weights[n].shape[1:], f"l0_{n}") for n, g in zip(PRE, g_pre0)]
    delta["w_ada"], new_m["w_ada"], new_v["w_ada"], *pre0_got = adamw(
        w_ada, g_ada, m_w_ada, v_w_ada, name="adamw_w_ada", carry=[rs_chip_routine(h) for h in pre0_h])
    for n, h, got in zip(PRE, pre0_h, pre0_got):
        g0 = _rs_end(h, got, True, weights[n].shape[1:], f"l0_{n}")
        grads[n] = lax.dynamic_update_slice(grads[n], g0[None], (0, 0, 0))
    for n in BIG:
        delta[n], new_m[n], new_v[n] = adamw(weights[n], grads[n], moments_m[n], moments_v[n], name="adamw_" + n)
    d_, m_, v_ = adamw(_pack([weights[n] for n in SMALL], F32)[None], g_small_sum[None],
                       _pack([moments_m[n] for n in SMALL], F32)[None], _pack([moments_v[n] for n in SMALL], F32)[None],
                       name="adamw_small")
    for tgt, packed in ((delta, d_), (new_m, m_), (new_v, v_)):
        tgt.update(zip(SMALL, _unpack(packed[0], small_shapes)))

    return (loss, grad_x, *[grads[n] for n in WEIGHTS], *[delta[n] for n in WEIGHTS],
            *[new_m[n] for n in WEIGHTS], *[new_v[n] for n in WEIGHTS])
```

```python
import functools
import math

import jax
import jax.numpy as jnp
import numpy as np
from jax import lax
from jax.experimental import pallas as pl
from jax.experimental.pallas import tpu as pltpu

F32 = jnp.float32
BF16 = jnp.bfloat16
N_DEV = 8
EPS = 1e-6
ROPE_THETA = 10000.0
GRID_W = 64

GQA_H, GQA_KV, HEAD = 6, 2, 128
MLA_H, MLA_QL, MLA_KVL, MLA_NOPE, MLA_ROPE, MLA_V = 4, 512, 256, 128, 64, 128
MLA_DK = 256
SSD_H, SSD_P, SSD_G, SSD_N, SSD_K, CHUNK = 12, 64, 2, 128, 5, 128
SSD_INNER = SSD_H * SSD_P
SSD_CONV_DIM = SSD_INNER + 2 * SSD_G * SSD_N
N_PAIR = SSD_H // 2
LANES = 128
IN_SPLITS = (768, 256, 256, 512, 256, 64, 768, 1280, 24)
IN_COLS = sum(IN_SPLITS)
SEG_ORDER = (0, 1, 2, 3, 4, 6, 7, 5, 8)
SEG_PAD = {5: 256, 8: 128}
SEG_LEAD = {5: 128}
IN_WIDTH = 4608

ADAM_LR, ADAM_B1, ADAM_B2, ADAM_EPS, ADAM_WD, ADAM_STEP = 0.001, 0.9, 0.999, 1e-08, 0.01, 10
VMEM_LIMIT = 56 * 1024 * 1024
MM_TM, MM_TN, MM_TK = 1024, 1408, 2048
MM_TK_HALVES = 2816


def _cp(**kw):
    return pltpu.CompilerParams(vmem_limit_bytes=VMEM_LIMIT, **kw)


def _tile(dim, cap, mult=128):
    if dim <= cap:
        return dim
    best = None
    t = mult
    while t <= cap:
        if dim % t == 0:
            best = t
        t += mult
    assert best is not None, (dim, cap)
    return best


def _sigmoid(x):
    return 1.0 / (1.0 + jnp.exp(-x))


def _dot(a, b, dims):
    return lax.dot_general(a, b, (dims, ((), ())), preferred_element_type=F32)


NN = ((1,), (0,))
NT = ((1,), (1,))
TN = ((0,), (0,))


def _dotf(a, b, dims=NN):
    return lax.dot_general(a, b, (dims, ((), ())), preferred_element_type=F32,
                           precision=lax.Precision.HIGHEST)


def _bf(x):
    return x.astype(BF16)


def mm(a, b, *, ta=False, tb=False, out_dtype=F32, a_silu=False, name, carry=(),
       a_halves=False, b_halves=False):
    if a_halves:
        assert not ta
        M, K = a.shape[1], 2 * a.shape[2]
    elif ta:
        K, M = a.shape
    else:
        M, K = a.shape
    if b_halves:
        assert not tb
        K2, N = b.shape[1], 2 * b.shape[2]
    elif tb:
        N, K2 = b.shape
    else:
        K2, N = b.shape
    assert K == K2, (a.shape, b.shape, ta, tb)
    tm = _tile(M, MM_TM)
    tn = _tile(N // 2, MM_TN) if b_halves else _tile(N, MM_TN)
    tk = _tile(K // 2, MM_TK_HALVES) if a_halves else _tile(K, MM_TK)
    nk = K // tk
    dims = ((0 if ta else 1,), (1 if tb else 0,))

    def partial_product(a_ref, b_ref):
        av = a_ref[...]
        if a_silu:
            av = av.astype(F32)
            av = av * _sigmoid(av)
        return _dot(_bf(av), _bf(b_ref[...]), dims)

    def body_single(a_ref, b_ref, o_ref):
        o_ref[...] = partial_product(a_ref, b_ref).astype(o_ref.dtype)

    def body_acc(a_ref, b_ref, o_ref, acc_ref):
        k = pl.program_id(2)

        @pl.when(k == 0)
        def _():
            acc_ref[...] = partial_product(a_ref, b_ref)

        @pl.when(k > 0)
        def _():
            acc_ref[...] += partial_product(a_ref, b_ref)

        @pl.when(k == nk - 1)
        def _():
            o_ref[...] = acc_ref[...].astype(o_ref.dtype)

    body = body_single if nk == 1 else body_acc

    a_spec = (pl.BlockSpec((tk, tm), lambda i, j, k: (k, i)) if ta
              else pl.BlockSpec((tm, tk), lambda i, j, k: (i, k)))
    b_spec = (pl.BlockSpec((tn, tk), lambda i, j, k: (j, k)) if tb
              else pl.BlockSpec((tk, tn), lambda i, j, k: (k, j)))
    if a_halves:
        a_spec = pl.BlockSpec((None, tm, tk), lambda i, j, k: (k // (nk // 2), i, k % (nk // 2)))
    if b_halves:
        nnh = N // tn // 2
        b_spec = pl.BlockSpec((None, tk, tn), lambda i, j, k: (j // nnh, k, j % nnh))
    res = carry_call(
        body, name=name, grid=(M // tm, N // tn, nk),
        in_specs=[a_spec, b_spec],
        out_specs=[pl.BlockSpec((tm, tn), lambda i, j, k: (i, j))],
        out_shape=[jax.ShapeDtypeStruct((M, N), out_dtype)],
        scratch_shapes=[] if nk == 1 else [pltpu.VMEM((tm, tn), F32)],
        carry=carry, dims=("parallel", "parallel", "arbitrary"),
    )(a, b)
    return res if carry else res[0]


def _shard_prep(p, cols):
    if cols and p.shape[1] % LANES:
        p = jnp.pad(p, ((0, 0), (0, -p.shape[1] % LANES)))
    return p.astype(BF16)


def _full_post(full, cols):
    return full if cols else full.reshape(full.shape[0] * full.shape[1], full.shape[2])


def _rs_begin(g_full, cols, shard_shape, name):
    gg = g_full if cols else g_full.reshape(N_DEV, shard_shape[0], shard_shape[1])
    x, y, c = _position()
    got = comm_call(rs_pair_routine(gg, cols), name=name + "_pair")[0]
    return add_own(gg, got, jnp.reshape(c, (1,)).astype(jnp.int32), cols=cols, name=name + "_add")


def _rs_end(h, got, cols, shard_shape, name):
    x, y, c = _position()
    d = sum_chips(h, got, jnp.reshape(2 * x + y, (1,)).astype(jnp.int32), name=name + "_sum")
    return d[:, :shard_shape[1]] if cols else d


def linear(a, w, *, out_dtype, name, a_silu=False, carry=()):
    kinds = [(cols, s.shape, tag) for s, cols, tag in carry]

    def run(a, w, *shards):
        routines = [ag_routine(_shard_prep(s, cols), cols) for s, (cols, _, _) in zip(shards, kinds)]
        res = mm(a, w, out_dtype=out_dtype, a_silu=a_silu, name=name + "_fwd", carry=routines)
        if not routines:
            return (res,)
        return (res[0], *[_full_post(fu, cols) for fu, (cols, _, _) in zip(res[1:], kinds)])

    @jax.custom_vjp
    def f(a, w, *shards):
        return run(a, w, *shards)

    def fwd(a, w, *shards):
        return run(a, w, *shards), (a, w)

    def bwd(res, cts):
        a, w = res
        g = cts[0]
        hs = [_rs_begin(gf, cols, shp, name + "_" + tag) for gf, (cols, shp, tag) in zip(cts[1:], kinds)]
        routines = [rs_chip_routine(h) for h in hs]
        if a_silu:
            assert not routines
            da = jnp.zeros_like(a)
            gots = []
        else:
            r = mm(g, w, tb=True, out_dtype=a.dtype, name=name + "_da", carry=routines)
            da, gots = (r[0], r[1:]) if routines else (r, [])
        dw = mm(a, g, ta=True, out_dtype=w.dtype, a_silu=a_silu, name=name + "_dw")
        dsh = [_rs_end(h, got, cols, shp, name + "_" + tag)
               for h, got, (cols, shp, tag) in zip(hs, gots, kinds)]
        return (da, dw, *dsh)

    f.defvjp(fwd, bwd)
    out = f(a, w, *[s for s, _, _ in carry])
    return (out[0], list(out[1:])) if carry else out[0]


def swiglu_up(h, w_gu, *, name, carry=()):
    T, D = h.shape
    F = w_gu.shape[1] // 2
    assert D <= MM_TK
    tm, tn = _tile(T, MM_TM), _tile(F, 512)
    nf = F // tn

    def body(a_ref, bg_ref, bu_ref, gu_ref, act_ref):
        av = a_ref[...]
        g = _dot(av, bg_ref[...], NN)
        u = _dot(av, bu_ref[...], NN)
        gu_ref[0] = g.astype(gu_ref.dtype)
        gu_ref[1] = u.astype(gu_ref.dtype)
        act_ref[...] = (g * _sigmoid(g) * u).astype(act_ref.dtype)

    return carry_call(
        body, name=name, grid=(T // tm, nf),
        in_specs=[pl.BlockSpec((tm, D), lambda i, j: (i, 0)), pl.BlockSpec((D, tn), lambda i, j: (0, j)),
                  pl.BlockSpec((D, tn), lambda i, j: (0, j + nf))],
        out_specs=[pl.BlockSpec((2, tm, tn), lambda i, j: (0, i, j)), pl.BlockSpec((tm, tn), lambda i, j: (i, j))],
        out_shape=[jax.ShapeDtypeStruct((2, T, F), BF16), jax.ShapeDtypeStruct((T, F), BF16)],
        carry=carry,
    )(h, w_gu, w_gu)


def swiglu_down_bwd(g, w_dn, gu, *, name, carry=()):
    T, D = g.shape
    F = w_dn.shape[0]
    assert D <= MM_TK
    tm, tn = _tile(T, MM_TM), _tile(F, 512)

    def body(a_ref, b_ref, gu_ref, o_ref):
        dact = _dot(_bf(a_ref[...]), b_ref[...], NT)
        gv = gu_ref[0].astype(F32)
        uv = gu_ref[1].astype(F32)
        sg = _sigmoid(gv)
        o_ref[0] = (dact * uv * (sg * (1.0 + gv * (1.0 - sg)))).astype(o_ref.dtype)
        o_ref[1] = (dact * gv * sg).astype(o_ref.dtype)

    half = pl.BlockSpec((2, tm, tn), lambda i, j: (0, i, j))
    return carry_call(
        body, name=name, grid=(T // tm, F // tn),
        in_specs=[pl.BlockSpec((tm, D), lambda i, j: (i, 0)), pl.BlockSpec((tn, D), lambda i, j: (j, 0)), half],
        out_specs=[half], out_shape=[jax.ShapeDtypeStruct((2, T, F), BF16)], carry=carry,
    )(g, w_dn, gu)


def ffn(h, w_gu, w_dn_shard, *, name, carry=()):
    kinds = [(cols, s.shape, tag) for s, cols, tag in carry]
    dn_shape = w_dn_shard.shape

    def run(h, w_gu, w_dn_shard, *shards):
        gu, act, w_dn = swiglu_up(h, w_gu, name=name + "_up", carry=[ag_routine(_shard_prep(w_dn_shard, False))])
        w_dn = _full_post(w_dn, False)
        routines = [ag_routine(_shard_prep(s, cols), cols) for s, (cols, _, _) in zip(shards, kinds)]
        res = mm(act, w_dn, out_dtype=F32, name=name + "_down", carry=routines)
        y, fulls = (res[0], res[1:]) if routines else (res, [])
        return (y, *[_full_post(fu, cols) for fu, (cols, _, _) in zip(fulls, kinds)]), (h, w_gu, w_dn, gu, act)

    @jax.custom_vjp
    def f(h, w_gu, w_dn_shard, *shards):
        return run(h, w_gu, w_dn_shard, *shards)[0]

    def fwd(h, w_gu, w_dn_shard, *shards):
        return run(h, w_gu, w_dn_shard, *shards)

    def bwd(res, cts):
        h, w_gu, w_dn, gu, act = res
        g = cts[0]
        hs = [_rs_begin(gf, cols, shp, name + "_" + tag) for gf, (cols, shp, tag) in zip(cts[1:], kinds)]
        dgu, *gots = swiglu_down_bwd(g, w_dn, gu, name=name + "_down_da", carry=[rs_chip_routine(x) for x in hs])
        dw_dn = mm(act, g, ta=True, out_dtype=w_dn.dtype, name=name + "_down_dw")
        h_dn = _rs_begin(dw_dn, False, dn_shape, name + "_w_down")
        dh, got_dn = mm(dgu, w_gu, tb=True, a_halves=True, out_dtype=h.dtype, name=name + "_up_da",
                        carry=[rs_chip_routine(h_dn)])
        dw_gu = mm(h, dgu, ta=True, b_halves=True, out_dtype=w_gu.dtype, name=name + "_up_dw")
        d_dn = _rs_end(h_dn, got_dn, False, dn_shape, name + "_w_down")
        dsh = [_rs_end(x, got, cols, shp, name + "_" + tag) for x, got, (cols, shp, tag) in zip(hs, gots, kinds)]
        return (dh, dw_gu, d_dn, *dsh)

    f.defvjp(fwd, bwd)
    out = f(h, w_gu, w_dn_shard, *[s for s, _, _ in carry])
    return out[0], list(out[1:])


def rmsmod(x, g, scale, shift, *, name):
    B, S, D = x.shape
    ts = _tile(S, 256, 8)
    row = pl.BlockSpec((1, ts, D), lambda b, j: (b, j, 0))
    per_b = pl.BlockSpec((1, 1, D), lambda b, j: (b, 0, 0))
    gspec = pl.BlockSpec((1, D), lambda b, j: (0, 0))

    def fwd_call(x, g, scale, shift):
        def body(x_ref, g_ref, sc_ref, sh_ref, o_ref):
            xv = x_ref[0]
            r = lax.rsqrt(jnp.mean(xv * xv, axis=-1, keepdims=True) + EPS)
            y = xv * r * g_ref[...]
            o_ref[0] = (y * (1.0 + sc_ref[0]) + sh_ref[0]).astype(o_ref.dtype)

        return pl.pallas_call(
            body, name=name + "_fwd", grid=(B, S // ts),
            in_specs=[row, gspec, per_b, per_b], out_specs=row,
            out_shape=jax.ShapeDtypeStruct((B, S, D), BF16), compiler_params=_cp(),
        )(x, g, scale, shift)

    def bwd_call(x, g, scale, dh):
        def body(x_ref, g_ref, sc_ref, dh_ref, dx_ref, dg_ref, dsc_ref, dsh_ref):
            j = pl.program_id(1)
            xv = x_ref[0]
            dh = dh_ref[0].astype(F32)
            r = lax.rsqrt(jnp.mean(xv * xv, axis=-1, keepdims=True) + EPS)
            xn = xv * r
            gv = g_ref[...]
            dy = dh * (1.0 + sc_ref[0])
            dxn = dy * gv
            dx_ref[0] = r * (dxn - xn * jnp.mean(dxn * xn, axis=-1, keepdims=True))

            @pl.when(j == 0)
            def _():
                dg_ref[...] = jnp.zeros_like(dg_ref)
                dsc_ref[...] = jnp.zeros_like(dsc_ref)
                dsh_ref[...] = jnp.zeros_like(dsh_ref)

            dg_ref[0] += jnp.sum(dy * xn, axis=0, keepdims=True)
            dsc_ref[0] += jnp.sum(dh * xn * gv, axis=0, keepdims=True)
            dsh_ref[0] += jnp.sum(dh, axis=0, keepdims=True)

        vec = jax.ShapeDtypeStruct((B, 1, D), F32)
        return pl.pallas_call(
            body, name=name + "_bwd", grid=(B, S // ts),
            in_specs=[row, gspec, per_b, row], out_specs=[row, per_b, per_b, per_b],
            out_shape=[jax.ShapeDtypeStruct((B, S, D), F32), vec, vec, vec], compiler_params=_cp(),
        )(x, g, scale, dh)

    @jax.custom_vjp
    def f(x, g, scale, shift):
        return fwd_call(x, g, scale, shift)

    def fwd(x, g, scale, shift):
        return fwd_call(x, g, scale, shift), (x, g, scale)

    def bwd(res, dh):
        x, g, scale = res
        dx, dg, dsc, dsh = bwd_call(x, g, scale, dh)
        return dx, jnp.sum(dg, axis=0), dsc, dsh

    f.defvjp(fwd, bwd)
    return f(x, g, scale, shift)


def group_rmsnorm(x, g, *, gs, out_dtype, name):
    T, W = x.shape
    ng = W // gs
    tr = _tile(T, 512, 8)
    row = pl.BlockSpec((tr, W), lambda i: (i, 0))
    gspec = pl.BlockSpec((1, W), lambda i: (0, 0))

    def fwd_call(x, g):
        def body(x_ref, g_ref, o_ref):
            for i in range(ng):
                sl = slice(i * gs, (i + 1) * gs)
                xv = x_ref[:, sl]
                r = lax.rsqrt(jnp.mean(xv * xv, axis=-1, keepdims=True) + EPS)
                o_ref[:, sl] = (xv * r * g_ref[:, sl]).astype(o_ref.dtype)

        return pl.pallas_call(
            body, name=name + "_fwd", grid=(T // tr,), in_specs=[row, gspec], out_specs=row,
            out_shape=jax.ShapeDtypeStruct((T, W), out_dtype), compiler_params=_cp(),
        )(x, g)

    def bwd_call(x, g, dy):
        def body(x_ref, g_ref, dy_ref, dx_ref, dg_ref):
            @pl.when(pl.program_id(0) == 0)
            def _():
                dg_ref[...] = jnp.zeros_like(dg_ref)

            for i in range(ng):
                sl = slice(i * gs, (i + 1) * gs)
                xv = x_ref[:, sl]
                dyv = dy_ref[:, sl].astype(F32)
                r = lax.rsqrt(jnp.mean(xv * xv, axis=-1, keepdims=True) + EPS)
                xn = xv * r
                dxn = dyv * g_ref[:, sl]
                dx_ref[:, sl] = r * (dxn - xn * jnp.mean(dxn * xn, axis=-1, keepdims=True))
                dg_ref[:, sl] += jnp.sum(dyv * xn, axis=0, keepdims=True)

        return pl.pallas_call(
            body, name=name + "_bwd", grid=(T // tr,), in_specs=[row, gspec, row],
            out_specs=[row, gspec],
            out_shape=[jax.ShapeDtypeStruct((T, W), F32), jax.ShapeDtypeStruct((1, W), F32)],
            compiler_params=_cp(),
        )(x, g, dy)

    @jax.custom_vjp
    def f(x, g):
        return fwd_call(x, g)

    def fwd(x, g):
        return fwd_call(x, g), (x, g)

    def bwd(res, dy):
        return bwd_call(res[0], res[1], dy)

    f.defvjp(fwd, bwd)
    return f(x, g)


def rope_tables(seq_len, rot_dim, width, slot=0, lead=0):
    rows = seq_len // GRID_W
    row_idx = jnp.repeat(jnp.arange(rows), GRID_W).astype(F32)
    col_idx = jnp.tile(jnp.arange(GRID_W), rows).astype(F32)
    axis_dim = rot_dim // 2
    inv_freq = jnp.power(ROPE_THETA, -jnp.arange(0, axis_dim, 2, dtype=F32) / axis_dim)
    ang_r = row_idx[:, None] * inv_freq[None, :]
    ang_c = col_idx[:, None] * inv_freq[None, :]
    cos = jnp.concatenate([jnp.cos(ang_r), jnp.cos(ang_r), jnp.cos(ang_c), jnp.cos(ang_c)], axis=-1)
    sin = jnp.concatenate([-jnp.sin(ang_r), jnp.sin(ang_r), -jnp.sin(ang_c), jnp.sin(ang_c)], axis=-1)
    if slot:
        ones = jnp.ones((seq_len, 1), F32)
        cos = jnp.concatenate([ones * jnp.ones((1, lead)), cos, ones * jnp.ones((1, slot - lead - rot_dim))], axis=-1)
        sin = jnp.concatenate([ones * jnp.zeros((1, lead)), sin, ones * jnp.zeros((1, slot - lead - rot_dim))], axis=-1)
        rot_dim = slot
    reps = width // rot_dim
    return jnp.tile(cos, (1, reps)), jnp.tile(sin, (1, reps))


def rope(x, cos, sin, *, d, name):
    B, S, W = x.shape
    ts = _tile(S, 512, 8)
    row = pl.BlockSpec((1, ts, W), lambda b, j: (b, j, 0))
    tab = pl.BlockSpec((ts, W), lambda b, j: (j, 0))

    def call(x, inverse, nm):
        def body(x_ref, c_ref, s_ref, o_ref):
            xv = x_ref[0]
            lane = lax.broadcasted_iota(jnp.int32, xv.shape, 1)
            first = (lane // d) % 2 == 0

            def swap(v):
                return jnp.where(first, pltpu.roll(v, W - d, 1), pltpu.roll(v, d, 1))

            if inverse:
                o_ref[0] = xv * c_ref[...] + swap(xv * s_ref[...])
            else:
                o_ref[0] = xv * c_ref[...] + swap(xv) * s_ref[...]

        return pl.pallas_call(
            body, name=nm, grid=(B, S // ts), in_specs=[row, tab, tab], out_specs=row,
            out_shape=jax.ShapeDtypeStruct((B, S, W), F32), compiler_params=_cp(),
        )(x, cos, sin)

    @jax.custom_vjp
    def f(x):
        return call(x, False, name + "_fwd")

    def fwd(x):
        return call(x, False, name + "_fwd"), None

    def bwd(_, g):
        return (call(g, True, name + "_bwd"),)

    f.defvjp(fwd, bwd)
    return f(x)


def attention(q, k, v, *, H, Hkv, dk, dv, scale, name, carry=(), k_shared=None):
    B, S, _ = q.shape
    rep = H // Hkv
    tq = _tile(S, 256, 8)
    kinds = [(cols, s.shape, tag) for s, cols, tag in carry]
    shared = [] if k_shared is None else [k_shared]
    ke_spec3 = [pl.BlockSpec((1, S, dk), lambda b, h, i: (b, 0, 0))] if shared else []
    ke_spec4 = [pl.BlockSpec((1, S, dk), lambda b, g, r, i: (b, 0, 0))] if shared else []

    def keys(k_ref, ke_refs):
        return _bf(k_ref[0] + ke_refs[0][0]) if ke_refs else _bf(k_ref[0])

    def fwd_call(q, k, v, ke, routines=()):
        def body(q_ref, k_ref, v_ref, *rest):
            ke_refs, (o_ref, lse_ref) = rest[:len(shared)], rest[len(shared):]
            s = _dot(_bf(q_ref[0]), keys(k_ref, ke_refs), NT) * scale
            m = jnp.max(s, axis=-1, keepdims=True)
            p = jnp.exp(s - m)
            l = jnp.sum(p, axis=-1, keepdims=True)
            o_ref[0] = _dot(_bf(p), _bf(v_ref[0]), NN) / l
            lse_ref[0, 0] = m + jnp.log(l)

        return carry_call(
            body, name=name + "_fwd", grid=(B, H, S // tq),
            in_specs=[pl.BlockSpec((1, tq, dk), lambda b, h, i: (b, i, h)),
                      pl.BlockSpec((1, S, dk), lambda b, h, i: (b, 0, h // rep)),
                      pl.BlockSpec((1, S, dv), lambda b, h, i: (b, 0, h // rep))] + ke_spec3,
            out_specs=[pl.BlockSpec((1, tq, dv), lambda b, h, i: (b, i, h)),
                       pl.BlockSpec((1, 1, tq, 1), lambda b, h, i: (b, h, i, 0))],
            out_shape=[jax.ShapeDtypeStruct((B, S, H * dv), F32),
                       jax.ShapeDtypeStruct((B, H, S, 1), F32)],
            carry=routines,
        )(q, k, v, *ke)

    def bwd_call(q, k, v, ke, o, do, lse, routines=()):
        def body(q_ref, k_ref, v_ref, *rest):
            ke_refs, rest = rest[:len(shared)], rest[len(shared):]
            o_ref, do_ref, lse_ref, dq_ref, dk_ref, dv_ref = rest[:6]

            @pl.when((pl.program_id(2) == 0) & (pl.program_id(3) == 0))
            def _():
                dk_ref[...] = jnp.zeros_like(dk_ref)
                dv_ref[...] = jnp.zeros_like(dv_ref)

            if shared:
                dke_ref = rest[6]

                @pl.when((pl.program_id(1) == 0) & (pl.program_id(2) == 0) & (pl.program_id(3) == 0))
                def _():
                    dke_ref[...] = jnp.zeros_like(dke_ref)

            qb = _bf(q_ref[0])
            kb = keys(k_ref, ke_refs)
            dov = do_ref[0]
            dob = _bf(dov)
            s = _dot(qb, kb, NT) * scale
            p = jnp.exp(s - lse_ref[0, 0])
            delta = jnp.sum(dov * o_ref[0], axis=-1, keepdims=True)
            dp = _dot(dob, _bf(v_ref[0]), NT)
            dsb = _bf(p * (dp - delta))
            dq_ref[0] = _dot(dsb, kb, NN) * scale
            dkv = _dot(dsb, qb, TN) * scale
            dk_ref[0] += dkv
            if shared:
                dke_ref[0] += dkv
            dv_ref[0] += _dot(_bf(p), dob, TN)

        qs = pl.BlockSpec((1, tq, dk), lambda b, g, r, i: (b, i, g * rep + r))
        os_ = pl.BlockSpec((1, tq, dv), lambda b, g, r, i: (b, i, g * rep + r))
        ks = pl.BlockSpec((1, S, dk), lambda b, g, r, i: (b, 0, g))
        vs = pl.BlockSpec((1, S, dv), lambda b, g, r, i: (b, 0, g))
        col = pl.BlockSpec((1, 1, tq, 1), lambda b, g, r, i: (b, g * rep + r, i, 0))
        return carry_call(
            body, name=name + "_bwd", grid=(B, Hkv, rep, S // tq),
            in_specs=[qs, ks, vs] + ke_spec4 + [os_, os_, col], out_specs=[qs, ks, vs] + ke_spec4,
            out_shape=[jax.ShapeDtypeStruct(q.shape, F32), jax.ShapeDtypeStruct(k.shape, F32),
                       jax.ShapeDtypeStruct(v.shape, F32)] + [jax.ShapeDtypeStruct(x.shape, F32) for x in ke],
            carry=routines,
        )(q, k, v, *ke, o, do, lse)

    n_ke = len(shared)

    def run(q, k, v, *rest):
        ke, shards = rest[:n_ke], rest[n_ke:]
        routines = [ag_routine(_shard_prep(s, cols), cols) for s, (cols, _, _) in zip(shards, kinds)]
        o, lse, *fulls = fwd_call(q, k, v, ke, routines)
        return (o, *[_full_post(fu, cols) for fu, (cols, _, _) in zip(fulls, kinds)]), lse

    @jax.custom_vjp
    def f(q, k, v, *rest):
        return run(q, k, v, *rest)[0]

    def fwd(q, k, v, *rest):
        outs, lse = run(q, k, v, *rest)
        return outs, (q, k, v, rest[:n_ke], outs[0], lse)

    def bwd(res, cts):
        q, k, v, ke, o, lse = res
        hs = [_rs_begin(gf, cols, shp, name + "_" + tag) for gf, (cols, shp, tag) in zip(cts[1:], kinds)]
        dq, dk_, dv_, *more = bwd_call(q, k, v, ke, o, cts[0], lse, [rs_chip_routine(h) for h in hs])
        dke, gots = more[:n_ke], more[n_ke:]
        dsh = [_rs_end(h, got, cols, shp, name + "_" + tag)
               for h, got, (cols, shp, tag) in zip(hs, gots, kinds)]
        return (dq, dk_, dv_, *dke, *dsh)

    f.defvjp(fwd, bwd)
    out = f(q, k, v, *shared, *[s for s, _, _ in carry])
    return (out[0], list(out[1:])) if carry else out[0]


def conv_silu(x, w, b, *, name):
    B, S, C = x.shape
    tc = _tile(C, 256)
    pad = SSD_K // 2
    xs = pl.BlockSpec((1, S, tc), lambda bi, j: (bi, 0, j))
    ws = pl.BlockSpec((SSD_K, tc), lambda bi, j: (0, j))
    bs = pl.BlockSpec((1, tc), lambda bi, j: (0, j))

    def shifted(v, off):
        if off == 0:
            return v
        t = lax.broadcasted_iota(jnp.int32, v.shape, 0)
        r = pltpu.roll(v, (-off) % S, 0)
        return jnp.where((t + off >= 0) & (t + off < S), r, 0.0)

    def pre_act(xv, wv, bv):
        u = jnp.zeros_like(xv) + bv
        for k in range(SSD_K):
            u = u + wv[k:k + 1, :] * shifted(xv, k - pad)
        return u

    def fwd_call(x, w, b):
        def body(x_ref, w_ref, b_ref, o_ref):
            u = pre_act(x_ref[0], w_ref[...], b_ref[...])
            o_ref[0] = u * _sigmoid(u)

        return pl.pallas_call(
            body, name=name + "_fwd", grid=(B, C // tc), in_specs=[xs, ws, bs], out_specs=xs,
            out_shape=jax.ShapeDtypeStruct((B, S, C), F32), compiler_params=_cp(),
        )(x, w, b)

    def bwd_call(x, w, b, dy):
        def body(x_ref, w_ref, b_ref, dy_ref, dx_ref, dw_ref):
            xv = x_ref[0]
            wv = w_ref[...]
            u = pre_act(xv, wv, b_ref[...])
            sg = _sigmoid(u)
            du = dy_ref[0] * (sg * (1.0 + u * (1.0 - sg)))
            dx = jnp.zeros_like(xv)
            for k in range(SSD_K):
                dx = dx + wv[k:k + 1, :] * shifted(du, pad - k)
                dw_ref[0, k:k + 1, :] = jnp.sum(du * shifted(xv, k - pad), axis=0, keepdims=True)
            dw_ref[0, SSD_K:SSD_K + 1, :] = jnp.sum(du, axis=0, keepdims=True)
            dw_ref[0, SSD_K + 1:8, :] = jnp.zeros((8 - SSD_K - 1, tc), F32)
            dx_ref[0] = dx

        return pl.pallas_call(
            body, name=name + "_bwd", grid=(B, C // tc), in_specs=[xs, ws, bs, xs],
            out_specs=[xs, pl.BlockSpec((1, 8, tc), lambda bi, j: (bi, 0, j))],
            out_shape=[jax.ShapeDtypeStruct((B, S, C), F32), jax.ShapeDtypeStruct((B, 8, C), F32)],
            compiler_params=_cp(),
        )(x, w, b, dy)

    @jax.custom_vjp
    def f(x, w, b):
        return fwd_call(x, w, b)

    def fwd(x, w, b):
        return fwd_call(x, w, b), (x, w, b)

    def bwd(res, dy):
        x, w, b = res
        dx, dwb = bwd_call(x, w, b, dy)
        dwb = jnp.sum(dwb, axis=0)
        return dx, dwb[:SSD_K], dwb[SSD_K:SSD_K + 1]

    f.defvjp(fwd, bwd)
    return f(x, w, b)


def _softplus(x):
    return jnp.maximum(x, 0.0) + jnp.log1p(jnp.exp(-jnp.abs(x)))


def _ssd_prep(raw, raw_t, brow, arow, bcol, acol, rev):
    li = lax.broadcasted_iota(jnp.int32, (CHUNK, CHUNK), 0)
    ki = lax.broadcasted_iota(jnp.int32, (CHUNK, CHUNK), 1)
    later = (li <= ki) if rev else (li >= ki)
    dt = _softplus(raw + brow)
    a = dt * arow
    cs = _dotf(later.astype(F32), a)
    tot = jnp.sum(a, axis=0, keepdims=True)
    a_t = _softplus(raw_t + bcol) * acol
    earlier = (li >= ki) if rev else (li <= ki)
    cs_t = _dotf(a_t, earlier.astype(F32))
    return dt, a, cs, tot, cs_t, later


def _lane_pick(mat, j):
    lane = lax.broadcasted_iota(jnp.int32, mat.shape, 1)
    return jnp.sum(jnp.where(lane == j, mat, 0.0), axis=1, keepdims=True)


def _head_sum(t, first):
    s0 = jnp.sum(jnp.where(first, t, 0.0), axis=1, keepdims=True)
    s1 = jnp.sum(jnp.where(first, 0.0, t), axis=1, keepdims=True)
    return s0, s1


def ssd_scan(xbc, raw, brow, arow, *, rev, name):
    B, S, _ = xbc.shape
    NC = S // CHUNK
    off = SSD_H if rev else 0
    n_dt = 2 * SSD_H

    def chunk_of(c):
        return (NC - 1 - c) if rev else c

    def specs(cmap):
        return dict(
            x=pl.BlockSpec((1, CHUNK, SSD_INNER), lambda b, c: (b, cmap(c), 0)),
            bm=pl.BlockSpec((1, CHUNK, 2 * SSD_N), lambda b, c: (b, cmap(c), SSD_INNER // (2 * SSD_N))),
            cm=pl.BlockSpec((1, CHUNK, 2 * SSD_N), lambda b, c: (b, cmap(c), SSD_INNER // (2 * SSD_N) + 1)),
            raw=pl.BlockSpec((1, CHUNK, LANES), lambda b, c: (b, cmap(c), 0)),
            raw_t=pl.BlockSpec((1, n_dt, CHUNK), lambda b, c: (b, 0, cmap(c))),
            row=pl.BlockSpec((1, LANES), lambda b, c: (0, 0)),
            colv=pl.BlockSpec((n_dt, 1), lambda b, c: (0, 0)),
            hs=pl.BlockSpec((1, 1, N_PAIR, SSD_N, LANES), lambda b, c: (b, cmap(c), 0, 0, 0)),
        )

    def head_terms(prep, j, first_dummy=None):
        dt, a, cs, tot, cs_t, later = prep
        cs_c = _lane_pick(cs, j)
        cs_r = cs_t[j:j + 1, :]
        dt_c = _lane_pick(dt, j)
        tot_j = _lane_pick(tot, j)
        L = jnp.exp(jnp.where(later, cs_c - cs_r, -1e30))
        return cs_c, cs_r, dt_c, tot_j, L

    def fwd_call(xbc, raw, raw_t, brow, arow, bcol, acol):
        def body(x_ref, bm_ref, cm_ref, raw_ref, rawt_ref, brow_ref, arow_ref, bcol_ref, acol_ref,
                 y_ref, hs_ref, st_ref):
            @pl.when(pl.program_id(1) == 0)
            def _():
                st_ref[...] = jnp.zeros_like(st_ref)

            prep = _ssd_prep(raw_ref[0], rawt_ref[0], brow_ref[...], arow_ref[...],
                             bcol_ref[...], acol_ref[...], rev)
            lane = lax.broadcasted_iota(jnp.int32, (CHUNK, LANES), 1)
            first = lane < SSD_P
            for g in range(SSD_G):
                Bg = _bf(bm_ref[0, :, g * SSD_N:(g + 1) * SSD_N])
                Cg = _bf(cm_ref[0, :, g * SSD_N:(g + 1) * SSD_N])
                G = _dot(Cg, Bg, NT)
                for pp in range(N_PAIR // SSD_G):
                    pi = g * (N_PAIR // SSD_G) + pp
                    c0, _, d0, t0, L0 = head_terms(prep, off + 2 * pi)
                    c1, _, d1, t1, L1 = head_terms(prep, off + 2 * pi + 1)
                    xd = x_ref[0, :, pi * LANES:(pi + 1) * LANES] * jnp.where(first, d0, d1)
                    xdb = _bf(xd)
                    y = jnp.where(first, _dot(_bf(G * L0), xdb, NN), _dot(_bf(G * L1), xdb, NN))
                    dec = jnp.where(first, jnp.exp(t0 - c0), jnp.exp(t1 - c1))
                    h_prev = st_ref[pi]
                    hs_ref[0, 0, pi] = h_prev
                    y = y + _dot(Cg, _bf(h_prev), NN) * jnp.where(first, jnp.exp(c0), jnp.exp(c1))
                    y_ref[0, :, pi * LANES:(pi + 1) * LANES] = y
                    etot = jnp.where(first[:1], jnp.exp(t0), jnp.exp(t1))
                    st_ref[pi] = h_prev * etot + _dot(Bg, _bf(xd * dec), TN)

        sp = specs(chunk_of)
        return pl.pallas_call(
            body, name=name + "_fwd", grid=(B, NC),
            in_specs=[sp["x"], sp["bm"], sp["cm"], sp["raw"], sp["raw_t"], sp["row"], sp["row"],
                      sp["colv"], sp["colv"]],
            out_specs=[sp["x"], sp["hs"]],
            out_shape=[jax.ShapeDtypeStruct((B, S, SSD_INNER), F32),
                       jax.ShapeDtypeStruct((B, NC, N_PAIR, SSD_N, LANES), F32)],
            scratch_shapes=[pltpu.VMEM((N_PAIR, SSD_N, LANES), F32)],
            compiler_params=_cp(),
        )(xbc, xbc, xbc, raw, raw_t, brow, arow, bcol, acol)

    def bwd_call(xbc, raw, raw_t, brow, arow, bcol, acol, hs, dy):
        def body(x_ref, bm_ref, cm_ref, raw_ref, rawt_ref, brow_ref, arow_ref, bcol_ref, acol_ref,
                 hs_ref, dy_ref, dxbc_ref, draw_ref, da_ref, dst_ref):
            @pl.when(pl.program_id(1) == 0)
            def _():
                dst_ref[...] = jnp.zeros_like(dst_ref)
                da_ref[...] = jnp.zeros_like(da_ref)

            raw_v = raw_ref[0]
            prep = _ssd_prep(raw_v, rawt_ref[0], brow_ref[...], arow_ref[...],
                             bcol_ref[...], acol_ref[...], rev)
            dt, a, cs, tot, cs_t, later = prep
            li = lax.broadcasted_iota(jnp.int32, (CHUNK, CHUNK), 0)
            ki = lax.broadcasted_iota(jnp.int32, (CHUNK, CHUNK), 1)
            later_t = (li >= ki) if rev else (li <= ki)
            lane = lax.broadcasted_iota(jnp.int32, (CHUNK, LANES), 1)
            first = lane < SSD_P
            dcs_all = jnp.zeros((CHUNK, LANES), F32)
            ddt_all = jnp.zeros((CHUNK, LANES), F32)
            dtot_all = jnp.zeros((1, LANES), F32)
            for g in range(SSD_G):
                Bg = _bf(bm_ref[0, :, g * SSD_N:(g + 1) * SSD_N])
                Cg = _bf(cm_ref[0, :, g * SSD_N:(g + 1) * SSD_N])
                G = _dot(Cg, Bg, NT)
                Gt = _dot(Bg, Cg, NT)
                dG = jnp.zeros((CHUNK, CHUNK), F32)
                dB = jnp.zeros((CHUNK, SSD_N), F32)
                dC = jnp.zeros((CHUNK, SSD_N), F32)
                for pp in range(N_PAIR // SSD_G):
                    pi = g * (N_PAIR // SSD_G) + pp
                    j0, j1 = off + 2 * pi, off + 2 * pi + 1
                    c0, r0, d0, t0, L0 = head_terms(prep, j0)
                    c1, r1, d1, t1, L1 = head_terms(prep, j1)
                    Lt0 = jnp.exp(jnp.where(later_t, r0 - c0, -1e30))
                    Lt1 = jnp.exp(jnp.where(later_t, r1 - c1, -1e30))
                    xv = x_ref[0, :, pi * LANES:(pi + 1) * LANES]
                    dtp = jnp.where(first, d0, d1)
                    xd = xv * dtp
                    xdb = _bf(xd)
                    dyv = dy_ref[0, :, pi * LANES:(pi + 1) * LANES]
                    dyb = _bf(dyv)
                    dec = jnp.where(first, jnp.exp(t0 - c0), jnp.exp(t1 - c1))
                    ecs = jnp.where(first, jnp.exp(c0), jnp.exp(c1))
                    et0, et1 = jnp.exp(t0), jnp.exp(t1)
                    etot = jnp.where(first[:1], et0, et1)
                    h_prev = hs_ref[0, 0, pi]
                    hpb = _bf(h_prev)
                    dhn = dst_ref[pi]
                    dhb = _bf(dhn)
                    W0, W1 = G * L0, G * L1
                    Wt0, Wt1 = Gt * Lt0, Gt * Lt1
                    bdh = _dot(Bg, dhb, NN)
                    dxd = jnp.where(first, _dot(_bf(Wt0), dyb, NN), _dot(_bf(Wt1), dyb, NN)) + bdh * dec
                    dy0 = _bf(jnp.where(first, dyv, 0.0))
                    dy1 = _bf(jnp.where(first, 0.0, dyv))
                    Q0, Q1 = _dot(dy0, xdb, NT), _dot(dy1, xdb, NT)
                    Qt0, Qt1 = _dot(xdb, dy0, NT), _dot(xdb, dy1, NT)
                    dG = dG + Q0 * L0 + Q1 * L1
                    dcs0 = (jnp.sum(Q0 * W0, axis=1, keepdims=True)
                            - jnp.sum(Qt0 * Wt0, axis=1, keepdims=True))
                    dcs1 = (jnp.sum(Q1 * W1, axis=1, keepdims=True)
                            - jnp.sum(Qt1 * Wt1, axis=1, keepdims=True))
                    dye = dyv * ecs
                    dyeb = _bf(dye)
                    s0, s1 = _head_sum(dye * _dot(Cg, hpb, NN), first)
                    dcs0, dcs1 = dcs0 + s0, dcs1 + s1
                    dC = dC + _dot(dyeb, hpb, NT)
                    dB = dB + _dot(_bf(xd * dec), dhb, NT)
                    u0, u1 = _head_sum(xd * bdh * dec, first)
                    dcs0, dcs1 = dcs0 - u0, dcs1 - u1
                    w = jnp.sum(dhn * h_prev, axis=0, keepdims=True)
                    w0, w1 = _head_sum(w, first[:1])
                    dt0 = jnp.sum(u0, axis=0, keepdims=True) + et0 * w0
                    dt1 = jnp.sum(u1, axis=0, keepdims=True) + et1 * w1
                    dst_ref[pi] = _dot(Cg, dyeb, TN) + dhn * etot
                    q0, q1 = _head_sum(dxd * xv, first)
                    dxbc_ref[0, :, pi * LANES:(pi + 1) * LANES] = dxd * dtp
                    dcs_all = dcs_all + jnp.where(lane == j0, dcs0, 0.0) + jnp.where(lane == j1, dcs1, 0.0)
                    ddt_all = ddt_all + jnp.where(lane == j0, q0, 0.0) + jnp.where(lane == j1, q1, 0.0)
                    dtot_all = (dtot_all + jnp.where(lane[:1] == j0, dt0, 0.0)
                                + jnp.where(lane[:1] == j1, dt1, 0.0))
                dGb = _bf(dG)
                dC = dC + _dot(dGb, Bg, NN)
                dB = dB + _dot(dGb, Cg, TN)
                dxbc_ref[0, :, SSD_INNER + g * SSD_N:SSD_INNER + (g + 1) * SSD_N] = dB
                dxbc_ref[0, :, SSD_INNER + (SSD_G + g) * SSD_N:SSD_INNER + (SSD_G + g + 1) * SSD_N] = dC
            da = _dotf(later_t.astype(F32), dcs_all) + dtot_all
            ddt = ddt_all + da * arow_ref[...]
            da_ref[0] += jnp.sum(da * dt, axis=0, keepdims=True)
            draw_ref[0] = ddt * _sigmoid(raw_v + brow_ref[...])

        def rchunk(c):
            return c if rev else (NC - 1 - c)

        sp = specs(rchunk)
        full = pl.BlockSpec((1, CHUNK, SSD_CONV_DIM), lambda b, c: (b, rchunk(c), 0))
        return pl.pallas_call(
            body, name=name + "_bwd", grid=(B, NC),
            in_specs=[sp["x"], sp["bm"], sp["cm"], sp["raw"], sp["raw_t"], sp["row"], sp["row"],
                      sp["colv"], sp["colv"], sp["hs"], sp["x"]],
            out_specs=[full, sp["raw"], pl.BlockSpec((1, 1, LANES), lambda b, c: (b, 0, 0))],
            out_shape=[jax.ShapeDtypeStruct((B, S, SSD_CONV_DIM), F32),
                       jax.ShapeDtypeStruct((B, S, LANES), F32),
                       jax.ShapeDtypeStruct((B, 1, LANES), F32)],
            scratch_shapes=[pltpu.VMEM((N_PAIR, SSD_N, LANES), F32)],
            compiler_params=_cp(),
        )(xbc, xbc, xbc, raw, raw_t, brow, arow, bcol, acol, hs, dy)

    def aux(raw, brow, arow):
        raw_t = jnp.swapaxes(raw[:, :, :n_dt], 1, 2)
        return raw_t, brow[0, :n_dt][:, None], arow[0, :n_dt][:, None]

    @jax.custom_vjp
    def f(xbc, raw, brow, arow):
        raw_t, bcol, acol = aux(raw, brow, arow)
        return fwd_call(xbc, raw, raw_t, brow, arow, bcol, acol)[0]

    def fwd(xbc, raw, brow, arow):
        raw_t, bcol, acol = aux(raw, brow, arow)
        y, hs = fwd_call(xbc, raw, raw_t, brow, arow, bcol, acol)
        return y, (xbc, raw, brow, arow, hs)

    def bwd(res, dy):
        xbc, raw, brow, arow, hs = res
        raw_t, bcol, acol = aux(raw, brow, arow)
        dxbc, draw, da = bwd_call(xbc, raw, raw_t, brow, arow, bcol, acol, hs, dy)
        dbrow = jnp.sum(draw, axis=(0, 1))[None, :]
        return dxbc, draw, dbrow, jnp.sum(da, axis=0)

    f.defvjp(fwd, bwd)
    return f(xbc, raw, brow, arow)


def ssd_out(yf, yb, xbc, z, dsk, g, *, name):
    T, W = yf.shape
    gs = W // SSD_G
    tr = _tile(T, 512, 8)
    row = pl.BlockSpec((tr, W), lambda i: (i, 0))
    vec = pl.BlockSpec((1, W), lambda i: (0, 0))

    def normed(yv, gv):
        outs, rs = [], []
        for i in range(SSD_G):
            sl = slice(i * gs, (i + 1) * gs)
            r = lax.rsqrt(jnp.mean(yv[:, sl] * yv[:, sl], axis=-1, keepdims=True) + EPS)
            rs.append(r)
            outs.append(yv[:, sl] * r)
        return outs, rs

    def fwd_call(yf, yb, xbc, z, dsk, g):
        def body(yf_ref, yb_ref, xs_ref, z_ref, dsk_ref, g_ref, o_ref):
            zv = z_ref[...]
            yv = (yf_ref[...] + yb_ref[...] + xs_ref[...] * dsk_ref[...]) * (zv * _sigmoid(zv))
            outs, _ = normed(yv, g_ref[...])
            for i in range(SSD_G):
                sl = slice(i * gs, (i + 1) * gs)
                o_ref[:, sl] = (outs[i] * g_ref[:, sl]).astype(o_ref.dtype)

        return pl.pallas_call(
            body, name=name + "_fwd", grid=(T // tr,), in_specs=[row, row, row, row, vec, vec],
            out_specs=row, out_shape=jax.ShapeDtypeStruct((T, W), BF16), compiler_params=_cp(),
        )(yf, yb, xbc, z, dsk, g)

    def bwd_call(yf, yb, xbc, z, dsk, g, do):
        def body(yf_ref, yb_ref, xs_ref, z_ref, dsk_ref, g_ref, do_ref, dy_ref, dxs_ref, dz_ref,
                 ddsk_ref, dg_ref):
            @pl.when(pl.program_id(0) == 0)
            def _():
                ddsk_ref[...] = jnp.zeros_like(ddsk_ref)
                dg_ref[...] = jnp.zeros_like(dg_ref)

            zv = z_ref[...]
            sg = _sigmoid(zv)
            sz = zv * sg
            xs = xs_ref[...]
            pre = yf_ref[...] + yb_ref[...] + xs * dsk_ref[...]
            yv = pre * sz
            outs, rs = normed(yv, g_ref[...])
            for i in range(SSD_G):
                sl = slice(i * gs, (i + 1) * gs)
                dov = do_ref[:, sl].astype(F32)
                xn = outs[i]
                dxn = dov * g_ref[:, sl]
                dyv = rs[i] * (dxn - xn * jnp.mean(dxn * xn, axis=-1, keepdims=True))
                dg_ref[:, sl] += jnp.sum(dov * xn, axis=0, keepdims=True)
                dpre = dyv * sz[:, sl]
                dy_ref[:, sl] = dpre
                dxs_ref[:, sl] = dpre * dsk_ref[:, sl]
                ddsk_ref[:, sl] += jnp.sum(dpre * xs[:, sl], axis=0, keepdims=True)
                dz_ref[:, sl] = dyv * pre[:, sl] * (sg[:, sl] * (1.0 + zv[:, sl] * (1.0 - sg[:, sl])))

        o = jax.ShapeDtypeStruct((T, W), F32)
        v = jax.ShapeDtypeStruct((1, W), F32)
        return pl.pallas_call(
            body, name=name + "_bwd", grid=(T // tr,), in_specs=[row, row, row, row, vec, vec, row],
            out_specs=[row, row, row, vec, vec], out_shape=[o, o, o, v, v], compiler_params=_cp(),
        )(yf, yb, xbc, z, dsk, g, do)

    @jax.custom_vjp
    def f(yf, yb, xbc, z, dsk, g):
        return fwd_call(yf, yb, xbc, z, dsk, g)

    def fwd(yf, yb, xbc, z, dsk, g):
        return fwd_call(yf, yb, xbc, z, dsk, g), (yf, yb, xbc, z, dsk, g)

    def bwd(res, do):
        dy, dxs, dz, ddsk, dg = bwd_call(*res, do)
        dxbc = jnp.pad(dxs, ((0, 0), (0, res[2].shape[1] - W)))
        return dy, dy, dxbc, dz, ddsk, dg

    f.defvjp(fwd, bwd)
    return f(yf, yb, xbc, z, dsk, g)


def swiglu(gu, *, name):
    T, F2 = gu.shape
    Fh = F2 // 2
    tr, tf = _tile(T, 512, 8), _tile(Fh, 512)
    nf = Fh // tf
    gs = pl.BlockSpec((tr, tf), lambda i, j: (i, j))
    us = pl.BlockSpec((tr, tf), lambda i, j: (i, j + nf))

    def fwd_call(gu):
        def body(g_ref, u_ref, o_ref):
            gv = g_ref[...].astype(F32)
            o_ref[...] = (gv * _sigmoid(gv) * u_ref[...].astype(F32)).astype(o_ref.dtype)

        return pl.pallas_call(
            body, name=name + "_fwd", grid=(T // tr, nf), in_specs=[gs, us], out_specs=gs,
            out_shape=jax.ShapeDtypeStruct((T, Fh), BF16), compiler_params=_cp(),
        )(gu, gu)

    def bwd_call(gu, da):
        def body(g_ref, u_ref, da_ref, dgu_ref):
            j = pl.program_id(1)
            gv = g_ref[...].astype(F32)
            uv = u_ref[...].astype(F32)
            dav = da_ref[...].astype(F32)
            sg = _sigmoid(gv)

            @pl.when(j < nf)
            def _():
                dgu_ref[...] = (dav * uv * (sg * (1.0 + gv * (1.0 - sg)))).astype(dgu_ref.dtype)

            @pl.when(j >= nf)
            def _():
                dgu_ref[...] = (dav * gv * sg).astype(dgu_ref.dtype)

        gsel = pl.BlockSpec((tr, tf), lambda i, j: (i, j % nf))
        usel = pl.BlockSpec((tr, tf), lambda i, j: (i, j % nf + nf))
        return pl.pallas_call(
            body, name=name + "_bwd", grid=(T // tr, 2 * nf), in_specs=[gsel, usel, gsel],
            out_specs=pl.BlockSpec((tr, tf), lambda i, j: (i, j)),
            out_shape=jax.ShapeDtypeStruct((T, F2), BF16), compiler_params=_cp(),
        )(gu, gu, da)

    @jax.custom_vjp
    def f(gu):
        return fwd_call(gu)

    def fwd(gu):
        return fwd_call(gu), gu

    def bwd(gu, da):
        return (bwd_call(gu, da),)

    f.defvjp(fwd, bwd)
    return f(gu)


def gated_residual(x, gate, y, *, name):
    B, S, D = x.shape
    ts = _tile(S, 256, 8)
    row = pl.BlockSpec((1, ts, D), lambda b, j: (b, j, 0))
    per_b = pl.BlockSpec((1, 1, D), lambda b, j: (b, 0, 0))

    def fwd_call(x, gate, y):
        def body(x_ref, gt_ref, y_ref, o_ref):
            o_ref[0] = x_ref[0] + gt_ref[0] * y_ref[0]

        return pl.pallas_call(
            body, name=name + "_fwd", grid=(B, S // ts), in_specs=[row, per_b, row], out_specs=row,
            out_shape=jax.ShapeDtypeStruct((B, S, D), F32), compiler_params=_cp(),
        )(x, gate, y)

    def bwd_call(gate, y, g):
        def body(gt_ref, y_ref, g_ref, dy_ref, dgt_ref):
            @pl.when(pl.program_id(1) == 0)
            def _():
                dgt_ref[...] = jnp.zeros_like(dgt_ref)

            gv = g_ref[0]
            dy_ref[0] = gt_ref[0] * gv
            dgt_ref[0] += jnp.sum(gv * y_ref[0], axis=0, keepdims=True)

        return pl.pallas_call(
            body, name=name + "_bwd", grid=(B, S // ts), in_specs=[per_b, row, row],
            out_specs=[row, per_b],
            out_shape=[jax.ShapeDtypeStruct((B, S, D), F32), jax.ShapeDtypeStruct((B, 1, D), F32)],
            compiler_params=_cp(),
        )(gate, y, g)

    @jax.custom_vjp
    def f(x, gate, y):
        return fwd_call(x, gate, y)

    def fwd(x, gate, y):
        return fwd_call(x, gate, y), (gate, y)

    def bwd(res, g):
        dy, dgate = bwd_call(res[0], res[1], g)
        return g, dgate, dy

    f.defvjp(fwd, bwd)
    return f(x, gate, y)


def res_norm(x, gate, y, g, scale, shift, *, name):
    B, S, D = x.shape
    ts = _tile(S, 256, 8)
    row = pl.BlockSpec((1, ts, D), lambda b, j: (b, j, 0))
    per_b = pl.BlockSpec((1, 1, D), lambda b, j: (b, 0, 0))
    gspec = pl.BlockSpec((1, D), lambda b, j: (0, 0))

    def fwd_call(x, gate, y, g, scale, shift):
        def body(x_ref, gt_ref, y_ref, g_ref, sc_ref, sh_ref, xo_ref, h_ref):
            xv = x_ref[0] + gt_ref[0] * y_ref[0]
            xo_ref[0] = xv
            r = lax.rsqrt(jnp.mean(xv * xv, axis=-1, keepdims=True) + EPS)
            h_ref[0] = (xv * r * g_ref[...] * (1.0 + sc_ref[0]) + sh_ref[0]).astype(h_ref.dtype)

        return pl.pallas_call(
            body, name=name + "_fwd", grid=(B, S // ts),
            in_specs=[row, per_b, row, gspec, per_b, per_b], out_specs=[row, row],
            out_shape=[jax.ShapeDtypeStruct((B, S, D), F32), jax.ShapeDtypeStruct((B, S, D), BF16)],
            compiler_params=_cp(),
        )(x, gate, y, g, scale, shift)

    def bwd_call(xn, g, scale, gate, y, dh, dxn):
        def body(x_ref, g_ref, sc_ref, gt_ref, y_ref, dh_ref, dxn_ref,
                 dx_ref, dy_ref, dg_ref, dsc_ref, dsh_ref, dgt_ref):
            @pl.when(pl.program_id(1) == 0)
            def _():
                for ref in (dg_ref, dsc_ref, dsh_ref, dgt_ref):
                    ref[...] = jnp.zeros_like(ref)

            xv = x_ref[0]
            dh = dh_ref[0].astype(F32)
            gv = g_ref[...]
            r = lax.rsqrt(jnp.mean(xv * xv, axis=-1, keepdims=True) + EPS)
            xh = xv * r
            dyv = dh * (1.0 + sc_ref[0])
            dxh = dyv * gv
            dx = dxn_ref[0] + r * (dxh - xh * jnp.mean(dxh * xh, axis=-1, keepdims=True))
            dx_ref[0] = dx
            dy_ref[0] = gt_ref[0] * dx
            dgt_ref[0] += jnp.sum(dx * y_ref[0], axis=0, keepdims=True)
            dg_ref[0] += jnp.sum(dyv * xh, axis=0, keepdims=True)
            dsc_ref[0] += jnp.sum(dh * xh * gv, axis=0, keepdims=True)
            dsh_ref[0] += jnp.sum(dh, axis=0, keepdims=True)

        big = jax.ShapeDtypeStruct((B, S, D), F32)
        vec = jax.ShapeDtypeStruct((B, 1, D), F32)
        return pl.pallas_call(
            body, name=name + "_bwd", grid=(B, S // ts),
            in_specs=[row, gspec, per_b, per_b, row, row, row],
            out_specs=[row, row, per_b, per_b, per_b, per_b],
            out_shape=[big, big, vec, vec, vec, vec], compiler_params=_cp(),
        )(xn, g, scale, gate, y, dh, dxn)

    @jax.custom_vjp
    def f(x, gate, y, g, scale, shift):
        return tuple(fwd_call(x, gate, y, g, scale, shift))

    def fwd(x, gate, y, g, scale, shift):
        xn, h = fwd_call(x, gate, y, g, scale, shift)
        return (xn, h), (xn, g, scale, gate, y)

    def bwd(res, cts):
        xn, g, scale, gate, y = res
        dx, dy, dg, dsc, dsh, dgt = bwd_call(xn, g, scale, gate, y, cts[1], cts[0])
        return dx, dgt, dy, jnp.sum(dg, axis=0), dsc, dsh

    f.defvjp(fwd, bwd)
    return f(x, gate, y, g, scale, shift)


def final_loss(x, g, target, *, name):
    T, D = x.shape
    tr = _tile(T, 256, 8)
    row = pl.BlockSpec((tr, D), lambda i: (i, 0))
    vec = pl.BlockSpec((1, D), lambda i: (0, 0))

    def fwd_call(x, g, target):
        def body(x_ref, g_ref, t_ref, o_ref):
            @pl.when(pl.program_id(0) == 0)
            def _():
                o_ref[...] = jnp.zeros_like(o_ref)

            xv = x_ref[...]
            r = lax.rsqrt(jnp.mean(xv * xv, axis=-1, keepdims=True) + EPS)
            e = xv * r * g_ref[...] - t_ref[...]
            o_ref[...] += jnp.sum(e * e, axis=0, keepdims=True)

        part = pl.pallas_call(
            body, name=name + "_fwd", grid=(T // tr,), in_specs=[row, vec, row], out_specs=vec,
            out_shape=jax.ShapeDtypeStruct((1, D), F32), compiler_params=_cp(),
        )(x, g, target)
        return (0.5 / D) * jnp.sum(part)

    def bwd_call(x, g, target, ct):
        def body(x_ref, g_ref, t_ref, ct_ref, dx_ref, dg_ref):
            @pl.when(pl.program_id(0) == 0)
            def _():
                dg_ref[...] = jnp.zeros_like(dg_ref)

            xv = x_ref[...]
            gv = g_ref[...]
            r = lax.rsqrt(jnp.mean(xv * xv, axis=-1, keepdims=True) + EPS)
            xn = xv * r
            dy = (xn * gv - t_ref[...]) * (ct_ref[...] * (1.0 / D))
            dxn = dy * gv
            dx_ref[...] = r * (dxn - xn * jnp.mean(dxn * xn, axis=-1, keepdims=True))
            dg_ref[...] += jnp.sum(dy * xn, axis=0, keepdims=True)

        return pl.pallas_call(
            body, name=name + "_bwd", grid=(T // tr,),
            in_specs=[row, vec, row, pl.BlockSpec((1, 1), lambda i: (0, 0))], out_specs=[row, vec],
            out_shape=[jax.ShapeDtypeStruct((T, D), F32), jax.ShapeDtypeStruct((1, D), F32)],
            compiler_params=_cp(),
        )(x, g, target, ct)

    @jax.custom_vjp
    def f(x, g, target):
        return fwd_call(x, g, target)

    def fwd(x, g, target):
        return fwd_call(x, g, target), (x, g, target)

    def bwd(res, ct):
        x, g, target = res
        dx, dg = bwd_call(x, g, target, jnp.reshape(ct, (1, 1)).astype(F32))
        return dx, dg, jnp.zeros_like(target)

    f.defvjp(fwd, bwd)
    return f(x, g, target)


def adamw(w, g, m, v, *, name, carry=()):
    L, R, C = w.shape
    tr = _tile(R, 512, 8)
    spec = pl.BlockSpec((1, tr, C), lambda l, i: (l, i, 0))
    c1 = 1.0 / (1.0 - ADAM_B1 ** ADAM_STEP)
    c2 = 1.0 / (1.0 - ADAM_B2 ** ADAM_STEP)

    def body(w_ref, g_ref, m_ref, v_ref, d_ref, nm_ref, nv_ref):
        gv = g_ref[...]
        nm = ADAM_B1 * m_ref[...] + (1.0 - ADAM_B1) * gv
        nv = ADAM_B2 * v_ref[...] + (1.0 - ADAM_B2) * (gv * gv)
        nm_ref[...] = nm
        nv_ref[...] = nv
        d_ref[...] = -ADAM_LR * ((nm * c1) / (jnp.sqrt(nv * c2) + ADAM_EPS) + ADAM_WD * w_ref[...])

    o = jax.ShapeDtypeStruct((L, R, C), F32)
    return carry_call(
        body, name=name, grid=(L, R // tr), in_specs=[spec] * 4, out_specs=[spec] * 3,
        out_shape=[o, o, o], carry=carry,
    )(w, g, m, v)


def _position():
    x, y, c = lax.axis_index("x"), lax.axis_index("y"), lax.axis_index("c")
    return x, y, c


def ag_routine(shard, cols=False):
    R, C = shard.shape
    assert not cols or C % LANES == 0

    def parts(ins, outs, send_sems, recv_sems, local_sems):
        (x_ref,), (out_ref,) = ins, outs
        x, y, c = _position()
        me, sibling = (x, y, c), (x, y, 1 - c)
        chips = [(1 - x, y), (x, 1 - y), (1 - x, 1 - y)]

        def block(px, py, pc):
            idx = 4 * px + 2 * py + pc
            if cols:
                return out_ref.at[:, pl.ds(pl.multiple_of(idx * C, LANES), C)]
            return out_ref.at[idx]

        def copy(k, blk, to, src=None):
            return pltpu.make_async_remote_copy(
                src_ref=block(*blk) if src is None else src, dst_ref=block(*blk),
                send_sem=send_sems.at[k], recv_sem=recv_sems.at[k],
                device_id=to, device_id_type=pl.DeviceIdType.MESH)

        mine = pltpu.make_async_copy(x_ref, block(*me), local_sems.at[0])
        first = [copy(0, me, sibling, src=x_ref)]
        first += [copy(1 + j, me, (*chip, c), src=x_ref) for j, chip in enumerate(chips)]
        passed = [copy(4 + j, (*chip, c), sibling) for j, chip in enumerate(chips)]
        return me, sibling, c, chips, copy, mine, first, passed

    def start(*refs):
        me, sibling, c, chips, copy, mine, first, passed = parts(*refs)
        mine.start()
        for cp in first:
            cp.start()

    def finish(*refs):
        me, sibling, c, chips, copy, mine, first, passed = parts(*refs)
        for j, chip in enumerate(chips):
            copy(1 + j, (*chip, c), me).wait_recv()
            passed[j].start()
        copy(0, sibling, me).wait_recv()
        for j, chip in enumerate(chips):
            copy(4 + j, (*chip, 1 - c), me).wait_recv()
        for cp in first + passed:
            cp.wait_send()
        mine.wait()

    out = jax.ShapeDtypeStruct((R, N_DEV * C) if cols else (N_DEV, R, C), shard.dtype)
    return dict(ins=[shard], outs=[out], n_sem=7, n_local=1, start=start, finish=finish)


def all_gather(shard, *, name, cols=False):
    return comm_call(ag_routine(shard, cols), name=name)[0]


def carry_call(body, *, name, grid, in_specs, out_specs, out_shape, scratch_shapes=(), carry=(),
               dims=None):
    in_specs, out_specs, out_shape = list(in_specs), list(out_specs), list(out_shape)
    scratch_shapes = list(scratch_shapes)
    if not carry:
        call = pl.pallas_call(body, name=name, grid=grid, in_specs=in_specs, out_specs=out_specs,
                              out_shape=out_shape, scratch_shapes=scratch_shapes,
                              compiler_params=_cp(dimension_semantics=dims) if dims else _cp())
        return lambda *args: list(call(*args))
    n_in, n_out, n_scr = len(in_specs), len(out_specs), len(scratch_shapes)
    c_ins = [a for r in carry for a in r["ins"]]
    c_outs = [o for r in carry for o in r["outs"]]
    sems = []
    for r in carry:
        sems += [pltpu.SemaphoreType.DMA((r["n_sem"],)), pltpu.SemaphoreType.DMA((r["n_sem"],)),
                 pltpu.SemaphoreType.DMA((r["n_local"],))]

    def wrapped(*refs):
        refs = list(refs)
        ins, refs = refs[:n_in], refs[n_in:]
        cin, refs = refs[:len(c_ins)], refs[len(c_ins):]
        outs, refs = refs[:n_out], refs[n_out:]
        cout, refs = refs[:len(c_outs)], refs[len(c_outs):]
        scr, csem = refs[:n_scr], refs[n_scr:]
        ids = [pl.program_id(i) for i in range(len(grid))]
        first = functools.reduce(jnp.logical_and, [i == 0 for i in ids])
        last = functools.reduce(jnp.logical_and, [i == g - 1 for i, g in zip(ids, grid)])

        def each(which):
            io = oo = 0
            for j, r in enumerate(carry):
                r[which](cin[io:io + len(r["ins"])], cout[oo:oo + len(r["outs"])], *csem[3 * j:3 * j + 3])
                io += len(r["ins"])
                oo += len(r["outs"])

        @pl.when(first)
        def _():
            each("start")

        body(*ins, *outs, *scr)

        @pl.when(last)
        def _():
            each("finish")

    any_spec = pl.BlockSpec(memory_space=pl.ANY)
    call = pl.pallas_call(
        wrapped, name=name, grid=grid, in_specs=in_specs + [any_spec] * len(c_ins),
        out_specs=out_specs + [any_spec] * len(c_outs), out_shape=out_shape + c_outs,
        scratch_shapes=scratch_shapes + sems, compiler_params=_cp())
    return lambda *args: list(call(*args, *c_ins))


N_CHIP = 4


def rs_pair_routine(g, cols):
    if cols:
        R, C = g.shape[0], g.shape[1] // N_DEV
        assert C % LANES == 0
    else:
        _, R, C = g.shape

    def blk(ref, idx):
        if cols:
            return ref.at[:, pl.ds(pl.multiple_of(idx * C, LANES), C)]
        return ref.at[idx]

    def copies(ins, outs, send_sems, recv_sems, local_sems):
        (g_ref,), (got_ref,) = ins, outs
        x, y, c = _position()
        local, remote = [], []
        for q in range(N_CHIP):
            remote.append(pltpu.make_async_remote_copy(
                src_ref=blk(g_ref, 2 * q + 1 - c), dst_ref=got_ref.at[q],
                send_sem=send_sems.at[q], recv_sem=recv_sems.at[q],
                device_id=(x, y, 1 - c), device_id_type=pl.DeviceIdType.MESH))
        return local, remote

    def start(*refs):
        local, remote = copies(*refs)
        for cp in remote + local:
            cp.start()

    def finish(*refs):
        local, remote = copies(*refs)
        for cp in remote:
            cp.wait_recv()
        for cp in remote:
            cp.wait_send()
        for cp in local:
            cp.wait()

    o = jax.ShapeDtypeStruct((N_CHIP, R, C), g.dtype)
    return dict(ins=[g], outs=[o], n_sem=N_CHIP, n_local=1, start=start, finish=finish)


def rs_chip_routine(h):
    _, R, C = h.shape
    RELATIONS = ((0, 1), (1, 0), (1, 1))

    def copies(ins, outs, send_sems, recv_sems, local_sems):
        (h_ref,), (out_ref,) = ins, outs
        x, y, c = _position()
        local, remote = [], []
        for k, (fx, fy) in enumerate(RELATIONS):
            px = (1 - x) if fx else x
            py = (1 - y) if fy else y
            remote.append(pltpu.make_async_remote_copy(
                src_ref=h_ref.at[2 * px + py], dst_ref=out_ref.at[k],
                send_sem=send_sems.at[k], recv_sem=recv_sems.at[k],
                device_id=(px, py, c), device_id_type=pl.DeviceIdType.MESH))
        return local, remote

    def start(*refs):
        local, remote = copies(*refs)
        for cp in remote + local:
            cp.start()

    def finish(*refs):
        local, remote = copies(*refs)
        for cp in remote:
            cp.wait_recv()
        for cp in remote:
            cp.wait_send()
        for cp in local:
            cp.wait()

    return dict(ins=[h], outs=[jax.ShapeDtypeStruct((N_CHIP - 1, R, C), h.dtype)], n_sem=3, n_local=1,
                start=start, finish=finish)


def comm_call(routine, *, name):
    n_in, n_out = len(routine["ins"]), len(routine["outs"])

    def body(*refs):
        ins, outs, sems = refs[:n_in], refs[n_in:n_in + n_out], refs[n_in + n_out:]
        routine["start"](ins, outs, *sems)
        routine["finish"](ins, outs, *sems)

    any_spec = pl.BlockSpec(memory_space=pl.ANY)
    return pl.pallas_call(
        body, name=name, out_shape=routine["outs"],
        in_specs=[any_spec] * n_in, out_specs=[any_spec] * n_out,
        scratch_shapes=[pltpu.SemaphoreType.DMA((routine["n_sem"],)), pltpu.SemaphoreType.DMA((routine["n_sem"],)),
                        pltpu.SemaphoreType.DMA((routine["n_local"],))],
    )(*routine["ins"])


def add_own(g, got, core, *, cols, name):
    n, R, C = got.shape
    tr = _tile(R, 256, 8)
    if cols:
        gspec = pl.BlockSpec((tr, C), lambda q, i, c_ref: (i, 2 * q + c_ref[0]))
    else:
        gspec = pl.BlockSpec((1, tr, C), lambda q, i, c_ref: (2 * q + c_ref[0], i, 0))
    spec = pl.BlockSpec((1, tr, C), lambda q, i, c_ref: (q, i, 0))

    def body(c_ref, g_ref, b_ref, o_ref):
        gv = g_ref[...] if cols else g_ref[0]
        o_ref[0] = (gv.astype(F32) + b_ref[0].astype(F32)).astype(o_ref.dtype)

    return pl.pallas_call(
        body, name=name,
        grid_spec=pltpu.PrefetchScalarGridSpec(num_scalar_prefetch=1, grid=(n, R // tr),
                                               in_specs=[gspec, spec], out_specs=spec),
        out_shape=jax.ShapeDtypeStruct((n, R, C), g.dtype), compiler_params=_cp(),
    )(core, g, got)


def sum_chips(h, got, chip, *, name):
    _, R, C = h.shape
    tr = _tile(R, 256, 8)

    def body(q_ref, h_ref, g_ref, o_ref):
        acc = h_ref[0].astype(F32)
        for k in range(N_CHIP - 1):
            acc = acc + g_ref[k].astype(F32)
        o_ref[...] = acc

    return pl.pallas_call(
        body, name=name,
        grid_spec=pltpu.PrefetchScalarGridSpec(
            num_scalar_prefetch=1, grid=(R // tr,),
            in_specs=[pl.BlockSpec((1, tr, C), lambda i, q_ref: (q_ref[0], i, 0)),
                      pl.BlockSpec((N_CHIP - 1, tr, C), lambda i, q_ref: (0, i, 0))],
            out_specs=pl.BlockSpec((tr, C), lambda i, q_ref: (i, 0))),
        out_shape=jax.ShapeDtypeStruct((R, C), F32), compiler_params=_cp(),
    )(chip, h, got)


def reduce_scatter(g, *, cols, name):
    x, y, c = _position()
    core = jnp.reshape(c, (1,)).astype(jnp.int32)
    chip = jnp.reshape(2 * x + y, (1,)).astype(jnp.int32)
    got = comm_call(rs_pair_routine(g, cols), name=name + "_pair")[0]
    h = add_own(g, got, core, cols=cols, name=name + "_add")
    return sum_chips(h, comm_call(rs_chip_routine(h), name=name + "_chip")[0], chip, name=name + "_sum")


def sum_blocks(stack, *, name):
    n, R, C = stack.shape
    tr = _tile(R, 256, 8)

    def body(x_ref, o_ref):
        acc = x_ref[0].astype(F32)
        for i in range(1, n):
            acc = acc + x_ref[i].astype(F32)
        o_ref[...] = acc

    return pl.pallas_call(
        body, name=name, grid=(R // tr,),
        in_specs=[pl.BlockSpec((n, tr, C), lambda i: (0, i, 0))],
        out_specs=pl.BlockSpec((tr, C), lambda i: (i, 0)),
        out_shape=jax.ShapeDtypeStruct((R, C), F32), compiler_params=_cp(),
    )(stack)


PACK_COLS = 1024
PACK_ROW_MULT = 8


def _pack(arrays, dtype):
    flat = jnp.concatenate([a.reshape(-1).astype(dtype) for a in arrays])
    n = flat.shape[0]
    unit = PACK_COLS * PACK_ROW_MULT
    padded = -(-n // unit) * unit
    return jnp.pad(flat, (0, padded - n)).reshape(padded // PACK_COLS, PACK_COLS)


def _unpack(packed, shapes):
    flat = packed.reshape(-1)
    out, o = [], 0
    for s in shapes:
        n = int(np.prod(s))
        out.append(flat[o:o + n].reshape(s))
        o += n
    return out


def fsdp_cols(shard, *, name):
    K, n = shard.shape
    npad = -(-n // LANES) * LANES

    @jax.custom_vjp
    def f(p):
        p = jnp.pad(p, ((0, 0), (0, npad - n))) if npad != n else p
        return all_gather(p.astype(BF16), cols=True, name=name + "_ag")

    def fwd(p):
        return f(p), None

    def bwd(_, g):
        d = reduce_scatter(g, cols=True, name=name + "_rs")
        return (d[:, :n] if npad != n else d,)

    f.defvjp(fwd, bwd)
    return f(shard)


def fsdp_rows(shard, *, name):
    k, N = shard.shape

    @jax.custom_vjp
    def f(p):
        return all_gather(p.astype(BF16), name=name + "_ag").reshape(N_DEV * k, N)

    def fwd(p):
        return f(p), None

    def bwd(_, g):
        return (reduce_scatter(g.reshape(N_DEV, k, N), cols=False, name=name + "_rs"),)

    f.defvjp(fwd, bwd)
    return f(shard)


def _unpad_cols(w, n):
    K = w.shape[0]
    npad = w.shape[1] // N_DEV
    if npad == n:
        return w
    return w.reshape(K, N_DEV, npad)[:, :, :n].reshape(K, N_DEV * n)


def gather_rows(part, me, *, name):
    rows, n = part.shape
    per = rows // N_DEV

    @jax.custom_vjp
    def f(part):
        full = all_gather(part, name=name + "_fwd")
        mine = lax.dynamic_slice_in_dim(full, me * per, per, axis=1)
        return jnp.swapaxes(mine, 0, 1).reshape(per, N_DEV * n)

    def fwd(part):
        return f(part), None

    def bwd(_, g):
        full = all_gather(g, name=name + "_bwd")
        mine = lax.dynamic_slice_in_dim(full, me * n, n, axis=2)
        return (mine.reshape(rows, n),)

    f.defvjp(fwd, bwd)
    return f(part)


def _seg_layout():
    offs = np.concatenate([[0], np.cumsum(IN_SPLITS)])
    cols, widths, leads = [], [], []
    for s in SEG_ORDER:
        cols.append((int(offs[s]), int(offs[s + 1])))
        widths.append(SEG_PAD.get(s, IN_SPLITS[s]))
        leads.append(SEG_LEAD.get(s, 0))
    return cols, widths, leads


def _arrange_w_in(w, n):
    D = w.shape[0]
    npad = w.shape[1] // N_DEV
    cols, widths, leads = _seg_layout()

    def pieces_of(a, b):
        out = []
        for d in range(N_DEV):
            lo, hi = max(a, n * d), min(b, n * (d + 1))
            if lo < hi:
                out.append((d, lo, hi))
        return out

    @jax.custom_vjp
    def f(w):
        parts = []
        for (a, b), wd, ld in zip(cols, widths, leads):
            if ld:
                parts.append(jnp.zeros((D, ld), w.dtype))
            parts += [w[:, npad * d + lo - n * d:npad * d + hi - n * d] for d, lo, hi in pieces_of(a, b)]
            if wd != ld + b - a:
                parts.append(jnp.zeros((D, wd - ld - (b - a)), w.dtype))
        parts.append(jnp.zeros((D, IN_WIDTH - sum(widths)), w.dtype))
        return jnp.concatenate(parts, axis=1)

    def fwd(w):
        return f(w), None

    def bwd(_, g):
        offs = np.concatenate([[0], np.cumsum(widths)])
        runs = []
        for ((a, b), off, ld) in zip(cols, offs[:-1], leads):
            runs += [(lo, int(off) + ld + lo - a, hi - lo) for _, lo, hi in pieces_of(a, b)]
        runs.sort()
        parts, d_next = [], 1
        for lo, o, ln in runs:
            while lo >= n * d_next:
                parts.append(jnp.zeros((D, npad - n), g.dtype))
                d_next += 1
            parts.append(g[:, o:o + ln])
        parts.append(jnp.zeros((D, npad - n), g.dtype))
        return (jnp.concatenate(parts, axis=1),)

    f.defvjp(fwd, bwd)
    return f(w)


def split_cols(proj, widths):
    @jax.custom_vjp
    def f(p):
        outs, o = [], 0
        for wd in widths:
            outs.append(p[:, o:o + wd])
            o += wd
        return tuple(outs)

    def fwd(p):
        return f(p), None

    def bwd(_, gs):
        rest = proj.shape[1] - sum(widths)
        tail = [jnp.zeros((proj.shape[0], rest), proj.dtype)] if rest else []
        return (jnp.concatenate(list(gs) + tail, axis=1),)

    f.defvjp(fwd, bwd)
    return f(proj)


BIG = ("w_in", "w_uq", "w_ukv", "conv_w", "w_out", "w_gate_up", "w_down")
SMALL = ("b_ada", "norm1_g", "norm2_g", "q_norm_g", "k_norm_g", "mla_q_norm_g", "mla_kv_norm_g",
         "conv_b", "dt_bias", "a_log", "d_skip", "ssd_norm_g", "final_norm_g")
WEIGHTS = ("w_ada", "b_ada", "norm1_g", "norm2_g", "w_in", "q_norm_g", "k_norm_g", "mla_q_norm_g",
           "w_uq", "mla_kv_norm_g", "w_ukv", "conv_w", "conv_b", "dt_bias", "a_log", "d_skip",
           "ssd_norm_g", "w_out", "w_gate_up", "w_down", "final_norm_g")


PRE = ("w_in", "w_uq", "w_ukv", "conv_w")


def _layer(l, x, h, mods, W, shards, nxt, P, tabs):
    B, S, D = x.shape
    T = B * S
    nm = f"l{l}_"
    shift1, scale1, gate1, shift2, scale2, gate2 = mods[l]
    cos_a, sin_a, cos_b, sin_b, cos_k, sin_k = tabs

    w_in = _arrange_w_in(W["w_in"], IN_COLS // N_DEV)
    proj, (w_out_full,) = linear(h.reshape(T, D), w_in, out_dtype=F32, name=nm + "in",
                                 carry=((shards["w_out"], False, "w_out"),))
    q_a, k_a, v_a, cq, ckv, z, xbc, kpe, dtr = split_cols(proj, _seg_layout()[1])

    qn = group_rmsnorm(q_a, jnp.tile(P["q_norm_g"][l], GQA_H)[None], gs=HEAD, out_dtype=F32, name=nm + "qnorm")
    kn = group_rmsnorm(k_a, jnp.tile(P["k_norm_g"][l], GQA_KV)[None], gs=HEAD, out_dtype=F32, name=nm + "knorm")
    qr = rope(qn.reshape(B, S, -1), cos_a[:, :GQA_H * HEAD], sin_a[:, :GQA_H * HEAD], d=HEAD // 4, name=nm + "qrope")
    kr = rope(kn.reshape(B, S, -1), cos_a[:, :GQA_KV * HEAD], sin_a[:, :GQA_KV * HEAD], d=HEAD // 4, name=nm + "krope")
    gu_rows = shards["w_gate_up"].shape[0] // 2
    o_a, (w_gu_top,) = attention(qr, kr, v_a.reshape(B, S, -1), H=GQA_H, Hkv=GQA_KV, dk=HEAD, dv=HEAD,
                                 scale=HEAD ** -0.5, name=nm + "gqa",
                                 carry=((shards["w_gate_up"][:gu_rows], True, "w_gate_up_top"),))

    slot_pad = MLA_DK - MLA_NOPE - MLA_ROPE
    w_uq = jnp.pad(W["w_uq"].reshape(MLA_QL, MLA_H, MLA_NOPE + MLA_ROPE), ((0, 0), (0, 0), (0, slot_pad)))
    w_uq = w_uq.reshape(MLA_QL, MLA_H * MLA_DK)
    w_ukv = W["w_ukv"].reshape(MLA_KVL, MLA_H, MLA_NOPE + MLA_V)
    w_ukv = jnp.concatenate(
        [jnp.pad(w_ukv[:, :, :MLA_NOPE], ((0, 0), (0, 0), (0, MLA_DK - MLA_NOPE))).reshape(MLA_KVL, -1),
         w_ukv[:, :, MLA_NOPE:].reshape(MLA_KVL, -1)], axis=1)
    cqn = group_rmsnorm(cq, P["mla_q_norm_g"][l][None], gs=MLA_QL, out_dtype=BF16, name=nm + "cqnorm")
    ckvn = group_rmsnorm(ckv, P["mla_kv_norm_g"][l][None], gs=MLA_KVL, out_dtype=BF16, name=nm + "ckvnorm")
    qb = linear(cqn, w_uq, out_dtype=F32, name=nm + "uq")
    kvb = linear(ckvn, w_ukv, out_dtype=F32, name=nm + "ukv")
    k_slots, v_b = split_cols(kvb, (MLA_H * MLA_DK, MLA_H * MLA_V))
    q_cat = rope(qb.reshape(B, S, -1), cos_b, sin_b, d=MLA_ROPE // 4, name=nm + "qpe_rope")
    k_pe = rope(kpe.reshape(B, S, -1), cos_k, sin_k, d=MLA_ROPE // 4, name=nm + "kpe_rope")
    o_b, (w_gu_bottom,) = attention(q_cat, k_slots.reshape(B, S, -1), v_b.reshape(B, S, -1), H=MLA_H, Hkv=MLA_H,
                                    dk=MLA_DK, dv=MLA_V, scale=(MLA_NOPE + MLA_ROPE) ** -0.5, name=nm + "mla",
                                    k_shared=k_pe,
                                    carry=((shards["w_gate_up"][gu_rows:], True, "w_gate_up_bottom"),))
    w_gu_full = jnp.concatenate([w_gu_top, w_gu_bottom], axis=0)

    xact = conv_silu(xbc.reshape(B, S, -1), W["conv_w"].astype(F32), P["conv_b"][l][None], name=nm + "conv")
    brow = jnp.pad(P["dt_bias"][l].reshape(1, -1), ((0, 0), (0, LANES - 2 * SSD_H)))
    arow = jnp.pad(-jnp.exp(P["a_log"][l].reshape(1, -1)), ((0, 0), (0, LANES - 2 * SSD_H)))
    raw = dtr.reshape(B, S, LANES)
    y_f = ssd_scan(xact, raw, brow, arow, rev=False, name=nm + "ssd_f")
    y_b = ssd_scan(xact, raw, brow, arow, rev=True, name=nm + "ssd_b")
    dsk = jnp.repeat(P["d_skip"][l], SSD_P)[None]
    o_c = ssd_out(y_f.reshape(T, -1), y_b.reshape(T, -1), xact.reshape(T, -1), z, dsk,
                  P["ssd_norm_g"][l][None], name=nm + "ssd_out")

    o = jnp.concatenate([o_a.reshape(T, -1).astype(BF16), o_b.reshape(T, -1).astype(BF16), o_c], axis=-1)
    mix = linear(o, w_out_full, out_dtype=F32, name=nm + "out")
    x, h = res_norm(x, gate1, mix.reshape(B, S, D), P["norm2_g"][l][None], scale2, shift2, name=nm + "res1_norm2")

    y, nxt_full = ffn(h.reshape(T, D), w_gu_full, shards["w_down"], name=nm + "ffn",
                      carry=tuple((nxt[n], True, "next_" + n) for n in PRE) if nxt else ())
    if l + 1 < len(mods):
        x, h = res_norm(x, gate2, y.reshape(B, S, D), P["norm1_g"][l + 1][None], mods[l + 1][1], mods[l + 1][0],
                        name=nm + "res2_norm1")
    else:
        x, h = gated_residual(x, gate2, y.reshape(B, S, D), name=nm + "res2"), None
    return x, h, nxt_full


def kernel(x, c, w_ada, b_ada, norm1_g, norm2_g, w_in, q_norm_g, k_norm_g, mla_q_norm_g, w_uq, mla_kv_norm_g, w_ukv, conv_w, conv_b, dt_bias, a_log, d_skip, ssd_norm_g, w_out, w_gate_up, w_down, final_norm_g, loss_target, m_w_ada, m_b_ada, m_norm1_g, m_norm2_g, m_w_in, m_q_norm_g, m_k_norm_g, m_mla_q_norm_g, m_w_uq, m_mla_kv_norm_g, m_w_ukv, m_conv_w, m_conv_b, m_dt_bias, m_a_log, m_d_skip, m_ssd_norm_g, m_w_out, m_w_gate_up, m_w_down, m_final_norm_g, v_w_ada, v_b_ada, v_norm1_g, v_norm2_g, v_w_in, v_q_norm_g, v_k_norm_g, v_mla_q_norm_g, v_w_uq, v_mla_kv_norm_g, v_w_ukv, v_conv_w, v_conv_b, v_dt_bias, v_a_log, v_d_skip, v_ssd_norm_g, v_w_out, v_w_gate_up, v_w_down, v_final_norm_g):
    args = dict(locals())
    weights = {n: args[n] for n in WEIGHTS}
    moments_m = {n: args["m_" + n] for n in WEIGHTS}
    moments_v = {n: args["v_" + n] for n in WEIGHTS}
    B, S, D = x.shape
    L = w_ada.shape[0]
    T = B * S
    px, py, pc = _position()
    me = 4 * px + 2 * py + pc
    small_shapes = [weights[n].shape for n in SMALL]

    tabs = (*rope_tables(S, HEAD, GQA_H * HEAD),
            *rope_tables(S, MLA_ROPE, MLA_H * MLA_DK, slot=MLA_DK, lead=MLA_NOPE),
            *rope_tables(S, MLA_ROPE, MLA_DK, slot=MLA_DK, lead=MLA_NOPE))
    c_all = all_gather(c, name="gather_c").reshape(N_DEV * B, D)

    pre0 = [all_gather(_shard_prep(weights[n][0], True), cols=True, name=f"l0_{n}_ag") for n in PRE]

    def local_loss(big, w_ada_s, small, x, pre):
        P = dict(zip(SMALL, small))
        parts = [linear(c_all, w_ada_s[l], out_dtype=F32, a_silu=True, name=f"l{l}_ada") for l in range(L)]
        n_ada = parts[0].shape[1]
        both = gather_rows(jnp.concatenate(parts, axis=1), me, name="mod").reshape(B, N_DEV, L, n_ada)
        mods = []
        for l in range(L):
            mod = both[:, :, l, :].reshape(B, N_DEV * n_ada) + P["b_ada"][l][None]
            mods.append([m[:, None, :] for m in jnp.split(mod, 6, axis=-1)])
        h = rmsmod(x, P["norm1_g"][0][None], mods[0][1], mods[0][0], name="l0_norm1")
        for l in range(L):
            W = {n: full if n == "w_in" else _unpad_cols(full, big[n].shape[2]) for n, full in zip(PRE, pre)}
            shards = {n: big[n][l] for n in ("w_out", "w_gate_up", "w_down")}
            nxt = {n: big[n][l + 1] for n in PRE} if l + 1 < L else None
            x, h, pre = _layer(l, x, h, mods, W, shards, nxt, P, tabs)
        return final_loss(x.reshape(T, D), P["final_norm_g"][None], loss_target.reshape(T, D), name="loss")

    big = {n: weights[n] for n in BIG}
    small = tuple(weights[n] for n in SMALL)
    loss, (g_big, g_ada, g_small, grad_x, g_pre0) = jax.value_and_grad(local_loss, argnums=(0, 1, 2, 3, 4))(
        big, w_ada, small, x, pre0)
    loss = lax.psum(loss, ("x", "y", "c"))

    grads = dict(g_big)
    grads["w_ada"] = g_ada
    g_small_sum = sum_blocks(all_gather(_pack(g_small, F32), name="small_grads_ag"), name="small_grads_sum")
    grads.update(zip(SMALL, _unpack(g_small_sum, small_shapes)))

    delta, new_m, new_v = {}, {}, {}
    pre0_h = [_rs_begin(g, True, weights[n].shape[1:], f"l0_{n}") for n, g in zip(PRE, g_pre0)]
    delta["w_ada"], new_m["w_ada"], new_v["w_ada"], *pre0_got = adamw(
        w_ada, g_ada, m_w_ada, v_w_ada, name="adamw_w_ada", carry=[rs_chip_routine(h) for h in pre0_h])
    for n, h, got in zip(PRE, pre0_h, pre0_got):
        g0 = _rs_end(h, got, True, weights[n].shape[1:], f"l0_{n}")
        grads[n] = lax.dynamic_update_slice(grads[n], g0[None], (0, 0, 0))
    for n in BIG:
        delta[n], new_m[n], new_v[n] = adamw(weights[n], grads[n], moments_m[n], moments_v[n], name="adamw_" + n)
    d_, m_, v_ = adamw(_pack([weights[n] for n in SMALL], F32)[None], g_small_sum[None],
                       _pack([moments_m[n] for n in SMALL], F32)[None], _pack([moments_v[n] for n in SMALL], F32)[None],
                       name="adamw_small")
    for tgt, packed in ((delta, d_), (new_m, m_), (new_v, v_)):
        tgt.update(zip(SMALL, _unpack(packed[0], small_shapes)))

    return (loss, grad_x, *[grads[n] for n in WEIGHTS], *[delta[n] for n in WEIGHTS],
            *[new_m[n] for n in WEIGHTS], *[new_v[n] for n in WEIGHTS])
```
